```python
import math
import jax, jax.numpy as jnp
from jax import lax
import numpy as np

D_MODEL = 1024
BATCH = 8
SEQ = 2048
DEPTH = 2

RMS_EPS = 1e-5

SSD_EXPAND = 2
SSD_INNER = SSD_EXPAND * D_MODEL
SSD_HEAD_DIM = 64
SSD_HEADS = SSD_INNER // SSD_HEAD_DIM
SSD_STATE = 128
SSD_GROUPS = 4
SSD_HPG = SSD_HEADS // SSD_GROUPS
SSD_CONV = 4
SSD_CHUNK = 128
SSD_CONV_CH = SSD_INNER + 2 * SSD_GROUPS * SSD_STATE

ATTN_HEAD_DIM = 128
ATTN_KV_HEADS = D_MODEL // ATTN_HEAD_DIM
ATTN_PATTERNS = ((128, 1), (512, 4), (2048, 16))
ATTN_N_PAT = len(ATTN_PATTERNS)
ATTN_Q_HEADS = ATTN_N_PAT * ATTN_KV_HEADS
ATTN_OUT = ATTN_KV_HEADS * ATTN_HEAD_DIM
ATTN_BLOCK = 128
ROPE_THETA = 500000.0
ROPE_DIM = ATTN_HEAD_DIM // 4

FFN_HIDDEN = -(-8 * D_MODEL // (3 * 256)) * 256

_SEGMENTS = (SSD_INNER, SSD_CONV_CH, SSD_HEADS, ATTN_Q_HEADS * ATTN_HEAD_DIM,
             ATTN_OUT, ATTN_OUT, D_MODEL, D_MODEL)
IN_SPLITS = tuple(int(s) for s in np.cumsum(_SEGMENTS)[:-1])
N_IN = int(sum(_SEGMENTS))

kernel_name = "hybrid_ssd_dilated_attn_gated_block"


def rms_norm(x, w):
    xf = x.astype(jnp.float32)
    y = xf * lax.rsqrt(jnp.mean(xf * xf, axis=-1, keepdims=True) + RMS_EPS)
    return (y * w.astype(jnp.float32)).astype(x.dtype)


def rope_tables(seq):
    inv = ROPE_THETA ** (-jnp.arange(0, ROPE_DIM, 2, dtype=jnp.float32) / ROPE_DIM)
    ang = jnp.arange(seq, dtype=jnp.float32)[:, None] * inv[None, :]
    return jnp.cos(ang), jnp.sin(ang)


def apply_partial_rope(t, cos, sin):
    half = ROPE_DIM // 2
    shape = (cos.shape[0],) + (1,) * (t.ndim - 3) + (half,)
    c, s = cos.reshape(shape), sin.reshape(shape)
    t1, t2 = t[..., :half], t[..., half:ROPE_DIM]
    return jnp.concatenate([t1 * c - t2 * s, t2 * c + t1 * s, t[..., ROPE_DIM:]], axis=-1).astype(t.dtype)


def causal_depthwise_conv(u, w, b):
    out = lax.conv_general_dilated(
        u, w[:, None, :].astype(u.dtype), window_strides=(1,), padding=((SSD_CONV - 1, 0),),
        dimension_numbers=("NWC", "WIO", "NWC"), feature_group_count=u.shape[-1])
    return out + b.astype(u.dtype)


def ssd_chunked(xh, dt, a, bm, cm):
    bsz, s, _, p = xh.shape
    nc = s // SSD_CHUNK
    x = (xh * dt[..., None]).reshape(bsz, nc, SSD_CHUNK, SSD_GROUPS, SSD_HPG, p)
    a_dt = (dt * a).reshape(bsz, nc, SSD_CHUNK, SSD_GROUPS, SSD_HPG).transpose(0, 3, 4, 1, 2)
    bc = bm.reshape(bsz, nc, SSD_CHUNK, SSD_GROUPS, SSD_STATE)
    cc = cm.reshape(bsz, nc, SSD_CHUNK, SSD_GROUPS, SSD_STATE)
    a_cum = jnp.cumsum(a_dt, axis=-1)
    causal = jnp.tril(jnp.ones((SSD_CHUNK, SSD_CHUNK), dtype=bool))
    decay = jnp.exp(jnp.where(causal, a_cum[..., :, None] - a_cum[..., None, :], -jnp.inf))
    cb = jnp.einsum("bclgn,bcsgn->bgcls", cc, bc)
    y_diag = jnp.einsum("bgecls,bcsgep->bclgep", cb[:, :, None] * decay, x)
    decay_states = jnp.exp(a_cum[..., -1:] - a_cum).transpose(0, 3, 4, 1, 2)
    states = jnp.einsum("bclgn,bclgep->bcgepn", bc, x * decay_states[..., None])
    chunk_decay = jnp.exp(a_cum[..., -1]).transpose(3, 0, 1, 2)

    def step(h, inp):
        s_c, d_c = inp
        return h * d_c[..., None, None] + s_c, h

    h0 = jnp.zeros((bsz, SSD_GROUPS, SSD_HPG, p, SSD_STATE), x.dtype)
    _, states_in = lax.scan(step, h0, (states.transpose(1, 0, 2, 3, 4, 5), chunk_decay))
    state_decay = jnp.exp(a_cum).transpose(0, 3, 4, 1, 2)[..., None]
    y_off = jnp.einsum("bclgn,cbgepn->bclgep", cc, states_in) * state_decay
    return (y_diag + y_off).reshape(bsz, s, SSD_HEADS, p)


def ssd_mixer(z, xbc, dt_raw, conv_w, conv_b, dt_bias, a_log, d_skip, norm_w):
    bsz, s, _ = z.shape
    f32 = jnp.float32
    xbc = jax.nn.silu(causal_depthwise_conv(xbc, conv_w, conv_b)).astype(f32)
    xs, bm, cm = jnp.split(xbc, [SSD_INNER, SSD_INNER + SSD_GROUPS * SSD_STATE], axis=-1)
    xs = xs.reshape(bsz, s, SSD_HEADS, SSD_HEAD_DIM)
    bm = bm.reshape(bsz, s, SSD_GROUPS, SSD_STATE)
    cm = cm.reshape(bsz, s, SSD_GROUPS, SSD_STATE)
    dt = jax.nn.softplus(dt_raw.astype(f32) + dt_bias.astype(f32))
    a = -jnp.exp(a_log.astype(f32))
    y = ssd_chunked(xs, dt, a, bm, cm) + d_skip.astype(f32)[:, None] * xs
    y = y.reshape(bsz, s, SSD_INNER) * jax.nn.silu(z.astype(f32))
    yg = y.reshape(bsz, s, SSD_GROUPS, SSD_INNER // SSD_GROUPS)
    yg = yg * lax.rsqrt(jnp.mean(yg * yg, axis=-1, keepdims=True) + RMS_EPS)
    return (yg.reshape(bsz, s, SSD_INNER) * norm_w.astype(f32)).astype(z.dtype)


def dilated_window_attention(q, k, v, dilation, steps):
    bsz, s, h, dh = q.shape
    length = s // dilation
    nb = -(-length // ATTN_BLOCK)
    lp = nb * ATTN_BLOCK

    def strided(t):
        t = t.reshape(bsz, length, dilation, h, dh).transpose(0, 2, 3, 1, 4)
        return jnp.pad(t, ((0, 0), (0, 0), (0, 0), (0, lp - length), (0, 0)))

    def banded(t):
        tp = jnp.pad(t, ((0, 0), (0, 0), (0, 0), (ATTN_BLOCK, 0), (0, 0)))
        prev = tp[..., :lp, :].reshape(bsz, dilation, h, nb, ATTN_BLOCK, dh)
        cur = t.reshape(bsz, dilation, h, nb, ATTN_BLOCK, dh)
        return jnp.concatenate([prev, cur], axis=-2)

    qb = strided(q).reshape(bsz, dilation, h, nb, ATTN_BLOCK, dh)
    kb, vb = banded(strided(k)), banded(strided(v))
    scores = jnp.einsum("brhnqe,brhnke->brhnqk", qb, kb, preferred_element_type=jnp.float32) * (dh ** -0.5)
    blk = jnp.arange(nb)[:, None, None]
    qi = jnp.arange(ATTN_BLOCK)[None, :, None] + ATTN_BLOCK
    kj = jnp.arange(2 * ATTN_BLOCK)[None, None, :]
    dist = qi - kj
    mask = (dist >= 0) & (dist <= steps) & (blk * ATTN_BLOCK + kj >= ATTN_BLOCK)
    scores = jnp.where(mask, scores, -jnp.inf)
    lse = jax.nn.logsumexp(scores, axis=-1)
    probs = jnp.exp(scores - lse[..., None])
    out = jnp.einsum("brhnqk,brhnke->brhnqe", probs.astype(v.dtype), vb, preferred_element_type=jnp.float32)

    def unstrided(t):
        tail = t.shape[5:]
        t = t.reshape((bsz, dilation, h, lp) + tail)[:, :, :, :length]
        return jnp.moveaxis(t, 3, 1).reshape((bsz, s, h) + tail)

    return unstrided(out), unstrided(lse)


def dilated_attention_mixer(q, k, v, cos, sin):
    bsz, s, _ = q.shape
    q = apply_partial_rope(q.reshape(bsz, s, ATTN_N_PAT, ATTN_KV_HEADS, ATTN_HEAD_DIM), cos, sin)
    k = apply_partial_rope(k.reshape(bsz, s, ATTN_KV_HEADS, ATTN_HEAD_DIM), cos, sin)
    v = v.reshape(bsz, s, ATTN_KV_HEADS, ATTN_HEAD_DIM)
    outs, lses = [], []
    for g, (window, dilation) in enumerate(ATTN_PATTERNS):
        o, l = dilated_window_attention(q[:, :, g], k, v, dilation, window // dilation)
        outs.append(o)
        lses.append(l)
    weights = jax.nn.softmax(jnp.stack(lses), axis=0)
    o = jnp.sum(weights[..., None] * jnp.stack(outs), axis=0)
    return o.reshape(bsz, s, ATTN_OUT).astype(v.dtype)


def _fwd_setup_inputs(seed: int = 0) -> dict:
    key = jax.random.key(seed)
    ks = jax.random.split(key, 16)
    f32 = jnp.float32

    def dense(k, shape, fan_in):
        return jax.random.normal(k, shape, f32) * fan_in ** -0.5

    def gain(k, shape):
        return 1.0 + 0.02 * jax.random.normal(k, shape, f32)

    dt0 = jnp.exp(jax.random.uniform(ks[5], (DEPTH, SSD_HEADS), f32, math.log(1e-3), math.log(1e-1)))
    return {
        "x": jax.random.normal(ks[0], (BATCH, SEQ, D_MODEL), f32),
        "norm_mix": gain(ks[1], (DEPTH, D_MODEL)),
        "w_in": dense(ks[2], (DEPTH, D_MODEL, N_IN), D_MODEL),
        "conv_w": dense(ks[3], (DEPTH, SSD_CONV, SSD_CONV_CH), SSD_CONV),
        "conv_b": 0.02 * jax.random.normal(ks[4], (DEPTH, SSD_CONV_CH), f32),
        "dt_bias": dt0 + jnp.log(-jnp.expm1(-dt0)),
        "a_log": jnp.log(jax.random.uniform(ks[6], (DEPTH, SSD_HEADS), f32, 1.0, 16.0)),
        "d_skip": gain(ks[7], (DEPTH, SSD_HEADS)),
        "ssd_norm": gain(ks[8], (DEPTH, SSD_INNER)),
        "w_ssd_branch": dense(ks[9], (DEPTH, SSD_INNER, D_MODEL), SSD_INNER),
        "w_attn_branch": dense(ks[10], (DEPTH, ATTN_OUT, D_MODEL), ATTN_OUT),
        "w_out": dense(ks[11], (DEPTH, D_MODEL, D_MODEL), D_MODEL),
        "norm_ffn": gain(ks[12], (DEPTH, D_MODEL)),
        "w_gate_up": dense(ks[13], (DEPTH, D_MODEL, 2 * FFN_HIDDEN), D_MODEL),
        "w_down": dense(ks[14], (DEPTH, FFN_HIDDEN, D_MODEL), FFN_HIDDEN),
        "norm_final": gain(ks[15], (D_MODEL,)),
    }


def _fwd_reference(x, norm_mix, w_in, conv_w, conv_b, dt_bias, a_log, d_skip, ssd_norm,
              w_ssd_branch, w_attn_branch, w_out, norm_ffn, w_gate_up, w_down, norm_final):
    cos, sin = rope_tables(x.shape[1])
    h = x
    for layer in range(DEPTH):
        u = rms_norm(h, norm_mix[layer])
        proj = u @ w_in[layer]
        z, xbc, dt_raw, q, k, v, g_ssd, g_attn = jnp.split(proj, IN_SPLITS, axis=-1)
        y_ssd = ssd_mixer(z, xbc, dt_raw, conv_w[layer], conv_b[layer], dt_bias[layer],
                          a_log[layer], d_skip[layer], ssd_norm[layer])
        y_attn = dilated_attention_mixer(q, k, v, cos, sin)
        merged = (jax.nn.sigmoid(g_ssd) * (y_ssd @ w_ssd_branch[layer])
                  + jax.nn.sigmoid(g_attn) * (y_attn @ w_attn_branch[layer]))
        h = h + merged @ w_out[layer]
        u = rms_norm(h, norm_ffn[layer])
        gate, up = jnp.split(u @ w_gate_up[layer], 2, axis=-1)
        h = h + (jax.nn.silu(gate) * up) @ w_down[layer]
    return rms_norm(h, norm_final)


import jax as _jax
import jax.numpy as _jnp

TWIN_FORMAT = 'train_step'
FWD_PARAMS = ['x', 'norm_mix', 'w_in', 'conv_w', 'conv_b', 'dt_bias', 'a_log', 'd_skip', 'ssd_norm', 'w_ssd_branch', 'w_attn_branch', 'w_out', 'norm_ffn', 'w_gate_up', 'w_down', 'norm_final']
TWIN_WEIGHTS = ['norm_mix', 'w_in', 'conv_w', 'conv_b', 'dt_bias', 'a_log', 'd_skip', 'ssd_norm', 'w_ssd_branch', 'w_attn_branch', 'w_out', 'norm_ffn', 'w_gate_up', 'w_down', 'norm_final']
TWIN_DIFF_INPUT = 'x'
TWIN_INPUTS = ['x', 'norm_mix', 'w_in', 'conv_w', 'conv_b', 'dt_bias', 'a_log', 'd_skip', 'ssd_norm', 'w_ssd_branch', 'w_attn_branch', 'w_out', 'norm_ffn', 'w_gate_up', 'w_down', 'norm_final', 'loss_target', 'm_norm_mix', 'm_w_in', 'm_conv_w', 'm_conv_b', 'm_dt_bias', 'm_a_log', 'm_d_skip', 'm_ssd_norm', 'm_w_ssd_branch', 'm_w_attn_branch', 'm_w_out', 'm_norm_ffn', 'm_w_gate_up', 'm_w_down', 'm_norm_final', 'v_norm_mix', 'v_w_in', 'v_conv_w', 'v_conv_b', 'v_dt_bias', 'v_a_log', 'v_d_skip', 'v_ssd_norm', 'v_w_ssd_branch', 'v_w_attn_branch', 'v_w_out', 'v_norm_ffn', 'v_w_gate_up', 'v_w_down', 'v_norm_final']
TWIN_OUTPUTS = ['loss', 'grad_x', 'grad_norm_mix', 'grad_w_in', 'grad_conv_w', 'grad_conv_b', 'grad_dt_bias', 'grad_a_log', 'grad_d_skip', 'grad_ssd_norm', 'grad_w_ssd_branch', 'grad_w_attn_branch', 'grad_w_out', 'grad_norm_ffn', 'grad_w_gate_up', 'grad_w_down', 'grad_norm_final', 'delta_norm_mix', 'delta_w_in', 'delta_conv_w', 'delta_conv_b', 'delta_dt_bias', 'delta_a_log', 'delta_d_skip', 'delta_ssd_norm', 'delta_w_ssd_branch', 'delta_w_attn_branch', 'delta_w_out', 'delta_norm_ffn', 'delta_w_gate_up', 'delta_w_down', 'delta_norm_final', 'new_m_norm_mix', 'new_m_w_in', 'new_m_conv_w', 'new_m_conv_b', 'new_m_dt_bias', 'new_m_a_log', 'new_m_d_skip', 'new_m_ssd_norm', 'new_m_w_ssd_branch', 'new_m_w_attn_branch', 'new_m_w_out', 'new_m_norm_ffn', 'new_m_w_gate_up', 'new_m_w_down', 'new_m_norm_final', 'new_v_norm_mix', 'new_v_w_in', 'new_v_conv_w', 'new_v_conv_b', 'new_v_dt_bias', 'new_v_a_log', 'new_v_d_skip', 'new_v_ssd_norm', 'new_v_w_ssd_branch', 'new_v_w_attn_branch', 'new_v_w_out', 'new_v_norm_ffn', 'new_v_w_gate_up', 'new_v_w_down', 'new_v_norm_final']
TWIN_LEAF_KINDS = {'loss': 'loss', 'grad_x': 'grad_x', 'grad_norm_mix': 'grad_w', 'grad_w_in': 'grad_w', 'grad_conv_w': 'grad_w', 'grad_conv_b': 'grad_w', 'grad_dt_bias': 'grad_w', 'grad_a_log': 'grad_w', 'grad_d_skip': 'grad_w', 'grad_ssd_norm': 'grad_w', 'grad_w_ssd_branch': 'grad_w', 'grad_w_attn_branch': 'grad_w', 'grad_w_out': 'grad_w', 'grad_norm_ffn': 'grad_w', 'grad_w_gate_up': 'grad_w', 'grad_w_down': 'grad_w', 'grad_norm_final': 'grad_w', 'delta_norm_mix': 'delta_w', 'delta_w_in': 'delta_w', 'delta_conv_w': 'delta_w', 'delta_conv_b': 'delta_w', 'delta_dt_bias': 'delta_w', 'delta_a_log': 'delta_w', 'delta_d_skip': 'delta_w', 'delta_ssd_norm': 'delta_w', 'delta_w_ssd_branch': 'delta_w', 'delta_w_attn_branch': 'delta_w', 'delta_w_out': 'delta_w', 'delta_norm_ffn': 'delta_w', 'delta_w_gate_up': 'delta_w', 'delta_w_down': 'delta_w', 'delta_norm_final': 'delta_w', 'new_m_norm_mix': 'new_m', 'new_m_w_in': 'new_m', 'new_m_conv_w': 'new_m', 'new_m_conv_b': 'new_m', 'new_m_dt_bias': 'new_m', 'new_m_a_log': 'new_m', 'new_m_d_skip': 'new_m', 'new_m_ssd_norm': 'new_m', 'new_m_w_ssd_branch': 'new_m', 'new_m_w_attn_branch': 'new_m', 'new_m_w_out': 'new_m', 'new_m_norm_ffn': 'new_m', 'new_m_w_gate_up': 'new_m', 'new_m_w_down': 'new_m', 'new_m_norm_final': 'new_m', 'new_v_norm_mix': 'new_v', 'new_v_w_in': 'new_v', 'new_v_conv_w': 'new_v', 'new_v_conv_b': 'new_v', 'new_v_dt_bias': 'new_v', 'new_v_a_log': 'new_v', 'new_v_d_skip': 'new_v', 'new_v_ssd_norm': 'new_v', 'new_v_w_ssd_branch': 'new_v', 'new_v_w_attn_branch': 'new_v', 'new_v_w_out': 'new_v', 'new_v_norm_ffn': 'new_v', 'new_v_w_gate_up': 'new_v', 'new_v_w_down': 'new_v', 'new_v_norm_final': 'new_v'}


def _forward(args):
    return _fwd_reference(*[args[k] for k in FWD_PARAMS])


def _output_shape():
    out = _jax.eval_shape(lambda: _forward(_fwd_setup_inputs(0)))
    return out.shape, out.dtype

N_MICROBATCH = 1
ADAM_LR = 0.001
ADAM_B1 = 0.9
ADAM_B2 = 0.999
ADAM_EPS = 1e-08
ADAM_WD = 0.01
ADAM_STEP = 10
PER_EXAMPLE_BATCH_AXIS = {'x': 0, 'loss_target': 0}
SHARED_INPUTS = []
_WEIGHT_DTYPES = {'norm_mix': _jnp.float32, 'w_in': _jnp.float32, 'conv_w': _jnp.float32, 'conv_b': _jnp.float32, 'dt_bias': _jnp.float32, 'a_log': _jnp.float32, 'd_skip': _jnp.float32, 'ssd_norm': _jnp.float32, 'w_ssd_branch': _jnp.float32, 'w_attn_branch': _jnp.float32, 'w_out': _jnp.float32, 'norm_ffn': _jnp.float32, 'w_gate_up': _jnp.float32, 'w_down': _jnp.float32, 'norm_final': _jnp.float32}
MOMENT_SCALE = {'norm_mix': 1.055074e-01, 'w_in': 3.074698e-02, 'conv_w': 4.228343e-02, 'conv_b': 5.860316e-02, 'dt_bias': 1.621803e-01, 'a_log': 1.542434e-01, 'd_skip': 2.345858e-01, 'ssd_norm': 4.764744e-02, 'w_ssd_branch': 6.766916e-02, 'w_attn_branch': 1.496846e-02, 'w_out': 6.812436e-02, 'norm_ffn': 9.024283e-02, 'w_gate_up': 3.693975e-02, 'w_down': 6.013245e-02, 'norm_final': 1.599390e+01}


def _to_microbatches(a, axis):
    t = _jnp.moveaxis(a, axis, 0)
    t = t.reshape((N_MICROBATCH, t.shape[0] // N_MICROBATCH) + t.shape[1:])
    return _jnp.moveaxis(t, 1, axis + 1)


def setup_inputs(seed: int = 0) -> dict:
    inp = _fwd_setup_inputs(seed)
    key = _jax.random.fold_in(_jax.random.key(seed), 7919)
    shape, _ = _output_shape()
    out = dict(inp)
    out["loss_target"] = _jax.random.normal(_jax.random.fold_in(key, 0), shape, _jnp.float32)
    for i, name in enumerate(TWIN_WEIGHTS):
        w = inp[name].astype(_jnp.float32)
        if MOMENT_SCALE is None:
            s = _jnp.sqrt(_jnp.mean(_jnp.square(w)) + 1e-30)
        else:
            s = MOMENT_SCALE[name]
        km, kv = _jax.random.split(_jax.random.fold_in(key, i + 1))
        out[name] = w
        out["m_" + name] = s * _jax.random.normal(km, w.shape, _jnp.float32)
        out["v_" + name] = (s * s) * _jax.random.uniform(kv, w.shape, _jnp.float32, 0.5, 1.5)
    if N_MICROBATCH > 1:
        for name, axis in PER_EXAMPLE_BATCH_AXIS.items():
            out[name] = _to_microbatches(out[name], axis)
    return {'x': out['x'], 'norm_mix': out['norm_mix'], 'w_in': out['w_in'], 'conv_w': out['conv_w'], 'conv_b': out['conv_b'], 'dt_bias': out['dt_bias'], 'a_log': out['a_log'], 'd_skip': out['d_skip'], 'ssd_norm': out['ssd_norm'], 'w_ssd_branch': out['w_ssd_branch'], 'w_attn_branch': out['w_attn_branch'], 'w_out': out['w_out'], 'norm_ffn': out['norm_ffn'], 'w_gate_up': out['w_gate_up'], 'w_down': out['w_down'], 'norm_final': out['norm_final'], 'loss_target': out['loss_target'], 'm_norm_mix': out['m_norm_mix'], 'm_w_in': out['m_w_in'], 'm_conv_w': out['m_conv_w'], 'm_conv_b': out['m_conv_b'], 'm_dt_bias': out['m_dt_bias'], 'm_a_log': out['m_a_log'], 'm_d_skip': out['m_d_skip'], 'm_ssd_norm': out['m_ssd_norm'], 'm_w_ssd_branch': out['m_w_ssd_branch'], 'm_w_attn_branch': out['m_w_attn_branch'], 'm_w_out': out['m_w_out'], 'm_norm_ffn': out['m_norm_ffn'], 'm_w_gate_up': out['m_w_gate_up'], 'm_w_down': out['m_w_down'], 'm_norm_final': out['m_norm_final'], 'v_norm_mix': out['v_norm_mix'], 'v_w_in': out['v_w_in'], 'v_conv_w': out['v_conv_w'], 'v_conv_b': out['v_conv_b'], 'v_dt_bias': out['v_dt_bias'], 'v_a_log': out['v_a_log'], 'v_d_skip': out['v_d_skip'], 'v_ssd_norm': out['v_ssd_norm'], 'v_w_ssd_branch': out['v_w_ssd_branch'], 'v_w_attn_branch': out['v_w_attn_branch'], 'v_w_out': out['v_w_out'], 'v_norm_ffn': out['v_norm_ffn'], 'v_w_gate_up': out['v_w_gate_up'], 'v_w_down': out['v_w_down'], 'v_norm_final': out['v_norm_final']}


def _loss(weights, diff, rest, loss_target):
    with _jax.named_scope("forward"):
        args = {**rest, TWIN_DIFF_INPUT: diff, **{k: w.astype(_WEIGHT_DTYPES[k]) for k, w in weights.items()}}
        y = _forward(args)
    with _jax.named_scope("loss_head"):
        err = _jnp.square(y.astype(_jnp.float32) - loss_target)
        return 0.5 * _jnp.sum(_jnp.mean(err, axis=-1)) if err.ndim else 0.5 * err


def _adamw(w, g, m, v):
    m = ADAM_B1 * m + (1.0 - ADAM_B1) * g
    v = ADAM_B2 * v + (1.0 - ADAM_B2) * _jnp.square(g)
    m_hat = m / (1.0 - ADAM_B1 ** ADAM_STEP)
    v_hat = v / (1.0 - ADAM_B2 ** ADAM_STEP)
    delta = -ADAM_LR * (m_hat / (_jnp.sqrt(v_hat) + ADAM_EPS) + ADAM_WD * w)
    return delta, m, v


def reference(x, norm_mix, w_in, conv_w, conv_b, dt_bias, a_log, d_skip, ssd_norm, w_ssd_branch, w_attn_branch, w_out, norm_ffn, w_gate_up, w_down, norm_final, loss_target, m_norm_mix, m_w_in, m_conv_w, m_conv_b, m_dt_bias, m_a_log, m_d_skip, m_ssd_norm, m_w_ssd_branch, m_w_attn_branch, m_w_out, m_norm_ffn, m_w_gate_up, m_w_down, m_norm_final, v_norm_mix, v_w_in, v_conv_w, v_conv_b, v_dt_bias, v_a_log, v_d_skip, v_ssd_norm, v_w_ssd_branch, v_w_attn_branch, v_w_out, v_norm_ffn, v_w_gate_up, v_w_down, v_norm_final):
    given = dict(x=x, norm_mix=norm_mix, w_in=w_in, conv_w=conv_w, conv_b=conv_b, dt_bias=dt_bias, a_log=a_log, d_skip=d_skip, ssd_norm=ssd_norm, w_ssd_branch=w_ssd_branch, w_attn_branch=w_attn_branch, w_out=w_out, norm_ffn=norm_ffn, w_gate_up=w_gate_up, w_down=w_down, norm_final=norm_final, loss_target=loss_target, m_norm_mix=m_norm_mix, m_w_in=m_w_in, m_conv_w=m_conv_w, m_conv_b=m_conv_b, m_dt_bias=m_dt_bias, m_a_log=m_a_log, m_d_skip=m_d_skip, m_ssd_norm=m_ssd_norm, m_w_ssd_branch=m_w_ssd_branch, m_w_attn_branch=m_w_attn_branch, m_w_out=m_w_out, m_norm_ffn=m_norm_ffn, m_w_gate_up=m_w_gate_up, m_w_down=m_w_down, m_norm_final=m_norm_final, v_norm_mix=v_norm_mix, v_w_in=v_w_in, v_conv_w=v_conv_w, v_conv_b=v_conv_b, v_dt_bias=v_dt_bias, v_a_log=v_a_log, v_d_skip=v_d_skip, v_ssd_norm=v_ssd_norm, v_w_ssd_branch=v_w_ssd_branch, v_w_attn_branch=v_w_attn_branch, v_w_out=v_w_out, v_norm_ffn=v_norm_ffn, v_w_gate_up=v_w_gate_up, v_w_down=v_w_down, v_norm_final=v_norm_final)
    weights = {n: given[n] for n in TWIN_WEIGHTS}
    shared = {n: given[n] for n in SHARED_INPUTS}
    per_example = {n: given[n] for n in ['x']}
    grad_fn = _jax.value_and_grad(_loss, argnums=(0, 1))

    def one_microbatch(ex, loss_target):
        ex = dict(ex)
        diff = ex.pop(TWIN_DIFF_INPUT)
        return grad_fn(weights, diff, {**shared, **ex}, loss_target)

    if N_MICROBATCH == 1:
        loss, (grad_w, grad_x) = one_microbatch(per_example, given["loss_target"])
    else:
        def body(carry, xs):
            loss_sum, grad_sum = carry
            l_k, (gw_k, gx_k) = one_microbatch(xs[0], xs[1])
            with _jax.named_scope("update"):
                return (loss_sum + l_k, _jax.tree.map(_jnp.add, grad_sum, gw_k)), gx_k

        init = (_jnp.zeros((), _jnp.float32), _jax.tree.map(_jnp.zeros_like, weights))
        (loss, grad_w), grad_x = _jax.lax.scan(body, init, (per_example, given["loss_target"]))
    with _jax.named_scope("update"):
        delta_w, new_m, new_v = {}, {}, {}
        for n in TWIN_WEIGHTS:
            delta_w[n], new_m[n], new_v[n] = _adamw(weights[n], grad_w[n], given["m_" + n], given["v_" + n])
    return (loss, grad_x, *[grad_w[n] for n in TWIN_WEIGHTS], *[delta_w[n] for n in TWIN_WEIGHTS],
            *[new_m[n] for n in TWIN_WEIGHTS], *[new_v[n] for n in TWIN_WEIGHTS])
```

```python
import functools

import jax
import jax.numpy as jnp
from jax import lax
from jax.experimental import pallas as pl
from jax.experimental.pallas import tpu as pltpu

F32, BF16 = jnp.float32, jnp.bfloat16
SDS = jax.ShapeDtypeStruct
MESH = pl.DeviceIdType.MESH

D_MODEL = 1024
SEQ = 2048
DEPTH = 2
RMS_EPS = 1e-5
SSD_INNER = 2048
SSD_HEAD_DIM = 64
SSD_HEADS = 32
SSD_STATE = 128
SSD_GROUPS = 4
SSD_CONV = 4
SSD_CHUNK = 128
SSD_CONV_CH = 3072
ATTN_HEAD_DIM = 128
ATTN_KV_HEADS = 8
ATTN_DILATIONS = (1, 4, 16)
ATTN_N_PAT = 3
ATTN_BLOCK = 128
ATTN_OUT = 1024
ROPE_THETA = 500000.0
ROPE_DIM = 32
FFN_HIDDEN = 2816
ADAM_LR, ADAM_B1, ADAM_B2, ADAM_EPS, ADAM_WD, ADAM_STEP = 0.001, 0.9, 0.999, 1e-08, 0.01, 10

N_DEV = 8
LANES = 128
VMEM_LIMIT = 56 * 1024 * 1024
HPAD = 128
HIGHEST = lax.Precision.HIGHEST

IN_ROWS = (("w_z", 2048), ("w_xbc", 3072), ("w_dt", 32), ("w_q0", 1024), ("w_q1", 1024), ("w_q2", 1024),
           ("w_k", 1024), ("w_v", 1024), ("w_gs", 1024), ("w_ga", 1024))
N_IN = sum(r for _, r in IN_ROWS)


def _cparams(sem):
    return pltpu.CompilerParams(dimension_semantics=sem, vmem_limit_bytes=VMEM_LIMIT)


def _sigmoid(x):
    return 1.0 / (1.0 + jnp.exp(-x))


def _silu(x):
    return x * _sigmoid(x)


def _softplus(x):
    return jnp.maximum(x, 0.0) + jnp.log(1.0 + jnp.exp(-jnp.abs(x)))


def _dot(a, b, dims=(((1,), (0,)), ((), ())), precision=None):
    return lax.dot_general(a, b, dims, precision=precision, preferred_element_type=F32)


NT = (((1,), (1,)), ((), ()))
TN = (((0,), (0,)), ((), ()))


def _bdot(a, b, dims=(((1,), (0,)), ((), ()))):
    return _dot(a.astype(BF16), b.astype(BF16), dims)


def _pick(dim, cands):
    for c in cands:
        if dim % c == 0:
            return c
    return dim


def matmul(a, b, *, name, ta=False, tb=False, out_dtype=F32, add=None):
    m, k = (a.shape[1], a.shape[0]) if ta else a.shape
    n = b.shape[0] if tb else b.shape[1]
    tm = _pick(m, (1024, 512, 256, 128))
    tn = _pick(n, (512, 256, 128))
    tk = _pick(k, (1024, 512, 256, 128))
    nk = k // tk
    a_spec = pl.BlockSpec((tk, tm), lambda i, j, kk: (kk, i)) if ta else pl.BlockSpec((tm, tk), lambda i, j, kk: (i, kk))
    b_spec = pl.BlockSpec((tn, tk), lambda i, j, kk: (j, kk)) if tb else pl.BlockSpec((tk, tn), lambda i, j, kk: (kk, j))
    dims = (((0 if ta else 1,), (1 if tb else 0,)), ((), ()))
    has_add = add is not None

    def body(*refs):
        if has_add:
            a_ref, b_ref, add_ref, o_ref, acc = refs
        else:
            a_ref, b_ref, o_ref, acc = refs
        kk = pl.program_id(2)

        @pl.when(kk == 0)
        def _():
            acc[...] = jnp.zeros_like(acc)

        acc[...] += _dot(a_ref[...].astype(BF16), b_ref[...].astype(BF16), dims)

        @pl.when(kk == nk - 1)
        def _():
            r = acc[...]
            if has_add:
                r = r + add_ref[...].astype(F32)
            o_ref[...] = r.astype(o_ref.dtype)

    in_specs = [a_spec, b_spec]
    args = [a, b]
    if has_add:
        in_specs.append(pl.BlockSpec((tm, tn), lambda i, j, kk: (i, j)))
        args.append(add)
    return pl.pallas_call(
        body, name=name, grid=(m // tm, n // tn, nk),
        in_specs=in_specs, out_specs=pl.BlockSpec((tm, tn), lambda i, j, kk: (i, j)),
        out_shape=SDS((m, n), out_dtype), scratch_shapes=[pltpu.VMEM((tm, tn), F32)],
        compiler_params=_cparams(("parallel", "parallel", "arbitrary")),
    )(*args)


def rowcall(name, fn, rows, params, row_outs, red_outs=(), tr=256):
    s = rows[0].shape[0]
    n_in = len(rows) + len(params)
    n_row = len(row_outs)

    def body(*refs):
        outs = fn(*[r[...] for r in refs[:n_in]])
        if not isinstance(outs, (tuple, list)):
            outs = (outs,)
        orefs = refs[n_in:]
        for r, o in zip(orefs[:n_row], outs[:n_row]):
            r[...] = o.astype(r.dtype)
        if red_outs:
            @pl.when(pl.program_id(0) == 0)
            def _():
                for r in orefs[n_row:]:
                    r[...] = jnp.zeros_like(r)
            for r, o in zip(orefs[n_row:], outs[n_row:]):
                r[...] += o.astype(F32)

    in_specs = [pl.BlockSpec((tr, a.shape[1]), lambda i: (i, 0)) for a in rows]
    in_specs += [pl.BlockSpec(p.shape, lambda i: (0, 0)) for p in params]
    out_specs = [pl.BlockSpec((tr, c), lambda i: (i, 0)) for c, _ in row_outs]
    out_specs += [pl.BlockSpec(shp, lambda i: (0, 0)) for shp in red_outs]
    out_shape = [SDS((s, c), dt) for c, dt in row_outs] + [SDS(shp, F32) for shp in red_outs]
    res = pl.pallas_call(
        body, name=name, grid=(s // tr,), in_specs=in_specs, out_specs=out_specs, out_shape=out_shape,
        compiler_params=_cparams(("arbitrary",) if red_outs else ("parallel",)),
    )(*rows, *params)
    return res


def _rms(x, w):
    return x * lax.rsqrt(jnp.mean(x * x, axis=-1, keepdims=True) + RMS_EPS) * w


def rms_fwd(h, w, name):
    return rowcall(name, _rms, [h], [w], [(D_MODEL, BF16)])[0]


def rms_bwd(h, du, dres, w, name):
    def fn(hb, dub, dresb, wb):
        _, vjp = jax.vjp(_rms, hb, wb)
        dh, dw = vjp(dub)
        return dh + dresb, dw
    return rowcall(name, fn, [h, du, dres], [w], [(D_MODEL, F32)], [(1, D_MODEL)])


def loss_head(h, target, w, name):
    def fn(hb, tb, wb):
        def f(hh, ww):
            err = _rms(hh, ww) - tb
            return 0.5 * jnp.sum(jnp.mean(err * err, axis=-1, keepdims=True), axis=0, keepdims=True)
        val, vjp = jax.vjp(f, hb, wb)
        dh, dw = vjp(jnp.ones((1, 1), F32))
        return dh, dw, jnp.broadcast_to(val, (1, LANES))
    return rowcall(name, fn, [h, target], [w], [(D_MODEL, F32)], [(1, D_MODEL), (1, LANES)])


def _gate(a, b, gs, ga):
    return _sigmoid(gs) * a + _sigmoid(ga) * b


def gate_fwd(a, b, gs, ga, name):
    return rowcall(name, _gate, [a, b, gs, ga], [], [(D_MODEL, BF16)])[0]


def gate_bwd(a, b, gs, ga, dm, name):
    def fn(ab, bb, gsb, gab, dmb):
        _, vjp = jax.vjp(_gate, ab, bb, gsb, gab)
        return vjp(dmb)
    return rowcall(name, fn, [a, b, gs, ga, dm], [], [(D_MODEL, BF16)] * 4)


def _swiglu(gu):
    return _silu(gu[:, :FFN_HIDDEN]) * gu[:, FFN_HIDDEN:]


def swiglu_fwd(gu, name):
    return rowcall(name, _swiglu, [gu], [], [(FFN_HIDDEN, BF16)])[0]


def swiglu_bwd(gu, dact, name):
    def fn(gub, db):
        _, vjp = jax.vjp(_swiglu, gub)
        return vjp(db.astype(F32))[0]
    return rowcall(name, fn, [gu, dact], [], [(2 * FFN_HIDDEN, BF16)])[0]


def _ssd_post(y, xs, z, dskip, normw):
    y = (y + dskip * xs) * _silu(z)
    gw = SSD_INNER // SSD_GROUPS
    parts = []
    for g in range(SSD_GROUPS):
        yg = y[:, g * gw:(g + 1) * gw]
        parts.append(yg * lax.rsqrt(jnp.mean(yg * yg, axis=-1, keepdims=True) + RMS_EPS))
    return jnp.concatenate(parts, axis=-1) * normw


def ssd_post_fwd(y, xc, z, dskip, normw, name):
    def fn(yb, xcb, zb, db, nb):
        return _ssd_post(yb, xcb[:, :SSD_INNER], zb, db, nb)
    return rowcall(name, fn, [y, xc, z], [dskip, normw], [(SSD_INNER, BF16)])[0]


def ssd_post_bwd(y, xc, z, dskip, normw, dyn, name):
    def fn(yb, xcb, zb, dynb, db, nb):
        _, vjp = jax.vjp(_ssd_post, yb, xcb[:, :SSD_INNER], zb, db, nb)
        return vjp(dynb)
    return rowcall(name, fn, [y, xc, z, dyn], [dskip, normw],
                   [(SSD_INNER, F32), (SSD_INNER, F32), (SSD_INNER, BF16)], [(1, SSD_INNER), (1, SSD_INNER)])


def _rope(t, cosf, sina, sinb):
    return t * cosf + pltpu.roll(t, LANES - ROPE_DIM // 2, 1) * sina + pltpu.roll(t, ROPE_DIM // 2, 1) * sinb


def rope_tables():
    half = ROPE_DIM // 2
    inv = ROPE_THETA ** (-jnp.arange(0, ROPE_DIM, 2, dtype=F32) / ROPE_DIM)
    ang = jnp.arange(SEQ, dtype=F32)[:, None] * inv[None, :]
    cos, sin = jnp.cos(ang), jnp.sin(ang)
    zeros = jnp.zeros((SEQ, LANES - ROPE_DIM), F32)
    z16 = jnp.zeros((SEQ, half), F32)
    cosf = jnp.concatenate([cos, cos, jnp.ones((SEQ, LANES - ROPE_DIM), F32)], axis=1)
    sina = jnp.concatenate([-sin, z16, zeros], axis=1)
    sinb = jnp.concatenate([z16, sin, zeros], axis=1)
    return cosf, sina, sinb


CONV_TC = 256


def _conv_pre(x, w, b, row):
    acc = x * w[SSD_CONV - 1:SSD_CONV, :] + b
    shifted = [x]
    for j in range(1, SSD_CONV):
        xs = jnp.where(row >= j, pltpu.roll(x, j, 0), 0.0)
        shifted.append(xs)
        acc = acc + xs * w[SSD_CONV - 1 - j:SSD_CONV - j, :]
    return acc, shifted


def conv_fwd(xbc, w, b, name):
    def body(x_ref, w_ref, b_ref, o_ref):
        row = lax.broadcasted_iota(jnp.int32, (SEQ, CONV_TC), 0)
        pre, _ = _conv_pre(x_ref[...], w_ref[...], b_ref[...], row)
        o_ref[...] = _silu(pre)
    return pl.pallas_call(
        body, name=name, grid=(SSD_CONV_CH // CONV_TC,),
        in_specs=[pl.BlockSpec((SEQ, CONV_TC), lambda i: (0, i)), pl.BlockSpec((SSD_CONV, CONV_TC), lambda i: (0, i)),
                  pl.BlockSpec((1, CONV_TC), lambda i: (0, i))],
        out_specs=pl.BlockSpec((SEQ, CONV_TC), lambda i: (0, i)),
        out_shape=SDS((SEQ, SSD_CONV_CH), F32), compiler_params=_cparams(("parallel",)),
    )(xbc, w, b)


def conv_bwd(xbc, w, b, dxc, name):
    def body(x_ref, w_ref, b_ref, dy_ref, dx_ref, dw_ref, db_ref):
        row = lax.broadcasted_iota(jnp.int32, (SEQ, CONV_TC), 0)
        wv = w_ref[...]
        pre, shifted = _conv_pre(x_ref[...], wv, b_ref[...], row)
        sg = _sigmoid(pre)
        ds = dy_ref[...] * (sg * (1.0 + pre * (1.0 - sg)))
        dx = ds * wv[SSD_CONV - 1:SSD_CONV, :]
        for j in range(1, SSD_CONV):
            dsj = jnp.where(row < SEQ - j, pltpu.roll(ds, SEQ - j, 0), 0.0)
            dx = dx + dsj * wv[SSD_CONV - 1 - j:SSD_CONV - j, :]
        dx_ref[...] = dx.astype(dx_ref.dtype)
        for j in range(SSD_CONV):
            dw_ref[SSD_CONV - 1 - j:SSD_CONV - j, :] = jnp.sum(ds * shifted[j], axis=0, keepdims=True)
        db_ref[...] = jnp.sum(ds, axis=0, keepdims=True)
    return pl.pallas_call(
        body, name=name, grid=(SSD_CONV_CH // CONV_TC,),
        in_specs=[pl.BlockSpec((SEQ, CONV_TC), lambda i: (0, i)), pl.BlockSpec((SSD_CONV, CONV_TC), lambda i: (0, i)),
                  pl.BlockSpec((1, CONV_TC), lambda i: (0, i)), pl.BlockSpec((SEQ, CONV_TC), lambda i: (0, i))],
        out_specs=[pl.BlockSpec((SEQ, CONV_TC), lambda i: (0, i)), pl.BlockSpec((SSD_CONV, CONV_TC), lambda i: (0, i)),
                   pl.BlockSpec((1, CONV_TC), lambda i: (0, i))],
        out_shape=[SDS((SEQ, SSD_CONV_CH), BF16), SDS((SSD_CONV, SSD_CONV_CH), F32), SDS((1, SSD_CONV_CH), F32)],
        compiler_params=_cparams(("parallel",)),
    )(xbc, w, b, dxc)


N_CHUNKS = SEQ // SSD_CHUNK
N_PAIRS = SSD_HEADS // 2
PAIRS_PER_GROUP = N_PAIRS // SSD_GROUPS
B_OFF = SSD_INNER
C_OFF = SSD_INNER + SSD_GROUPS * SSD_STATE


def _ssd_prefix(dtr, dtr_t, dtb, dtb_t, alog, alog_t):
    ln = SSD_CHUNK
    dt = _softplus(dtr + dtb)
    dt_t = _softplus(dtr_t + dtb_t)
    dta = dt * (-jnp.exp(alog))
    dta_t = dt_t * (-jnp.exp(alog_t))
    r = lax.broadcasted_iota(jnp.int32, (ln, ln), 0)
    c = lax.broadcasted_iota(jnp.int32, (ln, ln), 1)
    a_cum = _dot((r >= c).astype(F32), dta, precision=HIGHEST)
    a_cum_t = _dot(dta_t, (r <= c).astype(F32), precision=HIGHEST)
    a_last = jnp.sum(dta_t, axis=1, keepdims=True)
    return dt, a_cum, a_cum_t, a_last


def _ssd_pair(x_pair, bg, cg, hp, dt, a_cum, a_cum_t, a_last, *, e0):
    ln = SSD_CHUNK
    lane = lax.broadcasted_iota(jnp.int32, (ln, LANES), 1)
    sub = lax.broadcasted_iota(jnp.int32, (LANES, SSD_STATE), 0)
    row = lax.broadcasted_iota(jnp.int32, (ln, ln), 0)
    col = lax.broadcasted_iota(jnp.int32, (ln, ln), 1)
    lo = lane < SSD_HEAD_DIM
    e1 = e0 + 1
    c0, c1 = a_cum[:, e0:e0 + 1], a_cum[:, e1:e1 + 1]
    r0, r1 = a_cum_t[e0:e0 + 1, :], a_cum_t[e1:e1 + 1, :]
    l0, l1 = a_last[e0:e0 + 1, :], a_last[e1:e1 + 1, :]
    xd = x_pair * jnp.where(lo, dt[:, e0:e0 + 1], dt[:, e1:e1 + 1])
    causal = row >= col
    cb = _bdot(cg, bg, NT)
    m0 = cb * jnp.exp(jnp.where(causal, c0 - r0, -jnp.inf))
    m1 = cb * jnp.exp(jnp.where(causal, c1 - r1, -jnp.inf))
    y = _bdot(m0, jnp.where(lo, xd, 0.0)) + _bdot(m1, jnp.where(lo, 0.0, xd))
    acum_pair = jnp.where(lo, c0, c1)
    y = y + _bdot(cg, hp, NT) * jnp.exp(acum_pair)
    last_pair = jnp.where(lo, l0, l1)
    st = _bdot(xd * jnp.exp(last_pair - acum_pair), bg, TN)
    h_out = hp * jnp.exp(jnp.where(sub < SSD_HEAD_DIM, l0, l1)) + st
    return y, h_out


def _ssd_in_specs(chunk_of):
    return [
        pl.BlockSpec((SSD_CHUNK, SSD_CONV_CH), lambda i: (chunk_of(i), 0)),
        pl.BlockSpec((SSD_CHUNK, HPAD), lambda i: (chunk_of(i), 0)),
        pl.BlockSpec((HPAD, SSD_CHUNK), lambda i: (0, chunk_of(i))),
        pl.BlockSpec((1, HPAD), lambda i: (0, 0)), pl.BlockSpec((HPAD, 1), lambda i: (0, 0)),
        pl.BlockSpec((1, HPAD), lambda i: (0, 0)), pl.BlockSpec((HPAD, 1), lambda i: (0, 0)),
    ]


def ssd_fwd(xc, dtr, dtr_t, dtb, dtb_t, alog, alog_t, name):
    def body(xc_ref, dtr_ref, dtrt_ref, dtb_ref, dtbt_ref, al_ref, alt_ref, y_ref, hs_ref, h_scr):
        @pl.when(pl.program_id(0) == 0)
        def _():
            h_scr[...] = jnp.zeros_like(h_scr)

        hs_ref[0] = h_scr[...]
        dt, a_cum, a_cum_t, a_last = _ssd_prefix(dtr_ref[...], dtrt_ref[...], dtb_ref[...], dtbt_ref[...],
                                                  al_ref[...], alt_ref[...])
        for pr in range(N_PAIRS):
            g = pr // PAIRS_PER_GROUP
            sl = slice(pr * LANES, (pr + 1) * LANES)
            bg = xc_ref[:, B_OFF + g * SSD_STATE:B_OFF + (g + 1) * SSD_STATE]
            cg = xc_ref[:, C_OFF + g * SSD_STATE:C_OFF + (g + 1) * SSD_STATE]
            y, h_out = _ssd_pair(xc_ref[:, sl], bg, cg, h_scr[sl, :], dt, a_cum, a_cum_t, a_last, e0=2 * pr)
            y_ref[:, sl] = y
            h_scr[sl, :] = h_out

    return pl.pallas_call(
        body, name=name, grid=(N_CHUNKS,), in_specs=_ssd_in_specs(lambda i: i),
        out_specs=[pl.BlockSpec((SSD_CHUNK, SSD_INNER), lambda i: (i, 0)),
                   pl.BlockSpec((1, SSD_INNER, SSD_STATE), lambda i: (i, 0, 0))],
        out_shape=[SDS((SEQ, SSD_INNER), F32), SDS((N_CHUNKS, SSD_INNER, SSD_STATE), F32)],
        scratch_shapes=[pltpu.VMEM((SSD_INNER, SSD_STATE), F32)],
        compiler_params=_cparams(("arbitrary",)),
    )(xc, dtr, dtr_t, dtb, dtb_t, alog, alog_t)


def ssd_bwd(xc, dtr, dtr_t, dtb, dtb_t, alog, alog_t, hs, dy, dxs_extra, name):
    rev = lambda i: N_CHUNKS - 1 - i

    def body(xc_ref, dtr_ref, dtrt_ref, dtb_ref, dtbt_ref, al_ref, alt_ref, hs_ref, dy_ref, dxe_ref,
             dxc_ref, ddtr_ref, ddtrt_ref, ddtb_ref, ddtbt_ref, dal_ref, dalt_ref, dh_scr):
        @pl.when(pl.program_id(0) == 0)
        def _():
            dh_scr[...] = jnp.zeros_like(dh_scr)
            for r in (ddtb_ref, ddtbt_ref, dal_ref, dalt_ref):
                r[...] = jnp.zeros_like(r)

        prefix_in = (dtr_ref[...], dtrt_ref[...], dtb_ref[...], dtbt_ref[...], al_ref[...], alt_ref[...])
        (dt, a_cum, a_cum_t, a_last), prefix_vjp = jax.vjp(_ssd_prefix, *prefix_in)
        d_dt = jnp.zeros_like(dt)
        d_acum = jnp.zeros_like(a_cum)
        d_acum_t = jnp.zeros_like(a_cum_t)
        d_alast = jnp.zeros_like(a_last)
        for g in range(SSD_GROUPS):
            bg = xc_ref[:, B_OFF + g * SSD_STATE:B_OFF + (g + 1) * SSD_STATE]
            cg = xc_ref[:, C_OFF + g * SSD_STATE:C_OFF + (g + 1) * SSD_STATE]
            d_bg = jnp.zeros_like(bg)
            d_cg = jnp.zeros_like(cg)
            for j in range(PAIRS_PER_GROUP):
                pr = g * PAIRS_PER_GROUP + j
                sl = slice(pr * LANES, (pr + 1) * LANES)
                _, vjp = jax.vjp(functools.partial(_ssd_pair, e0=2 * pr),
                                 xc_ref[:, sl], bg, cg, hs_ref[0, sl, :], dt, a_cum, a_cum_t, a_last)
                dx, dbg, dcg, dhp, ddt, dac, dact, dal = vjp((dy_ref[:, sl], dh_scr[sl, :]))
                dxc_ref[:, sl] = dx + dxe_ref[:, sl]
                dh_scr[sl, :] = dhp
                d_bg, d_cg = d_bg + dbg, d_cg + dcg
                d_dt, d_acum, d_acum_t, d_alast = d_dt + ddt, d_acum + dac, d_acum_t + dact, d_alast + dal
            dxc_ref[:, B_OFF + g * SSD_STATE:B_OFF + (g + 1) * SSD_STATE] = d_bg
            dxc_ref[:, C_OFF + g * SSD_STATE:C_OFF + (g + 1) * SSD_STATE] = d_cg
        g_dtr, g_dtrt, g_dtb, g_dtbt, g_al, g_alt = prefix_vjp((d_dt, d_acum, d_acum_t, d_alast))
        ddtr_ref[...] = g_dtr
        ddtrt_ref[...] = g_dtrt
        ddtb_ref[...] += g_dtb
        ddtbt_ref[...] += g_dtbt
        dal_ref[...] += g_al
        dalt_ref[...] += g_alt

    in_specs = _ssd_in_specs(rev) + [
        pl.BlockSpec((1, SSD_INNER, SSD_STATE), lambda i: (rev(i), 0, 0)),
        pl.BlockSpec((SSD_CHUNK, SSD_INNER), lambda i: (rev(i), 0)),
        pl.BlockSpec((SSD_CHUNK, SSD_INNER), lambda i: (rev(i), 0)),
    ]
    out_specs = [
        pl.BlockSpec((SSD_CHUNK, SSD_CONV_CH), lambda i: (rev(i), 0)),
        pl.BlockSpec((SSD_CHUNK, HPAD), lambda i: (rev(i), 0)),
        pl.BlockSpec((HPAD, SSD_CHUNK), lambda i: (0, rev(i))),
        pl.BlockSpec((1, HPAD), lambda i: (0, 0)), pl.BlockSpec((HPAD, 1), lambda i: (0, 0)),
        pl.BlockSpec((1, HPAD), lambda i: (0, 0)), pl.BlockSpec((HPAD, 1), lambda i: (0, 0)),
    ]
    out_shape = [SDS((SEQ, SSD_CONV_CH), F32), SDS((SEQ, HPAD), F32), SDS((HPAD, SEQ), F32),
                 SDS((1, HPAD), F32), SDS((HPAD, 1), F32), SDS((1, HPAD), F32), SDS((HPAD, 1), F32)]
    return pl.pallas_call(
        body, name=name, grid=(N_CHUNKS,), in_specs=in_specs, out_specs=out_specs, out_shape=out_shape,
        scratch_shapes=[pltpu.VMEM((SSD_INNER, SSD_STATE), F32)],
        compiler_params=_cparams(("arbitrary",)),
    )(xc, dtr, dtr_t, dtb, dtb_t, alog, alog_t, hs, dy, dxs_extra)


ATTN_SCALE = ATTN_HEAD_DIM ** -0.5


def _attn_scores(q, kp, kc, has_prev):
    qi = lax.broadcasted_iota(jnp.int32, (ATTN_BLOCK, ATTN_BLOCK), 0)
    kj = lax.broadcasted_iota(jnp.int32, (ATTN_BLOCK, ATTN_BLOCK), 1)
    s_c = jnp.where(qi >= kj, _bdot(q, kc, NT) * ATTN_SCALE, -jnp.inf)
    s_p = jnp.where((kj >= qi) & has_prev, _bdot(q, kp, NT) * ATTN_SCALE, -jnp.inf)
    return s_p, s_c


def _for_units(unit):
    for g, d in enumerate(ATTN_DILATIONS):
        nb = SEQ // d // ATTN_BLOCK
        span = d * ATTN_BLOCK

        def per_residue(r, carry, g=g, d=d, nb=nb, span=span):
            def per_block(n, c2):
                start = r + n * span
                prev = jnp.where(n > 0, start - span, start)
                unit(g, pl.ds(start, ATTN_BLOCK, stride=d), pl.ds(prev, ATTN_BLOCK, stride=d), n > 0)
                return c2
            return lax.fori_loop(0, nb, per_block, carry)
        lax.fori_loop(0, d, per_residue, 0)


def _head_specs(n_q_groups):
    blk = (SEQ, ATTN_HEAD_DIM)
    q_specs = [pl.BlockSpec(blk, functools.partial(lambda h, g: (0, g * ATTN_KV_HEADS + h), g=g)) for g in range(n_q_groups)]
    head = pl.BlockSpec(blk, lambda h: (0, h))
    table = pl.BlockSpec(blk, lambda h: (0, 0))
    return q_specs, head, table


def attn_fwd(q, k, v, tabs, name):
    q_specs, head, table = _head_specs(ATTN_N_PAT)

    def body(q0_ref, q1_ref, q2_ref, k_ref, v_ref, c_ref, sa_ref, sb_ref, y_ref, lse_ref, *scr):
        qs, og, ls, ks = scr[0:3], scr[3:6], scr[6:9], scr[9]
        c, sa, sb = c_ref[...], sa_ref[...], sb_ref[...]
        for g, q_ref in enumerate((q0_ref, q1_ref, q2_ref)):
            qs[g][...] = _rope(q_ref[...], c, sa, sb)
        ks[...] = _rope(k_ref[...], c, sa, sb)

        def unit(g, rows, prows, has_prev):
            s_p, s_c = _attn_scores(qs[g][rows, :], ks[prows, :], ks[rows, :], has_prev)
            m = jnp.maximum(jnp.max(s_c, axis=1, keepdims=True), jnp.max(s_p, axis=1, keepdims=True))
            p_c, p_p = jnp.exp(s_c - m), jnp.exp(s_p - m)
            l = jnp.sum(p_c, axis=1, keepdims=True) + jnp.sum(p_p, axis=1, keepdims=True)
            o = _bdot(p_c, v_ref[rows, :]) + _bdot(p_p, v_ref[prows, :])
            og[g][rows, :] = o / l
            ls[g][rows, :] = jnp.broadcast_to(m + jnp.log(l), (ATTN_BLOCK, LANES))

        _for_units(unit)
        l0, l1, l2 = ls[0][...], ls[1][...], ls[2][...]
        m = jnp.maximum(jnp.maximum(l0, l1), l2)
        e0, e1, e2 = jnp.exp(l0 - m), jnp.exp(l1 - m), jnp.exp(l2 - m)
        den = e0 + e1 + e2
        y_ref[...] = ((e0 * og[0][...] + e1 * og[1][...] + e2 * og[2][...]) / den).astype(y_ref.dtype)
        lse_ref[...] = m + jnp.log(den)

    blk = (SEQ, ATTN_HEAD_DIM)
    return pl.pallas_call(
        body, name=name, grid=(ATTN_KV_HEADS,), in_specs=[*q_specs, head, head, table, table, table],
        out_specs=[head, head], out_shape=[SDS((SEQ, ATTN_OUT), BF16), SDS((SEQ, ATTN_OUT), F32)],
        scratch_shapes=[pltpu.VMEM(blk, F32)] * (3 * ATTN_N_PAT + 1),
        compiler_params=_cparams(("parallel",)),
    )(q, q, q, k, v, *tabs)


def attn_bwd(q, k, v, tabs, y, lse, dy, name):
    q_specs, head, table = _head_specs(ATTN_N_PAT)

    def body(q0_ref, q1_ref, q2_ref, k_ref, v_ref, c_ref, sa_ref, sb_ref, y_ref, lse_ref, dy_ref,
             dq0_ref, dq1_ref, dq2_ref, dk_ref, dv_ref, *scr):
        qs, dqs, ks, dks, dd = scr[0:3], scr[3:6], scr[6], scr[7], scr[8]
        c, sa, sb = c_ref[...], sa_ref[...], sb_ref[...]
        for g, q_ref in enumerate((q0_ref, q1_ref, q2_ref)):
            qs[g][...] = _rope(q_ref[...], c, sa, sb)
        ks[...] = _rope(k_ref[...], c, sa, sb)
        dks[...] = jnp.zeros_like(dks)
        dv_ref[...] = jnp.zeros_like(dv_ref)
        dyv = dy_ref[...]
        dd[...] = jnp.broadcast_to(jnp.sum(dyv * y_ref[...].astype(F32), axis=1, keepdims=True), dd.shape)

        def unit(g, rows, prows, has_prev):
            qv = qs[g][rows, :].astype(BF16)
            kc, kp = ks[rows, :].astype(BF16), ks[prows, :].astype(BF16)
            vc, vp = v_ref[rows, :].astype(BF16), v_ref[prows, :].astype(BF16)
            do = dy_ref[rows, :].astype(BF16)
            s_p, s_c = _attn_scores(qv, kp, kc, has_prev)
            lse_u = lse_ref[rows, :][:, 0:1]
            dsum = dd[rows, :][:, 0:1]
            p_c, p_p = jnp.exp(s_c - lse_u), jnp.exp(s_p - lse_u)
            ds_c = (p_c * (_dot(do, vc, NT) - dsum) * ATTN_SCALE).astype(BF16)
            ds_p = (p_p * (_dot(do, vp, NT) - dsum) * ATTN_SCALE).astype(BF16)
            dqs[g][rows, :] = _dot(ds_c, kc) + _dot(ds_p, kp)
            dks[rows, :] += _dot(ds_c, qv, TN)
            dks[prows, :] += _dot(ds_p, qv, TN)
            dv_ref[rows, :] += _bdot(p_c, do, TN)
            dv_ref[prows, :] += _bdot(p_p, do, TN)

        _for_units(unit)
        for g, dq_ref in enumerate((dq0_ref, dq1_ref, dq2_ref)):
            dq_ref[...] = _rope(dqs[g][...], c, -sa, -sb)
        dk_ref[...] = _rope(dks[...], c, -sa, -sb)

    blk = (SEQ, ATTN_HEAD_DIM)
    out = SDS((SEQ, ATTN_OUT), F32)
    return pl.pallas_call(
        body, name=name, grid=(ATTN_KV_HEADS,), in_specs=[*q_specs, head, head, table, table, table, head, head, head],
        out_specs=[head] * 5, out_shape=[out] * 5,
        scratch_shapes=[pltpu.VMEM(blk, F32)] * (2 * ATTN_N_PAT + 3),
        compiler_params=_cparams(("parallel",)),
    )(q, q, q, k, v, *tabs, y, lse, dy)


def layer_fwd(h, w, small, tabs, li):
    n = f"l{li}_"
    sv = {}
    u = rms_fwd(h, small["norm_mix"], n + "rms_mix")
    z = matmul(u, w["w_z"], name=n + "mm_z", tb=True)
    xbc = matmul(u, w["w_xbc"], name=n + "mm_xbc", tb=True)
    dtr = matmul(u, w["w_dt"], name=n + "mm_dt", tb=True)
    q = matmul(u, w["w_q"], name=n + "mm_q", tb=True)
    k = matmul(u, w["w_k"], name=n + "mm_k", tb=True)
    v = matmul(u, w["w_v"], name=n + "mm_v", tb=True)
    gs = matmul(u, w["w_gs"], name=n + "mm_gs", tb=True)
    ga = matmul(u, w["w_ga"], name=n + "mm_ga", tb=True)
    xc = conv_fwd(xbc, w["conv_w"], small["conv_b"], n + "conv")
    dtr_t = dtr.T
    y_ssd, hs = ssd_fwd(xc, dtr, dtr_t, small["dt_bias"], small["dt_bias"].T, small["a_log"], small["a_log"].T, n + "ssd")
    yn = ssd_post_fwd(y_ssd, xc, z, small["d_skip_x"], small["ssd_norm"], n + "ssd_post")
    y_attn, lse = attn_fwd(q, k, v, tabs, n + "attn")
    a = matmul(yn, w["w_ssd_branch"], name=n + "mm_a")
    b = matmul(y_attn, w["w_attn_branch"], name=n + "mm_b")
    merged = gate_fwd(a, b, gs, ga, n + "gate")
    h1 = matmul(merged, w["w_out"], name=n + "mm_o", add=h)
    u2 = rms_fwd(h1, small["norm_ffn"], n + "rms_ffn")
    gu = matmul(u2, w["w_gate_up"], name=n + "mm_gu", tb=True)
    act = swiglu_fwd(gu, n + "swiglu")
    h2 = matmul(act, w["w_down"], name=n + "mm_down", add=h1)
    sv.update(h=h, u=u, z=z, xbc=xbc, dtr=dtr, dtr_t=dtr_t, gs=gs, ga=ga, xc=xc, y_ssd=y_ssd, hs=hs, yn=yn,
              q=q, k=k, v=v, y_attn=y_attn, lse=lse, a=a, b=b, merged=merged, h1=h1, u2=u2, gu=gu, act=act)
    return h2, sv


def layer_bwd(dh, sv, w, small, tabs, li):
    n = f"l{li}_b_"
    gw, gsm = {}, {}
    dact = matmul(dh, w["w_down"], name=n + "mm_dact", tb=True, out_dtype=BF16)
    gw["w_down"] = matmul(sv["act"], dh, name=n + "mm_dwdown", ta=True, out_dtype=BF16)
    dgu = swiglu_bwd(sv["gu"], dact, n + "swiglu")
    du2 = matmul(dgu, w["w_gate_up"], name=n + "mm_du2")
    gw["w_gate_up"] = matmul(dgu, sv["u2"], name=n + "mm_dwgu", ta=True, out_dtype=BF16)
    dh1, gsm["norm_ffn"] = rms_bwd(sv["h1"], du2, dh, small["norm_ffn"], n + "rms_ffn")
    dmerged = matmul(dh1, w["w_out"], name=n + "mm_dmerged", tb=True)
    gw["w_out"] = matmul(sv["merged"], dh1, name=n + "mm_dwo", ta=True, out_dtype=BF16)
    da, db, dgs, dga = gate_bwd(sv["a"], sv["b"], sv["gs"], sv["ga"], dmerged, n + "gate")
    dyn = matmul(da, w["w_ssd_branch"], name=n + "mm_dyn", tb=True)
    gw["w_ssd_branch"] = matmul(sv["yn"], da, name=n + "mm_dwa", ta=True, out_dtype=BF16)
    dyattn = matmul(db, w["w_attn_branch"], name=n + "mm_dyattn", tb=True)
    gw["w_attn_branch"] = matmul(sv["y_attn"], db, name=n + "mm_dwb", ta=True, out_dtype=BF16)
    dy_ssd, dxs_extra, dz, gsm["d_skip_x"], gsm["ssd_norm"] = ssd_post_bwd(
        sv["y_ssd"], sv["xc"], sv["z"], small["d_skip_x"], small["ssd_norm"], dyn, n + "ssd_post")
    dxc, ddtr, ddtr_t, ddtb, ddtb_t, dal, dal_t = ssd_bwd(
        sv["xc"], sv["dtr"], sv["dtr_t"], small["dt_bias"], small["dt_bias"].T, small["a_log"], small["a_log"].T,
        sv["hs"], dy_ssd, dxs_extra, n + "ssd")
    ddtr = (ddtr + ddtr_t.T).astype(BF16)
    gsm["dt_bias"] = ddtb + ddtb_t.T
    gsm["a_log"] = dal + dal_t.T
    dxbc, gw["conv_w"], gsm["conv_b"] = conv_bwd(sv["xbc"], w["conv_w"], small["conv_b"], dxc, n + "conv")
    dq0, dq1, dq2, dk, dv = attn_bwd(sv["q"], sv["k"], sv["v"], tabs, sv["y_attn"], sv["lse"], dyattn, n + "attn")
    u = sv["u"]
    segs = [("w_z", dz), ("w_xbc", dxbc), ("w_dt", ddtr), ("w_q0", dq0), ("w_q1", dq1), ("w_q2", dq2),
            ("w_k", dk), ("w_v", dv), ("w_gs", dgs), ("w_ga", dga)]
    du = None
    gin = []
    for key, dseg in segs:
        du = matmul(dseg, w[key], name=n + "mm_du_" + key, add=du)
        gin.append(matmul(dseg, u, name=n + "mm_d" + key, ta=True, out_dtype=BF16))
    gin[2] = gin[2][:SSD_HEADS]
    gw["w_in"] = jnp.concatenate(gin, axis=0)
    dh0, gsm["norm_mix"] = rms_bwd(sv["h"], du, dh1, small["norm_mix"], n + "rms_mix")
    return dh0, gw, gsm


def _my_place():
    return lax.axis_index("x"), lax.axis_index("y"), lax.axis_index("c")


def _flip(place, k):
    x, y, c = place
    return (1 - x if k & 4 else x, 1 - y if k & 2 else y, 1 - c if k & 1 else c)


def _index(place):
    return 4 * place[0] + 2 * place[1] + place[2]


ANY = pl.BlockSpec(memory_space=pl.ANY)
CHIP_FLIPS = (4, 2, 6)


def all_gather(xs, name):
    na = len(xs)

    def body(*refs):
        x_refs, o_refs = refs[:na], refs[na:2 * na]
        send_sems, recv_sems, local_sems = refs[2 * na:]
        me = _my_place()
        sibling = _flip(me, 1)
        chips = [_flip(me, f) for f in CHIP_FLIPS]

        def copy(a, kk, block, to, src=None):
            dst = o_refs[a].at[_index(block)]
            return pltpu.make_async_remote_copy(
                src_ref=dst if src is None else src, dst_ref=dst, send_sem=send_sems.at[a, kk],
                recv_sem=recv_sems.at[a, kk], device_id=to, device_id_type=MESH)

        mine = [pltpu.make_async_copy(x_refs[a], o_refs[a].at[_index(me)], local_sems.at[a]) for a in range(na)]
        for cp in mine:
            cp.start()
        first = []
        for j, chip in enumerate(chips):
            first += [copy(a, 1 + j, me, chip, src=x_refs[a]) for a in range(na)]
        first += [copy(a, 0, me, sibling, src=x_refs[a]) for a in range(na)]
        for cp in first:
            cp.start()
        passed = []
        for j, chip in enumerate(chips):
            for a in range(na):
                copy(a, 1 + j, chip, me).wait_recv()
                cp = copy(a, 4 + j, chip, sibling)
                cp.start()
                passed.append(cp)
        for a in range(na):
            copy(a, 0, sibling, me).wait_recv()
        for j, chip in enumerate(chips):
            for a in range(na):
                copy(a, 4 + j, _flip(chip, 1), me).wait_recv()
        for cp in first + passed:
            cp.wait_send()
        for cp in mine:
            cp.wait()

    return pl.pallas_call(
        body, name=name, in_specs=[ANY] * na, out_specs=[ANY] * na,
        out_shape=[SDS((N_DEV,) + t.shape, t.dtype) for t in xs],
        scratch_shapes=[pltpu.SemaphoreType.DMA((na, N_DEV - 1)), pltpu.SemaphoreType.DMA((na, N_DEV - 1)),
                        pltpu.SemaphoreType.DMA((na,))],
    )(*xs)


def all_to_all(parts, name):
    na = len(parts)

    def body(*refs):
        p_refs, o_refs = refs[:na], refs[na:2 * na]
        send_sems, recv_sems, local_sems = refs[2 * na:]
        me = _my_place()
        mine = [pltpu.make_async_copy(p_refs[a].at[_index(me)], o_refs[a].at[_index(me)], local_sems.at[a]) for a in range(na)]
        for cp in mine:
            cp.start()
        copies = []
        for kk in range(1, N_DEV):
            peer = _flip(me, kk)
            for a in range(na):
                cp = pltpu.make_async_remote_copy(
                    src_ref=p_refs[a].at[_index(peer)], dst_ref=o_refs[a].at[_index(me)], send_sem=send_sems.at[a, kk - 1],
                    recv_sem=recv_sems.at[a, kk - 1], device_id=peer, device_id_type=MESH)
                cp.start()
                copies.append(cp)
        for cp in copies:
            cp.wait()
        for cp in mine:
            cp.wait()

    return pl.pallas_call(
        body, name=name, in_specs=[ANY] * na, out_specs=[ANY] * na,
        out_shape=[SDS(t.shape, t.dtype) for t in parts],
        scratch_shapes=[pltpu.SemaphoreType.DMA((na, N_DEV - 1)), pltpu.SemaphoreType.DMA((na, N_DEV - 1)),
                        pltpu.SemaphoreType.DMA((na,))],
    )(*parts)


def sum_parts(parts, name):
    _, r, c = parts.shape
    tc = _pick(c, (256, 128))

    def body(p_ref, o_ref):
        acc = p_ref[0].astype(F32)
        for i in range(1, N_DEV):
            acc = acc + p_ref[i].astype(F32)
        o_ref[...] = acc

    return pl.pallas_call(
        body, name=name, grid=(c // tc,), in_specs=[pl.BlockSpec((N_DEV, r, tc), lambda i: (0, 0, i))],
        out_specs=pl.BlockSpec((r, tc), lambda i: (0, i)), out_shape=SDS((r, c), F32),
        compiler_params=_cparams(("parallel",)),
    )(parts)


def adamw(w, g, m, v, name):
    shape = w.shape
    cols = shape[-1]
    rows = w.size // cols
    tr = _pick(rows, (256, 128, 64, 32, 16, 8))
    c1 = 1.0 / (1.0 - ADAM_B1 ** ADAM_STEP)
    c2 = 1.0 / (1.0 - ADAM_B2 ** ADAM_STEP)

    def body(w_ref, g_ref, m_ref, v_ref, d_ref, nm_ref, nv_ref):
        gg = g_ref[...]
        nm = ADAM_B1 * m_ref[...] + (1.0 - ADAM_B1) * gg
        nv = ADAM_B2 * v_ref[...] + (1.0 - ADAM_B2) * (gg * gg)
        d_ref[...] = -ADAM_LR * ((nm * c1) / (jnp.sqrt(nv * c2) + ADAM_EPS) + ADAM_WD * w_ref[...])
        nm_ref[...] = nm
        nv_ref[...] = nv

    spec = pl.BlockSpec((tr, cols), lambda i: (i, 0))
    outs = pl.pallas_call(
        body, name=name, grid=(rows // tr,), in_specs=[spec] * 4, out_specs=[spec] * 3,
        out_shape=[SDS((rows, cols), F32)] * 3, compiler_params=_cparams(("parallel",)),
    )(*[t.reshape(rows, cols) for t in (w, g, m, v)])
    return [o.reshape(shape) for o in outs]


BIG = ("w_in", "conv_w", "w_ssd_branch", "w_attn_branch", "w_out", "w_gate_up", "w_down")
TRANSPOSED = ("w_in", "w_gate_up")
SMALL = ("norm_mix", "conv_b", "dt_bias", "a_log", "d_skip", "ssd_norm", "norm_ffn")
SMALL_SIZE = {"norm_mix": 1024, "conv_b": 3072, "dt_bias": 32, "a_log": 32, "d_skip": 32, "ssd_norm": 2048, "norm_ffn": 1024}
FLAT_W = 512
SMALL_TOTAL = DEPTH * sum(SMALL_SIZE.values()) + D_MODEL + LANES
SMALL_ROWS = 32
assert SMALL_ROWS * FLAT_W >= SMALL_TOTAL


def to_wire(k, shard):
    if k in TRANSPOSED:
        return shard.transpose(0, 2, 1).astype(BF16)
    return shard if k == "conv_w" else shard.astype(BF16)


def full_weights(gathered, li):
    w = {}
    for k in BIG:
        g = gathered[k][:, li]
        if k == "conv_w":
            w[k] = g.transpose(1, 0, 2).reshape(SSD_CONV, SSD_CONV_CH)
        else:
            w[k] = g.reshape(-1, g.shape[-1])
    w_in = w.pop("w_in")
    off = 0
    for nm, r in IN_ROWS:
        w[nm] = w_in[off:off + r]
        off += r
    w["w_q"] = w_in[sum(r for _, r in IN_ROWS[:3]):sum(r for _, r in IN_ROWS[:6])]
    w["w_dt"] = jnp.pad(w["w_dt"], ((0, HPAD - SSD_HEADS), (0, 0)))
    return w


def grads_to_wire(k, g):
    if k == "conv_w":
        return g.reshape(SSD_CONV, N_DEV, SSD_CONV_CH // N_DEV).transpose(1, 0, 2)
    return g.reshape(N_DEV, g.shape[0] // N_DEV, g.shape[1])


def _pad_heads(t):
    return jnp.pad(t.reshape(1, SSD_HEADS), ((0, 0), (0, HPAD - SSD_HEADS)))


def local_step(x, target, weights, smalls, norm_final):
    tabs = rope_tables()
    sms = []
    for li in range(DEPTH):
        s = smalls[li]
        sms.append({
            "norm_mix": s["norm_mix"].reshape(1, -1), "conv_b": s["conv_b"].reshape(1, -1),
            "dt_bias": _pad_heads(s["dt_bias"]), "a_log": _pad_heads(s["a_log"]),
            "d_skip_x": jnp.repeat(s["d_skip"], SSD_HEAD_DIM).reshape(1, -1),
            "ssd_norm": s["ssd_norm"].reshape(1, -1), "norm_ffn": s["norm_ffn"].reshape(1, -1)})
    h = x
    saved = []
    for li in range(DEPTH):
        h, sv = layer_fwd(h, weights[li], sms[li], tabs, li)
        saved.append(sv)
    dh, g_final, loss = loss_head(h, target, norm_final.reshape(1, -1), "loss_head")
    gws, gsms = [None] * DEPTH, [None] * DEPTH
    for li in reversed(range(DEPTH)):
        dh, gw, gsm = layer_bwd(dh, saved[li], weights[li], sms[li], tabs, li)
        gws[li] = {k: gw[k] for k in BIG}
        gsms[li] = {
            "norm_mix": gsm["norm_mix"].reshape(-1), "conv_b": gsm["conv_b"].reshape(-1),
            "dt_bias": gsm["dt_bias"][0, :SSD_HEADS], "a_log": gsm["a_log"][0, :SSD_HEADS],
            "d_skip": gsm["d_skip_x"].reshape(SSD_HEADS, SSD_HEAD_DIM).sum(axis=1),
            "ssd_norm": gsm["ssd_norm"].reshape(-1), "norm_ffn": gsm["norm_ffn"].reshape(-1)}
    return loss, dh, gws, gsms, g_final.reshape(-1)


def kernel(x, norm_mix, w_in, conv_w, conv_b, dt_bias, a_log, d_skip, ssd_norm, w_ssd_branch, w_attn_branch, w_out, norm_ffn, w_gate_up, w_down, norm_final, loss_target, m_norm_mix, m_w_in, m_conv_w, m_conv_b, m_dt_bias, m_a_log, m_d_skip, m_ssd_norm, m_w_ssd_branch, m_w_attn_branch, m_w_out, m_norm_ffn, m_w_gate_up, m_w_down, m_norm_final, v_norm_mix, v_w_in, v_conv_w, v_conv_b, v_dt_bias, v_a_log, v_d_skip, v_ssd_norm, v_w_ssd_branch, v_w_attn_branch, v_w_out, v_norm_ffn, v_w_gate_up, v_w_down, v_norm_final):
    wv = dict(norm_mix=norm_mix, w_in=w_in, conv_w=conv_w, conv_b=conv_b, dt_bias=dt_bias, a_log=a_log, d_skip=d_skip,
              ssd_norm=ssd_norm, w_ssd_branch=w_ssd_branch, w_attn_branch=w_attn_branch, w_out=w_out, norm_ffn=norm_ffn,
              w_gate_up=w_gate_up, w_down=w_down, norm_final=norm_final)
    mv = dict(norm_mix=m_norm_mix, w_in=m_w_in, conv_w=m_conv_w, conv_b=m_conv_b, dt_bias=m_dt_bias, a_log=m_a_log,
              d_skip=m_d_skip, ssd_norm=m_ssd_norm, w_ssd_branch=m_w_ssd_branch, w_attn_branch=m_w_attn_branch,
              w_out=m_w_out, norm_ffn=m_norm_ffn, w_gate_up=m_w_gate_up, w_down=m_w_down, norm_final=m_norm_final)
    vv = dict(norm_mix=v_norm_mix, w_in=v_w_in, conv_w=v_conv_w, conv_b=v_conv_b, dt_bias=v_dt_bias, a_log=v_a_log,
              d_skip=v_d_skip, ssd_norm=v_ssd_norm, w_ssd_branch=v_w_ssd_branch, w_attn_branch=v_w_attn_branch,
              w_out=v_w_out, norm_ffn=v_norm_ffn, w_gate_up=v_w_gate_up, w_down=v_w_down, norm_final=v_norm_final)
    order = ("norm_mix", "w_in", "conv_w", "conv_b", "dt_bias", "a_log", "d_skip", "ssd_norm", "w_ssd_branch",
             "w_attn_branch", "w_out", "norm_ffn", "w_gate_up", "w_down", "norm_final")

    gathered = dict(zip(BIG, all_gather([to_wire(k, wv[k]) for k in BIG], "gather_weights")))
    weights = [full_weights(gathered, li) for li in range(DEPTH)]
    smalls = [{k: wv[k][li] for k in SMALL} for li in range(DEPTH)]

    loss_p, dx, gws, gsms, g_final = local_step(x[0], loss_target[0], weights, smalls, norm_final)

    shard_g = {k: [] for k in BIG}
    for li in range(DEPTH):
        recv = all_to_all([grads_to_wire(k, gws[li][k]) for k in BIG], f"exchange_grads_{li}")
        for k, r in zip(BIG, recv):
            if k == "conv_w":
                r = r.reshape(N_DEV, 1, -1)
            g = sum_parts(r, f"sum_{k}_{li}")
            shard_g[k].append(g.T if k in TRANSPOSED else g.reshape(wv[k].shape[1:]))
    grads = {k: jnp.stack(shard_g[k]) for k in BIG}

    flat = [gsms[li][k] for li in range(DEPTH) for k in SMALL] + [g_final, loss_p.reshape(-1)]
    flat.append(jnp.zeros((SMALL_ROWS * FLAT_W - SMALL_TOTAL,), F32))
    small_all = all_gather([jnp.concatenate(flat).reshape(SMALL_ROWS, FLAT_W)], "gather_small")[0]
    small_sum = sum_parts(small_all, "sum_small").reshape(-1)
    off = 0
    per_layer = {k: [] for k in SMALL}
    for li in range(DEPTH):
        for k in SMALL:
            per_layer[k].append(small_sum[off:off + SMALL_SIZE[k]])
            off += SMALL_SIZE[k]
    for k in SMALL:
        grads[k] = jnp.stack(per_layer[k])
    grads["norm_final"] = small_sum[off:off + D_MODEL]
    loss = small_sum[off + D_MODEL]

    deltas, new_m, new_v = {}, {}, {}
    for k in order:
        w2, g2, m2, v2 = wv[k], grads[k], mv[k], vv[k]
        if w2.ndim == 1:
            w2, g2, m2, v2 = (t.reshape(1, -1) for t in (w2, g2, m2, v2))
        d, nm, nv = adamw(w2, g2, m2, v2, "adamw_" + k)
        deltas[k], new_m[k], new_v[k] = (t.reshape(wv[k].shape) for t in (d, nm, nv))

    return (loss, dx.reshape(x.shape), *[grads[k] for k in order], *[deltas[k] for k in order],
            *[new_m[k] for k in order], *[new_v[k] for k in order])
```

```python
import functools

import jax
import jax.numpy as jnp
from jax import lax
from jax.experimental import pallas as pl
from jax.experimental.pallas import tpu as pltpu

F32, BF16 = jnp.float32, jnp.bfloat16
SDS = jax.ShapeDtypeStruct
MESH = pl.DeviceIdType.MESH

D_MODEL = 1024
SEQ = 2048
DEPTH = 2
RMS_EPS = 1e-5
SSD_INNER = 2048
SSD_HEAD_DIM = 64
SSD_HEADS = 32
SSD_STATE = 128
SSD_GROUPS = 4
SSD_CONV = 4
SSD_CHUNK = 128
SSD_CONV_CH = 3072
ATTN_HEAD_DIM = 128
ATTN_KV_HEADS = 8
ATTN_DILATIONS = (1, 4, 16)
ATTN_N_PAT = 3
ATTN_BLOCK = 128
ATTN_OUT = 1024
ROPE_THETA = 500000.0
ROPE_DIM = 32
FFN_HIDDEN = 2816
ADAM_LR, ADAM_B1, ADAM_B2, ADAM_EPS, ADAM_WD, ADAM_STEP = 0.001, 0.9, 0.999, 1e-08, 0.01, 10

N_DEV = 8
LANES = 128
VMEM_LIMIT = 56 * 1024 * 1024
HPAD = 128
HIGHEST = lax.Precision.HIGHEST

IN_ROWS = (("w_z", 2048), ("w_xbc", 3072), ("w_dt", 32), ("w_q0", 1024), ("w_q1", 1024), ("w_q2", 1024),
           ("w_k", 1024), ("w_v", 1024), ("w_gs", 1024), ("w_ga", 1024))
N_IN = sum(r for _, r in IN_ROWS)


def _cparams(sem):
    return pltpu.CompilerParams(dimension_semantics=sem, vmem_limit_bytes=VMEM_LIMIT)


def _sigmoid(x):
    return 1.0 / (1.0 + jnp.exp(-x))


def _silu(x):
    return x * _sigmoid(x)


def _softplus(x):
    return jnp.maximum(x, 0.0) + jnp.log(1.0 + jnp.exp(-jnp.abs(x)))


def _dot(a, b, dims=(((1,), (0,)), ((), ())), precision=None):
    return lax.dot_general(a, b, dims, precision=precision, preferred_element_type=F32)


NT = (((1,), (1,)), ((), ()))
TN = (((0,), (0,)), ((), ()))


def _bdot(a, b, dims=(((1,), (0,)), ((), ()))):
    return _dot(a.astype(BF16), b.astype(BF16), dims)


def _pick(dim, cands):
    for c in cands:
        if dim % c == 0:
            return c
    return dim


def matmul(a, b, *, name, ta=False, tb=False, out_dtype=F32, add=None):
    m, k = (a.shape[1], a.shape[0]) if ta else a.shape
    n = b.shape[0] if tb else b.shape[1]
    tm = _pick(m, (1024, 1408, 512, 256, 128))
    tn = _pick(n, (512, 256, 128))
    tk = _pick(k, (1024, 1408, 512, 256, 128))
    nk = k // tk
    a_spec = pl.BlockSpec((tk, tm), lambda i, j, kk: (kk, i)) if ta else pl.BlockSpec((tm, tk), lambda i, j, kk: (i, kk))
    b_spec = pl.BlockSpec((tn, tk), lambda i, j, kk: (j, kk)) if tb else pl.BlockSpec((tk, tn), lambda i, j, kk: (kk, j))
    dims = (((0 if ta else 1,), (1 if tb else 0,)), ((), ()))
    has_add = add is not None

    def body(*refs):
        if has_add:
            a_ref, b_ref, add_ref, o_ref, acc = refs
        else:
            a_ref, b_ref, o_ref, acc = refs
        kk = pl.program_id(2)

        @pl.when(kk == 0)
        def _():
            acc[...] = jnp.zeros_like(acc)

        acc[...] += _dot(a_ref[...].astype(BF16), b_ref[...].astype(BF16), dims)

        @pl.when(kk == nk - 1)
        def _():
            r = acc[...]
            if has_add:
                r = r + add_ref[...].astype(F32)
            o_ref[...] = r.astype(o_ref.dtype)

    in_specs = [a_spec, b_spec]
    args = [a, b]
    if has_add:
        in_specs.append(pl.BlockSpec((tm, tn), lambda i, j, kk: (i, j)))
        args.append(add)
    return pl.pallas_call(
        body, name=name, grid=(m // tm, n // tn, nk),
        in_specs=in_specs, out_specs=pl.BlockSpec((tm, tn), lambda i, j, kk: (i, j)),
        out_shape=SDS((m, n), out_dtype), scratch_shapes=[pltpu.VMEM((tm, tn), F32)],
        compiler_params=_cparams(("parallel", "parallel", "arbitrary")),
    )(*args)


def rowcall(name, fn, rows, params, row_outs, red_outs=(), tr=256):
    s = rows[0].shape[0]
    n_in = len(rows) + len(params)
    n_row = len(row_outs)

    def body(*refs):
        outs = fn(*[r[...] for r in refs[:n_in]])
        if not isinstance(outs, (tuple, list)):
            outs = (outs,)
        orefs = refs[n_in:]
        for r, o in zip(orefs[:n_row], outs[:n_row]):
            r[...] = o.astype(r.dtype)
        if red_outs:
            @pl.when(pl.program_id(0) == 0)
            def _():
                for r in orefs[n_row:]:
                    r[...] = jnp.zeros_like(r)
            for r, o in zip(orefs[n_row:], outs[n_row:]):
                r[...] += o.astype(F32)

    in_specs = [pl.BlockSpec((tr, a.shape[1]), lambda i: (i, 0)) for a in rows]
    in_specs += [pl.BlockSpec(p.shape, lambda i: (0, 0)) for p in params]
    out_specs = [pl.BlockSpec((tr, c), lambda i: (i, 0)) for c, _ in row_outs]
    out_specs += [pl.BlockSpec(shp, lambda i: (0, 0)) for shp in red_outs]
    out_shape = [SDS((s, c), dt) for c, dt in row_outs] + [SDS(shp, F32) for shp in red_outs]
    res = pl.pallas_call(
        body, name=name, grid=(s // tr,), in_specs=in_specs, out_specs=out_specs, out_shape=out_shape,
        compiler_params=_cparams(("arbitrary",) if red_outs else ("parallel",)),
    )(*rows, *params)
    return res


def _rms(x, w):
    return x * lax.rsqrt(jnp.mean(x * x, axis=-1, keepdims=True) + RMS_EPS) * w


def rms_fwd(h, w, name):
    return rowcall(name, _rms, [h], [w], [(D_MODEL, BF16)])[0]


def rms_bwd(h, du, dres, w, name):
    def fn(hb, dub, dresb, wb):
        _, vjp = jax.vjp(_rms, hb, wb)
        dh, dw = vjp(dub)
        return dh + dresb, dw
    return rowcall(name, fn, [h, du, dres], [w], [(D_MODEL, F32)], [(1, D_MODEL)])


def loss_head(h, target, w, name):
    def fn(hb, tb, wb):
        def f(hh, ww):
            err = _rms(hh, ww) - tb
            return 0.5 * jnp.sum(jnp.mean(err * err, axis=-1, keepdims=True), axis=0, keepdims=True)
        val, vjp = jax.vjp(f, hb, wb)
        dh, dw = vjp(jnp.ones((1, 1), F32))
        return dh, dw, jnp.broadcast_to(val, (1, LANES))
    return rowcall(name, fn, [h, target], [w], [(D_MODEL, F32)], [(1, D_MODEL), (1, LANES)])


def _gate(a, b, gs, ga):
    return _sigmoid(gs) * a + _sigmoid(ga) * b


def gate_fwd(a, b, gs, ga, name):
    return rowcall(name, _gate, [a, b, gs, ga], [], [(D_MODEL, BF16)])[0]


def gate_bwd(a, b, gs, ga, dm, name):
    def fn(ab, bb, gsb, gab, dmb):
        _, vjp = jax.vjp(_gate, ab, bb, gsb, gab)
        return vjp(dmb)
    return rowcall(name, fn, [a, b, gs, ga, dm], [], [(D_MODEL, BF16)] * 4)


def _swiglu(gu):
    return _silu(gu[:, :FFN_HIDDEN]) * gu[:, FFN_HIDDEN:]


def swiglu_fwd(gu, name):
    return rowcall(name, _swiglu, [gu], [], [(FFN_HIDDEN, BF16)])[0]


def swiglu_bwd(gu, dact, name):
    def fn(gub, db):
        _, vjp = jax.vjp(_swiglu, gub)
        return vjp(db.astype(F32))[0]
    return rowcall(name, fn, [gu, dact], [], [(2 * FFN_HIDDEN, BF16)])[0]


def _ssd_post(y, xs, z, dskip, normw):
    y = (y + dskip * xs) * _silu(z)
    gw = SSD_INNER // SSD_GROUPS
    parts = []
    for g in range(SSD_GROUPS):
        yg = y[:, g * gw:(g + 1) * gw]
        parts.append(yg * lax.rsqrt(jnp.mean(yg * yg, axis=-1, keepdims=True) + RMS_EPS))
    return jnp.concatenate(parts, axis=-1) * normw


def ssd_post_fwd(y, xc, z, dskip, normw, name):
    def fn(yb, xcb, zb, db, nb):
        return _ssd_post(yb, xcb[:, :SSD_INNER], zb, db, nb)
    return rowcall(name, fn, [y, xc, z], [dskip, normw], [(SSD_INNER, BF16)])[0]


def ssd_post_bwd(y, xc, z, dskip, normw, dyn, name):
    def fn(yb, xcb, zb, dynb, db, nb):
        _, vjp = jax.vjp(_ssd_post, yb, xcb[:, :SSD_INNER], zb, db, nb)
        return vjp(dynb)
    return rowcall(name, fn, [y, xc, z, dyn], [dskip, normw],
                   [(SSD_INNER, F32), (SSD_INNER, F32), (SSD_INNER, BF16)], [(1, SSD_INNER), (1, SSD_INNER)])


def _rope(t, cosf, sina, sinb):
    return t * cosf + pltpu.roll(t, LANES - ROPE_DIM // 2, 1) * sina + pltpu.roll(t, ROPE_DIM // 2, 1) * sinb


def rope_tables():
    half = ROPE_DIM // 2
    inv = ROPE_THETA ** (-jnp.arange(0, ROPE_DIM, 2, dtype=F32) / ROPE_DIM)
    ang = jnp.arange(SEQ, dtype=F32)[:, None] * inv[None, :]
    cos, sin = jnp.cos(ang), jnp.sin(ang)
    zeros = jnp.zeros((SEQ, LANES - ROPE_DIM), F32)
    z16 = jnp.zeros((SEQ, half), F32)
    cosf = jnp.concatenate([cos, cos, jnp.ones((SEQ, LANES - ROPE_DIM), F32)], axis=1)
    sina = jnp.concatenate([-sin, z16, zeros], axis=1)
    sinb = jnp.concatenate([z16, sin, zeros], axis=1)
    return cosf, sina, sinb


CONV_TC = 256


def _conv_pre(x, w, b, row):
    acc = x * w[SSD_CONV - 1:SSD_CONV, :] + b
    shifted = [x]
    for j in range(1, SSD_CONV):
        xs = jnp.where(row >= j, pltpu.roll(x, j, 0), 0.0)
        shifted.append(xs)
        acc = acc + xs * w[SSD_CONV - 1 - j:SSD_CONV - j, :]
    return acc, shifted


def conv_fwd(xbc, w, b, name):
    def body(x_ref, w_ref, b_ref, o_ref):
        row = lax.broadcasted_iota(jnp.int32, (SEQ, CONV_TC), 0)
        pre, _ = _conv_pre(x_ref[...], w_ref[...], b_ref[...], row)
        o_ref[...] = _silu(pre)
    return pl.pallas_call(
        body, name=name, grid=(SSD_CONV_CH // CONV_TC,),
        in_specs=[pl.BlockSpec((SEQ, CONV_TC), lambda i: (0, i)), pl.BlockSpec((SSD_CONV, CONV_TC), lambda i: (0, i)),
                  pl.BlockSpec((1, CONV_TC), lambda i: (0, i))],
        out_specs=pl.BlockSpec((SEQ, CONV_TC), lambda i: (0, i)),
        out_shape=SDS((SEQ, SSD_CONV_CH), F32), compiler_params=_cparams(("parallel",)),
    )(xbc, w, b)


def conv_bwd(xbc, w, b, dxc, name):
    def body(x_ref, w_ref, b_ref, dy_ref, dx_ref, dw_ref, db_ref):
        row = lax.broadcasted_iota(jnp.int32, (SEQ, CONV_TC), 0)
        wv = w_ref[...]
        pre, shifted = _conv_pre(x_ref[...], wv, b_ref[...], row)
        sg = _sigmoid(pre)
        ds = dy_ref[...] * (sg * (1.0 + pre * (1.0 - sg)))
        dx = ds * wv[SSD_CONV - 1:SSD_CONV, :]
        for j in range(1, SSD_CONV):
            dsj = jnp.where(row < SEQ - j, pltpu.roll(ds, SEQ - j, 0), 0.0)
            dx = dx + dsj * wv[SSD_CONV - 1 - j:SSD_CONV - j, :]
        dx_ref[...] = dx.astype(dx_ref.dtype)
        for j in range(SSD_CONV):
            dw_ref[SSD_CONV - 1 - j:SSD_CONV - j, :] = jnp.sum(ds * shifted[j], axis=0, keepdims=True)
        db_ref[...] = jnp.sum(ds, axis=0, keepdims=True)
    return pl.pallas_call(
        body, name=name, grid=(SSD_CONV_CH // CONV_TC,),
        in_specs=[pl.BlockSpec((SEQ, CONV_TC), lambda i: (0, i)), pl.BlockSpec((SSD_CONV, CONV_TC), lambda i: (0, i)),
                  pl.BlockSpec((1, CONV_TC), lambda i: (0, i)), pl.BlockSpec((SEQ, CONV_TC), lambda i: (0, i))],
        out_specs=[pl.BlockSpec((SEQ, CONV_TC), lambda i: (0, i)), pl.BlockSpec((SSD_CONV, CONV_TC), lambda i: (0, i)),
                   pl.BlockSpec((1, CONV_TC), lambda i: (0, i))],
        out_shape=[SDS((SEQ, SSD_CONV_CH), BF16), SDS((SSD_CONV, SSD_CONV_CH), F32), SDS((1, SSD_CONV_CH), F32)],
        compiler_params=_cparams(("parallel",)),
    )(xbc, w, b, dxc)


N_CHUNKS = SEQ // SSD_CHUNK
N_PAIRS = SSD_HEADS // 2
PAIRS_PER_GROUP = N_PAIRS // SSD_GROUPS
B_OFF = SSD_INNER
C_OFF = SSD_INNER + SSD_GROUPS * SSD_STATE


def _ssd_prefix(dtr, dtr_t, dtb, dtb_t, alog, alog_t):
    ln = SSD_CHUNK
    dt = _softplus(dtr + dtb)
    dt_t = _softplus(dtr_t + dtb_t)
    dta = dt * (-jnp.exp(alog))
    dta_t = dt_t * (-jnp.exp(alog_t))
    r = lax.broadcasted_iota(jnp.int32, (ln, ln), 0)
    c = lax.broadcasted_iota(jnp.int32, (ln, ln), 1)
    a_cum = _dot((r >= c).astype(F32), dta, precision=HIGHEST)
    a_cum_t = _dot(dta_t, (r <= c).astype(F32), precision=HIGHEST)
    a_last = jnp.sum(dta_t, axis=1, keepdims=True)
    return dt, a_cum, a_cum_t, a_last


def _ssd_pair(x_pair, bg, cg, hp, dt, a_cum, a_cum_t, a_last, *, e0):
    ln = SSD_CHUNK
    lane = lax.broadcasted_iota(jnp.int32, (ln, LANES), 1)
    sub = lax.broadcasted_iota(jnp.int32, (LANES, SSD_STATE), 0)
    row = lax.broadcasted_iota(jnp.int32, (ln, ln), 0)
    col = lax.broadcasted_iota(jnp.int32, (ln, ln), 1)
    lo = lane < SSD_HEAD_DIM
    e1 = e0 + 1
    c0, c1 = a_cum[:, e0:e0 + 1], a_cum[:, e1:e1 + 1]
    r0, r1 = a_cum_t[e0:e0 + 1, :], a_cum_t[e1:e1 + 1, :]
    l0, l1 = a_last[e0:e0 + 1, :], a_last[e1:e1 + 1, :]
    xd = x_pair * jnp.where(lo, dt[:, e0:e0 + 1], dt[:, e1:e1 + 1])
    causal = row >= col
    cb = _bdot(cg, bg, NT)
    m0 = cb * jnp.exp(jnp.where(causal, c0 - r0, -jnp.inf))
    m1 = cb * jnp.exp(jnp.where(causal, c1 - r1, -jnp.inf))
    y = _bdot(m0, jnp.where(lo, xd, 0.0)) + _bdot(m1, jnp.where(lo, 0.0, xd))
    acum_pair = jnp.where(lo, c0, c1)
    y = y + _bdot(cg, hp, NT) * jnp.exp(acum_pair)
    last_pair = jnp.where(lo, l0, l1)
    st = _bdot(xd * jnp.exp(last_pair - acum_pair), bg, TN)
    h_out = hp * jnp.exp(jnp.where(sub < SSD_HEAD_DIM, l0, l1)) + st
    return y, h_out


def _ssd_in_specs(chunk_of):
    return [
        pl.BlockSpec((SSD_CHUNK, SSD_CONV_CH), lambda i: (chunk_of(i), 0)),
        pl.BlockSpec((SSD_CHUNK, HPAD), lambda i: (chunk_of(i), 0)),
        pl.BlockSpec((HPAD, SSD_CHUNK), lambda i: (0, chunk_of(i))),
        pl.BlockSpec((1, HPAD), lambda i: (0, 0)), pl.BlockSpec((HPAD, 1), lambda i: (0, 0)),
        pl.BlockSpec((1, HPAD), lambda i: (0, 0)), pl.BlockSpec((HPAD, 1), lambda i: (0, 0)),
    ]


def ssd_fwd(xc, dtr, dtr_t, dtb, dtb_t, alog, alog_t, name):
    def body(xc_ref, dtr_ref, dtrt_ref, dtb_ref, dtbt_ref, al_ref, alt_ref, y_ref, hs_ref, h_scr):
        @pl.when(pl.program_id(0) == 0)
        def _():
            h_scr[...] = jnp.zeros_like(h_scr)

        hs_ref[0] = h_scr[...]
        dt, a_cum, a_cum_t, a_last = _ssd_prefix(dtr_ref[...], dtrt_ref[...], dtb_ref[...], dtbt_ref[...],
                                                  al_ref[...], alt_ref[...])
        for pr in range(N_PAIRS):
            g = pr // PAIRS_PER_GROUP
            sl = slice(pr * LANES, (pr + 1) * LANES)
            bg = xc_ref[:, B_OFF + g * SSD_STATE:B_OFF + (g + 1) * SSD_STATE]
            cg = xc_ref[:, C_OFF + g * SSD_STATE:C_OFF + (g + 1) * SSD_STATE]
            y, h_out = _ssd_pair(xc_ref[:, sl], bg, cg, h_scr[sl, :], dt, a_cum, a_cum_t, a_last, e0=2 * pr)
            y_ref[:, sl] = y
            h_scr[sl, :] = h_out

    return pl.pallas_call(
        body, name=name, grid=(N_CHUNKS,), in_specs=_ssd_in_specs(lambda i: i),
        out_specs=[pl.BlockSpec((SSD_CHUNK, SSD_INNER), lambda i: (i, 0)),
                   pl.BlockSpec((1, SSD_INNER, SSD_STATE), lambda i: (i, 0, 0))],
        out_shape=[SDS((SEQ, SSD_INNER), F32), SDS((N_CHUNKS, SSD_INNER, SSD_STATE), F32)],
        scratch_shapes=[pltpu.VMEM((SSD_INNER, SSD_STATE), F32)],
        compiler_params=_cparams(("arbitrary",)),
    )(xc, dtr, dtr_t, dtb, dtb_t, alog, alog_t)


def ssd_bwd(xc, dtr, dtr_t, dtb, dtb_t, alog, alog_t, hs, dy, dxs_extra, name):
    rev = lambda i: N_CHUNKS - 1 - i

    def body(xc_ref, dtr_ref, dtrt_ref, dtb_ref, dtbt_ref, al_ref, alt_ref, hs_ref, dy_ref, dxe_ref,
             dxc_ref, ddtr_ref, ddtrt_ref, ddtb_ref, ddtbt_ref, dal_ref, dalt_ref, dh_scr):
        @pl.when(pl.program_id(0) == 0)
        def _():
            dh_scr[...] = jnp.zeros_like(dh_scr)
            for r in (ddtb_ref, ddtbt_ref, dal_ref, dalt_ref):
                r[...] = jnp.zeros_like(r)

        prefix_in = (dtr_ref[...], dtrt_ref[...], dtb_ref[...], dtbt_ref[...], al_ref[...], alt_ref[...])
        (dt, a_cum, a_cum_t, a_last), prefix_vjp = jax.vjp(_ssd_prefix, *prefix_in)
        d_dt = jnp.zeros_like(dt)
        d_acum = jnp.zeros_like(a_cum)
        d_acum_t = jnp.zeros_like(a_cum_t)
        d_alast = jnp.zeros_like(a_last)
        for g in range(SSD_GROUPS):
            bg = xc_ref[:, B_OFF + g * SSD_STATE:B_OFF + (g + 1) * SSD_STATE]
            cg = xc_ref[:, C_OFF + g * SSD_STATE:C_OFF + (g + 1) * SSD_STATE]
            d_bg = jnp.zeros_like(bg)
            d_cg = jnp.zeros_like(cg)
            for j in range(PAIRS_PER_GROUP):
                pr = g * PAIRS_PER_GROUP + j
                sl = slice(pr * LANES, (pr + 1) * LANES)
                _, vjp = jax.vjp(functools.partial(_ssd_pair, e0=2 * pr),
                                 xc_ref[:, sl], bg, cg, hs_ref[0, sl, :], dt, a_cum, a_cum_t, a_last)
                dx, dbg, dcg, dhp, ddt, dac, dact, dal = vjp((dy_ref[:, sl], dh_scr[sl, :]))
                dxc_ref[:, sl] = dx + dxe_ref[:, sl]
                dh_scr[sl, :] = dhp
                d_bg, d_cg = d_bg + dbg, d_cg + dcg
                d_dt, d_acum, d_acum_t, d_alast = d_dt + ddt, d_acum + dac, d_acum_t + dact, d_alast + dal
            dxc_ref[:, B_OFF + g * SSD_STATE:B_OFF + (g + 1) * SSD_STATE] = d_bg
            dxc_ref[:, C_OFF + g * SSD_STATE:C_OFF + (g + 1) * SSD_STATE] = d_cg
        g_dtr, g_dtrt, g_dtb, g_dtbt, g_al, g_alt = prefix_vjp((d_dt, d_acum, d_acum_t, d_alast))
        ddtr_ref[...] = g_dtr
        ddtrt_ref[...] = g_dtrt
        ddtb_ref[...] += g_dtb
        ddtbt_ref[...] += g_dtbt
        dal_ref[...] += g_al
        dalt_ref[...] += g_alt

    in_specs = _ssd_in_specs(rev) + [
        pl.BlockSpec((1, SSD_INNER, SSD_STATE), lambda i: (rev(i), 0, 0)),
        pl.BlockSpec((SSD_CHUNK, SSD_INNER), lambda i: (rev(i), 0)),
        pl.BlockSpec((SSD_CHUNK, SSD_INNER), lambda i: (rev(i), 0)),
    ]
    out_specs = [
        pl.BlockSpec((SSD_CHUNK, SSD_CONV_CH), lambda i: (rev(i), 0)),
        pl.BlockSpec((SSD_CHUNK, HPAD), lambda i: (rev(i), 0)),
        pl.BlockSpec((HPAD, SSD_CHUNK), lambda i: (0, rev(i))),
        pl.BlockSpec((1, HPAD), lambda i: (0, 0)), pl.BlockSpec((HPAD, 1), lambda i: (0, 0)),
        pl.BlockSpec((1, HPAD), lambda i: (0, 0)), pl.BlockSpec((HPAD, 1), lambda i: (0, 0)),
    ]
    out_shape = [SDS((SEQ, SSD_CONV_CH), F32), SDS((SEQ, HPAD), F32), SDS((HPAD, SEQ), F32),
                 SDS((1, HPAD), F32), SDS((HPAD, 1), F32), SDS((1, HPAD), F32), SDS((HPAD, 1), F32)]
    return pl.pallas_call(
        body, name=name, grid=(N_CHUNKS,), in_specs=in_specs, out_specs=out_specs, out_shape=out_shape,
        scratch_shapes=[pltpu.VMEM((SSD_INNER, SSD_STATE), F32)],
        compiler_params=_cparams(("arbitrary",)),
    )(xc, dtr, dtr_t, dtb, dtb_t, alog, alog_t, hs, dy, dxs_extra)


ATTN_SCALE = ATTN_HEAD_DIM ** -0.5


def _attn_scores(q, kp, kc, has_prev):
    qi = lax.broadcasted_iota(jnp.int32, (ATTN_BLOCK, ATTN_BLOCK), 0)
    kj = lax.broadcasted_iota(jnp.int32, (ATTN_BLOCK, ATTN_BLOCK), 1)
    s_c = jnp.where(qi >= kj, _bdot(q, kc, NT) * ATTN_SCALE, -jnp.inf)
    s_p = jnp.where((kj >= qi) & has_prev, _bdot(q, kp, NT) * ATTN_SCALE, -jnp.inf)
    return s_p, s_c


def _for_units(unit):
    for g, d in enumerate(ATTN_DILATIONS):
        nb = SEQ // d // ATTN_BLOCK
        span = d * ATTN_BLOCK

        def per_residue(r, carry, g=g, d=d, nb=nb, span=span):
            def per_block(n, c2):
                start = r + n * span
                prev = jnp.where(n > 0, start - span, start)
                unit(g, pl.ds(start, ATTN_BLOCK, stride=d), pl.ds(prev, ATTN_BLOCK, stride=d), n > 0)
                return c2
            return lax.fori_loop(0, nb, per_block, carry)
        lax.fori_loop(0, d, per_residue, 0)


def _head_specs(n_q_groups):
    blk = (SEQ, ATTN_HEAD_DIM)
    q_specs = [pl.BlockSpec(blk, functools.partial(lambda h, g: (0, g * ATTN_KV_HEADS + h), g=g)) for g in range(n_q_groups)]
    head = pl.BlockSpec(blk, lambda h: (0, h))
    table = pl.BlockSpec(blk, lambda h: (0, 0))
    return q_specs, head, table


def attn_fwd(q, k, v, tabs, name):
    q_specs, head, table = _head_specs(ATTN_N_PAT)

    def body(q0_ref, q1_ref, q2_ref, k_ref, v_ref, c_ref, sa_ref, sb_ref, y_ref, lse_ref, *scr):
        qs, og, ls, ks = scr[0:3], scr[3:6], scr[6:9], scr[9]
        c, sa, sb = c_ref[...], sa_ref[...], sb_ref[...]
        for g, q_ref in enumerate((q0_ref, q1_ref, q2_ref)):
            qs[g][...] = _rope(q_ref[...], c, sa, sb)
        ks[...] = _rope(k_ref[...], c, sa, sb)

        def unit(g, rows, prows, has_prev):
            s_p, s_c = _attn_scores(qs[g][rows, :], ks[prows, :], ks[rows, :], has_prev)
            m = jnp.maximum(jnp.max(s_c, axis=1, keepdims=True), jnp.max(s_p, axis=1, keepdims=True))
            p_c, p_p = jnp.exp(s_c - m), jnp.exp(s_p - m)
            l = jnp.sum(p_c, axis=1, keepdims=True) + jnp.sum(p_p, axis=1, keepdims=True)
            o = _bdot(p_c, v_ref[rows, :]) + _bdot(p_p, v_ref[prows, :])
            og[g][rows, :] = o / l
            ls[g][rows, :] = jnp.broadcast_to(m + jnp.log(l), (ATTN_BLOCK, LANES))

        _for_units(unit)
        l0, l1, l2 = ls[0][...], ls[1][...], ls[2][...]
        m = jnp.maximum(jnp.maximum(l0, l1), l2)
        e0, e1, e2 = jnp.exp(l0 - m), jnp.exp(l1 - m), jnp.exp(l2 - m)
        den = e0 + e1 + e2
        y_ref[...] = ((e0 * og[0][...] + e1 * og[1][...] + e2 * og[2][...]) / den).astype(y_ref.dtype)
        lse_ref[...] = m + jnp.log(den)

    blk = (SEQ, ATTN_HEAD_DIM)
    return pl.pallas_call(
        body, name=name, grid=(ATTN_KV_HEADS,), in_specs=[*q_specs, head, head, table, table, table],
        out_specs=[head, head], out_shape=[SDS((SEQ, ATTN_OUT), BF16), SDS((SEQ, ATTN_OUT), F32)],
        scratch_shapes=[pltpu.VMEM(blk, F32)] * (3 * ATTN_N_PAT + 1),
        compiler_params=_cparams(("parallel",)),
    )(q, q, q, k, v, *tabs)


def attn_bwd(q, k, v, tabs, y, lse, dy, name):
    q_specs, head, table = _head_specs(ATTN_N_PAT)

    def body(q0_ref, q1_ref, q2_ref, k_ref, v_ref, c_ref, sa_ref, sb_ref, y_ref, lse_ref, dy_ref,
             dq0_ref, dq1_ref, dq2_ref, dk_ref, dv_ref, *scr):
        qs, dqs, ks, dks, dd = scr[0:3], scr[3:6], scr[6], scr[7], scr[8]
        c, sa, sb = c_ref[...], sa_ref[...], sb_ref[...]
        for g, q_ref in enumerate((q0_ref, q1_ref, q2_ref)):
            qs[g][...] = _rope(q_ref[...], c, sa, sb)
        ks[...] = _rope(k_ref[...], c, sa, sb)
        dks[...] = jnp.zeros_like(dks)
        dv_ref[...] = jnp.zeros_like(dv_ref)
        dyv = dy_ref[...]
        dd[...] = jnp.broadcast_to(jnp.sum(dyv * y_ref[...].astype(F32), axis=1, keepdims=True), dd.shape)

        def unit(g, rows, prows, has_prev):
            qv = qs[g][rows, :].astype(BF16)
            kc, kp = ks[rows, :].astype(BF16), ks[prows, :].astype(BF16)
            vc, vp = v_ref[rows, :].astype(BF16), v_ref[prows, :].astype(BF16)
            do = dy_ref[rows, :].astype(BF16)
            s_p, s_c = _attn_scores(qv, kp, kc, has_prev)
            lse_u = lse_ref[rows, :][:, 0:1]
            dsum = dd[rows, :][:, 0:1]
            p_c, p_p = jnp.exp(s_c - lse_u), jnp.exp(s_p - lse_u)
            ds_c = (p_c * (_dot(do, vc, NT) - dsum) * ATTN_SCALE).astype(BF16)
            ds_p = (p_p * (_dot(do, vp, NT) - dsum) * ATTN_SCALE).astype(BF16)
            dqs[g][rows, :] = _dot(ds_c, kc) + _dot(ds_p, kp)
            dks[rows, :] += _dot(ds_c, qv, TN)
            dks[prows, :] += _dot(ds_p, qv, TN)
            dv_ref[rows, :] += _bdot(p_c, do, TN)
            dv_ref[prows, :] += _bdot(p_p, do, TN)

        _for_units(unit)
        for g, dq_ref in enumerate((dq0_ref, dq1_ref, dq2_ref)):
            dq_ref[...] = _rope(dqs[g][...], c, -sa, -sb)
        dk_ref[...] = _rope(dks[...], c, -sa, -sb)

    blk = (SEQ, ATTN_HEAD_DIM)
    out = SDS((SEQ, ATTN_OUT), F32)
    return pl.pallas_call(
        body, name=name, grid=(ATTN_KV_HEADS,), in_specs=[*q_specs, head, head, table, table, table, head, head, head],
        out_specs=[head] * 5, out_shape=[out] * 5,
        scratch_shapes=[pltpu.VMEM(blk, F32)] * (2 * ATTN_N_PAT + 3),
        compiler_params=_cparams(("parallel",)),
    )(q, q, q, k, v, *tabs, y, lse, dy)


def layer_fwd(h, getw, small, tabs, li):
    n = f"l{li}_"
    sv = {}
    w = dict(getw(0, h))
    u = rms_fwd(h, small["norm_mix"], n + "rms_mix")
    z = matmul(u, w["w_z"], name=n + "mm_z", tb=True)
    xbc = matmul(u, w["w_xbc"], name=n + "mm_xbc", tb=True)
    dtr = matmul(u, w["w_dt"], name=n + "mm_dt", tb=True)
    q = matmul(u, w["w_q"], name=n + "mm_q", tb=True)
    k = matmul(u, w["w_k"], name=n + "mm_k", tb=True)
    v = matmul(u, w["w_v"], name=n + "mm_v", tb=True)
    gs = matmul(u, w["w_gs"], name=n + "mm_gs", tb=True)
    ga = matmul(u, w["w_ga"], name=n + "mm_ga", tb=True)
    xc = conv_fwd(xbc, w["conv_w"], small["conv_b"], n + "conv")
    dtr_t = dtr.T
    y_ssd, hs = ssd_fwd(xc, dtr, dtr_t, small["dt_bias"], small["dt_bias"].T, small["a_log"], small["a_log"].T, n + "ssd")
    yn = ssd_post_fwd(y_ssd, xc, z, small["d_skip_x"], small["ssd_norm"], n + "ssd_post")
    y_attn, lse = attn_fwd(q, k, v, tabs, n + "attn")
    w.update(getw(1, y_ssd))
    a = matmul(yn, w["w_ssd_branch"], name=n + "mm_a")
    b = matmul(y_attn, w["w_attn_branch"], name=n + "mm_b")
    merged = gate_fwd(a, b, gs, ga, n + "gate")
    h1 = matmul(merged, w["w_out"], name=n + "mm_o", add=h)
    w.update(getw(2, h1))
    u2 = rms_fwd(h1, small["norm_ffn"], n + "rms_ffn")
    gu = matmul(u2, w["w_gate_up"], name=n + "mm_gu", tb=True)
    act = swiglu_fwd(gu, n + "swiglu")
    h2 = matmul(act, w["w_down"], name=n + "mm_down", add=h1)
    sv.update(h=h, u=u, z=z, xbc=xbc, dtr=dtr, dtr_t=dtr_t, gs=gs, ga=ga, xc=xc, y_ssd=y_ssd, hs=hs, yn=yn,
              q=q, k=k, v=v, y_attn=y_attn, lse=lse, a=a, b=b, merged=merged, h1=h1, u2=u2, gu=gu, act=act, w=w)
    return h2, sv


def layer_bwd(dh, sv, small, tabs, li, emit):
    n = f"l{li}_b_"
    w = sv["w"]
    gw, gsm = {}, {}
    dact = matmul(dh, w["w_down"], name=n + "mm_dact", tb=True, out_dtype=BF16)
    gw["w_down"] = matmul(sv["act"], dh, name=n + "mm_dwdown", ta=True, out_dtype=BF16)
    dgu = swiglu_bwd(sv["gu"], dact, n + "swiglu")
    gw["w_gate_up"] = matmul(dgu, sv["u2"], name=n + "mm_dwgu", ta=True, out_dtype=BF16)
    tok = emit(2, gw)
    du2 = matmul(dgu, w["w_gate_up"], name=n + "mm_du2")
    dh1, gsm["norm_ffn"] = rms_bwd(sv["h1"], du2, dh, small["norm_ffn"] + tok, n + "rms_ffn")
    dmerged = matmul(dh1, w["w_out"], name=n + "mm_dmerged", tb=True)
    gw["w_out"] = matmul(sv["merged"], dh1, name=n + "mm_dwo", ta=True, out_dtype=BF16)
    da, db, dgs, dga = gate_bwd(sv["a"], sv["b"], sv["gs"], sv["ga"], dmerged, n + "gate")
    gw["w_ssd_branch"] = matmul(sv["yn"], da, name=n + "mm_dwa", ta=True, out_dtype=BF16)
    gw["w_attn_branch"] = matmul(sv["y_attn"], db, name=n + "mm_dwb", ta=True, out_dtype=BF16)
    tok = emit(1, gw)
    dyn = matmul(da, w["w_ssd_branch"], name=n + "mm_dyn", tb=True)
    dyattn = matmul(db, w["w_attn_branch"], name=n + "mm_dyattn", tb=True)
    dy_ssd, dxs_extra, dz, gsm["d_skip_x"], gsm["ssd_norm"] = ssd_post_bwd(
        sv["y_ssd"], sv["xc"], sv["z"], small["d_skip_x"] + tok, small["ssd_norm"], dyn, n + "ssd_post")
    dxc, ddtr, ddtr_t, ddtb, ddtb_t, dal, dal_t = ssd_bwd(
        sv["xc"], sv["dtr"], sv["dtr_t"], small["dt_bias"], small["dt_bias"].T, small["a_log"], small["a_log"].T,
        sv["hs"], dy_ssd, dxs_extra, n + "ssd")
    ddtr = (ddtr + ddtr_t.T).astype(BF16)
    gsm["dt_bias"] = ddtb + ddtb_t.T
    gsm["a_log"] = dal + dal_t.T
    dxbc, gw["conv_w"], gsm["conv_b"] = conv_bwd(sv["xbc"], w["conv_w"], small["conv_b"], dxc, n + "conv")
    dq0, dq1, dq2, dk, dv = attn_bwd(sv["q"], sv["k"], sv["v"], tabs, sv["y_attn"], sv["lse"], dyattn, n + "attn")
    u = sv["u"]
    segs = [("w_z", dz), ("w_xbc", dxbc), ("w_dt", ddtr), ("w_q0", dq0), ("w_q1", dq1), ("w_q2", dq2),
            ("w_k", dk), ("w_v", dv), ("w_gs", dgs), ("w_ga", dga)]
    gin = [matmul(dseg, u, name=n + "mm_d" + key, ta=True, out_dtype=BF16) for key, dseg in segs]
    gin[2] = gin[2][:SSD_HEADS]
    gw["w_in"] = jnp.concatenate(gin, axis=0)
    tok = emit(0, gw)
    du = None
    for key, dseg in segs:
        du = matmul(dseg, w[key], name=n + "mm_du_" + key, add=du)
    dh0, gsm["norm_mix"] = rms_bwd(sv["h"], du, dh1, small["norm_mix"] + tok, n + "rms_mix")
    return dh0, gsm


def _my_place():
    return lax.axis_index("x"), lax.axis_index("y"), lax.axis_index("c")


def _flip(place, k):
    x, y, c = place
    return (1 - x if k & 4 else x, 1 - y if k & 2 else y, 1 - c if k & 1 else c)


def _index(place):
    return 4 * place[0] + 2 * place[1] + place[2]


ANY = pl.BlockSpec(memory_space=pl.ANY)
CHIP_FLIPS = (4, 2, 6)


def all_gather(xs, name):
    na = len(xs)

    def body(*refs):
        x_refs, o_refs = refs[:na], refs[na:2 * na]
        send_sems, recv_sems, local_sems = refs[2 * na:]
        me = _my_place()
        sibling = _flip(me, 1)
        chips = [_flip(me, f) for f in CHIP_FLIPS]

        def copy(a, kk, block, to, src=None):
            dst = o_refs[a].at[_index(block)]
            return pltpu.make_async_remote_copy(
                src_ref=dst if src is None else src, dst_ref=dst, send_sem=send_sems.at[a, kk],
                recv_sem=recv_sems.at[a, kk], device_id=to, device_id_type=MESH)

        mine = [pltpu.make_async_copy(x_refs[a], o_refs[a].at[_index(me)], local_sems.at[a]) for a in range(na)]
        for cp in mine:
            cp.start()
        first = []
        for j, chip in enumerate(chips):
            first += [copy(a, 1 + j, me, chip, src=x_refs[a]) for a in range(na)]
        first += [copy(a, 0, me, sibling, src=x_refs[a]) for a in range(na)]
        for cp in first:
            cp.start()
        passed = []
        for j, chip in enumerate(chips):
            for a in range(na):
                copy(a, 1 + j, chip, me).wait_recv()
                cp = copy(a, 4 + j, chip, sibling)
                cp.start()
                passed.append(cp)
        for a in range(na):
            copy(a, 0, sibling, me).wait_recv()
        for j, chip in enumerate(chips):
            for a in range(na):
                copy(a, 4 + j, _flip(chip, 1), me).wait_recv()
        for cp in first + passed:
            cp.wait_send()
        for cp in mine:
            cp.wait()

    return pl.pallas_call(
        body, name=name, in_specs=[ANY] * na, out_specs=[ANY] * na,
        out_shape=[SDS((N_DEV,) + t.shape, t.dtype) for t in xs],
        scratch_shapes=[pltpu.SemaphoreType.DMA((na, N_DEV - 1)), pltpu.SemaphoreType.DMA((na, N_DEV - 1)),
                        pltpu.SemaphoreType.DMA((na,))],
    )(*xs)


HBM = pl.BlockSpec(memory_space=pltpu.HBM)
SEM = pl.BlockSpec(memory_space=pltpu.SEMAPHORE)
EFFECT = pltpu.SideEffectType.DATAFLOW_SIDE_EFFECTING
N_PEERS = N_DEV - 1


def _split_copy(src_ref, land_ref, send_sem, recv_sem, me, kk, scatter, landed_from_peer):
    peer = _flip(me, kk)
    src = src_ref.at[_index(peer)] if scatter else src_ref
    dst = land_ref.at[_index(peer if landed_from_peer else me)]
    return pltpu.make_async_remote_copy(src_ref=src, dst_ref=dst, send_sem=send_sem, recv_sem=recv_sem,
                                        device_id=peer, device_id_type=MESH)


def exchange_start(srcs, lands, group_sizes, scatter, name):
    na, ng = len(srcs), len(group_sizes)

    def body(*refs):
        s_refs, l_refs = refs[:na], refs[na:2 * na]
        sems = refs[2 * na:2 * na + 2 * ng]
        token = refs[-1]
        me = _my_place()
        a = 0
        for gi, gsz in enumerate(group_sizes):
            for j in range(gsz):
                for kk in range(1, N_DEV):
                    slot = j * N_PEERS + kk - 1
                    _split_copy(s_refs[a], l_refs[a], sems[2 * gi].at[slot], sems[2 * gi + 1].at[slot],
                                me, kk, scatter, False).start()
                a += 1
        token[...] = jnp.zeros_like(token)

    sem_shapes = []
    for gsz in group_sizes:
        sem_shapes += [pltpu.SemaphoreType.DMA((gsz * N_PEERS,))] * 2
    ins = [pltpu.with_memory_space_constraint(t, pltpu.HBM) for t in (*srcs, *lands)]
    res = pl.pallas_call(
        body, name=name, in_specs=[HBM] * (2 * na),
        out_specs=[SEM] * (2 * ng) + [HBM] * (2 * na) + [pl.BlockSpec(memory_space=pltpu.VMEM)],
        out_shape=sem_shapes + [pltpu.HBM(t.shape, t.dtype) for t in ins] + [SDS((8, LANES), F32)],
        input_output_aliases={i: 2 * ng + i for i in range(2 * na)},
        compiler_params=pltpu.CompilerParams(has_side_effects=EFFECT),
    )(*ins)
    sems = [(res[2 * gi], res[2 * gi + 1]) for gi in range(ng)]
    thru = res[2 * ng:2 * ng + 2 * na]
    return sems, thru[:na], thru[na:], res[-1]


def exchange_wait(srcs, lands, sems, after, scatter, name):
    n = len(srcs)

    def body(*refs):
        s_refs, l_refs = refs[:n], refs[n:2 * n]
        send_sems, recv_sems = refs[2 * n], refs[2 * n + 1]
        me = _my_place()
        for j in range(n):
            for kk in range(1, N_DEV):
                slot = j * N_PEERS + kk - 1
                cp = _split_copy(s_refs[j], l_refs[j], send_sems.at[slot], recv_sems.at[slot], me, kk, scatter, True)
                cp.wait_send()
                cp.wait_recv()

    res = pl.pallas_call(
        body, name=name, in_specs=[HBM] * (2 * n) + [SEM, SEM, ANY], out_specs=[HBM] * (2 * n),
        out_shape=[pltpu.HBM(t.shape, t.dtype) for t in (*srcs, *lands)],
        input_output_aliases={i: i for i in range(2 * n)},
        compiler_params=pltpu.CompilerParams(has_side_effects=EFFECT),
    )(*srcs, *lands, sems[0], sems[1], after)
    return res[n:]


def landing_zone(block, me_index):
    land = lax.empty((N_DEV,) + block.shape, block.dtype)
    return lax.dynamic_update_slice(land, block[None], (me_index,) + (0,) * block.ndim)


def sum_parts(parts, name):
    _, r, c = parts.shape
    tc = _pick(c, (256, 128))

    def body(p_ref, o_ref):
        acc = p_ref[0].astype(F32)
        for i in range(1, N_DEV):
            acc = acc + p_ref[i].astype(F32)
        o_ref[...] = acc

    return pl.pallas_call(
        body, name=name, grid=(c // tc,), in_specs=[pl.BlockSpec((N_DEV, r, tc), lambda i: (0, 0, i))],
        out_specs=pl.BlockSpec((r, tc), lambda i: (0, i)), out_shape=SDS((r, c), F32),
        compiler_params=_cparams(("parallel",)),
    )(parts)


def adamw(w, g, m, v, name):
    shape = w.shape
    cols = shape[-1]
    rows = w.size // cols
    tr = _pick(rows, (256, 128, 64, 32, 16, 8))
    c1 = 1.0 / (1.0 - ADAM_B1 ** ADAM_STEP)
    c2 = 1.0 / (1.0 - ADAM_B2 ** ADAM_STEP)

    def body(w_ref, g_ref, m_ref, v_ref, d_ref, nm_ref, nv_ref):
        gg = g_ref[...]
        nm = ADAM_B1 * m_ref[...] + (1.0 - ADAM_B1) * gg
        nv = ADAM_B2 * v_ref[...] + (1.0 - ADAM_B2) * (gg * gg)
        d_ref[...] = -ADAM_LR * ((nm * c1) / (jnp.sqrt(nv * c2) + ADAM_EPS) + ADAM_WD * w_ref[...])
        nm_ref[...] = nm
        nv_ref[...] = nv

    spec = pl.BlockSpec((tr, cols), lambda i: (i, 0))
    outs = pl.pallas_call(
        body, name=name, grid=(rows // tr,), in_specs=[spec] * 4, out_specs=[spec] * 3,
        out_shape=[SDS((rows, cols), F32)] * 3, compiler_params=_cparams(("parallel",)),
    )(*[t.reshape(rows, cols) for t in (w, g, m, v)])
    return [o.reshape(shape) for o in outs]


BIG = ("w_in", "conv_w", "w_ssd_branch", "w_attn_branch", "w_out", "w_gate_up", "w_down")
TRANSPOSED = ("w_in", "w_gate_up")
SMALL = ("norm_mix", "conv_b", "dt_bias", "a_log", "d_skip", "ssd_norm", "norm_ffn")
SMALL_SIZE = {"norm_mix": 1024, "conv_b": 3072, "dt_bias": 32, "a_log": 32, "d_skip": 32, "ssd_norm": 2048, "norm_ffn": 1024}
FLAT_W = 512
SMALL_TOTAL = DEPTH * sum(SMALL_SIZE.values()) + D_MODEL + LANES
SMALL_ROWS = 32
assert SMALL_ROWS * FLAT_W >= SMALL_TOTAL


GROUPS = (("w_in", "conv_w"), ("w_ssd_branch", "w_attn_branch", "w_out"), ("w_gate_up", "w_down"))


def to_wire(k, shard):
    if k in TRANSPOSED:
        return shard.T.astype(BF16)
    return shard if k == "conv_w" else shard.astype(BF16)


def full_weights(k, g):
    if k == "conv_w":
        return {k: g.transpose(1, 0, 2).reshape(SSD_CONV, SSD_CONV_CH)}
    full = g.reshape(-1, g.shape[-1])
    if k != "w_in":
        return {k: full}
    w, off = {}, 0
    for nm, r in IN_ROWS:
        w[nm] = full[off:off + r]
        off += r
    w["w_q"] = full[sum(r for _, r in IN_ROWS[:3]):sum(r for _, r in IN_ROWS[:6])]
    w["w_dt"] = jnp.pad(w["w_dt"], ((0, HPAD - SSD_HEADS), (0, 0)))
    return w


def grads_to_wire(k, g):
    if k == "conv_w":
        return g.reshape(SSD_CONV, N_DEV, SSD_CONV_CH // N_DEV).transpose(1, 0, 2)
    return g.reshape(N_DEV, g.shape[0] // N_DEV, g.shape[1])


def _pad_heads(t):
    return jnp.pad(t.reshape(1, SSD_HEADS), ((0, 0), (0, HPAD - SSD_HEADS)))


def local_step(x, target, getw, emit, smalls, norm_final):
    tabs = rope_tables()
    sms = []
    for li in range(DEPTH):
        s = smalls[li]
        sms.append({
            "norm_mix": s["norm_mix"].reshape(1, -1), "conv_b": s["conv_b"].reshape(1, -1),
            "dt_bias": _pad_heads(s["dt_bias"]), "a_log": _pad_heads(s["a_log"]),
            "d_skip_x": jnp.repeat(s["d_skip"], SSD_HEAD_DIM).reshape(1, -1),
            "ssd_norm": s["ssd_norm"].reshape(1, -1), "norm_ffn": s["norm_ffn"].reshape(1, -1)})
    h = x
    saved = []
    for li in range(DEPTH):
        h, sv = layer_fwd(h, functools.partial(getw, li), sms[li], tabs, li)
        saved.append(sv)
    dh, g_final, loss = loss_head(h, target, norm_final.reshape(1, -1), "loss_head")
    gsms = [None] * DEPTH
    for li in reversed(range(DEPTH)):
        dh, gsm = layer_bwd(dh, saved[li], sms[li], tabs, li, functools.partial(emit, li))
        gsms[li] = {
            "norm_mix": gsm["norm_mix"].reshape(-1), "conv_b": gsm["conv_b"].reshape(-1),
            "dt_bias": gsm["dt_bias"][0, :SSD_HEADS], "a_log": gsm["a_log"][0, :SSD_HEADS],
            "d_skip": gsm["d_skip_x"].reshape(SSD_HEADS, SSD_HEAD_DIM).sum(axis=1),
            "ssd_norm": gsm["ssd_norm"].reshape(-1), "norm_ffn": gsm["norm_ffn"].reshape(-1)}
    return loss, dh, gsms, g_final.reshape(-1)


def kernel(x, norm_mix, w_in, conv_w, conv_b, dt_bias, a_log, d_skip, ssd_norm, w_ssd_branch, w_attn_branch, w_out, norm_ffn, w_gate_up, w_down, norm_final, loss_target, m_norm_mix, m_w_in, m_conv_w, m_conv_b, m_dt_bias, m_a_log, m_d_skip, m_ssd_norm, m_w_ssd_branch, m_w_attn_branch, m_w_out, m_norm_ffn, m_w_gate_up, m_w_down, m_norm_final, v_norm_mix, v_w_in, v_conv_w, v_conv_b, v_dt_bias, v_a_log, v_d_skip, v_ssd_norm, v_w_ssd_branch, v_w_attn_branch, v_w_out, v_norm_ffn, v_w_gate_up, v_w_down, v_norm_final):
    wv = dict(norm_mix=norm_mix, w_in=w_in, conv_w=conv_w, conv_b=conv_b, dt_bias=dt_bias, a_log=a_log, d_skip=d_skip,
              ssd_norm=ssd_norm, w_ssd_branch=w_ssd_branch, w_attn_branch=w_attn_branch, w_out=w_out, norm_ffn=norm_ffn,
              w_gate_up=w_gate_up, w_down=w_down, norm_final=norm_final)
    mv = dict(norm_mix=m_norm_mix, w_in=m_w_in, conv_w=m_conv_w, conv_b=m_conv_b, dt_bias=m_dt_bias, a_log=m_a_log,
              d_skip=m_d_skip, ssd_norm=m_ssd_norm, w_ssd_branch=m_w_ssd_branch, w_attn_branch=m_w_attn_branch,
              w_out=m_w_out, norm_ffn=m_norm_ffn, w_gate_up=m_w_gate_up, w_down=m_w_down, norm_final=m_norm_final)
    vv = dict(norm_mix=v_norm_mix, w_in=v_w_in, conv_w=v_conv_w, conv_b=v_conv_b, dt_bias=v_dt_bias, a_log=v_a_log,
              d_skip=v_d_skip, ssd_norm=v_ssd_norm, w_ssd_branch=v_w_ssd_branch, w_attn_branch=v_w_attn_branch,
              w_out=v_w_out, norm_ffn=v_norm_ffn, w_gate_up=v_w_gate_up, w_down=v_w_down, norm_final=v_norm_final)
    order = ("norm_mix", "w_in", "conv_w", "conv_b", "dt_bias", "a_log", "d_skip", "ssd_norm", "w_ssd_branch",
             "w_attn_branch", "w_out", "norm_ffn", "w_gate_up", "w_down", "norm_final")

    me_index = _index(_my_place())
    smalls = [{k: wv[k][li] for k in SMALL} for li in range(DEPTH)]
    n_groups = len(GROUPS)

    srcs = [to_wire(k, wv[k][li]) for li in range(DEPTH) for grp in GROUPS for k in grp]
    sizes = [len(grp) for _ in range(DEPTH) for grp in GROUPS]
    w_sems, w_srcs, w_lands, token = exchange_start(srcs, [landing_zone(s, me_index) for s in srcs], sizes, False, "gather_start")
    smalls[0]["norm_mix"] = smalls[0]["norm_mix"] + token[0, 0]

    def getw(li, gi, after):
        first = sum(sizes[:li * n_groups + gi])
        sl = slice(first, first + len(GROUPS[gi]))
        lands = exchange_wait(w_srcs[sl], w_lands[sl], w_sems[li * n_groups + gi], after, False, f"gather_wait_{li}_{gi}")
        w = {}
        for k, land in zip(GROUPS[gi], lands):
            w.update(full_weights(k, land))
        return w

    pending = []

    def emit(li, gi, gw):
        parts = [grads_to_wire(k, gw[k]) for k in GROUPS[gi]]
        lands = [landing_zone(lax.dynamic_index_in_dim(p, me_index, 0, keepdims=False), me_index) for p in parts]
        sems, p_thru, l_thru, tok = exchange_start(parts, lands, [len(parts)], True, f"grads_start_{li}_{gi}")
        pending.append((li, gi, sems[0], p_thru, l_thru))
        return tok[0, 0]

    loss_p, dx, gsms, g_final = local_step(x[0], loss_target[0], getw, emit, smalls, norm_final)

    grads, deltas, new_m, new_v = {}, {}, {}, {}

    def update(k):
        w2, g2, m2, v2 = wv[k], grads[k], mv[k], vv[k]
        if w2.ndim == 1:
            w2, g2, m2, v2 = (t.reshape(1, -1) for t in (w2, g2, m2, v2))
        d, nm, nv = adamw(w2, g2, m2, v2, "adamw_" + k)
        deltas[k], new_m[k], new_v[k] = (t.reshape(wv[k].shape) for t in (d, nm, nv))
        return nv

    shard_g = {k: [None] * DEPTH for k in BIG}
    after = dx
    for li, gi, sems, p_thru, l_thru in pending:
        recv = exchange_wait(p_thru, l_thru, sems, after, True, f"grads_wait_{li}_{gi}")
        for k, r in zip(GROUPS[gi], recv):
            if k == "conv_w":
                r = r.reshape(N_DEV, 1, -1)
            after = sum_parts(r, f"sum_{k}_{li}")
            shard_g[k][li] = after.T if k in TRANSPOSED else after.reshape(wv[k].shape[1:])
        if li == 0:
            for k in GROUPS[gi]:
                grads[k] = jnp.stack(shard_g[k])
                after = update(k)

    flat = [gsms[li][k] for li in range(DEPTH) for k in SMALL] + [g_final, loss_p.reshape(-1)]
    flat.append(jnp.zeros((SMALL_ROWS * FLAT_W - SMALL_TOTAL,), F32))
    small_all = all_gather([jnp.concatenate(flat).reshape(SMALL_ROWS, FLAT_W)], "gather_small")[0]
    small_sum = sum_parts(small_all, "sum_small").reshape(-1)
    off = 0
    per_layer = {k: [] for k in SMALL}
    for li in range(DEPTH):
        for k in SMALL:
            per_layer[k].append(small_sum[off:off + SMALL_SIZE[k]])
            off += SMALL_SIZE[k]
    for k in SMALL:
        grads[k] = jnp.stack(per_layer[k])
    grads["norm_final"] = small_sum[off:off + D_MODEL]
    loss = small_sum[off + D_MODEL]
    for k in (*SMALL, "norm_final"):
        update(k)

    return (loss, dx.reshape(x.shape), *[grads[k] for k in order], *[deltas[k] for k in order],
            *[new_m[k] for k in order], *[new_v[k] for k in order])
```

```python
import functools

import jax
import jax.numpy as jnp
from jax import lax
from jax.experimental import pallas as pl
from jax.experimental.pallas import tpu as pltpu

F32, BF16 = jnp.float32, jnp.bfloat16
SDS = jax.ShapeDtypeStruct
MESH = pl.DeviceIdType.MESH

D_MODEL = 1024
SEQ = 2048
DEPTH = 2
RMS_EPS = 1e-5
SSD_INNER = 2048
SSD_HEAD_DIM = 64
SSD_HEADS = 32
SSD_STATE = 128
SSD_GROUPS = 4
SSD_CONV = 4
SSD_CHUNK = 128
SSD_CONV_CH = 3072
ATTN_HEAD_DIM = 128
ATTN_KV_HEADS = 8
ATTN_DILATIONS = (1, 4, 16)
ATTN_N_PAT = 3
ATTN_BLOCK = 128
ATTN_OUT = 1024
ROPE_THETA = 500000.0
ROPE_DIM = 32
FFN_HIDDEN = 2816
ADAM_LR, ADAM_B1, ADAM_B2, ADAM_EPS, ADAM_WD, ADAM_STEP = 0.001, 0.9, 0.999, 1e-08, 0.01, 10

N_DEV = 8
LANES = 128
VMEM_LIMIT = 56 * 1024 * 1024
HPAD = 128
HIGHEST = lax.Precision.HIGHEST

IN_ROWS = (("w_z", 2048), ("w_xbc", 3072), ("w_dt", 32), ("w_q0", 1024), ("w_q1", 1024), ("w_q2", 1024),
           ("w_k", 1024), ("w_v", 1024), ("w_gs", 1024), ("w_ga", 1024))
N_IN = sum(r for _, r in IN_ROWS)


def _cparams(sem):
    return pltpu.CompilerParams(dimension_semantics=sem, vmem_limit_bytes=VMEM_LIMIT)


def _sigmoid(x):
    return 0.5 * jnp.tanh(0.5 * x) + 0.5


def _silu(x):
    return x * _sigmoid(x)


def _softplus(x):
    return jnp.maximum(x, 0.0) + jnp.log(1.0 + jnp.exp(-jnp.abs(x)))


def _dot(a, b, dims=(((1,), (0,)), ((), ())), precision=None):
    return lax.dot_general(a, b, dims, precision=precision, preferred_element_type=F32)


NT = (((1,), (1,)), ((), ()))
TN = (((0,), (0,)), ((), ()))


def _bdot(a, b, dims=(((1,), (0,)), ((), ()))):
    return _dot(a.astype(BF16), b.astype(BF16), dims)


def _pick(dim, cands):
    for c in cands:
        if dim % c == 0:
            return c
    return dim


def matmul(a, b, *, name, ta=False, tb=False, out_dtype=F32, add=None):
    m, k = (a.shape[1], a.shape[0]) if ta else a.shape
    n = b.shape[0] if tb else b.shape[1]
    tm = _pick(m, (1024, 1408, 512, 256, 128))
    tn = _pick(n, (512, 256, 128))
    tk = _pick(k, (1024, 1408, 512, 256, 128))
    nk = k // tk
    a_spec = pl.BlockSpec((tk, tm), lambda i, j, kk: (kk, i)) if ta else pl.BlockSpec((tm, tk), lambda i, j, kk: (i, kk))
    b_spec = pl.BlockSpec((tn, tk), lambda i, j, kk: (j, kk)) if tb else pl.BlockSpec((tk, tn), lambda i, j, kk: (kk, j))
    dims = (((0 if ta else 1,), (1 if tb else 0,)), ((), ()))
    has_add = add is not None

    def body(*refs):
        if has_add:
            a_ref, b_ref, add_ref, o_ref, acc = refs
        else:
            a_ref, b_ref, o_ref, acc = refs
        kk = pl.program_id(2)

        @pl.when(kk == 0)
        def _():
            acc[...] = jnp.zeros_like(acc)

        acc[...] += _dot(a_ref[...].astype(BF16), b_ref[...].astype(BF16), dims)

        @pl.when(kk == nk - 1)
        def _():
            r = acc[...]
            if has_add:
                r = r + add_ref[...].astype(F32)
            o_ref[...] = r.astype(o_ref.dtype)

    in_specs = [a_spec, b_spec]
    args = [a, b]
    if has_add:
        in_specs.append(pl.BlockSpec((tm, tn), lambda i, j, kk: (i, j)))
        args.append(add)
    return pl.pallas_call(
        body, name=name, grid=(m // tm, n // tn, nk),
        in_specs=in_specs, out_specs=pl.BlockSpec((tm, tn), lambda i, j, kk: (i, j)),
        out_shape=SDS((m, n), out_dtype), scratch_shapes=[pltpu.VMEM((tm, tn), F32)],
        compiler_params=_cparams(("parallel", "parallel", "arbitrary")),
    )(*args)


def rowcall(name, fn, rows, params, row_outs, red_outs=(), tr=256):
    s = rows[0].shape[0]
    n_in = len(rows) + len(params)
    n_row = len(row_outs)

    def body(*refs):
        outs = fn(*[r[...] for r in refs[:n_in]])
        if not isinstance(outs, (tuple, list)):
            outs = (outs,)
        orefs = refs[n_in:]
        for r, o in zip(orefs[:n_row], outs[:n_row]):
            r[...] = o.astype(r.dtype)
        if red_outs:
            @pl.when(pl.program_id(0) == 0)
            def _():
                for r in orefs[n_row:]:
                    r[...] = jnp.zeros_like(r)
            for r, o in zip(orefs[n_row:], outs[n_row:]):
                r[...] += o.astype(F32)

    in_specs = [pl.BlockSpec((tr, a.shape[1]), lambda i: (i, 0)) for a in rows]
    in_specs += [pl.BlockSpec(p.shape, lambda i: (0, 0)) for p in params]
    out_specs = [pl.BlockSpec((tr, c), lambda i: (i, 0)) for c, _ in row_outs]
    out_specs += [pl.BlockSpec(shp, lambda i: (0, 0)) for shp in red_outs]
    out_shape = [SDS((s, c), dt) for c, dt in row_outs] + [SDS(shp, F32) for shp in red_outs]
    res = pl.pallas_call(
        body, name=name, grid=(s // tr,), in_specs=in_specs, out_specs=out_specs, out_shape=out_shape,
        compiler_params=_cparams(("arbitrary",) if red_outs else ("parallel",)),
    )(*rows, *params)
    return res


def _rms(x, w):
    return x * lax.rsqrt(jnp.mean(x * x, axis=-1, keepdims=True) + RMS_EPS) * w


def rms_fwd(h, w, name):
    return rowcall(name, _rms, [h], [w], [(D_MODEL, BF16)])[0]


def rms_bwd(h, du, dres, w, name):
    def fn(hb, dub, dresb, wb):
        _, vjp = jax.vjp(_rms, hb, wb)
        dh, dw = vjp(dub)
        return dh + dresb, dw
    return rowcall(name, fn, [h, du, dres], [w], [(D_MODEL, F32)], [(1, D_MODEL)])


def loss_head(h, target, w, name):
    def fn(hb, tb, wb):
        def f(hh, ww):
            err = _rms(hh, ww) - tb
            return 0.5 * jnp.sum(jnp.mean(err * err, axis=-1, keepdims=True), axis=0, keepdims=True)
        val, vjp = jax.vjp(f, hb, wb)
        dh, dw = vjp(jnp.ones((1, 1), F32))
        return dh, dw, jnp.broadcast_to(val, (1, LANES))
    return rowcall(name, fn, [h, target], [w], [(D_MODEL, F32)], [(1, D_MODEL), (1, LANES)])


def _gate(a, b, gs, ga):
    return _sigmoid(gs) * a + _sigmoid(ga) * b


def gate_fwd(a, b, gs, ga, name):
    return rowcall(name, _gate, [a, b, gs, ga], [], [(D_MODEL, BF16)])[0]


def gate_bwd(a, b, gs, ga, dm, name):
    def fn(ab, bb, gsb, gab, dmb):
        _, vjp = jax.vjp(_gate, ab, bb, gsb, gab)
        return vjp(dmb)
    return rowcall(name, fn, [a, b, gs, ga, dm], [], [(D_MODEL, BF16)] * 4)


def _swiglu(gu):
    return _silu(gu[:, :FFN_HIDDEN]) * gu[:, FFN_HIDDEN:]


def swiglu_fwd(gu, name):
    return rowcall(name, _swiglu, [gu], [], [(FFN_HIDDEN, BF16)])[0]


def swiglu_bwd(gu, dact, name):
    def fn(gub, db):
        _, vjp = jax.vjp(_swiglu, gub)
        return vjp(db.astype(F32))[0]
    return rowcall(name, fn, [gu, dact], [], [(2 * FFN_HIDDEN, BF16)])[0]


def _ssd_post(y, xs, z, dskip, normw):
    y = (y + dskip * xs) * _silu(z)
    gw = SSD_INNER // SSD_GROUPS
    parts = []
    for g in range(SSD_GROUPS):
        yg = y[:, g * gw:(g + 1) * gw]
        parts.append(yg * lax.rsqrt(jnp.mean(yg * yg, axis=-1, keepdims=True) + RMS_EPS))
    return jnp.concatenate(parts, axis=-1) * normw


def ssd_post_fwd(y, xc, z, dskip, normw, name):
    def fn(yb, xcb, zb, db, nb):
        return _ssd_post(yb, xcb[:, :SSD_INNER], zb, db, nb)
    return rowcall(name, fn, [y, xc, z], [dskip, normw], [(SSD_INNER, BF16)])[0]


def ssd_post_bwd(y, xc, z, dskip, normw, dyn, name):
    def fn(yb, xcb, zb, dynb, db, nb):
        _, vjp = jax.vjp(_ssd_post, yb, xcb[:, :SSD_INNER], zb, db, nb)
        return vjp(dynb)
    return rowcall(name, fn, [y, xc, z, dyn], [dskip, normw],
                   [(SSD_INNER, F32), (SSD_INNER, F32), (SSD_INNER, BF16)], [(1, SSD_INNER), (1, SSD_INNER)])


def _rope(t, cosf, sina, sinb):
    return t * cosf + pltpu.roll(t, LANES - ROPE_DIM // 2, 1) * sina + pltpu.roll(t, ROPE_DIM // 2, 1) * sinb


def rope_tables():
    half = ROPE_DIM // 2
    inv = ROPE_THETA ** (-jnp.arange(0, ROPE_DIM, 2, dtype=F32) / ROPE_DIM)
    ang = jnp.arange(SEQ, dtype=F32)[:, None] * inv[None, :]
    cos, sin = jnp.cos(ang), jnp.sin(ang)
    zeros = jnp.zeros((SEQ, LANES - ROPE_DIM), F32)
    z16 = jnp.zeros((SEQ, half), F32)
    cosf = jnp.concatenate([cos, cos, jnp.ones((SEQ, LANES - ROPE_DIM), F32)], axis=1)
    sina = jnp.concatenate([-sin, z16, zeros], axis=1)
    sinb = jnp.concatenate([z16, sin, zeros], axis=1)
    return cosf, sina, sinb


CONV_TC = 256


def _conv_pre(x, w, b, row):
    acc = x * w[SSD_CONV - 1:SSD_CONV, :] + b
    shifted = [x]
    for j in range(1, SSD_CONV):
        xs = jnp.where(row >= j, pltpu.roll(x, j, 0), 0.0)
        shifted.append(xs)
        acc = acc + xs * w[SSD_CONV - 1 - j:SSD_CONV - j, :]
    return acc, shifted


def conv_fwd(xbc, w, b, name):
    def body(x_ref, w_ref, b_ref, o_ref):
        row = lax.broadcasted_iota(jnp.int32, (SEQ, CONV_TC), 0)
        pre, _ = _conv_pre(x_ref[...], w_ref[...], b_ref[...], row)
        o_ref[...] = _silu(pre)
    return pl.pallas_call(
        body, name=name, grid=(SSD_CONV_CH // CONV_TC,),
        in_specs=[pl.BlockSpec((SEQ, CONV_TC), lambda i: (0, i)), pl.BlockSpec((SSD_CONV, CONV_TC), lambda i: (0, i)),
                  pl.BlockSpec((1, CONV_TC), lambda i: (0, i))],
        out_specs=pl.BlockSpec((SEQ, CONV_TC), lambda i: (0, i)),
        out_shape=SDS((SEQ, SSD_CONV_CH), F32), compiler_params=_cparams(("parallel",)),
    )(xbc, w, b)


def conv_bwd(xbc, w, b, dxc, name):
    def body(x_ref, w_ref, b_ref, dy_ref, dx_ref, dw_ref, db_ref):
        row = lax.broadcasted_iota(jnp.int32, (SEQ, CONV_TC), 0)
        wv = w_ref[...]
        pre, shifted = _conv_pre(x_ref[...], wv, b_ref[...], row)
        sg = _sigmoid(pre)
        ds = dy_ref[...] * (sg * (1.0 + pre * (1.0 - sg)))
        dx = ds * wv[SSD_CONV - 1:SSD_CONV, :]
        for j in range(1, SSD_CONV):
            dsj = jnp.where(row < SEQ - j, pltpu.roll(ds, SEQ - j, 0), 0.0)
            dx = dx + dsj * wv[SSD_CONV - 1 - j:SSD_CONV - j, :]
        dx_ref[...] = dx.astype(dx_ref.dtype)
        for j in range(SSD_CONV):
            dw_ref[SSD_CONV - 1 - j:SSD_CONV - j, :] = jnp.sum(ds * shifted[j], axis=0, keepdims=True)
        db_ref[...] = jnp.sum(ds, axis=0, keepdims=True)
    return pl.pallas_call(
        body, name=name, grid=(SSD_CONV_CH // CONV_TC,),
        in_specs=[pl.BlockSpec((SEQ, CONV_TC), lambda i: (0, i)), pl.BlockSpec((SSD_CONV, CONV_TC), lambda i: (0, i)),
                  pl.BlockSpec((1, CONV_TC), lambda i: (0, i)), pl.BlockSpec((SEQ, CONV_TC), lambda i: (0, i))],
        out_specs=[pl.BlockSpec((SEQ, CONV_TC), lambda i: (0, i)), pl.BlockSpec((SSD_CONV, CONV_TC), lambda i: (0, i)),
                   pl.BlockSpec((1, CONV_TC), lambda i: (0, i))],
        out_shape=[SDS((SEQ, SSD_CONV_CH), BF16), SDS((SSD_CONV, SSD_CONV_CH), F32), SDS((1, SSD_CONV_CH), F32)],
        compiler_params=_cparams(("parallel",)),
    )(xbc, w, b, dxc)


N_CHUNKS = SEQ // SSD_CHUNK
N_PAIRS = SSD_HEADS // 2
PAIRS_PER_GROUP = N_PAIRS // SSD_GROUPS
B_OFF = SSD_INNER
C_OFF = SSD_INNER + SSD_GROUPS * SSD_STATE


def _ssd_prefix(dtr, dtr_t, dtb, dtb_t, alog, alog_t):
    ln = SSD_CHUNK
    dt = _softplus(dtr + dtb)
    dt_t = _softplus(dtr_t + dtb_t)
    dta = dt * (-jnp.exp(alog))
    dta_t = dt_t * (-jnp.exp(alog_t))
    r = lax.broadcasted_iota(jnp.int32, (ln, ln), 0)
    c = lax.broadcasted_iota(jnp.int32, (ln, ln), 1)
    a_cum = _dot((r >= c).astype(F32), dta, precision=HIGHEST)
    a_cum_t = _dot(dta_t, (r <= c).astype(F32), precision=HIGHEST)
    a_last = jnp.sum(dta_t, axis=1, keepdims=True)
    return dt, a_cum, a_cum_t, a_last


def _ssd_pair(x_pair, bg, cg, hp, dt, a_cum, a_cum_t, a_last, *, e0):
    ln = SSD_CHUNK
    lane = lax.broadcasted_iota(jnp.int32, (ln, LANES), 1)
    sub = lax.broadcasted_iota(jnp.int32, (LANES, SSD_STATE), 0)
    row = lax.broadcasted_iota(jnp.int32, (ln, ln), 0)
    col = lax.broadcasted_iota(jnp.int32, (ln, ln), 1)
    lo = lane < SSD_HEAD_DIM
    e1 = e0 + 1
    c0, c1 = a_cum[:, e0:e0 + 1], a_cum[:, e1:e1 + 1]
    r0, r1 = a_cum_t[e0:e0 + 1, :], a_cum_t[e1:e1 + 1, :]
    l0, l1 = a_last[e0:e0 + 1, :], a_last[e1:e1 + 1, :]
    xd = x_pair * jnp.where(lo, dt[:, e0:e0 + 1], dt[:, e1:e1 + 1])
    causal = row >= col
    cb = _bdot(cg, bg, NT)
    m0 = cb * jnp.exp(jnp.where(causal, c0 - r0, -jnp.inf))
    m1 = cb * jnp.exp(jnp.where(causal, c1 - r1, -jnp.inf))
    y = _bdot(m0, jnp.where(lo, xd, 0.0)) + _bdot(m1, jnp.where(lo, 0.0, xd))
    acum_pair = jnp.where(lo, c0, c1)
    y = y + _bdot(cg, hp, NT) * jnp.exp(acum_pair)
    last_pair = jnp.where(lo, l0, l1)
    st = _bdot(xd * jnp.exp(last_pair - acum_pair), bg, TN)
    h_out = hp * jnp.exp(jnp.where(sub < SSD_HEAD_DIM, l0, l1)) + st
    return y, h_out


def _ssd_in_specs(chunk_of):
    return [
        pl.BlockSpec((SSD_CHUNK, SSD_CONV_CH), lambda i: (chunk_of(i), 0)),
        pl.BlockSpec((SSD_CHUNK, HPAD), lambda i: (chunk_of(i), 0)),
        pl.BlockSpec((HPAD, SSD_CHUNK), lambda i: (0, chunk_of(i))),
        pl.BlockSpec((1, HPAD), lambda i: (0, 0)), pl.BlockSpec((HPAD, 1), lambda i: (0, 0)),
        pl.BlockSpec((1, HPAD), lambda i: (0, 0)), pl.BlockSpec((HPAD, 1), lambda i: (0, 0)),
    ]


def ssd_fwd(xc, dtr, dtr_t, dtb, dtb_t, alog, alog_t, name):
    def body(xc_ref, dtr_ref, dtrt_ref, dtb_ref, dtbt_ref, al_ref, alt_ref, y_ref, hs_ref, h_scr):
        @pl.when(pl.program_id(0) == 0)
        def _():
            h_scr[...] = jnp.zeros_like(h_scr)

        hs_ref[0] = h_scr[...]
        dt, a_cum, a_cum_t, a_last = _ssd_prefix(dtr_ref[...], dtrt_ref[...], dtb_ref[...], dtbt_ref[...],
                                                  al_ref[...], alt_ref[...])
        for pr in range(N_PAIRS):
            g = pr // PAIRS_PER_GROUP
            sl = slice(pr * LANES, (pr + 1) * LANES)
            bg = xc_ref[:, B_OFF + g * SSD_STATE:B_OFF + (g + 1) * SSD_STATE]
            cg = xc_ref[:, C_OFF + g * SSD_STATE:C_OFF + (g + 1) * SSD_STATE]
            y, h_out = _ssd_pair(xc_ref[:, sl], bg, cg, h_scr[sl, :], dt, a_cum, a_cum_t, a_last, e0=2 * pr)
            y_ref[:, sl] = y
            h_scr[sl, :] = h_out

    return pl.pallas_call(
        body, name=name, grid=(N_CHUNKS,), in_specs=_ssd_in_specs(lambda i: i),
        out_specs=[pl.BlockSpec((SSD_CHUNK, SSD_INNER), lambda i: (i, 0)),
                   pl.BlockSpec((1, SSD_INNER, SSD_STATE), lambda i: (i, 0, 0))],
        out_shape=[SDS((SEQ, SSD_INNER), F32), SDS((N_CHUNKS, SSD_INNER, SSD_STATE), F32)],
        scratch_shapes=[pltpu.VMEM((SSD_INNER, SSD_STATE), F32)],
        compiler_params=_cparams(("arbitrary",)),
    )(xc, dtr, dtr_t, dtb, dtb_t, alog, alog_t)


def ssd_bwd(xc, dtr, dtr_t, dtb, dtb_t, alog, alog_t, hs, dy, dxs_extra, name):
    rev = lambda i: N_CHUNKS - 1 - i

    def body(xc_ref, dtr_ref, dtrt_ref, dtb_ref, dtbt_ref, al_ref, alt_ref, hs_ref, dy_ref, dxe_ref,
             dxc_ref, ddtr_ref, ddtrt_ref, ddtb_ref, ddtbt_ref, dal_ref, dalt_ref, dh_scr):
        @pl.when(pl.program_id(0) == 0)
        def _():
            dh_scr[...] = jnp.zeros_like(dh_scr)
            for r in (ddtb_ref, ddtbt_ref, dal_ref, dalt_ref):
                r[...] = jnp.zeros_like(r)

        prefix_in = (dtr_ref[...], dtrt_ref[...], dtb_ref[...], dtbt_ref[...], al_ref[...], alt_ref[...])
        (dt, a_cum, a_cum_t, a_last), prefix_vjp = jax.vjp(_ssd_prefix, *prefix_in)
        d_dt = jnp.zeros_like(dt)
        d_acum = jnp.zeros_like(a_cum)
        d_acum_t = jnp.zeros_like(a_cum_t)
        d_alast = jnp.zeros_like(a_last)
        for g in range(SSD_GROUPS):
            bg = xc_ref[:, B_OFF + g * SSD_STATE:B_OFF + (g + 1) * SSD_STATE]
            cg = xc_ref[:, C_OFF + g * SSD_STATE:C_OFF + (g + 1) * SSD_STATE]
            d_bg = jnp.zeros_like(bg)
            d_cg = jnp.zeros_like(cg)
            for j in range(PAIRS_PER_GROUP):
                pr = g * PAIRS_PER_GROUP + j
                sl = slice(pr * LANES, (pr + 1) * LANES)
                _, vjp = jax.vjp(functools.partial(_ssd_pair, e0=2 * pr),
                                 xc_ref[:, sl], bg, cg, hs_ref[0, sl, :], dt, a_cum, a_cum_t, a_last)
                dx, dbg, dcg, dhp, ddt, dac, dact, dal = vjp((dy_ref[:, sl], dh_scr[sl, :]))
                dxc_ref[:, sl] = dx + dxe_ref[:, sl]
                dh_scr[sl, :] = dhp
                d_bg, d_cg = d_bg + dbg, d_cg + dcg
                d_dt, d_acum, d_acum_t, d_alast = d_dt + ddt, d_acum + dac, d_acum_t + dact, d_alast + dal
            dxc_ref[:, B_OFF + g * SSD_STATE:B_OFF + (g + 1) * SSD_STATE] = d_bg
            dxc_ref[:, C_OFF + g * SSD_STATE:C_OFF + (g + 1) * SSD_STATE] = d_cg
        g_dtr, g_dtrt, g_dtb, g_dtbt, g_al, g_alt = prefix_vjp((d_dt, d_acum, d_acum_t, d_alast))
        ddtr_ref[...] = g_dtr
        ddtrt_ref[...] = g_dtrt
        ddtb_ref[...] += g_dtb
        ddtbt_ref[...] += g_dtbt
        dal_ref[...] += g_al
        dalt_ref[...] += g_alt

    in_specs = _ssd_in_specs(rev) + [
        pl.BlockSpec((1, SSD_INNER, SSD_STATE), lambda i: (rev(i), 0, 0)),
        pl.BlockSpec((SSD_CHUNK, SSD_INNER), lambda i: (rev(i), 0)),
        pl.BlockSpec((SSD_CHUNK, SSD_INNER), lambda i: (rev(i), 0)),
    ]
    out_specs = [
        pl.BlockSpec((SSD_CHUNK, SSD_CONV_CH), lambda i: (rev(i), 0)),
        pl.BlockSpec((SSD_CHUNK, HPAD), lambda i: (rev(i), 0)),
        pl.BlockSpec((HPAD, SSD_CHUNK), lambda i: (0, rev(i))),
        pl.BlockSpec((1, HPAD), lambda i: (0, 0)), pl.BlockSpec((HPAD, 1), lambda i: (0, 0)),
        pl.BlockSpec((1, HPAD), lambda i: (0, 0)), pl.BlockSpec((HPAD, 1), lambda i: (0, 0)),
    ]
    out_shape = [SDS((SEQ, SSD_CONV_CH), F32), SDS((SEQ, HPAD), F32), SDS((HPAD, SEQ), F32),
                 SDS((1, HPAD), F32), SDS((HPAD, 1), F32), SDS((1, HPAD), F32), SDS((HPAD, 1), F32)]
    return pl.pallas_call(
        body, name=name, grid=(N_CHUNKS,), in_specs=in_specs, out_specs=out_specs, out_shape=out_shape,
        scratch_shapes=[pltpu.VMEM((SSD_INNER, SSD_STATE), F32)],
        compiler_params=_cparams(("arbitrary",)),
    )(xc, dtr, dtr_t, dtb, dtb_t, alog, alog_t, hs, dy, dxs_extra)


ATTN_SCALE = ATTN_HEAD_DIM ** -0.5


def _attn_scores(q, kp, kc, has_prev):
    qi = lax.broadcasted_iota(jnp.int32, (ATTN_BLOCK, ATTN_BLOCK), 0)
    kj = lax.broadcasted_iota(jnp.int32, (ATTN_BLOCK, ATTN_BLOCK), 1)
    s_c = jnp.where(qi >= kj, _bdot(q, kc, NT) * ATTN_SCALE, -jnp.inf)
    s_p = jnp.where((kj >= qi) & has_prev, _bdot(q, kp, NT) * ATTN_SCALE, -jnp.inf)
    return s_p, s_c


UNITS_PER_PATTERN = SEQ // ATTN_BLOCK
ATTN_UNROLL = 2


def _for_units(unit):
    for g, d in enumerate(ATTN_DILATIONS):
        nb = UNITS_PER_PATTERN // d
        span = d * ATTN_BLOCK

        def one(i, carry, g=g, d=d, nb=nb, span=span):
            r = i >> (nb.bit_length() - 1)
            n = i & (nb - 1)
            start = r + n * span
            prev = jnp.where(n > 0, start - span, start)
            unit(g, pl.ds(start, ATTN_BLOCK, stride=d), pl.ds(prev, ATTN_BLOCK, stride=d), n > 0)
            return carry
        lax.fori_loop(0, UNITS_PER_PATTERN, one, 0, unroll=ATTN_UNROLL)


def _head_specs(n_q_groups):
    blk = (SEQ, ATTN_HEAD_DIM)
    q_specs = [pl.BlockSpec(blk, functools.partial(lambda h, g: (0, g * ATTN_KV_HEADS + h), g=g)) for g in range(n_q_groups)]
    head = pl.BlockSpec(blk, lambda h: (0, h))
    table = pl.BlockSpec(blk, lambda h: (0, 0))
    return q_specs, head, table


def attn_fwd(q, k, v, tabs, name):
    q_specs, head, table = _head_specs(ATTN_N_PAT)

    def body(q0_ref, q1_ref, q2_ref, k_ref, v_ref, c_ref, sa_ref, sb_ref, y_ref, lse_ref, *scr):
        qs, og, ls, ks = scr[0:3], scr[3:6], scr[6:9], scr[9]
        c, sa, sb = c_ref[...], sa_ref[...], sb_ref[...]
        for g, q_ref in enumerate((q0_ref, q1_ref, q2_ref)):
            qs[g][...] = _rope(q_ref[...], c, sa, sb)
        ks[...] = _rope(k_ref[...], c, sa, sb)

        def unit(g, rows, prows, has_prev):
            s_p, s_c = _attn_scores(qs[g][rows, :], ks[prows, :], ks[rows, :], has_prev)
            m = jnp.maximum(jnp.max(s_c, axis=1, keepdims=True), jnp.max(s_p, axis=1, keepdims=True))
            p_c, p_p = jnp.exp(s_c - m), jnp.exp(s_p - m)
            l = jnp.sum(p_c, axis=1, keepdims=True) + jnp.sum(p_p, axis=1, keepdims=True)
            o = _bdot(p_c, v_ref[rows, :]) + _bdot(p_p, v_ref[prows, :])
            og[g][rows, :] = o / l
            ls[g][rows, :] = jnp.broadcast_to(m + jnp.log(l), (ATTN_BLOCK, LANES))

        _for_units(unit)
        l0, l1, l2 = ls[0][...], ls[1][...], ls[2][...]
        m = jnp.maximum(jnp.maximum(l0, l1), l2)
        e0, e1, e2 = jnp.exp(l0 - m), jnp.exp(l1 - m), jnp.exp(l2 - m)
        den = e0 + e1 + e2
        y_ref[...] = ((e0 * og[0][...] + e1 * og[1][...] + e2 * og[2][...]) / den).astype(y_ref.dtype)
        lse_ref[...] = m + jnp.log(den)

    blk = (SEQ, ATTN_HEAD_DIM)
    return pl.pallas_call(
        body, name=name, grid=(ATTN_KV_HEADS,), in_specs=[*q_specs, head, head, table, table, table],
        out_specs=[head, head], out_shape=[SDS((SEQ, ATTN_OUT), BF16), SDS((SEQ, ATTN_OUT), F32)],
        scratch_shapes=[pltpu.VMEM(blk, F32)] * (3 * ATTN_N_PAT + 1),
        compiler_params=_cparams(("parallel",)),
    )(q, q, q, k, v, *tabs)


def attn_bwd(q, k, v, tabs, y, lse, dy, name):
    q_specs, head, table = _head_specs(ATTN_N_PAT)

    def body(q0_ref, q1_ref, q2_ref, k_ref, v_ref, c_ref, sa_ref, sb_ref, y_ref, lse_ref, dy_ref,
             dq0_ref, dq1_ref, dq2_ref, dk_ref, dv_ref, *scr):
        qs, dqs, ks, dks, dd, dvs = scr[0:3], scr[3:6], scr[6], scr[7], scr[8], scr[9]
        c, sa, sb = c_ref[...], sa_ref[...], sb_ref[...]
        for g, q_ref in enumerate((q0_ref, q1_ref, q2_ref)):
            qs[g][...] = _rope(q_ref[...], c, sa, sb)
        ks[...] = _rope(k_ref[...], c, sa, sb)
        dks[...] = jnp.zeros_like(dks)
        dvs[...] = jnp.zeros_like(dvs)
        dyv = dy_ref[...]
        dd[...] = jnp.broadcast_to(jnp.sum(dyv * y_ref[...].astype(F32), axis=1, keepdims=True), dd.shape)

        def unit(g, rows, prows, has_prev):
            qv = qs[g][rows, :].astype(BF16)
            kc, kp = ks[rows, :].astype(BF16), ks[prows, :].astype(BF16)
            vc, vp = v_ref[rows, :].astype(BF16), v_ref[prows, :].astype(BF16)
            do = dy_ref[rows, :].astype(BF16)
            s_p, s_c = _attn_scores(qv, kp, kc, has_prev)
            lse_u = lse_ref[rows, :][:, 0:1]
            dsum = dd[rows, :][:, 0:1]
            p_c, p_p = jnp.exp(s_c - lse_u), jnp.exp(s_p - lse_u)
            ds_c = (p_c * (_dot(do, vc, NT) - dsum) * ATTN_SCALE).astype(BF16)
            ds_p = (p_p * (_dot(do, vp, NT) - dsum) * ATTN_SCALE).astype(BF16)
            dqs[g][rows, :] = _dot(ds_c, kc) + _dot(ds_p, kp)
            dks[rows, :] += _dot(ds_c, qv, TN)
            dks[prows, :] += _dot(ds_p, qv, TN)
            dvs[rows, :] += _bdot(p_c, do, TN)
            dvs[prows, :] += _bdot(p_p, do, TN)

        _for_units(unit)
        for g, dq_ref in enumerate((dq0_ref, dq1_ref, dq2_ref)):
            dq_ref[...] = _rope(dqs[g][...], c, -sa, -sb).astype(dq_ref.dtype)
        dk_ref[...] = _rope(dks[...], c, -sa, -sb).astype(dk_ref.dtype)
        dv_ref[...] = dvs[...].astype(dv_ref.dtype)

    blk = (SEQ, ATTN_HEAD_DIM)
    out = SDS((SEQ, ATTN_OUT), BF16)
    return pl.pallas_call(
        body, name=name, grid=(ATTN_KV_HEADS,), in_specs=[*q_specs, head, head, table, table, table, head, head, head],
        out_specs=[head] * 5, out_shape=[out] * 5,
        scratch_shapes=[pltpu.VMEM(blk, F32)] * (2 * ATTN_N_PAT + 4),
        compiler_params=_cparams(("parallel",)),
    )(q, q, q, k, v, *tabs, y, lse, dy)


def layer_fwd(h, getw, small, tabs, li):
    n = f"l{li}_"
    sv = {}
    w = dict(getw(0, h))
    u = rms_fwd(h, small["norm_mix"], n + "rms_mix")
    z = matmul(u, w["w_z"], name=n + "mm_z", tb=True)
    xbc = matmul(u, w["w_xbc"], name=n + "mm_xbc", tb=True)
    dtr = matmul(u, w["w_dt"], name=n + "mm_dt", tb=True)
    q = matmul(u, w["w_q"], name=n + "mm_q", tb=True)
    k = matmul(u, w["w_k"], name=n + "mm_k", tb=True)
    v = matmul(u, w["w_v"], name=n + "mm_v", tb=True)
    gs = matmul(u, w["w_gs"], name=n + "mm_gs", tb=True)
    ga = matmul(u, w["w_ga"], name=n + "mm_ga", tb=True)
    xc = conv_fwd(xbc, w["conv_w"], small["conv_b"], n + "conv")
    dtr_t = dtr.T
    y_ssd, hs = ssd_fwd(xc, dtr, dtr_t, small["dt_bias"], small["dt_bias"].T, small["a_log"], small["a_log"].T, n + "ssd")
    yn = ssd_post_fwd(y_ssd, xc, z, small["d_skip_x"], small["ssd_norm"], n + "ssd_post")
    y_attn, lse = attn_fwd(q, k, v, tabs, n + "attn")
    w.update(getw(1, y_ssd))
    a = matmul(yn, w["w_ssd_branch"], name=n + "mm_a")
    b = matmul(y_attn, w["w_attn_branch"], name=n + "mm_b")
    merged = gate_fwd(a, b, gs, ga, n + "gate")
    h1 = matmul(merged, w["w_out"], name=n + "mm_o", add=h)
    w.update(getw(2, h1))
    u2 = rms_fwd(h1, small["norm_ffn"], n + "rms_ffn")
    gu = matmul(u2, w["w_gate_up"], name=n + "mm_gu", tb=True)
    act = swiglu_fwd(gu, n + "swiglu")
    h2 = matmul(act, w["w_down"], name=n + "mm_down", add=h1)
    sv.update(h=h, u=u, z=z, xbc=xbc, dtr=dtr, dtr_t=dtr_t, gs=gs, ga=ga, xc=xc, y_ssd=y_ssd, hs=hs, yn=yn,
              q=q, k=k, v=v, y_attn=y_attn, lse=lse, a=a, b=b, merged=merged, h1=h1, u2=u2, gu=gu, act=act, w=w)
    return h2, sv


def layer_bwd(dh, sv, small, tabs, li, emit):
    n = f"l{li}_b_"
    w = sv["w"]
    gw, gsm = {}, {}
    dact = matmul(dh, w["w_down"], name=n + "mm_dact", tb=True, out_dtype=BF16)
    gw["w_down"] = matmul(sv["act"], dh, name=n + "mm_dwdown", ta=True, out_dtype=BF16)
    dgu = swiglu_bwd(sv["gu"], dact, n + "swiglu")
    gw["w_gate_up"] = matmul(dgu, sv["u2"], name=n + "mm_dwgu", ta=True, out_dtype=BF16)
    tok = emit(2, gw)
    du2 = matmul(dgu, w["w_gate_up"], name=n + "mm_du2")
    dh1, gsm["norm_ffn"] = rms_bwd(sv["h1"], du2, dh, small["norm_ffn"] + tok, n + "rms_ffn")
    dmerged = matmul(dh1, w["w_out"], name=n + "mm_dmerged", tb=True)
    gw["w_out"] = matmul(sv["merged"], dh1, name=n + "mm_dwo", ta=True, out_dtype=BF16)
    da, db, dgs, dga = gate_bwd(sv["a"], sv["b"], sv["gs"], sv["ga"], dmerged, n + "gate")
    gw["w_ssd_branch"] = matmul(sv["yn"], da, name=n + "mm_dwa", ta=True, out_dtype=BF16)
    gw["w_attn_branch"] = matmul(sv["y_attn"], db, name=n + "mm_dwb", ta=True, out_dtype=BF16)
    tok = emit(1, gw)
    dyn = matmul(da, w["w_ssd_branch"], name=n + "mm_dyn", tb=True)
    dyattn = matmul(db, w["w_attn_branch"], name=n + "mm_dyattn", tb=True)
    dy_ssd, dxs_extra, dz, gsm["d_skip_x"], gsm["ssd_norm"] = ssd_post_bwd(
        sv["y_ssd"], sv["xc"], sv["z"], small["d_skip_x"] + tok, small["ssd_norm"], dyn, n + "ssd_post")
    dxc, ddtr, ddtr_t, ddtb, ddtb_t, dal, dal_t = ssd_bwd(
        sv["xc"], sv["dtr"], sv["dtr_t"], small["dt_bias"], small["dt_bias"].T, small["a_log"], small["a_log"].T,
        sv["hs"], dy_ssd, dxs_extra, n + "ssd")
    ddtr = (ddtr + ddtr_t.T).astype(BF16)
    gsm["dt_bias"] = ddtb + ddtb_t.T
    gsm["a_log"] = dal + dal_t.T
    dxbc, gw["conv_w"], gsm["conv_b"] = conv_bwd(sv["xbc"], w["conv_w"], small["conv_b"], dxc, n + "conv")
    dq0, dq1, dq2, dk, dv = attn_bwd(sv["q"], sv["k"], sv["v"], tabs, sv["y_attn"], sv["lse"], dyattn, n + "attn")
    u = sv["u"]
    segs = [("w_z", dz), ("w_xbc", dxbc), ("w_dt", ddtr), ("w_q0", dq0), ("w_q1", dq1), ("w_q2", dq2),
            ("w_k", dk), ("w_v", dv), ("w_gs", dgs), ("w_ga", dga)]
    gin = [matmul(dseg, u, name=n + "mm_d" + key, ta=True, out_dtype=BF16) for key, dseg in segs]
    gin[2] = gin[2][:SSD_HEADS]
    gw["w_in"] = jnp.concatenate(gin, axis=0)
    tok = emit(0, gw)
    du = None
    for key, dseg in segs:
        du = matmul(dseg, w[key], name=n + "mm_du_" + key, add=du)
    dh0, gsm["norm_mix"] = rms_bwd(sv["h"], du, dh1, small["norm_mix"] + tok, n + "rms_mix")
    return dh0, gsm


def _my_place():
    return lax.axis_index("x"), lax.axis_index("y"), lax.axis_index("c")


def _flip(place, k):
    x, y, c = place
    return (1 - x if k & 4 else x, 1 - y if k & 2 else y, 1 - c if k & 1 else c)


def _index(place):
    return 4 * place[0] + 2 * place[1] + place[2]


ANY = pl.BlockSpec(memory_space=pl.ANY)
CHIP_FLIPS = (4, 2, 6)


def all_gather(xs, name):
    na = len(xs)

    def body(*refs):
        x_refs, o_refs = refs[:na], refs[na:2 * na]
        send_sems, recv_sems, local_sems = refs[2 * na:]
        me = _my_place()
        sibling = _flip(me, 1)
        chips = [_flip(me, f) for f in CHIP_FLIPS]

        def copy(a, kk, block, to, src=None):
            dst = o_refs[a].at[_index(block)]
            return pltpu.make_async_remote_copy(
                src_ref=dst if src is None else src, dst_ref=dst, send_sem=send_sems.at[a, kk],
                recv_sem=recv_sems.at[a, kk], device_id=to, device_id_type=MESH)

        mine = [pltpu.make_async_copy(x_refs[a], o_refs[a].at[_index(me)], local_sems.at[a]) for a in range(na)]
        for cp in mine:
            cp.start()
        first = []
        for j, chip in enumerate(chips):
            first += [copy(a, 1 + j, me, chip, src=x_refs[a]) for a in range(na)]
        first += [copy(a, 0, me, sibling, src=x_refs[a]) for a in range(na)]
        for cp in first:
            cp.start()
        passed = []
        for j, chip in enumerate(chips):
            for a in range(na):
                copy(a, 1 + j, chip, me).wait_recv()
                cp = copy(a, 4 + j, chip, sibling)
                cp.start()
                passed.append(cp)
        for a in range(na):
            copy(a, 0, sibling, me).wait_recv()
        for j, chip in enumerate(chips):
            for a in range(na):
                copy(a, 4 + j, _flip(chip, 1), me).wait_recv()
        for cp in first + passed:
            cp.wait_send()
        for cp in mine:
            cp.wait()

    return pl.pallas_call(
        body, name=name, in_specs=[ANY] * na, out_specs=[ANY] * na,
        out_shape=[SDS((N_DEV,) + t.shape, t.dtype) for t in xs],
        scratch_shapes=[pltpu.SemaphoreType.DMA((na, N_DEV - 1)), pltpu.SemaphoreType.DMA((na, N_DEV - 1)),
                        pltpu.SemaphoreType.DMA((na,))],
    )(*xs)


HBM = pl.BlockSpec(memory_space=pltpu.HBM)
SEM = pl.BlockSpec(memory_space=pltpu.SEMAPHORE)
EFFECT = pltpu.SideEffectType.DATAFLOW_SIDE_EFFECTING
N_PEERS = N_DEV - 1


def _split_copy(src_ref, land_ref, send_sem, recv_sem, me, kk, scatter, landed_from_peer):
    peer = _flip(me, kk)
    src = src_ref.at[_index(peer)] if scatter else src_ref
    dst = land_ref.at[_index(peer if landed_from_peer else me)]
    return pltpu.make_async_remote_copy(src_ref=src, dst_ref=dst, send_sem=send_sem, recv_sem=recv_sem,
                                        device_id=peer, device_id_type=MESH)


def exchange_start(srcs, lands, group_sizes, scatter, name):
    na, ng = len(srcs), len(group_sizes)

    def body(*refs):
        s_refs, l_refs = refs[:na], refs[na:2 * na]
        sems = refs[2 * na:2 * na + 2 * ng]
        token = refs[-1]
        me = _my_place()
        a = 0
        for gi, gsz in enumerate(group_sizes):
            for j in range(gsz):
                for kk in range(1, N_DEV):
                    slot = j * N_PEERS + kk - 1
                    _split_copy(s_refs[a], l_refs[a], sems[2 * gi].at[slot], sems[2 * gi + 1].at[slot],
                                me, kk, scatter, False).start()
                a += 1
        token[...] = jnp.zeros_like(token)

    sem_shapes = []
    for gsz in group_sizes:
        sem_shapes += [pltpu.SemaphoreType.DMA((gsz * N_PEERS,))] * 2
    ins = [pltpu.with_memory_space_constraint(t, pltpu.HBM) for t in (*srcs, *lands)]
    res = pl.pallas_call(
        body, name=name, in_specs=[HBM] * (2 * na),
        out_specs=[SEM] * (2 * ng) + [HBM] * (2 * na) + [pl.BlockSpec(memory_space=pltpu.VMEM)],
        out_shape=sem_shapes + [pltpu.HBM(t.shape, t.dtype) for t in ins] + [SDS((8, LANES), F32)],
        input_output_aliases={i: 2 * ng + i for i in range(2 * na)},
        compiler_params=pltpu.CompilerParams(has_side_effects=EFFECT),
    )(*ins)
    sems = [(res[2 * gi], res[2 * gi + 1]) for gi in range(ng)]
    thru = res[2 * ng:2 * ng + 2 * na]
    return sems, thru[:na], thru[na:], res[-1]


def exchange_wait(srcs, lands, sems, after, scatter, name):
    n = len(srcs)

    def body(*refs):
        s_refs, l_refs = refs[:n], refs[n:2 * n]
        send_sems, recv_sems = refs[2 * n], refs[2 * n + 1]
        me = _my_place()
        for j in range(n):
            for kk in range(1, N_DEV):
                slot = j * N_PEERS + kk - 1
                cp = _split_copy(s_refs[j], l_refs[j], send_sems.at[slot], recv_sems.at[slot], me, kk, scatter, True)
                cp.wait_send()
                cp.wait_recv()

    res = pl.pallas_call(
        body, name=name, in_specs=[HBM] * (2 * n) + [SEM, SEM, ANY], out_specs=[HBM] * (2 * n),
        out_shape=[pltpu.HBM(t.shape, t.dtype) for t in (*srcs, *lands)],
        input_output_aliases={i: i for i in range(2 * n)},
        compiler_params=pltpu.CompilerParams(has_side_effects=EFFECT),
    )(*srcs, *lands, sems[0], sems[1], after)
    return res[n:]


def landing_zone(block, me_index):
    land = lax.empty((N_DEV,) + block.shape, block.dtype)
    return lax.dynamic_update_slice(land, block[None], (me_index,) + (0,) * block.ndim)


def sum_parts(parts, name):
    _, r, c = parts.shape
    tc = _pick(c, (256, 128))

    def body(p_ref, o_ref):
        acc = p_ref[0].astype(F32)
        for i in range(1, N_DEV):
            acc = acc + p_ref[i].astype(F32)
        o_ref[...] = acc

    return pl.pallas_call(
        body, name=name, grid=(c // tc,), in_specs=[pl.BlockSpec((N_DEV, r, tc), lambda i: (0, 0, i))],
        out_specs=pl.BlockSpec((r, tc), lambda i: (0, i)), out_shape=SDS((r, c), F32),
        compiler_params=_cparams(("parallel",)),
    )(parts)


def adamw(w, g, m, v, name):
    shape = w.shape
    cols = shape[-1]
    rows = w.size // cols
    tr = _pick(rows, (256, 128, 64, 32, 16, 8))
    c1 = 1.0 / (1.0 - ADAM_B1 ** ADAM_STEP)
    c2 = 1.0 / (1.0 - ADAM_B2 ** ADAM_STEP)

    def body(w_ref, g_ref, m_ref, v_ref, d_ref, nm_ref, nv_ref):
        gg = g_ref[...]
        nm = ADAM_B1 * m_ref[...] + (1.0 - ADAM_B1) * gg
        nv = ADAM_B2 * v_ref[...] + (1.0 - ADAM_B2) * (gg * gg)
        d_ref[...] = -ADAM_LR * ((nm * c1) / (jnp.sqrt(nv * c2) + ADAM_EPS) + ADAM_WD * w_ref[...])
        nm_ref[...] = nm
        nv_ref[...] = nv

    spec = pl.BlockSpec((tr, cols), lambda i: (i, 0))
    outs = pl.pallas_call(
        body, name=name, grid=(rows // tr,), in_specs=[spec] * 4, out_specs=[spec] * 3,
        out_shape=[SDS((rows, cols), F32)] * 3, compiler_params=_cparams(("parallel",)),
    )(*[t.reshape(rows, cols) for t in (w, g, m, v)])
    return [o.reshape(shape) for o in outs]


BIG = ("w_in", "conv_w", "w_ssd_branch", "w_attn_branch", "w_out", "w_gate_up", "w_down")
TRANSPOSED = ("w_in", "w_gate_up")
SMALL = ("norm_mix", "conv_b", "dt_bias", "a_log", "d_skip", "ssd_norm", "norm_ffn")
SMALL_SIZE = {"norm_mix": 1024, "conv_b": 3072, "dt_bias": 32, "a_log": 32, "d_skip": 32, "ssd_norm": 2048, "norm_ffn": 1024}
FLAT_W = 512
SMALL_TOTAL = DEPTH * sum(SMALL_SIZE.values()) + D_MODEL + LANES
SMALL_ROWS = 32
assert SMALL_ROWS * FLAT_W >= SMALL_TOTAL


GROUPS = (("w_in", "conv_w"), ("w_ssd_branch", "w_attn_branch", "w_out"), ("w_gate_up", "w_down"))


def to_wire(k, shard):
    if k in TRANSPOSED:
        return shard.T.astype(BF16)
    return shard if k == "conv_w" else shard.astype(BF16)


def full_weights(k, g):
    if k == "conv_w":
        return {k: g.transpose(1, 0, 2).reshape(SSD_CONV, SSD_CONV_CH)}
    full = g.reshape(-1, g.shape[-1])
    if k != "w_in":
        return {k: full}
    w, off = {}, 0
    for nm, r in IN_ROWS:
        w[nm] = full[off:off + r]
        off += r
    w["w_q"] = full[sum(r for _, r in IN_ROWS[:3]):sum(r for _, r in IN_ROWS[:6])]
    w["w_dt"] = jnp.pad(w["w_dt"], ((0, HPAD - SSD_HEADS), (0, 0)))
    return w


def grads_to_wire(k, g):
    if k == "conv_w":
        return g.reshape(SSD_CONV, N_DEV, SSD_CONV_CH // N_DEV).transpose(1, 0, 2)
    return g.reshape(N_DEV, g.shape[0] // N_DEV, g.shape[1])


def _pad_heads(t):
    return jnp.pad(t.reshape(1, SSD_HEADS), ((0, 0), (0, HPAD - SSD_HEADS)))


def local_step(x, target, getw, emit, smalls, norm_final):
    tabs = rope_tables()
    sms = []
    for li in range(DEPTH):
        s = smalls[li]
        sms.append({
            "norm_mix": s["norm_mix"].reshape(1, -1), "conv_b": s["conv_b"].reshape(1, -1),
            "dt_bias": _pad_heads(s["dt_bias"]), "a_log": _pad_heads(s["a_log"]),
            "d_skip_x": jnp.repeat(s["d_skip"], SSD_HEAD_DIM).reshape(1, -1),
            "ssd_norm": s["ssd_norm"].reshape(1, -1), "norm_ffn": s["norm_ffn"].reshape(1, -1)})
    h = x
    saved = []
    for li in range(DEPTH):
        h, sv = layer_fwd(h, functools.partial(getw, li), sms[li], tabs, li)
        saved.append(sv)
    dh, g_final, loss = loss_head(h, target, norm_final.reshape(1, -1), "loss_head")
    gsms = [None] * DEPTH
    for li in reversed(range(DEPTH)):
        dh, gsm = layer_bwd(dh, saved[li], sms[li], tabs, li, functools.partial(emit, li))
        gsms[li] = {
            "norm_mix": gsm["norm_mix"].reshape(-1), "conv_b": gsm["conv_b"].reshape(-1),
            "dt_bias": gsm["dt_bias"][0, :SSD_HEADS], "a_log": gsm["a_log"][0, :SSD_HEADS],
            "d_skip": gsm["d_skip_x"].reshape(SSD_HEADS, SSD_HEAD_DIM).sum(axis=1),
            "ssd_norm": gsm["ssd_norm"].reshape(-1), "norm_ffn": gsm["norm_ffn"].reshape(-1)}
    return loss, dh, gsms, g_final.reshape(-1)


def kernel(x, norm_mix, w_in, conv_w, conv_b, dt_bias, a_log, d_skip, ssd_norm, w_ssd_branch, w_attn_branch, w_out, norm_ffn, w_gate_up, w_down, norm_final, loss_target, m_norm_mix, m_w_in, m_conv_w, m_conv_b, m_dt_bias, m_a_log, m_d_skip, m_ssd_norm, m_w_ssd_branch, m_w_attn_branch, m_w_out, m_norm_ffn, m_w_gate_up, m_w_down, m_norm_final, v_norm_mix, v_w_in, v_conv_w, v_conv_b, v_dt_bias, v_a_log, v_d_skip, v_ssd_norm, v_w_ssd_branch, v_w_attn_branch, v_w_out, v_norm_ffn, v_w_gate_up, v_w_down, v_norm_final):
    wv = dict(norm_mix=norm_mix, w_in=w_in, conv_w=conv_w, conv_b=conv_b, dt_bias=dt_bias, a_log=a_log, d_skip=d_skip,
              ssd_norm=ssd_norm, w_ssd_branch=w_ssd_branch, w_attn_branch=w_attn_branch, w_out=w_out, norm_ffn=norm_ffn,
              w_gate_up=w_gate_up, w_down=w_down, norm_final=norm_final)
    mv = dict(norm_mix=m_norm_mix, w_in=m_w_in, conv_w=m_conv_w, conv_b=m_conv_b, dt_bias=m_dt_bias, a_log=m_a_log,
              d_skip=m_d_skip, ssd_norm=m_ssd_norm, w_ssd_branch=m_w_ssd_branch, w_attn_branch=m_w_attn_branch,
              w_out=m_w_out, norm_ffn=m_norm_ffn, w_gate_up=m_w_gate_up, w_down=m_w_down, norm_final=m_norm_final)
    vv = dict(norm_mix=v_norm_mix, w_in=v_w_in, conv_w=v_conv_w, conv_b=v_conv_b, dt_bias=v_dt_bias, a_log=v_a_log,
              d_skip=v_d_skip, ssd_norm=v_ssd_norm, w_ssd_branch=v_w_ssd_branch, w_attn_branch=v_w_attn_branch,
              w_out=v_w_out, norm_ffn=v_norm_ffn, w_gate_up=v_w_gate_up, w_down=v_w_down, norm_final=v_norm_final)
    order = ("norm_mix", "w_in", "conv_w", "conv_b", "dt_bias", "a_log", "d_skip", "ssd_norm", "w_ssd_branch",
             "w_attn_branch", "w_out", "norm_ffn", "w_gate_up", "w_down", "norm_final")

    me_index = _index(_my_place())
    smalls = [{k: wv[k][li] for k in SMALL} for li in range(DEPTH)]
    n_groups = len(GROUPS)

    first_lands = all_gather([to_wire(k, wv[k][0]) for k in GROUPS[0]], "gather_first")
    later = [(li, gi) for li in range(DEPTH) for gi in range(n_groups)][1:]
    behind_first = first_lands[1][0, 0, 0] * 0.0
    srcs = [to_wire(k, wv[k][li] + behind_first if k == "conv_w" else wv[k][li]) for li, gi in later for k in GROUPS[gi]]
    sizes = [len(GROUPS[gi]) for _, gi in later]
    w_sems, w_srcs, w_lands, token = exchange_start(srcs, [landing_zone(s, me_index) for s in srcs], sizes, False, "gather_start")
    smalls[0]["norm_mix"] = smalls[0]["norm_mix"] + token[0, 0]

    def getw(li, gi, after):
        if (li, gi) == (0, 0):
            lands = first_lands
        else:
            slot = later.index((li, gi))
            sl = slice(sum(sizes[:slot]), sum(sizes[:slot + 1]))
            lands = exchange_wait(w_srcs[sl], w_lands[sl], w_sems[slot], after, False, f"gather_wait_{li}_{gi}")
        w = {}
        for k, land in zip(GROUPS[gi], lands):
            w.update(full_weights(k, land))
        return w

    pending = []

    def emit(li, gi, gw):
        parts = [grads_to_wire(k, gw[k]) for k in GROUPS[gi]]
        lands = [landing_zone(lax.dynamic_index_in_dim(p, me_index, 0, keepdims=False), me_index) for p in parts]
        sems, p_thru, l_thru, tok = exchange_start(parts, lands, [len(parts)], True, f"grads_start_{li}_{gi}")
        pending.append((li, gi, sems[0], p_thru, l_thru))
        return tok[0, 0]

    loss_p, dx, gsms, g_final = local_step(x[0], loss_target[0], getw, emit, smalls, norm_final)

    grads, deltas, new_m, new_v = {}, {}, {}, {}

    def update(k):
        w2, g2, m2, v2 = wv[k], grads[k], mv[k], vv[k]
        if w2.ndim == 1:
            w2, g2, m2, v2 = (t.reshape(1, -1) for t in (w2, g2, m2, v2))
        d, nm, nv = adamw(w2, g2, m2, v2, "adamw_" + k)
        deltas[k], new_m[k], new_v[k] = (t.reshape(wv[k].shape) for t in (d, nm, nv))
        return nv

    shard_g = {k: [None] * DEPTH for k in BIG}
    after = dx
    for li, gi, sems, p_thru, l_thru in pending:
        recv = exchange_wait(p_thru, l_thru, sems, after, True, f"grads_wait_{li}_{gi}")
        for k, r in zip(GROUPS[gi], recv):
            if k == "conv_w":
                r = r.reshape(N_DEV, 1, -1)
            after = sum_parts(r, f"sum_{k}_{li}")
            shard_g[k][li] = after.T if k in TRANSPOSED else after.reshape(wv[k].shape[1:])
        if li == 0:
            for k in GROUPS[gi]:
                grads[k] = jnp.stack(shard_g[k])
                after = update(k)

    flat = [gsms[li][k] for li in range(DEPTH) for k in SMALL] + [g_final, loss_p.reshape(-1)]
    flat.append(jnp.zeros((SMALL_ROWS * FLAT_W - SMALL_TOTAL,), F32))
    small_all = all_gather([jnp.concatenate(flat).reshape(SMALL_ROWS, FLAT_W)], "gather_small")[0]
    small_sum = sum_parts(small_all, "sum_small").reshape(-1)
    off = 0
    per_layer = {k: [] for k in SMALL}
    for li in range(DEPTH):
        for k in SMALL:
            per_layer[k].append(small_sum[off:off + SMALL_SIZE[k]])
            off += SMALL_SIZE[k]
    for k in SMALL:
        grads[k] = jnp.stack(per_layer[k])
    grads["norm_final"] = small_sum[off:off + D_MODEL]
    loss = small_sum[off + D_MODEL]
    for k in (*SMALL, "norm_final"):
        update(k)

    return (loss, dx.reshape(x.shape), *[grads[k] for k in order], *[deltas[k] for k in order],
            *[new_m[k] for k in order], *[new_v[k] for k in order])
```

```python
import functools

import jax
import jax.numpy as jnp
from jax import lax
from jax.experimental import pallas as pl
from jax.experimental.pallas import tpu as pltpu

F32, BF16 = jnp.float32, jnp.bfloat16
SDS = jax.ShapeDtypeStruct
MESH = pl.DeviceIdType.MESH

D_MODEL = 1024
SEQ = 2048
DEPTH = 2
RMS_EPS = 1e-5
SSD_INNER = 2048
SSD_HEAD_DIM = 64
SSD_HEADS = 32
SSD_STATE = 128
SSD_GROUPS = 4
SSD_CONV = 4
SSD_CHUNK = 128
SSD_CONV_CH = 3072
ATTN_HEAD_DIM = 128
ATTN_KV_HEADS = 8
ATTN_DILATIONS = (1, 4, 16)
ATTN_N_PAT = 3
ATTN_BLOCK = 128
ATTN_OUT = 1024
ROPE_THETA = 500000.0
ROPE_DIM = 32
FFN_HIDDEN = 2816
ADAM_LR, ADAM_B1, ADAM_B2, ADAM_EPS, ADAM_WD, ADAM_STEP = 0.001, 0.9, 0.999, 1e-08, 0.01, 10

N_DEV = 8
LANES = 128
VMEM_LIMIT = 56 * 1024 * 1024
HPAD = 128
HIGHEST = lax.Precision.HIGHEST

IN_ROWS = (("w_z", 2048), ("w_xbc", 3072), ("w_dt", 32), ("w_q0", 1024), ("w_q1", 1024), ("w_q2", 1024),
           ("w_k", 1024), ("w_v", 1024), ("w_gs", 1024), ("w_ga", 1024))
N_IN = sum(r for _, r in IN_ROWS)


def _cparams(sem):
    return pltpu.CompilerParams(dimension_semantics=sem, vmem_limit_bytes=VMEM_LIMIT)


def _sigmoid(x):
    return 0.5 * jnp.tanh(0.5 * x) + 0.5


def _silu(x):
    return x * _sigmoid(x)


def _softplus(x):
    return jnp.maximum(x, 0.0) + jnp.log(1.0 + jnp.exp(-jnp.abs(x)))


def _dot(a, b, dims=(((1,), (0,)), ((), ())), precision=None):
    return lax.dot_general(a, b, dims, precision=precision, preferred_element_type=F32)


NT = (((1,), (1,)), ((), ()))
TN = (((0,), (0,)), ((), ()))


def _bdot(a, b, dims=(((1,), (0,)), ((), ()))):
    return _dot(a.astype(BF16), b.astype(BF16), dims)


def _pick(dim, cands):
    for c in cands:
        if dim % c == 0:
            return c
    return dim


def matmul(a, b, *, name, ta=False, tb=False, out_dtype=F32, add=None):
    m, k = (a.shape[1], a.shape[0]) if ta else a.shape
    n = b.shape[0] if tb else b.shape[1]
    tm = _pick(m, (1024, 1408, 512, 256, 128))
    tn = _pick(n, (512, 256, 128))
    tk = _pick(k, (1024, 1408, 512, 256, 128))
    nk = k // tk
    a_spec = pl.BlockSpec((tk, tm), lambda i, j, kk: (kk, i)) if ta else pl.BlockSpec((tm, tk), lambda i, j, kk: (i, kk))
    b_spec = pl.BlockSpec((tn, tk), lambda i, j, kk: (j, kk)) if tb else pl.BlockSpec((tk, tn), lambda i, j, kk: (kk, j))
    dims = (((0 if ta else 1,), (1 if tb else 0,)), ((), ()))
    has_add = add is not None

    def body(*refs):
        a_ref, b_ref = refs[:2]
        add_ref = refs[2] if has_add else None
        o_ref = refs[3] if has_add else refs[2]
        acc = refs[-1] if nk > 1 else None
        kk = pl.program_id(2)

        def product():
            return _dot(a_ref[...].astype(BF16), b_ref[...].astype(BF16), dims)

        def finish(r):
            if has_add:
                r = r + add_ref[...].astype(F32)
            o_ref[...] = r.astype(o_ref.dtype)

        if nk == 1:
            finish(product())
            return

        @pl.when(kk == 0)
        def _():
            acc[...] = product()

        @pl.when((kk > 0) & (kk < nk - 1))
        def _():
            acc[...] += product()

        @pl.when(kk == nk - 1)
        def _():
            finish(acc[...] + product())

    in_specs = [a_spec, b_spec]
    args = [a, b]
    if has_add:
        in_specs.append(pl.BlockSpec((tm, tn), lambda i, j, kk: (i, j)))
        args.append(add)
    return pl.pallas_call(
        body, name=name, grid=(m // tm, n // tn, nk),
        in_specs=in_specs, out_specs=pl.BlockSpec((tm, tn), lambda i, j, kk: (i, j)),
        out_shape=SDS((m, n), out_dtype), scratch_shapes=[pltpu.VMEM((tm, tn), F32)] if nk > 1 else [],
        compiler_params=_cparams(("parallel", "parallel", "arbitrary")),
    )(*args)


def rowcall(name, fn, rows, params, row_outs, red_outs=(), tr=256):
    s = rows[0].shape[0]
    n_in = len(rows) + len(params)
    n_row = len(row_outs)

    def body(*refs):
        outs = fn(*[r[...].astype(F32) for r in refs[:n_in]])
        if not isinstance(outs, (tuple, list)):
            outs = (outs,)
        orefs = refs[n_in:]
        for r, o in zip(orefs[:n_row], outs[:n_row]):
            r[...] = o.astype(r.dtype)
        if red_outs:
            @pl.when(pl.program_id(0) == 0)
            def _():
                for r in orefs[n_row:]:
                    r[...] = jnp.zeros_like(r)
            for r, o in zip(orefs[n_row:], outs[n_row:]):
                r[...] += o.astype(F32)

    in_specs = [pl.BlockSpec((tr, a.shape[1]), lambda i: (i, 0)) for a in rows]
    in_specs += [pl.BlockSpec(p.shape, lambda i: (0, 0)) for p in params]
    out_specs = [pl.BlockSpec((tr, c), lambda i: (i, 0)) for c, _ in row_outs]
    out_specs += [pl.BlockSpec(shp, lambda i: (0, 0)) for shp in red_outs]
    out_shape = [SDS((s, c), dt) for c, dt in row_outs] + [SDS(shp, F32) for shp in red_outs]
    res = pl.pallas_call(
        body, name=name, grid=(s // tr,), in_specs=in_specs, out_specs=out_specs, out_shape=out_shape,
        compiler_params=_cparams(("arbitrary",) if red_outs else ("parallel",)),
    )(*rows, *params)
    return res


def _rms(x, w):
    return x * lax.rsqrt(jnp.mean(x * x, axis=-1, keepdims=True) + RMS_EPS) * w


def rms_fwd(h, w, name):
    return rowcall(name, _rms, [h], [w], [(D_MODEL, BF16)])[0]


def rms_bwd(h, du, dres, w, name):
    def fn(hb, dub, dresb, wb):
        _, vjp = jax.vjp(_rms, hb, wb)
        dh, dw = vjp(dub)
        return dh + dresb, dw
    return rowcall(name, fn, [h, du, dres], [w], [(D_MODEL, F32)], [(1, D_MODEL)])


def loss_head(h, target, w, name):
    def fn(hb, tb, wb):
        def f(hh, ww):
            err = _rms(hh, ww) - tb
            return 0.5 * jnp.sum(jnp.mean(err * err, axis=-1, keepdims=True), axis=0, keepdims=True)
        val, vjp = jax.vjp(f, hb, wb)
        dh, dw = vjp(jnp.ones((1, 1), F32))
        return dh, dw, jnp.broadcast_to(val, (1, LANES))
    return rowcall(name, fn, [h, target], [w], [(D_MODEL, F32)], [(1, D_MODEL), (1, LANES)])


def _gate(a, b, gs, ga):
    return _sigmoid(gs) * a + _sigmoid(ga) * b


def gate_fwd(a, b, gs, ga, name):
    return rowcall(name, _gate, [a, b, gs, ga], [], [(D_MODEL, BF16)])[0]


def gate_bwd(a, b, gs, ga, dm, name):
    def fn(ab, bb, gsb, gab, dmb):
        _, vjp = jax.vjp(_gate, ab, bb, gsb, gab)
        return vjp(dmb)
    return rowcall(name, fn, [a, b, gs, ga, dm], [], [(D_MODEL, BF16)] * 4)


def _swiglu(gu):
    return _silu(gu[:, :FFN_HIDDEN]) * gu[:, FFN_HIDDEN:]


def swiglu_fwd(gu, name):
    return rowcall(name, _swiglu, [gu], [], [(FFN_HIDDEN, BF16)])[0]


def swiglu_bwd(gu, dact, name):
    def fn(gub, db):
        _, vjp = jax.vjp(_swiglu, gub)
        return vjp(db.astype(F32))[0]
    return rowcall(name, fn, [gu, dact], [], [(2 * FFN_HIDDEN, BF16)])[0]


def _ssd_post(y, xs, z, dskip, normw):
    y = (y + dskip * xs) * _silu(z)
    gw = SSD_INNER // SSD_GROUPS
    parts = []
    for g in range(SSD_GROUPS):
        yg = y[:, g * gw:(g + 1) * gw]
        parts.append(yg * lax.rsqrt(jnp.mean(yg * yg, axis=-1, keepdims=True) + RMS_EPS))
    return jnp.concatenate(parts, axis=-1) * normw


def ssd_post_fwd(y, xc, z, dskip, normw, name):
    def fn(yb, xcb, zb, db, nb):
        return _ssd_post(yb, xcb[:, :SSD_INNER], zb, db, nb)
    return rowcall(name, fn, [y, xc, z], [dskip, normw], [(SSD_INNER, BF16)])[0]


def ssd_post_bwd(y, xc, z, dskip, normw, dyn, name):
    def fn(yb, xcb, zb, dynb, db, nb):
        _, vjp = jax.vjp(_ssd_post, yb, xcb[:, :SSD_INNER], zb, db, nb)
        return vjp(dynb)
    return rowcall(name, fn, [y, xc, z, dyn], [dskip, normw],
                   [(SSD_INNER, F32), (SSD_INNER, F32), (SSD_INNER, BF16)], [(1, SSD_INNER), (1, SSD_INNER)])


def _rope(t, cosf, sina, sinb):
    return t * cosf + pltpu.roll(t, LANES - ROPE_DIM // 2, 1) * sina + pltpu.roll(t, ROPE_DIM // 2, 1) * sinb


def rope_tables():
    half = ROPE_DIM // 2
    inv = ROPE_THETA ** (-jnp.arange(0, ROPE_DIM, 2, dtype=F32) / ROPE_DIM)
    ang = jnp.arange(SEQ, dtype=F32)[:, None] * inv[None, :]
    cos, sin = jnp.cos(ang), jnp.sin(ang)
    zeros = jnp.zeros((SEQ, LANES - ROPE_DIM), F32)
    z16 = jnp.zeros((SEQ, half), F32)
    cosf = jnp.concatenate([cos, cos, jnp.ones((SEQ, LANES - ROPE_DIM), F32)], axis=1)
    sina = jnp.concatenate([-sin, z16, zeros], axis=1)
    sinb = jnp.concatenate([z16, sin, zeros], axis=1)
    return cosf, sina, sinb


CONV_TC = 256


def _conv_pre(x, w, b, row):
    acc = x * w[SSD_CONV - 1:SSD_CONV, :] + b
    shifted = [x]
    for j in range(1, SSD_CONV):
        xs = jnp.where(row >= j, pltpu.roll(x, j, 0), 0.0)
        shifted.append(xs)
        acc = acc + xs * w[SSD_CONV - 1 - j:SSD_CONV - j, :]
    return acc, shifted


def conv_fwd(xbc, w, b, name):
    def body(x_ref, w_ref, b_ref, o_ref):
        row = lax.broadcasted_iota(jnp.int32, (SEQ, CONV_TC), 0)
        pre, _ = _conv_pre(x_ref[...].astype(F32), w_ref[...], b_ref[...], row)
        o_ref[...] = _silu(pre)
    return pl.pallas_call(
        body, name=name, grid=(SSD_CONV_CH // CONV_TC,),
        in_specs=[pl.BlockSpec((SEQ, CONV_TC), lambda i: (0, i)), pl.BlockSpec((SSD_CONV, CONV_TC), lambda i: (0, i)),
                  pl.BlockSpec((1, CONV_TC), lambda i: (0, i))],
        out_specs=pl.BlockSpec((SEQ, CONV_TC), lambda i: (0, i)),
        out_shape=SDS((SEQ, SSD_CONV_CH), F32), compiler_params=_cparams(("parallel",)),
    )(xbc, w, b)


def conv_bwd(xbc, w, b, dxc, name):
    def body(x_ref, w_ref, b_ref, dy_ref, dx_ref, dw_ref, db_ref):
        row = lax.broadcasted_iota(jnp.int32, (SEQ, CONV_TC), 0)
        wv = w_ref[...]
        pre, shifted = _conv_pre(x_ref[...].astype(F32), wv, b_ref[...], row)
        sg = _sigmoid(pre)
        ds = dy_ref[...] * (sg * (1.0 + pre * (1.0 - sg)))
        dx = ds * wv[SSD_CONV - 1:SSD_CONV, :]
        for j in range(1, SSD_CONV):
            dsj = jnp.where(row < SEQ - j, pltpu.roll(ds, SEQ - j, 0), 0.0)
            dx = dx + dsj * wv[SSD_CONV - 1 - j:SSD_CONV - j, :]
        dx_ref[...] = dx.astype(dx_ref.dtype)
        for j in range(SSD_CONV):
            dw_ref[SSD_CONV - 1 - j:SSD_CONV - j, :] = jnp.sum(ds * shifted[j], axis=0, keepdims=True)
        db_ref[...] = jnp.sum(ds, axis=0, keepdims=True)
    return pl.pallas_call(
        body, name=name, grid=(SSD_CONV_CH // CONV_TC,),
        in_specs=[pl.BlockSpec((SEQ, CONV_TC), lambda i: (0, i)), pl.BlockSpec((SSD_CONV, CONV_TC), lambda i: (0, i)),
                  pl.BlockSpec((1, CONV_TC), lambda i: (0, i)), pl.BlockSpec((SEQ, CONV_TC), lambda i: (0, i))],
        out_specs=[pl.BlockSpec((SEQ, CONV_TC), lambda i: (0, i)), pl.BlockSpec((SSD_CONV, CONV_TC), lambda i: (0, i)),
                   pl.BlockSpec((1, CONV_TC), lambda i: (0, i))],
        out_shape=[SDS((SEQ, SSD_CONV_CH), BF16), SDS((SSD_CONV, SSD_CONV_CH), F32), SDS((1, SSD_CONV_CH), F32)],
        compiler_params=_cparams(("parallel",)),
    )(xbc, w, b, dxc)


N_CHUNKS = SEQ // SSD_CHUNK
N_PAIRS = SSD_HEADS // 2
PAIRS_PER_GROUP = N_PAIRS // SSD_GROUPS
B_OFF = SSD_INNER
C_OFF = SSD_INNER + SSD_GROUPS * SSD_STATE


def _ssd_prefix(dtr, dtr_t, dtb, dtb_t, alog, alog_t):
    ln = SSD_CHUNK
    dt = _softplus(dtr + dtb)
    dt_t = _softplus(dtr_t + dtb_t)
    dta = dt * (-jnp.exp(alog))
    dta_t = dt_t * (-jnp.exp(alog_t))
    r = lax.broadcasted_iota(jnp.int32, (ln, ln), 0)
    c = lax.broadcasted_iota(jnp.int32, (ln, ln), 1)
    a_cum = _dot((r >= c).astype(F32), dta, precision=HIGHEST)
    a_cum_t = _dot(dta_t, (r <= c).astype(F32), precision=HIGHEST)
    a_last = jnp.sum(dta_t, axis=1, keepdims=True)
    return dt, a_cum, a_cum_t, a_last


def _ssd_pair(x_pair, bg, cg, hp, dt, a_cum, a_cum_t, a_last, *, e0):
    ln = SSD_CHUNK
    lane = lax.broadcasted_iota(jnp.int32, (ln, LANES), 1)
    sub = lax.broadcasted_iota(jnp.int32, (LANES, SSD_STATE), 0)
    row = lax.broadcasted_iota(jnp.int32, (ln, ln), 0)
    col = lax.broadcasted_iota(jnp.int32, (ln, ln), 1)
    lo = lane < SSD_HEAD_DIM
    e1 = e0 + 1
    c0, c1 = a_cum[:, e0:e0 + 1], a_cum[:, e1:e1 + 1]
    r0, r1 = a_cum_t[e0:e0 + 1, :], a_cum_t[e1:e1 + 1, :]
    l0, l1 = a_last[e0:e0 + 1, :], a_last[e1:e1 + 1, :]
    xd = x_pair * jnp.where(lo, dt[:, e0:e0 + 1], dt[:, e1:e1 + 1])
    causal = row >= col
    cb = _bdot(cg, bg, NT)
    m0 = cb * jnp.exp(jnp.where(causal, c0 - r0, -jnp.inf))
    m1 = cb * jnp.exp(jnp.where(causal, c1 - r1, -jnp.inf))
    y = _bdot(m0, jnp.where(lo, xd, 0.0)) + _bdot(m1, jnp.where(lo, 0.0, xd))
    acum_pair = jnp.where(lo, c0, c1)
    y = y + _bdot(cg, hp, NT) * jnp.exp(acum_pair)
    last_pair = jnp.where(lo, l0, l1)
    st = _bdot(xd * jnp.exp(last_pair - acum_pair), bg, TN)
    h_out = hp * jnp.exp(jnp.where(sub < SSD_HEAD_DIM, l0, l1)) + st
    return y, h_out


def _ssd_in_specs(chunk_of):
    return [
        pl.BlockSpec((SSD_CHUNK, SSD_CONV_CH), lambda i: (chunk_of(i), 0)),
        pl.BlockSpec((SSD_CHUNK, HPAD), lambda i: (chunk_of(i), 0)),
        pl.BlockSpec((HPAD, SSD_CHUNK), lambda i: (0, chunk_of(i))),
        pl.BlockSpec((1, HPAD), lambda i: (0, 0)), pl.BlockSpec((HPAD, 1), lambda i: (0, 0)),
        pl.BlockSpec((1, HPAD), lambda i: (0, 0)), pl.BlockSpec((HPAD, 1), lambda i: (0, 0)),
    ]


def ssd_fwd(xc, dtr, dtr_t, dtb, dtb_t, alog, alog_t, name):
    def body(xc_ref, dtr_ref, dtrt_ref, dtb_ref, dtbt_ref, al_ref, alt_ref, y_ref, hs_ref, h_scr):
        @pl.when(pl.program_id(0) == 0)
        def _():
            h_scr[...] = jnp.zeros_like(h_scr)

        hs_ref[0] = h_scr[...]
        dt, a_cum, a_cum_t, a_last = _ssd_prefix(dtr_ref[...], dtrt_ref[...], dtb_ref[...], dtbt_ref[...],
                                                  al_ref[...], alt_ref[...])
        for pr in range(N_PAIRS):
            g = pr // PAIRS_PER_GROUP
            sl = slice(pr * LANES, (pr + 1) * LANES)
            bg = xc_ref[:, B_OFF + g * SSD_STATE:B_OFF + (g + 1) * SSD_STATE]
            cg = xc_ref[:, C_OFF + g * SSD_STATE:C_OFF + (g + 1) * SSD_STATE]
            y, h_out = _ssd_pair(xc_ref[:, sl], bg, cg, h_scr[sl, :], dt, a_cum, a_cum_t, a_last, e0=2 * pr)
            y_ref[:, sl] = y
            h_scr[sl, :] = h_out

    return pl.pallas_call(
        body, name=name, grid=(N_CHUNKS,), in_specs=_ssd_in_specs(lambda i: i),
        out_specs=[pl.BlockSpec((SSD_CHUNK, SSD_INNER), lambda i: (i, 0)),
                   pl.BlockSpec((1, SSD_INNER, SSD_STATE), lambda i: (i, 0, 0))],
        out_shape=[SDS((SEQ, SSD_INNER), F32), SDS((N_CHUNKS, SSD_INNER, SSD_STATE), F32)],
        scratch_shapes=[pltpu.VMEM((SSD_INNER, SSD_STATE), F32)],
        compiler_params=_cparams(("arbitrary",)),
    )(xc, dtr, dtr_t, dtb, dtb_t, alog, alog_t)


def ssd_bwd(xc, dtr, dtr_t, dtb, dtb_t, alog, alog_t, hs, dy, dxs_extra, name):
    rev = lambda i: N_CHUNKS - 1 - i

    def body(xc_ref, dtr_ref, dtrt_ref, dtb_ref, dtbt_ref, al_ref, alt_ref, hs_ref, dy_ref, dxe_ref,
             dxc_ref, ddtr_ref, ddtrt_ref, ddtb_ref, ddtbt_ref, dal_ref, dalt_ref, dh_scr):
        @pl.when(pl.program_id(0) == 0)
        def _():
            dh_scr[...] = jnp.zeros_like(dh_scr)
            for r in (ddtb_ref, ddtbt_ref, dal_ref, dalt_ref):
                r[...] = jnp.zeros_like(r)

        prefix_in = (dtr_ref[...], dtrt_ref[...], dtb_ref[...], dtbt_ref[...], al_ref[...], alt_ref[...])
        (dt, a_cum, a_cum_t, a_last), prefix_vjp = jax.vjp(_ssd_prefix, *prefix_in)
        d_dt = jnp.zeros_like(dt)
        d_acum = jnp.zeros_like(a_cum)
        d_acum_t = jnp.zeros_like(a_cum_t)
        d_alast = jnp.zeros_like(a_last)
        for g in range(SSD_GROUPS):
            bg = xc_ref[:, B_OFF + g * SSD_STATE:B_OFF + (g + 1) * SSD_STATE]
            cg = xc_ref[:, C_OFF + g * SSD_STATE:C_OFF + (g + 1) * SSD_STATE]
            d_bg = jnp.zeros_like(bg)
            d_cg = jnp.zeros_like(cg)
            for j in range(PAIRS_PER_GROUP):
                pr = g * PAIRS_PER_GROUP + j
                sl = slice(pr * LANES, (pr + 1) * LANES)
                _, vjp = jax.vjp(functools.partial(_ssd_pair, e0=2 * pr),
                                 xc_ref[:, sl], bg, cg, hs_ref[0, sl, :], dt, a_cum, a_cum_t, a_last)
                dx, dbg, dcg, dhp, ddt, dac, dact, dal = vjp((dy_ref[:, sl], dh_scr[sl, :]))
                dxc_ref[:, sl] = dx + dxe_ref[:, sl]
                dh_scr[sl, :] = dhp
                d_bg, d_cg = d_bg + dbg, d_cg + dcg
                d_dt, d_acum, d_acum_t, d_alast = d_dt + ddt, d_acum + dac, d_acum_t + dact, d_alast + dal
            dxc_ref[:, B_OFF + g * SSD_STATE:B_OFF + (g + 1) * SSD_STATE] = d_bg
            dxc_ref[:, C_OFF + g * SSD_STATE:C_OFF + (g + 1) * SSD_STATE] = d_cg
        g_dtr, g_dtrt, g_dtb, g_dtbt, g_al, g_alt = prefix_vjp((d_dt, d_acum, d_acum_t, d_alast))
        ddtr_ref[...] = g_dtr
        ddtrt_ref[...] = g_dtrt
        ddtb_ref[...] += g_dtb
        ddtbt_ref[...] += g_dtbt
        dal_ref[...] += g_al
        dalt_ref[...] += g_alt

    in_specs = _ssd_in_specs(rev) + [
        pl.BlockSpec((1, SSD_INNER, SSD_STATE), lambda i: (rev(i), 0, 0)),
        pl.BlockSpec((SSD_CHUNK, SSD_INNER), lambda i: (rev(i), 0)),
        pl.BlockSpec((SSD_CHUNK, SSD_INNER), lambda i: (rev(i), 0)),
    ]
    out_specs = [
        pl.BlockSpec((SSD_CHUNK, SSD_CONV_CH), lambda i: (rev(i), 0)),
        pl.BlockSpec((SSD_CHUNK, HPAD), lambda i: (rev(i), 0)),
        pl.BlockSpec((HPAD, SSD_CHUNK), lambda i: (0, rev(i))),
        pl.BlockSpec((1, HPAD), lambda i: (0, 0)), pl.BlockSpec((HPAD, 1), lambda i: (0, 0)),
        pl.BlockSpec((1, HPAD), lambda i: (0, 0)), pl.BlockSpec((HPAD, 1), lambda i: (0, 0)),
    ]
    out_shape = [SDS((SEQ, SSD_CONV_CH), F32), SDS((SEQ, HPAD), F32), SDS((HPAD, SEQ), F32),
                 SDS((1, HPAD), F32), SDS((HPAD, 1), F32), SDS((1, HPAD), F32), SDS((HPAD, 1), F32)]
    return pl.pallas_call(
        body, name=name, grid=(N_CHUNKS,), in_specs=in_specs, out_specs=out_specs, out_shape=out_shape,
        scratch_shapes=[pltpu.VMEM((SSD_INNER, SSD_STATE), F32)],
        compiler_params=_cparams(("arbitrary",)),
    )(xc, dtr, dtr_t, dtb, dtb_t, alog, alog_t, hs, dy, dxs_extra)


ATTN_SCALE = ATTN_HEAD_DIM ** -0.5


def _attn_scores(q, kp, kc, has_prev):
    qi = lax.broadcasted_iota(jnp.int32, (ATTN_BLOCK, ATTN_BLOCK), 0)
    kj = lax.broadcasted_iota(jnp.int32, (ATTN_BLOCK, ATTN_BLOCK), 1)
    s_c = jnp.where(qi >= kj, _bdot(q, kc, NT) * ATTN_SCALE, -jnp.inf)
    s_p = jnp.where((kj >= qi) & has_prev, _bdot(q, kp, NT) * ATTN_SCALE, -jnp.inf)
    return s_p, s_c


UNITS_PER_PATTERN = SEQ // ATTN_BLOCK
ATTN_UNROLL = 2


def _for_units(unit):
    for g, d in enumerate(ATTN_DILATIONS):
        nb = UNITS_PER_PATTERN // d
        span = d * ATTN_BLOCK

        def one(i, carry, g=g, d=d, nb=nb, span=span):
            r = i >> (nb.bit_length() - 1)
            n = i & (nb - 1)
            start = r + n * span
            prev = jnp.where(n > 0, start - span, start)
            unit(g, pl.ds(start, ATTN_BLOCK, stride=d), pl.ds(prev, ATTN_BLOCK, stride=d), n > 0)
            return carry
        lax.fori_loop(0, UNITS_PER_PATTERN, one, 0, unroll=ATTN_UNROLL)


def _head_specs(n_q_groups):
    blk = (SEQ, ATTN_HEAD_DIM)
    q_specs = [pl.BlockSpec(blk, functools.partial(lambda h, g: (0, g * ATTN_KV_HEADS + h), g=g)) for g in range(n_q_groups)]
    head = pl.BlockSpec(blk, lambda h: (0, h))
    table = pl.BlockSpec(blk, lambda h: (0, 0))
    return q_specs, head, table


def attn_fwd(q, k, v, tabs, name):
    q_specs, head, table = _head_specs(ATTN_N_PAT)

    def body(q0_ref, q1_ref, q2_ref, k_ref, v_ref, c_ref, sa_ref, sb_ref, y_ref, lse_ref, *scr):
        qs, og, ls, ks, vs = scr[0:3], scr[3:6], scr[6:9], scr[9], scr[10]
        c, sa, sb = c_ref[...], sa_ref[...], sb_ref[...]
        for g, q_ref in enumerate((q0_ref, q1_ref, q2_ref)):
            qs[g][...] = _rope(q_ref[...].astype(F32), c, sa, sb)
        ks[...] = _rope(k_ref[...].astype(F32), c, sa, sb)
        vs[...] = v_ref[...].astype(F32)

        def unit(g, rows, prows, has_prev):
            s_p, s_c = _attn_scores(qs[g][rows, :], ks[prows, :], ks[rows, :], has_prev)
            m = jnp.maximum(jnp.max(s_c, axis=1, keepdims=True), jnp.max(s_p, axis=1, keepdims=True))
            p_c, p_p = jnp.exp(s_c - m), jnp.exp(s_p - m)
            l = jnp.sum(p_c, axis=1, keepdims=True) + jnp.sum(p_p, axis=1, keepdims=True)
            o = _bdot(p_c, vs[rows, :]) + _bdot(p_p, vs[prows, :])
            og[g][rows, :] = o / l
            ls[g][rows, :] = jnp.broadcast_to(m + jnp.log(l), (ATTN_BLOCK, LANES))

        _for_units(unit)
        l0, l1, l2 = ls[0][...], ls[1][...], ls[2][...]
        m = jnp.maximum(jnp.maximum(l0, l1), l2)
        e0, e1, e2 = jnp.exp(l0 - m), jnp.exp(l1 - m), jnp.exp(l2 - m)
        den = e0 + e1 + e2
        y_ref[...] = ((e0 * og[0][...] + e1 * og[1][...] + e2 * og[2][...]) / den).astype(y_ref.dtype)
        lse_ref[...] = m + jnp.log(den)

    blk = (SEQ, ATTN_HEAD_DIM)
    return pl.pallas_call(
        body, name=name, grid=(ATTN_KV_HEADS,), in_specs=[*q_specs, head, head, table, table, table],
        out_specs=[head, head], out_shape=[SDS((SEQ, ATTN_OUT), BF16), SDS((SEQ, ATTN_OUT), F32)],
        scratch_shapes=[pltpu.VMEM(blk, F32)] * (3 * ATTN_N_PAT + 2),
        compiler_params=_cparams(("parallel",)),
    )(q, q, q, k, v, *tabs)


def attn_bwd(q, k, v, tabs, y, lse, dy, name):
    q_specs, head, table = _head_specs(ATTN_N_PAT)

    def body(q0_ref, q1_ref, q2_ref, k_ref, v_ref, c_ref, sa_ref, sb_ref, y_ref, lse_ref, dy_ref,
             dq0_ref, dq1_ref, dq2_ref, dk_ref, dv_ref, *scr):
        qs, dqs, ks, dks, dd, dvs, vs = scr[0:3], scr[3:6], scr[6], scr[7], scr[8], scr[9], scr[10]
        c, sa, sb = c_ref[...], sa_ref[...], sb_ref[...]
        for g, q_ref in enumerate((q0_ref, q1_ref, q2_ref)):
            qs[g][...] = _rope(q_ref[...].astype(F32), c, sa, sb)
        ks[...] = _rope(k_ref[...].astype(F32), c, sa, sb)
        vs[...] = v_ref[...].astype(F32)
        dks[...] = jnp.zeros_like(dks)
        dvs[...] = jnp.zeros_like(dvs)
        dyv = dy_ref[...]
        dd[...] = jnp.broadcast_to(jnp.sum(dyv * y_ref[...].astype(F32), axis=1, keepdims=True), dd.shape)

        def unit(g, rows, prows, has_prev):
            qv = qs[g][rows, :].astype(BF16)
            kc, kp = ks[rows, :].astype(BF16), ks[prows, :].astype(BF16)
            vc, vp = vs[rows, :].astype(BF16), vs[prows, :].astype(BF16)
            do = dy_ref[rows, :].astype(BF16)
            s_p, s_c = _attn_scores(qv, kp, kc, has_prev)
            lse_u = lse_ref[rows, :][:, 0:1]
            dsum = dd[rows, :][:, 0:1]
            p_c, p_p = jnp.exp(s_c - lse_u), jnp.exp(s_p - lse_u)
            ds_c = (p_c * (_dot(do, vc, NT) - dsum) * ATTN_SCALE).astype(BF16)
            ds_p = (p_p * (_dot(do, vp, NT) - dsum) * ATTN_SCALE).astype(BF16)
            dqs[g][rows, :] = _dot(ds_c, kc) + _dot(ds_p, kp)
            dks[rows, :] += _dot(ds_c, qv, TN)
            dks[prows, :] += _dot(ds_p, qv, TN)
            dvs[rows, :] += _bdot(p_c, do, TN)
            dvs[prows, :] += _bdot(p_p, do, TN)

        _for_units(unit)
        for g, dq_ref in enumerate((dq0_ref, dq1_ref, dq2_ref)):
            dq_ref[...] = _rope(dqs[g][...], c, -sa, -sb).astype(dq_ref.dtype)
        dk_ref[...] = _rope(dks[...], c, -sa, -sb).astype(dk_ref.dtype)
        dv_ref[...] = dvs[...].astype(dv_ref.dtype)

    blk = (SEQ, ATTN_HEAD_DIM)
    out = SDS((SEQ, ATTN_OUT), BF16)
    return pl.pallas_call(
        body, name=name, grid=(ATTN_KV_HEADS,), in_specs=[*q_specs, head, head, table, table, table, head, head, head],
        out_specs=[head] * 5, out_shape=[out] * 5,
        scratch_shapes=[pltpu.VMEM(blk, F32)] * (2 * ATTN_N_PAT + 5),
        compiler_params=_cparams(("parallel",)),
    )(q, q, q, k, v, *tabs, y, lse, dy)


def layer_fwd(h, getw, small, tabs, li):
    n = f"l{li}_"
    sv = {}
    w = dict(getw(0, h))
    u = rms_fwd(h, small["norm_mix"], n + "rms_mix")
    z = matmul(u, w["w_z"], name=n + "mm_z", tb=True, out_dtype=BF16)
    xbc = matmul(u, w["w_xbc"], name=n + "mm_xbc", tb=True, out_dtype=BF16)
    dtr = matmul(u, w["w_dt"], name=n + "mm_dt", tb=True)
    q = matmul(u, w["w_q"], name=n + "mm_q", tb=True, out_dtype=BF16)
    k = matmul(u, w["w_k"], name=n + "mm_k", tb=True, out_dtype=BF16)
    v = matmul(u, w["w_v"], name=n + "mm_v", tb=True, out_dtype=BF16)
    gs = matmul(u, w["w_gs"], name=n + "mm_gs", tb=True, out_dtype=BF16)
    ga = matmul(u, w["w_ga"], name=n + "mm_ga", tb=True, out_dtype=BF16)
    xc = conv_fwd(xbc, w["conv_w"], small["conv_b"], n + "conv")
    dtr_t = dtr.T
    y_ssd, hs = ssd_fwd(xc, dtr, dtr_t, small["dt_bias"], small["dt_bias"].T, small["a_log"], small["a_log"].T, n + "ssd")
    yn = ssd_post_fwd(y_ssd, xc, z, small["d_skip_x"], small["ssd_norm"], n + "ssd_post")
    y_attn, lse = attn_fwd(q, k, v, tabs, n + "attn")
    w.update(getw(1, y_ssd))
    a = matmul(yn, w["w_ssd_branch"], name=n + "mm_a", out_dtype=BF16)
    b = matmul(y_attn, w["w_attn_branch"], name=n + "mm_b", out_dtype=BF16)
    merged = gate_fwd(a, b, gs, ga, n + "gate")
    h1 = matmul(merged, w["w_out"], name=n + "mm_o", add=h)
    w.update(getw(2, h1))
    u2 = rms_fwd(h1, small["norm_ffn"], n + "rms_ffn")
    gu = matmul(u2, w["w_gate_up"], name=n + "mm_gu", tb=True, out_dtype=BF16)
    act = swiglu_fwd(gu, n + "swiglu")
    h2 = matmul(act, w["w_down"], name=n + "mm_down", add=h1)
    sv.update(h=h, u=u, z=z, xbc=xbc, dtr=dtr, dtr_t=dtr_t, gs=gs, ga=ga, xc=xc, y_ssd=y_ssd, hs=hs, yn=yn,
              q=q, k=k, v=v, y_attn=y_attn, lse=lse, a=a, b=b, merged=merged, h1=h1, u2=u2, gu=gu, act=act, w=w)
    return h2, sv


def layer_bwd(dh, sv, small, tabs, li, emit):
    n = f"l{li}_b_"
    w = sv["w"]
    gw, gsm = {}, {}
    dact = matmul(dh, w["w_down"], name=n + "mm_dact", tb=True, out_dtype=BF16)
    gw["w_down"] = matmul(sv["act"], dh, name=n + "mm_dwdown", ta=True, out_dtype=BF16)
    dgu = swiglu_bwd(sv["gu"], dact, n + "swiglu")
    gw["w_gate_up"] = matmul(dgu, sv["u2"], name=n + "mm_dwgu", ta=True, out_dtype=BF16)
    tok = emit(2, gw)
    du2 = matmul(dgu, w["w_gate_up"], name=n + "mm_du2")
    dh1, gsm["norm_ffn"] = rms_bwd(sv["h1"], du2, dh, small["norm_ffn"] + tok, n + "rms_ffn")
    dmerged = matmul(dh1, w["w_out"], name=n + "mm_dmerged", tb=True)
    gw["w_out"] = matmul(sv["merged"], dh1, name=n + "mm_dwo", ta=True, out_dtype=BF16)
    da, db, dgs, dga = gate_bwd(sv["a"], sv["b"], sv["gs"], sv["ga"], dmerged, n + "gate")
    gw["w_ssd_branch"] = matmul(sv["yn"], da, name=n + "mm_dwa", ta=True, out_dtype=BF16)
    gw["w_attn_branch"] = matmul(sv["y_attn"], db, name=n + "mm_dwb", ta=True, out_dtype=BF16)
    tok = emit(1, gw)
    dyn = matmul(da, w["w_ssd_branch"], name=n + "mm_dyn", tb=True)
    dyattn = matmul(db, w["w_attn_branch"], name=n + "mm_dyattn", tb=True)
    dy_ssd, dxs_extra, dz, gsm["d_skip_x"], gsm["ssd_norm"] = ssd_post_bwd(
        sv["y_ssd"], sv["xc"], sv["z"], small["d_skip_x"] + tok, small["ssd_norm"], dyn, n + "ssd_post")
    dxc, ddtr, ddtr_t, ddtb, ddtb_t, dal, dal_t = ssd_bwd(
        sv["xc"], sv["dtr"], sv["dtr_t"], small["dt_bias"], small["dt_bias"].T, small["a_log"], small["a_log"].T,
        sv["hs"], dy_ssd, dxs_extra, n + "ssd")
    ddtr = (ddtr + ddtr_t.T).astype(BF16)
    gsm["dt_bias"] = ddtb + ddtb_t.T
    gsm["a_log"] = dal + dal_t.T
    dxbc, gw["conv_w"], gsm["conv_b"] = conv_bwd(sv["xbc"], w["conv_w"], small["conv_b"], dxc, n + "conv")
    dq0, dq1, dq2, dk, dv = attn_bwd(sv["q"], sv["k"], sv["v"], tabs, sv["y_attn"], sv["lse"], dyattn, n + "attn")
    u = sv["u"]
    segs = [("w_z", dz), ("w_xbc", dxbc), ("w_dt", ddtr), ("w_q0", dq0), ("w_q1", dq1), ("w_q2", dq2),
            ("w_k", dk), ("w_v", dv), ("w_gs", dgs), ("w_ga", dga)]
    gin = [matmul(dseg, u, name=n + "mm_d" + key, ta=True, out_dtype=BF16) for key, dseg in segs]
    gin[2] = gin[2][:SSD_HEADS]
    gw["w_in"] = jnp.concatenate(gin, axis=0)
    tok = emit(0, gw)
    du = None
    for key, dseg in segs:
        du = matmul(dseg, w[key], name=n + "mm_du_" + key, add=du)
    dh0, gsm["norm_mix"] = rms_bwd(sv["h"], du, dh1, small["norm_mix"] + tok, n + "rms_mix")
    return dh0, gsm


def _my_place():
    return lax.axis_index("x"), lax.axis_index("y"), lax.axis_index("c")


def _flip(place, k):
    x, y, c = place
    return (1 - x if k & 4 else x, 1 - y if k & 2 else y, 1 - c if k & 1 else c)


def _index(place):
    return 4 * place[0] + 2 * place[1] + place[2]


ANY = pl.BlockSpec(memory_space=pl.ANY)
CHIP_FLIPS = (4, 2, 6)


def all_gather(xs, name):
    na = len(xs)

    def body(*refs):
        x_refs, o_refs = refs[:na], refs[na:2 * na]
        send_sems, recv_sems, local_sems = refs[2 * na:]
        me = _my_place()
        sibling = _flip(me, 1)
        chips = [_flip(me, f) for f in CHIP_FLIPS]

        def copy(a, kk, block, to, src=None):
            dst = o_refs[a].at[_index(block)]
            return pltpu.make_async_remote_copy(
                src_ref=dst if src is None else src, dst_ref=dst, send_sem=send_sems.at[a, kk],
                recv_sem=recv_sems.at[a, kk], device_id=to, device_id_type=MESH)

        mine = [pltpu.make_async_copy(x_refs[a], o_refs[a].at[_index(me)], local_sems.at[a]) for a in range(na)]
        for cp in mine:
            cp.start()
        first = []
        for j, chip in enumerate(chips):
            first += [copy(a, 1 + j, me, chip, src=x_refs[a]) for a in range(na)]
        first += [copy(a, 0, me, sibling, src=x_refs[a]) for a in range(na)]
        for cp in first:
            cp.start()
        passed = []
        for j, chip in enumerate(chips):
            for a in range(na):
                copy(a, 1 + j, chip, me).wait_recv()
                cp = copy(a, 4 + j, chip, sibling)
                cp.start()
                passed.append(cp)
        for a in range(na):
            copy(a, 0, sibling, me).wait_recv()
        for j, chip in enumerate(chips):
            for a in range(na):
                copy(a, 4 + j, _flip(chip, 1), me).wait_recv()
        for cp in first + passed:
            cp.wait_send()
        for cp in mine:
            cp.wait()

    return pl.pallas_call(
        body, name=name, in_specs=[ANY] * na, out_specs=[ANY] * na,
        out_shape=[SDS((N_DEV,) + t.shape, t.dtype) for t in xs],
        scratch_shapes=[pltpu.SemaphoreType.DMA((na, N_DEV - 1)), pltpu.SemaphoreType.DMA((na, N_DEV - 1)),
                        pltpu.SemaphoreType.DMA((na,))],
    )(*xs)


HBM = pl.BlockSpec(memory_space=pltpu.HBM)
SEM = pl.BlockSpec(memory_space=pltpu.SEMAPHORE)
EFFECT = pltpu.SideEffectType.DATAFLOW_SIDE_EFFECTING
N_PEERS = N_DEV - 1


def _split_copy(src_ref, land_ref, send_sem, recv_sem, me, kk, scatter, landed_from_peer):
    peer = _flip(me, kk)
    src = src_ref.at[_index(peer)] if scatter else src_ref
    dst = land_ref.at[_index(peer if landed_from_peer else me)]
    return pltpu.make_async_remote_copy(src_ref=src, dst_ref=dst, send_sem=send_sem, recv_sem=recv_sem,
                                        device_id=peer, device_id_type=MESH)


def exchange_start(srcs, lands, group_sizes, scatter, name):
    na, ng = len(srcs), len(group_sizes)

    def body(*refs):
        s_refs, l_refs = refs[:na], refs[na:2 * na]
        sems = refs[2 * na:2 * na + 2 * ng]
        token = refs[-1]
        me = _my_place()
        a = 0
        for gi, gsz in enumerate(group_sizes):
            for j in range(gsz):
                for kk in range(1, N_DEV):
                    slot = j * N_PEERS + kk - 1
                    _split_copy(s_refs[a], l_refs[a], sems[2 * gi].at[slot], sems[2 * gi + 1].at[slot],
                                me, kk, scatter, False).start()
                a += 1
        token[...] = jnp.zeros_like(token)

    sem_shapes = []
    for gsz in group_sizes:
        sem_shapes += [pltpu.SemaphoreType.DMA((gsz * N_PEERS,))] * 2
    ins = [pltpu.with_memory_space_constraint(t, pltpu.HBM) for t in (*srcs, *lands)]
    res = pl.pallas_call(
        body, name=name, in_specs=[HBM] * (2 * na),
        out_specs=[SEM] * (2 * ng) + [HBM] * (2 * na) + [pl.BlockSpec(memory_space=pltpu.VMEM)],
        out_shape=sem_shapes + [pltpu.HBM(t.shape, t.dtype) for t in ins] + [SDS((8, LANES), F32)],
        input_output_aliases={i: 2 * ng + i for i in range(2 * na)},
        compiler_params=pltpu.CompilerParams(has_side_effects=EFFECT),
    )(*ins)
    sems = [(res[2 * gi], res[2 * gi + 1]) for gi in range(ng)]
    thru = res[2 * ng:2 * ng + 2 * na]
    return sems, thru[:na], thru[na:], res[-1]


def exchange_wait(srcs, lands, sems, after, scatter, name):
    n = len(srcs)

    def body(*refs):
        s_refs, l_refs = refs[:n], refs[n:2 * n]
        send_sems, recv_sems = refs[2 * n], refs[2 * n + 1]
        me = _my_place()
        for j in range(n):
            for kk in range(1, N_DEV):
                slot = j * N_PEERS + kk - 1
                cp = _split_copy(s_refs[j], l_refs[j], send_sems.at[slot], recv_sems.at[slot], me, kk, scatter, True)
                cp.wait_send()
                cp.wait_recv()

    res = pl.pallas_call(
        body, name=name, in_specs=[HBM] * (2 * n) + [SEM, SEM, ANY], out_specs=[HBM] * (2 * n),
        out_shape=[pltpu.HBM(t.shape, t.dtype) for t in (*srcs, *lands)],
        input_output_aliases={i: i for i in range(2 * n)},
        compiler_params=pltpu.CompilerParams(has_side_effects=EFFECT),
    )(*srcs, *lands, sems[0], sems[1], after)
    return res[n:]


def landing_zone(block, me_index):
    land = lax.empty((N_DEV,) + block.shape, block.dtype)
    return lax.dynamic_update_slice(land, block[None], (me_index,) + (0,) * block.ndim)


def sum_parts(parts, name):
    _, r, c = parts.shape
    tc = _pick(c, (256, 128))

    def body(p_ref, o_ref):
        acc = p_ref[0].astype(F32)
        for i in range(1, N_DEV):
            acc = acc + p_ref[i].astype(F32)
        o_ref[...] = acc

    return pl.pallas_call(
        body, name=name, grid=(c // tc,), in_specs=[pl.BlockSpec((N_DEV, r, tc), lambda i: (0, 0, i))],
        out_specs=pl.BlockSpec((r, tc), lambda i: (0, i)), out_shape=SDS((r, c), F32),
        compiler_params=_cparams(("parallel",)),
    )(parts)


def adamw(w, g, m, v, name):
    shape = w.shape
    cols = shape[-1]
    rows = w.size // cols
    tr = _pick(rows, (256, 128, 64, 32, 16, 8))
    c1 = 1.0 / (1.0 - ADAM_B1 ** ADAM_STEP)
    c2 = 1.0 / (1.0 - ADAM_B2 ** ADAM_STEP)

    def body(w_ref, g_ref, m_ref, v_ref, d_ref, nm_ref, nv_ref):
        gg = g_ref[...]
        nm = ADAM_B1 * m_ref[...] + (1.0 - ADAM_B1) * gg
        nv = ADAM_B2 * v_ref[...] + (1.0 - ADAM_B2) * (gg * gg)
        d_ref[...] = -ADAM_LR * ((nm * c1) / (jnp.sqrt(nv * c2) + ADAM_EPS) + ADAM_WD * w_ref[...])
        nm_ref[...] = nm
        nv_ref[...] = nv

    spec = pl.BlockSpec((tr, cols), lambda i: (i, 0))
    outs = pl.pallas_call(
        body, name=name, grid=(rows // tr,), in_specs=[spec] * 4, out_specs=[spec] * 3,
        out_shape=[SDS((rows, cols), F32)] * 3, compiler_params=_cparams(("parallel",)),
    )(*[t.reshape(rows, cols) for t in (w, g, m, v)])
    return [o.reshape(shape) for o in outs]


BIG = ("w_in", "conv_w", "w_ssd_branch", "w_attn_branch", "w_out", "w_gate_up", "w_down")
TRANSPOSED = ("w_in", "w_gate_up")
SMALL = ("norm_mix", "conv_b", "dt_bias", "a_log", "d_skip", "ssd_norm", "norm_ffn")
SMALL_SIZE = {"norm_mix": 1024, "conv_b": 3072, "dt_bias": 32, "a_log": 32, "d_skip": 32, "ssd_norm": 2048, "norm_ffn": 1024}
FLAT_W = 512
SMALL_TOTAL = DEPTH * sum(SMALL_SIZE.values()) + D_MODEL + LANES
SMALL_ROWS = 32
assert SMALL_ROWS * FLAT_W >= SMALL_TOTAL


GROUPS = (("w_in", "conv_w"), ("w_ssd_branch", "w_attn_branch", "w_out"), ("w_gate_up", "w_down"))


def to_wire(k, shard):
    if k in TRANSPOSED:
        return shard.T.astype(BF16)
    return shard if k == "conv_w" else shard.astype(BF16)


def full_weights(k, g):
    if k == "conv_w":
        return {k: g.transpose(1, 0, 2).reshape(SSD_CONV, SSD_CONV_CH)}
    full = g.reshape(-1, g.shape[-1])
    if k != "w_in":
        return {k: full}
    w, off = {}, 0
    for nm, r in IN_ROWS:
        w[nm] = full[off:off + r]
        off += r
    w["w_q"] = full[sum(r for _, r in IN_ROWS[:3]):sum(r for _, r in IN_ROWS[:6])]
    w["w_dt"] = jnp.pad(w["w_dt"], ((0, HPAD - SSD_HEADS), (0, 0)))
    return w


def grads_to_wire(k, g):
    if k == "conv_w":
        return g.reshape(SSD_CONV, N_DEV, SSD_CONV_CH // N_DEV).transpose(1, 0, 2)
    return g.reshape(N_DEV, g.shape[0] // N_DEV, g.shape[1])


def _pad_heads(t):
    return jnp.pad(t.reshape(1, SSD_HEADS), ((0, 0), (0, HPAD - SSD_HEADS)))


def local_step(x, target, getw, emit, smalls, norm_final):
    tabs = rope_tables()
    sms = []
    for li in range(DEPTH):
        s = smalls[li]
        sms.append({
            "norm_mix": s["norm_mix"].reshape(1, -1), "conv_b": s["conv_b"].reshape(1, -1),
            "dt_bias": _pad_heads(s["dt_bias"]), "a_log": _pad_heads(s["a_log"]),
            "d_skip_x": jnp.repeat(s["d_skip"], SSD_HEAD_DIM).reshape(1, -1),
            "ssd_norm": s["ssd_norm"].reshape(1, -1), "norm_ffn": s["norm_ffn"].reshape(1, -1)})
    h = x
    saved = []
    for li in range(DEPTH):
        h, sv = layer_fwd(h, functools.partial(getw, li), sms[li], tabs, li)
        saved.append(sv)
    dh, g_final, loss = loss_head(h, target, norm_final.reshape(1, -1), "loss_head")
    gsms = [None] * DEPTH
    for li in reversed(range(DEPTH)):
        dh, gsm = layer_bwd(dh, saved[li], sms[li], tabs, li, functools.partial(emit, li))
        gsms[li] = {
            "norm_mix": gsm["norm_mix"].reshape(-1), "conv_b": gsm["conv_b"].reshape(-1),
            "dt_bias": gsm["dt_bias"][0, :SSD_HEADS], "a_log": gsm["a_log"][0, :SSD_HEADS],
            "d_skip": gsm["d_skip_x"].reshape(SSD_HEADS, SSD_HEAD_DIM).sum(axis=1),
            "ssd_norm": gsm["ssd_norm"].reshape(-1), "norm_ffn": gsm["norm_ffn"].reshape(-1)}
    return loss, dh, gsms, g_final.reshape(-1)


def kernel(x, norm_mix, w_in, conv_w, conv_b, dt_bias, a_log, d_skip, ssd_norm, w_ssd_branch, w_attn_branch, w_out, norm_ffn, w_gate_up, w_down, norm_final, loss_target, m_norm_mix, m_w_in, m_conv_w, m_conv_b, m_dt_bias, m_a_log, m_d_skip, m_ssd_norm, m_w_ssd_branch, m_w_attn_branch, m_w_out, m_norm_ffn, m_w_gate_up, m_w_down, m_norm_final, v_norm_mix, v_w_in, v_conv_w, v_conv_b, v_dt_bias, v_a_log, v_d_skip, v_ssd_norm, v_w_ssd_branch, v_w_attn_branch, v_w_out, v_norm_ffn, v_w_gate_up, v_w_down, v_norm_final):
    wv = dict(norm_mix=norm_mix, w_in=w_in, conv_w=conv_w, conv_b=conv_b, dt_bias=dt_bias, a_log=a_log, d_skip=d_skip,
              ssd_norm=ssd_norm, w_ssd_branch=w_ssd_branch, w_attn_branch=w_attn_branch, w_out=w_out, norm_ffn=norm_ffn,
              w_gate_up=w_gate_up, w_down=w_down, norm_final=norm_final)
    mv = dict(norm_mix=m_norm_mix, w_in=m_w_in, conv_w=m_conv_w, conv_b=m_conv_b, dt_bias=m_dt_bias, a_log=m_a_log,
              d_skip=m_d_skip, ssd_norm=m_ssd_norm, w_ssd_branch=m_w_ssd_branch, w_attn_branch=m_w_attn_branch,
              w_out=m_w_out, norm_ffn=m_norm_ffn, w_gate_up=m_w_gate_up, w_down=m_w_down, norm_final=m_norm_final)
    vv = dict(norm_mix=v_norm_mix, w_in=v_w_in, conv_w=v_conv_w, conv_b=v_conv_b, dt_bias=v_dt_bias, a_log=v_a_log,
              d_skip=v_d_skip, ssd_norm=v_ssd_norm, w_ssd_branch=v_w_ssd_branch, w_attn_branch=v_w_attn_branch,
              w_out=v_w_out, norm_ffn=v_norm_ffn, w_gate_up=v_w_gate_up, w_down=v_w_down, norm_final=v_norm_final)
    order = ("norm_mix", "w_in", "conv_w", "conv_b", "dt_bias", "a_log", "d_skip", "ssd_norm", "w_ssd_branch",
             "w_attn_branch", "w_out", "norm_ffn", "w_gate_up", "w_down", "norm_final")

    me_index = _index(_my_place())
    smalls = [{k: wv[k][li] for k in SMALL} for li in range(DEPTH)]
    n_groups = len(GROUPS)

    first_lands = all_gather([to_wire(k, wv[k][0]) for k in GROUPS[0]], "gather_first")
    later = [(li, gi) for li in range(DEPTH) for gi in range(n_groups)][1:]
    behind_first = first_lands[1][0, 0, 0] * 0.0
    srcs = [to_wire(k, wv[k][li] + behind_first if k == "conv_w" else wv[k][li]) for li, gi in later for k in GROUPS[gi]]
    sizes = [len(GROUPS[gi]) for _, gi in later]
    w_sems, w_srcs, w_lands, token = exchange_start(srcs, [landing_zone(s, me_index) for s in srcs], sizes, False, "gather_start")
    smalls[0]["norm_mix"] = smalls[0]["norm_mix"] + token[0, 0]

    def getw(li, gi, after):
        if (li, gi) == (0, 0):
            lands = first_lands
        else:
            slot = later.index((li, gi))
            sl = slice(sum(sizes[:slot]), sum(sizes[:slot + 1]))
            lands = exchange_wait(w_srcs[sl], w_lands[sl], w_sems[slot], after, False, f"gather_wait_{li}_{gi}")
        w = {}
        for k, land in zip(GROUPS[gi], lands):
            w.update(full_weights(k, land))
        return w

    pending = []

    def emit(li, gi, gw):
        parts = [grads_to_wire(k, gw[k]) for k in GROUPS[gi]]
        lands = [landing_zone(lax.dynamic_index_in_dim(p, me_index, 0, keepdims=False), me_index) for p in parts]
        sems, p_thru, l_thru, tok = exchange_start(parts, lands, [len(parts)], True, f"grads_start_{li}_{gi}")
        pending.append((li, gi, sems[0], p_thru, l_thru))
        return tok[0, 0]

    loss_p, dx, gsms, g_final = local_step(x[0], loss_target[0], getw, emit, smalls, norm_final)

    grads, deltas, new_m, new_v = {}, {}, {}, {}

    def update(k):
        w2, g2, m2, v2 = wv[k], grads[k], mv[k], vv[k]
        if w2.ndim == 1:
            w2, g2, m2, v2 = (t.reshape(1, -1) for t in (w2, g2, m2, v2))
        d, nm, nv = adamw(w2, g2, m2, v2, "adamw_" + k)
        deltas[k], new_m[k], new_v[k] = (t.reshape(wv[k].shape) for t in (d, nm, nv))
        return nv

    shard_g = {k: [None] * DEPTH for k in BIG}
    after = dx
    for li, gi, sems, p_thru, l_thru in pending:
        recv = exchange_wait(p_thru, l_thru, sems, after, True, f"grads_wait_{li}_{gi}")
        for k, r in zip(GROUPS[gi], recv):
            if k == "conv_w":
                r = r.reshape(N_DEV, 1, -1)
            after = sum_parts(r, f"sum_{k}_{li}")
            shard_g[k][li] = after.T if k in TRANSPOSED else after.reshape(wv[k].shape[1:])
        if li == 0:
            for k in GROUPS[gi]:
                grads[k] = jnp.stack(shard_g[k])
                after = update(k)

    flat = [gsms[li][k] for li in range(DEPTH) for k in SMALL] + [g_final, loss_p.reshape(-1)]
    flat.append(jnp.zeros((SMALL_ROWS * FLAT_W - SMALL_TOTAL,), F32))
    small_all = all_gather([jnp.concatenate(flat).reshape(SMALL_ROWS, FLAT_W)], "gather_small")[0]
    small_sum = sum_parts(small_all, "sum_small").reshape(-1)
    off = 0
    per_layer = {k: [] for k in SMALL}
    for li in range(DEPTH):
        for k in SMALL:
            per_layer[k].append(small_sum[off:off + SMALL_SIZE[k]])
            off += SMALL_SIZE[k]
    for k in SMALL:
        grads[k] = jnp.stack(per_layer[k])
    grads["norm_final"] = small_sum[off:off + D_MODEL]
    loss = small_sum[off + D_MODEL]
    for k in (*SMALL, "norm_final"):
        update(k)

    return (loss, dx.reshape(x.shape), *[grads[k] for k in order], *[deltas[k] for k in order],
            *[new_m[k] for k in order], *[new_v[k] for k in order])
```

```python
import functools

import jax
import jax.numpy as jnp
from jax import lax
from jax.experimental import pallas as pl
from jax.experimental.pallas import tpu as pltpu

F32, BF16 = jnp.float32, jnp.bfloat16
SDS = jax.ShapeDtypeStruct
MESH = pl.DeviceIdType.MESH

D_MODEL = 1024
SEQ = 2048
DEPTH = 2
RMS_EPS = 1e-5
SSD_INNER = 2048
SSD_HEAD_DIM = 64
SSD_HEADS = 32
SSD_STATE = 128
SSD_GROUPS = 4
SSD_CONV = 4
SSD_CHUNK = 128
SSD_CONV_CH = 3072
ATTN_HEAD_DIM = 128
ATTN_KV_HEADS = 8
ATTN_DILATIONS = (1, 4, 16)
ATTN_N_PAT = 3
ATTN_BLOCK = 128
ATTN_OUT = 1024
ROPE_THETA = 500000.0
ROPE_DIM = 32
FFN_HIDDEN = 2816
ADAM_LR, ADAM_B1, ADAM_B2, ADAM_EPS, ADAM_WD, ADAM_STEP = 0.001, 0.9, 0.999, 1e-08, 0.01, 10

N_DEV = 8
LANES = 128
VMEM_LIMIT = 56 * 1024 * 1024
HPAD = 128
HIGHEST = lax.Precision.HIGHEST

IN_ROWS = (("w_z", 2048), ("w_xbc", 3072), ("w_dt", 32), ("w_q0", 1024), ("w_q1", 1024), ("w_q2", 1024),
           ("w_k", 1024), ("w_v", 1024), ("w_gs", 1024), ("w_ga", 1024))
N_IN = sum(r for _, r in IN_ROWS)


def _cparams(sem):
    return pltpu.CompilerParams(dimension_semantics=sem, vmem_limit_bytes=VMEM_LIMIT)


def _sigmoid(x):
    return 0.5 * jnp.tanh(0.5 * x) + 0.5


def _silu(x):
    return x * _sigmoid(x)


def _softplus(x):
    return jnp.maximum(x, 0.0) + jnp.log(1.0 + jnp.exp(-jnp.abs(x)))


def _dot(a, b, dims=(((1,), (0,)), ((), ())), precision=None):
    return lax.dot_general(a, b, dims, precision=precision, preferred_element_type=F32)


NT = (((1,), (1,)), ((), ()))
TN = (((0,), (0,)), ((), ()))


def _bdot(a, b, dims=(((1,), (0,)), ((), ()))):
    return _dot(a.astype(BF16), b.astype(BF16), dims)


def _pick(dim, cands):
    for c in cands:
        if dim % c == 0:
            return c
    return dim


def matmul(a, b, *, name, ta=False, tb=False, out_dtype=F32, add=None):
    m, k = (a.shape[1], a.shape[0]) if ta else a.shape
    n = b.shape[0] if tb else b.shape[1]
    tm = _pick(m, (1024, 1408, 512, 256, 128))
    tn = _pick(n, (512, 256, 128))
    tk = _pick(k, (1024, 1408, 512, 256, 128))
    nk = k // tk
    a_spec = pl.BlockSpec((tk, tm), lambda i, j, kk: (kk, i)) if ta else pl.BlockSpec((tm, tk), lambda i, j, kk: (i, kk))
    b_spec = pl.BlockSpec((tn, tk), lambda i, j, kk: (j, kk)) if tb else pl.BlockSpec((tk, tn), lambda i, j, kk: (kk, j))
    dims = (((0 if ta else 1,), (1 if tb else 0,)), ((), ()))
    has_add = add is not None

    def body(*refs):
        a_ref, b_ref = refs[:2]
        add_ref = refs[2] if has_add else None
        o_ref = refs[3] if has_add else refs[2]
        acc = refs[-1] if nk > 1 else None
        kk = pl.program_id(2)

        def product():
            return _dot(a_ref[...].astype(BF16), b_ref[...].astype(BF16), dims)

        def finish(r):
            if has_add:
                r = r + add_ref[...].astype(F32)
            o_ref[...] = r.astype(o_ref.dtype)

        if nk == 1:
            finish(product())
            return

        @pl.when(kk == 0)
        def _():
            acc[...] = product()

        @pl.when((kk > 0) & (kk < nk - 1))
        def _():
            acc[...] += product()

        @pl.when(kk == nk - 1)
        def _():
            finish(acc[...] + product())

    in_specs = [a_spec, b_spec]
    args = [a, b]
    if has_add:
        in_specs.append(pl.BlockSpec((tm, tn), lambda i, j, kk: (i, j)))
        args.append(add)
    return pl.pallas_call(
        body, name=name, grid=(m // tm, n // tn, nk),
        in_specs=in_specs, out_specs=pl.BlockSpec((tm, tn), lambda i, j, kk: (i, j)),
        out_shape=SDS((m, n), out_dtype), scratch_shapes=[pltpu.VMEM((tm, tn), F32)] if nk > 1 else [],
        compiler_params=_cparams(("parallel", "parallel", "arbitrary")),
    )(*args)


def rowcall(name, fn, rows, params, row_outs, red_outs=(), tr=256):
    s = rows[0].shape[0]
    n_in = len(rows) + len(params)
    n_row = len(row_outs)

    def body(*refs):
        outs = fn(*[r[...].astype(F32) for r in refs[:n_in]])
        if not isinstance(outs, (tuple, list)):
            outs = (outs,)
        orefs = refs[n_in:]
        for r, o in zip(orefs[:n_row], outs[:n_row]):
            r[...] = o.astype(r.dtype)
        if red_outs:
            @pl.when(pl.program_id(0) == 0)
            def _():
                for r in orefs[n_row:]:
                    r[...] = jnp.zeros_like(r)
            for r, o in zip(orefs[n_row:], outs[n_row:]):
                r[...] += o.astype(F32)

    in_specs = [pl.BlockSpec((tr, a.shape[1]), lambda i: (i, 0)) for a in rows]
    in_specs += [pl.BlockSpec(p.shape, lambda i: (0, 0)) for p in params]
    out_specs = [pl.BlockSpec((tr, c), lambda i: (i, 0)) for c, _ in row_outs]
    out_specs += [pl.BlockSpec(shp, lambda i: (0, 0)) for shp in red_outs]
    out_shape = [SDS((s, c), dt) for c, dt in row_outs] + [SDS(shp, F32) for shp in red_outs]
    res = pl.pallas_call(
        body, name=name, grid=(s // tr,), in_specs=in_specs, out_specs=out_specs, out_shape=out_shape,
        compiler_params=_cparams(("arbitrary",) if red_outs else ("parallel",)),
    )(*rows, *params)
    return res


def _rms(x, w):
    return x * lax.rsqrt(jnp.mean(x * x, axis=-1, keepdims=True) + RMS_EPS) * w


def rms_fwd(h, w, name):
    return rowcall(name, _rms, [h], [w], [(D_MODEL, BF16)])[0]


def rms_bwd(h, du, dres, w, name):
    def fn(hb, dub, dresb, wb):
        _, vjp = jax.vjp(_rms, hb, wb)
        dh, dw = vjp(dub)
        return dh + dresb, dw
    return rowcall(name, fn, [h, du, dres], [w], [(D_MODEL, F32)], [(1, D_MODEL)])


def loss_head(h, target, w, name):
    def fn(hb, tb, wb):
        def f(hh, ww):
            err = _rms(hh, ww) - tb
            return 0.5 * jnp.sum(jnp.mean(err * err, axis=-1, keepdims=True), axis=0, keepdims=True)
        val, vjp = jax.vjp(f, hb, wb)
        dh, dw = vjp(jnp.ones((1, 1), F32))
        return dh, dw, jnp.broadcast_to(val, (1, LANES))
    return rowcall(name, fn, [h, target], [w], [(D_MODEL, F32)], [(1, D_MODEL), (1, LANES)])


def _gate(a, b, gs, ga):
    return _sigmoid(gs) * a + _sigmoid(ga) * b


def gate_fwd(a, b, gs, ga, name):
    return rowcall(name, _gate, [a, b, gs, ga], [], [(D_MODEL, BF16)])[0]


def gate_bwd(a, b, gs, ga, dm, name):
    def fn(ab, bb, gsb, gab, dmb):
        _, vjp = jax.vjp(_gate, ab, bb, gsb, gab)
        return vjp(dmb)
    return rowcall(name, fn, [a, b, gs, ga, dm], [], [(D_MODEL, BF16)] * 4)


def _swiglu(gu):
    return _silu(gu[:, :FFN_HIDDEN]) * gu[:, FFN_HIDDEN:]


def swiglu_fwd(gu, name):
    return rowcall(name, _swiglu, [gu], [], [(FFN_HIDDEN, BF16)])[0]


def swiglu_bwd(gu, dact, name):
    def fn(gub, db):
        _, vjp = jax.vjp(_swiglu, gub)
        return vjp(db.astype(F32))[0]
    return rowcall(name, fn, [gu, dact], [], [(2 * FFN_HIDDEN, BF16)])[0]


def _ssd_post(y, xs, z, dskip, normw):
    y = (y + dskip * xs) * _silu(z)
    gw = SSD_INNER // SSD_GROUPS
    parts = []
    for g in range(SSD_GROUPS):
        yg = y[:, g * gw:(g + 1) * gw]
        parts.append(yg * lax.rsqrt(jnp.mean(yg * yg, axis=-1, keepdims=True) + RMS_EPS))
    return jnp.concatenate(parts, axis=-1) * normw


def ssd_post_fwd(y, xc, z, dskip, normw, name):
    def fn(yb, xcb, zb, db, nb):
        return _ssd_post(yb, xcb[:, :SSD_INNER], zb, db, nb)
    return rowcall(name, fn, [y, xc, z], [dskip, normw], [(SSD_INNER, BF16)])[0]


def ssd_post_bwd(y, xc, z, dskip, normw, dyn, name):
    def fn(yb, xcb, zb, dynb, db, nb):
        _, vjp = jax.vjp(_ssd_post, yb, xcb[:, :SSD_INNER], zb, db, nb)
        return vjp(dynb)
    return rowcall(name, fn, [y, xc, z, dyn], [dskip, normw],
                   [(SSD_INNER, F32), (SSD_INNER, F32), (SSD_INNER, BF16)], [(1, SSD_INNER), (1, SSD_INNER)])


def _rope(t, cosf, sina, sinb):
    return t * cosf + pltpu.roll(t, LANES - ROPE_DIM // 2, 1) * sina + pltpu.roll(t, ROPE_DIM // 2, 1) * sinb


def rope_tables():
    half = ROPE_DIM // 2
    inv = ROPE_THETA ** (-jnp.arange(0, ROPE_DIM, 2, dtype=F32) / ROPE_DIM)
    ang = jnp.arange(SEQ, dtype=F32)[:, None] * inv[None, :]
    cos, sin = jnp.cos(ang), jnp.sin(ang)
    zeros = jnp.zeros((SEQ, LANES - ROPE_DIM), F32)
    z16 = jnp.zeros((SEQ, half), F32)
    cosf = jnp.concatenate([cos, cos, jnp.ones((SEQ, LANES - ROPE_DIM), F32)], axis=1)
    sina = jnp.concatenate([-sin, z16, zeros], axis=1)
    sinb = jnp.concatenate([z16, sin, zeros], axis=1)
    return cosf, sina, sinb


CONV_TC = 256


def _conv_pre(x, w, b, row):
    acc = x * w[SSD_CONV - 1:SSD_CONV, :] + b
    shifted = [x]
    for j in range(1, SSD_CONV):
        xs = jnp.where(row >= j, pltpu.roll(x, j, 0), 0.0)
        shifted.append(xs)
        acc = acc + xs * w[SSD_CONV - 1 - j:SSD_CONV - j, :]
    return acc, shifted


def conv_fwd(xbc, w, b, name):
    def body(x_ref, w_ref, b_ref, o_ref):
        row = lax.broadcasted_iota(jnp.int32, (SEQ, CONV_TC), 0)
        pre, _ = _conv_pre(x_ref[...].astype(F32), w_ref[...], b_ref[...], row)
        o_ref[...] = _silu(pre)
    return pl.pallas_call(
        body, name=name, grid=(SSD_CONV_CH // CONV_TC,),
        in_specs=[pl.BlockSpec((SEQ, CONV_TC), lambda i: (0, i)), pl.BlockSpec((SSD_CONV, CONV_TC), lambda i: (0, i)),
                  pl.BlockSpec((1, CONV_TC), lambda i: (0, i))],
        out_specs=pl.BlockSpec((SEQ, CONV_TC), lambda i: (0, i)),
        out_shape=SDS((SEQ, SSD_CONV_CH), F32), compiler_params=_cparams(("parallel",)),
    )(xbc, w, b)


def conv_bwd(xbc, w, b, dxc, name):
    def body(x_ref, w_ref, b_ref, dy_ref, dx_ref, dw_ref, db_ref):
        row = lax.broadcasted_iota(jnp.int32, (SEQ, CONV_TC), 0)
        wv = w_ref[...]
        pre, shifted = _conv_pre(x_ref[...].astype(F32), wv, b_ref[...], row)
        sg = _sigmoid(pre)
        ds = dy_ref[...] * (sg * (1.0 + pre * (1.0 - sg)))
        dx = ds * wv[SSD_CONV - 1:SSD_CONV, :]
        for j in range(1, SSD_CONV):
            dsj = jnp.where(row < SEQ - j, pltpu.roll(ds, SEQ - j, 0), 0.0)
            dx = dx + dsj * wv[SSD_CONV - 1 - j:SSD_CONV - j, :]
        dx_ref[...] = dx.astype(dx_ref.dtype)
        for j in range(SSD_CONV):
            dw_ref[SSD_CONV - 1 - j:SSD_CONV - j, :] = jnp.sum(ds * shifted[j], axis=0, keepdims=True)
        db_ref[...] = jnp.sum(ds, axis=0, keepdims=True)
    return pl.pallas_call(
        body, name=name, grid=(SSD_CONV_CH // CONV_TC,),
        in_specs=[pl.BlockSpec((SEQ, CONV_TC), lambda i: (0, i)), pl.BlockSpec((SSD_CONV, CONV_TC), lambda i: (0, i)),
                  pl.BlockSpec((1, CONV_TC), lambda i: (0, i)), pl.BlockSpec((SEQ, CONV_TC), lambda i: (0, i))],
        out_specs=[pl.BlockSpec((SEQ, CONV_TC), lambda i: (0, i)), pl.BlockSpec((SSD_CONV, CONV_TC), lambda i: (0, i)),
                   pl.BlockSpec((1, CONV_TC), lambda i: (0, i))],
        out_shape=[SDS((SEQ, SSD_CONV_CH), BF16), SDS((SSD_CONV, SSD_CONV_CH), F32), SDS((1, SSD_CONV_CH), F32)],
        compiler_params=_cparams(("parallel",)),
    )(xbc, w, b, dxc)


N_CHUNKS = SEQ // SSD_CHUNK
N_PAIRS = SSD_HEADS // 2
PAIRS_PER_GROUP = N_PAIRS // SSD_GROUPS
B_OFF = SSD_INNER
C_OFF = SSD_INNER + SSD_GROUPS * SSD_STATE


def _ssd_prefix(dtr, dtr_t, dtb, dtb_t, alog, alog_t):
    ln = SSD_CHUNK
    dt = _softplus(dtr + dtb)
    dt_t = _softplus(dtr_t + dtb_t)
    dta = dt * (-jnp.exp(alog))
    dta_t = dt_t * (-jnp.exp(alog_t))
    r = lax.broadcasted_iota(jnp.int32, (ln, ln), 0)
    c = lax.broadcasted_iota(jnp.int32, (ln, ln), 1)
    a_cum = _dot((r >= c).astype(F32), dta, precision=HIGHEST)
    a_cum_t = _dot(dta_t, (r <= c).astype(F32), precision=HIGHEST)
    a_last = jnp.sum(dta_t, axis=1, keepdims=True)
    return dt, a_cum, a_cum_t, a_last


def _ssd_pair(x_pair, bg, cg, hp, dt, a_cum, a_cum_t, a_last, *, e0):
    ln = SSD_CHUNK
    lane = lax.broadcasted_iota(jnp.int32, (ln, LANES), 1)
    sub = lax.broadcasted_iota(jnp.int32, (LANES, SSD_STATE), 0)
    row = lax.broadcasted_iota(jnp.int32, (ln, ln), 0)
    col = lax.broadcasted_iota(jnp.int32, (ln, ln), 1)
    lo = lane < SSD_HEAD_DIM
    e1 = e0 + 1
    c0, c1 = a_cum[:, e0:e0 + 1], a_cum[:, e1:e1 + 1]
    r0, r1 = a_cum_t[e0:e0 + 1, :], a_cum_t[e1:e1 + 1, :]
    l0, l1 = a_last[e0:e0 + 1, :], a_last[e1:e1 + 1, :]
    xd = x_pair * jnp.where(lo, dt[:, e0:e0 + 1], dt[:, e1:e1 + 1])
    causal = row >= col
    cb = _bdot(cg, bg, NT)
    m0 = cb * jnp.exp(jnp.where(causal, c0 - r0, -jnp.inf))
    m1 = cb * jnp.exp(jnp.where(causal, c1 - r1, -jnp.inf))
    y = _bdot(m0, jnp.where(lo, xd, 0.0)) + _bdot(m1, jnp.where(lo, 0.0, xd))
    acum_pair = jnp.where(lo, c0, c1)
    y = y + _bdot(cg, hp, NT) * jnp.exp(acum_pair)
    last_pair = jnp.where(lo, l0, l1)
    st = _bdot(xd * jnp.exp(last_pair - acum_pair), bg, TN)
    h_out = hp * jnp.exp(jnp.where(sub < SSD_HEAD_DIM, l0, l1)) + st
    return y, h_out


def _ssd_in_specs(chunk_of):
    return [
        pl.BlockSpec((SSD_CHUNK, SSD_CONV_CH), lambda i: (chunk_of(i), 0)),
        pl.BlockSpec((SSD_CHUNK, HPAD), lambda i: (chunk_of(i), 0)),
        pl.BlockSpec((HPAD, SSD_CHUNK), lambda i: (0, chunk_of(i))),
        pl.BlockSpec((1, HPAD), lambda i: (0, 0)), pl.BlockSpec((HPAD, 1), lambda i: (0, 0)),
        pl.BlockSpec((1, HPAD), lambda i: (0, 0)), pl.BlockSpec((HPAD, 1), lambda i: (0, 0)),
    ]


def ssd_fwd(xc, dtr, dtr_t, dtb, dtb_t, alog, alog_t, name):
    def body(xc_ref, dtr_ref, dtrt_ref, dtb_ref, dtbt_ref, al_ref, alt_ref, y_ref, hs_ref, h_scr):
        @pl.when(pl.program_id(0) == 0)
        def _():
            h_scr[...] = jnp.zeros_like(h_scr)

        hs_ref[0] = h_scr[...]
        dt, a_cum, a_cum_t, a_last = _ssd_prefix(dtr_ref[...], dtrt_ref[...], dtb_ref[...], dtbt_ref[...],
                                                  al_ref[...], alt_ref[...])
        for pr in range(N_PAIRS):
            g = pr // PAIRS_PER_GROUP
            sl = slice(pr * LANES, (pr + 1) * LANES)
            bg = xc_ref[:, B_OFF + g * SSD_STATE:B_OFF + (g + 1) * SSD_STATE]
            cg = xc_ref[:, C_OFF + g * SSD_STATE:C_OFF + (g + 1) * SSD_STATE]
            y, h_out = _ssd_pair(xc_ref[:, sl], bg, cg, h_scr[sl, :], dt, a_cum, a_cum_t, a_last, e0=2 * pr)
            y_ref[:, sl] = y
            h_scr[sl, :] = h_out

    return pl.pallas_call(
        body, name=name, grid=(N_CHUNKS,), in_specs=_ssd_in_specs(lambda i: i),
        out_specs=[pl.BlockSpec((SSD_CHUNK, SSD_INNER), lambda i: (i, 0)),
                   pl.BlockSpec((1, SSD_INNER, SSD_STATE), lambda i: (i, 0, 0))],
        out_shape=[SDS((SEQ, SSD_INNER), F32), SDS((N_CHUNKS, SSD_INNER, SSD_STATE), F32)],
        scratch_shapes=[pltpu.VMEM((SSD_INNER, SSD_STATE), F32)],
        compiler_params=_cparams(("arbitrary",)),
    )(xc, dtr, dtr_t, dtb, dtb_t, alog, alog_t)


def ssd_bwd(xc, dtr, dtr_t, dtb, dtb_t, alog, alog_t, hs, dy, dxs_extra, name):
    rev = lambda i: N_CHUNKS - 1 - i

    def body(xc_ref, dtr_ref, dtrt_ref, dtb_ref, dtbt_ref, al_ref, alt_ref, hs_ref, dy_ref, dxe_ref,
             dxc_ref, ddtr_ref, ddtrt_ref, ddtb_ref, ddtbt_ref, dal_ref, dalt_ref, dh_scr):
        @pl.when(pl.program_id(0) == 0)
        def _():
            dh_scr[...] = jnp.zeros_like(dh_scr)
            for r in (ddtb_ref, ddtbt_ref, dal_ref, dalt_ref):
                r[...] = jnp.zeros_like(r)

        prefix_in = (dtr_ref[...], dtrt_ref[...], dtb_ref[...], dtbt_ref[...], al_ref[...], alt_ref[...])
        (dt, a_cum, a_cum_t, a_last), prefix_vjp = jax.vjp(_ssd_prefix, *prefix_in)
        d_dt = jnp.zeros_like(dt)
        d_acum = jnp.zeros_like(a_cum)
        d_acum_t = jnp.zeros_like(a_cum_t)
        d_alast = jnp.zeros_like(a_last)
        for g in range(SSD_GROUPS):
            bg = xc_ref[:, B_OFF + g * SSD_STATE:B_OFF + (g + 1) * SSD_STATE]
            cg = xc_ref[:, C_OFF + g * SSD_STATE:C_OFF + (g + 1) * SSD_STATE]
            d_bg = jnp.zeros_like(bg)
            d_cg = jnp.zeros_like(cg)
            for j in range(PAIRS_PER_GROUP):
                pr = g * PAIRS_PER_GROUP + j
                sl = slice(pr * LANES, (pr + 1) * LANES)
                _, vjp = jax.vjp(functools.partial(_ssd_pair, e0=2 * pr),
                                 xc_ref[:, sl], bg, cg, hs_ref[0, sl, :], dt, a_cum, a_cum_t, a_last)
                dx, dbg, dcg, dhp, ddt, dac, dact, dal = vjp((dy_ref[:, sl], dh_scr[sl, :]))
                dxc_ref[:, sl] = dx + dxe_ref[:, sl]
                dh_scr[sl, :] = dhp
                d_bg, d_cg = d_bg + dbg, d_cg + dcg
                d_dt, d_acum, d_acum_t, d_alast = d_dt + ddt, d_acum + dac, d_acum_t + dact, d_alast + dal
            dxc_ref[:, B_OFF + g * SSD_STATE:B_OFF + (g + 1) * SSD_STATE] = d_bg
            dxc_ref[:, C_OFF + g * SSD_STATE:C_OFF + (g + 1) * SSD_STATE] = d_cg
        g_dtr, g_dtrt, g_dtb, g_dtbt, g_al, g_alt = prefix_vjp((d_dt, d_acum, d_acum_t, d_alast))
        ddtr_ref[...] = g_dtr
        ddtrt_ref[...] = g_dtrt
        ddtb_ref[...] += g_dtb
        ddtbt_ref[...] += g_dtbt
        dal_ref[...] += g_al
        dalt_ref[...] += g_alt

    in_specs = _ssd_in_specs(rev) + [
        pl.BlockSpec((1, SSD_INNER, SSD_STATE), lambda i: (rev(i), 0, 0)),
        pl.BlockSpec((SSD_CHUNK, SSD_INNER), lambda i: (rev(i), 0)),
        pl.BlockSpec((SSD_CHUNK, SSD_INNER), lambda i: (rev(i), 0)),
    ]
    out_specs = [
        pl.BlockSpec((SSD_CHUNK, SSD_CONV_CH), lambda i: (rev(i), 0)),
        pl.BlockSpec((SSD_CHUNK, HPAD), lambda i: (rev(i), 0)),
        pl.BlockSpec((HPAD, SSD_CHUNK), lambda i: (0, rev(i))),
        pl.BlockSpec((1, HPAD), lambda i: (0, 0)), pl.BlockSpec((HPAD, 1), lambda i: (0, 0)),
        pl.BlockSpec((1, HPAD), lambda i: (0, 0)), pl.BlockSpec((HPAD, 1), lambda i: (0, 0)),
    ]
    out_shape = [SDS((SEQ, SSD_CONV_CH), F32), SDS((SEQ, HPAD), F32), SDS((HPAD, SEQ), F32),
                 SDS((1, HPAD), F32), SDS((HPAD, 1), F32), SDS((1, HPAD), F32), SDS((HPAD, 1), F32)]
    return pl.pallas_call(
        body, name=name, grid=(N_CHUNKS,), in_specs=in_specs, out_specs=out_specs, out_shape=out_shape,
        scratch_shapes=[pltpu.VMEM((SSD_INNER, SSD_STATE), F32)],
        compiler_params=_cparams(("arbitrary",)),
    )(xc, dtr, dtr_t, dtb, dtb_t, alog, alog_t, hs, dy, dxs_extra)


ATTN_SCALE = ATTN_HEAD_DIM ** -0.5


def _attn_scores(q, kp, kc, has_prev):
    qi = lax.broadcasted_iota(jnp.int32, (ATTN_BLOCK, ATTN_BLOCK), 0)
    kj = lax.broadcasted_iota(jnp.int32, (ATTN_BLOCK, ATTN_BLOCK), 1)
    s_c = jnp.where(qi >= kj, _bdot(q, kc, NT) * ATTN_SCALE, -jnp.inf)
    s_p = jnp.where((kj >= qi) & has_prev, _bdot(q, kp, NT) * ATTN_SCALE, -jnp.inf)
    return s_p, s_c


UNITS_PER_PATTERN = SEQ // ATTN_BLOCK
ATTN_UNROLL = 2


def _for_units(unit):
    for g, d in enumerate(ATTN_DILATIONS):
        nb = UNITS_PER_PATTERN // d
        span = d * ATTN_BLOCK

        def one(i, carry, g=g, d=d, nb=nb, span=span):
            r = i >> (nb.bit_length() - 1)
            n = i & (nb - 1)
            start = r + n * span
            prev = jnp.where(n > 0, start - span, start)
            unit(g, pl.ds(start, ATTN_BLOCK, stride=d), pl.ds(prev, ATTN_BLOCK, stride=d), n > 0)
            return carry
        lax.fori_loop(0, UNITS_PER_PATTERN, one, 0, unroll=ATTN_UNROLL)


def _head_specs(n_q_groups):
    blk = (SEQ, ATTN_HEAD_DIM)
    q_specs = [pl.BlockSpec(blk, functools.partial(lambda h, g: (0, g * ATTN_KV_HEADS + h), g=g)) for g in range(n_q_groups)]
    head = pl.BlockSpec(blk, lambda h: (0, h))
    table = pl.BlockSpec(blk, lambda h: (0, 0))
    return q_specs, head, table


def attn_fwd(q, k, v, tabs, name):
    q_specs, head, table = _head_specs(ATTN_N_PAT)

    def body(q0_ref, q1_ref, q2_ref, k_ref, v_ref, c_ref, sa_ref, sb_ref, y_ref, lse_ref, *scr):
        qs, og, ls, ks, vs = scr[0:3], scr[3:6], scr[6:9], scr[9], scr[10]
        c, sa, sb = c_ref[...], sa_ref[...], sb_ref[...]
        for g, q_ref in enumerate((q0_ref, q1_ref, q2_ref)):
            qs[g][...] = _rope(q_ref[...].astype(F32), c, sa, sb)
        ks[...] = _rope(k_ref[...].astype(F32), c, sa, sb)
        vs[...] = v_ref[...].astype(F32)

        def unit(g, rows, prows, has_prev):
            s_p, s_c = _attn_scores(qs[g][rows, :], ks[prows, :], ks[rows, :], has_prev)
            m = jnp.maximum(jnp.max(s_c, axis=1, keepdims=True), jnp.max(s_p, axis=1, keepdims=True))
            p_c, p_p = jnp.exp(s_c - m), jnp.exp(s_p - m)
            l = jnp.sum(p_c, axis=1, keepdims=True) + jnp.sum(p_p, axis=1, keepdims=True)
            o = _bdot(p_c, vs[rows, :]) + _bdot(p_p, vs[prows, :])
            og[g][rows, :] = o / l
            ls[g][rows, :] = jnp.broadcast_to(m + jnp.log(l), (ATTN_BLOCK, LANES))

        _for_units(unit)
        l0, l1, l2 = ls[0][...], ls[1][...], ls[2][...]
        m = jnp.maximum(jnp.maximum(l0, l1), l2)
        e0, e1, e2 = jnp.exp(l0 - m), jnp.exp(l1 - m), jnp.exp(l2 - m)
        den = e0 + e1 + e2
        y_ref[...] = ((e0 * og[0][...] + e1 * og[1][...] + e2 * og[2][...]) / den).astype(y_ref.dtype)
        lse_ref[...] = m + jnp.log(den)

    blk = (SEQ, ATTN_HEAD_DIM)
    return pl.pallas_call(
        body, name=name, grid=(ATTN_KV_HEADS,), in_specs=[*q_specs, head, head, table, table, table],
        out_specs=[head, head], out_shape=[SDS((SEQ, ATTN_OUT), BF16), SDS((SEQ, ATTN_OUT), F32)],
        scratch_shapes=[pltpu.VMEM(blk, F32)] * (3 * ATTN_N_PAT + 2),
        compiler_params=_cparams(("parallel",)),
    )(q, q, q, k, v, *tabs)


def attn_bwd(q, k, v, tabs, y, lse, dy, name):
    q_specs, head, table = _head_specs(ATTN_N_PAT)

    def body(q0_ref, q1_ref, q2_ref, k_ref, v_ref, c_ref, sa_ref, sb_ref, y_ref, lse_ref, dy_ref,
             dq0_ref, dq1_ref, dq2_ref, dk_ref, dv_ref, *scr):
        qs, dqs, ks, dks, dd, dvs, vs = scr[0:3], scr[3:6], scr[6], scr[7], scr[8], scr[9], scr[10]
        c, sa, sb = c_ref[...], sa_ref[...], sb_ref[...]
        for g, q_ref in enumerate((q0_ref, q1_ref, q2_ref)):
            qs[g][...] = _rope(q_ref[...].astype(F32), c, sa, sb)
        ks[...] = _rope(k_ref[...].astype(F32), c, sa, sb)
        vs[...] = v_ref[...].astype(F32)
        dks[...] = jnp.zeros_like(dks)
        dvs[...] = jnp.zeros_like(dvs)
        dyv = dy_ref[...]
        dd[...] = jnp.broadcast_to(jnp.sum(dyv * y_ref[...].astype(F32), axis=1, keepdims=True), dd.shape)

        def unit(g, rows, prows, has_prev):
            qv = qs[g][rows, :].astype(BF16)
            kc, kp = ks[rows, :].astype(BF16), ks[prows, :].astype(BF16)
            vc, vp = vs[rows, :].astype(BF16), vs[prows, :].astype(BF16)
            do = dy_ref[rows, :].astype(BF16)
            s_p, s_c = _attn_scores(qv, kp, kc, has_prev)
            lse_u = lse_ref[rows, :][:, 0:1]
            dsum = dd[rows, :][:, 0:1]
            p_c, p_p = jnp.exp(s_c - lse_u), jnp.exp(s_p - lse_u)
            ds_c = (p_c * (_dot(do, vc, NT) - dsum) * ATTN_SCALE).astype(BF16)
            ds_p = (p_p * (_dot(do, vp, NT) - dsum) * ATTN_SCALE).astype(BF16)
            dqs[g][rows, :] = _dot(ds_c, kc) + _dot(ds_p, kp)
            dks[rows, :] += _dot(ds_c, qv, TN)
            dks[prows, :] += _dot(ds_p, qv, TN)
            dvs[rows, :] += _bdot(p_c, do, TN)
            dvs[prows, :] += _bdot(p_p, do, TN)

        _for_units(unit)
        for g, dq_ref in enumerate((dq0_ref, dq1_ref, dq2_ref)):
            dq_ref[...] = _rope(dqs[g][...], c, -sa, -sb).astype(dq_ref.dtype)
        dk_ref[...] = _rope(dks[...], c, -sa, -sb).astype(dk_ref.dtype)
        dv_ref[...] = dvs[...].astype(dv_ref.dtype)

    blk = (SEQ, ATTN_HEAD_DIM)
    out = SDS((SEQ, ATTN_OUT), BF16)
    return pl.pallas_call(
        body, name=name, grid=(ATTN_KV_HEADS,), in_specs=[*q_specs, head, head, table, table, table, head, head, head],
        out_specs=[head] * 5, out_shape=[out] * 5,
        scratch_shapes=[pltpu.VMEM(blk, F32)] * (2 * ATTN_N_PAT + 5),
        compiler_params=_cparams(("parallel",)),
    )(q, q, q, k, v, *tabs, y, lse, dy)


def layer_fwd(h, getw, small, tabs, li):
    n = f"l{li}_"
    sv = {}
    w = dict(getw(0, h))
    u = rms_fwd(h, small["norm_mix"], n + "rms_mix")
    z = matmul(u, w["w_z"], name=n + "mm_z", tb=True, out_dtype=BF16)
    xbc = matmul(u, w["w_xbc"], name=n + "mm_xbc", tb=True, out_dtype=BF16)
    dtr = matmul(u, w["w_dt"], name=n + "mm_dt", tb=True)
    q = matmul(u, w["w_q"], name=n + "mm_q", tb=True, out_dtype=BF16)
    k = matmul(u, w["w_k"], name=n + "mm_k", tb=True, out_dtype=BF16)
    v = matmul(u, w["w_v"], name=n + "mm_v", tb=True, out_dtype=BF16)
    gs = matmul(u, w["w_gs"], name=n + "mm_gs", tb=True, out_dtype=BF16)
    ga = matmul(u, w["w_ga"], name=n + "mm_ga", tb=True, out_dtype=BF16)
    xc = conv_fwd(xbc, w["conv_w"], small["conv_b"], n + "conv")
    dtr_t = dtr.T
    y_ssd, hs = ssd_fwd(xc, dtr, dtr_t, small["dt_bias"], small["dt_bias"].T, small["a_log"], small["a_log"].T, n + "ssd")
    yn = ssd_post_fwd(y_ssd, xc, z, small["d_skip_x"], small["ssd_norm"], n + "ssd_post")
    y_attn, lse = attn_fwd(q, k, v, tabs, n + "attn")
    w.update(getw(1, y_ssd))
    a = matmul(yn, w["w_ssd_branch"], name=n + "mm_a", out_dtype=BF16)
    b = matmul(y_attn, w["w_attn_branch"], name=n + "mm_b", out_dtype=BF16)
    merged = gate_fwd(a, b, gs, ga, n + "gate")
    h1 = matmul(merged, w["w_out"], name=n + "mm_o", add=h)
    w.update(getw(2, h1))
    u2 = rms_fwd(h1, small["norm_ffn"], n + "rms_ffn")
    gu = matmul(u2, w["w_gate_up"], name=n + "mm_gu", tb=True, out_dtype=BF16)
    act = swiglu_fwd(gu, n + "swiglu")
    h2 = matmul(act, w["w_down"], name=n + "mm_down", add=h1)
    sv.update(h=h, u=u, z=z, xbc=xbc, dtr=dtr, dtr_t=dtr_t, gs=gs, ga=ga, xc=xc, y_ssd=y_ssd, hs=hs, yn=yn,
              q=q, k=k, v=v, y_attn=y_attn, lse=lse, a=a, b=b, merged=merged, h1=h1, u2=u2, gu=gu, act=act, w=w)
    return h2, sv


def layer_bwd(dh, sv, small, tabs, li, emit):
    n = f"l{li}_b_"
    w = sv["w"]
    gw, gsm = {}, {}
    dact = matmul(dh, w["w_down"], name=n + "mm_dact", tb=True, out_dtype=BF16)
    gw["w_down"] = matmul(sv["act"], dh, name=n + "mm_dwdown", ta=True, out_dtype=BF16)
    dgu = swiglu_bwd(sv["gu"], dact, n + "swiglu")
    gw["w_gate_up"] = matmul(dgu, sv["u2"], name=n + "mm_dwgu", ta=True, out_dtype=BF16)
    tok = emit(2, gw)
    du2 = matmul(dgu, w["w_gate_up"], name=n + "mm_du2")
    dh1, gsm["norm_ffn"] = rms_bwd(sv["h1"], du2, dh, small["norm_ffn"] + tok, n + "rms_ffn")
    dmerged = matmul(dh1, w["w_out"], name=n + "mm_dmerged", tb=True)
    gw["w_out"] = matmul(sv["merged"], dh1, name=n + "mm_dwo", ta=True, out_dtype=BF16)
    da, db, dgs, dga = gate_bwd(sv["a"], sv["b"], sv["gs"], sv["ga"], dmerged, n + "gate")
    gw["w_ssd_branch"] = matmul(sv["yn"], da, name=n + "mm_dwa", ta=True, out_dtype=BF16)
    gw["w_attn_branch"] = matmul(sv["y_attn"], db, name=n + "mm_dwb", ta=True, out_dtype=BF16)
    tok = emit(1, gw)
    dyn = matmul(da, w["w_ssd_branch"], name=n + "mm_dyn", tb=True)
    dyattn = matmul(db, w["w_attn_branch"], name=n + "mm_dyattn", tb=True)
    dy_ssd, dxs_extra, dz, gsm["d_skip_x"], gsm["ssd_norm"] = ssd_post_bwd(
        sv["y_ssd"], sv["xc"], sv["z"], small["d_skip_x"] + tok, small["ssd_norm"], dyn, n + "ssd_post")
    dxc, ddtr, ddtr_t, ddtb, ddtb_t, dal, dal_t = ssd_bwd(
        sv["xc"], sv["dtr"], sv["dtr_t"], small["dt_bias"], small["dt_bias"].T, small["a_log"], small["a_log"].T,
        sv["hs"], dy_ssd, dxs_extra, n + "ssd")
    ddtr = (ddtr + ddtr_t.T).astype(BF16)
    gsm["dt_bias"] = ddtb + ddtb_t.T
    gsm["a_log"] = dal + dal_t.T
    dxbc, gw["conv_w"], gsm["conv_b"] = conv_bwd(sv["xbc"], w["conv_w"], small["conv_b"], dxc, n + "conv")
    dq0, dq1, dq2, dk, dv = attn_bwd(sv["q"], sv["k"], sv["v"], tabs, sv["y_attn"], sv["lse"], dyattn, n + "attn")
    u = sv["u"]
    segs = [("w_z", dz), ("w_xbc", dxbc), ("w_dt", ddtr), ("w_q0", dq0), ("w_q1", dq1), ("w_q2", dq2),
            ("w_k", dk), ("w_v", dv), ("w_gs", dgs), ("w_ga", dga)]
    gin = [matmul(dseg, u, name=n + "mm_d" + key, ta=True, out_dtype=BF16) for key, dseg in segs]
    gin[2] = gin[2][:SSD_HEADS]
    gw["w_in"] = jnp.concatenate(gin, axis=0)
    tok = emit(0, gw)
    du = jnp.zeros((SEQ, D_MODEL), F32) + tok
    for key, dseg in segs:
        du = matmul(dseg, w[key], name=n + "mm_du_" + key, add=du)
    dh0, gsm["norm_mix"] = rms_bwd(sv["h"], du, dh1, small["norm_mix"] + tok, n + "rms_mix")
    return dh0, gsm


def _my_place():
    return lax.axis_index("x"), lax.axis_index("y"), lax.axis_index("c")


def _flip(place, k):
    x, y, c = place
    return (1 - x if k & 4 else x, 1 - y if k & 2 else y, 1 - c if k & 1 else c)


def _index(place):
    return 4 * place[0] + 2 * place[1] + place[2]


ANY = pl.BlockSpec(memory_space=pl.ANY)
CHIP_FLIPS = (4, 2, 6)


def all_gather(xs, name):
    na = len(xs)

    def body(*refs):
        x_refs, o_refs = refs[:na], refs[na:2 * na]
        send_sems, recv_sems, local_sems = refs[2 * na:]
        me = _my_place()
        sibling = _flip(me, 1)
        chips = [_flip(me, f) for f in CHIP_FLIPS]

        def copy(a, kk, block, to, src=None):
            dst = o_refs[a].at[_index(block)]
            return pltpu.make_async_remote_copy(
                src_ref=dst if src is None else src, dst_ref=dst, send_sem=send_sems.at[a, kk],
                recv_sem=recv_sems.at[a, kk], device_id=to, device_id_type=MESH)

        mine = [pltpu.make_async_copy(x_refs[a], o_refs[a].at[_index(me)], local_sems.at[a]) for a in range(na)]
        for cp in mine:
            cp.start()
        first = []
        for j, chip in enumerate(chips):
            first += [copy(a, 1 + j, me, chip, src=x_refs[a]) for a in range(na)]
        first += [copy(a, 0, me, sibling, src=x_refs[a]) for a in range(na)]
        for cp in first:
            cp.start()
        passed = []
        for j, chip in enumerate(chips):
            for a in range(na):
                copy(a, 1 + j, chip, me).wait_recv()
                cp = copy(a, 4 + j, chip, sibling)
                cp.start()
                passed.append(cp)
        for a in range(na):
            copy(a, 0, sibling, me).wait_recv()
        for j, chip in enumerate(chips):
            for a in range(na):
                copy(a, 4 + j, _flip(chip, 1), me).wait_recv()
        for cp in first + passed:
            cp.wait_send()
        for cp in mine:
            cp.wait()

    return pl.pallas_call(
        body, name=name, in_specs=[ANY] * na, out_specs=[ANY] * na,
        out_shape=[SDS((N_DEV,) + t.shape, t.dtype) for t in xs],
        scratch_shapes=[pltpu.SemaphoreType.DMA((na, N_DEV - 1)), pltpu.SemaphoreType.DMA((na, N_DEV - 1)),
                        pltpu.SemaphoreType.DMA((na,))],
    )(*xs)


HBM = pl.BlockSpec(memory_space=pltpu.HBM)
SEM = pl.BlockSpec(memory_space=pltpu.SEMAPHORE)
EFFECT = pltpu.SideEffectType.DATAFLOW_SIDE_EFFECTING
N_PEERS = N_DEV - 1


def _split_copy(src_ref, land_ref, send_sem, recv_sem, me, kk, scatter, landed_from_peer):
    peer = _flip(me, kk)
    src = src_ref.at[_index(peer)] if scatter else src_ref
    dst = land_ref.at[_index(peer if landed_from_peer else me)]
    return pltpu.make_async_remote_copy(src_ref=src, dst_ref=dst, send_sem=send_sem, recv_sem=recv_sem,
                                        device_id=peer, device_id_type=MESH)


def exchange_start(srcs, lands, group_sizes, scatter, name):
    na, ng = len(srcs), len(group_sizes)

    def body(*refs):
        s_refs, l_refs = refs[:na], refs[na:2 * na]
        sems = refs[2 * na:2 * na + 2 * ng]
        token = refs[-1]
        me = _my_place()
        a = 0
        for gi, gsz in enumerate(group_sizes):
            for j in range(gsz):
                for kk in range(1, N_DEV):
                    slot = j * N_PEERS + kk - 1
                    _split_copy(s_refs[a], l_refs[a], sems[2 * gi].at[slot], sems[2 * gi + 1].at[slot],
                                me, kk, scatter, False).start()
                a += 1
        token[...] = jnp.zeros_like(token)

    sem_shapes = []
    for gsz in group_sizes:
        sem_shapes += [pltpu.SemaphoreType.DMA((gsz * N_PEERS,))] * 2
    ins = [pltpu.with_memory_space_constraint(t, pltpu.HBM) for t in (*srcs, *lands)]
    res = pl.pallas_call(
        body, name=name, in_specs=[HBM] * (2 * na),
        out_specs=[SEM] * (2 * ng) + [HBM] * (2 * na) + [pl.BlockSpec(memory_space=pltpu.VMEM)],
        out_shape=sem_shapes + [pltpu.HBM(t.shape, t.dtype) for t in ins] + [SDS((8, LANES), F32)],
        input_output_aliases={i: 2 * ng + i for i in range(2 * na)},
        compiler_params=pltpu.CompilerParams(has_side_effects=EFFECT),
    )(*ins)
    sems = [(res[2 * gi], res[2 * gi + 1]) for gi in range(ng)]
    thru = res[2 * ng:2 * ng + 2 * na]
    return sems, thru[:na], thru[na:], res[-1]


def exchange_wait(srcs, lands, sems, after, scatter, name):
    n = len(srcs)

    def body(*refs):
        s_refs, l_refs = refs[:n], refs[n:2 * n]
        send_sems, recv_sems = refs[2 * n], refs[2 * n + 1]
        me = _my_place()
        for j in range(n):
            for kk in range(1, N_DEV):
                slot = j * N_PEERS + kk - 1
                cp = _split_copy(s_refs[j], l_refs[j], send_sems.at[slot], recv_sems.at[slot], me, kk, scatter, True)
                cp.wait_send()
                cp.wait_recv()

    res = pl.pallas_call(
        body, name=name, in_specs=[HBM] * (2 * n) + [SEM, SEM, ANY], out_specs=[HBM] * (2 * n),
        out_shape=[pltpu.HBM(t.shape, t.dtype) for t in (*srcs, *lands)],
        input_output_aliases={i: i for i in range(2 * n)},
        compiler_params=pltpu.CompilerParams(has_side_effects=EFFECT),
    )(*srcs, *lands, sems[0], sems[1], after)
    return res[n:]


def landing_zone(block, me_index):
    land = lax.empty((N_DEV,) + block.shape, block.dtype)
    return lax.dynamic_update_slice(land, block[None], (me_index,) + (0,) * block.ndim)


def sum_parts(parts, name):
    _, r, c = parts.shape
    tc = _pick(c, (256, 128))

    def body(p_ref, o_ref):
        acc = p_ref[0].astype(F32)
        for i in range(1, N_DEV):
            acc = acc + p_ref[i].astype(F32)
        o_ref[...] = acc

    return pl.pallas_call(
        body, name=name, grid=(c // tc,), in_specs=[pl.BlockSpec((N_DEV, r, tc), lambda i: (0, 0, i))],
        out_specs=pl.BlockSpec((r, tc), lambda i: (0, i)), out_shape=SDS((r, c), F32),
        compiler_params=_cparams(("parallel",)),
    )(parts)


def adamw(w, g, m, v, name):
    shape = w.shape
    cols = shape[-1]
    rows = w.size // cols
    tr = _pick(rows, (256, 128, 64, 32, 16, 8))
    c1 = 1.0 / (1.0 - ADAM_B1 ** ADAM_STEP)
    c2 = 1.0 / (1.0 - ADAM_B2 ** ADAM_STEP)

    def body(w_ref, g_ref, m_ref, v_ref, d_ref, nm_ref, nv_ref):
        gg = g_ref[...]
        nm = ADAM_B1 * m_ref[...] + (1.0 - ADAM_B1) * gg
        nv = ADAM_B2 * v_ref[...] + (1.0 - ADAM_B2) * (gg * gg)
        d_ref[...] = -ADAM_LR * ((nm * c1) / (jnp.sqrt(nv * c2) + ADAM_EPS) + ADAM_WD * w_ref[...])
        nm_ref[...] = nm
        nv_ref[...] = nv

    spec = pl.BlockSpec((tr, cols), lambda i: (i, 0))
    outs = pl.pallas_call(
        body, name=name, grid=(rows // tr,), in_specs=[spec] * 4, out_specs=[spec] * 3,
        out_shape=[SDS((rows, cols), F32)] * 3, compiler_params=_cparams(("parallel",)),
    )(*[t.reshape(rows, cols) for t in (w, g, m, v)])
    return [o.reshape(shape) for o in outs]


BIG = ("w_in", "conv_w", "w_ssd_branch", "w_attn_branch", "w_out", "w_gate_up", "w_down")
TRANSPOSED = ("w_in", "w_gate_up")
SMALL = ("norm_mix", "conv_b", "dt_bias", "a_log", "d_skip", "ssd_norm", "norm_ffn")
SMALL_SIZE = {"norm_mix": 1024, "conv_b": 3072, "dt_bias": 32, "a_log": 32, "d_skip": 32, "ssd_norm": 2048, "norm_ffn": 1024}
FLAT_W = 512
SMALL_TOTAL = DEPTH * sum(SMALL_SIZE.values()) + D_MODEL + LANES
SMALL_ROWS = 32
assert SMALL_ROWS * FLAT_W >= SMALL_TOTAL


GROUPS = (("w_in", "conv_w"), ("w_ssd_branch", "w_attn_branch", "w_out"), ("w_gate_up", "w_down"))


def to_wire(k, shard):
    if k in TRANSPOSED:
        return shard.T.astype(BF16)
    return shard if k == "conv_w" else shard.astype(BF16)


def full_weights(k, g):
    if k == "conv_w":
        return {k: g.transpose(1, 0, 2).reshape(SSD_CONV, SSD_CONV_CH)}
    full = g.reshape(-1, g.shape[-1])
    if k != "w_in":
        return {k: full}
    w, off = {}, 0
    for nm, r in IN_ROWS:
        w[nm] = full[off:off + r]
        off += r
    w["w_q"] = full[sum(r for _, r in IN_ROWS[:3]):sum(r for _, r in IN_ROWS[:6])]
    w["w_dt"] = jnp.pad(w["w_dt"], ((0, HPAD - SSD_HEADS), (0, 0)))
    return w


def grads_to_wire(k, g):
    if k == "conv_w":
        return g.reshape(SSD_CONV, N_DEV, SSD_CONV_CH // N_DEV).transpose(1, 0, 2)
    return g.reshape(N_DEV, g.shape[0] // N_DEV, g.shape[1])


def _pad_heads(t):
    return jnp.pad(t.reshape(1, SSD_HEADS), ((0, 0), (0, HPAD - SSD_HEADS)))


def local_step(x, target, getw, emit, smalls, norm_final):
    tabs = rope_tables()
    sms = []
    for li in range(DEPTH):
        s = smalls[li]
        sms.append({
            "norm_mix": s["norm_mix"].reshape(1, -1), "conv_b": s["conv_b"].reshape(1, -1),
            "dt_bias": _pad_heads(s["dt_bias"]), "a_log": _pad_heads(s["a_log"]),
            "d_skip_x": jnp.repeat(s["d_skip"], SSD_HEAD_DIM).reshape(1, -1),
            "ssd_norm": s["ssd_norm"].reshape(1, -1), "norm_ffn": s["norm_ffn"].reshape(1, -1)})
    h = x
    saved = []
    for li in range(DEPTH):
        h, sv = layer_fwd(h, functools.partial(getw, li), sms[li], tabs, li)
        saved.append(sv)
    dh, g_final, loss = loss_head(h, target, norm_final.reshape(1, -1), "loss_head")
    gsms = [None] * DEPTH
    for li in reversed(range(DEPTH)):
        dh, gsm = layer_bwd(dh, saved[li], sms[li], tabs, li, functools.partial(emit, li))
        gsms[li] = {
            "norm_mix": gsm["norm_mix"].reshape(-1), "conv_b": gsm["conv_b"].reshape(-1),
            "dt_bias": gsm["dt_bias"][0, :SSD_HEADS], "a_log": gsm["a_log"][0, :SSD_HEADS],
            "d_skip": gsm["d_skip_x"].reshape(SSD_HEADS, SSD_HEAD_DIM).sum(axis=1),
            "ssd_norm": gsm["ssd_norm"].reshape(-1), "norm_ffn": gsm["norm_ffn"].reshape(-1)}
    return loss, dh, gsms, g_final.reshape(-1)


def kernel(x, norm_mix, w_in, conv_w, conv_b, dt_bias, a_log, d_skip, ssd_norm, w_ssd_branch, w_attn_branch, w_out, norm_ffn, w_gate_up, w_down, norm_final, loss_target, m_norm_mix, m_w_in, m_conv_w, m_conv_b, m_dt_bias, m_a_log, m_d_skip, m_ssd_norm, m_w_ssd_branch, m_w_attn_branch, m_w_out, m_norm_ffn, m_w_gate_up, m_w_down, m_norm_final, v_norm_mix, v_w_in, v_conv_w, v_conv_b, v_dt_bias, v_a_log, v_d_skip, v_ssd_norm, v_w_ssd_branch, v_w_attn_branch, v_w_out, v_norm_ffn, v_w_gate_up, v_w_down, v_norm_final):
    wv = dict(norm_mix=norm_mix, w_in=w_in, conv_w=conv_w, conv_b=conv_b, dt_bias=dt_bias, a_log=a_log, d_skip=d_skip,
              ssd_norm=ssd_norm, w_ssd_branch=w_ssd_branch, w_attn_branch=w_attn_branch, w_out=w_out, norm_ffn=norm_ffn,
              w_gate_up=w_gate_up, w_down=w_down, norm_final=norm_final)
    mv = dict(norm_mix=m_norm_mix, w_in=m_w_in, conv_w=m_conv_w, conv_b=m_conv_b, dt_bias=m_dt_bias, a_log=m_a_log,
              d_skip=m_d_skip, ssd_norm=m_ssd_norm, w_ssd_branch=m_w_ssd_branch, w_attn_branch=m_w_attn_branch,
              w_out=m_w_out, norm_ffn=m_norm_ffn, w_gate_up=m_w_gate_up, w_down=m_w_down, norm_final=m_norm_final)
    vv = dict(norm_mix=v_norm_mix, w_in=v_w_in, conv_w=v_conv_w, conv_b=v_conv_b, dt_bias=v_dt_bias, a_log=v_a_log,
              d_skip=v_d_skip, ssd_norm=v_ssd_norm, w_ssd_branch=v_w_ssd_branch, w_attn_branch=v_w_attn_branch,
              w_out=v_w_out, norm_ffn=v_norm_ffn, w_gate_up=v_w_gate_up, w_down=v_w_down, norm_final=v_norm_final)
    order = ("norm_mix", "w_in", "conv_w", "conv_b", "dt_bias", "a_log", "d_skip", "ssd_norm", "w_ssd_branch",
             "w_attn_branch", "w_out", "norm_ffn", "w_gate_up", "w_down", "norm_final")

    me_index = _index(_my_place())
    smalls = [{k: wv[k][li] for k in SMALL} for li in range(DEPTH)]
    n_groups = len(GROUPS)

    first_lands = all_gather([to_wire(k, wv[k][0]) for k in GROUPS[0]], "gather_first")
    later = [(li, gi) for li in range(DEPTH) for gi in range(n_groups)][1:]
    behind_first = first_lands[1][0, 0, 0] * 0.0
    srcs = [to_wire(k, wv[k][li] + behind_first if k == "conv_w" else wv[k][li]) for li, gi in later for k in GROUPS[gi]]
    sizes = [len(GROUPS[gi]) for _, gi in later]
    w_sems, w_srcs, w_lands, token = exchange_start(srcs, [landing_zone(s, me_index) for s in srcs], sizes, False, "gather_start")
    smalls[0]["norm_mix"] = smalls[0]["norm_mix"] + token[0, 0]

    def getw(li, gi, after):
        if (li, gi) == (0, 0):
            lands = first_lands
        else:
            slot = later.index((li, gi))
            sl = slice(sum(sizes[:slot]), sum(sizes[:slot + 1]))
            lands = exchange_wait(w_srcs[sl], w_lands[sl], w_sems[slot], after, False, f"gather_wait_{li}_{gi}")
        w = {}
        for k, land in zip(GROUPS[gi], lands):
            w.update(full_weights(k, land))
        return w

    pending = []

    def emit(li, gi, gw):
        parts = [grads_to_wire(k, gw[k]) for k in GROUPS[gi]]
        lands = [landing_zone(lax.dynamic_index_in_dim(p, me_index, 0, keepdims=False), me_index) for p in parts]
        sems, p_thru, l_thru, tok = exchange_start(parts, lands, [len(parts)], True, f"grads_start_{li}_{gi}")
        pending.append((li, gi, sems[0], p_thru, l_thru))
        return tok[0, 0]

    loss_p, dx, gsms, g_final = local_step(x[0], loss_target[0], getw, emit, smalls, norm_final)

    grads, deltas, new_m, new_v = {}, {}, {}, {}

    def update(k):
        w2, g2, m2, v2 = wv[k], grads[k], mv[k], vv[k]
        if w2.ndim == 1:
            w2, g2, m2, v2 = (t.reshape(1, -1) for t in (w2, g2, m2, v2))
        d, nm, nv = adamw(w2, g2, m2, v2, "adamw_" + k)
        deltas[k], new_m[k], new_v[k] = (t.reshape(wv[k].shape) for t in (d, nm, nv))
        return nv

    shard_g = {k: [None] * DEPTH for k in BIG}

    def collect(entry, after):
        li, gi, sems, p_thru, l_thru = entry
        recv = exchange_wait(p_thru, l_thru, sems, after, True, f"grads_wait_{li}_{gi}")
        for k, r in zip(GROUPS[gi], recv):
            if k == "conv_w":
                r = r.reshape(N_DEV, 1, -1)
            after = sum_parts(r, f"sum_{k}_{li}")
            shard_g[k][li] = after.T if k in TRANSPOSED else after.reshape(wv[k].shape[1:])
        return after

    after = dx
    for entry in pending[:-1]:
        after = collect(entry, after)
    done = [after[:1, :1].reshape(1)]
    for gi in (2, 1):
        for k in GROUPS[gi]:
            grads[k] = jnp.stack(shard_g[k])
            done.append(update(k).reshape(-1)[:1])

    flat = [gsms[li][k] for li in range(DEPTH) for k in SMALL] + [g_final, loss_p.reshape(-1)]
    flat.append(jnp.zeros((SMALL_ROWS * FLAT_W - SMALL_TOTAL,), F32))
    small_all = all_gather([jnp.concatenate(flat).reshape(SMALL_ROWS, FLAT_W)], "gather_small")[0]
    small_sum = sum_parts(small_all, "sum_small").reshape(-1)
    off = 0
    per_layer = {k: [] for k in SMALL}
    for li in range(DEPTH):
        for k in SMALL:
            per_layer[k].append(small_sum[off:off + SMALL_SIZE[k]])
            off += SMALL_SIZE[k]
    for k in SMALL:
        grads[k] = jnp.stack(per_layer[k])
    grads["norm_final"] = small_sum[off:off + D_MODEL]
    loss = small_sum[off + D_MODEL]
    for k in (*SMALL, "norm_final"):
        done.append(update(k).reshape(-1)[:1])

    collect(pending[-1], jnp.concatenate(done))
    for k in GROUPS[0]:
        grads[k] = jnp.stack(shard_g[k])
        update(k)

    return (loss, dx.reshape(x.shape), *[grads[k] for k in order], *[deltas[k] for k in order],
            *[new_m[k] for k in order], *[new_v[k] for k in order])
```

```python
import functools

import jax
import jax.numpy as jnp
from jax import lax
from jax.experimental import pallas as pl
from jax.experimental.pallas import tpu as pltpu

F32, BF16 = jnp.float32, jnp.bfloat16
SDS = jax.ShapeDtypeStruct
MESH = pl.DeviceIdType.MESH

D_MODEL = 1024
SEQ = 2048
DEPTH = 2
RMS_EPS = 1e-5
SSD_INNER = 2048
SSD_HEAD_DIM = 64
SSD_HEADS = 32
SSD_STATE = 128
SSD_GROUPS = 4
SSD_CONV = 4
SSD_CHUNK = 128
SSD_CONV_CH = 3072
ATTN_HEAD_DIM = 128
ATTN_KV_HEADS = 8
ATTN_DILATIONS = (1, 4, 16)
ATTN_N_PAT = 3
ATTN_BLOCK = 128
ATTN_OUT = 1024
ROPE_THETA = 500000.0
ROPE_DIM = 32
FFN_HIDDEN = 2816
ADAM_LR, ADAM_B1, ADAM_B2, ADAM_EPS, ADAM_WD, ADAM_STEP = 0.001, 0.9, 0.999, 1e-08, 0.01, 10

N_DEV = 8
LANES = 128
VMEM_LIMIT = 56 * 1024 * 1024
HPAD = 128
HIGHEST = lax.Precision.HIGHEST

IN_ROWS = (("w_z", 2048), ("w_xbc", 3072), ("w_dt", 32), ("w_q0", 1024), ("w_q1", 1024), ("w_q2", 1024),
           ("w_k", 1024), ("w_v", 1024), ("w_gs", 1024), ("w_ga", 1024))
N_IN = sum(r for _, r in IN_ROWS)


def _cparams(sem):
    return pltpu.CompilerParams(dimension_semantics=sem, vmem_limit_bytes=VMEM_LIMIT)


def _sigmoid(x):
    return 0.5 * jnp.tanh(0.5 * x) + 0.5


def _silu(x):
    return x * _sigmoid(x)


def _softplus(x):
    return jnp.maximum(x, 0.0) + jnp.log(1.0 + jnp.exp(-jnp.abs(x)))


def _dot(a, b, dims=(((1,), (0,)), ((), ())), precision=None):
    return lax.dot_general(a, b, dims, precision=precision, preferred_element_type=F32)


NT = (((1,), (1,)), ((), ()))
TN = (((0,), (0,)), ((), ()))


def _bdot(a, b, dims=(((1,), (0,)), ((), ()))):
    return _dot(a.astype(BF16), b.astype(BF16), dims)


def _pick(dim, cands):
    for c in cands:
        if dim % c == 0:
            return c
    return dim


def matmul(a, b, *, name, ta=False, tb=False, out_dtype=F32, add=None):
    m, k = (a.shape[1], a.shape[0]) if ta else a.shape
    n = b.shape[0] if tb else b.shape[1]
    tm = _pick(m, (1024, 1408, 512, 256, 128))
    tn = _pick(n, (512, 256, 128))
    tk = _pick(k, (1024, 1408, 512, 256, 128))
    nk = k // tk
    a_spec = pl.BlockSpec((tk, tm), lambda i, j, kk: (kk, i)) if ta else pl.BlockSpec((tm, tk), lambda i, j, kk: (i, kk))
    b_spec = pl.BlockSpec((tn, tk), lambda i, j, kk: (j, kk)) if tb else pl.BlockSpec((tk, tn), lambda i, j, kk: (kk, j))
    dims = (((0 if ta else 1,), (1 if tb else 0,)), ((), ()))
    has_add = add is not None

    def body(*refs):
        a_ref, b_ref = refs[:2]
        add_ref = refs[2] if has_add else None
        o_ref = refs[3] if has_add else refs[2]
        acc = refs[-1] if nk > 1 else None
        kk = pl.program_id(2)

        def product():
            return _dot(a_ref[...].astype(BF16), b_ref[...].astype(BF16), dims)

        def finish(r):
            if has_add:
                r = r + add_ref[...].astype(F32)
            o_ref[...] = r.astype(o_ref.dtype)

        if nk == 1:
            finish(product())
            return

        @pl.when(kk == 0)
        def _():
            acc[...] = product()

        @pl.when((kk > 0) & (kk < nk - 1))
        def _():
            acc[...] += product()

        @pl.when(kk == nk - 1)
        def _():
            finish(acc[...] + product())

    in_specs = [a_spec, b_spec]
    args = [a, b]
    if has_add:
        in_specs.append(pl.BlockSpec((tm, tn), lambda i, j, kk: (i, j)))
        args.append(add)
    return pl.pallas_call(
        body, name=name, grid=(m // tm, n // tn, nk),
        in_specs=in_specs, out_specs=pl.BlockSpec((tm, tn), lambda i, j, kk: (i, j)),
        out_shape=SDS((m, n), out_dtype), scratch_shapes=[pltpu.VMEM((tm, tn), F32)] if nk > 1 else [],
        compiler_params=_cparams(("parallel", "parallel", "arbitrary")),
    )(*args)


def rowcall(name, fn, rows, params, row_outs, red_outs=(), tr=256):
    s = rows[0].shape[0]
    n_in = len(rows) + len(params)
    n_row = len(row_outs)

    def body(*refs):
        outs = fn(*[r[...].astype(F32) for r in refs[:n_in]])
        if not isinstance(outs, (tuple, list)):
            outs = (outs,)
        orefs = refs[n_in:]
        for r, o in zip(orefs[:n_row], outs[:n_row]):
            r[...] = o.astype(r.dtype)
        if red_outs:
            @pl.when(pl.program_id(0) == 0)
            def _():
                for r in orefs[n_row:]:
                    r[...] = jnp.zeros_like(r)
            for r, o in zip(orefs[n_row:], outs[n_row:]):
                r[...] += o.astype(F32)

    in_specs = [pl.BlockSpec((tr, a.shape[1]), lambda i: (i, 0)) for a in rows]
    in_specs += [pl.BlockSpec(p.shape, lambda i: (0, 0)) for p in params]
    out_specs = [pl.BlockSpec((tr, c), lambda i: (i, 0)) for c, _ in row_outs]
    out_specs += [pl.BlockSpec(shp, lambda i: (0, 0)) for shp in red_outs]
    out_shape = [SDS((s, c), dt) for c, dt in row_outs] + [SDS(shp, F32) for shp in red_outs]
    res = pl.pallas_call(
        body, name=name, grid=(s // tr,), in_specs=in_specs, out_specs=out_specs, out_shape=out_shape,
        compiler_params=_cparams(("arbitrary",) if red_outs else ("parallel",)),
    )(*rows, *params)
    return res


def _rms(x, w):
    return x * lax.rsqrt(jnp.mean(x * x, axis=-1, keepdims=True) + RMS_EPS) * w


def rms_fwd(h, w, name):
    return rowcall(name, _rms, [h], [w], [(D_MODEL, BF16)])[0]


def rms_bwd(h, du, dres, w, name):
    def fn(hb, dub, dresb, wb):
        _, vjp = jax.vjp(_rms, hb, wb)
        dh, dw = vjp(dub)
        return dh + dresb, dw
    return rowcall(name, fn, [h, du, dres], [w], [(D_MODEL, F32)], [(1, D_MODEL)])


def loss_head(h, target, w, name):
    def fn(hb, tb, wb):
        def f(hh, ww):
            err = _rms(hh, ww) - tb
            return 0.5 * jnp.sum(jnp.mean(err * err, axis=-1, keepdims=True), axis=0, keepdims=True)
        val, vjp = jax.vjp(f, hb, wb)
        dh, dw = vjp(jnp.ones((1, 1), F32))
        return dh, dw, jnp.broadcast_to(val, (1, LANES))
    return rowcall(name, fn, [h, target], [w], [(D_MODEL, F32)], [(1, D_MODEL), (1, LANES)])


def _gate(a, b, gs, ga):
    return _sigmoid(gs) * a + _sigmoid(ga) * b


def gate_fwd(a, b, gs, ga, name):
    return rowcall(name, _gate, [a, b, gs, ga], [], [(D_MODEL, BF16)])[0]


def gate_bwd(a, b, gs, ga, dm, name):
    def fn(ab, bb, gsb, gab, dmb):
        _, vjp = jax.vjp(_gate, ab, bb, gsb, gab)
        return vjp(dmb)
    return rowcall(name, fn, [a, b, gs, ga, dm], [], [(D_MODEL, BF16)] * 4)


def _swiglu(gu):
    return _silu(gu[:, :FFN_HIDDEN]) * gu[:, FFN_HIDDEN:]


def swiglu_fwd(gu, name):
    return rowcall(name, _swiglu, [gu], [], [(FFN_HIDDEN, BF16)])[0]


def swiglu_bwd(gu, dact, name):
    def fn(gub, db):
        _, vjp = jax.vjp(_swiglu, gub)
        return vjp(db.astype(F32))[0]
    return rowcall(name, fn, [gu, dact], [], [(2 * FFN_HIDDEN, BF16)])[0]


def _ssd_post(y, xs, z, dskip, normw):
    y = (y + dskip * xs) * _silu(z)
    gw = SSD_INNER // SSD_GROUPS
    parts = []
    for g in range(SSD_GROUPS):
        yg = y[:, g * gw:(g + 1) * gw]
        parts.append(yg * lax.rsqrt(jnp.mean(yg * yg, axis=-1, keepdims=True) + RMS_EPS))
    return jnp.concatenate(parts, axis=-1) * normw


def ssd_post_fwd(y, xc, z, dskip, normw, name):
    def fn(yb, xcb, zb, db, nb):
        return _ssd_post(yb, xcb[:, :SSD_INNER], zb, db, nb)
    return rowcall(name, fn, [y, xc, z], [dskip, normw], [(SSD_INNER, BF16)])[0]


def ssd_post_bwd(y, xc, z, dskip, normw, dyn, name):
    def fn(yb, xcb, zb, dynb, db, nb):
        _, vjp = jax.vjp(_ssd_post, yb, xcb[:, :SSD_INNER], zb, db, nb)
        return vjp(dynb)
    return rowcall(name, fn, [y, xc, z, dyn], [dskip, normw],
                   [(SSD_INNER, F32), (SSD_INNER, F32), (SSD_INNER, BF16)], [(1, SSD_INNER), (1, SSD_INNER)])


def _rope(t, cosf, sina, sinb):
    return t * cosf + pltpu.roll(t, LANES - ROPE_DIM // 2, 1) * sina + pltpu.roll(t, ROPE_DIM // 2, 1) * sinb


def rope_tables():
    half = ROPE_DIM // 2
    inv = ROPE_THETA ** (-jnp.arange(0, ROPE_DIM, 2, dtype=F32) / ROPE_DIM)
    ang = jnp.arange(SEQ, dtype=F32)[:, None] * inv[None, :]
    cos, sin = jnp.cos(ang), jnp.sin(ang)
    zeros = jnp.zeros((SEQ, LANES - ROPE_DIM), F32)
    z16 = jnp.zeros((SEQ, half), F32)
    cosf = jnp.concatenate([cos, cos, jnp.ones((SEQ, LANES - ROPE_DIM), F32)], axis=1)
    sina = jnp.concatenate([-sin, z16, zeros], axis=1)
    sinb = jnp.concatenate([z16, sin, zeros], axis=1)
    return cosf, sina, sinb


CONV_TC = 256


def _conv_pre(x, w, b, row):
    acc = x * w[SSD_CONV - 1:SSD_CONV, :] + b
    shifted = [x]
    for j in range(1, SSD_CONV):
        xs = jnp.where(row >= j, pltpu.roll(x, j, 0), 0.0)
        shifted.append(xs)
        acc = acc + xs * w[SSD_CONV - 1 - j:SSD_CONV - j, :]
    return acc, shifted


def conv_fwd(xbc, w, b, name):
    def body(x_ref, w_ref, b_ref, o_ref):
        row = lax.broadcasted_iota(jnp.int32, (SEQ, CONV_TC), 0)
        pre, _ = _conv_pre(x_ref[...].astype(F32), w_ref[...], b_ref[...], row)
        o_ref[...] = _silu(pre)
    return pl.pallas_call(
        body, name=name, grid=(SSD_CONV_CH // CONV_TC,),
        in_specs=[pl.BlockSpec((SEQ, CONV_TC), lambda i: (0, i)), pl.BlockSpec((SSD_CONV, CONV_TC), lambda i: (0, i)),
                  pl.BlockSpec((1, CONV_TC), lambda i: (0, i))],
        out_specs=pl.BlockSpec((SEQ, CONV_TC), lambda i: (0, i)),
        out_shape=SDS((SEQ, SSD_CONV_CH), F32), compiler_params=_cparams(("parallel",)),
    )(xbc, w, b)


def conv_bwd(xbc, w, b, dxc, name):
    def body(x_ref, w_ref, b_ref, dy_ref, dx_ref, dw_ref, db_ref):
        row = lax.broadcasted_iota(jnp.int32, (SEQ, CONV_TC), 0)
        wv = w_ref[...]
        pre, shifted = _conv_pre(x_ref[...].astype(F32), wv, b_ref[...], row)
        sg = _sigmoid(pre)
        ds = dy_ref[...] * (sg * (1.0 + pre * (1.0 - sg)))
        dx = ds * wv[SSD_CONV - 1:SSD_CONV, :]
        for j in range(1, SSD_CONV):
            dsj = jnp.where(row < SEQ - j, pltpu.roll(ds, SEQ - j, 0), 0.0)
            dx = dx + dsj * wv[SSD_CONV - 1 - j:SSD_CONV - j, :]
        dx_ref[...] = dx.astype(dx_ref.dtype)
        for j in range(SSD_CONV):
            dw_ref[SSD_CONV - 1 - j:SSD_CONV - j, :] = jnp.sum(ds * shifted[j], axis=0, keepdims=True)
        db_ref[...] = jnp.sum(ds, axis=0, keepdims=True)
    return pl.pallas_call(
        body, name=name, grid=(SSD_CONV_CH // CONV_TC,),
        in_specs=[pl.BlockSpec((SEQ, CONV_TC), lambda i: (0, i)), pl.BlockSpec((SSD_CONV, CONV_TC), lambda i: (0, i)),
                  pl.BlockSpec((1, CONV_TC), lambda i: (0, i)), pl.BlockSpec((SEQ, CONV_TC), lambda i: (0, i))],
        out_specs=[pl.BlockSpec((SEQ, CONV_TC), lambda i: (0, i)), pl.BlockSpec((SSD_CONV, CONV_TC), lambda i: (0, i)),
                   pl.BlockSpec((1, CONV_TC), lambda i: (0, i))],
        out_shape=[SDS((SEQ, SSD_CONV_CH), BF16), SDS((SSD_CONV, SSD_CONV_CH), F32), SDS((1, SSD_CONV_CH), F32)],
        compiler_params=_cparams(("parallel",)),
    )(xbc, w, b, dxc)


N_CHUNKS = SEQ // SSD_CHUNK
N_PAIRS = SSD_HEADS // 2
PAIRS_PER_GROUP = N_PAIRS // SSD_GROUPS
B_OFF = SSD_INNER
C_OFF = SSD_INNER + SSD_GROUPS * SSD_STATE


def _ssd_prefix(dtr, dtr_t, dtb, dtb_t, alog, alog_t):
    ln = SSD_CHUNK
    dt = _softplus(dtr + dtb)
    dt_t = _softplus(dtr_t + dtb_t)
    dta = dt * (-jnp.exp(alog))
    dta_t = dt_t * (-jnp.exp(alog_t))
    r = lax.broadcasted_iota(jnp.int32, (ln, ln), 0)
    c = lax.broadcasted_iota(jnp.int32, (ln, ln), 1)
    a_cum = _dot((r >= c).astype(F32), dta, precision=HIGHEST)
    a_cum_t = _dot(dta_t, (r <= c).astype(F32), precision=HIGHEST)
    a_last = jnp.sum(dta_t, axis=1, keepdims=True)
    return dt, a_cum, a_cum_t, a_last


def _ssd_pair(x_pair, bg, cg, hp, dt, a_cum, a_cum_t, a_last, *, e0):
    ln = SSD_CHUNK
    lane = lax.broadcasted_iota(jnp.int32, (ln, LANES), 1)
    sub = lax.broadcasted_iota(jnp.int32, (LANES, SSD_STATE), 0)
    row = lax.broadcasted_iota(jnp.int32, (ln, ln), 0)
    col = lax.broadcasted_iota(jnp.int32, (ln, ln), 1)
    lo = lane < SSD_HEAD_DIM
    e1 = e0 + 1
    c0, c1 = a_cum[:, e0:e0 + 1], a_cum[:, e1:e1 + 1]
    r0, r1 = a_cum_t[e0:e0 + 1, :], a_cum_t[e1:e1 + 1, :]
    l0, l1 = a_last[e0:e0 + 1, :], a_last[e1:e1 + 1, :]
    xd = x_pair * jnp.where(lo, dt[:, e0:e0 + 1], dt[:, e1:e1 + 1])
    causal = row >= col
    cb = _bdot(cg, bg, NT)
    m0 = cb * jnp.exp(jnp.where(causal, c0 - r0, -jnp.inf))
    m1 = cb * jnp.exp(jnp.where(causal, c1 - r1, -jnp.inf))
    y = _bdot(m0, jnp.where(lo, xd, 0.0)) + _bdot(m1, jnp.where(lo, 0.0, xd))
    acum_pair = jnp.where(lo, c0, c1)
    y = y + _bdot(cg, hp, NT) * jnp.exp(acum_pair)
    last_pair = jnp.where(lo, l0, l1)
    st = _bdot(xd * jnp.exp(last_pair - acum_pair), bg, TN)
    h_out = hp * jnp.exp(jnp.where(sub < SSD_HEAD_DIM, l0, l1)) + st
    return y, h_out


def _ssd_in_specs(chunk_of):
    return [
        pl.BlockSpec((SSD_CHUNK, SSD_CONV_CH), lambda i: (chunk_of(i), 0)),
        pl.BlockSpec((SSD_CHUNK, HPAD), lambda i: (chunk_of(i), 0)),
        pl.BlockSpec((HPAD, SSD_CHUNK), lambda i: (0, chunk_of(i))),
        pl.BlockSpec((1, HPAD), lambda i: (0, 0)), pl.BlockSpec((HPAD, 1), lambda i: (0, 0)),
        pl.BlockSpec((1, HPAD), lambda i: (0, 0)), pl.BlockSpec((HPAD, 1), lambda i: (0, 0)),
    ]


def ssd_fwd(xc, dtr, dtr_t, dtb, dtb_t, alog, alog_t, name):
    def body(xc_ref, dtr_ref, dtrt_ref, dtb_ref, dtbt_ref, al_ref, alt_ref, y_ref, hs_ref, h_scr):
        @pl.when(pl.program_id(0) == 0)
        def _():
            h_scr[...] = jnp.zeros_like(h_scr)

        hs_ref[0] = h_scr[...]
        dt, a_cum, a_cum_t, a_last = _ssd_prefix(dtr_ref[...], dtrt_ref[...], dtb_ref[...], dtbt_ref[...],
                                                  al_ref[...], alt_ref[...])
        for pr in range(N_PAIRS):
            g = pr // PAIRS_PER_GROUP
            sl = slice(pr * LANES, (pr + 1) * LANES)
            bg = xc_ref[:, B_OFF + g * SSD_STATE:B_OFF + (g + 1) * SSD_STATE]
            cg = xc_ref[:, C_OFF + g * SSD_STATE:C_OFF + (g + 1) * SSD_STATE]
            y, h_out = _ssd_pair(xc_ref[:, sl], bg, cg, h_scr[sl, :], dt, a_cum, a_cum_t, a_last, e0=2 * pr)
            y_ref[:, sl] = y
            h_scr[sl, :] = h_out

    return pl.pallas_call(
        body, name=name, grid=(N_CHUNKS,), in_specs=_ssd_in_specs(lambda i: i),
        out_specs=[pl.BlockSpec((SSD_CHUNK, SSD_INNER), lambda i: (i, 0)),
                   pl.BlockSpec((1, SSD_INNER, SSD_STATE), lambda i: (i, 0, 0))],
        out_shape=[SDS((SEQ, SSD_INNER), F32), SDS((N_CHUNKS, SSD_INNER, SSD_STATE), F32)],
        scratch_shapes=[pltpu.VMEM((SSD_INNER, SSD_STATE), F32)],
        compiler_params=_cparams(("arbitrary",)),
    )(xc, dtr, dtr_t, dtb, dtb_t, alog, alog_t)


def ssd_bwd(xc, dtr, dtr_t, dtb, dtb_t, alog, alog_t, hs, dy, dxs_extra, name):
    rev = lambda i: N_CHUNKS - 1 - i

    def body(xc_ref, dtr_ref, dtrt_ref, dtb_ref, dtbt_ref, al_ref, alt_ref, hs_ref, dy_ref, dxe_ref,
             dxc_ref, ddtr_ref, ddtrt_ref, ddtb_ref, ddtbt_ref, dal_ref, dalt_ref, dh_scr):
        @pl.when(pl.program_id(0) == 0)
        def _():
            dh_scr[...] = jnp.zeros_like(dh_scr)
            for r in (ddtb_ref, ddtbt_ref, dal_ref, dalt_ref):
                r[...] = jnp.zeros_like(r)

        prefix_in = (dtr_ref[...], dtrt_ref[...], dtb_ref[...], dtbt_ref[...], al_ref[...], alt_ref[...])
        (dt, a_cum, a_cum_t, a_last), prefix_vjp = jax.vjp(_ssd_prefix, *prefix_in)
        d_dt = jnp.zeros_like(dt)
        d_acum = jnp.zeros_like(a_cum)
        d_acum_t = jnp.zeros_like(a_cum_t)
        d_alast = jnp.zeros_like(a_last)
        for g in range(SSD_GROUPS):
            bg = xc_ref[:, B_OFF + g * SSD_STATE:B_OFF + (g + 1) * SSD_STATE]
            cg = xc_ref[:, C_OFF + g * SSD_STATE:C_OFF + (g + 1) * SSD_STATE]
            d_bg = jnp.zeros_like(bg)
            d_cg = jnp.zeros_like(cg)
            for j in range(PAIRS_PER_GROUP):
                pr = g * PAIRS_PER_GROUP + j
                sl = slice(pr * LANES, (pr + 1) * LANES)
                _, vjp = jax.vjp(functools.partial(_ssd_pair, e0=2 * pr),
                                 xc_ref[:, sl], bg, cg, hs_ref[0, sl, :], dt, a_cum, a_cum_t, a_last)
                dx, dbg, dcg, dhp, ddt, dac, dact, dal = vjp((dy_ref[:, sl], dh_scr[sl, :]))
                dxc_ref[:, sl] = dx + dxe_ref[:, sl]
                dh_scr[sl, :] = dhp
                d_bg, d_cg = d_bg + dbg, d_cg + dcg
                d_dt, d_acum, d_acum_t, d_alast = d_dt + ddt, d_acum + dac, d_acum_t + dact, d_alast + dal
            dxc_ref[:, B_OFF + g * SSD_STATE:B_OFF + (g + 1) * SSD_STATE] = d_bg
            dxc_ref[:, C_OFF + g * SSD_STATE:C_OFF + (g + 1) * SSD_STATE] = d_cg
        g_dtr, g_dtrt, g_dtb, g_dtbt, g_al, g_alt = prefix_vjp((d_dt, d_acum, d_acum_t, d_alast))
        ddtr_ref[...] = g_dtr
        ddtrt_ref[...] = g_dtrt
        ddtb_ref[...] += g_dtb
        ddtbt_ref[...] += g_dtbt
        dal_ref[...] += g_al
        dalt_ref[...] += g_alt

    in_specs = _ssd_in_specs(rev) + [
        pl.BlockSpec((1, SSD_INNER, SSD_STATE), lambda i: (rev(i), 0, 0)),
        pl.BlockSpec((SSD_CHUNK, SSD_INNER), lambda i: (rev(i), 0)),
        pl.BlockSpec((SSD_CHUNK, SSD_INNER), lambda i: (rev(i), 0)),
    ]
    out_specs = [
        pl.BlockSpec((SSD_CHUNK, SSD_CONV_CH), lambda i: (rev(i), 0)),
        pl.BlockSpec((SSD_CHUNK, HPAD), lambda i: (rev(i), 0)),
        pl.BlockSpec((HPAD, SSD_CHUNK), lambda i: (0, rev(i))),
        pl.BlockSpec((1, HPAD), lambda i: (0, 0)), pl.BlockSpec((HPAD, 1), lambda i: (0, 0)),
        pl.BlockSpec((1, HPAD), lambda i: (0, 0)), pl.BlockSpec((HPAD, 1), lambda i: (0, 0)),
    ]
    out_shape = [SDS((SEQ, SSD_CONV_CH), F32), SDS((SEQ, HPAD), F32), SDS((HPAD, SEQ), F32),
                 SDS((1, HPAD), F32), SDS((HPAD, 1), F32), SDS((1, HPAD), F32), SDS((HPAD, 1), F32)]
    return pl.pallas_call(
        body, name=name, grid=(N_CHUNKS,), in_specs=in_specs, out_specs=out_specs, out_shape=out_shape,
        scratch_shapes=[pltpu.VMEM((SSD_INNER, SSD_STATE), F32)],
        compiler_params=_cparams(("arbitrary",)),
    )(xc, dtr, dtr_t, dtb, dtb_t, alog, alog_t, hs, dy, dxs_extra)


ATTN_SCALE = ATTN_HEAD_DIM ** -0.5


def _attn_scores(q, kp, kc, has_prev):
    qi = lax.broadcasted_iota(jnp.int32, (ATTN_BLOCK, ATTN_BLOCK), 0)
    kj = lax.broadcasted_iota(jnp.int32, (ATTN_BLOCK, ATTN_BLOCK), 1)
    s_c = jnp.where(qi >= kj, _bdot(q, kc, NT) * ATTN_SCALE, -jnp.inf)
    s_p = jnp.where((kj >= qi) & has_prev, _bdot(q, kp, NT) * ATTN_SCALE, -jnp.inf)
    return s_p, s_c


UNITS_PER_PATTERN = SEQ // ATTN_BLOCK
ATTN_UNROLL = 2


def _for_units(unit):
    for g, d in enumerate(ATTN_DILATIONS):
        nb = UNITS_PER_PATTERN // d
        span = d * ATTN_BLOCK

        def one(i, carry, g=g, d=d, nb=nb, span=span):
            r = i >> (nb.bit_length() - 1)
            n = i & (nb - 1)
            start = r + n * span
            prev = jnp.where(n > 0, start - span, start)
            unit(g, pl.ds(start, ATTN_BLOCK, stride=d), pl.ds(prev, ATTN_BLOCK, stride=d), n > 0)
            return carry
        lax.fori_loop(0, UNITS_PER_PATTERN, one, 0, unroll=ATTN_UNROLL)


def _head_specs(n_q_groups):
    blk = (SEQ, ATTN_HEAD_DIM)
    q_specs = [pl.BlockSpec(blk, functools.partial(lambda h, g: (0, g * ATTN_KV_HEADS + h), g=g)) for g in range(n_q_groups)]
    head = pl.BlockSpec(blk, lambda h: (0, h))
    table = pl.BlockSpec(blk, lambda h: (0, 0))
    return q_specs, head, table


def attn_fwd(q, k, v, tabs, name):
    q_specs, head, table = _head_specs(ATTN_N_PAT)

    def body(q0_ref, q1_ref, q2_ref, k_ref, v_ref, c_ref, sa_ref, sb_ref, y_ref, lse_ref, *scr):
        qs, og, ls, ks, vs = scr[0:3], scr[3:6], scr[6:9], scr[9], scr[10]
        c, sa, sb = c_ref[...], sa_ref[...], sb_ref[...]
        for g, q_ref in enumerate((q0_ref, q1_ref, q2_ref)):
            qs[g][...] = _rope(q_ref[...].astype(F32), c, sa, sb)
        ks[...] = _rope(k_ref[...].astype(F32), c, sa, sb)
        vs[...] = v_ref[...].astype(F32)

        def unit(g, rows, prows, has_prev):
            s_p, s_c = _attn_scores(qs[g][rows, :], ks[prows, :], ks[rows, :], has_prev)
            m = jnp.maximum(jnp.max(s_c, axis=1, keepdims=True), jnp.max(s_p, axis=1, keepdims=True))
            p_c, p_p = jnp.exp(s_c - m), jnp.exp(s_p - m)
            l = jnp.sum(p_c, axis=1, keepdims=True) + jnp.sum(p_p, axis=1, keepdims=True)
            o = _bdot(p_c, vs[rows, :]) + _bdot(p_p, vs[prows, :])
            og[g][rows, :] = o / l
            ls[g][rows, :] = jnp.broadcast_to(m + jnp.log(l), (ATTN_BLOCK, LANES))

        _for_units(unit)
        l0, l1, l2 = ls[0][...], ls[1][...], ls[2][...]
        m = jnp.maximum(jnp.maximum(l0, l1), l2)
        e0, e1, e2 = jnp.exp(l0 - m), jnp.exp(l1 - m), jnp.exp(l2 - m)
        den = e0 + e1 + e2
        y_ref[...] = ((e0 * og[0][...] + e1 * og[1][...] + e2 * og[2][...]) / den).astype(y_ref.dtype)
        lse_ref[...] = m + jnp.log(den)

    blk = (SEQ, ATTN_HEAD_DIM)
    return pl.pallas_call(
        body, name=name, grid=(ATTN_KV_HEADS,), in_specs=[*q_specs, head, head, table, table, table],
        out_specs=[head, head], out_shape=[SDS((SEQ, ATTN_OUT), BF16), SDS((SEQ, ATTN_OUT), F32)],
        scratch_shapes=[pltpu.VMEM(blk, F32)] * (3 * ATTN_N_PAT + 2),
        compiler_params=_cparams(("parallel",)),
    )(q, q, q, k, v, *tabs)


def attn_bwd(q, k, v, tabs, y, lse, dy, name):
    q_specs, head, table = _head_specs(ATTN_N_PAT)

    def body(q0_ref, q1_ref, q2_ref, k_ref, v_ref, c_ref, sa_ref, sb_ref, y_ref, lse_ref, dy_ref,
             dq0_ref, dq1_ref, dq2_ref, dk_ref, dv_ref, *scr):
        qs, dqs, ks, dks, dd, dvs, vs = scr[0:3], scr[3:6], scr[6], scr[7], scr[8], scr[9], scr[10]
        c, sa, sb = c_ref[...], sa_ref[...], sb_ref[...]
        for g, q_ref in enumerate((q0_ref, q1_ref, q2_ref)):
            qs[g][...] = _rope(q_ref[...].astype(F32), c, sa, sb)
        ks[...] = _rope(k_ref[...].astype(F32), c, sa, sb)
        vs[...] = v_ref[...].astype(F32)
        dks[...] = jnp.zeros_like(dks)
        dvs[...] = jnp.zeros_like(dvs)
        dyv = dy_ref[...]
        dd[...] = jnp.broadcast_to(jnp.sum(dyv * y_ref[...].astype(F32), axis=1, keepdims=True), dd.shape)

        def unit(g, rows, prows, has_prev):
            qv = qs[g][rows, :].astype(BF16)
            kc, kp = ks[rows, :].astype(BF16), ks[prows, :].astype(BF16)
            vc, vp = vs[rows, :].astype(BF16), vs[prows, :].astype(BF16)
            do = dy_ref[rows, :].astype(BF16)
            s_p, s_c = _attn_scores(qv, kp, kc, has_prev)
            lse_u = lse_ref[rows, :][:, 0:1]
            dsum = dd[rows, :][:, 0:1]
            p_c, p_p = jnp.exp(s_c - lse_u), jnp.exp(s_p - lse_u)
            ds_c = (p_c * (_dot(do, vc, NT) - dsum) * ATTN_SCALE).astype(BF16)
            ds_p = (p_p * (_dot(do, vp, NT) - dsum) * ATTN_SCALE).astype(BF16)
            dqs[g][rows, :] = _dot(ds_c, kc) + _dot(ds_p, kp)
            dks[rows, :] += _dot(ds_c, qv, TN)
            dks[prows, :] += _dot(ds_p, qv, TN)
            dvs[rows, :] += _bdot(p_c, do, TN)
            dvs[prows, :] += _bdot(p_p, do, TN)

        _for_units(unit)
        for g, dq_ref in enumerate((dq0_ref, dq1_ref, dq2_ref)):
            dq_ref[...] = _rope(dqs[g][...], c, -sa, -sb).astype(dq_ref.dtype)
        dk_ref[...] = _rope(dks[...], c, -sa, -sb).astype(dk_ref.dtype)
        dv_ref[...] = dvs[...].astype(dv_ref.dtype)

    blk = (SEQ, ATTN_HEAD_DIM)
    out = SDS((SEQ, ATTN_OUT), BF16)
    return pl.pallas_call(
        body, name=name, grid=(ATTN_KV_HEADS,), in_specs=[*q_specs, head, head, table, table, table, head, head, head],
        out_specs=[head] * 5, out_shape=[out] * 5,
        scratch_shapes=[pltpu.VMEM(blk, F32)] * (2 * ATTN_N_PAT + 5),
        compiler_params=_cparams(("parallel",)),
    )(q, q, q, k, v, *tabs, y, lse, dy)


def layer_fwd(h, getw, small, tabs, li):
    n = f"l{li}_"
    sv = {}
    w = dict(getw(0, h))
    u = rms_fwd(h, small["norm_mix"], n + "rms_mix")
    z = matmul(u, w["w_z"], name=n + "mm_z", tb=True, out_dtype=BF16)
    xbc = matmul(u, w["w_xbc"], name=n + "mm_xbc", tb=True, out_dtype=BF16)
    dtr = matmul(u, w["w_dt"], name=n + "mm_dt", tb=True)
    q = matmul(u, w["w_q"], name=n + "mm_q", tb=True, out_dtype=BF16)
    k = matmul(u, w["w_k"], name=n + "mm_k", tb=True, out_dtype=BF16)
    v = matmul(u, w["w_v"], name=n + "mm_v", tb=True, out_dtype=BF16)
    gs = matmul(u, w["w_gs"], name=n + "mm_gs", tb=True, out_dtype=BF16)
    ga = matmul(u, w["w_ga"], name=n + "mm_ga", tb=True, out_dtype=BF16)
    xc = conv_fwd(xbc, w["conv_w"], small["conv_b"], n + "conv")
    dtr_t = dtr.T
    y_ssd, hs = ssd_fwd(xc, dtr, dtr_t, small["dt_bias"], small["dt_bias"].T, small["a_log"], small["a_log"].T, n + "ssd")
    yn = ssd_post_fwd(y_ssd, xc, z, small["d_skip_x"], small["ssd_norm"], n + "ssd_post")
    y_attn, lse = attn_fwd(q, k, v, tabs, n + "attn")
    w.update(getw(1, y_ssd))
    a = matmul(yn, w["w_ssd_branch"], name=n + "mm_a", out_dtype=BF16)
    b = matmul(y_attn, w["w_attn_branch"], name=n + "mm_b", out_dtype=BF16)
    merged = gate_fwd(a, b, gs, ga, n + "gate")
    h1 = matmul(merged, w["w_out"], name=n + "mm_o", add=h)
    w.update(getw(2, h1))
    u2 = rms_fwd(h1, small["norm_ffn"], n + "rms_ffn")
    gu = matmul(u2, w["w_gate_up"], name=n + "mm_gu", tb=True, out_dtype=BF16)
    act = swiglu_fwd(gu, n + "swiglu")
    h2 = matmul(act, w["w_down"], name=n + "mm_down", add=h1)
    sv.update(h=h, u=u, z=z, xbc=xbc, dtr=dtr, dtr_t=dtr_t, gs=gs, ga=ga, xc=xc, y_ssd=y_ssd, hs=hs, yn=yn,
              q=q, k=k, v=v, y_attn=y_attn, lse=lse, a=a, b=b, merged=merged, h1=h1, u2=u2, gu=gu, act=act, w=w)
    return h2, sv


def layer_bwd(dh, sv, small, tabs, li, emit):
    n = f"l{li}_b_"
    w = sv["w"]
    gw, gsm = {}, {}
    dact = matmul(dh, w["w_down"], name=n + "mm_dact", tb=True, out_dtype=BF16)
    gw["w_down"] = matmul(sv["act"], dh, name=n + "mm_dwdown", ta=True, out_dtype=BF16)
    dgu = swiglu_bwd(sv["gu"], dact, n + "swiglu")
    gw["w_gate_up"] = matmul(dgu, sv["u2"], name=n + "mm_dwgu", ta=True, out_dtype=BF16)
    tok = emit(2, gw)
    du2 = matmul(dgu, w["w_gate_up"], name=n + "mm_du2")
    dh1, gsm["norm_ffn"] = rms_bwd(sv["h1"], du2, dh, small["norm_ffn"] + tok, n + "rms_ffn")
    dmerged = matmul(dh1, w["w_out"], name=n + "mm_dmerged", tb=True)
    gw["w_out"] = matmul(sv["merged"], dh1, name=n + "mm_dwo", ta=True, out_dtype=BF16)
    da, db, dgs, dga = gate_bwd(sv["a"], sv["b"], sv["gs"], sv["ga"], dmerged, n + "gate")
    gw["w_ssd_branch"] = matmul(sv["yn"], da, name=n + "mm_dwa", ta=True, out_dtype=BF16)
    gw["w_attn_branch"] = matmul(sv["y_attn"], db, name=n + "mm_dwb", ta=True, out_dtype=BF16)
    tok = emit(1, gw)
    dyn = matmul(da, w["w_ssd_branch"], name=n + "mm_dyn", tb=True)
    dyattn = matmul(db, w["w_attn_branch"], name=n + "mm_dyattn", tb=True)
    dy_ssd, dxs_extra, dz, gsm["d_skip_x"], gsm["ssd_norm"] = ssd_post_bwd(
        sv["y_ssd"], sv["xc"], sv["z"], small["d_skip_x"] + tok, small["ssd_norm"], dyn, n + "ssd_post")
    dxc, ddtr, ddtr_t, ddtb, ddtb_t, dal, dal_t = ssd_bwd(
        sv["xc"], sv["dtr"], sv["dtr_t"], small["dt_bias"], small["dt_bias"].T, small["a_log"], small["a_log"].T,
        sv["hs"], dy_ssd, dxs_extra, n + "ssd")
    ddtr = (ddtr + ddtr_t.T).astype(BF16)
    gsm["dt_bias"] = ddtb + ddtb_t.T
    gsm["a_log"] = dal + dal_t.T
    dxbc, gw["conv_w"], gsm["conv_b"] = conv_bwd(sv["xbc"], w["conv_w"], small["conv_b"], dxc, n + "conv")
    dq0, dq1, dq2, dk, dv = attn_bwd(sv["q"], sv["k"], sv["v"], tabs, sv["y_attn"], sv["lse"], dyattn, n + "attn")
    u = sv["u"]
    segs = [("w_z", dz), ("w_xbc", dxbc), ("w_dt", ddtr), ("w_q0", dq0), ("w_q1", dq1), ("w_q2", dq2),
            ("w_k", dk), ("w_v", dv), ("w_gs", dgs), ("w_ga", dga)]
    gin = [matmul(dseg, u, name=n + "mm_d" + key, ta=True, out_dtype=BF16) for key, dseg in segs]
    gin[2] = gin[2][:SSD_HEADS]
    gw["w_in"] = jnp.concatenate(gin, axis=0)
    tok = emit(0, gw)
    du = jnp.zeros((SEQ, D_MODEL), F32) + tok
    for key, dseg in segs:
        du = matmul(dseg, w[key], name=n + "mm_du_" + key, add=du)
    dh0, gsm["norm_mix"] = rms_bwd(sv["h"], du, dh1, small["norm_mix"] + tok, n + "rms_mix")
    return dh0, gsm


def _my_place():
    return lax.axis_index("x"), lax.axis_index("y"), lax.axis_index("c")


def _flip(place, k):
    x, y, c = place
    return (1 - x if k & 4 else x, 1 - y if k & 2 else y, 1 - c if k & 1 else c)


def _index(place):
    return 4 * place[0] + 2 * place[1] + place[2]


ANY = pl.BlockSpec(memory_space=pl.ANY)
CHIP_FLIPS = (4, 2, 6)


def all_gather(xs, name):
    na = len(xs)

    def body(*refs):
        x_refs, o_refs = refs[:na], refs[na:2 * na]
        send_sems, recv_sems, local_sems = refs[2 * na:]
        me = _my_place()
        sibling = _flip(me, 1)
        chips = [_flip(me, f) for f in CHIP_FLIPS]

        def copy(a, kk, block, to, src=None):
            dst = o_refs[a].at[_index(block)]
            return pltpu.make_async_remote_copy(
                src_ref=dst if src is None else src, dst_ref=dst, send_sem=send_sems.at[a, kk],
                recv_sem=recv_sems.at[a, kk], device_id=to, device_id_type=MESH)

        mine = [pltpu.make_async_copy(x_refs[a], o_refs[a].at[_index(me)], local_sems.at[a]) for a in range(na)]
        for cp in mine:
            cp.start()
        first = []
        for j, chip in enumerate(chips):
            first += [copy(a, 1 + j, me, chip, src=x_refs[a]) for a in range(na)]
        first += [copy(a, 0, me, sibling, src=x_refs[a]) for a in range(na)]
        for cp in first:
            cp.start()
        passed = []
        for j, chip in enumerate(chips):
            for a in range(na):
                copy(a, 1 + j, chip, me).wait_recv()
                cp = copy(a, 4 + j, chip, sibling)
                cp.start()
                passed.append(cp)
        for a in range(na):
            copy(a, 0, sibling, me).wait_recv()
        for j, chip in enumerate(chips):
            for a in range(na):
                copy(a, 4 + j, _flip(chip, 1), me).wait_recv()
        for cp in first + passed:
            cp.wait_send()
        for cp in mine:
            cp.wait()

    return pl.pallas_call(
        body, name=name, in_specs=[ANY] * na, out_specs=[ANY] * na,
        out_shape=[SDS((N_DEV,) + t.shape, t.dtype) for t in xs],
        scratch_shapes=[pltpu.SemaphoreType.DMA((na, N_DEV - 1)), pltpu.SemaphoreType.DMA((na, N_DEV - 1)),
                        pltpu.SemaphoreType.DMA((na,))],
    )(*xs)


HBM = pl.BlockSpec(memory_space=pltpu.HBM)
SEM = pl.BlockSpec(memory_space=pltpu.SEMAPHORE)
EFFECT = pltpu.SideEffectType.DATAFLOW_SIDE_EFFECTING
N_PEERS = N_DEV - 1


def _split_copy(src_ref, land_ref, send_sem, recv_sem, me, kk, scatter, landed_from_peer):
    peer = _flip(me, kk)
    src = src_ref.at[_index(peer)] if scatter else src_ref
    dst = land_ref.at[_index(peer if landed_from_peer else me)]
    return pltpu.make_async_remote_copy(src_ref=src, dst_ref=dst, send_sem=send_sem, recv_sem=recv_sem,
                                        device_id=peer, device_id_type=MESH)


def exchange_start(srcs, lands, group_sizes, scatter, name):
    na, ng = len(srcs), len(group_sizes)

    def body(*refs):
        s_refs, l_refs = refs[:na], refs[na:2 * na]
        sems = refs[2 * na:2 * na + 2 * ng]
        token = refs[-1]
        me = _my_place()
        a = 0
        for gi, gsz in enumerate(group_sizes):
            for j in range(gsz):
                for kk in range(1, N_DEV):
                    slot = j * N_PEERS + kk - 1
                    _split_copy(s_refs[a], l_refs[a], sems[2 * gi].at[slot], sems[2 * gi + 1].at[slot],
                                me, kk, scatter, False).start()
                a += 1
        token[...] = jnp.zeros_like(token)

    sem_shapes = []
    for gsz in group_sizes:
        sem_shapes += [pltpu.SemaphoreType.DMA((gsz * N_PEERS,))] * 2
    ins = [pltpu.with_memory_space_constraint(t, pltpu.HBM) for t in (*srcs, *lands)]
    res = pl.pallas_call(
        body, name=name, in_specs=[HBM] * (2 * na),
        out_specs=[SEM] * (2 * ng) + [HBM] * (2 * na) + [pl.BlockSpec(memory_space=pltpu.VMEM)],
        out_shape=sem_shapes + [pltpu.HBM(t.shape, t.dtype) for t in ins] + [SDS((8, LANES), F32)],
        input_output_aliases={i: 2 * ng + i for i in range(2 * na)},
        compiler_params=pltpu.CompilerParams(has_side_effects=EFFECT),
    )(*ins)
    sems = [(res[2 * gi], res[2 * gi + 1]) for gi in range(ng)]
    thru = res[2 * ng:2 * ng + 2 * na]
    return sems, thru[:na], thru[na:], res[-1]


def exchange_wait(srcs, lands, sems, after, scatter, name):
    n = len(srcs)

    def body(*refs):
        s_refs, l_refs = refs[:n], refs[n:2 * n]
        send_sems, recv_sems = refs[2 * n], refs[2 * n + 1]
        me = _my_place()
        for j in range(n):
            for kk in range(1, N_DEV):
                slot = j * N_PEERS + kk - 1
                cp = _split_copy(s_refs[j], l_refs[j], send_sems.at[slot], recv_sems.at[slot], me, kk, scatter, True)
                cp.wait_send()
                cp.wait_recv()

    res = pl.pallas_call(
        body, name=name, in_specs=[HBM] * (2 * n) + [SEM, SEM, ANY], out_specs=[HBM] * (2 * n),
        out_shape=[pltpu.HBM(t.shape, t.dtype) for t in (*srcs, *lands)],
        input_output_aliases={i: i for i in range(2 * n)},
        compiler_params=pltpu.CompilerParams(has_side_effects=EFFECT),
    )(*srcs, *lands, sems[0], sems[1], after)
    return res[n:]


def landing_zone(block, me_index):
    land = lax.empty((N_DEV,) + block.shape, block.dtype)
    return lax.dynamic_update_slice(land, block[None], (me_index,) + (0,) * block.ndim)


def sum_parts(parts, name):
    _, r, c = parts.shape
    tc = _pick(c, (256, 128))

    def body(p_ref, o_ref):
        acc = p_ref[0].astype(F32)
        for i in range(1, N_DEV):
            acc = acc + p_ref[i].astype(F32)
        o_ref[...] = acc

    return pl.pallas_call(
        body, name=name, grid=(c // tc,), in_specs=[pl.BlockSpec((N_DEV, r, tc), lambda i: (0, 0, i))],
        out_specs=pl.BlockSpec((r, tc), lambda i: (0, i)), out_shape=SDS((r, c), F32),
        compiler_params=_cparams(("parallel",)),
    )(parts)


ADAMW_BLOCK_BYTES = 2 * 1024 * 1024


def adamw(w, g, m, v, name):
    shape = w.shape
    lay, rows, cols = ((1, 1) + shape)[-3:]
    tr = _pick(rows, (256, 128))
    tc = cols if tr * cols * 4 <= ADAMW_BLOCK_BYTES else _pick(cols, (256, 128))
    c1 = 1.0 / (1.0 - ADAM_B1 ** ADAM_STEP)
    c2 = 1.0 / (1.0 - ADAM_B2 ** ADAM_STEP)

    def body(w_ref, g_ref, m_ref, v_ref, d_ref, nm_ref, nv_ref):
        gg = g_ref[...]
        nm = ADAM_B1 * m_ref[...] + (1.0 - ADAM_B1) * gg
        nv = ADAM_B2 * v_ref[...] + (1.0 - ADAM_B2) * (gg * gg)
        d_ref[...] = -ADAM_LR * ((nm * c1) / (jnp.sqrt(nv * c2) + ADAM_EPS) + ADAM_WD * w_ref[...])
        nm_ref[...] = nm
        nv_ref[...] = nv

    spec = pl.BlockSpec((1, tr, tc), lambda l, i, j: (l, i, j))
    outs = pl.pallas_call(
        body, name=name, grid=(lay, rows // tr, cols // tc), in_specs=[spec] * 4, out_specs=[spec] * 3,
        out_shape=[SDS((lay, rows, cols), F32)] * 3, compiler_params=_cparams(("parallel",) * 3),
    )(*[t.reshape(lay, rows, cols) for t in (w, g, m, v)])
    return [o.reshape(shape) for o in outs]


def adamw_layer_inner(w, gs, m, v, name):
    rows, lay, cols = w.shape
    tc = LANES
    c1 = 1.0 / (1.0 - ADAM_B1 ** ADAM_STEP)
    c2 = 1.0 / (1.0 - ADAM_B2 ** ADAM_STEP)

    def body(*refs):
        w_ref, m_ref, v_ref = refs[:3]
        g_refs = refs[3:3 + lay]
        go_ref, d_ref, nm_ref, nv_ref = refs[3 + lay:]
        for l, g_ref in enumerate(g_refs):
            gg = g_ref[...]
            nm = ADAM_B1 * m_ref[:, l, :] + (1.0 - ADAM_B1) * gg
            nv = ADAM_B2 * v_ref[:, l, :] + (1.0 - ADAM_B2) * (gg * gg)
            d_ref[:, l, :] = -ADAM_LR * ((nm * c1) / (jnp.sqrt(nv * c2) + ADAM_EPS) + ADAM_WD * w_ref[:, l, :])
            go_ref[:, l, :] = gg
            nm_ref[:, l, :] = nm
            nv_ref[:, l, :] = nv

    inner = pl.BlockSpec((rows, lay, tc), lambda j: (0, 0, j))
    plain = pl.BlockSpec((rows, tc), lambda j: (0, j))
    return pl.pallas_call(
        body, name=name, grid=(cols // tc,), in_specs=[inner] * 3 + [plain] * lay, out_specs=[inner] * 4,
        out_shape=[SDS((rows, lay, cols), F32)] * 4, compiler_params=_cparams(("parallel",)),
    )(w, m, v, *gs)


BIG = ("w_in", "conv_w", "w_ssd_branch", "w_attn_branch", "w_out", "w_gate_up", "w_down")
TRANSPOSED = ("w_in", "w_gate_up")
SMALL = ("norm_mix", "conv_b", "dt_bias", "a_log", "d_skip", "ssd_norm", "norm_ffn")
SMALL_SIZE = {"norm_mix": 1024, "conv_b": 3072, "dt_bias": 32, "a_log": 32, "d_skip": 32, "ssd_norm": 2048, "norm_ffn": 1024}
FLAT_W = 512
SMALL_TOTAL = DEPTH * sum(SMALL_SIZE.values()) + D_MODEL + LANES
SMALL_ROWS = 32
assert SMALL_ROWS * FLAT_W >= SMALL_TOTAL


GROUPS = (("w_in", "conv_w"), ("w_ssd_branch", "w_attn_branch", "w_out"), ("w_gate_up", "w_down"))


def to_wire(k, shard):
    if k in TRANSPOSED:
        return shard.T.astype(BF16)
    return shard if k == "conv_w" else shard.astype(BF16)


def full_weights(k, g):
    if k == "conv_w":
        return {k: g.transpose(1, 0, 2).reshape(SSD_CONV, SSD_CONV_CH)}
    full = g.reshape(-1, g.shape[-1])
    if k != "w_in":
        return {k: full}
    w, off = {}, 0
    for nm, r in IN_ROWS:
        w[nm] = full[off:off + r]
        off += r
    w["w_q"] = full[sum(r for _, r in IN_ROWS[:3]):sum(r for _, r in IN_ROWS[:6])]
    w["w_dt"] = jnp.pad(w["w_dt"], ((0, HPAD - SSD_HEADS), (0, 0)))
    return w


def grads_to_wire(k, g):
    if k == "conv_w":
        return g.reshape(SSD_CONV, N_DEV, SSD_CONV_CH // N_DEV).transpose(1, 0, 2)
    return g.reshape(N_DEV, g.shape[0] // N_DEV, g.shape[1])


def _pad_heads(t):
    return jnp.pad(t.reshape(1, SSD_HEADS), ((0, 0), (0, HPAD - SSD_HEADS)))


def local_step(x, target, getw, emit, smalls, norm_final):
    tabs = rope_tables()
    sms = []
    for li in range(DEPTH):
        s = smalls[li]
        sms.append({
            "norm_mix": s["norm_mix"].reshape(1, -1), "conv_b": s["conv_b"].reshape(1, -1),
            "dt_bias": _pad_heads(s["dt_bias"]), "a_log": _pad_heads(s["a_log"]),
            "d_skip_x": jnp.repeat(s["d_skip"], SSD_HEAD_DIM).reshape(1, -1),
            "ssd_norm": s["ssd_norm"].reshape(1, -1), "norm_ffn": s["norm_ffn"].reshape(1, -1)})
    h = x
    saved = []
    for li in range(DEPTH):
        h, sv = layer_fwd(h, functools.partial(getw, li), sms[li], tabs, li)
        saved.append(sv)
    dh, g_final, loss = loss_head(h, target, norm_final.reshape(1, -1), "loss_head")
    gsms = [None] * DEPTH
    for li in reversed(range(DEPTH)):
        dh, gsm = layer_bwd(dh, saved[li], sms[li], tabs, li, functools.partial(emit, li))
        gsms[li] = {
            "norm_mix": gsm["norm_mix"].reshape(-1), "conv_b": gsm["conv_b"].reshape(-1),
            "dt_bias": gsm["dt_bias"][0, :SSD_HEADS], "a_log": gsm["a_log"][0, :SSD_HEADS],
            "d_skip": gsm["d_skip_x"].reshape(SSD_HEADS, SSD_HEAD_DIM).sum(axis=1),
            "ssd_norm": gsm["ssd_norm"].reshape(-1), "norm_ffn": gsm["norm_ffn"].reshape(-1)}
    return loss, dh, gsms, g_final.reshape(-1)


def kernel(x, norm_mix, w_in, conv_w, conv_b, dt_bias, a_log, d_skip, ssd_norm, w_ssd_branch, w_attn_branch, w_out, norm_ffn, w_gate_up, w_down, norm_final, loss_target, m_norm_mix, m_w_in, m_conv_w, m_conv_b, m_dt_bias, m_a_log, m_d_skip, m_ssd_norm, m_w_ssd_branch, m_w_attn_branch, m_w_out, m_norm_ffn, m_w_gate_up, m_w_down, m_norm_final, v_norm_mix, v_w_in, v_conv_w, v_conv_b, v_dt_bias, v_a_log, v_d_skip, v_ssd_norm, v_w_ssd_branch, v_w_attn_branch, v_w_out, v_norm_ffn, v_w_gate_up, v_w_down, v_norm_final):
    wv = dict(norm_mix=norm_mix, w_in=w_in, conv_w=conv_w, conv_b=conv_b, dt_bias=dt_bias, a_log=a_log, d_skip=d_skip,
              ssd_norm=ssd_norm, w_ssd_branch=w_ssd_branch, w_attn_branch=w_attn_branch, w_out=w_out, norm_ffn=norm_ffn,
              w_gate_up=w_gate_up, w_down=w_down, norm_final=norm_final)
    mv = dict(norm_mix=m_norm_mix, w_in=m_w_in, conv_w=m_conv_w, conv_b=m_conv_b, dt_bias=m_dt_bias, a_log=m_a_log,
              d_skip=m_d_skip, ssd_norm=m_ssd_norm, w_ssd_branch=m_w_ssd_branch, w_attn_branch=m_w_attn_branch,
              w_out=m_w_out, norm_ffn=m_norm_ffn, w_gate_up=m_w_gate_up, w_down=m_w_down, norm_final=m_norm_final)
    vv = dict(norm_mix=v_norm_mix, w_in=v_w_in, conv_w=v_conv_w, conv_b=v_conv_b, dt_bias=v_dt_bias, a_log=v_a_log,
              d_skip=v_d_skip, ssd_norm=v_ssd_norm, w_ssd_branch=v_w_ssd_branch, w_attn_branch=v_w_attn_branch,
              w_out=v_w_out, norm_ffn=v_norm_ffn, w_gate_up=v_w_gate_up, w_down=v_w_down, norm_final=v_norm_final)
    order = ("norm_mix", "w_in", "conv_w", "conv_b", "dt_bias", "a_log", "d_skip", "ssd_norm", "w_ssd_branch",
             "w_attn_branch", "w_out", "norm_ffn", "w_gate_up", "w_down", "norm_final")

    me_index = _index(_my_place())
    smalls = [{k: wv[k][li] for k in SMALL} for li in range(DEPTH)]
    n_groups = len(GROUPS)

    first_lands = all_gather([to_wire(k, wv[k][0]) for k in GROUPS[0]], "gather_first")
    later = [(li, gi) for li in range(DEPTH) for gi in range(n_groups)][1:]
    behind_first = first_lands[1][0, 0, 0] * 0.0
    srcs = [to_wire(k, wv[k][li] + behind_first if k == "conv_w" else wv[k][li]) for li, gi in later for k in GROUPS[gi]]
    sizes = [len(GROUPS[gi]) for _, gi in later]
    w_sems, w_srcs, w_lands, token = exchange_start(srcs, [landing_zone(s, me_index) for s in srcs], sizes, False, "gather_start")
    smalls[0]["norm_mix"] = smalls[0]["norm_mix"] + token[0, 0]

    def getw(li, gi, after):
        if (li, gi) == (0, 0):
            lands = first_lands
        else:
            slot = later.index((li, gi))
            sl = slice(sum(sizes[:slot]), sum(sizes[:slot + 1]))
            lands = exchange_wait(w_srcs[sl], w_lands[sl], w_sems[slot], after, False, f"gather_wait_{li}_{gi}")
        w = {}
        for k, land in zip(GROUPS[gi], lands):
            w.update(full_weights(k, land))
        return w

    pending = []

    def emit(li, gi, gw):
        parts = [grads_to_wire(k, gw[k]) for k in GROUPS[gi]]
        lands = [landing_zone(lax.dynamic_index_in_dim(p, me_index, 0, keepdims=False), me_index) for p in parts]
        sems, p_thru, l_thru, tok = exchange_start(parts, lands, [len(parts)], True, f"grads_start_{li}_{gi}")
        pending.append((li, gi, sems[0], p_thru, l_thru))
        return tok[0, 0]

    loss_p, dx, gsms, g_final = local_step(x[0], loss_target[0], getw, emit, smalls, norm_final)

    grads, deltas, new_m, new_v = {}, {}, {}, {}

    def update(k):
        if k == "w_in":
            inner = lambda t: t.transpose(2, 0, 1)
            outs = adamw_layer_inner(inner(wv[k]), shard_g[k], inner(mv[k]), inner(vv[k]), "adamw_" + k)
            grads[k], deltas[k], new_m[k], new_v[k] = (t.transpose(1, 2, 0) for t in outs)
            return outs[3]
        if k in BIG:
            grads[k] = jnp.stack([g.T if k in TRANSPOSED else g for g in shard_g[k]])
        deltas[k], new_m[k], new_v[k] = adamw(wv[k], grads[k], mv[k], vv[k], "adamw_" + k)
        return new_v[k]

    shard_g = {k: [None] * DEPTH for k in BIG}

    def collect(entry, after):
        li, gi, sems, p_thru, l_thru = entry
        recv = exchange_wait(p_thru, l_thru, sems, after, True, f"grads_wait_{li}_{gi}")
        for k, r in zip(GROUPS[gi], recv):
            if k == "conv_w":
                r = r.reshape(N_DEV, 1, -1)
            after = sum_parts(r, f"sum_{k}_{li}")
            shard_g[k][li] = after if k in TRANSPOSED else after.reshape(wv[k].shape[1:])
        return after

    after = dx
    for entry in pending[:-1]:
        after = collect(entry, after)
    done = [after[:1, :1].reshape(1)]
    for gi in (2, 1):
        for k in GROUPS[gi]:
            done.append(update(k).reshape(-1)[:1])

    flat = [gsms[li][k] for li in range(DEPTH) for k in SMALL] + [g_final, loss_p.reshape(-1)]
    flat.append(jnp.zeros((SMALL_ROWS * FLAT_W - SMALL_TOTAL,), F32))
    small_all = all_gather([jnp.concatenate(flat).reshape(SMALL_ROWS, FLAT_W)], "gather_small")[0]
    small_sum = sum_parts(small_all, "sum_small").reshape(-1)
    off = 0
    per_layer = {k: [] for k in SMALL}
    for li in range(DEPTH):
        for k in SMALL:
            per_layer[k].append(small_sum[off:off + SMALL_SIZE[k]])
            off += SMALL_SIZE[k]
    for k in SMALL:
        grads[k] = jnp.stack(per_layer[k])
    grads["norm_final"] = small_sum[off:off + D_MODEL]
    loss = small_sum[off + D_MODEL]
    for k in (*SMALL, "norm_final"):
        done.append(update(k).reshape(-1)[:1])

    collect(pending[-1], jnp.concatenate(done))
    for k in GROUPS[0]:
        update(k)

    return (loss, dx.reshape(x.shape), *[grads[k] for k in order], *[deltas[k] for k in order],
            *[new_m[k] for k in order], *[new_v[k] for k in order])
```

```python
import functools

import jax
import jax.numpy as jnp
from jax import lax
from jax.experimental import pallas as pl
from jax.experimental.pallas import tpu as pltpu

F32, BF16 = jnp.float32, jnp.bfloat16
SDS = jax.ShapeDtypeStruct
MESH = pl.DeviceIdType.MESH

D_MODEL = 1024
SEQ = 2048
DEPTH = 2
RMS_EPS = 1e-5
SSD_INNER = 2048
SSD_HEAD_DIM = 64
SSD_HEADS = 32
SSD_STATE = 128
SSD_GROUPS = 4
SSD_CONV = 4
SSD_CHUNK = 128
SSD_CONV_CH = 3072
ATTN_HEAD_DIM = 128
ATTN_KV_HEADS = 8
ATTN_DILATIONS = (1, 4, 16)
ATTN_N_PAT = 3
ATTN_BLOCK = 128
ATTN_OUT = 1024
ROPE_THETA = 500000.0
ROPE_DIM = 32
FFN_HIDDEN = 2816
ADAM_LR, ADAM_B1, ADAM_B2, ADAM_EPS, ADAM_WD, ADAM_STEP = 0.001, 0.9, 0.999, 1e-08, 0.01, 10

N_DEV = 8
LANES = 128
VMEM_LIMIT = 56 * 1024 * 1024
HPAD = 128
HIGHEST = lax.Precision.HIGHEST

IN_ROWS = (("w_z", 2048), ("w_xbc", 3072), ("w_dt", 32), ("w_q0", 1024), ("w_q1", 1024), ("w_q2", 1024),
           ("w_k", 1024), ("w_v", 1024), ("w_gs", 1024), ("w_ga", 1024))
N_IN = sum(r for _, r in IN_ROWS)


def _cparams(sem):
    return pltpu.CompilerParams(dimension_semantics=sem, vmem_limit_bytes=VMEM_LIMIT)


def _sigmoid(x):
    return 0.5 * jnp.tanh(0.5 * x) + 0.5


def _silu(x):
    return x * _sigmoid(x)


def _softplus(x):
    return jnp.maximum(x, 0.0) + jnp.log(1.0 + jnp.exp(-jnp.abs(x)))


def _dot(a, b, dims=(((1,), (0,)), ((), ())), precision=None):
    return lax.dot_general(a, b, dims, precision=precision, preferred_element_type=F32)


NT = (((1,), (1,)), ((), ()))
TN = (((0,), (0,)), ((), ()))


def _bdot(a, b, dims=(((1,), (0,)), ((), ()))):
    return _dot(a.astype(BF16), b.astype(BF16), dims)


def _pick(dim, cands):
    for c in cands:
        if dim % c == 0:
            return c
    return dim


def matmul(a, b, *, name, ta=False, tb=False, out_dtype=F32, add=None):
    m, k = (a.shape[1], a.shape[0]) if ta else a.shape
    n = b.shape[0] if tb else b.shape[1]
    tn = _pick(n, (1024, 1408, 512, 256, 128))
    tm = _pick(m, (512, 1408, 256, 128)) if tn == n else _pick(m, (1024, 1408, 512, 256, 128))
    tk = _pick(k, (1024, 1408, 512, 256, 128))
    nk = k // tk
    a_spec = pl.BlockSpec((tk, tm), lambda i, j, kk: (kk, i)) if ta else pl.BlockSpec((tm, tk), lambda i, j, kk: (i, kk))
    b_spec = pl.BlockSpec((tn, tk), lambda i, j, kk: (j, kk)) if tb else pl.BlockSpec((tk, tn), lambda i, j, kk: (kk, j))
    dims = (((0 if ta else 1,), (1 if tb else 0,)), ((), ()))
    has_add = add is not None

    def body(*refs):
        a_ref, b_ref = refs[:2]
        add_ref = refs[2] if has_add else None
        o_ref = refs[3] if has_add else refs[2]
        acc = refs[-1] if nk > 1 else None
        kk = pl.program_id(2)

        def product():
            return _dot(a_ref[...].astype(BF16), b_ref[...].astype(BF16), dims)

        def finish(r):
            if has_add:
                r = r + add_ref[...].astype(F32)
            o_ref[...] = r.astype(o_ref.dtype)

        if nk == 1:
            finish(product())
            return

        @pl.when(kk == 0)
        def _():
            acc[...] = product()

        @pl.when((kk > 0) & (kk < nk - 1))
        def _():
            acc[...] += product()

        @pl.when(kk == nk - 1)
        def _():
            finish(acc[...] + product())

    in_specs = [a_spec, b_spec]
    args = [a, b]
    if has_add:
        in_specs.append(pl.BlockSpec((tm, tn), lambda i, j, kk: (i, j)))
        args.append(add)
    return pl.pallas_call(
        body, name=name, grid=(m // tm, n // tn, nk),
        in_specs=in_specs, out_specs=pl.BlockSpec((tm, tn), lambda i, j, kk: (i, j)),
        out_shape=SDS((m, n), out_dtype), scratch_shapes=[pltpu.VMEM((tm, tn), F32)] if nk > 1 else [],
        compiler_params=_cparams(("parallel", "parallel", "arbitrary")),
    )(*args)


def rowcall(name, fn, rows, params, row_outs, red_outs=(), tr=256):
    s = rows[0].shape[0]
    n_in = len(rows) + len(params)
    n_row = len(row_outs)

    def body(*refs):
        outs = fn(*[r[...].astype(F32) for r in refs[:n_in]])
        if not isinstance(outs, (tuple, list)):
            outs = (outs,)
        orefs = refs[n_in:]
        for r, o in zip(orefs[:n_row], outs[:n_row]):
            r[...] = o.astype(r.dtype)
        if red_outs:
            @pl.when(pl.program_id(0) == 0)
            def _():
                for r in orefs[n_row:]:
                    r[...] = jnp.zeros_like(r)
            for r, o in zip(orefs[n_row:], outs[n_row:]):
                r[...] += o.astype(F32)

    in_specs = [pl.BlockSpec((tr, a.shape[1]), lambda i: (i, 0)) for a in rows]
    in_specs += [pl.BlockSpec(p.shape, lambda i: (0, 0)) for p in params]
    out_specs = [pl.BlockSpec((tr, c), lambda i: (i, 0)) for c, _ in row_outs]
    out_specs += [pl.BlockSpec(shp, lambda i: (0, 0)) for shp in red_outs]
    out_shape = [SDS((s, c), dt) for c, dt in row_outs] + [SDS(shp, F32) for shp in red_outs]
    res = pl.pallas_call(
        body, name=name, grid=(s // tr,), in_specs=in_specs, out_specs=out_specs, out_shape=out_shape,
        compiler_params=_cparams(("arbitrary",) if red_outs else ("parallel",)),
    )(*rows, *params)
    return res


def _rms(x, w):
    return x * lax.rsqrt(jnp.mean(x * x, axis=-1, keepdims=True) + RMS_EPS) * w


def rms_fwd(h, w, name):
    return rowcall(name, _rms, [h], [w], [(D_MODEL, BF16)])[0]


def rms_bwd(h, du, dres, w, name):
    def fn(hb, dub, dresb, wb):
        _, vjp = jax.vjp(_rms, hb, wb)
        dh, dw = vjp(dub)
        return dh + dresb, dw
    return rowcall(name, fn, [h, du, dres], [w], [(D_MODEL, F32)], [(1, D_MODEL)])


def loss_head(h, target, w, name):
    def fn(hb, tb, wb):
        def f(hh, ww):
            err = _rms(hh, ww) - tb
            return 0.5 * jnp.sum(jnp.mean(err * err, axis=-1, keepdims=True), axis=0, keepdims=True)
        val, vjp = jax.vjp(f, hb, wb)
        dh, dw = vjp(jnp.ones((1, 1), F32))
        return dh, dw, jnp.broadcast_to(val, (1, LANES))
    return rowcall(name, fn, [h, target], [w], [(D_MODEL, F32)], [(1, D_MODEL), (1, LANES)])


def _gate(a, b, gs, ga):
    return _sigmoid(gs) * a + _sigmoid(ga) * b


def gate_fwd(a, b, gs, ga, name):
    return rowcall(name, _gate, [a, b, gs, ga], [], [(D_MODEL, BF16)])[0]


def gate_bwd(a, b, gs, ga, dm, name):
    def fn(ab, bb, gsb, gab, dmb):
        _, vjp = jax.vjp(_gate, ab, bb, gsb, gab)
        return vjp(dmb)
    return rowcall(name, fn, [a, b, gs, ga, dm], [], [(D_MODEL, BF16)] * 4)


def _swiglu(gu):
    return _silu(gu[:, :FFN_HIDDEN]) * gu[:, FFN_HIDDEN:]


def swiglu_fwd(gu, name):
    return rowcall(name, _swiglu, [gu], [], [(FFN_HIDDEN, BF16)])[0]


def swiglu_bwd(gu, dact, name):
    def fn(gub, db):
        _, vjp = jax.vjp(_swiglu, gub)
        return vjp(db.astype(F32))[0]
    return rowcall(name, fn, [gu, dact], [], [(2 * FFN_HIDDEN, BF16)])[0]


def _ssd_post(y, xs, z, dskip, normw):
    y = (y + dskip * xs) * _silu(z)
    gw = SSD_INNER // SSD_GROUPS
    parts = []
    for g in range(SSD_GROUPS):
        yg = y[:, g * gw:(g + 1) * gw]
        parts.append(yg * lax.rsqrt(jnp.mean(yg * yg, axis=-1, keepdims=True) + RMS_EPS))
    return jnp.concatenate(parts, axis=-1) * normw


def ssd_post_fwd(y, xc, z, dskip, normw, name):
    def fn(yb, xcb, zb, db, nb):
        return _ssd_post(yb, xcb[:, :SSD_INNER], zb, db, nb)
    return rowcall(name, fn, [y, xc, z], [dskip, normw], [(SSD_INNER, BF16)])[0]


def ssd_post_bwd(y, xc, z, dskip, normw, dyn, name):
    def fn(yb, xcb, zb, dynb, db, nb):
        _, vjp = jax.vjp(_ssd_post, yb, xcb[:, :SSD_INNER], zb, db, nb)
        return vjp(dynb)
    return rowcall(name, fn, [y, xc, z, dyn], [dskip, normw],
                   [(SSD_INNER, F32), (SSD_INNER, F32), (SSD_INNER, BF16)], [(1, SSD_INNER), (1, SSD_INNER)])


def _rope(t, cosf, sina, sinb):
    return t * cosf + pltpu.roll(t, LANES - ROPE_DIM // 2, 1) * sina + pltpu.roll(t, ROPE_DIM // 2, 1) * sinb


def rope_tables():
    half = ROPE_DIM // 2
    inv = ROPE_THETA ** (-jnp.arange(0, ROPE_DIM, 2, dtype=F32) / ROPE_DIM)
    ang = jnp.arange(SEQ, dtype=F32)[:, None] * inv[None, :]
    cos, sin = jnp.cos(ang), jnp.sin(ang)
    zeros = jnp.zeros((SEQ, LANES - ROPE_DIM), F32)
    z16 = jnp.zeros((SEQ, half), F32)
    cosf = jnp.concatenate([cos, cos, jnp.ones((SEQ, LANES - ROPE_DIM), F32)], axis=1)
    sina = jnp.concatenate([-sin, z16, zeros], axis=1)
    sinb = jnp.concatenate([z16, sin, zeros], axis=1)
    return cosf, sina, sinb


CONV_TC = 256


def _conv_pre(x, w, b, row):
    acc = x * w[SSD_CONV - 1:SSD_CONV, :] + b
    shifted = [x]
    for j in range(1, SSD_CONV):
        xs = jnp.where(row >= j, pltpu.roll(x, j, 0), 0.0)
        shifted.append(xs)
        acc = acc + xs * w[SSD_CONV - 1 - j:SSD_CONV - j, :]
    return acc, shifted


def conv_fwd(xbc, w, b, name):
    def body(x_ref, w_ref, b_ref, o_ref):
        row = lax.broadcasted_iota(jnp.int32, (SEQ, CONV_TC), 0)
        pre, _ = _conv_pre(x_ref[...].astype(F32), w_ref[...], b_ref[...], row)
        o_ref[...] = _silu(pre)
    return pl.pallas_call(
        body, name=name, grid=(SSD_CONV_CH // CONV_TC,),
        in_specs=[pl.BlockSpec((SEQ, CONV_TC), lambda i: (0, i)), pl.BlockSpec((SSD_CONV, CONV_TC), lambda i: (0, i)),
                  pl.BlockSpec((1, CONV_TC), lambda i: (0, i))],
        out_specs=pl.BlockSpec((SEQ, CONV_TC), lambda i: (0, i)),
        out_shape=SDS((SEQ, SSD_CONV_CH), F32), compiler_params=_cparams(("parallel",)),
    )(xbc, w, b)


def conv_bwd(xbc, w, b, dxc, name):
    def body(x_ref, w_ref, b_ref, dy_ref, dx_ref, dw_ref, db_ref):
        row = lax.broadcasted_iota(jnp.int32, (SEQ, CONV_TC), 0)
        wv = w_ref[...]
        pre, shifted = _conv_pre(x_ref[...].astype(F32), wv, b_ref[...], row)
        sg = _sigmoid(pre)
        ds = dy_ref[...] * (sg * (1.0 + pre * (1.0 - sg)))
        dx = ds * wv[SSD_CONV - 1:SSD_CONV, :]
        for j in range(1, SSD_CONV):
            dsj = jnp.where(row < SEQ - j, pltpu.roll(ds, SEQ - j, 0), 0.0)
            dx = dx + dsj * wv[SSD_CONV - 1 - j:SSD_CONV - j, :]
        dx_ref[...] = dx.astype(dx_ref.dtype)
        for j in range(SSD_CONV):
            dw_ref[SSD_CONV - 1 - j:SSD_CONV - j, :] = jnp.sum(ds * shifted[j], axis=0, keepdims=True)
        db_ref[...] = jnp.sum(ds, axis=0, keepdims=True)
    return pl.pallas_call(
        body, name=name, grid=(SSD_CONV_CH // CONV_TC,),
        in_specs=[pl.BlockSpec((SEQ, CONV_TC), lambda i: (0, i)), pl.BlockSpec((SSD_CONV, CONV_TC), lambda i: (0, i)),
                  pl.BlockSpec((1, CONV_TC), lambda i: (0, i)), pl.BlockSpec((SEQ, CONV_TC), lambda i: (0, i))],
        out_specs=[pl.BlockSpec((SEQ, CONV_TC), lambda i: (0, i)), pl.BlockSpec((SSD_CONV, CONV_TC), lambda i: (0, i)),
                   pl.BlockSpec((1, CONV_TC), lambda i: (0, i))],
        out_shape=[SDS((SEQ, SSD_CONV_CH), BF16), SDS((SSD_CONV, SSD_CONV_CH), F32), SDS((1, SSD_CONV_CH), F32)],
        compiler_params=_cparams(("parallel",)),
    )(xbc, w, b, dxc)


N_CHUNKS = SEQ // SSD_CHUNK
N_PAIRS = SSD_HEADS // 2
PAIRS_PER_GROUP = N_PAIRS // SSD_GROUPS
B_OFF = SSD_INNER
C_OFF = SSD_INNER + SSD_GROUPS * SSD_STATE


def _ssd_prefix(dtr, dtr_t, dtb, dtb_t, alog, alog_t):
    ln = SSD_CHUNK
    dt = _softplus(dtr + dtb)
    dt_t = _softplus(dtr_t + dtb_t)
    dta = dt * (-jnp.exp(alog))
    dta_t = dt_t * (-jnp.exp(alog_t))
    r = lax.broadcasted_iota(jnp.int32, (ln, ln), 0)
    c = lax.broadcasted_iota(jnp.int32, (ln, ln), 1)
    a_cum = _dot((r >= c).astype(F32), dta, precision=HIGHEST)
    a_cum_t = _dot(dta_t, (r <= c).astype(F32), precision=HIGHEST)
    a_last = jnp.sum(dta_t, axis=1, keepdims=True)
    return dt, a_cum, a_cum_t, a_last


def _ssd_pair(x_pair, bg, cg, hp, dt, a_cum, a_cum_t, a_last, *, e0):
    ln = SSD_CHUNK
    lane = lax.broadcasted_iota(jnp.int32, (ln, LANES), 1)
    sub = lax.broadcasted_iota(jnp.int32, (LANES, SSD_STATE), 0)
    row = lax.broadcasted_iota(jnp.int32, (ln, ln), 0)
    col = lax.broadcasted_iota(jnp.int32, (ln, ln), 1)
    lo = lane < SSD_HEAD_DIM
    e1 = e0 + 1
    c0, c1 = a_cum[:, e0:e0 + 1], a_cum[:, e1:e1 + 1]
    r0, r1 = a_cum_t[e0:e0 + 1, :], a_cum_t[e1:e1 + 1, :]
    l0, l1 = a_last[e0:e0 + 1, :], a_last[e1:e1 + 1, :]
    xd = x_pair * jnp.where(lo, dt[:, e0:e0 + 1], dt[:, e1:e1 + 1])
    causal = row >= col
    cb = _bdot(cg, bg, NT)
    m0 = cb * jnp.exp(jnp.where(causal, c0 - r0, -jnp.inf))
    m1 = cb * jnp.exp(jnp.where(causal, c1 - r1, -jnp.inf))
    y = _bdot(m0, jnp.where(lo, xd, 0.0)) + _bdot(m1, jnp.where(lo, 0.0, xd))
    acum_pair = jnp.where(lo, c0, c1)
    y = y + _bdot(cg, hp, NT) * jnp.exp(acum_pair)
    last_pair = jnp.where(lo, l0, l1)
    st = _bdot(xd * jnp.exp(last_pair - acum_pair), bg, TN)
    h_out = hp * jnp.exp(jnp.where(sub < SSD_HEAD_DIM, l0, l1)) + st
    return y, h_out


def _ssd_in_specs(chunk_of):
    return [
        pl.BlockSpec((SSD_CHUNK, SSD_CONV_CH), lambda i: (chunk_of(i), 0)),
        pl.BlockSpec((SSD_CHUNK, HPAD), lambda i: (chunk_of(i), 0)),
        pl.BlockSpec((HPAD, SSD_CHUNK), lambda i: (0, chunk_of(i))),
        pl.BlockSpec((1, HPAD), lambda i: (0, 0)), pl.BlockSpec((HPAD, 1), lambda i: (0, 0)),
        pl.BlockSpec((1, HPAD), lambda i: (0, 0)), pl.BlockSpec((HPAD, 1), lambda i: (0, 0)),
    ]


def ssd_fwd(xc, dtr, dtr_t, dtb, dtb_t, alog, alog_t, name):
    def body(xc_ref, dtr_ref, dtrt_ref, dtb_ref, dtbt_ref, al_ref, alt_ref, y_ref, hs_ref, h_scr):
        @pl.when(pl.program_id(0) == 0)
        def _():
            h_scr[...] = jnp.zeros_like(h_scr)

        hs_ref[0] = h_scr[...]
        dt, a_cum, a_cum_t, a_last = _ssd_prefix(dtr_ref[...], dtrt_ref[...], dtb_ref[...], dtbt_ref[...],
                                                  al_ref[...], alt_ref[...])
        for pr in range(N_PAIRS):
            g = pr // PAIRS_PER_GROUP
            sl = slice(pr * LANES, (pr + 1) * LANES)
            bg = xc_ref[:, B_OFF + g * SSD_STATE:B_OFF + (g + 1) * SSD_STATE]
            cg = xc_ref[:, C_OFF + g * SSD_STATE:C_OFF + (g + 1) * SSD_STATE]
            y, h_out = _ssd_pair(xc_ref[:, sl], bg, cg, h_scr[sl, :], dt, a_cum, a_cum_t, a_last, e0=2 * pr)
            y_ref[:, sl] = y
            h_scr[sl, :] = h_out

    return pl.pallas_call(
        body, name=name, grid=(N_CHUNKS,), in_specs=_ssd_in_specs(lambda i: i),
        out_specs=[pl.BlockSpec((SSD_CHUNK, SSD_INNER), lambda i: (i, 0)),
                   pl.BlockSpec((1, SSD_INNER, SSD_STATE), lambda i: (i, 0, 0))],
        out_shape=[SDS((SEQ, SSD_INNER), F32), SDS((N_CHUNKS, SSD_INNER, SSD_STATE), F32)],
        scratch_shapes=[pltpu.VMEM((SSD_INNER, SSD_STATE), F32)],
        compiler_params=_cparams(("arbitrary",)),
    )(xc, dtr, dtr_t, dtb, dtb_t, alog, alog_t)


def ssd_bwd(xc, dtr, dtr_t, dtb, dtb_t, alog, alog_t, hs, dy, dxs_extra, name):
    rev = lambda i: N_CHUNKS - 1 - i

    def body(xc_ref, dtr_ref, dtrt_ref, dtb_ref, dtbt_ref, al_ref, alt_ref, hs_ref, dy_ref, dxe_ref,
             dxc_ref, ddtr_ref, ddtrt_ref, ddtb_ref, ddtbt_ref, dal_ref, dalt_ref, dh_scr):
        @pl.when(pl.program_id(0) == 0)
        def _():
            dh_scr[...] = jnp.zeros_like(dh_scr)
            for r in (ddtb_ref, ddtbt_ref, dal_ref, dalt_ref):
                r[...] = jnp.zeros_like(r)

        prefix_in = (dtr_ref[...], dtrt_ref[...], dtb_ref[...], dtbt_ref[...], al_ref[...], alt_ref[...])
        (dt, a_cum, a_cum_t, a_last), prefix_vjp = jax.vjp(_ssd_prefix, *prefix_in)
        d_dt = jnp.zeros_like(dt)
        d_acum = jnp.zeros_like(a_cum)
        d_acum_t = jnp.zeros_like(a_cum_t)
        d_alast = jnp.zeros_like(a_last)
        for g in range(SSD_GROUPS):
            bg = xc_ref[:, B_OFF + g * SSD_STATE:B_OFF + (g + 1) * SSD_STATE]
            cg = xc_ref[:, C_OFF + g * SSD_STATE:C_OFF + (g + 1) * SSD_STATE]
            d_bg = jnp.zeros_like(bg)
            d_cg = jnp.zeros_like(cg)
            for j in range(PAIRS_PER_GROUP):
                pr = g * PAIRS_PER_GROUP + j
                sl = slice(pr * LANES, (pr + 1) * LANES)
                _, vjp = jax.vjp(functools.partial(_ssd_pair, e0=2 * pr),
                                 xc_ref[:, sl], bg, cg, hs_ref[0, sl, :], dt, a_cum, a_cum_t, a_last)
                dx, dbg, dcg, dhp, ddt, dac, dact, dal = vjp((dy_ref[:, sl], dh_scr[sl, :]))
                dxc_ref[:, sl] = dx + dxe_ref[:, sl]
                dh_scr[sl, :] = dhp
                d_bg, d_cg = d_bg + dbg, d_cg + dcg
                d_dt, d_acum, d_acum_t, d_alast = d_dt + ddt, d_acum + dac, d_acum_t + dact, d_alast + dal
            dxc_ref[:, B_OFF + g * SSD_STATE:B_OFF + (g + 1) * SSD_STATE] = d_bg
            dxc_ref[:, C_OFF + g * SSD_STATE:C_OFF + (g + 1) * SSD_STATE] = d_cg
        g_dtr, g_dtrt, g_dtb, g_dtbt, g_al, g_alt = prefix_vjp((d_dt, d_acum, d_acum_t, d_alast))
        ddtr_ref[...] = g_dtr
        ddtrt_ref[...] = g_dtrt
        ddtb_ref[...] += g_dtb
        ddtbt_ref[...] += g_dtbt
        dal_ref[...] += g_al
        dalt_ref[...] += g_alt

    in_specs = _ssd_in_specs(rev) + [
        pl.BlockSpec((1, SSD_INNER, SSD_STATE), lambda i: (rev(i), 0, 0)),
        pl.BlockSpec((SSD_CHUNK, SSD_INNER), lambda i: (rev(i), 0)),
        pl.BlockSpec((SSD_CHUNK, SSD_INNER), lambda i: (rev(i), 0)),
    ]
    out_specs = [
        pl.BlockSpec((SSD_CHUNK, SSD_CONV_CH), lambda i: (rev(i), 0)),
        pl.BlockSpec((SSD_CHUNK, HPAD), lambda i: (rev(i), 0)),
        pl.BlockSpec((HPAD, SSD_CHUNK), lambda i: (0, rev(i))),
        pl.BlockSpec((1, HPAD), lambda i: (0, 0)), pl.BlockSpec((HPAD, 1), lambda i: (0, 0)),
        pl.BlockSpec((1, HPAD), lambda i: (0, 0)), pl.BlockSpec((HPAD, 1), lambda i: (0, 0)),
    ]
    out_shape = [SDS((SEQ, SSD_CONV_CH), F32), SDS((SEQ, HPAD), F32), SDS((HPAD, SEQ), F32),
                 SDS((1, HPAD), F32), SDS((HPAD, 1), F32), SDS((1, HPAD), F32), SDS((HPAD, 1), F32)]
    return pl.pallas_call(
        body, name=name, grid=(N_CHUNKS,), in_specs=in_specs, out_specs=out_specs, out_shape=out_shape,
        scratch_shapes=[pltpu.VMEM((SSD_INNER, SSD_STATE), F32)],
        compiler_params=_cparams(("arbitrary",)),
    )(xc, dtr, dtr_t, dtb, dtb_t, alog, alog_t, hs, dy, dxs_extra)


ATTN_SCALE = ATTN_HEAD_DIM ** -0.5


def _attn_scores(q, kp, kc, has_prev):
    qi = lax.broadcasted_iota(jnp.int32, (ATTN_BLOCK, ATTN_BLOCK), 0)
    kj = lax.broadcasted_iota(jnp.int32, (ATTN_BLOCK, ATTN_BLOCK), 1)
    s_c = jnp.where(qi >= kj, _bdot(q, kc, NT) * ATTN_SCALE, -jnp.inf)
    s_p = jnp.where((kj >= qi) & has_prev, _bdot(q, kp, NT) * ATTN_SCALE, -jnp.inf)
    return s_p, s_c


UNITS_PER_PATTERN = SEQ // ATTN_BLOCK
ATTN_UNROLL = 2


def _for_units(unit):
    for g, d in enumerate(ATTN_DILATIONS):
        nb = UNITS_PER_PATTERN // d
        span = d * ATTN_BLOCK

        def one(i, carry, g=g, d=d, nb=nb, span=span):
            r = i >> (nb.bit_length() - 1)
            n = i & (nb - 1)
            start = r + n * span
            prev = jnp.where(n > 0, start - span, start)
            unit(g, pl.ds(start, ATTN_BLOCK, stride=d), pl.ds(prev, ATTN_BLOCK, stride=d), n > 0)
            return carry
        lax.fori_loop(0, UNITS_PER_PATTERN, one, 0, unroll=ATTN_UNROLL)


def _head_specs(n_q_groups):
    blk = (SEQ, ATTN_HEAD_DIM)
    q_specs = [pl.BlockSpec(blk, functools.partial(lambda h, g: (0, g * ATTN_KV_HEADS + h), g=g)) for g in range(n_q_groups)]
    head = pl.BlockSpec(blk, lambda h: (0, h))
    table = pl.BlockSpec(blk, lambda h: (0, 0))
    return q_specs, head, table


def attn_fwd(q, k, v, tabs, name):
    q_specs, head, table = _head_specs(ATTN_N_PAT)

    def body(q0_ref, q1_ref, q2_ref, k_ref, v_ref, c_ref, sa_ref, sb_ref, y_ref, lse_ref, *scr):
        qs, og, ls, ks, vs = scr[0:3], scr[3:6], scr[6:9], scr[9], scr[10]
        c, sa, sb = c_ref[...], sa_ref[...], sb_ref[...]
        for g, q_ref in enumerate((q0_ref, q1_ref, q2_ref)):
            qs[g][...] = _rope(q_ref[...].astype(F32), c, sa, sb)
        ks[...] = _rope(k_ref[...].astype(F32), c, sa, sb)
        vs[...] = v_ref[...].astype(F32)

        def unit(g, rows, prows, has_prev):
            s_p, s_c = _attn_scores(qs[g][rows, :], ks[prows, :], ks[rows, :], has_prev)
            m = jnp.maximum(jnp.max(s_c, axis=1, keepdims=True), jnp.max(s_p, axis=1, keepdims=True))
            p_c, p_p = jnp.exp(s_c - m), jnp.exp(s_p - m)
            l = jnp.sum(p_c, axis=1, keepdims=True) + jnp.sum(p_p, axis=1, keepdims=True)
            o = _bdot(p_c, vs[rows, :]) + _bdot(p_p, vs[prows, :])
            og[g][rows, :] = o / l
            ls[g][rows, :] = jnp.broadcast_to(m + jnp.log(l), (ATTN_BLOCK, LANES))

        _for_units(unit)
        l0, l1, l2 = ls[0][...], ls[1][...], ls[2][...]
        m = jnp.maximum(jnp.maximum(l0, l1), l2)
        e0, e1, e2 = jnp.exp(l0 - m), jnp.exp(l1 - m), jnp.exp(l2 - m)
        den = e0 + e1 + e2
        y_ref[...] = ((e0 * og[0][...] + e1 * og[1][...] + e2 * og[2][...]) / den).astype(y_ref.dtype)
        lse_ref[...] = m + jnp.log(den)

    blk = (SEQ, ATTN_HEAD_DIM)
    return pl.pallas_call(
        body, name=name, grid=(ATTN_KV_HEADS,), in_specs=[*q_specs, head, head, table, table, table],
        out_specs=[head, head], out_shape=[SDS((SEQ, ATTN_OUT), BF16), SDS((SEQ, ATTN_OUT), F32)],
        scratch_shapes=[pltpu.VMEM(blk, F32)] * (3 * ATTN_N_PAT + 2),
        compiler_params=_cparams(("parallel",)),
    )(q, q, q, k, v, *tabs)


def attn_bwd(q, k, v, tabs, y, lse, dy, name):
    q_specs, head, table = _head_specs(ATTN_N_PAT)

    def body(q0_ref, q1_ref, q2_ref, k_ref, v_ref, c_ref, sa_ref, sb_ref, y_ref, lse_ref, dy_ref,
             dq0_ref, dq1_ref, dq2_ref, dk_ref, dv_ref, *scr):
        qs, dqs, ks, dks, dd, dvs, vs = scr[0:3], scr[3:6], scr[6], scr[7], scr[8], scr[9], scr[10]
        c, sa, sb = c_ref[...], sa_ref[...], sb_ref[...]
        for g, q_ref in enumerate((q0_ref, q1_ref, q2_ref)):
            qs[g][...] = _rope(q_ref[...].astype(F32), c, sa, sb)
        ks[...] = _rope(k_ref[...].astype(F32), c, sa, sb)
        vs[...] = v_ref[...].astype(F32)
        dks[...] = jnp.zeros_like(dks)
        dvs[...] = jnp.zeros_like(dvs)
        dyv = dy_ref[...]
        dd[...] = jnp.broadcast_to(jnp.sum(dyv * y_ref[...].astype(F32), axis=1, keepdims=True), dd.shape)

        def unit(g, rows, prows, has_prev):
            qv = qs[g][rows, :].astype(BF16)
            kc, kp = ks[rows, :].astype(BF16), ks[prows, :].astype(BF16)
            vc, vp = vs[rows, :].astype(BF16), vs[prows, :].astype(BF16)
            do = dy_ref[rows, :].astype(BF16)
            s_p, s_c = _attn_scores(qv, kp, kc, has_prev)
            lse_u = lse_ref[rows, :][:, 0:1]
            dsum = dd[rows, :][:, 0:1]
            p_c, p_p = jnp.exp(s_c - lse_u), jnp.exp(s_p - lse_u)
            ds_c = (p_c * (_dot(do, vc, NT) - dsum) * ATTN_SCALE).astype(BF16)
            ds_p = (p_p * (_dot(do, vp, NT) - dsum) * ATTN_SCALE).astype(BF16)
            dqs[g][rows, :] = _dot(ds_c, kc) + _dot(ds_p, kp)
            dks[rows, :] += _dot(ds_c, qv, TN)
            dks[prows, :] += _dot(ds_p, qv, TN)
            dvs[rows, :] += _bdot(p_c, do, TN)
            dvs[prows, :] += _bdot(p_p, do, TN)

        _for_units(unit)
        for g, dq_ref in enumerate((dq0_ref, dq1_ref, dq2_ref)):
            dq_ref[...] = _rope(dqs[g][...], c, -sa, -sb).astype(dq_ref.dtype)
        dk_ref[...] = _rope(dks[...], c, -sa, -sb).astype(dk_ref.dtype)
        dv_ref[...] = dvs[...].astype(dv_ref.dtype)

    blk = (SEQ, ATTN_HEAD_DIM)
    out = SDS((SEQ, ATTN_OUT), BF16)
    return pl.pallas_call(
        body, name=name, grid=(ATTN_KV_HEADS,), in_specs=[*q_specs, head, head, table, table, table, head, head, head],
        out_specs=[head] * 5, out_shape=[out] * 5,
        scratch_shapes=[pltpu.VMEM(blk, F32)] * (2 * ATTN_N_PAT + 5),
        compiler_params=_cparams(("parallel",)),
    )(q, q, q, k, v, *tabs, y, lse, dy)


def layer_fwd(h, getw, small, tabs, li):
    n = f"l{li}_"
    sv = {}
    w = dict(getw(0, h))
    u = rms_fwd(h, small["norm_mix"], n + "rms_mix")
    z = matmul(u, w["w_z"], name=n + "mm_z", tb=True, out_dtype=BF16)
    xbc = matmul(u, w["w_xbc"], name=n + "mm_xbc", tb=True, out_dtype=BF16)
    dtr = matmul(u, w["w_dt"], name=n + "mm_dt", tb=True)
    q = matmul(u, w["w_q"], name=n + "mm_q", tb=True, out_dtype=BF16)
    k = matmul(u, w["w_k"], name=n + "mm_k", tb=True, out_dtype=BF16)
    v = matmul(u, w["w_v"], name=n + "mm_v", tb=True, out_dtype=BF16)
    gs = matmul(u, w["w_gs"], name=n + "mm_gs", tb=True, out_dtype=BF16)
    ga = matmul(u, w["w_ga"], name=n + "mm_ga", tb=True, out_dtype=BF16)
    xc = conv_fwd(xbc, w["conv_w"], small["conv_b"], n + "conv")
    dtr_t = dtr.T
    y_ssd, hs = ssd_fwd(xc, dtr, dtr_t, small["dt_bias"], small["dt_bias"].T, small["a_log"], small["a_log"].T, n + "ssd")
    yn = ssd_post_fwd(y_ssd, xc, z, small["d_skip_x"], small["ssd_norm"], n + "ssd_post")
    y_attn, lse = attn_fwd(q, k, v, tabs, n + "attn")
    w.update(getw(1, y_ssd))
    a = matmul(yn, w["w_ssd_branch"], name=n + "mm_a", out_dtype=BF16)
    b = matmul(y_attn, w["w_attn_branch"], name=n + "mm_b", out_dtype=BF16)
    merged = gate_fwd(a, b, gs, ga, n + "gate")
    h1 = matmul(merged, w["w_out"], name=n + "mm_o", add=h)
    w.update(getw(2, h1))
    u2 = rms_fwd(h1, small["norm_ffn"], n + "rms_ffn")
    gu = matmul(u2, w["w_gate_up"], name=n + "mm_gu", tb=True, out_dtype=BF16)
    act = swiglu_fwd(gu, n + "swiglu")
    h2 = matmul(act, w["w_down"], name=n + "mm_down", add=h1)
    sv.update(h=h, u=u, z=z, xbc=xbc, dtr=dtr, dtr_t=dtr_t, gs=gs, ga=ga, xc=xc, y_ssd=y_ssd, hs=hs, yn=yn,
              q=q, k=k, v=v, y_attn=y_attn, lse=lse, a=a, b=b, merged=merged, h1=h1, u2=u2, gu=gu, act=act, w=w)
    return h2, sv


def layer_bwd(dh, sv, small, tabs, li, emit):
    n = f"l{li}_b_"
    w = sv["w"]
    gw, gsm = {}, {}
    dact = matmul(dh, w["w_down"], name=n + "mm_dact", tb=True, out_dtype=BF16)
    gw["w_down"] = matmul(sv["act"], dh, name=n + "mm_dwdown", ta=True, out_dtype=BF16)
    dgu = swiglu_bwd(sv["gu"], dact, n + "swiglu")
    gw["w_gate_up"] = matmul(dgu, sv["u2"], name=n + "mm_dwgu", ta=True, out_dtype=BF16)
    tok = emit(2, gw)
    du2 = matmul(dgu, w["w_gate_up"], name=n + "mm_du2")
    dh1, gsm["norm_ffn"] = rms_bwd(sv["h1"], du2, dh, small["norm_ffn"] + tok, n + "rms_ffn")
    dmerged = matmul(dh1, w["w_out"], name=n + "mm_dmerged", tb=True)
    gw["w_out"] = matmul(sv["merged"], dh1, name=n + "mm_dwo", ta=True, out_dtype=BF16)
    da, db, dgs, dga = gate_bwd(sv["a"], sv["b"], sv["gs"], sv["ga"], dmerged, n + "gate")
    gw["w_ssd_branch"] = matmul(sv["yn"], da, name=n + "mm_dwa", ta=True, out_dtype=BF16)
    gw["w_attn_branch"] = matmul(sv["y_attn"], db, name=n + "mm_dwb", ta=True, out_dtype=BF16)
    tok = emit(1, gw)
    dyn = matmul(da, w["w_ssd_branch"], name=n + "mm_dyn", tb=True)
    dyattn = matmul(db, w["w_attn_branch"], name=n + "mm_dyattn", tb=True)
    dy_ssd, dxs_extra, dz, gsm["d_skip_x"], gsm["ssd_norm"] = ssd_post_bwd(
        sv["y_ssd"], sv["xc"], sv["z"], small["d_skip_x"] + tok, small["ssd_norm"], dyn, n + "ssd_post")
    dxc, ddtr, ddtr_t, ddtb, ddtb_t, dal, dal_t = ssd_bwd(
        sv["xc"], sv["dtr"], sv["dtr_t"], small["dt_bias"], small["dt_bias"].T, small["a_log"], small["a_log"].T,
        sv["hs"], dy_ssd, dxs_extra, n + "ssd")
    ddtr = (ddtr + ddtr_t.T).astype(BF16)
    gsm["dt_bias"] = ddtb + ddtb_t.T
    gsm["a_log"] = dal + dal_t.T
    dxbc, gw["conv_w"], gsm["conv_b"] = conv_bwd(sv["xbc"], w["conv_w"], small["conv_b"], dxc, n + "conv")
    dq0, dq1, dq2, dk, dv = attn_bwd(sv["q"], sv["k"], sv["v"], tabs, sv["y_attn"], sv["lse"], dyattn, n + "attn")
    u = sv["u"]
    segs = [("w_z", dz), ("w_xbc", dxbc), ("w_dt", ddtr), ("w_q0", dq0), ("w_q1", dq1), ("w_q2", dq2),
            ("w_k", dk), ("w_v", dv), ("w_gs", dgs), ("w_ga", dga)]
    gin = [matmul(dseg, u, name=n + "mm_d" + key, ta=True, out_dtype=BF16) for key, dseg in segs]
    gin[2] = gin[2][:SSD_HEADS]
    gw["w_in"] = jnp.concatenate(gin, axis=0)
    tok = emit(0, gw)
    du = jnp.zeros((SEQ, D_MODEL), F32) + tok
    for key, dseg in segs:
        du = matmul(dseg, w[key], name=n + "mm_du_" + key, add=du)
    dh0, gsm["norm_mix"] = rms_bwd(sv["h"], du, dh1, small["norm_mix"] + tok, n + "rms_mix")
    return dh0, gsm


def _my_place():
    return lax.axis_index("x"), lax.axis_index("y"), lax.axis_index("c")


def _flip(place, k):
    x, y, c = place
    return (1 - x if k & 4 else x, 1 - y if k & 2 else y, 1 - c if k & 1 else c)


def _index(place):
    return 4 * place[0] + 2 * place[1] + place[2]


ANY = pl.BlockSpec(memory_space=pl.ANY)
CHIP_FLIPS = (4, 2, 6)


def all_gather(xs, name):
    na = len(xs)

    def body(*refs):
        x_refs, o_refs = refs[:na], refs[na:2 * na]
        send_sems, recv_sems, local_sems = refs[2 * na:]
        me = _my_place()
        sibling = _flip(me, 1)
        chips = [_flip(me, f) for f in CHIP_FLIPS]

        def copy(a, kk, block, to, src=None):
            dst = o_refs[a].at[_index(block)]
            return pltpu.make_async_remote_copy(
                src_ref=dst if src is None else src, dst_ref=dst, send_sem=send_sems.at[a, kk],
                recv_sem=recv_sems.at[a, kk], device_id=to, device_id_type=MESH)

        mine = [pltpu.make_async_copy(x_refs[a], o_refs[a].at[_index(me)], local_sems.at[a]) for a in range(na)]
        for cp in mine:
            cp.start()
        first = []
        for j, chip in enumerate(chips):
            first += [copy(a, 1 + j, me, chip, src=x_refs[a]) for a in range(na)]
        first += [copy(a, 0, me, sibling, src=x_refs[a]) for a in range(na)]
        for cp in first:
            cp.start()
        passed = []
        for j, chip in enumerate(chips):
            for a in range(na):
                copy(a, 1 + j, chip, me).wait_recv()
                cp = copy(a, 4 + j, chip, sibling)
                cp.start()
                passed.append(cp)
        for a in range(na):
            copy(a, 0, sibling, me).wait_recv()
        for j, chip in enumerate(chips):
            for a in range(na):
                copy(a, 4 + j, _flip(chip, 1), me).wait_recv()
        for cp in first + passed:
            cp.wait_send()
        for cp in mine:
            cp.wait()

    return pl.pallas_call(
        body, name=name, in_specs=[ANY] * na, out_specs=[ANY] * na,
        out_shape=[SDS((N_DEV,) + t.shape, t.dtype) for t in xs],
        scratch_shapes=[pltpu.SemaphoreType.DMA((na, N_DEV - 1)), pltpu.SemaphoreType.DMA((na, N_DEV - 1)),
                        pltpu.SemaphoreType.DMA((na,))],
    )(*xs)


HBM = pl.BlockSpec(memory_space=pltpu.HBM)
SEM = pl.BlockSpec(memory_space=pltpu.SEMAPHORE)
EFFECT = pltpu.SideEffectType.DATAFLOW_SIDE_EFFECTING
N_PEERS = N_DEV - 1


def _split_copy(src_ref, land_ref, send_sem, recv_sem, me, kk, scatter, landed_from_peer):
    peer = _flip(me, kk)
    src = src_ref.at[_index(peer)] if scatter else src_ref
    dst = land_ref.at[_index(peer if landed_from_peer else me)]
    return pltpu.make_async_remote_copy(src_ref=src, dst_ref=dst, send_sem=send_sem, recv_sem=recv_sem,
                                        device_id=peer, device_id_type=MESH)


def exchange_start(srcs, lands, group_sizes, scatter, name):
    na, ng = len(srcs), len(group_sizes)

    def body(*refs):
        s_refs, l_refs = refs[:na], refs[na:2 * na]
        sems = refs[2 * na:2 * na + 2 * ng]
        token = refs[-1]
        me = _my_place()
        a = 0
        for gi, gsz in enumerate(group_sizes):
            for j in range(gsz):
                for kk in range(1, N_DEV):
                    slot = j * N_PEERS + kk - 1
                    _split_copy(s_refs[a], l_refs[a], sems[2 * gi].at[slot], sems[2 * gi + 1].at[slot],
                                me, kk, scatter, False).start()
                a += 1
        token[...] = jnp.zeros_like(token)

    sem_shapes = []
    for gsz in group_sizes:
        sem_shapes += [pltpu.SemaphoreType.DMA((gsz * N_PEERS,))] * 2
    ins = [pltpu.with_memory_space_constraint(t, pltpu.HBM) for t in (*srcs, *lands)]
    res = pl.pallas_call(
        body, name=name, in_specs=[HBM] * (2 * na),
        out_specs=[SEM] * (2 * ng) + [HBM] * (2 * na) + [pl.BlockSpec(memory_space=pltpu.VMEM)],
        out_shape=sem_shapes + [pltpu.HBM(t.shape, t.dtype) for t in ins] + [SDS((8, LANES), F32)],
        input_output_aliases={i: 2 * ng + i for i in range(2 * na)},
        compiler_params=pltpu.CompilerParams(has_side_effects=EFFECT),
    )(*ins)
    sems = [(res[2 * gi], res[2 * gi + 1]) for gi in range(ng)]
    thru = res[2 * ng:2 * ng + 2 * na]
    return sems, thru[:na], thru[na:], res[-1]


def exchange_wait(srcs, lands, sems, after, scatter, name):
    n = len(srcs)

    def body(*refs):
        s_refs, l_refs = refs[:n], refs[n:2 * n]
        send_sems, recv_sems = refs[2 * n], refs[2 * n + 1]
        me = _my_place()
        for j in range(n):
            for kk in range(1, N_DEV):
                slot = j * N_PEERS + kk - 1
                cp = _split_copy(s_refs[j], l_refs[j], send_sems.at[slot], recv_sems.at[slot], me, kk, scatter, True)
                cp.wait_send()
                cp.wait_recv()

    res = pl.pallas_call(
        body, name=name, in_specs=[HBM] * (2 * n) + [SEM, SEM, ANY], out_specs=[HBM] * (2 * n),
        out_shape=[pltpu.HBM(t.shape, t.dtype) for t in (*srcs, *lands)],
        input_output_aliases={i: i for i in range(2 * n)},
        compiler_params=pltpu.CompilerParams(has_side_effects=EFFECT),
    )(*srcs, *lands, sems[0], sems[1], after)
    return res[n:]


def landing_zone(block, me_index):
    land = lax.empty((N_DEV,) + block.shape, block.dtype)
    return lax.dynamic_update_slice(land, block[None], (me_index,) + (0,) * block.ndim)


def sum_parts(parts, name):
    _, r, c = parts.shape
    tc = _pick(c, (256, 128))

    def body(p_ref, o_ref):
        acc = p_ref[0].astype(F32)
        for i in range(1, N_DEV):
            acc = acc + p_ref[i].astype(F32)
        o_ref[...] = acc

    return pl.pallas_call(
        body, name=name, grid=(c // tc,), in_specs=[pl.BlockSpec((N_DEV, r, tc), lambda i: (0, 0, i))],
        out_specs=pl.BlockSpec((r, tc), lambda i: (0, i)), out_shape=SDS((r, c), F32),
        compiler_params=_cparams(("parallel",)),
    )(parts)


ADAMW_BLOCK_BYTES = 2 * 1024 * 1024


def adamw(w, g, m, v, name):
    shape = w.shape
    lay, rows, cols = ((1, 1) + shape)[-3:]
    tr = _pick(rows, (256, 128))
    tc = cols if tr * cols * 4 <= ADAMW_BLOCK_BYTES else _pick(cols, (256, 128))
    c1 = 1.0 / (1.0 - ADAM_B1 ** ADAM_STEP)
    c2 = 1.0 / (1.0 - ADAM_B2 ** ADAM_STEP)

    def body(w_ref, g_ref, m_ref, v_ref, d_ref, nm_ref, nv_ref):
        gg = g_ref[...]
        nm = ADAM_B1 * m_ref[...] + (1.0 - ADAM_B1) * gg
        nv = ADAM_B2 * v_ref[...] + (1.0 - ADAM_B2) * (gg * gg)
        d_ref[...] = -ADAM_LR * ((nm * c1) / (jnp.sqrt(nv * c2) + ADAM_EPS) + ADAM_WD * w_ref[...])
        nm_ref[...] = nm
        nv_ref[...] = nv

    spec = pl.BlockSpec((1, tr, tc), lambda l, i, j: (l, i, j))
    outs = pl.pallas_call(
        body, name=name, grid=(lay, rows // tr, cols // tc), in_specs=[spec] * 4, out_specs=[spec] * 3,
        out_shape=[SDS((lay, rows, cols), F32)] * 3, compiler_params=_cparams(("parallel",) * 3),
    )(*[t.reshape(lay, rows, cols) for t in (w, g, m, v)])
    return [o.reshape(shape) for o in outs]


def adamw_layer_inner(w, gs, m, v, name):
    rows, lay, cols = w.shape
    tc = LANES
    c1 = 1.0 / (1.0 - ADAM_B1 ** ADAM_STEP)
    c2 = 1.0 / (1.0 - ADAM_B2 ** ADAM_STEP)

    def body(*refs):
        w_ref, m_ref, v_ref = refs[:3]
        g_refs = refs[3:3 + lay]
        go_ref, d_ref, nm_ref, nv_ref = refs[3 + lay:]
        for l, g_ref in enumerate(g_refs):
            gg = g_ref[...]
            nm = ADAM_B1 * m_ref[:, l, :] + (1.0 - ADAM_B1) * gg
            nv = ADAM_B2 * v_ref[:, l, :] + (1.0 - ADAM_B2) * (gg * gg)
            d_ref[:, l, :] = -ADAM_LR * ((nm * c1) / (jnp.sqrt(nv * c2) + ADAM_EPS) + ADAM_WD * w_ref[:, l, :])
            go_ref[:, l, :] = gg
            nm_ref[:, l, :] = nm
            nv_ref[:, l, :] = nv

    inner = pl.BlockSpec((rows, lay, tc), lambda j: (0, 0, j))
    plain = pl.BlockSpec((rows, tc), lambda j: (0, j))
    return pl.pallas_call(
        body, name=name, grid=(cols // tc,), in_specs=[inner] * 3 + [plain] * lay, out_specs=[inner] * 4,
        out_shape=[SDS((rows, lay, cols), F32)] * 4, compiler_params=_cparams(("parallel",)),
    )(w, m, v, *gs)


BIG = ("w_in", "conv_w", "w_ssd_branch", "w_attn_branch", "w_out", "w_gate_up", "w_down")
TRANSPOSED = ("w_in", "w_gate_up")
SMALL = ("norm_mix", "conv_b", "dt_bias", "a_log", "d_skip", "ssd_norm", "norm_ffn")
SMALL_SIZE = {"norm_mix": 1024, "conv_b": 3072, "dt_bias": 32, "a_log": 32, "d_skip": 32, "ssd_norm": 2048, "norm_ffn": 1024}
FLAT_W = 512
SMALL_TOTAL = DEPTH * sum(SMALL_SIZE.values()) + D_MODEL + LANES
SMALL_ROWS = 32
assert SMALL_ROWS * FLAT_W >= SMALL_TOTAL


GROUPS = (("w_in", "conv_w"), ("w_ssd_branch", "w_attn_branch", "w_out"), ("w_gate_up", "w_down"))


def to_wire(k, shard):
    if k in TRANSPOSED:
        return shard.T.astype(BF16)
    return shard if k == "conv_w" else shard.astype(BF16)


def full_weights(k, g):
    if k == "conv_w":
        return {k: g.transpose(1, 0, 2).reshape(SSD_CONV, SSD_CONV_CH)}
    full = g.reshape(-1, g.shape[-1])
    if k != "w_in":
        return {k: full}
    w, off = {}, 0
    for nm, r in IN_ROWS:
        w[nm] = full[off:off + r]
        off += r
    w["w_q"] = full[sum(r for _, r in IN_ROWS[:3]):sum(r for _, r in IN_ROWS[:6])]
    w["w_dt"] = jnp.pad(w["w_dt"], ((0, HPAD - SSD_HEADS), (0, 0)))
    return w


def grads_to_wire(k, g):
    if k == "conv_w":
        return g.reshape(SSD_CONV, N_DEV, SSD_CONV_CH // N_DEV).transpose(1, 0, 2)
    return g.reshape(N_DEV, g.shape[0] // N_DEV, g.shape[1])


def _pad_heads(t):
    return jnp.pad(t.reshape(1, SSD_HEADS), ((0, 0), (0, HPAD - SSD_HEADS)))


def local_step(x, target, getw, emit, smalls, norm_final):
    tabs = rope_tables()
    sms = []
    for li in range(DEPTH):
        s = smalls[li]
        sms.append({
            "norm_mix": s["norm_mix"].reshape(1, -1), "conv_b": s["conv_b"].reshape(1, -1),
            "dt_bias": _pad_heads(s["dt_bias"]), "a_log": _pad_heads(s["a_log"]),
            "d_skip_x": jnp.repeat(s["d_skip"], SSD_HEAD_DIM).reshape(1, -1),
            "ssd_norm": s["ssd_norm"].reshape(1, -1), "norm_ffn": s["norm_ffn"].reshape(1, -1)})
    h = x
    saved = []
    for li in range(DEPTH):
        h, sv = layer_fwd(h, functools.partial(getw, li), sms[li], tabs, li)
        saved.append(sv)
    dh, g_final, loss = loss_head(h, target, norm_final.reshape(1, -1), "loss_head")
    gsms = [None] * DEPTH
    for li in reversed(range(DEPTH)):
        dh, gsm = layer_bwd(dh, saved[li], sms[li], tabs, li, functools.partial(emit, li))
        gsms[li] = {
            "norm_mix": gsm["norm_mix"].reshape(-1), "conv_b": gsm["conv_b"].reshape(-1),
            "dt_bias": gsm["dt_bias"][0, :SSD_HEADS], "a_log": gsm["a_log"][0, :SSD_HEADS],
            "d_skip": gsm["d_skip_x"].reshape(SSD_HEADS, SSD_HEAD_DIM).sum(axis=1),
            "ssd_norm": gsm["ssd_norm"].reshape(-1), "norm_ffn": gsm["norm_ffn"].reshape(-1)}
    return loss, dh, gsms, g_final.reshape(-1)


def kernel(x, norm_mix, w_in, conv_w, conv_b, dt_bias, a_log, d_skip, ssd_norm, w_ssd_branch, w_attn_branch, w_out, norm_ffn, w_gate_up, w_down, norm_final, loss_target, m_norm_mix, m_w_in, m_conv_w, m_conv_b, m_dt_bias, m_a_log, m_d_skip, m_ssd_norm, m_w_ssd_branch, m_w_attn_branch, m_w_out, m_norm_ffn, m_w_gate_up, m_w_down, m_norm_final, v_norm_mix, v_w_in, v_conv_w, v_conv_b, v_dt_bias, v_a_log, v_d_skip, v_ssd_norm, v_w_ssd_branch, v_w_attn_branch, v_w_out, v_norm_ffn, v_w_gate_up, v_w_down, v_norm_final):
    wv = dict(norm_mix=norm_mix, w_in=w_in, conv_w=conv_w, conv_b=conv_b, dt_bias=dt_bias, a_log=a_log, d_skip=d_skip,
              ssd_norm=ssd_norm, w_ssd_branch=w_ssd_branch, w_attn_branch=w_attn_branch, w_out=w_out, norm_ffn=norm_ffn,
              w_gate_up=w_gate_up, w_down=w_down, norm_final=norm_final)
    mv = dict(norm_mix=m_norm_mix, w_in=m_w_in, conv_w=m_conv_w, conv_b=m_conv_b, dt_bias=m_dt_bias, a_log=m_a_log,
              d_skip=m_d_skip, ssd_norm=m_ssd_norm, w_ssd_branch=m_w_ssd_branch, w_attn_branch=m_w_attn_branch,
              w_out=m_w_out, norm_ffn=m_norm_ffn, w_gate_up=m_w_gate_up, w_down=m_w_down, norm_final=m_norm_final)
    vv = dict(norm_mix=v_norm_mix, w_in=v_w_in, conv_w=v_conv_w, conv_b=v_conv_b, dt_bias=v_dt_bias, a_log=v_a_log,
              d_skip=v_d_skip, ssd_norm=v_ssd_norm, w_ssd_branch=v_w_ssd_branch, w_attn_branch=v_w_attn_branch,
              w_out=v_w_out, norm_ffn=v_norm_ffn, w_gate_up=v_w_gate_up, w_down=v_w_down, norm_final=v_norm_final)
    order = ("norm_mix", "w_in", "conv_w", "conv_b", "dt_bias", "a_log", "d_skip", "ssd_norm", "w_ssd_branch",
             "w_attn_branch", "w_out", "norm_ffn", "w_gate_up", "w_down", "norm_final")

    me_index = _index(_my_place())
    smalls = [{k: wv[k][li] for k in SMALL} for li in range(DEPTH)]
    n_groups = len(GROUPS)

    first_lands = all_gather([to_wire(k, wv[k][0]) for k in GROUPS[0]], "gather_first")
    later = [(li, gi) for li in range(DEPTH) for gi in range(n_groups)][1:]
    behind_first = first_lands[1][0, 0, 0] * 0.0
    srcs = [to_wire(k, wv[k][li] + behind_first if k == "conv_w" else wv[k][li]) for li, gi in later for k in GROUPS[gi]]
    sizes = [len(GROUPS[gi]) for _, gi in later]
    w_sems, w_srcs, w_lands, token = exchange_start(srcs, [landing_zone(s, me_index) for s in srcs], sizes, False, "gather_start")
    smalls[0]["norm_mix"] = smalls[0]["norm_mix"] + token[0, 0]

    def getw(li, gi, after):
        if (li, gi) == (0, 0):
            lands = first_lands
        else:
            slot = later.index((li, gi))
            sl = slice(sum(sizes[:slot]), sum(sizes[:slot + 1]))
            lands = exchange_wait(w_srcs[sl], w_lands[sl], w_sems[slot], after, False, f"gather_wait_{li}_{gi}")
        w = {}
        for k, land in zip(GROUPS[gi], lands):
            w.update(full_weights(k, land))
        return w

    pending = []

    def emit(li, gi, gw):
        parts = [grads_to_wire(k, gw[k]) for k in GROUPS[gi]]
        lands = [landing_zone(lax.dynamic_index_in_dim(p, me_index, 0, keepdims=False), me_index) for p in parts]
        sems, p_thru, l_thru, tok = exchange_start(parts, lands, [len(parts)], True, f"grads_start_{li}_{gi}")
        pending.append((li, gi, sems[0], p_thru, l_thru))
        return tok[0, 0]

    loss_p, dx, gsms, g_final = local_step(x[0], loss_target[0], getw, emit, smalls, norm_final)

    grads, deltas, new_m, new_v = {}, {}, {}, {}

    def update(k):
        if k == "w_in":
            inner = lambda t: t.transpose(2, 0, 1)
            outs = adamw_layer_inner(inner(wv[k]), shard_g[k], inner(mv[k]), inner(vv[k]), "adamw_" + k)
            grads[k], deltas[k], new_m[k], new_v[k] = (t.transpose(1, 2, 0) for t in outs)
            return outs[3]
        if k in BIG:
            grads[k] = jnp.stack([g.T if k in TRANSPOSED else g for g in shard_g[k]])
        deltas[k], new_m[k], new_v[k] = adamw(wv[k], grads[k], mv[k], vv[k], "adamw_" + k)
        return new_v[k]

    shard_g = {k: [None] * DEPTH for k in BIG}

    def collect(entry, after):
        li, gi, sems, p_thru, l_thru = entry
        recv = exchange_wait(p_thru, l_thru, sems, after, True, f"grads_wait_{li}_{gi}")
        for k, r in zip(GROUPS[gi], recv):
            if k == "conv_w":
                r = r.reshape(N_DEV, 1, -1)
            after = sum_parts(r, f"sum_{k}_{li}")
            shard_g[k][li] = after if k in TRANSPOSED else after.reshape(wv[k].shape[1:])
        return after

    after = dx
    for entry in pending[:-1]:
        after = collect(entry, after)
    done = [after[:1, :1].reshape(1)]
    for gi in (2, 1):
        for k in GROUPS[gi]:
            done.append(update(k).reshape(-1)[:1])

    flat = [gsms[li][k] for li in range(DEPTH) for k in SMALL] + [g_final, loss_p.reshape(-1)]
    flat.append(jnp.zeros((SMALL_ROWS * FLAT_W - SMALL_TOTAL,), F32))
    small_all = all_gather([jnp.concatenate(flat).reshape(SMALL_ROWS, FLAT_W)], "gather_small")[0]
    small_sum = sum_parts(small_all, "sum_small").reshape(-1)
    off = 0
    per_layer = {k: [] for k in SMALL}
    for li in range(DEPTH):
        for k in SMALL:
            per_layer[k].append(small_sum[off:off + SMALL_SIZE[k]])
            off += SMALL_SIZE[k]
    for k in SMALL:
        grads[k] = jnp.stack(per_layer[k])
    grads["norm_final"] = small_sum[off:off + D_MODEL]
    loss = small_sum[off + D_MODEL]
    for k in (*SMALL, "norm_final"):
        done.append(update(k).reshape(-1)[:1])

    collect(pending[-1], jnp.concatenate(done))
    for k in GROUPS[0]:
        update(k)

    return (loss, dx.reshape(x.shape), *[grads[k] for k in order], *[deltas[k] for k in order],
            *[new_m[k] for k in order], *[new_v[k] for k in order])
```

```python
import functools

import jax
import jax.numpy as jnp
from jax import lax
from jax.experimental import pallas as pl
from jax.experimental.pallas import tpu as pltpu

F32, BF16 = jnp.float32, jnp.bfloat16
SDS = jax.ShapeDtypeStruct
MESH = pl.DeviceIdType.MESH

D_MODEL = 1024
SEQ = 2048
DEPTH = 2
RMS_EPS = 1e-5
SSD_INNER = 2048
SSD_HEAD_DIM = 64
SSD_HEADS = 32
SSD_STATE = 128
SSD_GROUPS = 4
SSD_CONV = 4
SSD_CHUNK = 128
SSD_CONV_CH = 3072
ATTN_HEAD_DIM = 128
ATTN_KV_HEADS = 8
ATTN_DILATIONS = (1, 4, 16)
ATTN_N_PAT = 3
ATTN_BLOCK = 128
ATTN_OUT = 1024
ROPE_THETA = 500000.0
ROPE_DIM = 32
FFN_HIDDEN = 2816
ADAM_LR, ADAM_B1, ADAM_B2, ADAM_EPS, ADAM_WD, ADAM_STEP = 0.001, 0.9, 0.999, 1e-08, 0.01, 10

N_DEV = 8
LANES = 128
VMEM_LIMIT = 56 * 1024 * 1024
HPAD = 128
HIGHEST = lax.Precision.HIGHEST

IN_ROWS = (("w_z", 2048), ("w_xbc", 3072), ("w_dt", 32), ("w_q0", 1024), ("w_q1", 1024), ("w_q2", 1024),
           ("w_k", 1024), ("w_v", 1024), ("w_gs", 1024), ("w_ga", 1024))
N_IN = sum(r for _, r in IN_ROWS)


def _cparams(sem):
    return pltpu.CompilerParams(dimension_semantics=sem, vmem_limit_bytes=VMEM_LIMIT)


def _sigmoid(x):
    return 0.5 * jnp.tanh(0.5 * x) + 0.5


def _silu(x):
    return x * _sigmoid(x)


def _softplus(x):
    return jnp.maximum(x, 0.0) + jnp.log(1.0 + jnp.exp(-jnp.abs(x)))


def _dot(a, b, dims=(((1,), (0,)), ((), ())), precision=None):
    return lax.dot_general(a, b, dims, precision=precision, preferred_element_type=F32)


NT = (((1,), (1,)), ((), ()))
TN = (((0,), (0,)), ((), ()))


def _bdot(a, b, dims=(((1,), (0,)), ((), ()))):
    return _dot(a.astype(BF16), b.astype(BF16), dims)


def _pick(dim, cands):
    for c in cands:
        if dim % c == 0:
            return c
    return dim


def matmul(a, b, *, name, ta=False, tb=False, out_dtype=F32, add=None):
    m, k = (a.shape[1], a.shape[0]) if ta else a.shape
    n = b.shape[0] if tb else b.shape[1]
    tn = _pick(n, (1024, 1408, 512, 256, 128))
    tm = _pick(m, (512, 1408, 256, 128)) if tn == n else _pick(m, (1024, 1408, 512, 256, 128))
    tk = _pick(k, (1024, 1408, 512, 256, 128))
    nk = k // tk
    a_spec = pl.BlockSpec((tk, tm), lambda i, j, kk: (kk, i)) if ta else pl.BlockSpec((tm, tk), lambda i, j, kk: (i, kk))
    b_spec = pl.BlockSpec((tn, tk), lambda i, j, kk: (j, kk)) if tb else pl.BlockSpec((tk, tn), lambda i, j, kk: (kk, j))
    dims = (((0 if ta else 1,), (1 if tb else 0,)), ((), ()))
    has_add = add is not None

    def body(*refs):
        a_ref, b_ref = refs[:2]
        add_ref = refs[2] if has_add else None
        o_ref = refs[3] if has_add else refs[2]
        acc = refs[-1] if nk > 1 else None
        kk = pl.program_id(2)

        def product():
            return _dot(a_ref[...].astype(BF16), b_ref[...].astype(BF16), dims)

        def finish(r):
            if has_add:
                r = r + add_ref[...].astype(F32)
            o_ref[...] = r.astype(o_ref.dtype)

        if nk == 1:
            finish(product())
            return

        @pl.when(kk == 0)
        def _():
            acc[...] = product()

        @pl.when((kk > 0) & (kk < nk - 1))
        def _():
            acc[...] += product()

        @pl.when(kk == nk - 1)
        def _():
            finish(acc[...] + product())

    in_specs = [a_spec, b_spec]
    args = [a, b]
    if has_add:
        in_specs.append(pl.BlockSpec((tm, tn), lambda i, j, kk: (i, j)))
        args.append(add)
    return pl.pallas_call(
        body, name=name, grid=(m // tm, n // tn, nk),
        in_specs=in_specs, out_specs=pl.BlockSpec((tm, tn), lambda i, j, kk: (i, j)),
        out_shape=SDS((m, n), out_dtype), scratch_shapes=[pltpu.VMEM((tm, tn), F32)] if nk > 1 else [],
        compiler_params=_cparams(("parallel", "parallel", "arbitrary")),
    )(*args)


def rowcall(name, fn, rows, params, row_outs, red_outs=(), tr=256):
    s = rows[0].shape[0]
    n_in = len(rows) + len(params)
    n_row = len(row_outs)

    def body(*refs):
        outs = fn(*[r[...].astype(F32) for r in refs[:n_in]])
        if not isinstance(outs, (tuple, list)):
            outs = (outs,)
        orefs = refs[n_in:]
        for r, o in zip(orefs[:n_row], outs[:n_row]):
            r[...] = o.astype(r.dtype)
        if red_outs:
            @pl.when(pl.program_id(0) == 0)
            def _():
                for r in orefs[n_row:]:
                    r[...] = jnp.zeros_like(r)
            for r, o in zip(orefs[n_row:], outs[n_row:]):
                r[...] += o.astype(F32)

    widths = [a[1] if isinstance(a, tuple) else a.shape[1] for a in rows]
    rows = [a[0] if isinstance(a, tuple) else a for a in rows]
    in_specs = [pl.BlockSpec((tr, wd), lambda i: (i, 0)) for wd in widths]
    in_specs += [pl.BlockSpec(p.shape, lambda i: (0, 0)) for p in params]
    out_specs = [pl.BlockSpec((tr, c), lambda i: (i, 0)) for c, _ in row_outs]
    out_specs += [pl.BlockSpec(shp, lambda i: (0, 0)) for shp in red_outs]
    out_shape = [SDS((s, c), dt) for c, dt in row_outs] + [SDS(shp, F32) for shp in red_outs]
    res = pl.pallas_call(
        body, name=name, grid=(s // tr,), in_specs=in_specs, out_specs=out_specs, out_shape=out_shape,
        compiler_params=_cparams(("arbitrary",) if red_outs else ("parallel",)),
    )(*rows, *params)
    return res


def _rms(x, w):
    return x * lax.rsqrt(jnp.mean(x * x, axis=-1, keepdims=True) + RMS_EPS) * w


def rms_fwd(h, w, name):
    return rowcall(name, _rms, [h], [w], [(D_MODEL, BF16)])[0]


def rms_bwd(h, du, dres, w, name):
    def fn(hb, dub, dresb, wb):
        _, vjp = jax.vjp(_rms, hb, wb)
        dh, dw = vjp(dub)
        return dh + dresb, dw
    return rowcall(name, fn, [h, du, dres], [w], [(D_MODEL, F32)], [(1, D_MODEL)])


def loss_head(h, target, w, name):
    def fn(hb, tb, wb):
        def f(hh, ww):
            err = _rms(hh, ww) - tb
            return 0.5 * jnp.sum(jnp.mean(err * err, axis=-1, keepdims=True), axis=0, keepdims=True)
        val, vjp = jax.vjp(f, hb, wb)
        dh, dw = vjp(jnp.ones((1, 1), F32))
        return dh, dw, jnp.broadcast_to(val, (1, LANES))
    return rowcall(name, fn, [h, target], [w], [(D_MODEL, F32)], [(1, D_MODEL), (1, LANES)])


def _gate(a, b, gs, ga):
    return _sigmoid(gs) * a + _sigmoid(ga) * b


def gate_fwd(a, b, gs, ga, name):
    return rowcall(name, _gate, [a, b, gs, ga], [], [(D_MODEL, BF16)])[0]


def gate_bwd(a, b, gs, ga, dm, name):
    def fn(ab, bb, gsb, gab, dmb):
        _, vjp = jax.vjp(_gate, ab, bb, gsb, gab)
        return vjp(dmb)
    return rowcall(name, fn, [a, b, gs, ga, dm], [], [(D_MODEL, BF16)] * 4)


def _swiglu(gu):
    return _silu(gu[:, :FFN_HIDDEN]) * gu[:, FFN_HIDDEN:]


def swiglu_fwd(gu, name):
    return rowcall(name, _swiglu, [gu], [], [(FFN_HIDDEN, BF16)])[0]


def swiglu_bwd(gu, dact, name):
    def fn(gub, db):
        _, vjp = jax.vjp(_swiglu, gub)
        return vjp(db.astype(F32))[0]
    return rowcall(name, fn, [gu, dact], [], [(2 * FFN_HIDDEN, BF16)])[0]


def _ssd_post(y, xs, z, dskip, normw):
    y = (y + dskip * xs) * _silu(z)
    gw = SSD_INNER // SSD_GROUPS
    parts = []
    for g in range(SSD_GROUPS):
        yg = y[:, g * gw:(g + 1) * gw]
        parts.append(yg * lax.rsqrt(jnp.mean(yg * yg, axis=-1, keepdims=True) + RMS_EPS))
    return jnp.concatenate(parts, axis=-1) * normw


def ssd_post_fwd(y, xc, z, dskip, normw, name):
    return rowcall(name, _ssd_post, [y, (xc, SSD_INNER), z], [dskip, normw], [(SSD_INNER, BF16)])[0]


def ssd_post_bwd(y, xc, z, dskip, normw, dyn, name):
    def fn(yb, xsb, zb, dynb, db, nb):
        _, vjp = jax.vjp(_ssd_post, yb, xsb, zb, db, nb)
        return vjp(dynb)
    return rowcall(name, fn, [y, (xc, SSD_INNER), z, dyn], [dskip, normw],
                   [(SSD_INNER, F32), (SSD_INNER, F32), (SSD_INNER, BF16)], [(1, SSD_INNER), (1, SSD_INNER)])


def _rope(t, cosf, sina, sinb):
    return t * cosf + pltpu.roll(t, LANES - ROPE_DIM // 2, 1) * sina + pltpu.roll(t, ROPE_DIM // 2, 1) * sinb


def rope_tables():
    half = ROPE_DIM // 2
    inv = ROPE_THETA ** (-jnp.arange(0, ROPE_DIM, 2, dtype=F32) / ROPE_DIM)
    ang = jnp.arange(SEQ, dtype=F32)[:, None] * inv[None, :]
    cos, sin = jnp.cos(ang), jnp.sin(ang)
    zeros = jnp.zeros((SEQ, LANES - ROPE_DIM), F32)
    z16 = jnp.zeros((SEQ, half), F32)
    cosf = jnp.concatenate([cos, cos, jnp.ones((SEQ, LANES - ROPE_DIM), F32)], axis=1)
    sina = jnp.concatenate([-sin, z16, zeros], axis=1)
    sinb = jnp.concatenate([z16, sin, zeros], axis=1)
    return cosf, sina, sinb


CONV_TC = 256


def _conv_pre(x, w, b, row):
    acc = x * w[SSD_CONV - 1:SSD_CONV, :] + b
    shifted = [x]
    for j in range(1, SSD_CONV):
        xs = jnp.where(row >= j, pltpu.roll(x, j, 0), 0.0)
        shifted.append(xs)
        acc = acc + xs * w[SSD_CONV - 1 - j:SSD_CONV - j, :]
    return acc, shifted


def conv_fwd(xbc, w, b, name):
    def body(x_ref, w_ref, b_ref, o_ref):
        row = lax.broadcasted_iota(jnp.int32, (SEQ, CONV_TC), 0)
        pre, _ = _conv_pre(x_ref[...].astype(F32), w_ref[...], b_ref[...], row)
        o_ref[...] = _silu(pre)
    return pl.pallas_call(
        body, name=name, grid=(SSD_CONV_CH // CONV_TC,),
        in_specs=[pl.BlockSpec((SEQ, CONV_TC), lambda i: (0, i)), pl.BlockSpec((SSD_CONV, CONV_TC), lambda i: (0, i)),
                  pl.BlockSpec((1, CONV_TC), lambda i: (0, i))],
        out_specs=pl.BlockSpec((SEQ, CONV_TC), lambda i: (0, i)),
        out_shape=SDS((SEQ, SSD_CONV_CH), F32), compiler_params=_cparams(("parallel",)),
    )(xbc, w, b)


def conv_bwd(xbc, w, b, dxc, name):
    def body(x_ref, w_ref, b_ref, dy_ref, dx_ref, dw_ref, db_ref):
        row = lax.broadcasted_iota(jnp.int32, (SEQ, CONV_TC), 0)
        wv = w_ref[...]
        pre, shifted = _conv_pre(x_ref[...].astype(F32), wv, b_ref[...], row)
        sg = _sigmoid(pre)
        ds = dy_ref[...] * (sg * (1.0 + pre * (1.0 - sg)))
        dx = ds * wv[SSD_CONV - 1:SSD_CONV, :]
        for j in range(1, SSD_CONV):
            dsj = jnp.where(row < SEQ - j, pltpu.roll(ds, SEQ - j, 0), 0.0)
            dx = dx + dsj * wv[SSD_CONV - 1 - j:SSD_CONV - j, :]
        dx_ref[...] = dx.astype(dx_ref.dtype)
        for j in range(SSD_CONV):
            dw_ref[SSD_CONV - 1 - j:SSD_CONV - j, :] = jnp.sum(ds * shifted[j], axis=0, keepdims=True)
        db_ref[...] = jnp.sum(ds, axis=0, keepdims=True)
    return pl.pallas_call(
        body, name=name, grid=(SSD_CONV_CH // CONV_TC,),
        in_specs=[pl.BlockSpec((SEQ, CONV_TC), lambda i: (0, i)), pl.BlockSpec((SSD_CONV, CONV_TC), lambda i: (0, i)),
                  pl.BlockSpec((1, CONV_TC), lambda i: (0, i)), pl.BlockSpec((SEQ, CONV_TC), lambda i: (0, i))],
        out_specs=[pl.BlockSpec((SEQ, CONV_TC), lambda i: (0, i)), pl.BlockSpec((SSD_CONV, CONV_TC), lambda i: (0, i)),
                   pl.BlockSpec((1, CONV_TC), lambda i: (0, i))],
        out_shape=[SDS((SEQ, SSD_CONV_CH), BF16), SDS((SSD_CONV, SSD_CONV_CH), F32), SDS((1, SSD_CONV_CH), F32)],
        compiler_params=_cparams(("parallel",)),
    )(xbc, w, b, dxc)


N_CHUNKS = SEQ // SSD_CHUNK
N_PAIRS = SSD_HEADS // 2
PAIRS_PER_GROUP = N_PAIRS // SSD_GROUPS
B_OFF = SSD_INNER
C_OFF = SSD_INNER + SSD_GROUPS * SSD_STATE


def _ssd_prefix(dtr, dtr_t, dtb, dtb_t, alog, alog_t):
    ln = SSD_CHUNK
    dt = _softplus(dtr + dtb)
    dt_t = _softplus(dtr_t + dtb_t)
    dta = dt * (-jnp.exp(alog))
    dta_t = dt_t * (-jnp.exp(alog_t))
    r = lax.broadcasted_iota(jnp.int32, (ln, ln), 0)
    c = lax.broadcasted_iota(jnp.int32, (ln, ln), 1)
    a_cum = _dot((r >= c).astype(F32), dta, precision=HIGHEST)
    a_cum_t = _dot(dta_t, (r <= c).astype(F32), precision=HIGHEST)
    a_last = jnp.sum(dta_t, axis=1, keepdims=True)
    return dt, a_cum, a_cum_t, a_last


def _ssd_pair(x_pair, bg, cg, hp, dt, a_cum, a_cum_t, a_last, *, e0):
    ln = SSD_CHUNK
    lane = lax.broadcasted_iota(jnp.int32, (ln, LANES), 1)
    sub = lax.broadcasted_iota(jnp.int32, (LANES, SSD_STATE), 0)
    row = lax.broadcasted_iota(jnp.int32, (ln, ln), 0)
    col = lax.broadcasted_iota(jnp.int32, (ln, ln), 1)
    lo = lane < SSD_HEAD_DIM
    e1 = e0 + 1
    c0, c1 = a_cum[:, e0:e0 + 1], a_cum[:, e1:e1 + 1]
    r0, r1 = a_cum_t[e0:e0 + 1, :], a_cum_t[e1:e1 + 1, :]
    l0, l1 = a_last[e0:e0 + 1, :], a_last[e1:e1 + 1, :]
    xd = x_pair * jnp.where(lo, dt[:, e0:e0 + 1], dt[:, e1:e1 + 1])
    causal = row >= col
    cb = _bdot(cg, bg, NT)
    m0 = cb * jnp.exp(jnp.where(causal, c0 - r0, -jnp.inf))
    m1 = cb * jnp.exp(jnp.where(causal, c1 - r1, -jnp.inf))
    y = _bdot(m0, jnp.where(lo, xd, 0.0)) + _bdot(m1, jnp.where(lo, 0.0, xd))
    acum_pair = jnp.where(lo, c0, c1)
    y = y + _bdot(cg, hp, NT) * jnp.exp(acum_pair)
    last_pair = jnp.where(lo, l0, l1)
    st = _bdot(xd * jnp.exp(last_pair - acum_pair), bg, TN)
    h_out = hp * jnp.exp(jnp.where(sub < SSD_HEAD_DIM, l0, l1)) + st
    return y, h_out


def _ssd_in_specs(chunk_of):
    return [
        pl.BlockSpec((SSD_CHUNK, SSD_CONV_CH), lambda i: (chunk_of(i), 0)),
        pl.BlockSpec((SSD_CHUNK, HPAD), lambda i: (chunk_of(i), 0)),
        pl.BlockSpec((HPAD, SSD_CHUNK), lambda i: (0, chunk_of(i))),
        pl.BlockSpec((1, HPAD), lambda i: (0, 0)), pl.BlockSpec((HPAD, 1), lambda i: (0, 0)),
        pl.BlockSpec((1, HPAD), lambda i: (0, 0)), pl.BlockSpec((HPAD, 1), lambda i: (0, 0)),
    ]


def ssd_fwd(xc, dtr, dtr_t, dtb, dtb_t, alog, alog_t, name):
    def body(xc_ref, dtr_ref, dtrt_ref, dtb_ref, dtbt_ref, al_ref, alt_ref, y_ref, hs_ref, h_scr):
        @pl.when(pl.program_id(0) == 0)
        def _():
            h_scr[...] = jnp.zeros_like(h_scr)

        hs_ref[0] = h_scr[...]
        dt, a_cum, a_cum_t, a_last = _ssd_prefix(dtr_ref[...], dtrt_ref[...], dtb_ref[...], dtbt_ref[...],
                                                  al_ref[...], alt_ref[...])
        for pr in range(N_PAIRS):
            g = pr // PAIRS_PER_GROUP
            sl = slice(pr * LANES, (pr + 1) * LANES)
            bg = xc_ref[:, B_OFF + g * SSD_STATE:B_OFF + (g + 1) * SSD_STATE]
            cg = xc_ref[:, C_OFF + g * SSD_STATE:C_OFF + (g + 1) * SSD_STATE]
            y, h_out = _ssd_pair(xc_ref[:, sl], bg, cg, h_scr[sl, :], dt, a_cum, a_cum_t, a_last, e0=2 * pr)
            y_ref[:, sl] = y
            h_scr[sl, :] = h_out

    return pl.pallas_call(
        body, name=name, grid=(N_CHUNKS,), in_specs=_ssd_in_specs(lambda i: i),
        out_specs=[pl.BlockSpec((SSD_CHUNK, SSD_INNER), lambda i: (i, 0)),
                   pl.BlockSpec((1, SSD_INNER, SSD_STATE), lambda i: (i, 0, 0))],
        out_shape=[SDS((SEQ, SSD_INNER), F32), SDS((N_CHUNKS, SSD_INNER, SSD_STATE), F32)],
        scratch_shapes=[pltpu.VMEM((SSD_INNER, SSD_STATE), F32)],
        compiler_params=_cparams(("arbitrary",)),
    )(xc, dtr, dtr_t, dtb, dtb_t, alog, alog_t)


def ssd_bwd(xc, dtr, dtr_t, dtb, dtb_t, alog, alog_t, hs, dy, dxs_extra, name):
    rev = lambda i: N_CHUNKS - 1 - i

    def body(xc_ref, dtr_ref, dtrt_ref, dtb_ref, dtbt_ref, al_ref, alt_ref, hs_ref, dy_ref, dxe_ref,
             dxc_ref, ddtr_ref, ddtrt_ref, ddtb_ref, ddtbt_ref, dal_ref, dalt_ref, dh_scr):
        @pl.when(pl.program_id(0) == 0)
        def _():
            dh_scr[...] = jnp.zeros_like(dh_scr)
            for r in (ddtb_ref, ddtbt_ref, dal_ref, dalt_ref):
                r[...] = jnp.zeros_like(r)

        prefix_in = (dtr_ref[...], dtrt_ref[...], dtb_ref[...], dtbt_ref[...], al_ref[...], alt_ref[...])
        (dt, a_cum, a_cum_t, a_last), prefix_vjp = jax.vjp(_ssd_prefix, *prefix_in)
        d_dt = jnp.zeros_like(dt)
        d_acum = jnp.zeros_like(a_cum)
        d_acum_t = jnp.zeros_like(a_cum_t)
        d_alast = jnp.zeros_like(a_last)
        for g in range(SSD_GROUPS):
            bg = xc_ref[:, B_OFF + g * SSD_STATE:B_OFF + (g + 1) * SSD_STATE]
            cg = xc_ref[:, C_OFF + g * SSD_STATE:C_OFF + (g + 1) * SSD_STATE]
            d_bg = jnp.zeros_like(bg)
            d_cg = jnp.zeros_like(cg)
            for j in range(PAIRS_PER_GROUP):
                pr = g * PAIRS_PER_GROUP + j
                sl = slice(pr * LANES, (pr + 1) * LANES)
                _, vjp = jax.vjp(functools.partial(_ssd_pair, e0=2 * pr),
                                 xc_ref[:, sl], bg, cg, hs_ref[0, sl, :], dt, a_cum, a_cum_t, a_last)
                dx, dbg, dcg, dhp, ddt, dac, dact, dal = vjp((dy_ref[:, sl], dh_scr[sl, :]))
                dxc_ref[:, sl] = dx + dxe_ref[:, sl]
                dh_scr[sl, :] = dhp
                d_bg, d_cg = d_bg + dbg, d_cg + dcg
                d_dt, d_acum, d_acum_t, d_alast = d_dt + ddt, d_acum + dac, d_acum_t + dact, d_alast + dal
            dxc_ref[:, B_OFF + g * SSD_STATE:B_OFF + (g + 1) * SSD_STATE] = d_bg
            dxc_ref[:, C_OFF + g * SSD_STATE:C_OFF + (g + 1) * SSD_STATE] = d_cg
        g_dtr, g_dtrt, g_dtb, g_dtbt, g_al, g_alt = prefix_vjp((d_dt, d_acum, d_acum_t, d_alast))
        ddtr_ref[...] = g_dtr
        ddtrt_ref[...] = g_dtrt
        ddtb_ref[...] += g_dtb
        ddtbt_ref[...] += g_dtbt
        dal_ref[...] += g_al
        dalt_ref[...] += g_alt

    in_specs = _ssd_in_specs(rev) + [
        pl.BlockSpec((1, SSD_INNER, SSD_STATE), lambda i: (rev(i), 0, 0)),
        pl.BlockSpec((SSD_CHUNK, SSD_INNER), lambda i: (rev(i), 0)),
        pl.BlockSpec((SSD_CHUNK, SSD_INNER), lambda i: (rev(i), 0)),
    ]
    out_specs = [
        pl.BlockSpec((SSD_CHUNK, SSD_CONV_CH), lambda i: (rev(i), 0)),
        pl.BlockSpec((SSD_CHUNK, HPAD), lambda i: (rev(i), 0)),
        pl.BlockSpec((HPAD, SSD_CHUNK), lambda i: (0, rev(i))),
        pl.BlockSpec((1, HPAD), lambda i: (0, 0)), pl.BlockSpec((HPAD, 1), lambda i: (0, 0)),
        pl.BlockSpec((1, HPAD), lambda i: (0, 0)), pl.BlockSpec((HPAD, 1), lambda i: (0, 0)),
    ]
    out_shape = [SDS((SEQ, SSD_CONV_CH), F32), SDS((SEQ, HPAD), F32), SDS((HPAD, SEQ), F32),
                 SDS((1, HPAD), F32), SDS((HPAD, 1), F32), SDS((1, HPAD), F32), SDS((HPAD, 1), F32)]
    return pl.pallas_call(
        body, name=name, grid=(N_CHUNKS,), in_specs=in_specs, out_specs=out_specs, out_shape=out_shape,
        scratch_shapes=[pltpu.VMEM((SSD_INNER, SSD_STATE), F32)],
        compiler_params=_cparams(("arbitrary",)),
    )(xc, dtr, dtr_t, dtb, dtb_t, alog, alog_t, hs, dy, dxs_extra)


ATTN_SCALE = ATTN_HEAD_DIM ** -0.5


def _attn_scores(q, kp, kc, has_prev):
    qi = lax.broadcasted_iota(jnp.int32, (ATTN_BLOCK, ATTN_BLOCK), 0)
    kj = lax.broadcasted_iota(jnp.int32, (ATTN_BLOCK, ATTN_BLOCK), 1)
    s_c = jnp.where(qi >= kj, _bdot(q, kc, NT) * ATTN_SCALE, -jnp.inf)
    s_p = jnp.where((kj >= qi) & has_prev, _bdot(q, kp, NT) * ATTN_SCALE, -jnp.inf)
    return s_p, s_c


UNITS_PER_PATTERN = SEQ // ATTN_BLOCK
ATTN_UNROLL = 2


def _for_units(unit):
    for g, d in enumerate(ATTN_DILATIONS):
        nb = UNITS_PER_PATTERN // d
        span = d * ATTN_BLOCK

        def one(i, carry, g=g, d=d, nb=nb, span=span):
            r = i >> (nb.bit_length() - 1)
            n = i & (nb - 1)
            start = r + n * span
            prev = jnp.where(n > 0, start - span, start)
            unit(g, pl.ds(start, ATTN_BLOCK, stride=d), pl.ds(prev, ATTN_BLOCK, stride=d), n > 0)
            return carry
        lax.fori_loop(0, UNITS_PER_PATTERN, one, 0, unroll=ATTN_UNROLL)


def _head_specs(n_q_groups):
    blk = (SEQ, ATTN_HEAD_DIM)
    q_specs = [pl.BlockSpec(blk, functools.partial(lambda h, g: (0, g * ATTN_KV_HEADS + h), g=g)) for g in range(n_q_groups)]
    head = pl.BlockSpec(blk, lambda h: (0, h))
    table = pl.BlockSpec(blk, lambda h: (0, 0))
    return q_specs, head, table


def attn_fwd(q, k, v, tabs, name):
    q_specs, head, table = _head_specs(ATTN_N_PAT)

    def body(q0_ref, q1_ref, q2_ref, k_ref, v_ref, c_ref, sa_ref, sb_ref, y_ref, lse_ref, *scr):
        qs, og, ls, ks, vs = scr[0:3], scr[3:6], scr[6:9], scr[9], scr[10]
        c, sa, sb = c_ref[...], sa_ref[...], sb_ref[...]
        for g, q_ref in enumerate((q0_ref, q1_ref, q2_ref)):
            qs[g][...] = _rope(q_ref[...].astype(F32), c, sa, sb)
        ks[...] = _rope(k_ref[...].astype(F32), c, sa, sb)
        vs[...] = v_ref[...].astype(F32)

        def unit(g, rows, prows, has_prev):
            s_p, s_c = _attn_scores(qs[g][rows, :], ks[prows, :], ks[rows, :], has_prev)
            m = jnp.maximum(jnp.max(s_c, axis=1, keepdims=True), jnp.max(s_p, axis=1, keepdims=True))
            p_c, p_p = jnp.exp(s_c - m), jnp.exp(s_p - m)
            l = jnp.sum(p_c, axis=1, keepdims=True) + jnp.sum(p_p, axis=1, keepdims=True)
            o = _bdot(p_c, vs[rows, :]) + _bdot(p_p, vs[prows, :])
            og[g][rows, :] = o / l
            ls[g][rows, :] = jnp.broadcast_to(m + jnp.log(l), (ATTN_BLOCK, LANES))

        _for_units(unit)
        l0, l1, l2 = ls[0][...], ls[1][...], ls[2][...]
        m = jnp.maximum(jnp.maximum(l0, l1), l2)
        e0, e1, e2 = jnp.exp(l0 - m), jnp.exp(l1 - m), jnp.exp(l2 - m)
        den = e0 + e1 + e2
        y_ref[...] = ((e0 * og[0][...] + e1 * og[1][...] + e2 * og[2][...]) / den).astype(y_ref.dtype)
        lse_ref[...] = m + jnp.log(den)

    blk = (SEQ, ATTN_HEAD_DIM)
    return pl.pallas_call(
        body, name=name, grid=(ATTN_KV_HEADS,), in_specs=[*q_specs, head, head, table, table, table],
        out_specs=[head, head], out_shape=[SDS((SEQ, ATTN_OUT), BF16), SDS((SEQ, ATTN_OUT), F32)],
        scratch_shapes=[pltpu.VMEM(blk, F32)] * (3 * ATTN_N_PAT + 2),
        compiler_params=_cparams(("parallel",)),
    )(q, q, q, k, v, *tabs)


def attn_bwd(q, k, v, tabs, y, lse, dy, name):
    q_specs, head, table = _head_specs(ATTN_N_PAT)

    def body(q0_ref, q1_ref, q2_ref, k_ref, v_ref, c_ref, sa_ref, sb_ref, y_ref, lse_ref, dy_ref,
             dq0_ref, dq1_ref, dq2_ref, dk_ref, dv_ref, *scr):
        qs, dqs, ks, dks, dd, dvs, vs = scr[0:3], scr[3:6], scr[6], scr[7], scr[8], scr[9], scr[10]
        c, sa, sb = c_ref[...], sa_ref[...], sb_ref[...]
        for g, q_ref in enumerate((q0_ref, q1_ref, q2_ref)):
            qs[g][...] = _rope(q_ref[...].astype(F32), c, sa, sb)
        ks[...] = _rope(k_ref[...].astype(F32), c, sa, sb)
        vs[...] = v_ref[...].astype(F32)
        dks[...] = jnp.zeros_like(dks)
        dvs[...] = jnp.zeros_like(dvs)
        dyv = dy_ref[...]
        dd[...] = jnp.broadcast_to(jnp.sum(dyv * y_ref[...].astype(F32), axis=1, keepdims=True), dd.shape)

        def unit(g, rows, prows, has_prev):
            qv = qs[g][rows, :].astype(BF16)
            kc, kp = ks[rows, :].astype(BF16), ks[prows, :].astype(BF16)
            vc, vp = vs[rows, :].astype(BF16), vs[prows, :].astype(BF16)
            do = dy_ref[rows, :].astype(BF16)
            s_p, s_c = _attn_scores(qv, kp, kc, has_prev)
            lse_u = lse_ref[rows, :][:, 0:1]
            dsum = dd[rows, :][:, 0:1]
            p_c, p_p = jnp.exp(s_c - lse_u), jnp.exp(s_p - lse_u)
            ds_c = (p_c * (_dot(do, vc, NT) - dsum) * ATTN_SCALE).astype(BF16)
            ds_p = (p_p * (_dot(do, vp, NT) - dsum) * ATTN_SCALE).astype(BF16)
            dqs[g][rows, :] = _dot(ds_c, kc) + _dot(ds_p, kp)
            dks[rows, :] += _dot(ds_c, qv, TN)
            dks[prows, :] += _dot(ds_p, qv, TN)
            dvs[rows, :] += _bdot(p_c, do, TN)
            dvs[prows, :] += _bdot(p_p, do, TN)

        _for_units(unit)
        for g, dq_ref in enumerate((dq0_ref, dq1_ref, dq2_ref)):
            dq_ref[...] = _rope(dqs[g][...], c, -sa, -sb).astype(dq_ref.dtype)
        dk_ref[...] = _rope(dks[...], c, -sa, -sb).astype(dk_ref.dtype)
        dv_ref[...] = dvs[...].astype(dv_ref.dtype)

    blk = (SEQ, ATTN_HEAD_DIM)
    out = SDS((SEQ, ATTN_OUT), BF16)
    return pl.pallas_call(
        body, name=name, grid=(ATTN_KV_HEADS,), in_specs=[*q_specs, head, head, table, table, table, head, head, head],
        out_specs=[head] * 5, out_shape=[out] * 5,
        scratch_shapes=[pltpu.VMEM(blk, F32)] * (2 * ATTN_N_PAT + 5),
        compiler_params=_cparams(("parallel",)),
    )(q, q, q, k, v, *tabs, y, lse, dy)


def layer_fwd(h, getw, small, tabs, li):
    n = f"l{li}_"
    sv = {}
    w = dict(getw(0, h))
    u = rms_fwd(h, small["norm_mix"], n + "rms_mix")
    z = matmul(u, w["w_z"], name=n + "mm_z", tb=True, out_dtype=BF16)
    xbc = matmul(u, w["w_xbc"], name=n + "mm_xbc", tb=True, out_dtype=BF16)
    dtr = matmul(u, w["w_dt"], name=n + "mm_dt", tb=True)
    q = matmul(u, w["w_q"], name=n + "mm_q", tb=True, out_dtype=BF16)
    k = matmul(u, w["w_k"], name=n + "mm_k", tb=True, out_dtype=BF16)
    v = matmul(u, w["w_v"], name=n + "mm_v", tb=True, out_dtype=BF16)
    gs = matmul(u, w["w_gs"], name=n + "mm_gs", tb=True, out_dtype=BF16)
    ga = matmul(u, w["w_ga"], name=n + "mm_ga", tb=True, out_dtype=BF16)
    xc = conv_fwd(xbc, w["conv_w"], small["conv_b"], n + "conv")
    dtr_t = dtr.T
    y_ssd, hs = ssd_fwd(xc, dtr, dtr_t, small["dt_bias"], small["dt_bias"].T, small["a_log"], small["a_log"].T, n + "ssd")
    yn = ssd_post_fwd(y_ssd, xc, z, small["d_skip_x"], small["ssd_norm"], n + "ssd_post")
    y_attn, lse = attn_fwd(q, k, v, tabs, n + "attn")
    w.update(getw(1, y_ssd))
    a = matmul(yn, w["w_ssd_branch"], name=n + "mm_a", out_dtype=BF16)
    b = matmul(y_attn, w["w_attn_branch"], name=n + "mm_b", out_dtype=BF16)
    merged = gate_fwd(a, b, gs, ga, n + "gate")
    h1 = matmul(merged, w["w_out"], name=n + "mm_o", add=h)
    w.update(getw(2, h1))
    u2 = rms_fwd(h1, small["norm_ffn"], n + "rms_ffn")
    gu = matmul(u2, w["w_gate_up"], name=n + "mm_gu", tb=True, out_dtype=BF16)
    act = swiglu_fwd(gu, n + "swiglu")
    h2 = matmul(act, w["w_down"], name=n + "mm_down", add=h1)
    sv.update(h=h, u=u, z=z, xbc=xbc, dtr=dtr, dtr_t=dtr_t, gs=gs, ga=ga, xc=xc, y_ssd=y_ssd, hs=hs, yn=yn,
              q=q, k=k, v=v, y_attn=y_attn, lse=lse, a=a, b=b, merged=merged, h1=h1, u2=u2, gu=gu, act=act, w=w)
    return h2, sv


def layer_bwd(dh, sv, small, tabs, li, emit):
    n = f"l{li}_b_"
    w = sv["w"]
    gw, gsm = {}, {}
    dact = matmul(dh, w["w_down"], name=n + "mm_dact", tb=True, out_dtype=BF16)
    gw["w_down"] = matmul(sv["act"], dh, name=n + "mm_dwdown", ta=True, out_dtype=BF16)
    dgu = swiglu_bwd(sv["gu"], dact, n + "swiglu")
    gw["w_gate_up"] = matmul(dgu, sv["u2"], name=n + "mm_dwgu", ta=True, out_dtype=BF16)
    tok = emit(2, gw)
    du2 = matmul(dgu, w["w_gate_up"], name=n + "mm_du2")
    dh1, gsm["norm_ffn"] = rms_bwd(sv["h1"], du2, dh, small["norm_ffn"] + tok, n + "rms_ffn")
    dmerged = matmul(dh1, w["w_out"], name=n + "mm_dmerged", tb=True)
    gw["w_out"] = matmul(sv["merged"], dh1, name=n + "mm_dwo", ta=True, out_dtype=BF16)
    da, db, dgs, dga = gate_bwd(sv["a"], sv["b"], sv["gs"], sv["ga"], dmerged, n + "gate")
    gw["w_ssd_branch"] = matmul(sv["yn"], da, name=n + "mm_dwa", ta=True, out_dtype=BF16)
    gw["w_attn_branch"] = matmul(sv["y_attn"], db, name=n + "mm_dwb", ta=True, out_dtype=BF16)
    tok = emit(1, gw)
    dyn = matmul(da, w["w_ssd_branch"], name=n + "mm_dyn", tb=True)
    dyattn = matmul(db, w["w_attn_branch"], name=n + "mm_dyattn", tb=True)
    dy_ssd, dxs_extra, dz, gsm["d_skip_x"], gsm["ssd_norm"] = ssd_post_bwd(
        sv["y_ssd"], sv["xc"], sv["z"], small["d_skip_x"] + tok, small["ssd_norm"], dyn, n + "ssd_post")
    dxc, ddtr, ddtr_t, ddtb, ddtb_t, dal, dal_t = ssd_bwd(
        sv["xc"], sv["dtr"], sv["dtr_t"], small["dt_bias"], small["dt_bias"].T, small["a_log"], small["a_log"].T,
        sv["hs"], dy_ssd, dxs_extra, n + "ssd")
    ddtr = (ddtr + ddtr_t.T).astype(BF16)
    gsm["dt_bias"] = ddtb + ddtb_t.T
    gsm["a_log"] = dal + dal_t.T
    dxbc, gw["conv_w"], gsm["conv_b"] = conv_bwd(sv["xbc"], w["conv_w"], small["conv_b"], dxc, n + "conv")
    dq0, dq1, dq2, dk, dv = attn_bwd(sv["q"], sv["k"], sv["v"], tabs, sv["y_attn"], sv["lse"], dyattn, n + "attn")
    u = sv["u"]
    segs = [("w_z", dz), ("w_xbc", dxbc), ("w_dt", ddtr), ("w_q0", dq0), ("w_q1", dq1), ("w_q2", dq2),
            ("w_k", dk), ("w_v", dv), ("w_gs", dgs), ("w_ga", dga)]
    gin = [matmul(dseg, u, name=n + "mm_d" + key, ta=True, out_dtype=BF16) for key, dseg in segs]
    gin[2] = gin[2][:SSD_HEADS]
    gw["w_in"] = jnp.concatenate(gin, axis=0)
    tok = emit(0, gw)
    du = jnp.zeros((SEQ, D_MODEL), F32) + tok
    for key, dseg in segs:
        du = matmul(dseg, w[key], name=n + "mm_du_" + key, add=du)
    dh0, gsm["norm_mix"] = rms_bwd(sv["h"], du, dh1, small["norm_mix"] + tok, n + "rms_mix")
    return dh0, gsm


def _my_place():
    return lax.axis_index("x"), lax.axis_index("y"), lax.axis_index("c")


def _flip(place, k):
    x, y, c = place
    return (1 - x if k & 4 else x, 1 - y if k & 2 else y, 1 - c if k & 1 else c)


def _index(place):
    return 4 * place[0] + 2 * place[1] + place[2]


ANY = pl.BlockSpec(memory_space=pl.ANY)
CHIP_FLIPS = (4, 2, 6)


def all_gather(xs, name):
    na = len(xs)

    def body(*refs):
        x_refs, o_refs = refs[:na], refs[na:2 * na]
        send_sems, recv_sems, local_sems = refs[2 * na:]
        me = _my_place()
        sibling = _flip(me, 1)
        chips = [_flip(me, f) for f in CHIP_FLIPS]

        def copy(a, kk, block, to, src=None):
            dst = o_refs[a].at[_index(block)]
            return pltpu.make_async_remote_copy(
                src_ref=dst if src is None else src, dst_ref=dst, send_sem=send_sems.at[a, kk],
                recv_sem=recv_sems.at[a, kk], device_id=to, device_id_type=MESH)

        mine = [pltpu.make_async_copy(x_refs[a], o_refs[a].at[_index(me)], local_sems.at[a]) for a in range(na)]
        for cp in mine:
            cp.start()
        first = []
        for j, chip in enumerate(chips):
            first += [copy(a, 1 + j, me, chip, src=x_refs[a]) for a in range(na)]
        first += [copy(a, 0, me, sibling, src=x_refs[a]) for a in range(na)]
        for cp in first:
            cp.start()
        passed = []
        for j, chip in enumerate(chips):
            for a in range(na):
                copy(a, 1 + j, chip, me).wait_recv()
                cp = copy(a, 4 + j, chip, sibling)
                cp.start()
                passed.append(cp)
        for a in range(na):
            copy(a, 0, sibling, me).wait_recv()
        for j, chip in enumerate(chips):
            for a in range(na):
                copy(a, 4 + j, _flip(chip, 1), me).wait_recv()
        for cp in first + passed:
            cp.wait_send()
        for cp in mine:
            cp.wait()

    return pl.pallas_call(
        body, name=name, in_specs=[ANY] * na, out_specs=[ANY] * na,
        out_shape=[SDS((N_DEV,) + t.shape, t.dtype) for t in xs],
        scratch_shapes=[pltpu.SemaphoreType.DMA((na, N_DEV - 1)), pltpu.SemaphoreType.DMA((na, N_DEV - 1)),
                        pltpu.SemaphoreType.DMA((na,))],
    )(*xs)


HBM = pl.BlockSpec(memory_space=pltpu.HBM)
SEM = pl.BlockSpec(memory_space=pltpu.SEMAPHORE)
EFFECT = pltpu.SideEffectType.DATAFLOW_SIDE_EFFECTING
N_PEERS = N_DEV - 1


def _split_copy(src_ref, land_ref, send_sem, recv_sem, me, kk, scatter, landed_from_peer):
    peer = _flip(me, kk)
    src = src_ref.at[_index(peer)] if scatter else src_ref
    dst = land_ref.at[_index(peer if landed_from_peer else me)]
    return pltpu.make_async_remote_copy(src_ref=src, dst_ref=dst, send_sem=send_sem, recv_sem=recv_sem,
                                        device_id=peer, device_id_type=MESH)


def exchange_start(srcs, lands, group_sizes, scatter, name):
    na, ng = len(srcs), len(group_sizes)

    def body(*refs):
        s_refs, l_refs = refs[:na], refs[na:2 * na]
        sems = refs[2 * na:2 * na + 2 * ng]
        token = refs[-1]
        me = _my_place()
        a = 0
        for gi, gsz in enumerate(group_sizes):
            for j in range(gsz):
                for kk in range(1, N_DEV):
                    slot = j * N_PEERS + kk - 1
                    _split_copy(s_refs[a], l_refs[a], sems[2 * gi].at[slot], sems[2 * gi + 1].at[slot],
                                me, kk, scatter, False).start()
                a += 1
        token[...] = jnp.zeros_like(token)

    sem_shapes = []
    for gsz in group_sizes:
        sem_shapes += [pltpu.SemaphoreType.DMA((gsz * N_PEERS,))] * 2
    ins = [pltpu.with_memory_space_constraint(t, pltpu.HBM) for t in (*srcs, *lands)]
    res = pl.pallas_call(
        body, name=name, in_specs=[HBM] * (2 * na),
        out_specs=[SEM] * (2 * ng) + [HBM] * (2 * na) + [pl.BlockSpec(memory_space=pltpu.VMEM)],
        out_shape=sem_shapes + [pltpu.HBM(t.shape, t.dtype) for t in ins] + [SDS((8, LANES), F32)],
        input_output_aliases={i: 2 * ng + i for i in range(2 * na)},
        compiler_params=pltpu.CompilerParams(has_side_effects=EFFECT),
    )(*ins)
    sems = [(res[2 * gi], res[2 * gi + 1]) for gi in range(ng)]
    thru = res[2 * ng:2 * ng + 2 * na]
    return sems, thru[:na], thru[na:], res[-1]


def exchange_wait(srcs, lands, sems, after, scatter, name):
    n = len(srcs)

    def body(*refs):
        s_refs, l_refs = refs[:n], refs[n:2 * n]
        send_sems, recv_sems = refs[2 * n], refs[2 * n + 1]
        me = _my_place()
        for j in range(n):
            for kk in range(1, N_DEV):
                slot = j * N_PEERS + kk - 1
                cp = _split_copy(s_refs[j], l_refs[j], send_sems.at[slot], recv_sems.at[slot], me, kk, scatter, True)
                cp.wait_send()
                cp.wait_recv()

    res = pl.pallas_call(
        body, name=name, in_specs=[HBM] * (2 * n) + [SEM, SEM, ANY], out_specs=[HBM] * (2 * n),
        out_shape=[pltpu.HBM(t.shape, t.dtype) for t in (*srcs, *lands)],
        input_output_aliases={i: i for i in range(2 * n)},
        compiler_params=pltpu.CompilerParams(has_side_effects=EFFECT),
    )(*srcs, *lands, sems[0], sems[1], after)
    return res[n:]


def landing_zone(block, me_index):
    land = lax.empty((N_DEV,) + block.shape, block.dtype)
    return lax.dynamic_update_slice(land, block[None], (me_index,) + (0,) * block.ndim)


def sum_parts(parts, name, row_major_3d=False):
    _, r, c = parts.shape
    tc = _pick(c, (256, 128))

    def body(p_ref, o_ref):
        acc = p_ref[0].astype(F32)
        for i in range(1, N_DEV):
            acc = acc + p_ref[i].astype(F32)
        if row_major_3d:
            o_ref[:, 0, :] = acc
        else:
            o_ref[...] = acc

    out_spec = pl.BlockSpec((r, 1, tc), lambda i: (0, 0, i)) if row_major_3d else pl.BlockSpec((r, tc), lambda i: (0, i))
    return pl.pallas_call(
        body, name=name, grid=(c // tc,), in_specs=[pl.BlockSpec((N_DEV, r, tc), lambda i: (0, 0, i))],
        out_specs=out_spec, out_shape=SDS((r, 1, c) if row_major_3d else (r, c), F32),
        compiler_params=_cparams(("parallel",)),
    )(parts)


ADAMW_BLOCK_BYTES = 2 * 1024 * 1024


def adamw(w, g, m, v, name):
    shape = w.shape
    lay, rows, cols = ((1, 1) + shape)[-3:]
    tr = _pick(rows, (256, 128))
    tc = cols if tr * cols * 4 <= ADAMW_BLOCK_BYTES else _pick(cols, (256, 128))
    c1 = 1.0 / (1.0 - ADAM_B1 ** ADAM_STEP)
    c2 = 1.0 / (1.0 - ADAM_B2 ** ADAM_STEP)

    def body(w_ref, g_ref, m_ref, v_ref, d_ref, nm_ref, nv_ref):
        gg = g_ref[...]
        nm = ADAM_B1 * m_ref[...] + (1.0 - ADAM_B1) * gg
        nv = ADAM_B2 * v_ref[...] + (1.0 - ADAM_B2) * (gg * gg)
        d_ref[...] = -ADAM_LR * ((nm * c1) / (jnp.sqrt(nv * c2) + ADAM_EPS) + ADAM_WD * w_ref[...])
        nm_ref[...] = nm
        nv_ref[...] = nv

    spec = pl.BlockSpec((1, tr, tc), lambda l, i, j: (l, i, j))
    outs = pl.pallas_call(
        body, name=name, grid=(lay, rows // tr, cols // tc), in_specs=[spec] * 4, out_specs=[spec] * 3,
        out_shape=[SDS((lay, rows, cols), F32)] * 3, compiler_params=_cparams(("parallel",) * 3),
    )(*[t.reshape(lay, rows, cols) for t in (w, g, m, v)])
    return [o.reshape(shape) for o in outs]


def adamw_layer_inner(w, gs, m, v, name):
    rows, lay, cols = w.shape
    tr = _pick(rows, (256, 220, 128))
    c1 = 1.0 / (1.0 - ADAM_B1 ** ADAM_STEP)
    c2 = 1.0 / (1.0 - ADAM_B2 ** ADAM_STEP)

    def body(*refs):
        w_ref, m_ref, v_ref = refs[:3]
        g_refs = refs[3:3 + lay]
        go_ref, d_ref, nm_ref, nv_ref = refs[3 + lay:]
        for l, g_ref in enumerate(g_refs):
            gg = g_ref[:, 0, :]
            nm = ADAM_B1 * m_ref[:, l, :] + (1.0 - ADAM_B1) * gg
            nv = ADAM_B2 * v_ref[:, l, :] + (1.0 - ADAM_B2) * (gg * gg)
            d_ref[:, l, :] = -ADAM_LR * ((nm * c1) / (jnp.sqrt(nv * c2) + ADAM_EPS) + ADAM_WD * w_ref[:, l, :])
            go_ref[:, l, :] = gg
            nm_ref[:, l, :] = nm
            nv_ref[:, l, :] = nv

    inner = pl.BlockSpec((tr, lay, cols), lambda i: (i, 0, 0))
    plain = pl.BlockSpec((tr, 1, cols), lambda i: (i, 0, 0))
    return pl.pallas_call(
        body, name=name, grid=(rows // tr,), in_specs=[inner] * 3 + [plain] * lay, out_specs=[inner] * 4,
        out_shape=[SDS((rows, lay, cols), F32)] * 4, compiler_params=_cparams(("parallel",)),
    )(w, m, v, *gs)


BIG = ("w_in", "conv_w", "w_ssd_branch", "w_attn_branch", "w_out", "w_gate_up", "w_down")
TRANSPOSED = ("w_in", "w_gate_up")
SMALL = ("norm_mix", "conv_b", "dt_bias", "a_log", "d_skip", "ssd_norm", "norm_ffn")
SMALL_SIZE = {"norm_mix": 1024, "conv_b": 3072, "dt_bias": 32, "a_log": 32, "d_skip": 32, "ssd_norm": 2048, "norm_ffn": 1024}
FLAT_W = 512
SMALL_TOTAL = DEPTH * sum(SMALL_SIZE.values()) + D_MODEL + LANES
SMALL_ROWS = 32
assert SMALL_ROWS * FLAT_W >= SMALL_TOTAL


GROUPS = (("w_in", "conv_w"), ("w_ssd_branch", "w_attn_branch", "w_out"), ("w_gate_up", "w_down"))


def to_wire(k, shard):
    if k in TRANSPOSED:
        return shard.T.astype(BF16)
    return shard if k == "conv_w" else shard.astype(BF16)


def full_weights(k, g):
    if k == "conv_w":
        return {k: g.transpose(1, 0, 2).reshape(SSD_CONV, SSD_CONV_CH)}
    full = g.reshape(-1, g.shape[-1])
    if k != "w_in":
        return {k: full}
    w, off = {}, 0
    for nm, r in IN_ROWS:
        w[nm] = full[off:off + r]
        off += r
    w["w_q"] = full[sum(r for _, r in IN_ROWS[:3]):sum(r for _, r in IN_ROWS[:6])]
    w["w_dt"] = jnp.pad(w["w_dt"], ((0, HPAD - SSD_HEADS), (0, 0)))
    return w


def grads_to_wire(k, g):
    if k == "conv_w":
        return g.reshape(SSD_CONV, N_DEV, SSD_CONV_CH // N_DEV).transpose(1, 0, 2)
    return g.reshape(N_DEV, g.shape[0] // N_DEV, g.shape[1])


def _pad_heads(t):
    return jnp.pad(t.reshape(1, SSD_HEADS), ((0, 0), (0, HPAD - SSD_HEADS)))


def local_step(x, target, getw, emit, smalls, norm_final):
    tabs = rope_tables()
    sms = []
    for li in range(DEPTH):
        s = smalls[li]
        sms.append({
            "norm_mix": s["norm_mix"].reshape(1, -1), "conv_b": s["conv_b"].reshape(1, -1),
            "dt_bias": _pad_heads(s["dt_bias"]), "a_log": _pad_heads(s["a_log"]),
            "d_skip_x": jnp.repeat(s["d_skip"], SSD_HEAD_DIM).reshape(1, -1),
            "ssd_norm": s["ssd_norm"].reshape(1, -1), "norm_ffn": s["norm_ffn"].reshape(1, -1)})
    h = x
    saved = []
    for li in range(DEPTH):
        h, sv = layer_fwd(h, functools.partial(getw, li), sms[li], tabs, li)
        saved.append(sv)
    dh, g_final, loss = loss_head(h, target, norm_final.reshape(1, -1), "loss_head")
    gsms = [None] * DEPTH
    for li in reversed(range(DEPTH)):
        dh, gsm = layer_bwd(dh, saved[li], sms[li], tabs, li, functools.partial(emit, li))
        gsms[li] = {
            "norm_mix": gsm["norm_mix"].reshape(-1), "conv_b": gsm["conv_b"].reshape(-1),
            "dt_bias": gsm["dt_bias"][0, :SSD_HEADS], "a_log": gsm["a_log"][0, :SSD_HEADS],
            "d_skip": gsm["d_skip_x"].reshape(SSD_HEADS, SSD_HEAD_DIM).sum(axis=1),
            "ssd_norm": gsm["ssd_norm"].reshape(-1), "norm_ffn": gsm["norm_ffn"].reshape(-1)}
    return loss, dh, gsms, g_final.reshape(-1)


def kernel(x, norm_mix, w_in, conv_w, conv_b, dt_bias, a_log, d_skip, ssd_norm, w_ssd_branch, w_attn_branch, w_out, norm_ffn, w_gate_up, w_down, norm_final, loss_target, m_norm_mix, m_w_in, m_conv_w, m_conv_b, m_dt_bias, m_a_log, m_d_skip, m_ssd_norm, m_w_ssd_branch, m_w_attn_branch, m_w_out, m_norm_ffn, m_w_gate_up, m_w_down, m_norm_final, v_norm_mix, v_w_in, v_conv_w, v_conv_b, v_dt_bias, v_a_log, v_d_skip, v_ssd_norm, v_w_ssd_branch, v_w_attn_branch, v_w_out, v_norm_ffn, v_w_gate_up, v_w_down, v_norm_final):
    wv = dict(norm_mix=norm_mix, w_in=w_in, conv_w=conv_w, conv_b=conv_b, dt_bias=dt_bias, a_log=a_log, d_skip=d_skip,
              ssd_norm=ssd_norm, w_ssd_branch=w_ssd_branch, w_attn_branch=w_attn_branch, w_out=w_out, norm_ffn=norm_ffn,
              w_gate_up=w_gate_up, w_down=w_down, norm_final=norm_final)
    mv = dict(norm_mix=m_norm_mix, w_in=m_w_in, conv_w=m_conv_w, conv_b=m_conv_b, dt_bias=m_dt_bias, a_log=m_a_log,
              d_skip=m_d_skip, ssd_norm=m_ssd_norm, w_ssd_branch=m_w_ssd_branch, w_attn_branch=m_w_attn_branch,
              w_out=m_w_out, norm_ffn=m_norm_ffn, w_gate_up=m_w_gate_up, w_down=m_w_down, norm_final=m_norm_final)
    vv = dict(norm_mix=v_norm_mix, w_in=v_w_in, conv_w=v_conv_w, conv_b=v_conv_b, dt_bias=v_dt_bias, a_log=v_a_log,
              d_skip=v_d_skip, ssd_norm=v_ssd_norm, w_ssd_branch=v_w_ssd_branch, w_attn_branch=v_w_attn_branch,
              w_out=v_w_out, norm_ffn=v_norm_ffn, w_gate_up=v_w_gate_up, w_down=v_w_down, norm_final=v_norm_final)
    order = ("norm_mix", "w_in", "conv_w", "conv_b", "dt_bias", "a_log", "d_skip", "ssd_norm", "w_ssd_branch",
             "w_attn_branch", "w_out", "norm_ffn", "w_gate_up", "w_down", "norm_final")

    me_index = _index(_my_place())
    smalls = [{k: wv[k][li] for k in SMALL} for li in range(DEPTH)]
    n_groups = len(GROUPS)

    first_lands = all_gather([to_wire(k, wv[k][0]) for k in GROUPS[0]], "gather_first")
    later = [(li, gi) for li in range(DEPTH) for gi in range(n_groups)][1:]
    behind_first = first_lands[1][0, 0, 0] * 0.0
    srcs = [to_wire(k, wv[k][li] + behind_first if k == "conv_w" else wv[k][li]) for li, gi in later for k in GROUPS[gi]]
    sizes = [len(GROUPS[gi]) for _, gi in later]
    w_sems, w_srcs, w_lands, token = exchange_start(srcs, [landing_zone(s, me_index) for s in srcs], sizes, False, "gather_start")
    smalls[0]["norm_mix"] = smalls[0]["norm_mix"] + token[0, 0]

    def getw(li, gi, after):
        if (li, gi) == (0, 0):
            lands = first_lands
        else:
            slot = later.index((li, gi))
            sl = slice(sum(sizes[:slot]), sum(sizes[:slot + 1]))
            lands = exchange_wait(w_srcs[sl], w_lands[sl], w_sems[slot], after, False, f"gather_wait_{li}_{gi}")
        w = {}
        for k, land in zip(GROUPS[gi], lands):
            w.update(full_weights(k, land))
        return w

    pending = []

    def emit(li, gi, gw):
        parts = [grads_to_wire(k, gw[k]) for k in GROUPS[gi]]
        lands = [landing_zone(lax.dynamic_index_in_dim(p, me_index, 0, keepdims=False), me_index) for p in parts]
        sems, p_thru, l_thru, tok = exchange_start(parts, lands, [len(parts)], True, f"grads_start_{li}_{gi}")
        pending.append((li, gi, sems[0], p_thru, l_thru))
        return tok[0, 0]

    loss_p, dx, gsms, g_final = local_step(x[0], loss_target[0], getw, emit, smalls, norm_final)

    grads, deltas, new_m, new_v = {}, {}, {}, {}

    def update(k):
        if k == "w_in":
            inner = lambda t: t.transpose(2, 0, 1)
            outs = adamw_layer_inner(inner(wv[k]), shard_g[k], inner(mv[k]), inner(vv[k]), "adamw_" + k)
            grads[k], deltas[k], new_m[k], new_v[k] = (t.transpose(1, 2, 0) for t in outs)
            return outs[3]
        if k in BIG:
            grads[k] = jnp.stack([g.T if k in TRANSPOSED else g for g in shard_g[k]])
        deltas[k], new_m[k], new_v[k] = adamw(wv[k], grads[k], mv[k], vv[k], "adamw_" + k)
        return new_v[k]

    shard_g = {k: [None] * DEPTH for k in BIG}

    def collect(entry, after):
        li, gi, sems, p_thru, l_thru = entry
        recv = exchange_wait(p_thru, l_thru, sems, after, True, f"grads_wait_{li}_{gi}")
        for k, r in zip(GROUPS[gi], recv):
            if k == "conv_w":
                r = r.reshape(N_DEV, 1, -1)
            after = sum_parts(r, f"sum_{k}_{li}", row_major_3d=(k == "w_in"))
            shard_g[k][li] = after if k in TRANSPOSED else after.reshape(wv[k].shape[1:])
        return after

    after = dx
    for entry in pending[:-1]:
        after = collect(entry, after)
    done = [after[:1, :1].reshape(1)]
    for gi in (2, 1):
        for k in GROUPS[gi]:
            done.append(update(k).reshape(-1)[:1])

    flat = [gsms[li][k] for li in range(DEPTH) for k in SMALL] + [g_final, loss_p.reshape(-1)]
    flat.append(jnp.zeros((SMALL_ROWS * FLAT_W - SMALL_TOTAL,), F32))
    small_all = all_gather([jnp.concatenate(flat).reshape(SMALL_ROWS, FLAT_W)], "gather_small")[0]
    small_sum = sum_parts(small_all, "sum_small").reshape(-1)
    off = 0
    per_layer = {k: [] for k in SMALL}
    for li in range(DEPTH):
        for k in SMALL:
            per_layer[k].append(small_sum[off:off + SMALL_SIZE[k]])
            off += SMALL_SIZE[k]
    for k in SMALL:
        grads[k] = jnp.stack(per_layer[k])
    grads["norm_final"] = small_sum[off:off + D_MODEL]
    loss = small_sum[off + D_MODEL]
    for k in (*SMALL, "norm_final"):
        done.append(update(k).reshape(-1)[:1])

    collect(pending[-1], jnp.concatenate(done))
    for k in GROUPS[0]:
        update(k)

    return (loss, dx.reshape(x.shape), *[grads[k] for k in order], *[deltas[k] for k in order],
            *[new_m[k] for k in order], *[new_v[k] for k in order])
```

```python
import functools

import jax
import jax.numpy as jnp
from jax import lax
from jax.experimental import pallas as pl
from jax.experimental.pallas import tpu as pltpu

F32, BF16 = jnp.float32, jnp.bfloat16
SDS = jax.ShapeDtypeStruct
MESH = pl.DeviceIdType.MESH

D_MODEL = 1024
SEQ = 2048
DEPTH = 2
RMS_EPS = 1e-5
SSD_INNER = 2048
SSD_HEAD_DIM = 64
SSD_HEADS = 32
SSD_STATE = 128
SSD_GROUPS = 4
SSD_CONV = 4
SSD_CHUNK = 128
SSD_CONV_CH = 3072
ATTN_HEAD_DIM = 128
ATTN_KV_HEADS = 8
ATTN_DILATIONS = (1, 4, 16)
ATTN_N_PAT = 3
ATTN_BLOCK = 128
ATTN_OUT = 1024
ROPE_THETA = 500000.0
ROPE_DIM = 32
FFN_HIDDEN = 2816
ADAM_LR, ADAM_B1, ADAM_B2, ADAM_EPS, ADAM_WD, ADAM_STEP = 0.001, 0.9, 0.999, 1e-08, 0.01, 10

N_DEV = 8
LANES = 128
VMEM_LIMIT = 56 * 1024 * 1024
HPAD = 128
HIGHEST = lax.Precision.HIGHEST

IN_ROWS = (("w_z", 2048), ("w_xbc", 3072), ("w_dt", 32), ("w_q0", 1024), ("w_q1", 1024), ("w_q2", 1024),
           ("w_k", 1024), ("w_v", 1024), ("w_gs", 1024), ("w_ga", 1024))
N_IN = sum(r for _, r in IN_ROWS)
DT_ROW = 2048 + 3072
DT_SLOT = 1024


def _cparams(sem):
    return pltpu.CompilerParams(dimension_semantics=sem, vmem_limit_bytes=VMEM_LIMIT)


def _sigmoid(x):
    return 0.5 * jnp.tanh(0.5 * x) + 0.5


def _silu(x):
    return x * _sigmoid(x)


def _softplus(x):
    return jnp.maximum(x, 0.0) + jnp.log(1.0 + jnp.exp(-jnp.abs(x)))


def _dot(a, b, dims=(((1,), (0,)), ((), ())), precision=None):
    return lax.dot_general(a, b, dims, precision=precision, preferred_element_type=F32)


NT = (((1,), (1,)), ((), ()))
TN = (((0,), (0,)), ((), ()))


def _bdot(a, b, dims=(((1,), (0,)), ((), ()))):
    return _dot(a.astype(BF16), b.astype(BF16), dims)


def _pick(dim, cands):
    for c in cands:
        if dim % c == 0:
            return c
    return dim


def matmul(a, b, *, name, ta=False, tb=False, out_dtype=F32, add=None):
    m, k = (a.shape[1], a.shape[0]) if ta else a.shape
    n = b.shape[0] if tb else b.shape[1]
    tn = _pick(n, (1024, 1408, 512, 256, 128))
    short_rows = tn == n and (k <= 4 * 1408 or ta)
    tm = _pick(m, (512, 1408, 256, 128)) if short_rows else _pick(m, (1024, 1408, 512, 256, 128))
    tk = _pick(k, (1024, 1408, 512, 256, 128))
    nk = k // tk
    a_spec = pl.BlockSpec((tk, tm), lambda i, j, kk: (kk, i)) if ta else pl.BlockSpec((tm, tk), lambda i, j, kk: (i, kk))
    b_spec = pl.BlockSpec((tn, tk), lambda i, j, kk: (j, kk)) if tb else pl.BlockSpec((tk, tn), lambda i, j, kk: (kk, j))
    dims = (((0 if ta else 1,), (1 if tb else 0,)), ((), ()))
    has_add = add is not None

    def body(*refs):
        a_ref, b_ref = refs[:2]
        add_ref = refs[2] if has_add else None
        o_ref = refs[3] if has_add else refs[2]
        acc = refs[-1] if nk > 1 else None
        kk = pl.program_id(2)

        def product():
            return _dot(a_ref[...].astype(BF16), b_ref[...].astype(BF16), dims)

        def finish(r):
            if has_add:
                r = r + add_ref[...].astype(F32)
            o_ref[...] = r.astype(o_ref.dtype)

        if nk == 1:
            finish(product())
            return

        @pl.when(kk == 0)
        def _():
            acc[...] = product()

        @pl.when((kk > 0) & (kk < nk - 1))
        def _():
            acc[...] += product()

        @pl.when(kk == nk - 1)
        def _():
            finish(acc[...] + product())

    in_specs = [a_spec, b_spec]
    args = [a, b]
    if has_add:
        in_specs.append(pl.BlockSpec((tm, tn), lambda i, j, kk: (i, j)))
        args.append(add)
    return pl.pallas_call(
        body, name=name, grid=(m // tm, n // tn, nk),
        in_specs=in_specs, out_specs=pl.BlockSpec((tm, tn), lambda i, j, kk: (i, j)),
        out_shape=SDS((m, n), out_dtype), scratch_shapes=[pltpu.VMEM((tm, tn), F32)] if nk > 1 else [],
        compiler_params=_cparams(("parallel", "parallel", "arbitrary")),
    )(*args)


def rowcall(name, fn, rows, params, row_outs, red_outs=(), tr=256):
    s = rows[0].shape[0]
    n_in = len(rows) + len(params)
    n_row = len(row_outs)

    def body(*refs):
        outs = fn(*[r[...].astype(F32) for r in refs[:n_in]])
        if not isinstance(outs, (tuple, list)):
            outs = (outs,)
        orefs = refs[n_in:]
        for r, o in zip(orefs[:n_row], outs[:n_row]):
            r[...] = o.astype(r.dtype)
        if red_outs:
            @pl.when(pl.program_id(0) == 0)
            def _():
                for r in orefs[n_row:]:
                    r[...] = jnp.zeros_like(r)
            for r, o in zip(orefs[n_row:], outs[n_row:]):
                r[...] += o.astype(F32)

    widths = [a[1] if isinstance(a, tuple) else a.shape[1] for a in rows]
    rows = [a[0] if isinstance(a, tuple) else a for a in rows]
    in_specs = [pl.BlockSpec((tr, wd), lambda i: (i, 0)) for wd in widths]
    in_specs += [pl.BlockSpec(p.shape, lambda i: (0, 0)) for p in params]
    out_specs = [pl.BlockSpec((tr, c), lambda i: (i, 0)) for c, _ in row_outs]
    out_specs += [pl.BlockSpec(shp, lambda i: (0, 0)) for shp in red_outs]
    out_shape = [SDS((s, c), dt) for c, dt in row_outs] + [SDS(shp, F32) for shp in red_outs]
    res = pl.pallas_call(
        body, name=name, grid=(s // tr,), in_specs=in_specs, out_specs=out_specs, out_shape=out_shape,
        compiler_params=_cparams(("arbitrary",) if red_outs else ("parallel",)),
    )(*rows, *params)
    return res


def _rms(x, w):
    return x * lax.rsqrt(jnp.mean(x * x, axis=-1, keepdims=True) + RMS_EPS) * w


def rms_fwd(h, w, name):
    return rowcall(name, _rms, [h], [w], [(D_MODEL, BF16)])[0]


def rms_bwd(h, du, dres, w, name):
    def fn(hb, dub, dresb, wb):
        _, vjp = jax.vjp(_rms, hb, wb)
        dh, dw = vjp(dub)
        return dh + dresb, dw
    return rowcall(name, fn, [h, du, dres], [w], [(D_MODEL, F32)], [(1, D_MODEL)])


def loss_head(h, target, w, name):
    def fn(hb, tb, wb):
        def f(hh, ww):
            err = _rms(hh, ww) - tb
            return 0.5 * jnp.sum(jnp.mean(err * err, axis=-1, keepdims=True), axis=0, keepdims=True)
        val, vjp = jax.vjp(f, hb, wb)
        dh, dw = vjp(jnp.ones((1, 1), F32))
        return dh, dw, jnp.broadcast_to(val, (1, LANES))
    return rowcall(name, fn, [h, target], [w], [(D_MODEL, F32)], [(1, D_MODEL), (1, LANES)])


def _gate(a, b, gs, ga):
    return _sigmoid(gs) * a + _sigmoid(ga) * b


def gate_fwd(a, b, gs, ga, name):
    return rowcall(name, _gate, [a, b, gs, ga], [], [(D_MODEL, BF16)])[0]


def gate_bwd(a, b, gs, ga, dm, name):
    def fn(ab, bb, gsb, gab, dmb):
        _, vjp = jax.vjp(_gate, ab, bb, gsb, gab)
        return vjp(dmb)
    return rowcall(name, fn, [a, b, gs, ga, dm], [], [(D_MODEL, BF16)] * 4)


def _swiglu(gu):
    return _silu(gu[:, :FFN_HIDDEN]) * gu[:, FFN_HIDDEN:]


def swiglu_fwd(gu, name):
    return rowcall(name, _swiglu, [gu], [], [(FFN_HIDDEN, BF16)])[0]


def swiglu_bwd(gu, dact, name):
    def fn(gub, db):
        _, vjp = jax.vjp(_swiglu, gub)
        return vjp(db.astype(F32))[0]
    return rowcall(name, fn, [gu, dact], [], [(2 * FFN_HIDDEN, BF16)])[0]


def _ssd_post(y, xs, z, dskip, normw):
    y = (y + dskip * xs) * _silu(z)
    gw = SSD_INNER // SSD_GROUPS
    parts = []
    for g in range(SSD_GROUPS):
        yg = y[:, g * gw:(g + 1) * gw]
        parts.append(yg * lax.rsqrt(jnp.mean(yg * yg, axis=-1, keepdims=True) + RMS_EPS))
    return jnp.concatenate(parts, axis=-1) * normw


def ssd_post_fwd(y, xc, z, dskip, normw, name):
    return rowcall(name, _ssd_post, [y, (xc, SSD_INNER), z], [dskip, normw], [(SSD_INNER, BF16)])[0]


def ssd_post_bwd(y, xc, z, dskip, normw, dyn, name):
    def fn(yb, xsb, zb, dynb, db, nb):
        _, vjp = jax.vjp(_ssd_post, yb, xsb, zb, db, nb)
        return vjp(dynb)
    return rowcall(name, fn, [y, (xc, SSD_INNER), z, dyn], [dskip, normw],
                   [(SSD_INNER, F32), (SSD_INNER, F32), (SSD_INNER, BF16)], [(1, SSD_INNER), (1, SSD_INNER)])


def _rope(t, cosf, sina, sinb):
    return t * cosf + pltpu.roll(t, LANES - ROPE_DIM // 2, 1) * sina + pltpu.roll(t, ROPE_DIM // 2, 1) * sinb


def rope_tables():
    half = ROPE_DIM // 2
    inv = ROPE_THETA ** (-jnp.arange(0, ROPE_DIM, 2, dtype=F32) / ROPE_DIM)
    ang = jnp.arange(SEQ, dtype=F32)[:, None] * inv[None, :]
    cos, sin = jnp.cos(ang), jnp.sin(ang)
    zeros = jnp.zeros((SEQ, LANES - ROPE_DIM), F32)
    z16 = jnp.zeros((SEQ, half), F32)
    cosf = jnp.concatenate([cos, cos, jnp.ones((SEQ, LANES - ROPE_DIM), F32)], axis=1)
    sina = jnp.concatenate([-sin, z16, zeros], axis=1)
    sinb = jnp.concatenate([z16, sin, zeros], axis=1)
    return cosf, sina, sinb


CONV_TC = 256


def _conv_pre(x, w, b, row):
    acc = x * w[SSD_CONV - 1:SSD_CONV, :] + b
    shifted = [x]
    for j in range(1, SSD_CONV):
        xs = jnp.where(row >= j, pltpu.roll(x, j, 0), 0.0)
        shifted.append(xs)
        acc = acc + xs * w[SSD_CONV - 1 - j:SSD_CONV - j, :]
    return acc, shifted


def conv_fwd(xbc, w, b, name):
    def body(x_ref, w_ref, b_ref, o_ref):
        row = lax.broadcasted_iota(jnp.int32, (SEQ, CONV_TC), 0)
        pre, _ = _conv_pre(x_ref[...].astype(F32), w_ref[...], b_ref[...], row)
        o_ref[...] = _silu(pre)
    return pl.pallas_call(
        body, name=name, grid=(SSD_CONV_CH // CONV_TC,),
        in_specs=[pl.BlockSpec((SEQ, CONV_TC), lambda i: (0, i)), pl.BlockSpec((SSD_CONV, CONV_TC), lambda i: (0, i)),
                  pl.BlockSpec((1, CONV_TC), lambda i: (0, i))],
        out_specs=pl.BlockSpec((SEQ, CONV_TC), lambda i: (0, i)),
        out_shape=SDS((SEQ, SSD_CONV_CH), F32), compiler_params=_cparams(("parallel",)),
    )(xbc, w, b)


def conv_bwd(xbc, w, b, dxc, name):
    def body(x_ref, w_ref, b_ref, dy_ref, dx_ref, dw_ref, db_ref):
        row = lax.broadcasted_iota(jnp.int32, (SEQ, CONV_TC), 0)
        wv = w_ref[...]
        pre, shifted = _conv_pre(x_ref[...].astype(F32), wv, b_ref[...], row)
        sg = _sigmoid(pre)
        ds = dy_ref[...] * (sg * (1.0 + pre * (1.0 - sg)))
        dx = ds * wv[SSD_CONV - 1:SSD_CONV, :]
        for j in range(1, SSD_CONV):
            dsj = jnp.where(row < SEQ - j, pltpu.roll(ds, SEQ - j, 0), 0.0)
            dx = dx + dsj * wv[SSD_CONV - 1 - j:SSD_CONV - j, :]
        dx_ref[...] = dx.astype(dx_ref.dtype)
        for j in range(SSD_CONV):
            dw_ref[SSD_CONV - 1 - j:SSD_CONV - j, :] = jnp.sum(ds * shifted[j], axis=0, keepdims=True)
        db_ref[...] = jnp.sum(ds, axis=0, keepdims=True)
    return pl.pallas_call(
        body, name=name, grid=(SSD_CONV_CH // CONV_TC,),
        in_specs=[pl.BlockSpec((SEQ, CONV_TC), lambda i: (0, i)), pl.BlockSpec((SSD_CONV, CONV_TC), lambda i: (0, i)),
                  pl.BlockSpec((1, CONV_TC), lambda i: (0, i)), pl.BlockSpec((SEQ, CONV_TC), lambda i: (0, i))],
        out_specs=[pl.BlockSpec((SEQ, CONV_TC), lambda i: (0, i)), pl.BlockSpec((SSD_CONV, CONV_TC), lambda i: (0, i)),
                   pl.BlockSpec((1, CONV_TC), lambda i: (0, i))],
        out_shape=[SDS((SEQ, SSD_CONV_CH), BF16), SDS((SSD_CONV, SSD_CONV_CH), F32), SDS((1, SSD_CONV_CH), F32)],
        compiler_params=_cparams(("parallel",)),
    )(xbc, w, b, dxc)


N_CHUNKS = SEQ // SSD_CHUNK
N_PAIRS = SSD_HEADS // 2
PAIRS_PER_GROUP = N_PAIRS // SSD_GROUPS
B_OFF = SSD_INNER
C_OFF = SSD_INNER + SSD_GROUPS * SSD_STATE


def _ssd_prefix(dtr, dtr_t, dtb, dtb_t, alog, alog_t):
    ln = SSD_CHUNK
    dt = _softplus(dtr + dtb)
    dt_t = _softplus(dtr_t + dtb_t)
    dta = dt * (-jnp.exp(alog))
    dta_t = dt_t * (-jnp.exp(alog_t))
    r = lax.broadcasted_iota(jnp.int32, (ln, ln), 0)
    c = lax.broadcasted_iota(jnp.int32, (ln, ln), 1)
    a_cum = _dot((r >= c).astype(F32), dta, precision=HIGHEST)
    a_cum_t = _dot(dta_t, (r <= c).astype(F32), precision=HIGHEST)
    a_last = jnp.sum(dta_t, axis=1, keepdims=True)
    return dt, a_cum, a_cum_t, a_last


def _ssd_pair(x_pair, bg, cg, hp, dt, a_cum, a_cum_t, a_last, *, e0):
    ln = SSD_CHUNK
    lane = lax.broadcasted_iota(jnp.int32, (ln, LANES), 1)
    sub = lax.broadcasted_iota(jnp.int32, (LANES, SSD_STATE), 0)
    row = lax.broadcasted_iota(jnp.int32, (ln, ln), 0)
    col = lax.broadcasted_iota(jnp.int32, (ln, ln), 1)
    lo = lane < SSD_HEAD_DIM
    e1 = e0 + 1
    c0, c1 = a_cum[:, e0:e0 + 1], a_cum[:, e1:e1 + 1]
    r0, r1 = a_cum_t[e0:e0 + 1, :], a_cum_t[e1:e1 + 1, :]
    l0, l1 = a_last[e0:e0 + 1, :], a_last[e1:e1 + 1, :]
    xd = x_pair * jnp.where(lo, dt[:, e0:e0 + 1], dt[:, e1:e1 + 1])
    causal = row >= col
    cb = _bdot(cg, bg, NT)
    m0 = cb * jnp.exp(jnp.where(causal, c0 - r0, -jnp.inf))
    m1 = cb * jnp.exp(jnp.where(causal, c1 - r1, -jnp.inf))
    y = _bdot(m0, jnp.where(lo, xd, 0.0)) + _bdot(m1, jnp.where(lo, 0.0, xd))
    acum_pair = jnp.where(lo, c0, c1)
    y = y + _bdot(cg, hp, NT) * jnp.exp(acum_pair)
    last_pair = jnp.where(lo, l0, l1)
    st = _bdot(xd * jnp.exp(last_pair - acum_pair), bg, TN)
    h_out = hp * jnp.exp(jnp.where(sub < SSD_HEAD_DIM, l0, l1)) + st
    return y, h_out


def _ssd_in_specs(chunk_of):
    return [
        pl.BlockSpec((SSD_CHUNK, SSD_CONV_CH), lambda i: (chunk_of(i), 0)),
        pl.BlockSpec((SSD_CHUNK, HPAD), lambda i: (chunk_of(i), 0)),
        pl.BlockSpec((HPAD, SSD_CHUNK), lambda i: (0, chunk_of(i))),
        pl.BlockSpec((1, HPAD), lambda i: (0, 0)), pl.BlockSpec((HPAD, 1), lambda i: (0, 0)),
        pl.BlockSpec((1, HPAD), lambda i: (0, 0)), pl.BlockSpec((HPAD, 1), lambda i: (0, 0)),
    ]


def ssd_fwd(xc, dtr, dtr_t, dtb, dtb_t, alog, alog_t, name):
    def body(xc_ref, dtr_ref, dtrt_ref, dtb_ref, dtbt_ref, al_ref, alt_ref, y_ref, hs_ref, h_scr):
        @pl.when(pl.program_id(0) == 0)
        def _():
            h_scr[...] = jnp.zeros_like(h_scr)

        hs_ref[0] = h_scr[...]
        dt, a_cum, a_cum_t, a_last = _ssd_prefix(dtr_ref[...], dtrt_ref[...], dtb_ref[...], dtbt_ref[...],
                                                  al_ref[...], alt_ref[...])
        for pr in range(N_PAIRS):
            g = pr // PAIRS_PER_GROUP
            sl = slice(pr * LANES, (pr + 1) * LANES)
            bg = xc_ref[:, B_OFF + g * SSD_STATE:B_OFF + (g + 1) * SSD_STATE]
            cg = xc_ref[:, C_OFF + g * SSD_STATE:C_OFF + (g + 1) * SSD_STATE]
            y, h_out = _ssd_pair(xc_ref[:, sl], bg, cg, h_scr[sl, :], dt, a_cum, a_cum_t, a_last, e0=2 * pr)
            y_ref[:, sl] = y
            h_scr[sl, :] = h_out

    return pl.pallas_call(
        body, name=name, grid=(N_CHUNKS,), in_specs=_ssd_in_specs(lambda i: i),
        out_specs=[pl.BlockSpec((SSD_CHUNK, SSD_INNER), lambda i: (i, 0)),
                   pl.BlockSpec((1, SSD_INNER, SSD_STATE), lambda i: (i, 0, 0))],
        out_shape=[SDS((SEQ, SSD_INNER), F32), SDS((N_CHUNKS, SSD_INNER, SSD_STATE), F32)],
        scratch_shapes=[pltpu.VMEM((SSD_INNER, SSD_STATE), F32)],
        compiler_params=_cparams(("arbitrary",)),
    )(xc, dtr, dtr_t, dtb, dtb_t, alog, alog_t)


def ssd_bwd(xc, dtr, dtr_t, dtb, dtb_t, alog, alog_t, hs, dy, dxs_extra, name):
    rev = lambda i: N_CHUNKS - 1 - i

    def body(xc_ref, dtr_ref, dtrt_ref, dtb_ref, dtbt_ref, al_ref, alt_ref, hs_ref, dy_ref, dxe_ref,
             dxc_ref, ddtr_ref, ddtrt_ref, ddtb_ref, ddtbt_ref, dal_ref, dalt_ref, dh_scr):
        @pl.when(pl.program_id(0) == 0)
        def _():
            dh_scr[...] = jnp.zeros_like(dh_scr)
            for r in (ddtb_ref, ddtbt_ref, dal_ref, dalt_ref):
                r[...] = jnp.zeros_like(r)

        prefix_in = (dtr_ref[...], dtrt_ref[...], dtb_ref[...], dtbt_ref[...], al_ref[...], alt_ref[...])
        (dt, a_cum, a_cum_t, a_last), prefix_vjp = jax.vjp(_ssd_prefix, *prefix_in)
        d_dt = jnp.zeros_like(dt)
        d_acum = jnp.zeros_like(a_cum)
        d_acum_t = jnp.zeros_like(a_cum_t)
        d_alast = jnp.zeros_like(a_last)
        for g in range(SSD_GROUPS):
            bg = xc_ref[:, B_OFF + g * SSD_STATE:B_OFF + (g + 1) * SSD_STATE]
            cg = xc_ref[:, C_OFF + g * SSD_STATE:C_OFF + (g + 1) * SSD_STATE]
            d_bg = jnp.zeros_like(bg)
            d_cg = jnp.zeros_like(cg)
            for j in range(PAIRS_PER_GROUP):
                pr = g * PAIRS_PER_GROUP + j
                sl = slice(pr * LANES, (pr + 1) * LANES)
                _, vjp = jax.vjp(functools.partial(_ssd_pair, e0=2 * pr),
                                 xc_ref[:, sl], bg, cg, hs_ref[0, sl, :], dt, a_cum, a_cum_t, a_last)
                dx, dbg, dcg, dhp, ddt, dac, dact, dal = vjp((dy_ref[:, sl], dh_scr[sl, :]))
                dxc_ref[:, sl] = dx + dxe_ref[:, sl]
                dh_scr[sl, :] = dhp
                d_bg, d_cg = d_bg + dbg, d_cg + dcg
                d_dt, d_acum, d_acum_t, d_alast = d_dt + ddt, d_acum + dac, d_acum_t + dact, d_alast + dal
            dxc_ref[:, B_OFF + g * SSD_STATE:B_OFF + (g + 1) * SSD_STATE] = d_bg
            dxc_ref[:, C_OFF + g * SSD_STATE:C_OFF + (g + 1) * SSD_STATE] = d_cg
        g_dtr, g_dtrt, g_dtb, g_dtbt, g_al, g_alt = prefix_vjp((d_dt, d_acum, d_acum_t, d_alast))
        ddtr_ref[...] = g_dtr
        ddtrt_ref[...] = g_dtrt
        ddtb_ref[...] += g_dtb
        ddtbt_ref[...] += g_dtbt
        dal_ref[...] += g_al
        dalt_ref[...] += g_alt

    in_specs = _ssd_in_specs(rev) + [
        pl.BlockSpec((1, SSD_INNER, SSD_STATE), lambda i: (rev(i), 0, 0)),
        pl.BlockSpec((SSD_CHUNK, SSD_INNER), lambda i: (rev(i), 0)),
        pl.BlockSpec((SSD_CHUNK, SSD_INNER), lambda i: (rev(i), 0)),
    ]
    out_specs = [
        pl.BlockSpec((SSD_CHUNK, SSD_CONV_CH), lambda i: (rev(i), 0)),
        pl.BlockSpec((SSD_CHUNK, HPAD), lambda i: (rev(i), 0)),
        pl.BlockSpec((HPAD, SSD_CHUNK), lambda i: (0, rev(i))),
        pl.BlockSpec((1, HPAD), lambda i: (0, 0)), pl.BlockSpec((HPAD, 1), lambda i: (0, 0)),
        pl.BlockSpec((1, HPAD), lambda i: (0, 0)), pl.BlockSpec((HPAD, 1), lambda i: (0, 0)),
    ]
    out_shape = [SDS((SEQ, SSD_CONV_CH), F32), SDS((SEQ, HPAD), F32), SDS((HPAD, SEQ), F32),
                 SDS((1, HPAD), F32), SDS((HPAD, 1), F32), SDS((1, HPAD), F32), SDS((HPAD, 1), F32)]
    return pl.pallas_call(
        body, name=name, grid=(N_CHUNKS,), in_specs=in_specs, out_specs=out_specs, out_shape=out_shape,
        scratch_shapes=[pltpu.VMEM((SSD_INNER, SSD_STATE), F32)],
        compiler_params=_cparams(("arbitrary",)),
    )(xc, dtr, dtr_t, dtb, dtb_t, alog, alog_t, hs, dy, dxs_extra)


ATTN_SCALE = ATTN_HEAD_DIM ** -0.5


def _attn_scores(q, k_blk, prev):
    qi = lax.broadcasted_iota(jnp.int32, (ATTN_BLOCK, ATTN_BLOCK), 0)
    kj = lax.broadcasted_iota(jnp.int32, (ATTN_BLOCK, ATTN_BLOCK), 1)
    keep = (kj >= qi) if prev else (qi >= kj)
    return jnp.where(keep, _bdot(q, k_blk, NT) * ATTN_SCALE, -jnp.inf)


UNITS_PER_PATTERN = SEQ // ATTN_BLOCK


def _for_units(unit):
    for g, d in enumerate(ATTN_DILATIONS):
        nb = UNITS_PER_PATTERN // d
        span = d * ATTN_BLOCK

        def residue(r, carry, g=g, d=d, nb=nb, span=span):
            unit(g, pl.ds(r, ATTN_BLOCK, stride=d), None)

            def later(n, c2):
                start = r + n * span
                unit(g, pl.ds(start, ATTN_BLOCK, stride=d), pl.ds(start - span, ATTN_BLOCK, stride=d))
                return c2
            if nb > 1:
                lax.fori_loop(1, nb, later, 0, unroll=3)
            return carry
        if d == 1:
            residue(0, 0)
        else:
            lax.fori_loop(0, d, residue, 0, unroll=2 if nb == 1 else 1)


def _head_specs(n_q_groups):
    blk = (SEQ, ATTN_HEAD_DIM)
    q_specs = [pl.BlockSpec(blk, functools.partial(lambda h, g: (0, g * ATTN_KV_HEADS + h), g=g)) for g in range(n_q_groups)]
    head = pl.BlockSpec(blk, lambda h: (0, h))
    table = pl.BlockSpec(blk, lambda h: (0, 0))
    return q_specs, head, table


def attn_fwd(q, k, v, tabs, name):
    q_specs, head, table = _head_specs(ATTN_N_PAT)

    def body(q0_ref, q1_ref, q2_ref, k_ref, v_ref, c_ref, sa_ref, sb_ref, y_ref, lse_ref, *scr):
        qs, og, ls, ks, vs = scr[0:3], scr[3:6], scr[6:9], scr[9], scr[10]
        c, sa, sb = c_ref[...], sa_ref[...], sb_ref[...]
        for g, q_ref in enumerate((q0_ref, q1_ref, q2_ref)):
            qs[g][...] = _rope(q_ref[...].astype(F32), c, sa, sb)
        ks[...] = _rope(k_ref[...].astype(F32), c, sa, sb)
        vs[...] = v_ref[...].astype(F32)

        def unit(g, rows, prows):
            q_u = qs[g][rows, :]
            s_c = _attn_scores(q_u, ks[rows, :], False)
            m = jnp.max(s_c, axis=1, keepdims=True)
            if prows is not None:
                s_p = _attn_scores(q_u, ks[prows, :], True)
                m = jnp.maximum(m, jnp.max(s_p, axis=1, keepdims=True))
            p_c = jnp.exp(s_c - m)
            l = jnp.sum(p_c, axis=1, keepdims=True)
            o = _bdot(p_c, vs[rows, :])
            if prows is not None:
                p_p = jnp.exp(s_p - m)
                l = l + jnp.sum(p_p, axis=1, keepdims=True)
                o = o + _bdot(p_p, vs[prows, :])
            og[g][rows, :] = o / l
            ls[g][rows, :] = jnp.broadcast_to(m + jnp.log(l), (ATTN_BLOCK, LANES))

        _for_units(unit)
        l0, l1, l2 = ls[0][...], ls[1][...], ls[2][...]
        m = jnp.maximum(jnp.maximum(l0, l1), l2)
        e0, e1, e2 = jnp.exp(l0 - m), jnp.exp(l1 - m), jnp.exp(l2 - m)
        den = e0 + e1 + e2
        y_ref[...] = ((e0 * og[0][...] + e1 * og[1][...] + e2 * og[2][...]) / den).astype(y_ref.dtype)
        lse_ref[...] = m + jnp.log(den)

    blk = (SEQ, ATTN_HEAD_DIM)
    return pl.pallas_call(
        body, name=name, grid=(ATTN_KV_HEADS,), in_specs=[*q_specs, head, head, table, table, table],
        out_specs=[head, head], out_shape=[SDS((SEQ, ATTN_OUT), BF16), SDS((SEQ, ATTN_OUT), F32)],
        scratch_shapes=[pltpu.VMEM(blk, F32)] * (3 * ATTN_N_PAT + 2),
        compiler_params=_cparams(("parallel",)),
    )(q, q, q, k, v, *tabs)


def attn_bwd(q, k, v, tabs, y, lse, dy, name):
    q_specs, head, table = _head_specs(ATTN_N_PAT)

    def body(q0_ref, q1_ref, q2_ref, k_ref, v_ref, c_ref, sa_ref, sb_ref, y_ref, lse_ref, dy_ref,
             dq0_ref, dq1_ref, dq2_ref, dk_ref, dv_ref, *scr):
        qs, dqs, ks, dks, dd, dvs, vs = scr[0:3], scr[3:6], scr[6], scr[7], scr[8], scr[9], scr[10]
        c, sa, sb = c_ref[...], sa_ref[...], sb_ref[...]
        for g, q_ref in enumerate((q0_ref, q1_ref, q2_ref)):
            qs[g][...] = _rope(q_ref[...].astype(F32), c, sa, sb)
        ks[...] = _rope(k_ref[...].astype(F32), c, sa, sb)
        vs[...] = v_ref[...].astype(F32)
        dks[...] = jnp.zeros_like(dks)
        dvs[...] = jnp.zeros_like(dvs)
        dyv = dy_ref[...]
        dd[...] = jnp.broadcast_to(jnp.sum(dyv * y_ref[...].astype(F32), axis=1, keepdims=True), dd.shape)

        def unit(g, rows, prows):
            qv = qs[g][rows, :].astype(BF16)
            do = dy_ref[rows, :].astype(BF16)
            lse_u = lse_ref[rows, :][:, 0:1]
            dsum = dd[rows, :][:, 0:1]
            dq = None
            for krows, prev in ((rows, False), (prows, True)):
                if krows is None:
                    continue
                kb, vb = ks[krows, :].astype(BF16), vs[krows, :].astype(BF16)
                p = jnp.exp(_attn_scores(qv, kb, prev) - lse_u)
                ds = (p * (_dot(do, vb, NT) - dsum) * ATTN_SCALE).astype(BF16)
                dq = _dot(ds, kb) if dq is None else dq + _dot(ds, kb)
                dks[krows, :] += _dot(ds, qv, TN)
                dvs[krows, :] += _bdot(p, do, TN)
            dqs[g][rows, :] = dq

        _for_units(unit)
        for g, dq_ref in enumerate((dq0_ref, dq1_ref, dq2_ref)):
            dq_ref[...] = _rope(dqs[g][...], c, -sa, -sb).astype(dq_ref.dtype)
        dk_ref[...] = _rope(dks[...], c, -sa, -sb).astype(dk_ref.dtype)
        dv_ref[...] = dvs[...].astype(dv_ref.dtype)

    blk = (SEQ, ATTN_HEAD_DIM)
    out = SDS((SEQ, ATTN_OUT), BF16)
    return pl.pallas_call(
        body, name=name, grid=(ATTN_KV_HEADS,), in_specs=[*q_specs, head, head, table, table, table, head, head, head],
        out_specs=[head] * 5, out_shape=[out] * 5,
        scratch_shapes=[pltpu.VMEM(blk, F32)] * (2 * ATTN_N_PAT + 5),
        compiler_params=_cparams(("parallel",)),
    )(q, q, q, k, v, *tabs, y, lse, dy)


def layer_fwd(h, getw, small, tabs, li):
    n = f"l{li}_"
    sv = {}
    w = dict(getw(0, h))
    u = rms_fwd(h, small["norm_mix"], n + "rms_mix")
    z = matmul(u, w["w_z"], name=n + "mm_z", tb=True, out_dtype=BF16)
    xbc = matmul(u, w["w_xbc"], name=n + "mm_xbc", tb=True, out_dtype=BF16)
    dtr = matmul(u, w["w_dt"], name=n + "mm_dt", tb=True)
    q = matmul(u, w["w_q"], name=n + "mm_q", tb=True, out_dtype=BF16)
    k = matmul(u, w["w_k"], name=n + "mm_k", tb=True, out_dtype=BF16)
    v = matmul(u, w["w_v"], name=n + "mm_v", tb=True, out_dtype=BF16)
    gs = matmul(u, w["w_gs"], name=n + "mm_gs", tb=True, out_dtype=BF16)
    ga = matmul(u, w["w_ga"], name=n + "mm_ga", tb=True, out_dtype=BF16)
    xc = conv_fwd(xbc, w["conv_w"], small["conv_b"], n + "conv")
    dtr_t = dtr.T
    y_ssd, hs = ssd_fwd(xc, dtr, dtr_t, small["dt_bias"], small["dt_bias"].T, small["a_log"], small["a_log"].T, n + "ssd")
    yn = ssd_post_fwd(y_ssd, xc, z, small["d_skip_x"], small["ssd_norm"], n + "ssd_post")
    y_attn, lse = attn_fwd(q, k, v, tabs, n + "attn")
    w.update(getw(1, y_ssd))
    a = matmul(yn, w["w_ssd_branch"], name=n + "mm_a", out_dtype=BF16)
    b = matmul(y_attn, w["w_attn_branch"], name=n + "mm_b", out_dtype=BF16)
    merged = gate_fwd(a, b, gs, ga, n + "gate")
    h1 = matmul(merged, w["w_out"], name=n + "mm_o", add=h)
    w.update(getw(2, h1))
    u2 = rms_fwd(h1, small["norm_ffn"], n + "rms_ffn")
    gu = matmul(u2, w["w_gate_up"], name=n + "mm_gu", tb=True, out_dtype=BF16)
    act = swiglu_fwd(gu, n + "swiglu")
    h2 = matmul(act, w["w_down"], name=n + "mm_down", add=h1)
    sv.update(h=h, u=u, z=z, xbc=xbc, dtr=dtr, dtr_t=dtr_t, gs=gs, ga=ga, xc=xc, y_ssd=y_ssd, hs=hs, yn=yn,
              q=q, k=k, v=v, y_attn=y_attn, lse=lse, a=a, b=b, merged=merged, h1=h1, u2=u2, gu=gu, act=act, w=w)
    return h2, sv


def layer_bwd(dh, sv, small, tabs, li, emit):
    n = f"l{li}_b_"
    w = sv["w"]
    gw, gsm = {}, {}
    dact = matmul(dh, w["w_down"], name=n + "mm_dact", tb=True, out_dtype=BF16)
    gw["w_down"] = matmul(sv["act"], dh, name=n + "mm_dwdown", ta=True, out_dtype=BF16)
    dgu = swiglu_bwd(sv["gu"], dact, n + "swiglu")
    gw["w_gate_up"] = matmul(dgu, sv["u2"], name=n + "mm_dwgu", ta=True, out_dtype=BF16)
    tok = emit(2, gw)
    du2 = matmul(dgu, w["w_gate_up"], name=n + "mm_du2")
    dh1, gsm["norm_ffn"] = rms_bwd(sv["h1"], du2, dh, small["norm_ffn"] + tok, n + "rms_ffn")
    dmerged = matmul(dh1, w["w_out"], name=n + "mm_dmerged", tb=True)
    gw["w_out"] = matmul(sv["merged"], dh1, name=n + "mm_dwo", ta=True, out_dtype=BF16)
    da, db, dgs, dga = gate_bwd(sv["a"], sv["b"], sv["gs"], sv["ga"], dmerged, n + "gate")
    gw["w_ssd_branch"] = matmul(sv["yn"], da, name=n + "mm_dwa", ta=True, out_dtype=BF16)
    gw["w_attn_branch"] = matmul(sv["y_attn"], db, name=n + "mm_dwb", ta=True, out_dtype=BF16)
    tok = emit(1, gw)
    dyn = matmul(da, w["w_ssd_branch"], name=n + "mm_dyn", tb=True)
    dyattn = matmul(db, w["w_attn_branch"], name=n + "mm_dyattn", tb=True)
    dy_ssd, dxs_extra, dz, gsm["d_skip_x"], gsm["ssd_norm"] = ssd_post_bwd(
        sv["y_ssd"], sv["xc"], sv["z"], small["d_skip_x"] + tok, small["ssd_norm"], dyn, n + "ssd_post")
    dxc, ddtr, ddtr_t, ddtb, ddtb_t, dal, dal_t = ssd_bwd(
        sv["xc"], sv["dtr"], sv["dtr_t"], small["dt_bias"], small["dt_bias"].T, small["a_log"], small["a_log"].T,
        sv["hs"], dy_ssd, dxs_extra, n + "ssd")
    ddtr = (ddtr + ddtr_t.T).astype(BF16)
    gsm["dt_bias"] = ddtb + ddtb_t.T
    gsm["a_log"] = dal + dal_t.T
    dxbc, gw["conv_w"], gsm["conv_b"] = conv_bwd(sv["xbc"], w["conv_w"], small["conv_b"], dxc, n + "conv")
    dq0, dq1, dq2, dk, dv = attn_bwd(sv["q"], sv["k"], sv["v"], tabs, sv["y_attn"], sv["lse"], dyattn, n + "attn")
    ddt_wide = jnp.pad(ddtr, ((0, 0), (0, DT_SLOT - HPAD)))
    dall = jnp.concatenate([dz, dxbc, ddt_wide, dq0, dq1, dq2, dk, dv, dgs, dga], axis=1)
    gall = matmul(dall, sv["u"], name=n + "mm_dw_in", ta=True, out_dtype=BF16)
    gw["w_in"] = jnp.concatenate([gall[:DT_ROW + SSD_HEADS], gall[DT_ROW + DT_SLOT:]], axis=0)
    tok = emit(0, gw)
    zero = jnp.zeros((SEQ, D_MODEL), F32) + tok
    du = matmul(dall, w["w_all"], name=n + "mm_du", add=zero)
    dh0, gsm["norm_mix"] = rms_bwd(sv["h"], du, dh1, small["norm_mix"] + tok, n + "rms_mix")
    return dh0, gsm


def _my_place():
    return lax.axis_index("x"), lax.axis_index("y"), lax.axis_index("c")


def _flip(place, k):
    x, y, c = place
    return (1 - x if k & 4 else x, 1 - y if k & 2 else y, 1 - c if k & 1 else c)


def _index(place):
    return 4 * place[0] + 2 * place[1] + place[2]


ANY = pl.BlockSpec(memory_space=pl.ANY)
CHIP_FLIPS = (4, 2, 6)


def all_gather(xs, name):
    na = len(xs)

    def body(*refs):
        x_refs, o_refs = refs[:na], refs[na:2 * na]
        send_sems, recv_sems, local_sems = refs[2 * na:]
        me = _my_place()
        sibling = _flip(me, 1)
        chips = [_flip(me, f) for f in CHIP_FLIPS]

        def copy(a, kk, block, to, src=None):
            dst = o_refs[a].at[_index(block)]
            return pltpu.make_async_remote_copy(
                src_ref=dst if src is None else src, dst_ref=dst, send_sem=send_sems.at[a, kk],
                recv_sem=recv_sems.at[a, kk], device_id=to, device_id_type=MESH)

        mine = [pltpu.make_async_copy(x_refs[a], o_refs[a].at[_index(me)], local_sems.at[a]) for a in range(na)]
        for cp in mine:
            cp.start()
        first = []
        for j, chip in enumerate(chips):
            first += [copy(a, 1 + j, me, chip, src=x_refs[a]) for a in range(na)]
        first += [copy(a, 0, me, sibling, src=x_refs[a]) for a in range(na)]
        for cp in first:
            cp.start()
        passed = []
        for j, chip in enumerate(chips):
            for a in range(na):
                copy(a, 1 + j, chip, me).wait_recv()
                cp = copy(a, 4 + j, chip, sibling)
                cp.start()
                passed.append(cp)
        for a in range(na):
            copy(a, 0, sibling, me).wait_recv()
        for j, chip in enumerate(chips):
            for a in range(na):
                copy(a, 4 + j, _flip(chip, 1), me).wait_recv()
        for cp in first + passed:
            cp.wait_send()
        for cp in mine:
            cp.wait()

    return pl.pallas_call(
        body, name=name, in_specs=[ANY] * na, out_specs=[ANY] * na,
        out_shape=[SDS((N_DEV,) + t.shape, t.dtype) for t in xs],
        scratch_shapes=[pltpu.SemaphoreType.DMA((na, N_DEV - 1)), pltpu.SemaphoreType.DMA((na, N_DEV - 1)),
                        pltpu.SemaphoreType.DMA((na,))],
    )(*xs)


HBM = pl.BlockSpec(memory_space=pltpu.HBM)
SEM = pl.BlockSpec(memory_space=pltpu.SEMAPHORE)
EFFECT = pltpu.SideEffectType.DATAFLOW_SIDE_EFFECTING
N_PEERS = N_DEV - 1


def _split_copy(src_ref, land_ref, send_sem, recv_sem, me, kk, scatter, landed_from_peer):
    peer = _flip(me, kk)
    src = src_ref.at[_index(peer)] if scatter else src_ref
    dst = land_ref.at[_index(peer if landed_from_peer else me)]
    return pltpu.make_async_remote_copy(src_ref=src, dst_ref=dst, send_sem=send_sem, recv_sem=recv_sem,
                                        device_id=peer, device_id_type=MESH)


def exchange_start(srcs, lands, group_sizes, scatter, name):
    na, ng = len(srcs), len(group_sizes)

    def body(*refs):
        s_refs, l_refs = refs[:na], refs[na:2 * na]
        sems = refs[2 * na:2 * na + 2 * ng]
        token = refs[-1]
        me = _my_place()
        a = 0
        for gi, gsz in enumerate(group_sizes):
            for j in range(gsz):
                for kk in range(1, N_DEV):
                    slot = j * N_PEERS + kk - 1
                    _split_copy(s_refs[a], l_refs[a], sems[2 * gi].at[slot], sems[2 * gi + 1].at[slot],
                                me, kk, scatter, False).start()
                a += 1
        token[...] = jnp.zeros_like(token)

    sem_shapes = []
    for gsz in group_sizes:
        sem_shapes += [pltpu.SemaphoreType.DMA((gsz * N_PEERS,))] * 2
    ins = [pltpu.with_memory_space_constraint(t, pltpu.HBM) for t in (*srcs, *lands)]
    res = pl.pallas_call(
        body, name=name, in_specs=[HBM] * (2 * na),
        out_specs=[SEM] * (2 * ng) + [HBM] * (2 * na) + [pl.BlockSpec(memory_space=pltpu.VMEM)],
        out_shape=sem_shapes + [pltpu.HBM(t.shape, t.dtype) for t in ins] + [SDS((8, LANES), F32)],
        input_output_aliases={i: 2 * ng + i for i in range(2 * na)},
        compiler_params=pltpu.CompilerParams(has_side_effects=EFFECT),
    )(*ins)
    sems = [(res[2 * gi], res[2 * gi + 1]) for gi in range(ng)]
    thru = res[2 * ng:2 * ng + 2 * na]
    return sems, thru[:na], thru[na:], res[-1]


def exchange_wait(srcs, lands, sems, after, scatter, name):
    n = len(srcs)

    def body(*refs):
        s_refs, l_refs = refs[:n], refs[n:2 * n]
        send_sems, recv_sems = refs[2 * n], refs[2 * n + 1]
        me = _my_place()
        for j in range(n):
            for kk in range(1, N_DEV):
                slot = j * N_PEERS + kk - 1
                cp = _split_copy(s_refs[j], l_refs[j], send_sems.at[slot], recv_sems.at[slot], me, kk, scatter, True)
                cp.wait_send()
                cp.wait_recv()

    res = pl.pallas_call(
        body, name=name, in_specs=[HBM] * (2 * n) + [SEM, SEM, ANY], out_specs=[HBM] * (2 * n),
        out_shape=[pltpu.HBM(t.shape, t.dtype) for t in (*srcs, *lands)],
        input_output_aliases={i: i for i in range(2 * n)},
        compiler_params=pltpu.CompilerParams(has_side_effects=EFFECT),
    )(*srcs, *lands, sems[0], sems[1], after)
    return res[n:]


def landing_zone(block, me_index):
    land = lax.empty((N_DEV,) + block.shape, block.dtype)
    return lax.dynamic_update_slice(land, block[None], (me_index,) + (0,) * block.ndim)


def sum_parts(parts, name, row_major_3d=False):
    _, r, c = parts.shape
    tc = _pick(c, (256, 128))

    def body(p_ref, o_ref):
        acc = p_ref[0].astype(F32)
        for i in range(1, N_DEV):
            acc = acc + p_ref[i].astype(F32)
        if row_major_3d:
            o_ref[:, 0, :] = acc
        else:
            o_ref[...] = acc

    out_spec = pl.BlockSpec((r, 1, tc), lambda i: (0, 0, i)) if row_major_3d else pl.BlockSpec((r, tc), lambda i: (0, i))
    return pl.pallas_call(
        body, name=name, grid=(c // tc,), in_specs=[pl.BlockSpec((N_DEV, r, tc), lambda i: (0, 0, i))],
        out_specs=out_spec, out_shape=SDS((r, 1, c) if row_major_3d else (r, c), F32),
        compiler_params=_cparams(("parallel",)),
    )(parts)


ADAMW_BLOCK_BYTES = 2 * 1024 * 1024


def adamw(w, g, m, v, name):
    shape = w.shape
    lay, rows, cols = ((1, 1) + shape)[-3:]
    tr = _pick(rows, (256, 128))
    tc = cols if tr * cols * 4 <= ADAMW_BLOCK_BYTES else _pick(cols, (256, 128))
    c1 = 1.0 / (1.0 - ADAM_B1 ** ADAM_STEP)
    c2 = 1.0 / (1.0 - ADAM_B2 ** ADAM_STEP)

    def body(w_ref, g_ref, m_ref, v_ref, d_ref, nm_ref, nv_ref):
        gg = g_ref[...]
        nm = ADAM_B1 * m_ref[...] + (1.0 - ADAM_B1) * gg
        nv = ADAM_B2 * v_ref[...] + (1.0 - ADAM_B2) * (gg * gg)
        d_ref[...] = -ADAM_LR * ((nm * c1) / (jnp.sqrt(nv * c2) + ADAM_EPS) + ADAM_WD * w_ref[...])
        nm_ref[...] = nm
        nv_ref[...] = nv

    spec = pl.BlockSpec((1, tr, tc), lambda l, i, j: (l, i, j))
    outs = pl.pallas_call(
        body, name=name, grid=(lay, rows // tr, cols // tc), in_specs=[spec] * 4, out_specs=[spec] * 3,
        out_shape=[SDS((lay, rows, cols), F32)] * 3, compiler_params=_cparams(("parallel",) * 3),
    )(*[t.reshape(lay, rows, cols) for t in (w, g, m, v)])
    return [o.reshape(shape) for o in outs]


def adamw_layer_inner(w, gs, m, v, name):
    rows, lay, cols = w.shape
    tr = _pick(rows, (256, 220, 128))
    c1 = 1.0 / (1.0 - ADAM_B1 ** ADAM_STEP)
    c2 = 1.0 / (1.0 - ADAM_B2 ** ADAM_STEP)

    def body(*refs):
        w_ref, m_ref, v_ref = refs[:3]
        g_refs = refs[3:3 + lay]
        go_ref, d_ref, nm_ref, nv_ref = refs[3 + lay:]
        for l, g_ref in enumerate(g_refs):
            gg = g_ref[:, 0, :]
            nm = ADAM_B1 * m_ref[:, l, :] + (1.0 - ADAM_B1) * gg
            nv = ADAM_B2 * v_ref[:, l, :] + (1.0 - ADAM_B2) * (gg * gg)
            d_ref[:, l, :] = -ADAM_LR * ((nm * c1) / (jnp.sqrt(nv * c2) + ADAM_EPS) + ADAM_WD * w_ref[:, l, :])
            go_ref[:, l, :] = gg
            nm_ref[:, l, :] = nm
            nv_ref[:, l, :] = nv

    inner = pl.BlockSpec((tr, lay, cols), lambda i: (i, 0, 0))
    plain = pl.BlockSpec((tr, 1, cols), lambda i: (i, 0, 0))
    return pl.pallas_call(
        body, name=name, grid=(rows // tr,), in_specs=[inner] * 3 + [plain] * lay, out_specs=[inner] * 4,
        out_shape=[SDS((rows, lay, cols), F32)] * 4, compiler_params=_cparams(("parallel",)),
    )(w, m, v, *gs)


BIG = ("w_in", "conv_w", "w_ssd_branch", "w_attn_branch", "w_out", "w_gate_up", "w_down")
TRANSPOSED = ("w_in", "w_gate_up")
SMALL = ("norm_mix", "conv_b", "dt_bias", "a_log", "d_skip", "ssd_norm", "norm_ffn")
SMALL_SIZE = {"norm_mix": 1024, "conv_b": 3072, "dt_bias": 32, "a_log": 32, "d_skip": 32, "ssd_norm": 2048, "norm_ffn": 1024}
FLAT_W = 512
SMALL_TOTAL = DEPTH * sum(SMALL_SIZE.values()) + D_MODEL + LANES
SMALL_ROWS = 32
assert SMALL_ROWS * FLAT_W >= SMALL_TOTAL


GROUPS = (("w_in", "conv_w"), ("w_ssd_branch", "w_attn_branch", "w_out"), ("w_gate_up", "w_down"))


def to_wire(k, shard):
    if k in TRANSPOSED:
        return shard.T.astype(BF16)
    return shard if k == "conv_w" else shard.astype(BF16)


def full_weights(k, g):
    if k == "conv_w":
        return {k: g.transpose(1, 0, 2).reshape(SSD_CONV, SSD_CONV_CH)}
    full = g.reshape(-1, g.shape[-1])
    if k != "w_in":
        return {k: full}
    w, off = {}, 0
    for nm, r in IN_ROWS:
        w[nm] = full[off:off + r]
        off += r
    w["w_q"] = full[sum(r for _, r in IN_ROWS[:3]):sum(r for _, r in IN_ROWS[:6])]
    w["w_dt"] = jnp.pad(w["w_dt"], ((0, HPAD - SSD_HEADS), (0, 0)))
    gap = jnp.zeros((DT_SLOT - SSD_HEADS, full.shape[1]), full.dtype)
    w["w_all"] = jnp.concatenate([full[:DT_ROW + SSD_HEADS], gap, full[DT_ROW + SSD_HEADS:]], axis=0)
    return w


def grads_to_wire(k, g):
    if k == "conv_w":
        return g.reshape(SSD_CONV, N_DEV, SSD_CONV_CH // N_DEV).transpose(1, 0, 2)
    return g.reshape(N_DEV, g.shape[0] // N_DEV, g.shape[1])


def _pad_heads(t):
    return jnp.pad(t.reshape(1, SSD_HEADS), ((0, 0), (0, HPAD - SSD_HEADS)))


def local_step(x, target, getw, emit, smalls, norm_final):
    tabs = rope_tables()
    sms = []
    for li in range(DEPTH):
        s = smalls[li]
        sms.append({
            "norm_mix": s["norm_mix"].reshape(1, -1), "conv_b": s["conv_b"].reshape(1, -1),
            "dt_bias": _pad_heads(s["dt_bias"]), "a_log": _pad_heads(s["a_log"]),
            "d_skip_x": jnp.repeat(s["d_skip"], SSD_HEAD_DIM).reshape(1, -1),
            "ssd_norm": s["ssd_norm"].reshape(1, -1), "norm_ffn": s["norm_ffn"].reshape(1, -1)})
    h = x
    saved = []
    for li in range(DEPTH):
        h, sv = layer_fwd(h, functools.partial(getw, li), sms[li], tabs, li)
        saved.append(sv)
    dh, g_final, loss = loss_head(h, target, norm_final.reshape(1, -1), "loss_head")
    gsms = [None] * DEPTH
    for li in reversed(range(DEPTH)):
        dh, gsm = layer_bwd(dh, saved[li], sms[li], tabs, li, functools.partial(emit, li))
        gsms[li] = {
            "norm_mix": gsm["norm_mix"].reshape(-1), "conv_b": gsm["conv_b"].reshape(-1),
            "dt_bias": gsm["dt_bias"][0, :SSD_HEADS], "a_log": gsm["a_log"][0, :SSD_HEADS],
            "d_skip": gsm["d_skip_x"].reshape(SSD_HEADS, SSD_HEAD_DIM).sum(axis=1),
            "ssd_norm": gsm["ssd_norm"].reshape(-1), "norm_ffn": gsm["norm_ffn"].reshape(-1)}
    return loss, dh, gsms, g_final.reshape(-1)


def kernel(x, norm_mix, w_in, conv_w, conv_b, dt_bias, a_log, d_skip, ssd_norm, w_ssd_branch, w_attn_branch, w_out, norm_ffn, w_gate_up, w_down, norm_final, loss_target, m_norm_mix, m_w_in, m_conv_w, m_conv_b, m_dt_bias, m_a_log, m_d_skip, m_ssd_norm, m_w_ssd_branch, m_w_attn_branch, m_w_out, m_norm_ffn, m_w_gate_up, m_w_down, m_norm_final, v_norm_mix, v_w_in, v_conv_w, v_conv_b, v_dt_bias, v_a_log, v_d_skip, v_ssd_norm, v_w_ssd_branch, v_w_attn_branch, v_w_out, v_norm_ffn, v_w_gate_up, v_w_down, v_norm_final):
    wv = dict(norm_mix=norm_mix, w_in=w_in, conv_w=conv_w, conv_b=conv_b, dt_bias=dt_bias, a_log=a_log, d_skip=d_skip,
              ssd_norm=ssd_norm, w_ssd_branch=w_ssd_branch, w_attn_branch=w_attn_branch, w_out=w_out, norm_ffn=norm_ffn,
              w_gate_up=w_gate_up, w_down=w_down, norm_final=norm_final)
    mv = dict(norm_mix=m_norm_mix, w_in=m_w_in, conv_w=m_conv_w, conv_b=m_conv_b, dt_bias=m_dt_bias, a_log=m_a_log,
              d_skip=m_d_skip, ssd_norm=m_ssd_norm, w_ssd_branch=m_w_ssd_branch, w_attn_branch=m_w_attn_branch,
              w_out=m_w_out, norm_ffn=m_norm_ffn, w_gate_up=m_w_gate_up, w_down=m_w_down, norm_final=m_norm_final)
    vv = dict(norm_mix=v_norm_mix, w_in=v_w_in, conv_w=v_conv_w, conv_b=v_conv_b, dt_bias=v_dt_bias, a_log=v_a_log,
              d_skip=v_d_skip, ssd_norm=v_ssd_norm, w_ssd_branch=v_w_ssd_branch, w_attn_branch=v_w_attn_branch,
              w_out=v_w_out, norm_ffn=v_norm_ffn, w_gate_up=v_w_gate_up, w_down=v_w_down, norm_final=v_norm_final)
    order = ("norm_mix", "w_in", "conv_w", "conv_b", "dt_bias", "a_log", "d_skip", "ssd_norm", "w_ssd_branch",
             "w_attn_branch", "w_out", "norm_ffn", "w_gate_up", "w_down", "norm_final")

    me_index = _index(_my_place())
    smalls = [{k: wv[k][li] for k in SMALL} for li in range(DEPTH)]
    n_groups = len(GROUPS)

    first_lands = all_gather([to_wire(k, wv[k][0]) for k in GROUPS[0]], "gather_first")
    later = [(li, gi) for li in range(DEPTH) for gi in range(n_groups)][1:]
    behind_first = first_lands[1][0, 0, 0] * 0.0
    srcs = [to_wire(k, wv[k][li] + behind_first if k == "conv_w" else wv[k][li]) for li, gi in later for k in GROUPS[gi]]
    sizes = [len(GROUPS[gi]) for _, gi in later]
    w_sems, w_srcs, w_lands, token = exchange_start(srcs, [landing_zone(s, me_index) for s in srcs], sizes, False, "gather_start")
    smalls[0]["norm_mix"] = smalls[0]["norm_mix"] + token[0, 0]

    def getw(li, gi, after):
        if (li, gi) == (0, 0):
            lands = first_lands
        else:
            slot = later.index((li, gi))
            sl = slice(sum(sizes[:slot]), sum(sizes[:slot + 1]))
            lands = exchange_wait(w_srcs[sl], w_lands[sl], w_sems[slot], after, False, f"gather_wait_{li}_{gi}")
        w = {}
        for k, land in zip(GROUPS[gi], lands):
            w.update(full_weights(k, land))
        return w

    pending = []

    def emit(li, gi, gw):
        parts = [grads_to_wire(k, gw[k]) for k in GROUPS[gi]]
        lands = [landing_zone(lax.dynamic_index_in_dim(p, me_index, 0, keepdims=False), me_index) for p in parts]
        sems, p_thru, l_thru, tok = exchange_start(parts, lands, [len(parts)], True, f"grads_start_{li}_{gi}")
        pending.append((li, gi, sems[0], p_thru, l_thru))
        return tok[0, 0]

    loss_p, dx, gsms, g_final = local_step(x[0], loss_target[0], getw, emit, smalls, norm_final)

    grads, deltas, new_m, new_v = {}, {}, {}, {}

    def update(k):
        if k == "w_in":
            inner = lambda t: t.transpose(2, 0, 1)
            outs = adamw_layer_inner(inner(wv[k]), shard_g[k], inner(mv[k]), inner(vv[k]), "adamw_" + k)
            grads[k], deltas[k], new_m[k], new_v[k] = (t.transpose(1, 2, 0) for t in outs)
            return outs[3]
        if k in BIG:
            grads[k] = jnp.stack([g.T if k in TRANSPOSED else g for g in shard_g[k]])
        deltas[k], new_m[k], new_v[k] = adamw(wv[k], grads[k], mv[k], vv[k], "adamw_" + k)
        return new_v[k]

    shard_g = {k: [None] * DEPTH for k in BIG}

    def collect(entry, after):
        li, gi, sems, p_thru, l_thru = entry
        recv = exchange_wait(p_thru, l_thru, sems, after, True, f"grads_wait_{li}_{gi}")
        for k, r in zip(GROUPS[gi], recv):
            if k == "conv_w":
                r = r.reshape(N_DEV, 1, -1)
            after = sum_parts(r, f"sum_{k}_{li}", row_major_3d=(k == "w_in"))
            shard_g[k][li] = after if k in TRANSPOSED else after.reshape(wv[k].shape[1:])
        return after

    after = dx
    for entry in pending[:-1]:
        after = collect(entry, after)
    done = [after[:1, :1].reshape(1)]
    for gi in (2, 1):
        for k in GROUPS[gi]:
            done.append(update(k).reshape(-1)[:1])

    flat = [gsms[li][k] for li in range(DEPTH) for k in SMALL] + [g_final, loss_p.reshape(-1)]
    flat.append(jnp.zeros((SMALL_ROWS * FLAT_W - SMALL_TOTAL,), F32))
    small_all = all_gather([jnp.concatenate(flat).reshape(SMALL_ROWS, FLAT_W)], "gather_small")[0]
    small_sum = sum_parts(small_all, "sum_small").reshape(-1)
    off = 0
    per_layer = {k: [] for k in SMALL}
    for li in range(DEPTH):
        for k in SMALL:
            per_layer[k].append(small_sum[off:off + SMALL_SIZE[k]])
            off += SMALL_SIZE[k]
    for k in SMALL:
        grads[k] = jnp.stack(per_layer[k])
    grads["norm_final"] = small_sum[off:off + D_MODEL]
    loss = small_sum[off + D_MODEL]
    for k in (*SMALL, "norm_final"):
        done.append(update(k).reshape(-1)[:1])

    collect(pending[-1], jnp.concatenate(done))
    for k in GROUPS[0]:
        update(k)

    return (loss, dx.reshape(x.shape), *[grads[k] for k in order], *[deltas[k] for k in order],
            *[new_m[k] for k in order], *[new_v[k] for k in order])
```

```python
import functools

import jax
import jax.numpy as jnp
from jax import lax
from jax.experimental import pallas as pl
from jax.experimental.pallas import tpu as pltpu

F32, BF16 = jnp.float32, jnp.bfloat16
SDS = jax.ShapeDtypeStruct
MESH = pl.DeviceIdType.MESH

D_MODEL = 1024
SEQ = 2048
DEPTH = 2
RMS_EPS = 1e-5
SSD_INNER = 2048
SSD_HEAD_DIM = 64
SSD_HEADS = 32
SSD_STATE = 128
SSD_GROUPS = 4
SSD_CONV = 4
SSD_CHUNK = 128
SSD_CONV_CH = 3072
ATTN_HEAD_DIM = 128
ATTN_KV_HEADS = 8
ATTN_DILATIONS = (1, 4, 16)
ATTN_N_PAT = 3
ATTN_BLOCK = 128
ATTN_OUT = 1024
ROPE_THETA = 500000.0
ROPE_DIM = 32
FFN_HIDDEN = 2816
ADAM_LR, ADAM_B1, ADAM_B2, ADAM_EPS, ADAM_WD, ADAM_STEP = 0.001, 0.9, 0.999, 1e-08, 0.01, 10

N_DEV = 8
LANES = 128
VMEM_LIMIT = 56 * 1024 * 1024
HPAD = 128
HIGHEST = lax.Precision.HIGHEST

IN_ROWS = (("w_z", 2048), ("w_xbc", 3072), ("w_dt", 32), ("w_q0", 1024), ("w_q1", 1024), ("w_q2", 1024),
           ("w_k", 1024), ("w_v", 1024), ("w_gs", 1024), ("w_ga", 1024))
N_IN = sum(r for _, r in IN_ROWS)


def _cparams(sem):
    return pltpu.CompilerParams(dimension_semantics=sem, vmem_limit_bytes=VMEM_LIMIT)


def _sigmoid(x):
    return 0.5 * jnp.tanh(0.5 * x) + 0.5


def _silu(x):
    return x * _sigmoid(x)


def _softplus(x):
    return jnp.maximum(x, 0.0) + jnp.log(1.0 + jnp.exp(-jnp.abs(x)))


def _dot(a, b, dims=(((1,), (0,)), ((), ())), precision=None):
    return lax.dot_general(a, b, dims, precision=precision, preferred_element_type=F32)


NT = (((1,), (1,)), ((), ()))
TN = (((0,), (0,)), ((), ()))


def _bdot(a, b, dims=(((1,), (0,)), ((), ()))):
    return _dot(a.astype(BF16), b.astype(BF16), dims)


def _pick(dim, cands):
    for c in cands:
        if dim % c == 0:
            return c
    return dim


def matmul(a, b, *, name, ta=False, tb=False, out_dtype=F32, add=None):
    m, k = (a.shape[1], a.shape[0]) if ta else a.shape
    n = b.shape[0] if tb else b.shape[1]
    tn = _pick(n, (1024, 1408, 512, 256, 128))
    tm = _pick(m, (512, 1408, 256, 128)) if tn == n else _pick(m, (1024, 1408, 512, 256, 128))
    tk = _pick(k, (1024, 1408, 512, 256, 128))
    nk = k // tk
    a_spec = pl.BlockSpec((tk, tm), lambda i, j, kk: (kk, i)) if ta else pl.BlockSpec((tm, tk), lambda i, j, kk: (i, kk))
    b_spec = pl.BlockSpec((tn, tk), lambda i, j, kk: (j, kk)) if tb else pl.BlockSpec((tk, tn), lambda i, j, kk: (kk, j))
    dims = (((0 if ta else 1,), (1 if tb else 0,)), ((), ()))
    has_add = add is not None

    def body(*refs):
        a_ref, b_ref = refs[:2]
        add_ref = refs[2] if has_add else None
        o_ref = refs[3] if has_add else refs[2]
        acc = refs[-1] if nk > 1 else None
        kk = pl.program_id(2)

        def product():
            return _dot(a_ref[...].astype(BF16), b_ref[...].astype(BF16), dims)

        def finish(r):
            if has_add:
                r = r + add_ref[...].astype(F32)
            o_ref[...] = r.astype(o_ref.dtype)

        if nk == 1:
            finish(product())
            return

        @pl.when(kk == 0)
        def _():
            acc[...] = product()

        @pl.when((kk > 0) & (kk < nk - 1))
        def _():
            acc[...] += product()

        @pl.when(kk == nk - 1)
        def _():
            finish(acc[...] + product())

    in_specs = [a_spec, b_spec]
    args = [a, b]
    if has_add:
        in_specs.append(pl.BlockSpec((tm, tn), lambda i, j, kk: (i, j)))
        args.append(add)
    return pl.pallas_call(
        body, name=name, grid=(m // tm, n // tn, nk),
        in_specs=in_specs, out_specs=pl.BlockSpec((tm, tn), lambda i, j, kk: (i, j)),
        out_shape=SDS((m, n), out_dtype), scratch_shapes=[pltpu.VMEM((tm, tn), F32)] if nk > 1 else [],
        compiler_params=_cparams(("parallel", "parallel", "arbitrary")),
    )(*args)


def rowcall(name, fn, rows, params, row_outs, red_outs=(), tr=256):
    s = rows[0].shape[0]
    n_in = len(rows) + len(params)
    n_row = len(row_outs)

    def body(*refs):
        outs = fn(*[r[...].astype(F32) for r in refs[:n_in]])
        if not isinstance(outs, (tuple, list)):
            outs = (outs,)
        orefs = refs[n_in:]
        for r, o in zip(orefs[:n_row], outs[:n_row]):
            r[...] = o.astype(r.dtype)
        if red_outs:
            @pl.when(pl.program_id(0) == 0)
            def _():
                for r in orefs[n_row:]:
                    r[...] = jnp.zeros_like(r)
            for r, o in zip(orefs[n_row:], outs[n_row:]):
                r[...] += o.astype(F32)

    widths = [a[1] if isinstance(a, tuple) else a.shape[1] for a in rows]
    rows = [a[0] if isinstance(a, tuple) else a for a in rows]
    in_specs = [pl.BlockSpec((tr, wd), lambda i: (i, 0)) for wd in widths]
    in_specs += [pl.BlockSpec(p.shape, lambda i: (0, 0)) for p in params]
    out_specs = [pl.BlockSpec((tr, c), lambda i: (i, 0)) for c, _ in row_outs]
    out_specs += [pl.BlockSpec(shp, lambda i: (0, 0)) for shp in red_outs]
    out_shape = [SDS((s, c), dt) for c, dt in row_outs] + [SDS(shp, F32) for shp in red_outs]
    res = pl.pallas_call(
        body, name=name, grid=(s // tr,), in_specs=in_specs, out_specs=out_specs, out_shape=out_shape,
        compiler_params=_cparams(("arbitrary",) if red_outs else ("parallel",)),
    )(*rows, *params)
    return res


def _rms(x, w):
    return x * lax.rsqrt(jnp.mean(x * x, axis=-1, keepdims=True) + RMS_EPS) * w


def rms_fwd(h, w, name):
    return rowcall(name, _rms, [h], [w], [(D_MODEL, BF16)])[0]


def rms_bwd(h, du, dres, w, name):
    def fn(hb, dub, dresb, wb):
        _, vjp = jax.vjp(_rms, hb, wb)
        dh, dw = vjp(dub)
        return dh + dresb, dw
    return rowcall(name, fn, [h, du, dres], [w], [(D_MODEL, F32)], [(1, D_MODEL)])


def loss_head(h, target, w, name):
    def fn(hb, tb, wb):
        def f(hh, ww):
            err = _rms(hh, ww) - tb
            return 0.5 * jnp.sum(jnp.mean(err * err, axis=-1, keepdims=True), axis=0, keepdims=True)
        val, vjp = jax.vjp(f, hb, wb)
        dh, dw = vjp(jnp.ones((1, 1), F32))
        return dh, dw, jnp.broadcast_to(val, (1, LANES))
    return rowcall(name, fn, [h, target], [w], [(D_MODEL, F32)], [(1, D_MODEL), (1, LANES)])


def _gate(a, b, gs, ga):
    return _sigmoid(gs) * a + _sigmoid(ga) * b


def gate_fwd(a, b, gs, ga, name):
    return rowcall(name, _gate, [a, b, gs, ga], [], [(D_MODEL, BF16)])[0]


def gate_bwd(a, b, gs, ga, dm, name):
    def fn(ab, bb, gsb, gab, dmb):
        _, vjp = jax.vjp(_gate, ab, bb, gsb, gab)
        return vjp(dmb)
    return rowcall(name, fn, [a, b, gs, ga, dm], [], [(D_MODEL, BF16)] * 4)


def _swiglu(gu):
    return _silu(gu[:, :FFN_HIDDEN]) * gu[:, FFN_HIDDEN:]


def swiglu_fwd(gu, name):
    return rowcall(name, _swiglu, [gu], [], [(FFN_HIDDEN, BF16)])[0]


def swiglu_bwd(gu, dact, name):
    def fn(gub, db):
        _, vjp = jax.vjp(_swiglu, gub)
        return vjp(db.astype(F32))[0]
    return rowcall(name, fn, [gu, dact], [], [(2 * FFN_HIDDEN, BF16)])[0]


def _ssd_post(y, xs, z, dskip, normw):
    y = (y + dskip * xs) * _silu(z)
    gw = SSD_INNER // SSD_GROUPS
    parts = []
    for g in range(SSD_GROUPS):
        yg = y[:, g * gw:(g + 1) * gw]
        parts.append(yg * lax.rsqrt(jnp.mean(yg * yg, axis=-1, keepdims=True) + RMS_EPS))
    return jnp.concatenate(parts, axis=-1) * normw


def ssd_post_fwd(y, xc, z, dskip, normw, name):
    return rowcall(name, _ssd_post, [y, (xc, SSD_INNER), z], [dskip, normw], [(SSD_INNER, BF16)])[0]


def ssd_post_bwd(y, xc, z, dskip, normw, dyn, name):
    def fn(yb, xsb, zb, dynb, db, nb):
        _, vjp = jax.vjp(_ssd_post, yb, xsb, zb, db, nb)
        return vjp(dynb)
    return rowcall(name, fn, [y, (xc, SSD_INNER), z, dyn], [dskip, normw],
                   [(SSD_INNER, F32), (SSD_INNER, F32), (SSD_INNER, BF16)], [(1, SSD_INNER), (1, SSD_INNER)])


def _rope(t, cosf, sina, sinb):
    return t * cosf + pltpu.roll(t, LANES - ROPE_DIM // 2, 1) * sina + pltpu.roll(t, ROPE_DIM // 2, 1) * sinb


def rope_tables():
    half = ROPE_DIM // 2
    inv = ROPE_THETA ** (-jnp.arange(0, ROPE_DIM, 2, dtype=F32) / ROPE_DIM)
    ang = jnp.arange(SEQ, dtype=F32)[:, None] * inv[None, :]
    cos, sin = jnp.cos(ang), jnp.sin(ang)
    zeros = jnp.zeros((SEQ, LANES - ROPE_DIM), F32)
    z16 = jnp.zeros((SEQ, half), F32)
    cosf = jnp.concatenate([cos, cos, jnp.ones((SEQ, LANES - ROPE_DIM), F32)], axis=1)
    sina = jnp.concatenate([-sin, z16, zeros], axis=1)
    sinb = jnp.concatenate([z16, sin, zeros], axis=1)
    return cosf, sina, sinb


CONV_TC = 256


def _conv_pre(x, w, b, row):
    acc = x * w[SSD_CONV - 1:SSD_CONV, :] + b
    shifted = [x]
    for j in range(1, SSD_CONV):
        xs = jnp.where(row >= j, pltpu.roll(x, j, 0), 0.0)
        shifted.append(xs)
        acc = acc + xs * w[SSD_CONV - 1 - j:SSD_CONV - j, :]
    return acc, shifted


def conv_fwd(xbc, w, b, name):
    def body(x_ref, w_ref, b_ref, o_ref):
        row = lax.broadcasted_iota(jnp.int32, (SEQ, CONV_TC), 0)
        pre, _ = _conv_pre(x_ref[...].astype(F32), w_ref[...], b_ref[...], row)
        o_ref[...] = _silu(pre)
    return pl.pallas_call(
        body, name=name, grid=(SSD_CONV_CH // CONV_TC,),
        in_specs=[pl.BlockSpec((SEQ, CONV_TC), lambda i: (0, i)), pl.BlockSpec((SSD_CONV, CONV_TC), lambda i: (0, i)),
                  pl.BlockSpec((1, CONV_TC), lambda i: (0, i))],
        out_specs=pl.BlockSpec((SEQ, CONV_TC), lambda i: (0, i)),
        out_shape=SDS((SEQ, SSD_CONV_CH), F32), compiler_params=_cparams(("parallel",)),
    )(xbc, w, b)


def conv_bwd(xbc, w, b, dxc, name):
    def body(x_ref, w_ref, b_ref, dy_ref, dx_ref, dw_ref, db_ref):
        row = lax.broadcasted_iota(jnp.int32, (SEQ, CONV_TC), 0)
        wv = w_ref[...]
        pre, shifted = _conv_pre(x_ref[...].astype(F32), wv, b_ref[...], row)
        sg = _sigmoid(pre)
        ds = dy_ref[...] * (sg * (1.0 + pre * (1.0 - sg)))
        dx = ds * wv[SSD_CONV - 1:SSD_CONV, :]
        for j in range(1, SSD_CONV):
            dsj = jnp.where(row < SEQ - j, pltpu.roll(ds, SEQ - j, 0), 0.0)
            dx = dx + dsj * wv[SSD_CONV - 1 - j:SSD_CONV - j, :]
        dx_ref[...] = dx.astype(dx_ref.dtype)
        for j in range(SSD_CONV):
            dw_ref[SSD_CONV - 1 - j:SSD_CONV - j, :] = jnp.sum(ds * shifted[j], axis=0, keepdims=True)
        db_ref[...] = jnp.sum(ds, axis=0, keepdims=True)
    return pl.pallas_call(
        body, name=name, grid=(SSD_CONV_CH // CONV_TC,),
        in_specs=[pl.BlockSpec((SEQ, CONV_TC), lambda i: (0, i)), pl.BlockSpec((SSD_CONV, CONV_TC), lambda i: (0, i)),
                  pl.BlockSpec((1, CONV_TC), lambda i: (0, i)), pl.BlockSpec((SEQ, CONV_TC), lambda i: (0, i))],
        out_specs=[pl.BlockSpec((SEQ, CONV_TC), lambda i: (0, i)), pl.BlockSpec((SSD_CONV, CONV_TC), lambda i: (0, i)),
                   pl.BlockSpec((1, CONV_TC), lambda i: (0, i))],
        out_shape=[SDS((SEQ, SSD_CONV_CH), BF16), SDS((SSD_CONV, SSD_CONV_CH), F32), SDS((1, SSD_CONV_CH), F32)],
        compiler_params=_cparams(("parallel",)),
    )(xbc, w, b, dxc)


N_CHUNKS = SEQ // SSD_CHUNK
N_PAIRS = SSD_HEADS // 2
PAIRS_PER_GROUP = N_PAIRS // SSD_GROUPS
B_OFF = SSD_INNER
C_OFF = SSD_INNER + SSD_GROUPS * SSD_STATE


def _ssd_prefix(dtr, dtr_t, dtb, dtb_t, alog, alog_t):
    ln = SSD_CHUNK
    dt = _softplus(dtr + dtb)
    dt_t = _softplus(dtr_t + dtb_t)
    dta = dt * (-jnp.exp(alog))
    dta_t = dt_t * (-jnp.exp(alog_t))
    r = lax.broadcasted_iota(jnp.int32, (ln, ln), 0)
    c = lax.broadcasted_iota(jnp.int32, (ln, ln), 1)
    a_cum = _dot((r >= c).astype(F32), dta, precision=HIGHEST)
    a_cum_t = _dot(dta_t, (r <= c).astype(F32), precision=HIGHEST)
    a_last = jnp.sum(dta_t, axis=1, keepdims=True)
    return dt, a_cum, a_cum_t, a_last


def _ssd_pair(x_pair, bg, cg, hp, dt, a_cum, a_cum_t, a_last, *, e0):
    ln = SSD_CHUNK
    lane = lax.broadcasted_iota(jnp.int32, (ln, LANES), 1)
    sub = lax.broadcasted_iota(jnp.int32, (LANES, SSD_STATE), 0)
    row = lax.broadcasted_iota(jnp.int32, (ln, ln), 0)
    col = lax.broadcasted_iota(jnp.int32, (ln, ln), 1)
    lo = lane < SSD_HEAD_DIM
    e1 = e0 + 1
    c0, c1 = a_cum[:, e0:e0 + 1], a_cum[:, e1:e1 + 1]
    r0, r1 = a_cum_t[e0:e0 + 1, :], a_cum_t[e1:e1 + 1, :]
    l0, l1 = a_last[e0:e0 + 1, :], a_last[e1:e1 + 1, :]
    xd = x_pair * jnp.where(lo, dt[:, e0:e0 + 1], dt[:, e1:e1 + 1])
    causal = row >= col
    cb = _bdot(cg, bg, NT)
    m0 = cb * jnp.exp(jnp.where(causal, c0 - r0, -jnp.inf))
    m1 = cb * jnp.exp(jnp.where(causal, c1 - r1, -jnp.inf))
    y = _bdot(m0, jnp.where(lo, xd, 0.0)) + _bdot(m1, jnp.where(lo, 0.0, xd))
    acum_pair = jnp.where(lo, c0, c1)
    y = y + _bdot(cg, hp, NT) * jnp.exp(acum_pair)
    last_pair = jnp.where(lo, l0, l1)
    st = _bdot(xd * jnp.exp(last_pair - acum_pair), bg, TN)
    h_out = hp * jnp.exp(jnp.where(sub < SSD_HEAD_DIM, l0, l1)) + st
    return y, h_out


def _ssd_in_specs(chunk_of):
    return [
        pl.BlockSpec((SSD_CHUNK, SSD_CONV_CH), lambda i: (chunk_of(i), 0)),
        pl.BlockSpec((SSD_CHUNK, HPAD), lambda i: (chunk_of(i), 0)),
        pl.BlockSpec((HPAD, SSD_CHUNK), lambda i: (0, chunk_of(i))),
        pl.BlockSpec((1, HPAD), lambda i: (0, 0)), pl.BlockSpec((HPAD, 1), lambda i: (0, 0)),
        pl.BlockSpec((1, HPAD), lambda i: (0, 0)), pl.BlockSpec((HPAD, 1), lambda i: (0, 0)),
    ]


def ssd_fwd(xc, dtr, dtr_t, dtb, dtb_t, alog, alog_t, name):
    def body(xc_ref, dtr_ref, dtrt_ref, dtb_ref, dtbt_ref, al_ref, alt_ref, y_ref, hs_ref, h_scr):
        @pl.when(pl.program_id(0) == 0)
        def _():
            h_scr[...] = jnp.zeros_like(h_scr)

        hs_ref[0] = h_scr[...]
        dt, a_cum, a_cum_t, a_last = _ssd_prefix(dtr_ref[...], dtrt_ref[...], dtb_ref[...], dtbt_ref[...],
                                                  al_ref[...], alt_ref[...])
        for pr in range(N_PAIRS):
            g = pr // PAIRS_PER_GROUP
            sl = slice(pr * LANES, (pr + 1) * LANES)
            bg = xc_ref[:, B_OFF + g * SSD_STATE:B_OFF + (g + 1) * SSD_STATE]
            cg = xc_ref[:, C_OFF + g * SSD_STATE:C_OFF + (g + 1) * SSD_STATE]
            y, h_out = _ssd_pair(xc_ref[:, sl], bg, cg, h_scr[sl, :], dt, a_cum, a_cum_t, a_last, e0=2 * pr)
            y_ref[:, sl] = y
            h_scr[sl, :] = h_out

    return pl.pallas_call(
        body, name=name, grid=(N_CHUNKS,), in_specs=_ssd_in_specs(lambda i: i),
        out_specs=[pl.BlockSpec((SSD_CHUNK, SSD_INNER), lambda i: (i, 0)),
                   pl.BlockSpec((1, SSD_INNER, SSD_STATE), lambda i: (i, 0, 0))],
        out_shape=[SDS((SEQ, SSD_INNER), F32), SDS((N_CHUNKS, SSD_INNER, SSD_STATE), F32)],
        scratch_shapes=[pltpu.VMEM((SSD_INNER, SSD_STATE), F32)],
        compiler_params=_cparams(("arbitrary",)),
    )(xc, dtr, dtr_t, dtb, dtb_t, alog, alog_t)


def ssd_bwd(xc, dtr, dtr_t, dtb, dtb_t, alog, alog_t, hs, dy, dxs_extra, name):
    rev = lambda i: N_CHUNKS - 1 - i

    def body(xc_ref, dtr_ref, dtrt_ref, dtb_ref, dtbt_ref, al_ref, alt_ref, hs_ref, dy_ref, dxe_ref,
             dxc_ref, ddtr_ref, ddtrt_ref, ddtb_ref, ddtbt_ref, dal_ref, dalt_ref, dh_scr):
        @pl.when(pl.program_id(0) == 0)
        def _():
            dh_scr[...] = jnp.zeros_like(dh_scr)
            for r in (ddtb_ref, ddtbt_ref, dal_ref, dalt_ref):
                r[...] = jnp.zeros_like(r)

        prefix_in = (dtr_ref[...], dtrt_ref[...], dtb_ref[...], dtbt_ref[...], al_ref[...], alt_ref[...])
        (dt, a_cum, a_cum_t, a_last), prefix_vjp = jax.vjp(_ssd_prefix, *prefix_in)
        d_dt = jnp.zeros_like(dt)
        d_acum = jnp.zeros_like(a_cum)
        d_acum_t = jnp.zeros_like(a_cum_t)
        d_alast = jnp.zeros_like(a_last)
        for g in range(SSD_GROUPS):
            bg = xc_ref[:, B_OFF + g * SSD_STATE:B_OFF + (g + 1) * SSD_STATE]
            cg = xc_ref[:, C_OFF + g * SSD_STATE:C_OFF + (g + 1) * SSD_STATE]
            d_bg = jnp.zeros_like(bg)
            d_cg = jnp.zeros_like(cg)
            for j in range(PAIRS_PER_GROUP):
                pr = g * PAIRS_PER_GROUP + j
                sl = slice(pr * LANES, (pr + 1) * LANES)
                _, vjp = jax.vjp(functools.partial(_ssd_pair, e0=2 * pr),
                                 xc_ref[:, sl], bg, cg, hs_ref[0, sl, :], dt, a_cum, a_cum_t, a_last)
                dx, dbg, dcg, dhp, ddt, dac, dact, dal = vjp((dy_ref[:, sl], dh_scr[sl, :]))
                dxc_ref[:, sl] = dx + dxe_ref[:, sl]
                dh_scr[sl, :] = dhp
                d_bg, d_cg = d_bg + dbg, d_cg + dcg
                d_dt, d_acum, d_acum_t, d_alast = d_dt + ddt, d_acum + dac, d_acum_t + dact, d_alast + dal
            dxc_ref[:, B_OFF + g * SSD_STATE:B_OFF + (g + 1) * SSD_STATE] = d_bg
            dxc_ref[:, C_OFF + g * SSD_STATE:C_OFF + (g + 1) * SSD_STATE] = d_cg
        g_dtr, g_dtrt, g_dtb, g_dtbt, g_al, g_alt = prefix_vjp((d_dt, d_acum, d_acum_t, d_alast))
        ddtr_ref[...] = g_dtr
        ddtrt_ref[...] = g_dtrt
        ddtb_ref[...] += g_dtb
        ddtbt_ref[...] += g_dtbt
        dal_ref[...] += g_al
        dalt_ref[...] += g_alt

    in_specs = _ssd_in_specs(rev) + [
        pl.BlockSpec((1, SSD_INNER, SSD_STATE), lambda i: (rev(i), 0, 0)),
        pl.BlockSpec((SSD_CHUNK, SSD_INNER), lambda i: (rev(i), 0)),
        pl.BlockSpec((SSD_CHUNK, SSD_INNER), lambda i: (rev(i), 0)),
    ]
    out_specs = [
        pl.BlockSpec((SSD_CHUNK, SSD_CONV_CH), lambda i: (rev(i), 0)),
        pl.BlockSpec((SSD_CHUNK, HPAD), lambda i: (rev(i), 0)),
        pl.BlockSpec((HPAD, SSD_CHUNK), lambda i: (0, rev(i))),
        pl.BlockSpec((1, HPAD), lambda i: (0, 0)), pl.BlockSpec((HPAD, 1), lambda i: (0, 0)),
        pl.BlockSpec((1, HPAD), lambda i: (0, 0)), pl.BlockSpec((HPAD, 1), lambda i: (0, 0)),
    ]
    out_shape = [SDS((SEQ, SSD_CONV_CH), F32), SDS((SEQ, HPAD), F32), SDS((HPAD, SEQ), F32),
                 SDS((1, HPAD), F32), SDS((HPAD, 1), F32), SDS((1, HPAD), F32), SDS((HPAD, 1), F32)]
    return pl.pallas_call(
        body, name=name, grid=(N_CHUNKS,), in_specs=in_specs, out_specs=out_specs, out_shape=out_shape,
        scratch_shapes=[pltpu.VMEM((SSD_INNER, SSD_STATE), F32)],
        compiler_params=_cparams(("arbitrary",)),
    )(xc, dtr, dtr_t, dtb, dtb_t, alog, alog_t, hs, dy, dxs_extra)


ATTN_SCALE = ATTN_HEAD_DIM ** -0.5


UNITS_PER_PATTERN = SEQ // ATTN_BLOCK
ATTN_BATCH = 8


def _for_unit_batches(batch):
    for g, d in enumerate(ATTN_DILATIONS):
        nb = UNITS_PER_PATTERN // d
        span = d * ATTN_BLOCK

        def trip(t, carry, g=g, d=d, nb=nb, span=span):
            units = []
            for j in range(ATTN_BATCH):
                i = t * ATTN_BATCH + j
                r = i >> (nb.bit_length() - 1)
                n = i & (nb - 1)
                start = r + n * span
                prev = jnp.where(n > 0, start - span, start)
                units.append((pl.ds(start, ATTN_BLOCK, stride=d), pl.ds(prev, ATTN_BLOCK, stride=d), n > 0))
            batch(g, units)
            return carry
        lax.fori_loop(0, UNITS_PER_PATTERN // ATTN_BATCH, trip, 0)


def _unit_operands(units, q_scr, k_scr, v_scr):
    def pair(scr, rows, prows):
        return jnp.concatenate([scr[prows, :], scr[rows, :]], axis=0)
    qb = jnp.stack([q_scr[rows, :] for rows, _, _ in units]).astype(BF16)
    kb = jnp.stack([pair(k_scr, rows, prows) for rows, prows, _ in units]).astype(BF16)
    vb = jnp.stack([pair(v_scr, rows, prows) for rows, prows, _ in units]).astype(BF16)
    return qb, kb, vb


def _unit_scores(qb, kb, units):
    s = jnp.einsum("bqd,bkd->bqk", qb, kb, preferred_element_type=F32) * ATTN_SCALE
    qi = lax.broadcasted_iota(jnp.int32, (ATTN_BLOCK, 2 * ATTN_BLOCK), 0)
    kj = lax.broadcasted_iota(jnp.int32, (ATTN_BLOCK, 2 * ATTN_BLOCK), 1)
    own = (kj >= ATTN_BLOCK) & (kj - ATTN_BLOCK <= qi)
    before = (kj < ATTN_BLOCK) & (kj >= qi)
    keep = jnp.stack([own | (before & has_prev) for _, _, has_prev in units])
    return jnp.where(keep, s, -jnp.inf)


def _head_specs(n_q_groups):
    blk = (SEQ, ATTN_HEAD_DIM)
    q_specs = [pl.BlockSpec(blk, functools.partial(lambda h, g: (0, g * ATTN_KV_HEADS + h), g=g)) for g in range(n_q_groups)]
    head = pl.BlockSpec(blk, lambda h: (0, h))
    table = pl.BlockSpec(blk, lambda h: (0, 0))
    return q_specs, head, table


def attn_fwd(q, k, v, tabs, name):
    q_specs, head, table = _head_specs(ATTN_N_PAT)

    def body(q0_ref, q1_ref, q2_ref, k_ref, v_ref, c_ref, sa_ref, sb_ref, y_ref, lse_ref, *scr):
        qs, og, ls, ks, vs = scr[0:3], scr[3:6], scr[6:9], scr[9], scr[10]
        c, sa, sb = c_ref[...], sa_ref[...], sb_ref[...]
        for g, q_ref in enumerate((q0_ref, q1_ref, q2_ref)):
            qs[g][...] = _rope(q_ref[...].astype(F32), c, sa, sb)
        ks[...] = _rope(k_ref[...].astype(F32), c, sa, sb)
        vs[...] = v_ref[...].astype(F32)

        def batch(g, units):
            qb, kb, vb = _unit_operands(units, qs[g], ks, vs)
            s = _unit_scores(qb, kb, units)
            m = jnp.max(s, axis=2, keepdims=True)
            p = jnp.exp(s - m)
            l = jnp.sum(p, axis=2, keepdims=True)
            o = jnp.einsum("bqk,bkd->bqd", p.astype(BF16), vb, preferred_element_type=F32) / l
            lse_b = m + jnp.log(l)
            for j, (rows, _, _) in enumerate(units):
                og[g][rows, :] = o[j]
                ls[g][rows, :] = jnp.broadcast_to(lse_b[j], (ATTN_BLOCK, LANES))

        _for_unit_batches(batch)
        l0, l1, l2 = ls[0][...], ls[1][...], ls[2][...]
        m = jnp.maximum(jnp.maximum(l0, l1), l2)
        e0, e1, e2 = jnp.exp(l0 - m), jnp.exp(l1 - m), jnp.exp(l2 - m)
        den = e0 + e1 + e2
        y_ref[...] = ((e0 * og[0][...] + e1 * og[1][...] + e2 * og[2][...]) / den).astype(y_ref.dtype)
        lse_ref[...] = m + jnp.log(den)

    blk = (SEQ, ATTN_HEAD_DIM)
    return pl.pallas_call(
        body, name=name, grid=(ATTN_KV_HEADS,), in_specs=[*q_specs, head, head, table, table, table],
        out_specs=[head, head], out_shape=[SDS((SEQ, ATTN_OUT), BF16), SDS((SEQ, ATTN_OUT), F32)],
        scratch_shapes=[pltpu.VMEM(blk, F32)] * (3 * ATTN_N_PAT + 2),
        compiler_params=_cparams(("parallel",)),
    )(q, q, q, k, v, *tabs)


def attn_bwd(q, k, v, tabs, y, lse, dy, name):
    q_specs, head, table = _head_specs(ATTN_N_PAT)

    def body(q0_ref, q1_ref, q2_ref, k_ref, v_ref, c_ref, sa_ref, sb_ref, y_ref, lse_ref, dy_ref,
             dq0_ref, dq1_ref, dq2_ref, dk_ref, dv_ref, *scr):
        qs, dqs, ks, dks, dd, dvs, vs = scr[0:3], scr[3:6], scr[6], scr[7], scr[8], scr[9], scr[10]
        c, sa, sb = c_ref[...], sa_ref[...], sb_ref[...]
        for g, q_ref in enumerate((q0_ref, q1_ref, q2_ref)):
            qs[g][...] = _rope(q_ref[...].astype(F32), c, sa, sb)
        ks[...] = _rope(k_ref[...].astype(F32), c, sa, sb)
        vs[...] = v_ref[...].astype(F32)
        dks[...] = jnp.zeros_like(dks)
        dvs[...] = jnp.zeros_like(dvs)
        dyv = dy_ref[...]
        dd[...] = jnp.broadcast_to(jnp.sum(dyv * y_ref[...].astype(F32), axis=1, keepdims=True), dd.shape)

        def batch(g, units):
            qb, kb, vb = _unit_operands(units, qs[g], ks, vs)
            dob = jnp.stack([dy_ref[rows, :] for rows, _, _ in units]).astype(BF16)
            lse_b = jnp.stack([lse_ref[rows, :][:, 0:1] for rows, _, _ in units])
            dsum_b = jnp.stack([dd[rows, :][:, 0:1] for rows, _, _ in units])
            p = jnp.exp(_unit_scores(qb, kb, units) - lse_b)
            dp = jnp.einsum("bqd,bkd->bqk", dob, vb, preferred_element_type=F32)
            ds = (p * (dp - dsum_b) * ATTN_SCALE).astype(BF16)
            dq = jnp.einsum("bqk,bkd->bqd", ds, kb, preferred_element_type=F32)
            dk = jnp.einsum("bqk,bqd->bkd", ds, qb, preferred_element_type=F32)
            dv = jnp.einsum("bqk,bqd->bkd", p.astype(BF16), dob, preferred_element_type=F32)
            for j, (rows, prows, _) in enumerate(units):
                dqs[g][rows, :] = dq[j]
                dks[prows, :] += dk[j, :ATTN_BLOCK]
                dks[rows, :] += dk[j, ATTN_BLOCK:]
                dvs[prows, :] += dv[j, :ATTN_BLOCK]
                dvs[rows, :] += dv[j, ATTN_BLOCK:]

        _for_unit_batches(batch)
        for g, dq_ref in enumerate((dq0_ref, dq1_ref, dq2_ref)):
            dq_ref[...] = _rope(dqs[g][...], c, -sa, -sb).astype(dq_ref.dtype)
        dk_ref[...] = _rope(dks[...], c, -sa, -sb).astype(dk_ref.dtype)
        dv_ref[...] = dvs[...].astype(dv_ref.dtype)

    blk = (SEQ, ATTN_HEAD_DIM)
    out = SDS((SEQ, ATTN_OUT), BF16)
    return pl.pallas_call(
        body, name=name, grid=(ATTN_KV_HEADS,), in_specs=[*q_specs, head, head, table, table, table, head, head, head],
        out_specs=[head] * 5, out_shape=[out] * 5,
        scratch_shapes=[pltpu.VMEM(blk, F32)] * (2 * ATTN_N_PAT + 5),
        compiler_params=_cparams(("parallel",)),
    )(q, q, q, k, v, *tabs, y, lse, dy)


def layer_fwd(h, getw, small, tabs, li):
    n = f"l{li}_"
    sv = {}
    w = dict(getw(0, h))
    u = rms_fwd(h, small["norm_mix"], n + "rms_mix")
    z = matmul(u, w["w_z"], name=n + "mm_z", tb=True, out_dtype=BF16)
    xbc = matmul(u, w["w_xbc"], name=n + "mm_xbc", tb=True, out_dtype=BF16)
    dtr = matmul(u, w["w_dt"], name=n + "mm_dt", tb=True)
    q = matmul(u, w["w_q"], name=n + "mm_q", tb=True, out_dtype=BF16)
    k = matmul(u, w["w_k"], name=n + "mm_k", tb=True, out_dtype=BF16)
    v = matmul(u, w["w_v"], name=n + "mm_v", tb=True, out_dtype=BF16)
    gs = matmul(u, w["w_gs"], name=n + "mm_gs", tb=True, out_dtype=BF16)
    ga = matmul(u, w["w_ga"], name=n + "mm_ga", tb=True, out_dtype=BF16)
    xc = conv_fwd(xbc, w["conv_w"], small["conv_b"], n + "conv")
    dtr_t = dtr.T
    y_ssd, hs = ssd_fwd(xc, dtr, dtr_t, small["dt_bias"], small["dt_bias"].T, small["a_log"], small["a_log"].T, n + "ssd")
    yn = ssd_post_fwd(y_ssd, xc, z, small["d_skip_x"], small["ssd_norm"], n + "ssd_post")
    y_attn, lse = attn_fwd(q, k, v, tabs, n + "attn")
    w.update(getw(1, y_ssd))
    a = matmul(yn, w["w_ssd_branch"], name=n + "mm_a", out_dtype=BF16)
    b = matmul(y_attn, w["w_attn_branch"], name=n + "mm_b", out_dtype=BF16)
    merged = gate_fwd(a, b, gs, ga, n + "gate")
    h1 = matmul(merged, w["w_out"], name=n + "mm_o", add=h)
    w.update(getw(2, h1))
    u2 = rms_fwd(h1, small["norm_ffn"], n + "rms_ffn")
    gu = matmul(u2, w["w_gate_up"], name=n + "mm_gu", tb=True, out_dtype=BF16)
    act = swiglu_fwd(gu, n + "swiglu")
    h2 = matmul(act, w["w_down"], name=n + "mm_down", add=h1)
    sv.update(h=h, u=u, z=z, xbc=xbc, dtr=dtr, dtr_t=dtr_t, gs=gs, ga=ga, xc=xc, y_ssd=y_ssd, hs=hs, yn=yn,
              q=q, k=k, v=v, y_attn=y_attn, lse=lse, a=a, b=b, merged=merged, h1=h1, u2=u2, gu=gu, act=act, w=w)
    return h2, sv


def layer_bwd(dh, sv, small, tabs, li, emit):
    n = f"l{li}_b_"
    w = sv["w"]
    gw, gsm = {}, {}
    dact = matmul(dh, w["w_down"], name=n + "mm_dact", tb=True, out_dtype=BF16)
    gw["w_down"] = matmul(sv["act"], dh, name=n + "mm_dwdown", ta=True, out_dtype=BF16)
    dgu = swiglu_bwd(sv["gu"], dact, n + "swiglu")
    gw["w_gate_up"] = matmul(dgu, sv["u2"], name=n + "mm_dwgu", ta=True, out_dtype=BF16)
    tok = emit(2, gw)
    du2 = matmul(dgu, w["w_gate_up"], name=n + "mm_du2")
    dh1, gsm["norm_ffn"] = rms_bwd(sv["h1"], du2, dh, small["norm_ffn"] + tok, n + "rms_ffn")
    dmerged = matmul(dh1, w["w_out"], name=n + "mm_dmerged", tb=True)
    gw["w_out"] = matmul(sv["merged"], dh1, name=n + "mm_dwo", ta=True, out_dtype=BF16)
    da, db, dgs, dga = gate_bwd(sv["a"], sv["b"], sv["gs"], sv["ga"], dmerged, n + "gate")
    gw["w_ssd_branch"] = matmul(sv["yn"], da, name=n + "mm_dwa", ta=True, out_dtype=BF16)
    gw["w_attn_branch"] = matmul(sv["y_attn"], db, name=n + "mm_dwb", ta=True, out_dtype=BF16)
    tok = emit(1, gw)
    dyn = matmul(da, w["w_ssd_branch"], name=n + "mm_dyn", tb=True)
    dyattn = matmul(db, w["w_attn_branch"], name=n + "mm_dyattn", tb=True)
    dy_ssd, dxs_extra, dz, gsm["d_skip_x"], gsm["ssd_norm"] = ssd_post_bwd(
        sv["y_ssd"], sv["xc"], sv["z"], small["d_skip_x"] + tok, small["ssd_norm"], dyn, n + "ssd_post")
    dxc, ddtr, ddtr_t, ddtb, ddtb_t, dal, dal_t = ssd_bwd(
        sv["xc"], sv["dtr"], sv["dtr_t"], small["dt_bias"], small["dt_bias"].T, small["a_log"], small["a_log"].T,
        sv["hs"], dy_ssd, dxs_extra, n + "ssd")
    ddtr = (ddtr + ddtr_t.T).astype(BF16)
    gsm["dt_bias"] = ddtb + ddtb_t.T
    gsm["a_log"] = dal + dal_t.T
    dxbc, gw["conv_w"], gsm["conv_b"] = conv_bwd(sv["xbc"], w["conv_w"], small["conv_b"], dxc, n + "conv")
    dq0, dq1, dq2, dk, dv = attn_bwd(sv["q"], sv["k"], sv["v"], tabs, sv["y_attn"], sv["lse"], dyattn, n + "attn")
    u = sv["u"]
    segs = [("w_z", dz), ("w_xbc", dxbc), ("w_dt", ddtr), ("w_q0", dq0), ("w_q1", dq1), ("w_q2", dq2),
            ("w_k", dk), ("w_v", dv), ("w_gs", dgs), ("w_ga", dga)]
    gin = [matmul(dseg, u, name=n + "mm_d" + key, ta=True, out_dtype=BF16) for key, dseg in segs]
    gin[2] = gin[2][:SSD_HEADS]
    gw["w_in"] = jnp.concatenate(gin, axis=0)
    tok = emit(0, gw)
    du = jnp.zeros((SEQ, D_MODEL), F32) + tok
    for key, dseg in segs:
        du = matmul(dseg, w[key], name=n + "mm_du_" + key, add=du)
    dh0, gsm["norm_mix"] = rms_bwd(sv["h"], du, dh1, small["norm_mix"] + tok, n + "rms_mix")
    return dh0, gsm


def _my_place():
    return lax.axis_index("x"), lax.axis_index("y"), lax.axis_index("c")


def _flip(place, k):
    x, y, c = place
    return (1 - x if k & 4 else x, 1 - y if k & 2 else y, 1 - c if k & 1 else c)


def _index(place):
    return 4 * place[0] + 2 * place[1] + place[2]


ANY = pl.BlockSpec(memory_space=pl.ANY)
CHIP_FLIPS = (4, 2, 6)


def all_gather(xs, name):
    na = len(xs)

    def body(*refs):
        x_refs, o_refs = refs[:na], refs[na:2 * na]
        send_sems, recv_sems, local_sems = refs[2 * na:]
        me = _my_place()
        sibling = _flip(me, 1)
        chips = [_flip(me, f) for f in CHIP_FLIPS]

        def copy(a, kk, block, to, src=None):
            dst = o_refs[a].at[_index(block)]
            return pltpu.make_async_remote_copy(
                src_ref=dst if src is None else src, dst_ref=dst, send_sem=send_sems.at[a, kk],
                recv_sem=recv_sems.at[a, kk], device_id=to, device_id_type=MESH)

        mine = [pltpu.make_async_copy(x_refs[a], o_refs[a].at[_index(me)], local_sems.at[a]) for a in range(na)]
        for cp in mine:
            cp.start()
        first = []
        for j, chip in enumerate(chips):
            first += [copy(a, 1 + j, me, chip, src=x_refs[a]) for a in range(na)]
        first += [copy(a, 0, me, sibling, src=x_refs[a]) for a in range(na)]
        for cp in first:
            cp.start()
        passed = []
        for j, chip in enumerate(chips):
            for a in range(na):
                copy(a, 1 + j, chip, me).wait_recv()
                cp = copy(a, 4 + j, chip, sibling)
                cp.start()
                passed.append(cp)
        for a in range(na):
            copy(a, 0, sibling, me).wait_recv()
        for j, chip in enumerate(chips):
            for a in range(na):
                copy(a, 4 + j, _flip(chip, 1), me).wait_recv()
        for cp in first + passed:
            cp.wait_send()
        for cp in mine:
            cp.wait()

    return pl.pallas_call(
        body, name=name, in_specs=[ANY] * na, out_specs=[ANY] * na,
        out_shape=[SDS((N_DEV,) + t.shape, t.dtype) for t in xs],
        scratch_shapes=[pltpu.SemaphoreType.DMA((na, N_DEV - 1)), pltpu.SemaphoreType.DMA((na, N_DEV - 1)),
                        pltpu.SemaphoreType.DMA((na,))],
    )(*xs)


HBM = pl.BlockSpec(memory_space=pltpu.HBM)
SEM = pl.BlockSpec(memory_space=pltpu.SEMAPHORE)
EFFECT = pltpu.SideEffectType.DATAFLOW_SIDE_EFFECTING
N_PEERS = N_DEV - 1


def _split_copy(src_ref, land_ref, send_sem, recv_sem, me, kk, scatter, landed_from_peer):
    peer = _flip(me, kk)
    src = src_ref.at[_index(peer)] if scatter else src_ref
    dst = land_ref.at[_index(peer if landed_from_peer else me)]
    return pltpu.make_async_remote_copy(src_ref=src, dst_ref=dst, send_sem=send_sem, recv_sem=recv_sem,
                                        device_id=peer, device_id_type=MESH)


def exchange_start(srcs, lands, group_sizes, scatter, name):
    na, ng = len(srcs), len(group_sizes)

    def body(*refs):
        s_refs, l_refs = refs[:na], refs[na:2 * na]
        sems = refs[2 * na:2 * na + 2 * ng]
        token = refs[-1]
        me = _my_place()
        a = 0
        for gi, gsz in enumerate(group_sizes):
            for j in range(gsz):
                for kk in range(1, N_DEV):
                    slot = j * N_PEERS + kk - 1
                    _split_copy(s_refs[a], l_refs[a], sems[2 * gi].at[slot], sems[2 * gi + 1].at[slot],
                                me, kk, scatter, False).start()
                a += 1
        token[...] = jnp.zeros_like(token)

    sem_shapes = []
    for gsz in group_sizes:
        sem_shapes += [pltpu.SemaphoreType.DMA((gsz * N_PEERS,))] * 2
    ins = [pltpu.with_memory_space_constraint(t, pltpu.HBM) for t in (*srcs, *lands)]
    res = pl.pallas_call(
        body, name=name, in_specs=[HBM] * (2 * na),
        out_specs=[SEM] * (2 * ng) + [HBM] * (2 * na) + [pl.BlockSpec(memory_space=pltpu.VMEM)],
        out_shape=sem_shapes + [pltpu.HBM(t.shape, t.dtype) for t in ins] + [SDS((8, LANES), F32)],
        input_output_aliases={i: 2 * ng + i for i in range(2 * na)},
        compiler_params=pltpu.CompilerParams(has_side_effects=EFFECT),
    )(*ins)
    sems = [(res[2 * gi], res[2 * gi + 1]) for gi in range(ng)]
    thru = res[2 * ng:2 * ng + 2 * na]
    return sems, thru[:na], thru[na:], res[-1]


def exchange_wait(srcs, lands, sems, after, scatter, name):
    n = len(srcs)

    def body(*refs):
        s_refs, l_refs = refs[:n], refs[n:2 * n]
        send_sems, recv_sems = refs[2 * n], refs[2 * n + 1]
        me = _my_place()
        for j in range(n):
            for kk in range(1, N_DEV):
                slot = j * N_PEERS + kk - 1
                cp = _split_copy(s_refs[j], l_refs[j], send_sems.at[slot], recv_sems.at[slot], me, kk, scatter, True)
                cp.wait_send()
                cp.wait_recv()

    res = pl.pallas_call(
        body, name=name, in_specs=[HBM] * (2 * n) + [SEM, SEM, ANY], out_specs=[HBM] * (2 * n),
        out_shape=[pltpu.HBM(t.shape, t.dtype) for t in (*srcs, *lands)],
        input_output_aliases={i: i for i in range(2 * n)},
        compiler_params=pltpu.CompilerParams(has_side_effects=EFFECT),
    )(*srcs, *lands, sems[0], sems[1], after)
    return res[n:]


def landing_zone(block, me_index):
    land = lax.empty((N_DEV,) + block.shape, block.dtype)
    return lax.dynamic_update_slice(land, block[None], (me_index,) + (0,) * block.ndim)


def sum_parts(parts, name, row_major_3d=False):
    _, r, c = parts.shape
    tc = _pick(c, (256, 128))

    def body(p_ref, o_ref):
        acc = p_ref[0].astype(F32)
        for i in range(1, N_DEV):
            acc = acc + p_ref[i].astype(F32)
        if row_major_3d:
            o_ref[:, 0, :] = acc
        else:
            o_ref[...] = acc

    out_spec = pl.BlockSpec((r, 1, tc), lambda i: (0, 0, i)) if row_major_3d else pl.BlockSpec((r, tc), lambda i: (0, i))
    return pl.pallas_call(
        body, name=name, grid=(c // tc,), in_specs=[pl.BlockSpec((N_DEV, r, tc), lambda i: (0, 0, i))],
        out_specs=out_spec, out_shape=SDS((r, 1, c) if row_major_3d else (r, c), F32),
        compiler_params=_cparams(("parallel",)),
    )(parts)


ADAMW_BLOCK_BYTES = 2 * 1024 * 1024


def adamw(w, g, m, v, name):
    shape = w.shape
    lay, rows, cols = ((1, 1) + shape)[-3:]
    tr = _pick(rows, (256, 128))
    tc = cols if tr * cols * 4 <= ADAMW_BLOCK_BYTES else _pick(cols, (256, 128))
    c1 = 1.0 / (1.0 - ADAM_B1 ** ADAM_STEP)
    c2 = 1.0 / (1.0 - ADAM_B2 ** ADAM_STEP)

    def body(w_ref, g_ref, m_ref, v_ref, d_ref, nm_ref, nv_ref):
        gg = g_ref[...]
        nm = ADAM_B1 * m_ref[...] + (1.0 - ADAM_B1) * gg
        nv = ADAM_B2 * v_ref[...] + (1.0 - ADAM_B2) * (gg * gg)
        d_ref[...] = -ADAM_LR * ((nm * c1) / (jnp.sqrt(nv * c2) + ADAM_EPS) + ADAM_WD * w_ref[...])
        nm_ref[...] = nm
        nv_ref[...] = nv

    spec = pl.BlockSpec((1, tr, tc), lambda l, i, j: (l, i, j))
    outs = pl.pallas_call(
        body, name=name, grid=(lay, rows // tr, cols // tc), in_specs=[spec] * 4, out_specs=[spec] * 3,
        out_shape=[SDS((lay, rows, cols), F32)] * 3, compiler_params=_cparams(("parallel",) * 3),
    )(*[t.reshape(lay, rows, cols) for t in (w, g, m, v)])
    return [o.reshape(shape) for o in outs]


def adamw_layer_inner(w, gs, m, v, name):
    rows, lay, cols = w.shape
    tr = _pick(rows, (256, 220, 128))
    c1 = 1.0 / (1.0 - ADAM_B1 ** ADAM_STEP)
    c2 = 1.0 / (1.0 - ADAM_B2 ** ADAM_STEP)

    def body(*refs):
        w_ref, m_ref, v_ref = refs[:3]
        g_refs = refs[3:3 + lay]
        go_ref, d_ref, nm_ref, nv_ref = refs[3 + lay:]
        for l, g_ref in enumerate(g_refs):
            gg = g_ref[:, 0, :]
            nm = ADAM_B1 * m_ref[:, l, :] + (1.0 - ADAM_B1) * gg
            nv = ADAM_B2 * v_ref[:, l, :] + (1.0 - ADAM_B2) * (gg * gg)
            d_ref[:, l, :] = -ADAM_LR * ((nm * c1) / (jnp.sqrt(nv * c2) + ADAM_EPS) + ADAM_WD * w_ref[:, l, :])
            go_ref[:, l, :] = gg
            nm_ref[:, l, :] = nm
            nv_ref[:, l, :] = nv

    inner = pl.BlockSpec((tr, lay, cols), lambda i: (i, 0, 0))
    plain = pl.BlockSpec((tr, 1, cols), lambda i: (i, 0, 0))
    return pl.pallas_call(
        body, name=name, grid=(rows // tr,), in_specs=[inner] * 3 + [plain] * lay, out_specs=[inner] * 4,
        out_shape=[SDS((rows, lay, cols), F32)] * 4, compiler_params=_cparams(("parallel",)),
    )(w, m, v, *gs)


BIG = ("w_in", "conv_w", "w_ssd_branch", "w_attn_branch", "w_out", "w_gate_up", "w_down")
TRANSPOSED = ("w_in", "w_gate_up")
SMALL = ("norm_mix", "conv_b", "dt_bias", "a_log", "d_skip", "ssd_norm", "norm_ffn")
SMALL_SIZE = {"norm_mix": 1024, "conv_b": 3072, "dt_bias": 32, "a_log": 32, "d_skip": 32, "ssd_norm": 2048, "norm_ffn": 1024}
FLAT_W = 512
SMALL_TOTAL = DEPTH * sum(SMALL_SIZE.values()) + D_MODEL + LANES
SMALL_ROWS = 32
assert SMALL_ROWS * FLAT_W >= SMALL_TOTAL


GROUPS = (("w_in", "conv_w"), ("w_ssd_branch", "w_attn_branch", "w_out"), ("w_gate_up", "w_down"))


def to_wire(k, shard):
    if k in TRANSPOSED:
        return shard.T.astype(BF16)
    return shard if k == "conv_w" else shard.astype(BF16)


def full_weights(k, g):
    if k == "conv_w":
        return {k: g.transpose(1, 0, 2).reshape(SSD_CONV, SSD_CONV_CH)}
    full = g.reshape(-1, g.shape[-1])
    if k != "w_in":
        return {k: full}
    w, off = {}, 0
    for nm, r in IN_ROWS:
        w[nm] = full[off:off + r]
        off += r
    w["w_q"] = full[sum(r for _, r in IN_ROWS[:3]):sum(r for _, r in IN_ROWS[:6])]
    w["w_dt"] = jnp.pad(w["w_dt"], ((0, HPAD - SSD_HEADS), (0, 0)))
    return w


def grads_to_wire(k, g):
    if k == "conv_w":
        return g.reshape(SSD_CONV, N_DEV, SSD_CONV_CH // N_DEV).transpose(1, 0, 2)
    return g.reshape(N_DEV, g.shape[0] // N_DEV, g.shape[1])


def _pad_heads(t):
    return jnp.pad(t.reshape(1, SSD_HEADS), ((0, 0), (0, HPAD - SSD_HEADS)))


def local_step(x, target, getw, emit, smalls, norm_final):
    tabs = rope_tables()
    sms = []
    for li in range(DEPTH):
        s = smalls[li]
        sms.append({
            "norm_mix": s["norm_mix"].reshape(1, -1), "conv_b": s["conv_b"].reshape(1, -1),
            "dt_bias": _pad_heads(s["dt_bias"]), "a_log": _pad_heads(s["a_log"]),
            "d_skip_x": jnp.repeat(s["d_skip"], SSD_HEAD_DIM).reshape(1, -1),
            "ssd_norm": s["ssd_norm"].reshape(1, -1), "norm_ffn": s["norm_ffn"].reshape(1, -1)})
    h = x
    saved = []
    for li in range(DEPTH):
        h, sv = layer_fwd(h, functools.partial(getw, li), sms[li], tabs, li)
        saved.append(sv)
    dh, g_final, loss = loss_head(h, target, norm_final.reshape(1, -1), "loss_head")
    gsms = [None] * DEPTH
    for li in reversed(range(DEPTH)):
        dh, gsm = layer_bwd(dh, saved[li], sms[li], tabs, li, functools.partial(emit, li))
        gsms[li] = {
            "norm_mix": gsm["norm_mix"].reshape(-1), "conv_b": gsm["conv_b"].reshape(-1),
            "dt_bias": gsm["dt_bias"][0, :SSD_HEADS], "a_log": gsm["a_log"][0, :SSD_HEADS],
            "d_skip": gsm["d_skip_x"].reshape(SSD_HEADS, SSD_HEAD_DIM).sum(axis=1),
            "ssd_norm": gsm["ssd_norm"].reshape(-1), "norm_ffn": gsm["norm_ffn"].reshape(-1)}
    return loss, dh, gsms, g_final.reshape(-1)


def kernel(x, norm_mix, w_in, conv_w, conv_b, dt_bias, a_log, d_skip, ssd_norm, w_ssd_branch, w_attn_branch, w_out, norm_ffn, w_gate_up, w_down, norm_final, loss_target, m_norm_mix, m_w_in, m_conv_w, m_conv_b, m_dt_bias, m_a_log, m_d_skip, m_ssd_norm, m_w_ssd_branch, m_w_attn_branch, m_w_out, m_norm_ffn, m_w_gate_up, m_w_down, m_norm_final, v_norm_mix, v_w_in, v_conv_w, v_conv_b, v_dt_bias, v_a_log, v_d_skip, v_ssd_norm, v_w_ssd_branch, v_w_attn_branch, v_w_out, v_norm_ffn, v_w_gate_up, v_w_down, v_norm_final):
    wv = dict(norm_mix=norm_mix, w_in=w_in, conv_w=conv_w, conv_b=conv_b, dt_bias=dt_bias, a_log=a_log, d_skip=d_skip,
              ssd_norm=ssd_norm, w_ssd_branch=w_ssd_branch, w_attn_branch=w_attn_branch, w_out=w_out, norm_ffn=norm_ffn,
              w_gate_up=w_gate_up, w_down=w_down, norm_final=norm_final)
    mv = dict(norm_mix=m_norm_mix, w_in=m_w_in, conv_w=m_conv_w, conv_b=m_conv_b, dt_bias=m_dt_bias, a_log=m_a_log,
              d_skip=m_d_skip, ssd_norm=m_ssd_norm, w_ssd_branch=m_w_ssd_branch, w_attn_branch=m_w_attn_branch,
              w_out=m_w_out, norm_ffn=m_norm_ffn, w_gate_up=m_w_gate_up, w_down=m_w_down, norm_final=m_norm_final)
    vv = dict(norm_mix=v_norm_mix, w_in=v_w_in, conv_w=v_conv_w, conv_b=v_conv_b, dt_bias=v_dt_bias, a_log=v_a_log,
              d_skip=v_d_skip, ssd_norm=v_ssd_norm, w_ssd_branch=v_w_ssd_branch, w_attn_branch=v_w_attn_branch,
              w_out=v_w_out, norm_ffn=v_norm_ffn, w_gate_up=v_w_gate_up, w_down=v_w_down, norm_final=v_norm_final)
    order = ("norm_mix", "w_in", "conv_w", "conv_b", "dt_bias", "a_log", "d_skip", "ssd_norm", "w_ssd_branch",
             "w_attn_branch", "w_out", "norm_ffn", "w_gate_up", "w_down", "norm_final")

    me_index = _index(_my_place())
    smalls = [{k: wv[k][li] for k in SMALL} for li in range(DEPTH)]
    n_groups = len(GROUPS)

    first_lands = all_gather([to_wire(k, wv[k][0]) for k in GROUPS[0]], "gather_first")
    later = [(li, gi) for li in range(DEPTH) for gi in range(n_groups)][1:]
    behind_first = first_lands[1][0, 0, 0] * 0.0
    srcs = [to_wire(k, wv[k][li] + behind_first if k == "conv_w" else wv[k][li]) for li, gi in later for k in GROUPS[gi]]
    sizes = [len(GROUPS[gi]) for _, gi in later]
    w_sems, w_srcs, w_lands, token = exchange_start(srcs, [landing_zone(s, me_index) for s in srcs], sizes, False, "gather_start")
    smalls[0]["norm_mix"] = smalls[0]["norm_mix"] + token[0, 0]

    def getw(li, gi, after):
        if (li, gi) == (0, 0):
            lands = first_lands
        else:
            slot = later.index((li, gi))
            sl = slice(sum(sizes[:slot]), sum(sizes[:slot + 1]))
            lands = exchange_wait(w_srcs[sl], w_lands[sl], w_sems[slot], after, False, f"gather_wait_{li}_{gi}")
        w = {}
        for k, land in zip(GROUPS[gi], lands):
            w.update(full_weights(k, land))
        return w

    pending = []

    def emit(li, gi, gw):
        parts = [grads_to_wire(k, gw[k]) for k in GROUPS[gi]]
        lands = [landing_zone(lax.dynamic_index_in_dim(p, me_index, 0, keepdims=False), me_index) for p in parts]
        sems, p_thru, l_thru, tok = exchange_start(parts, lands, [len(parts)], True, f"grads_start_{li}_{gi}")
        pending.append((li, gi, sems[0], p_thru, l_thru))
        return tok[0, 0]

    loss_p, dx, gsms, g_final = local_step(x[0], loss_target[0], getw, emit, smalls, norm_final)

    grads, deltas, new_m, new_v = {}, {}, {}, {}

    def update(k):
        if k == "w_in":
            inner = lambda t: t.transpose(2, 0, 1)
            outs = adamw_layer_inner(inner(wv[k]), shard_g[k], inner(mv[k]), inner(vv[k]), "adamw_" + k)
            grads[k], deltas[k], new_m[k], new_v[k] = (t.transpose(1, 2, 0) for t in outs)
            return outs[3]
        if k in BIG:
            grads[k] = jnp.stack([g.T if k in TRANSPOSED else g for g in shard_g[k]])
        deltas[k], new_m[k], new_v[k] = adamw(wv[k], grads[k], mv[k], vv[k], "adamw_" + k)
        return new_v[k]

    shard_g = {k: [None] * DEPTH for k in BIG}

    def collect(entry, after):
        li, gi, sems, p_thru, l_thru = entry
        recv = exchange_wait(p_thru, l_thru, sems, after, True, f"grads_wait_{li}_{gi}")
        for k, r in zip(GROUPS[gi], recv):
            if k == "conv_w":
                r = r.reshape(N_DEV, 1, -1)
            after = sum_parts(r, f"sum_{k}_{li}", row_major_3d=(k == "w_in"))
            shard_g[k][li] = after if k in TRANSPOSED else after.reshape(wv[k].shape[1:])
        return after

    after = dx
    for entry in pending[:-1]:
        after = collect(entry, after)
    done = [after[:1, :1].reshape(1)]
    for gi in (2, 1):
        for k in GROUPS[gi]:
            done.append(update(k).reshape(-1)[:1])

    flat = [gsms[li][k] for li in range(DEPTH) for k in SMALL] + [g_final, loss_p.reshape(-1)]
    flat.append(jnp.zeros((SMALL_ROWS * FLAT_W - SMALL_TOTAL,), F32))
    small_all = all_gather([jnp.concatenate(flat).reshape(SMALL_ROWS, FLAT_W)], "gather_small")[0]
    small_sum = sum_parts(small_all, "sum_small").reshape(-1)
    off = 0
    per_layer = {k: [] for k in SMALL}
    for li in range(DEPTH):
        for k in SMALL:
            per_layer[k].append(small_sum[off:off + SMALL_SIZE[k]])
            off += SMALL_SIZE[k]
    for k in SMALL:
        grads[k] = jnp.stack(per_layer[k])
    grads["norm_final"] = small_sum[off:off + D_MODEL]
    loss = small_sum[off + D_MODEL]
    for k in (*SMALL, "norm_final"):
        done.append(update(k).reshape(-1)[:1])

    collect(pending[-1], jnp.concatenate(done))
    for k in GROUPS[0]:
        update(k)

    return (loss, dx.reshape(x.shape), *[grads[k] for k in order], *[deltas[k] for k in order],
            *[new_m[k] for k in order], *[new_v[k] for k in order])
```

```python
import functools

import jax
import jax.numpy as jnp
from jax import lax
from jax.experimental import pallas as pl
from jax.experimental.pallas import tpu as pltpu

F32, BF16 = jnp.float32, jnp.bfloat16
SDS = jax.ShapeDtypeStruct
MESH = pl.DeviceIdType.MESH

D_MODEL = 1024
SEQ = 2048
DEPTH = 2
RMS_EPS = 1e-5
SSD_INNER = 2048
SSD_HEAD_DIM = 64
SSD_HEADS = 32
SSD_STATE = 128
SSD_GROUPS = 4
SSD_CONV = 4
SSD_CHUNK = 128
SSD_CONV_CH = 3072
ATTN_HEAD_DIM = 128
ATTN_KV_HEADS = 8
ATTN_DILATIONS = (1, 4, 16)
ATTN_N_PAT = 3
ATTN_BLOCK = 128
ATTN_OUT = 1024
ROPE_THETA = 500000.0
ROPE_DIM = 32
FFN_HIDDEN = 2816
ADAM_LR, ADAM_B1, ADAM_B2, ADAM_EPS, ADAM_WD, ADAM_STEP = 0.001, 0.9, 0.999, 1e-08, 0.01, 10

N_DEV = 8
LANES = 128
VMEM_LIMIT = 56 * 1024 * 1024
HPAD = 128
HIGHEST = lax.Precision.HIGHEST

IN_ROWS = (("w_z", 2048), ("w_xbc", 3072), ("w_dt", 32), ("w_q0", 1024), ("w_q1", 1024), ("w_q2", 1024),
           ("w_k", 1024), ("w_v", 1024), ("w_gs", 1024), ("w_ga", 1024))
N_IN = sum(r for _, r in IN_ROWS)


def _cparams(sem):
    return pltpu.CompilerParams(dimension_semantics=sem, vmem_limit_bytes=VMEM_LIMIT)


def _sigmoid(x):
    return 0.5 * jnp.tanh(0.5 * x) + 0.5


def _silu(x):
    return x * _sigmoid(x)


def _softplus(x):
    return jnp.maximum(x, 0.0) + jnp.log(1.0 + jnp.exp(-jnp.abs(x)))


def _dot(a, b, dims=(((1,), (0,)), ((), ())), precision=None):
    return lax.dot_general(a, b, dims, precision=precision, preferred_element_type=F32)


NT = (((1,), (1,)), ((), ()))
TN = (((0,), (0,)), ((), ()))


def _bdot(a, b, dims=(((1,), (0,)), ((), ()))):
    return _dot(a.astype(BF16), b.astype(BF16), dims)


def _pick(dim, cands):
    for c in cands:
        if dim % c == 0:
            return c
    return dim


def matmul(a, b, *, name, ta=False, tb=False, out_dtype=F32, add=None):
    m, k = (a.shape[1], a.shape[0]) if ta else a.shape
    n = b.shape[0] if tb else b.shape[1]
    tn = _pick(n, (1024, 1408, 512, 256, 128))
    tm = _pick(m, (512, 1408, 256, 128)) if tn == n else _pick(m, (1024, 1408, 512, 256, 128))
    tk = _pick(k, (1024, 1408, 512, 256, 128))
    nk = k // tk
    a_spec = pl.BlockSpec((tk, tm), lambda i, j, kk: (kk, i)) if ta else pl.BlockSpec((tm, tk), lambda i, j, kk: (i, kk))
    b_spec = pl.BlockSpec((tn, tk), lambda i, j, kk: (j, kk)) if tb else pl.BlockSpec((tk, tn), lambda i, j, kk: (kk, j))
    dims = (((0 if ta else 1,), (1 if tb else 0,)), ((), ()))
    has_add = add is not None

    def body(*refs):
        a_ref, b_ref = refs[:2]
        add_ref = refs[2] if has_add else None
        o_ref = refs[3] if has_add else refs[2]
        acc = refs[-1] if nk > 1 else None
        kk = pl.program_id(2)

        def product():
            return _dot(a_ref[...].astype(BF16), b_ref[...].astype(BF16), dims)

        def finish(r):
            if has_add:
                r = r + add_ref[...].astype(F32)
            o_ref[...] = r.astype(o_ref.dtype)

        if nk == 1:
            finish(product())
            return

        @pl.when(kk == 0)
        def _():
            acc[...] = product()

        @pl.when((kk > 0) & (kk < nk - 1))
        def _():
            acc[...] += product()

        @pl.when(kk == nk - 1)
        def _():
            finish(acc[...] + product())

    in_specs = [a_spec, b_spec]
    args = [a, b]
    if has_add:
        in_specs.append(pl.BlockSpec((tm, tn), lambda i, j, kk: (i, j)))
        args.append(add)
    return pl.pallas_call(
        body, name=name, grid=(m // tm, n // tn, nk),
        in_specs=in_specs, out_specs=pl.BlockSpec((tm, tn), lambda i, j, kk: (i, j)),
        out_shape=SDS((m, n), out_dtype), scratch_shapes=[pltpu.VMEM((tm, tn), F32)] if nk > 1 else [],
        compiler_params=_cparams(("parallel", "parallel", "arbitrary")),
    )(*args)


def rowcall(name, fn, rows, params, row_outs, red_outs=(), tr=256):
    s = rows[0].shape[0]
    n_in = len(rows) + len(params)
    n_row = len(row_outs)

    def body(*refs):
        outs = fn(*[r[...].astype(F32) for r in refs[:n_in]])
        if not isinstance(outs, (tuple, list)):
            outs = (outs,)
        orefs = refs[n_in:]
        for r, o in zip(orefs[:n_row], outs[:n_row]):
            r[...] = o.astype(r.dtype)
        if red_outs:
            @pl.when(pl.program_id(0) == 0)
            def _():
                for r in orefs[n_row:]:
                    r[...] = jnp.zeros_like(r)
            for r, o in zip(orefs[n_row:], outs[n_row:]):
                r[...] += o.astype(F32)

    widths = [a[1] if isinstance(a, tuple) else a.shape[1] for a in rows]
    rows = [a[0] if isinstance(a, tuple) else a for a in rows]
    in_specs = [pl.BlockSpec((tr, wd), lambda i: (i, 0)) for wd in widths]
    in_specs += [pl.BlockSpec(p.shape, lambda i: (0, 0)) for p in params]
    out_specs = [pl.BlockSpec((tr, c), lambda i: (i, 0)) for c, _ in row_outs]
    out_specs += [pl.BlockSpec(shp, lambda i: (0, 0)) for shp in red_outs]
    out_shape = [SDS((s, c), dt) for c, dt in row_outs] + [SDS(shp, F32) for shp in red_outs]
    res = pl.pallas_call(
        body, name=name, grid=(s // tr,), in_specs=in_specs, out_specs=out_specs, out_shape=out_shape,
        compiler_params=_cparams(("arbitrary",) if red_outs else ("parallel",)),
    )(*rows, *params)
    return res


def _rms(x, w):
    return x * lax.rsqrt(jnp.mean(x * x, axis=-1, keepdims=True) + RMS_EPS) * w


def rms_fwd(h, w, name):
    return rowcall(name, _rms, [h], [w], [(D_MODEL, BF16)])[0]


def rms_bwd(h, du, dres, w, name):
    def fn(hb, dub, dresb, wb):
        _, vjp = jax.vjp(_rms, hb, wb)
        dh, dw = vjp(dub)
        return dh + dresb, dw
    return rowcall(name, fn, [h, du, dres], [w], [(D_MODEL, F32)], [(1, D_MODEL)])


def loss_head(h, target, w, name):
    def fn(hb, tb, wb):
        def f(hh, ww):
            err = _rms(hh, ww) - tb
            return 0.5 * jnp.sum(jnp.mean(err * err, axis=-1, keepdims=True), axis=0, keepdims=True)
        val, vjp = jax.vjp(f, hb, wb)
        dh, dw = vjp(jnp.ones((1, 1), F32))
        return dh, dw, jnp.broadcast_to(val, (1, LANES))
    return rowcall(name, fn, [h, target], [w], [(D_MODEL, F32)], [(1, D_MODEL), (1, LANES)])


def _gate(a, b, gs, ga):
    return _sigmoid(gs) * a + _sigmoid(ga) * b


def gate_fwd(a, b, gs, ga, name):
    return rowcall(name, _gate, [a, b, gs, ga], [], [(D_MODEL, BF16)])[0]


def gate_bwd(a, b, gs, ga, dm, name):
    def fn(ab, bb, gsb, gab, dmb):
        _, vjp = jax.vjp(_gate, ab, bb, gsb, gab)
        return vjp(dmb)
    return rowcall(name, fn, [a, b, gs, ga, dm], [], [(D_MODEL, BF16)] * 4)


def _swiglu(gu):
    return _silu(gu[:, :FFN_HIDDEN]) * gu[:, FFN_HIDDEN:]


def swiglu_fwd(gu, name):
    return rowcall(name, _swiglu, [gu], [], [(FFN_HIDDEN, BF16)])[0]


def swiglu_bwd(gu, dact, name):
    def fn(gub, db):
        _, vjp = jax.vjp(_swiglu, gub)
        return vjp(db.astype(F32))[0]
    return rowcall(name, fn, [gu, dact], [], [(2 * FFN_HIDDEN, BF16)])[0]


def _ssd_post(y, xs, z, dskip, normw):
    y = (y + dskip * xs) * _silu(z)
    gw = SSD_INNER // SSD_GROUPS
    parts = []
    for g in range(SSD_GROUPS):
        yg = y[:, g * gw:(g + 1) * gw]
        parts.append(yg * lax.rsqrt(jnp.mean(yg * yg, axis=-1, keepdims=True) + RMS_EPS))
    return jnp.concatenate(parts, axis=-1) * normw


def ssd_post_fwd(y, xc, z, dskip, normw, name):
    return rowcall(name, _ssd_post, [y, (xc, SSD_INNER), z], [dskip, normw], [(SSD_INNER, BF16)])[0]


def ssd_post_bwd(y, xc, z, dskip, normw, dyn, name):
    def fn(yb, xsb, zb, dynb, db, nb):
        _, vjp = jax.vjp(_ssd_post, yb, xsb, zb, db, nb)
        return vjp(dynb)
    return rowcall(name, fn, [y, (xc, SSD_INNER), z, dyn], [dskip, normw],
                   [(SSD_INNER, F32), (SSD_INNER, F32), (SSD_INNER, BF16)], [(1, SSD_INNER), (1, SSD_INNER)])


def _rope(t, cosf, sina, sinb):
    return t * cosf + pltpu.roll(t, LANES - ROPE_DIM // 2, 1) * sina + pltpu.roll(t, ROPE_DIM // 2, 1) * sinb


def rope_tables():
    half = ROPE_DIM // 2
    inv = ROPE_THETA ** (-jnp.arange(0, ROPE_DIM, 2, dtype=F32) / ROPE_DIM)
    ang = jnp.arange(SEQ, dtype=F32)[:, None] * inv[None, :]
    cos, sin = jnp.cos(ang), jnp.sin(ang)
    zeros = jnp.zeros((SEQ, LANES - ROPE_DIM), F32)
    z16 = jnp.zeros((SEQ, half), F32)
    cosf = jnp.concatenate([cos, cos, jnp.ones((SEQ, LANES - ROPE_DIM), F32)], axis=1)
    sina = jnp.concatenate([-sin, z16, zeros], axis=1)
    sinb = jnp.concatenate([z16, sin, zeros], axis=1)
    return cosf, sina, sinb


CONV_TC = 256


def _conv_pre(x, w, b, row):
    acc = x * w[SSD_CONV - 1:SSD_CONV, :] + b
    shifted = [x]
    for j in range(1, SSD_CONV):
        xs = jnp.where(row >= j, pltpu.roll(x, j, 0), 0.0)
        shifted.append(xs)
        acc = acc + xs * w[SSD_CONV - 1 - j:SSD_CONV - j, :]
    return acc, shifted


def conv_fwd(xbc, w, b, name):
    def body(x_ref, w_ref, b_ref, o_ref):
        row = lax.broadcasted_iota(jnp.int32, (SEQ, CONV_TC), 0)
        pre, _ = _conv_pre(x_ref[...].astype(F32), w_ref[...], b_ref[...], row)
        o_ref[...] = _silu(pre)
    return pl.pallas_call(
        body, name=name, grid=(SSD_CONV_CH // CONV_TC,),
        in_specs=[pl.BlockSpec((SEQ, CONV_TC), lambda i: (0, i)), pl.BlockSpec((SSD_CONV, CONV_TC), lambda i: (0, i)),
                  pl.BlockSpec((1, CONV_TC), lambda i: (0, i))],
        out_specs=pl.BlockSpec((SEQ, CONV_TC), lambda i: (0, i)),
        out_shape=SDS((SEQ, SSD_CONV_CH), F32), compiler_params=_cparams(("parallel",)),
    )(xbc, w, b)


def conv_bwd(xbc, w, b, dxc, name):
    def body(x_ref, w_ref, b_ref, dy_ref, dx_ref, dw_ref, db_ref):
        row = lax.broadcasted_iota(jnp.int32, (SEQ, CONV_TC), 0)
        wv = w_ref[...]
        pre, shifted = _conv_pre(x_ref[...].astype(F32), wv, b_ref[...], row)
        sg = _sigmoid(pre)
        ds = dy_ref[...] * (sg * (1.0 + pre * (1.0 - sg)))
        dx = ds * wv[SSD_CONV - 1:SSD_CONV, :]
        for j in range(1, SSD_CONV):
            dsj = jnp.where(row < SEQ - j, pltpu.roll(ds, SEQ - j, 0), 0.0)
            dx = dx + dsj * wv[SSD_CONV - 1 - j:SSD_CONV - j, :]
        dx_ref[...] = dx.astype(dx_ref.dtype)
        for j in range(SSD_CONV):
            dw_ref[SSD_CONV - 1 - j:SSD_CONV - j, :] = jnp.sum(ds * shifted[j], axis=0, keepdims=True)
        db_ref[...] = jnp.sum(ds, axis=0, keepdims=True)
    return pl.pallas_call(
        body, name=name, grid=(SSD_CONV_CH // CONV_TC,),
        in_specs=[pl.BlockSpec((SEQ, CONV_TC), lambda i: (0, i)), pl.BlockSpec((SSD_CONV, CONV_TC), lambda i: (0, i)),
                  pl.BlockSpec((1, CONV_TC), lambda i: (0, i)), pl.BlockSpec((SEQ, CONV_TC), lambda i: (0, i))],
        out_specs=[pl.BlockSpec((SEQ, CONV_TC), lambda i: (0, i)), pl.BlockSpec((SSD_CONV, CONV_TC), lambda i: (0, i)),
                   pl.BlockSpec((1, CONV_TC), lambda i: (0, i))],
        out_shape=[SDS((SEQ, SSD_CONV_CH), BF16), SDS((SSD_CONV, SSD_CONV_CH), F32), SDS((1, SSD_CONV_CH), F32)],
        compiler_params=_cparams(("parallel",)),
    )(xbc, w, b, dxc)


N_CHUNKS = SEQ // SSD_CHUNK
N_PAIRS = SSD_HEADS // 2
PAIRS_PER_GROUP = N_PAIRS // SSD_GROUPS
B_OFF = SSD_INNER
C_OFF = SSD_INNER + SSD_GROUPS * SSD_STATE


def _ssd_prefix(dtr, dtr_t, dtb, dtb_t, alog, alog_t):
    ln = SSD_CHUNK
    dt = _softplus(dtr + dtb)
    dt_t = _softplus(dtr_t + dtb_t)
    dta = dt * (-jnp.exp(alog))
    dta_t = dt_t * (-jnp.exp(alog_t))
    r = lax.broadcasted_iota(jnp.int32, (ln, ln), 0)
    c = lax.broadcasted_iota(jnp.int32, (ln, ln), 1)
    a_cum = _dot((r >= c).astype(F32), dta, precision=HIGHEST)
    a_cum_t = _dot(dta_t, (r <= c).astype(F32), precision=HIGHEST)
    a_last = jnp.sum(dta_t, axis=1, keepdims=True)
    return dt, a_cum, a_cum_t, a_last


def _bein(spec, a, b):
    return jnp.einsum(spec, a.astype(BF16), b.astype(BF16), preferred_element_type=F32)


SSD_GROUPS_PER_BATCH = 4


def _ssd_group(xs3, bgs, cgs, h3, dt, a_cum, a_cum_t, a_last, *, groups):
    ln = SSD_CHUNK
    lane = lax.broadcasted_iota(jnp.int32, (ln, LANES), 1)
    sub = lax.broadcasted_iota(jnp.int32, (LANES, SSD_STATE), 0)
    row = lax.broadcasted_iota(jnp.int32, (ln, ln), 0)
    col = lax.broadcasted_iota(jnp.int32, (ln, ln), 1)
    lo = lane < SSD_HEAD_DIM
    causal = row >= col
    m_lo, m_hi, dts, acs, lasts, cds, cg3, bg3 = [], [], [], [], [], [], [], []
    for g, bg, cg in zip(groups, bgs, cgs):
        cb = _bdot(cg, bg, NT)
        for j in range(PAIRS_PER_GROUP):
            e0 = 2 * (g * PAIRS_PER_GROUP + j)
            e1 = e0 + 1
            c0, c1 = a_cum[:, e0:e0 + 1], a_cum[:, e1:e1 + 1]
            r0, r1 = a_cum_t[e0:e0 + 1, :], a_cum_t[e1:e1 + 1, :]
            l0, l1 = a_last[e0:e0 + 1, :], a_last[e1:e1 + 1, :]
            m_lo.append(cb * jnp.exp(jnp.where(causal, c0 - r0, -jnp.inf)))
            m_hi.append(cb * jnp.exp(jnp.where(causal, c1 - r1, -jnp.inf)))
            dts.append(jnp.where(lo, dt[:, e0:e0 + 1], dt[:, e1:e1 + 1]))
            acs.append(jnp.where(lo, c0, c1))
            lasts.append(jnp.where(lo, l0, l1))
            cds.append(jnp.exp(jnp.where(sub < SSD_HEAD_DIM, l0, l1)))
            cg3.append(cg)
            bg3.append(bg)
    xd = xs3 * jnp.stack(dts)
    acum = jnp.stack(acs)
    y = (_bein("pls,psq->plq", jnp.stack(m_lo), jnp.where(lo[None], xd, 0.0))
         + _bein("pls,psq->plq", jnp.stack(m_hi), jnp.where(lo[None], 0.0, xd)))
    y = y + _bein("pln,pqn->plq", jnp.stack(cg3), h3) * jnp.exp(acum)
    st = _bein("plq,pln->pqn", xd * jnp.exp(jnp.stack(lasts) - acum), jnp.stack(bg3))
    h_out = h3 * jnp.stack(cds) + st
    return y, h_out


def _group_slabs(groups):
    pairs = [g * PAIRS_PER_GROUP + j for g in groups for j in range(PAIRS_PER_GROUP)]
    return [slice(p * LANES, (p + 1) * LANES) for p in pairs]


def _group_batches():
    return [tuple(range(g, g + SSD_GROUPS_PER_BATCH)) for g in range(0, SSD_GROUPS, SSD_GROUPS_PER_BATCH)]


def _bc_of(xc_ref, g):
    return (xc_ref[:, B_OFF + g * SSD_STATE:B_OFF + (g + 1) * SSD_STATE],
            xc_ref[:, C_OFF + g * SSD_STATE:C_OFF + (g + 1) * SSD_STATE])


def _ssd_in_specs(chunk_of):
    return [
        pl.BlockSpec((SSD_CHUNK, SSD_CONV_CH), lambda i: (chunk_of(i), 0)),
        pl.BlockSpec((SSD_CHUNK, HPAD), lambda i: (chunk_of(i), 0)),
        pl.BlockSpec((HPAD, SSD_CHUNK), lambda i: (0, chunk_of(i))),
        pl.BlockSpec((1, HPAD), lambda i: (0, 0)), pl.BlockSpec((HPAD, 1), lambda i: (0, 0)),
        pl.BlockSpec((1, HPAD), lambda i: (0, 0)), pl.BlockSpec((HPAD, 1), lambda i: (0, 0)),
    ]


def ssd_fwd(xc, dtr, dtr_t, dtb, dtb_t, alog, alog_t, name):
    def body(xc_ref, dtr_ref, dtrt_ref, dtb_ref, dtbt_ref, al_ref, alt_ref, y_ref, hs_ref, h_scr):
        @pl.when(pl.program_id(0) == 0)
        def _():
            h_scr[...] = jnp.zeros_like(h_scr)

        hs_ref[0] = h_scr[...]
        dt, a_cum, a_cum_t, a_last = _ssd_prefix(dtr_ref[...], dtrt_ref[...], dtb_ref[...], dtbt_ref[...],
                                                  al_ref[...], alt_ref[...])
        for groups in _group_batches():
            slabs = _group_slabs(groups)
            bgs, cgs = zip(*[_bc_of(xc_ref, g) for g in groups])
            xs3 = jnp.stack([xc_ref[:, sl] for sl in slabs])
            h3 = jnp.stack([h_scr[sl, :] for sl in slabs])
            y3, h3_out = _ssd_group(xs3, bgs, cgs, h3, dt, a_cum, a_cum_t, a_last, groups=groups)
            for j, sl in enumerate(slabs):
                y_ref[:, sl] = y3[j]
                h_scr[sl, :] = h3_out[j]

    return pl.pallas_call(
        body, name=name, grid=(N_CHUNKS,), in_specs=_ssd_in_specs(lambda i: i),
        out_specs=[pl.BlockSpec((SSD_CHUNK, SSD_INNER), lambda i: (i, 0)),
                   pl.BlockSpec((1, SSD_INNER, SSD_STATE), lambda i: (i, 0, 0))],
        out_shape=[SDS((SEQ, SSD_INNER), F32), SDS((N_CHUNKS, SSD_INNER, SSD_STATE), F32)],
        scratch_shapes=[pltpu.VMEM((SSD_INNER, SSD_STATE), F32)],
        compiler_params=_cparams(("arbitrary",)),
    )(xc, dtr, dtr_t, dtb, dtb_t, alog, alog_t)


def ssd_bwd(xc, dtr, dtr_t, dtb, dtb_t, alog, alog_t, hs, dy, dxs_extra, name):
    rev = lambda i: N_CHUNKS - 1 - i

    def body(xc_ref, dtr_ref, dtrt_ref, dtb_ref, dtbt_ref, al_ref, alt_ref, hs_ref, dy_ref, dxe_ref,
             dxc_ref, ddtr_ref, ddtrt_ref, ddtb_ref, ddtbt_ref, dal_ref, dalt_ref, dh_scr):
        @pl.when(pl.program_id(0) == 0)
        def _():
            dh_scr[...] = jnp.zeros_like(dh_scr)
            for r in (ddtb_ref, ddtbt_ref, dal_ref, dalt_ref):
                r[...] = jnp.zeros_like(r)

        prefix_in = (dtr_ref[...], dtrt_ref[...], dtb_ref[...], dtbt_ref[...], al_ref[...], alt_ref[...])
        (dt, a_cum, a_cum_t, a_last), prefix_vjp = jax.vjp(_ssd_prefix, *prefix_in)
        d_dt = jnp.zeros_like(dt)
        d_acum = jnp.zeros_like(a_cum)
        d_acum_t = jnp.zeros_like(a_cum_t)
        d_alast = jnp.zeros_like(a_last)
        for groups in _group_batches():
            slabs = _group_slabs(groups)
            bgs, cgs = zip(*[_bc_of(xc_ref, g) for g in groups])
            xs3 = jnp.stack([xc_ref[:, sl] for sl in slabs])
            h3 = jnp.stack([hs_ref[0, sl, :] for sl in slabs])
            _, vjp = jax.vjp(functools.partial(_ssd_group, groups=groups), xs3, bgs, cgs, h3, dt, a_cum, a_cum_t, a_last)
            dy3 = jnp.stack([dy_ref[:, sl] for sl in slabs])
            dh3 = jnp.stack([dh_scr[sl, :] for sl in slabs])
            dxs3, d_bgs, d_cgs, dh3_in, ddt, dac, dact, dal = vjp((dy3, dh3))
            for j, sl in enumerate(slabs):
                dxc_ref[:, sl] = dxs3[j] + dxe_ref[:, sl]
                dh_scr[sl, :] = dh3_in[j]
            d_dt, d_acum, d_acum_t, d_alast = d_dt + ddt, d_acum + dac, d_acum_t + dact, d_alast + dal
            for g, d_bg, d_cg in zip(groups, d_bgs, d_cgs):
                dxc_ref[:, B_OFF + g * SSD_STATE:B_OFF + (g + 1) * SSD_STATE] = d_bg
                dxc_ref[:, C_OFF + g * SSD_STATE:C_OFF + (g + 1) * SSD_STATE] = d_cg
        g_dtr, g_dtrt, g_dtb, g_dtbt, g_al, g_alt = prefix_vjp((d_dt, d_acum, d_acum_t, d_alast))
        ddtr_ref[...] = g_dtr
        ddtrt_ref[...] = g_dtrt
        ddtb_ref[...] += g_dtb
        ddtbt_ref[...] += g_dtbt
        dal_ref[...] += g_al
        dalt_ref[...] += g_alt

    in_specs = _ssd_in_specs(rev) + [
        pl.BlockSpec((1, SSD_INNER, SSD_STATE), lambda i: (rev(i), 0, 0)),
        pl.BlockSpec((SSD_CHUNK, SSD_INNER), lambda i: (rev(i), 0)),
        pl.BlockSpec((SSD_CHUNK, SSD_INNER), lambda i: (rev(i), 0)),
    ]
    out_specs = [
        pl.BlockSpec((SSD_CHUNK, SSD_CONV_CH), lambda i: (rev(i), 0)),
        pl.BlockSpec((SSD_CHUNK, HPAD), lambda i: (rev(i), 0)),
        pl.BlockSpec((HPAD, SSD_CHUNK), lambda i: (0, rev(i))),
        pl.BlockSpec((1, HPAD), lambda i: (0, 0)), pl.BlockSpec((HPAD, 1), lambda i: (0, 0)),
        pl.BlockSpec((1, HPAD), lambda i: (0, 0)), pl.BlockSpec((HPAD, 1), lambda i: (0, 0)),
    ]
    out_shape = [SDS((SEQ, SSD_CONV_CH), F32), SDS((SEQ, HPAD), F32), SDS((HPAD, SEQ), F32),
                 SDS((1, HPAD), F32), SDS((HPAD, 1), F32), SDS((1, HPAD), F32), SDS((HPAD, 1), F32)]
    return pl.pallas_call(
        body, name=name, grid=(N_CHUNKS,), in_specs=in_specs, out_specs=out_specs, out_shape=out_shape,
        scratch_shapes=[pltpu.VMEM((SSD_INNER, SSD_STATE), F32)],
        compiler_params=_cparams(("arbitrary",)),
    )(xc, dtr, dtr_t, dtb, dtb_t, alog, alog_t, hs, dy, dxs_extra)


ATTN_SCALE = ATTN_HEAD_DIM ** -0.5


UNITS_PER_PATTERN = SEQ // ATTN_BLOCK
ATTN_BATCH = 8


def _for_unit_batches(batch):
    for g, d in enumerate(ATTN_DILATIONS):
        nb = UNITS_PER_PATTERN // d
        span = d * ATTN_BLOCK

        def trip(t, carry, g=g, d=d, nb=nb, span=span):
            units = []
            for j in range(ATTN_BATCH):
                i = t * ATTN_BATCH + j
                r = i >> (nb.bit_length() - 1)
                n = i & (nb - 1)
                start = r + n * span
                prev = jnp.where(n > 0, start - span, start)
                units.append((pl.ds(start, ATTN_BLOCK, stride=d), pl.ds(prev, ATTN_BLOCK, stride=d), n > 0))
            batch(g, units)
            return carry
        lax.fori_loop(0, UNITS_PER_PATTERN // ATTN_BATCH, trip, 0)


def _unit_operands(units, q_scr, k_scr, v_scr):
    def pair(scr, rows, prows):
        return jnp.concatenate([scr[prows, :], scr[rows, :]], axis=0)
    qb = jnp.stack([q_scr[rows, :] for rows, _, _ in units]).astype(BF16)
    kb = jnp.stack([pair(k_scr, rows, prows) for rows, prows, _ in units]).astype(BF16)
    vb = jnp.stack([pair(v_scr, rows, prows) for rows, prows, _ in units]).astype(BF16)
    return qb, kb, vb


def _unit_scores(qb, kb, units):
    s = jnp.einsum("bqd,bkd->bqk", qb, kb, preferred_element_type=F32) * ATTN_SCALE
    qi = lax.broadcasted_iota(jnp.int32, (ATTN_BLOCK, 2 * ATTN_BLOCK), 0)
    kj = lax.broadcasted_iota(jnp.int32, (ATTN_BLOCK, 2 * ATTN_BLOCK), 1)
    own = (kj >= ATTN_BLOCK) & (kj - ATTN_BLOCK <= qi)
    before = (kj < ATTN_BLOCK) & (kj >= qi)
    keep = jnp.stack([own | (before & has_prev) for _, _, has_prev in units])
    return jnp.where(keep, s, -jnp.inf)


def _head_specs(n_q_groups):
    blk = (SEQ, ATTN_HEAD_DIM)
    q_specs = [pl.BlockSpec(blk, functools.partial(lambda h, g: (0, g * ATTN_KV_HEADS + h), g=g)) for g in range(n_q_groups)]
    head = pl.BlockSpec(blk, lambda h: (0, h))
    table = pl.BlockSpec(blk, lambda h: (0, 0))
    return q_specs, head, table


def attn_fwd(q, k, v, tabs, name):
    q_specs, head, table = _head_specs(ATTN_N_PAT)

    def body(q0_ref, q1_ref, q2_ref, k_ref, v_ref, c_ref, sa_ref, sb_ref, y_ref, lse_ref, *scr):
        qs, og, ls, ks, vs = scr[0:3], scr[3:6], scr[6:9], scr[9], scr[10]
        c, sa, sb = c_ref[...], sa_ref[...], sb_ref[...]
        for g, q_ref in enumerate((q0_ref, q1_ref, q2_ref)):
            qs[g][...] = _rope(q_ref[...].astype(F32), c, sa, sb)
        ks[...] = _rope(k_ref[...].astype(F32), c, sa, sb)
        vs[...] = v_ref[...].astype(F32)

        def batch(g, units):
            qb, kb, vb = _unit_operands(units, qs[g], ks, vs)
            s = _unit_scores(qb, kb, units)
            m = jnp.max(s, axis=2, keepdims=True)
            p = jnp.exp(s - m)
            l = jnp.sum(p, axis=2, keepdims=True)
            o = jnp.einsum("bqk,bkd->bqd", p.astype(BF16), vb, preferred_element_type=F32) / l
            lse_b = m + jnp.log(l)
            for j, (rows, _, _) in enumerate(units):
                og[g][rows, :] = o[j]
                ls[g][rows, :] = jnp.broadcast_to(lse_b[j], (ATTN_BLOCK, LANES))

        _for_unit_batches(batch)
        l0, l1, l2 = ls[0][...], ls[1][...], ls[2][...]
        m = jnp.maximum(jnp.maximum(l0, l1), l2)
        e0, e1, e2 = jnp.exp(l0 - m), jnp.exp(l1 - m), jnp.exp(l2 - m)
        den = e0 + e1 + e2
        y_ref[...] = ((e0 * og[0][...] + e1 * og[1][...] + e2 * og[2][...]) / den).astype(y_ref.dtype)
        lse_ref[...] = m + jnp.log(den)

    blk = (SEQ, ATTN_HEAD_DIM)
    return pl.pallas_call(
        body, name=name, grid=(ATTN_KV_HEADS,), in_specs=[*q_specs, head, head, table, table, table],
        out_specs=[head, head], out_shape=[SDS((SEQ, ATTN_OUT), BF16), SDS((SEQ, ATTN_OUT), F32)],
        scratch_shapes=[pltpu.VMEM(blk, F32)] * (3 * ATTN_N_PAT + 2),
        compiler_params=_cparams(("parallel",)),
    )(q, q, q, k, v, *tabs)


def attn_bwd(q, k, v, tabs, y, lse, dy, name):
    q_specs, head, table = _head_specs(ATTN_N_PAT)

    def body(q0_ref, q1_ref, q2_ref, k_ref, v_ref, c_ref, sa_ref, sb_ref, y_ref, lse_ref, dy_ref,
             dq0_ref, dq1_ref, dq2_ref, dk_ref, dv_ref, *scr):
        qs, dqs, ks, dks, dd, dvs, vs = scr[0:3], scr[3:6], scr[6], scr[7], scr[8], scr[9], scr[10]
        c, sa, sb = c_ref[...], sa_ref[...], sb_ref[...]
        for g, q_ref in enumerate((q0_ref, q1_ref, q2_ref)):
            qs[g][...] = _rope(q_ref[...].astype(F32), c, sa, sb)
        ks[...] = _rope(k_ref[...].astype(F32), c, sa, sb)
        vs[...] = v_ref[...].astype(F32)
        dks[...] = jnp.zeros_like(dks)
        dvs[...] = jnp.zeros_like(dvs)
        dyv = dy_ref[...]
        dd[...] = jnp.broadcast_to(jnp.sum(dyv * y_ref[...].astype(F32), axis=1, keepdims=True), dd.shape)

        def batch(g, units):
            qb, kb, vb = _unit_operands(units, qs[g], ks, vs)
            dob = jnp.stack([dy_ref[rows, :] for rows, _, _ in units]).astype(BF16)
            lse_b = jnp.stack([lse_ref[rows, :][:, 0:1] for rows, _, _ in units])
            dsum_b = jnp.stack([dd[rows, :][:, 0:1] for rows, _, _ in units])
            p = jnp.exp(_unit_scores(qb, kb, units) - lse_b)
            dp = jnp.einsum("bqd,bkd->bqk", dob, vb, preferred_element_type=F32)
            ds = (p * (dp - dsum_b) * ATTN_SCALE).astype(BF16)
            dq = jnp.einsum("bqk,bkd->bqd", ds, kb, preferred_element_type=F32)
            dk = jnp.einsum("bqk,bqd->bkd", ds, qb, preferred_element_type=F32)
            dv = jnp.einsum("bqk,bqd->bkd", p.astype(BF16), dob, preferred_element_type=F32)
            for j, (rows, prows, _) in enumerate(units):
                dqs[g][rows, :] = dq[j]
                dks[prows, :] += dk[j, :ATTN_BLOCK]
                dks[rows, :] += dk[j, ATTN_BLOCK:]
                dvs[prows, :] += dv[j, :ATTN_BLOCK]
                dvs[rows, :] += dv[j, ATTN_BLOCK:]

        _for_unit_batches(batch)
        for g, dq_ref in enumerate((dq0_ref, dq1_ref, dq2_ref)):
            dq_ref[...] = _rope(dqs[g][...], c, -sa, -sb).astype(dq_ref.dtype)
        dk_ref[...] = _rope(dks[...], c, -sa, -sb).astype(dk_ref.dtype)
        dv_ref[...] = dvs[...].astype(dv_ref.dtype)

    blk = (SEQ, ATTN_HEAD_DIM)
    out = SDS((SEQ, ATTN_OUT), BF16)
    return pl.pallas_call(
        body, name=name, grid=(ATTN_KV_HEADS,), in_specs=[*q_specs, head, head, table, table, table, head, head, head],
        out_specs=[head] * 5, out_shape=[out] * 5,
        scratch_shapes=[pltpu.VMEM(blk, F32)] * (2 * ATTN_N_PAT + 5),
        compiler_params=_cparams(("parallel",)),
    )(q, q, q, k, v, *tabs, y, lse, dy)


def layer_fwd(h, getw, small, tabs, li):
    n = f"l{li}_"
    sv = {}
    w = dict(getw(0, h))
    u = rms_fwd(h, small["norm_mix"], n + "rms_mix")
    z = matmul(u, w["w_z"], name=n + "mm_z", tb=True, out_dtype=BF16)
    xbc = matmul(u, w["w_xbc"], name=n + "mm_xbc", tb=True, out_dtype=BF16)
    dtr = matmul(u, w["w_dt"], name=n + "mm_dt", tb=True)
    q = matmul(u, w["w_q"], name=n + "mm_q", tb=True, out_dtype=BF16)
    k = matmul(u, w["w_k"], name=n + "mm_k", tb=True, out_dtype=BF16)
    v = matmul(u, w["w_v"], name=n + "mm_v", tb=True, out_dtype=BF16)
    gs = matmul(u, w["w_gs"], name=n + "mm_gs", tb=True, out_dtype=BF16)
    ga = matmul(u, w["w_ga"], name=n + "mm_ga", tb=True, out_dtype=BF16)
    xc = conv_fwd(xbc, w["conv_w"], small["conv_b"], n + "conv")
    dtr_t = dtr.T
    y_ssd, hs = ssd_fwd(xc, dtr, dtr_t, small["dt_bias"], small["dt_bias"].T, small["a_log"], small["a_log"].T, n + "ssd")
    yn = ssd_post_fwd(y_ssd, xc, z, small["d_skip_x"], small["ssd_norm"], n + "ssd_post")
    y_attn, lse = attn_fwd(q, k, v, tabs, n + "attn")
    w.update(getw(1, y_ssd))
    a = matmul(yn, w["w_ssd_branch"], name=n + "mm_a", out_dtype=BF16)
    b = matmul(y_attn, w["w_attn_branch"], name=n + "mm_b", out_dtype=BF16)
    merged = gate_fwd(a, b, gs, ga, n + "gate")
    h1 = matmul(merged, w["w_out"], name=n + "mm_o", add=h)
    w.update(getw(2, h1))
    u2 = rms_fwd(h1, small["norm_ffn"], n + "rms_ffn")
    gu = matmul(u2, w["w_gate_up"], name=n + "mm_gu", tb=True, out_dtype=BF16)
    act = swiglu_fwd(gu, n + "swiglu")
    h2 = matmul(act, w["w_down"], name=n + "mm_down", add=h1)
    sv.update(h=h, u=u, z=z, xbc=xbc, dtr=dtr, dtr_t=dtr_t, gs=gs, ga=ga, xc=xc, y_ssd=y_ssd, hs=hs, yn=yn,
              q=q, k=k, v=v, y_attn=y_attn, lse=lse, a=a, b=b, merged=merged, h1=h1, u2=u2, gu=gu, act=act, w=w)
    return h2, sv


def layer_bwd(dh, sv, small, tabs, li, emit):
    n = f"l{li}_b_"
    w = sv["w"]
    gw, gsm = {}, {}
    dact = matmul(dh, w["w_down"], name=n + "mm_dact", tb=True, out_dtype=BF16)
    gw["w_down"] = matmul(sv["act"], dh, name=n + "mm_dwdown", ta=True, out_dtype=BF16)
    dgu = swiglu_bwd(sv["gu"], dact, n + "swiglu")
    gw["w_gate_up"] = matmul(dgu, sv["u2"], name=n + "mm_dwgu", ta=True, out_dtype=BF16)
    tok = emit(2, gw)
    du2 = matmul(dgu, w["w_gate_up"], name=n + "mm_du2")
    dh1, gsm["norm_ffn"] = rms_bwd(sv["h1"], du2, dh, small["norm_ffn"] + tok, n + "rms_ffn")
    dmerged = matmul(dh1, w["w_out"], name=n + "mm_dmerged", tb=True)
    gw["w_out"] = matmul(sv["merged"], dh1, name=n + "mm_dwo", ta=True, out_dtype=BF16)
    da, db, dgs, dga = gate_bwd(sv["a"], sv["b"], sv["gs"], sv["ga"], dmerged, n + "gate")
    gw["w_ssd_branch"] = matmul(sv["yn"], da, name=n + "mm_dwa", ta=True, out_dtype=BF16)
    gw["w_attn_branch"] = matmul(sv["y_attn"], db, name=n + "mm_dwb", ta=True, out_dtype=BF16)
    tok = emit(1, gw)
    dyn = matmul(da, w["w_ssd_branch"], name=n + "mm_dyn", tb=True)
    dyattn = matmul(db, w["w_attn_branch"], name=n + "mm_dyattn", tb=True)
    dy_ssd, dxs_extra, dz, gsm["d_skip_x"], gsm["ssd_norm"] = ssd_post_bwd(
        sv["y_ssd"], sv["xc"], sv["z"], small["d_skip_x"] + tok, small["ssd_norm"], dyn, n + "ssd_post")
    dxc, ddtr, ddtr_t, ddtb, ddtb_t, dal, dal_t = ssd_bwd(
        sv["xc"], sv["dtr"], sv["dtr_t"], small["dt_bias"], small["dt_bias"].T, small["a_log"], small["a_log"].T,
        sv["hs"], dy_ssd, dxs_extra, n + "ssd")
    ddtr = (ddtr + ddtr_t.T).astype(BF16)
    gsm["dt_bias"] = ddtb + ddtb_t.T
    gsm["a_log"] = dal + dal_t.T
    dxbc, gw["conv_w"], gsm["conv_b"] = conv_bwd(sv["xbc"], w["conv_w"], small["conv_b"], dxc, n + "conv")
    dq0, dq1, dq2, dk, dv = attn_bwd(sv["q"], sv["k"], sv["v"], tabs, sv["y_attn"], sv["lse"], dyattn, n + "attn")
    u = sv["u"]
    segs = [("w_z", dz), ("w_xbc", dxbc), ("w_dt", ddtr), ("w_q0", dq0), ("w_q1", dq1), ("w_q2", dq2),
            ("w_k", dk), ("w_v", dv), ("w_gs", dgs), ("w_ga", dga)]
    gin = [matmul(dseg, u, name=n + "mm_d" + key, ta=True, out_dtype=BF16) for key, dseg in segs]
    gin[2] = gin[2][:SSD_HEADS]
    gw["w_in"] = jnp.concatenate(gin, axis=0)
    tok = emit(0, gw)
    du = jnp.zeros((SEQ, D_MODEL), F32) + tok
    for key, dseg in segs:
        du = matmul(dseg, w[key], name=n + "mm_du_" + key, add=du)
    dh0, gsm["norm_mix"] = rms_bwd(sv["h"], du, dh1, small["norm_mix"] + tok, n + "rms_mix")
    return dh0, gsm


def _my_place():
    return lax.axis_index("x"), lax.axis_index("y"), lax.axis_index("c")


def _flip(place, k):
    x, y, c = place
    return (1 - x if k & 4 else x, 1 - y if k & 2 else y, 1 - c if k & 1 else c)


def _index(place):
    return 4 * place[0] + 2 * place[1] + place[2]


ANY = pl.BlockSpec(memory_space=pl.ANY)
CHIP_FLIPS = (4, 2, 6)


def all_gather(xs, name):
    na = len(xs)

    def body(*refs):
        x_refs, o_refs = refs[:na], refs[na:2 * na]
        send_sems, recv_sems, local_sems = refs[2 * na:]
        me = _my_place()
        sibling = _flip(me, 1)
        chips = [_flip(me, f) for f in CHIP_FLIPS]

        def copy(a, kk, block, to, src=None):
            dst = o_refs[a].at[_index(block)]
            return pltpu.make_async_remote_copy(
                src_ref=dst if src is None else src, dst_ref=dst, send_sem=send_sems.at[a, kk],
                recv_sem=recv_sems.at[a, kk], device_id=to, device_id_type=MESH)

        mine = [pltpu.make_async_copy(x_refs[a], o_refs[a].at[_index(me)], local_sems.at[a]) for a in range(na)]
        for cp in mine:
            cp.start()
        first = []
        for j, chip in enumerate(chips):
            first += [copy(a, 1 + j, me, chip, src=x_refs[a]) for a in range(na)]
        first += [copy(a, 0, me, sibling, src=x_refs[a]) for a in range(na)]
        for cp in first:
            cp.start()
        passed = []
        for j, chip in enumerate(chips):
            for a in range(na):
                copy(a, 1 + j, chip, me).wait_recv()
                cp = copy(a, 4 + j, chip, sibling)
                cp.start()
                passed.append(cp)
        for a in range(na):
            copy(a, 0, sibling, me).wait_recv()
        for j, chip in enumerate(chips):
            for a in range(na):
                copy(a, 4 + j, _flip(chip, 1), me).wait_recv()
        for cp in first + passed:
            cp.wait_send()
        for cp in mine:
            cp.wait()

    return pl.pallas_call(
        body, name=name, in_specs=[ANY] * na, out_specs=[ANY] * na,
        out_shape=[SDS((N_DEV,) + t.shape, t.dtype) for t in xs],
        scratch_shapes=[pltpu.SemaphoreType.DMA((na, N_DEV - 1)), pltpu.SemaphoreType.DMA((na, N_DEV - 1)),
                        pltpu.SemaphoreType.DMA((na,))],
    )(*xs)


HBM = pl.BlockSpec(memory_space=pltpu.HBM)
SEM = pl.BlockSpec(memory_space=pltpu.SEMAPHORE)
EFFECT = pltpu.SideEffectType.DATAFLOW_SIDE_EFFECTING
N_PEERS = N_DEV - 1


def _split_copy(src_ref, land_ref, send_sem, recv_sem, me, kk, scatter, landed_from_peer):
    peer = _flip(me, kk)
    src = src_ref.at[_index(peer)] if scatter else src_ref
    dst = land_ref.at[_index(peer if landed_from_peer else me)]
    return pltpu.make_async_remote_copy(src_ref=src, dst_ref=dst, send_sem=send_sem, recv_sem=recv_sem,
                                        device_id=peer, device_id_type=MESH)


def exchange_start(srcs, lands, group_sizes, scatter, name):
    na, ng = len(srcs), len(group_sizes)

    def body(*refs):
        s_refs, l_refs = refs[:na], refs[na:2 * na]
        sems = refs[2 * na:2 * na + 2 * ng]
        token = refs[-1]
        me = _my_place()
        a = 0
        for gi, gsz in enumerate(group_sizes):
            for j in range(gsz):
                for kk in range(1, N_DEV):
                    slot = j * N_PEERS + kk - 1
                    _split_copy(s_refs[a], l_refs[a], sems[2 * gi].at[slot], sems[2 * gi + 1].at[slot],
                                me, kk, scatter, False).start()
                a += 1
        token[...] = jnp.zeros_like(token)

    sem_shapes = []
    for gsz in group_sizes:
        sem_shapes += [pltpu.SemaphoreType.DMA((gsz * N_PEERS,))] * 2
    ins = [pltpu.with_memory_space_constraint(t, pltpu.HBM) for t in (*srcs, *lands)]
    res = pl.pallas_call(
        body, name=name, in_specs=[HBM] * (2 * na),
        out_specs=[SEM] * (2 * ng) + [HBM] * (2 * na) + [pl.BlockSpec(memory_space=pltpu.VMEM)],
        out_shape=sem_shapes + [pltpu.HBM(t.shape, t.dtype) for t in ins] + [SDS((8, LANES), F32)],
        input_output_aliases={i: 2 * ng + i for i in range(2 * na)},
        compiler_params=pltpu.CompilerParams(has_side_effects=EFFECT),
    )(*ins)
    sems = [(res[2 * gi], res[2 * gi + 1]) for gi in range(ng)]
    thru = res[2 * ng:2 * ng + 2 * na]
    return sems, thru[:na], thru[na:], res[-1]


def exchange_wait(srcs, lands, sems, after, scatter, name):
    n = len(srcs)

    def body(*refs):
        s_refs, l_refs = refs[:n], refs[n:2 * n]
        send_sems, recv_sems = refs[2 * n], refs[2 * n + 1]
        me = _my_place()
        for j in range(n):
            for kk in range(1, N_DEV):
                slot = j * N_PEERS + kk - 1
                cp = _split_copy(s_refs[j], l_refs[j], send_sems.at[slot], recv_sems.at[slot], me, kk, scatter, True)
                cp.wait_send()
                cp.wait_recv()

    res = pl.pallas_call(
        body, name=name, in_specs=[HBM] * (2 * n) + [SEM, SEM, ANY], out_specs=[HBM] * (2 * n),
        out_shape=[pltpu.HBM(t.shape, t.dtype) for t in (*srcs, *lands)],
        input_output_aliases={i: i for i in range(2 * n)},
        compiler_params=pltpu.CompilerParams(has_side_effects=EFFECT),
    )(*srcs, *lands, sems[0], sems[1], after)
    return res[n:]


def landing_zone(block, me_index):
    land = lax.empty((N_DEV,) + block.shape, block.dtype)
    return lax.dynamic_update_slice(land, block[None], (me_index,) + (0,) * block.ndim)


def sum_parts(parts, name, row_major_3d=False):
    _, r, c = parts.shape
    tc = _pick(c, (256, 128))

    def body(p_ref, o_ref):
        acc = p_ref[0].astype(F32)
        for i in range(1, N_DEV):
            acc = acc + p_ref[i].astype(F32)
        if row_major_3d:
            o_ref[:, 0, :] = acc
        else:
            o_ref[...] = acc

    out_spec = pl.BlockSpec((r, 1, tc), lambda i: (0, 0, i)) if row_major_3d else pl.BlockSpec((r, tc), lambda i: (0, i))
    return pl.pallas_call(
        body, name=name, grid=(c // tc,), in_specs=[pl.BlockSpec((N_DEV, r, tc), lambda i: (0, 0, i))],
        out_specs=out_spec, out_shape=SDS((r, 1, c) if row_major_3d else (r, c), F32),
        compiler_params=_cparams(("parallel",)),
    )(parts)


ADAMW_BLOCK_BYTES = 2 * 1024 * 1024


def adamw(w, g, m, v, name):
    shape = w.shape
    lay, rows, cols = ((1, 1) + shape)[-3:]
    tr = _pick(rows, (256, 128))
    tc = cols if tr * cols * 4 <= ADAMW_BLOCK_BYTES else _pick(cols, (256, 128))
    c1 = 1.0 / (1.0 - ADAM_B1 ** ADAM_STEP)
    c2 = 1.0 / (1.0 - ADAM_B2 ** ADAM_STEP)

    def body(w_ref, g_ref, m_ref, v_ref, d_ref, nm_ref, nv_ref):
        gg = g_ref[...]
        nm = ADAM_B1 * m_ref[...] + (1.0 - ADAM_B1) * gg
        nv = ADAM_B2 * v_ref[...] + (1.0 - ADAM_B2) * (gg * gg)
        d_ref[...] = -ADAM_LR * ((nm * c1) / (jnp.sqrt(nv * c2) + ADAM_EPS) + ADAM_WD * w_ref[...])
        nm_ref[...] = nm
        nv_ref[...] = nv

    spec = pl.BlockSpec((1, tr, tc), lambda l, i, j: (l, i, j))
    outs = pl.pallas_call(
        body, name=name, grid=(lay, rows // tr, cols // tc), in_specs=[spec] * 4, out_specs=[spec] * 3,
        out_shape=[SDS((lay, rows, cols), F32)] * 3, compiler_params=_cparams(("parallel",) * 3),
    )(*[t.reshape(lay, rows, cols) for t in (w, g, m, v)])
    return [o.reshape(shape) for o in outs]


def adamw_layer_inner(w, gs, m, v, name):
    rows, lay, cols = w.shape
    tr = _pick(rows, (256, 220, 128))
    c1 = 1.0 / (1.0 - ADAM_B1 ** ADAM_STEP)
    c2 = 1.0 / (1.0 - ADAM_B2 ** ADAM_STEP)

    def body(*refs):
        w_ref, m_ref, v_ref = refs[:3]
        g_refs = refs[3:3 + lay]
        go_ref, d_ref, nm_ref, nv_ref = refs[3 + lay:]
        for l, g_ref in enumerate(g_refs):
            gg = g_ref[:, 0, :]
            nm = ADAM_B1 * m_ref[:, l, :] + (1.0 - ADAM_B1) * gg
            nv = ADAM_B2 * v_ref[:, l, :] + (1.0 - ADAM_B2) * (gg * gg)
            d_ref[:, l, :] = -ADAM_LR * ((nm * c1) / (jnp.sqrt(nv * c2) + ADAM_EPS) + ADAM_WD * w_ref[:, l, :])
            go_ref[:, l, :] = gg
            nm_ref[:, l, :] = nm
            nv_ref[:, l, :] = nv

    inner = pl.BlockSpec((tr, lay, cols), lambda i: (i, 0, 0))
    plain = pl.BlockSpec((tr, 1, cols), lambda i: (i, 0, 0))
    return pl.pallas_call(
        body, name=name, grid=(rows // tr,), in_specs=[inner] * 3 + [plain] * lay, out_specs=[inner] * 4,
        out_shape=[SDS((rows, lay, cols), F32)] * 4, compiler_params=_cparams(("parallel",)),
    )(w, m, v, *gs)


BIG = ("w_in", "conv_w", "w_ssd_branch", "w_attn_branch", "w_out", "w_gate_up", "w_down")
TRANSPOSED = ("w_in", "w_gate_up")
SMALL = ("norm_mix", "conv_b", "dt_bias", "a_log", "d_skip", "ssd_norm", "norm_ffn")
SMALL_SIZE = {"norm_mix": 1024, "conv_b": 3072, "dt_bias": 32, "a_log": 32, "d_skip": 32, "ssd_norm": 2048, "norm_ffn": 1024}
FLAT_W = 512
SMALL_TOTAL = DEPTH * sum(SMALL_SIZE.values()) + D_MODEL + LANES
SMALL_ROWS = 32
assert SMALL_ROWS * FLAT_W >= SMALL_TOTAL


GROUPS = (("w_in", "conv_w"), ("w_ssd_branch", "w_attn_branch", "w_out"), ("w_gate_up", "w_down"))


def to_wire(k, shard):
    if k in TRANSPOSED:
        return shard.T.astype(BF16)
    return shard if k == "conv_w" else shard.astype(BF16)


def full_weights(k, g):
    if k == "conv_w":
        return {k: g.transpose(1, 0, 2).reshape(SSD_CONV, SSD_CONV_CH)}
    full = g.reshape(-1, g.shape[-1])
    if k != "w_in":
        return {k: full}
    w, off = {}, 0
    for nm, r in IN_ROWS:
        w[nm] = full[off:off + r]
        off += r
    w["w_q"] = full[sum(r for _, r in IN_ROWS[:3]):sum(r for _, r in IN_ROWS[:6])]
    w["w_dt"] = jnp.pad(w["w_dt"], ((0, HPAD - SSD_HEADS), (0, 0)))
    return w


def grads_to_wire(k, g):
    if k == "conv_w":
        return g.reshape(SSD_CONV, N_DEV, SSD_CONV_CH // N_DEV).transpose(1, 0, 2)
    return g.reshape(N_DEV, g.shape[0] // N_DEV, g.shape[1])


def _pad_heads(t):
    return jnp.pad(t.reshape(1, SSD_HEADS), ((0, 0), (0, HPAD - SSD_HEADS)))


def local_step(x, target, getw, emit, smalls, norm_final):
    tabs = rope_tables()
    sms = []
    for li in range(DEPTH):
        s = smalls[li]
        sms.append({
            "norm_mix": s["norm_mix"].reshape(1, -1), "conv_b": s["conv_b"].reshape(1, -1),
            "dt_bias": _pad_heads(s["dt_bias"]), "a_log": _pad_heads(s["a_log"]),
            "d_skip_x": jnp.repeat(s["d_skip"], SSD_HEAD_DIM).reshape(1, -1),
            "ssd_norm": s["ssd_norm"].reshape(1, -1), "norm_ffn": s["norm_ffn"].reshape(1, -1)})
    h = x
    saved = []
    for li in range(DEPTH):
        h, sv = layer_fwd(h, functools.partial(getw, li), sms[li], tabs, li)
        saved.append(sv)
    dh, g_final, loss = loss_head(h, target, norm_final.reshape(1, -1), "loss_head")
    gsms = [None] * DEPTH
    for li in reversed(range(DEPTH)):
        dh, gsm = layer_bwd(dh, saved[li], sms[li], tabs, li, functools.partial(emit, li))
        gsms[li] = {
            "norm_mix": gsm["norm_mix"].reshape(-1), "conv_b": gsm["conv_b"].reshape(-1),
            "dt_bias": gsm["dt_bias"][0, :SSD_HEADS], "a_log": gsm["a_log"][0, :SSD_HEADS],
            "d_skip": gsm["d_skip_x"].reshape(SSD_HEADS, SSD_HEAD_DIM).sum(axis=1),
            "ssd_norm": gsm["ssd_norm"].reshape(-1), "norm_ffn": gsm["norm_ffn"].reshape(-1)}
    return loss, dh, gsms, g_final.reshape(-1)


def kernel(x, norm_mix, w_in, conv_w, conv_b, dt_bias, a_log, d_skip, ssd_norm, w_ssd_branch, w_attn_branch, w_out, norm_ffn, w_gate_up, w_down, norm_final, loss_target, m_norm_mix, m_w_in, m_conv_w, m_conv_b, m_dt_bias, m_a_log, m_d_skip, m_ssd_norm, m_w_ssd_branch, m_w_attn_branch, m_w_out, m_norm_ffn, m_w_gate_up, m_w_down, m_norm_final, v_norm_mix, v_w_in, v_conv_w, v_conv_b, v_dt_bias, v_a_log, v_d_skip, v_ssd_norm, v_w_ssd_branch, v_w_attn_branch, v_w_out, v_norm_ffn, v_w_gate_up, v_w_down, v_norm_final):
    wv = dict(norm_mix=norm_mix, w_in=w_in, conv_w=conv_w, conv_b=conv_b, dt_bias=dt_bias, a_log=a_log, d_skip=d_skip,
              ssd_norm=ssd_norm, w_ssd_branch=w_ssd_branch, w_attn_branch=w_attn_branch, w_out=w_out, norm_ffn=norm_ffn,
              w_gate_up=w_gate_up, w_down=w_down, norm_final=norm_final)
    mv = dict(norm_mix=m_norm_mix, w_in=m_w_in, conv_w=m_conv_w, conv_b=m_conv_b, dt_bias=m_dt_bias, a_log=m_a_log,
              d_skip=m_d_skip, ssd_norm=m_ssd_norm, w_ssd_branch=m_w_ssd_branch, w_attn_branch=m_w_attn_branch,
              w_out=m_w_out, norm_ffn=m_norm_ffn, w_gate_up=m_w_gate_up, w_down=m_w_down, norm_final=m_norm_final)
    vv = dict(norm_mix=v_norm_mix, w_in=v_w_in, conv_w=v_conv_w, conv_b=v_conv_b, dt_bias=v_dt_bias, a_log=v_a_log,
              d_skip=v_d_skip, ssd_norm=v_ssd_norm, w_ssd_branch=v_w_ssd_branch, w_attn_branch=v_w_attn_branch,
              w_out=v_w_out, norm_ffn=v_norm_ffn, w_gate_up=v_w_gate_up, w_down=v_w_down, norm_final=v_norm_final)
    order = ("norm_mix", "w_in", "conv_w", "conv_b", "dt_bias", "a_log", "d_skip", "ssd_norm", "w_ssd_branch",
             "w_attn_branch", "w_out", "norm_ffn", "w_gate_up", "w_down", "norm_final")

    me_index = _index(_my_place())
    smalls = [{k: wv[k][li] for k in SMALL} for li in range(DEPTH)]
    n_groups = len(GROUPS)

    first_lands = all_gather([to_wire(k, wv[k][0]) for k in GROUPS[0]], "gather_first")
    later = [(li, gi) for li in range(DEPTH) for gi in range(n_groups)][1:]
    behind_first = first_lands[1][0, 0, 0] * 0.0
    srcs = [to_wire(k, wv[k][li] + behind_first if k == "conv_w" else wv[k][li]) for li, gi in later for k in GROUPS[gi]]
    sizes = [len(GROUPS[gi]) for _, gi in later]
    w_sems, w_srcs, w_lands, token = exchange_start(srcs, [landing_zone(s, me_index) for s in srcs], sizes, False, "gather_start")
    smalls[0]["norm_mix"] = smalls[0]["norm_mix"] + token[0, 0]

    def getw(li, gi, after):
        if (li, gi) == (0, 0):
            lands = first_lands
        else:
            slot = later.index((li, gi))
            sl = slice(sum(sizes[:slot]), sum(sizes[:slot + 1]))
            lands = exchange_wait(w_srcs[sl], w_lands[sl], w_sems[slot], after, False, f"gather_wait_{li}_{gi}")
        w = {}
        for k, land in zip(GROUPS[gi], lands):
            w.update(full_weights(k, land))
        return w

    pending = []

    def emit(li, gi, gw):
        parts = [grads_to_wire(k, gw[k]) for k in GROUPS[gi]]
        lands = [landing_zone(lax.dynamic_index_in_dim(p, me_index, 0, keepdims=False), me_index) for p in parts]
        sems, p_thru, l_thru, tok = exchange_start(parts, lands, [len(parts)], True, f"grads_start_{li}_{gi}")
        pending.append((li, gi, sems[0], p_thru, l_thru))
        return tok[0, 0]

    loss_p, dx, gsms, g_final = local_step(x[0], loss_target[0], getw, emit, smalls, norm_final)

    grads, deltas, new_m, new_v = {}, {}, {}, {}

    def update(k):
        if k == "w_in":
            inner = lambda t: t.transpose(2, 0, 1)
            outs = adamw_layer_inner(inner(wv[k]), shard_g[k], inner(mv[k]), inner(vv[k]), "adamw_" + k)
            grads[k], deltas[k], new_m[k], new_v[k] = (t.transpose(1, 2, 0) for t in outs)
            return outs[3]
        if k in BIG:
            grads[k] = jnp.stack([g.T if k in TRANSPOSED else g for g in shard_g[k]])
        deltas[k], new_m[k], new_v[k] = adamw(wv[k], grads[k], mv[k], vv[k], "adamw_" + k)
        return new_v[k]

    shard_g = {k: [None] * DEPTH for k in BIG}

    def collect(entry, after):
        li, gi, sems, p_thru, l_thru = entry
        recv = exchange_wait(p_thru, l_thru, sems, after, True, f"grads_wait_{li}_{gi}")
        for k, r in zip(GROUPS[gi], recv):
            if k == "conv_w":
                r = r.reshape(N_DEV, 1, -1)
            after = sum_parts(r, f"sum_{k}_{li}", row_major_3d=(k == "w_in"))
            shard_g[k][li] = after if k in TRANSPOSED else after.reshape(wv[k].shape[1:])
        return after

    after = dx
    for entry in pending[:-1]:
        after = collect(entry, after)
    done = [after[:1, :1].reshape(1)]
    for gi in (2, 1):
        for k in GROUPS[gi]:
            done.append(update(k).reshape(-1)[:1])

    flat = [gsms[li][k] for li in range(DEPTH) for k in SMALL] + [g_final, loss_p.reshape(-1)]
    flat.append(jnp.zeros((SMALL_ROWS * FLAT_W - SMALL_TOTAL,), F32))
    small_all = all_gather([jnp.concatenate(flat).reshape(SMALL_ROWS, FLAT_W)], "gather_small")[0]
    small_sum = sum_parts(small_all, "sum_small").reshape(-1)
    off = 0
    per_layer = {k: [] for k in SMALL}
    for li in range(DEPTH):
        for k in SMALL:
            per_layer[k].append(small_sum[off:off + SMALL_SIZE[k]])
            off += SMALL_SIZE[k]
    for k in SMALL:
        grads[k] = jnp.stack(per_layer[k])
    grads["norm_final"] = small_sum[off:off + D_MODEL]
    loss = small_sum[off + D_MODEL]
    for k in (*SMALL, "norm_final"):
        done.append(update(k).reshape(-1)[:1])

    collect(pending[-1], jnp.concatenate(done))
    for k in GROUPS[0]:
        update(k)

    return (loss, dx.reshape(x.shape), *[grads[k] for k in order], *[deltas[k] for k in order],
            *[new_m[k] for k in order], *[new_v[k] for k in order])
```

```python
import functools

import jax
import jax.numpy as jnp
from jax import lax
from jax.experimental import pallas as pl
from jax.experimental.pallas import tpu as pltpu

F32, BF16 = jnp.float32, jnp.bfloat16
SDS = jax.ShapeDtypeStruct
MESH = pl.DeviceIdType.MESH

D_MODEL = 1024
SEQ = 2048
DEPTH = 2
RMS_EPS = 1e-5
SSD_INNER = 2048
SSD_HEAD_DIM = 64
SSD_HEADS = 32
SSD_STATE = 128
SSD_GROUPS = 4
SSD_CONV = 4
SSD_CHUNK = 128
SSD_CONV_CH = 3072
ATTN_HEAD_DIM = 128
ATTN_KV_HEADS = 8
ATTN_DILATIONS = (1, 4, 16)
ATTN_N_PAT = 3
ATTN_BLOCK = 128
ATTN_OUT = 1024
ROPE_THETA = 500000.0
ROPE_DIM = 32
FFN_HIDDEN = 2816
ADAM_LR, ADAM_B1, ADAM_B2, ADAM_EPS, ADAM_WD, ADAM_STEP = 0.001, 0.9, 0.999, 1e-08, 0.01, 10

N_DEV = 8
LANES = 128
VMEM_LIMIT = 56 * 1024 * 1024
HPAD = 128
HIGHEST = lax.Precision.HIGHEST

IN_ROWS = (("w_z", 2048), ("w_xbc", 3072), ("w_dt", 32), ("w_q0", 1024), ("w_q1", 1024), ("w_q2", 1024),
           ("w_k", 1024), ("w_v", 1024), ("w_gs", 1024), ("w_ga", 1024))
N_IN = sum(r for _, r in IN_ROWS)


def _cparams(sem):
    return pltpu.CompilerParams(dimension_semantics=sem, vmem_limit_bytes=VMEM_LIMIT)


def _sigmoid(x):
    return 0.5 * jnp.tanh(0.5 * x) + 0.5


def _silu(x):
    return x * _sigmoid(x)


def _softplus(x):
    return jnp.maximum(x, 0.0) + jnp.log(1.0 + jnp.exp(-jnp.abs(x)))


def _dot(a, b, dims=(((1,), (0,)), ((), ())), precision=None):
    return lax.dot_general(a, b, dims, precision=precision, preferred_element_type=F32)


NT = (((1,), (1,)), ((), ()))
TN = (((0,), (0,)), ((), ()))


def _bdot(a, b, dims=(((1,), (0,)), ((), ()))):
    return _dot(a.astype(BF16), b.astype(BF16), dims)


def _pick(dim, cands):
    for c in cands:
        if dim % c == 0:
            return c
    return dim


def matmul(a, b, *, name, ta=False, tb=False, out_dtype=F32, add=None):
    m, k = (a.shape[1], a.shape[0]) if ta else a.shape
    n = b.shape[0] if tb else b.shape[1]
    tn = _pick(n, (1024, 1408, 512, 256, 128))
    tm = _pick(m, (512, 1408, 256, 128)) if tn == n else _pick(m, (1024, 1408, 512, 256, 128))
    tk = _pick(k, (1024, 1408, 512, 256, 128))
    nk = k // tk
    a_spec = pl.BlockSpec((tk, tm), lambda i, j, kk: (kk, i)) if ta else pl.BlockSpec((tm, tk), lambda i, j, kk: (i, kk))
    b_spec = pl.BlockSpec((tn, tk), lambda i, j, kk: (j, kk)) if tb else pl.BlockSpec((tk, tn), lambda i, j, kk: (kk, j))
    dims = (((0 if ta else 1,), (1 if tb else 0,)), ((), ()))
    has_add = add is not None

    def body(*refs):
        a_ref, b_ref = refs[:2]
        add_ref = refs[2] if has_add else None
        o_ref = refs[3] if has_add else refs[2]
        acc = refs[-1] if nk > 1 else None
        kk = pl.program_id(2)

        def product():
            return _dot(a_ref[...].astype(BF16), b_ref[...].astype(BF16), dims)

        def finish(r):
            if has_add:
                r = r + add_ref[...].astype(F32)
            o_ref[...] = r.astype(o_ref.dtype)

        if nk == 1:
            finish(product())
            return

        @pl.when(kk == 0)
        def _():
            acc[...] = product()

        @pl.when((kk > 0) & (kk < nk - 1))
        def _():
            acc[...] += product()

        @pl.when(kk == nk - 1)
        def _():
            finish(acc[...] + product())

    in_specs = [a_spec, b_spec]
    args = [a, b]
    if has_add:
        in_specs.append(pl.BlockSpec((tm, tn), lambda i, j, kk: (i, j)))
        args.append(add)
    return pl.pallas_call(
        body, name=name, grid=(m // tm, n // tn, nk),
        in_specs=in_specs, out_specs=pl.BlockSpec((tm, tn), lambda i, j, kk: (i, j)),
        out_shape=SDS((m, n), out_dtype), scratch_shapes=[pltpu.VMEM((tm, tn), F32)] if nk > 1 else [],
        compiler_params=_cparams(("parallel", "parallel", "arbitrary")),
    )(*args)


def rowcall(name, fn, rows, params, row_outs, red_outs=(), tr=256):
    s = rows[0].shape[0]
    n_in = len(rows) + len(params)
    n_row = len(row_outs)

    def body(*refs):
        outs = fn(*[r[...].astype(F32) for r in refs[:n_in]])
        if not isinstance(outs, (tuple, list)):
            outs = (outs,)
        orefs = refs[n_in:]
        for r, o in zip(orefs[:n_row], outs[:n_row]):
            r[...] = o.astype(r.dtype)
        if red_outs:
            @pl.when(pl.program_id(0) == 0)
            def _():
                for r in orefs[n_row:]:
                    r[...] = jnp.zeros_like(r)
            for r, o in zip(orefs[n_row:], outs[n_row:]):
                r[...] += o.astype(F32)

    widths = [a[1] if isinstance(a, tuple) else a.shape[1] for a in rows]
    rows = [a[0] if isinstance(a, tuple) else a for a in rows]
    in_specs = [pl.BlockSpec((tr, wd), lambda i: (i, 0)) for wd in widths]
    in_specs += [pl.BlockSpec(p.shape, lambda i: (0, 0)) for p in params]
    out_specs = [pl.BlockSpec((tr, c), lambda i: (i, 0)) for c, _ in row_outs]
    out_specs += [pl.BlockSpec(shp, lambda i: (0, 0)) for shp in red_outs]
    out_shape = [SDS((s, c), dt) for c, dt in row_outs] + [SDS(shp, F32) for shp in red_outs]
    res = pl.pallas_call(
        body, name=name, grid=(s // tr,), in_specs=in_specs, out_specs=out_specs, out_shape=out_shape,
        compiler_params=_cparams(("arbitrary",) if red_outs else ("parallel",)),
    )(*rows, *params)
    return res


def _rms(x, w):
    return x * lax.rsqrt(jnp.mean(x * x, axis=-1, keepdims=True) + RMS_EPS) * w


def rms_fwd(h, w, name):
    return rowcall(name, _rms, [h], [w], [(D_MODEL, BF16)])[0]


def rms_bwd(h, du, dres, w, name):
    def fn(hb, dub, dresb, wb):
        _, vjp = jax.vjp(_rms, hb, wb)
        dh, dw = vjp(dub)
        return dh + dresb, dw
    return rowcall(name, fn, [h, du, dres], [w], [(D_MODEL, F32)], [(1, D_MODEL)])


def loss_head(h, target, w, name):
    def fn(hb, tb, wb):
        def f(hh, ww):
            err = _rms(hh, ww) - tb
            return 0.5 * jnp.sum(jnp.mean(err * err, axis=-1, keepdims=True), axis=0, keepdims=True)
        val, vjp = jax.vjp(f, hb, wb)
        dh, dw = vjp(jnp.ones((1, 1), F32))
        return dh, dw, jnp.broadcast_to(val, (1, LANES))
    return rowcall(name, fn, [h, target], [w], [(D_MODEL, F32)], [(1, D_MODEL), (1, LANES)])


def _gate(a, b, gs, ga):
    return _sigmoid(gs) * a + _sigmoid(ga) * b


def gate_fwd(a, b, gs, ga, name):
    return rowcall(name, _gate, [a, b, gs, ga], [], [(D_MODEL, BF16)])[0]


def gate_bwd(a, b, gs, ga, dm, name):
    def fn(ab, bb, gsb, gab, dmb):
        _, vjp = jax.vjp(_gate, ab, bb, gsb, gab)
        return vjp(dmb)
    return rowcall(name, fn, [a, b, gs, ga, dm], [], [(D_MODEL, BF16)] * 4)


def _swiglu(gu):
    return _silu(gu[:, :FFN_HIDDEN]) * gu[:, FFN_HIDDEN:]


def swiglu_fwd(gu, name):
    return rowcall(name, _swiglu, [gu], [], [(FFN_HIDDEN, BF16)])[0]


def swiglu_bwd(gu, dact, name):
    def fn(gub, db):
        _, vjp = jax.vjp(_swiglu, gub)
        return vjp(db.astype(F32))[0]
    return rowcall(name, fn, [gu, dact], [], [(2 * FFN_HIDDEN, BF16)])[0]


def _ssd_post(y, xs, z, dskip, normw):
    y = (y + dskip * xs) * _silu(z)
    gw = SSD_INNER // SSD_GROUPS
    parts = []
    for g in range(SSD_GROUPS):
        yg = y[:, g * gw:(g + 1) * gw]
        parts.append(yg * lax.rsqrt(jnp.mean(yg * yg, axis=-1, keepdims=True) + RMS_EPS))
    return jnp.concatenate(parts, axis=-1) * normw


def ssd_post_fwd(y, xc, z, dskip, normw, name):
    return rowcall(name, _ssd_post, [y, (xc, SSD_INNER), z], [dskip, normw], [(SSD_INNER, BF16)])[0]


def ssd_post_bwd(y, xc, z, dskip, normw, dyn, name):
    def fn(yb, xsb, zb, dynb, db, nb):
        _, vjp = jax.vjp(_ssd_post, yb, xsb, zb, db, nb)
        return vjp(dynb)
    return rowcall(name, fn, [y, (xc, SSD_INNER), z, dyn], [dskip, normw],
                   [(SSD_INNER, F32), (SSD_INNER, F32), (SSD_INNER, BF16)], [(1, SSD_INNER), (1, SSD_INNER)])


def _rope(t, cosf, sina, sinb):
    return t * cosf + pltpu.roll(t, LANES - ROPE_DIM // 2, 1) * sina + pltpu.roll(t, ROPE_DIM // 2, 1) * sinb


def rope_tables():
    half = ROPE_DIM // 2
    inv = ROPE_THETA ** (-jnp.arange(0, ROPE_DIM, 2, dtype=F32) / ROPE_DIM)
    ang = jnp.arange(SEQ, dtype=F32)[:, None] * inv[None, :]
    cos, sin = jnp.cos(ang), jnp.sin(ang)
    zeros = jnp.zeros((SEQ, LANES - ROPE_DIM), F32)
    z16 = jnp.zeros((SEQ, half), F32)
    cosf = jnp.concatenate([cos, cos, jnp.ones((SEQ, LANES - ROPE_DIM), F32)], axis=1)
    sina = jnp.concatenate([-sin, z16, zeros], axis=1)
    sinb = jnp.concatenate([z16, sin, zeros], axis=1)
    return cosf, sina, sinb


CONV_TC = 256


def _conv_pre(x, w, b, row):
    acc = x * w[SSD_CONV - 1:SSD_CONV, :] + b
    shifted = [x]
    for j in range(1, SSD_CONV):
        xs = jnp.where(row >= j, pltpu.roll(x, j, 0), 0.0)
        shifted.append(xs)
        acc = acc + xs * w[SSD_CONV - 1 - j:SSD_CONV - j, :]
    return acc, shifted


def conv_fwd(xbc, w, b, name):
    def body(x_ref, w_ref, b_ref, o_ref):
        row = lax.broadcasted_iota(jnp.int32, (SEQ, CONV_TC), 0)
        pre, _ = _conv_pre(x_ref[...].astype(F32), w_ref[...], b_ref[...], row)
        o_ref[...] = _silu(pre)
    return pl.pallas_call(
        body, name=name, grid=(SSD_CONV_CH // CONV_TC,),
        in_specs=[pl.BlockSpec((SEQ, CONV_TC), lambda i: (0, i)), pl.BlockSpec((SSD_CONV, CONV_TC), lambda i: (0, i)),
                  pl.BlockSpec((1, CONV_TC), lambda i: (0, i))],
        out_specs=pl.BlockSpec((SEQ, CONV_TC), lambda i: (0, i)),
        out_shape=SDS((SEQ, SSD_CONV_CH), F32), compiler_params=_cparams(("parallel",)),
    )(xbc, w, b)


def conv_bwd(xbc, w, b, dxc, name):
    def body(x_ref, w_ref, b_ref, dy_ref, dx_ref, dw_ref, db_ref):
        row = lax.broadcasted_iota(jnp.int32, (SEQ, CONV_TC), 0)
        wv = w_ref[...]
        pre, shifted = _conv_pre(x_ref[...].astype(F32), wv, b_ref[...], row)
        sg = _sigmoid(pre)
        ds = dy_ref[...] * (sg * (1.0 + pre * (1.0 - sg)))
        dx = ds * wv[SSD_CONV - 1:SSD_CONV, :]
        for j in range(1, SSD_CONV):
            dsj = jnp.where(row < SEQ - j, pltpu.roll(ds, SEQ - j, 0), 0.0)
            dx = dx + dsj * wv[SSD_CONV - 1 - j:SSD_CONV - j, :]
        dx_ref[...] = dx.astype(dx_ref.dtype)
        for j in range(SSD_CONV):
            dw_ref[SSD_CONV - 1 - j:SSD_CONV - j, :] = jnp.sum(ds * shifted[j], axis=0, keepdims=True)
        db_ref[...] = jnp.sum(ds, axis=0, keepdims=True)
    return pl.pallas_call(
        body, name=name, grid=(SSD_CONV_CH // CONV_TC,),
        in_specs=[pl.BlockSpec((SEQ, CONV_TC), lambda i: (0, i)), pl.BlockSpec((SSD_CONV, CONV_TC), lambda i: (0, i)),
                  pl.BlockSpec((1, CONV_TC), lambda i: (0, i)), pl.BlockSpec((SEQ, CONV_TC), lambda i: (0, i))],
        out_specs=[pl.BlockSpec((SEQ, CONV_TC), lambda i: (0, i)), pl.BlockSpec((SSD_CONV, CONV_TC), lambda i: (0, i)),
                   pl.BlockSpec((1, CONV_TC), lambda i: (0, i))],
        out_shape=[SDS((SEQ, SSD_CONV_CH), BF16), SDS((SSD_CONV, SSD_CONV_CH), F32), SDS((1, SSD_CONV_CH), F32)],
        compiler_params=_cparams(("parallel",)),
    )(xbc, w, b, dxc)


N_CHUNKS = SEQ // SSD_CHUNK
N_PAIRS = SSD_HEADS // 2
PAIRS_PER_GROUP = N_PAIRS // SSD_GROUPS
B_OFF = SSD_INNER
C_OFF = SSD_INNER + SSD_GROUPS * SSD_STATE


def _ssd_prefix(dtr, dtr_t, dtb, dtb_t, alog, alog_t):
    ln = SSD_CHUNK
    dt = _softplus(dtr + dtb)
    dt_t = _softplus(dtr_t + dtb_t)
    dta = dt * (-jnp.exp(alog))
    dta_t = dt_t * (-jnp.exp(alog_t))
    r = lax.broadcasted_iota(jnp.int32, (ln, ln), 0)
    c = lax.broadcasted_iota(jnp.int32, (ln, ln), 1)
    a_cum = _dot((r >= c).astype(F32), dta, precision=HIGHEST)
    a_cum_t = _dot(dta_t, (r <= c).astype(F32), precision=HIGHEST)
    a_last = jnp.sum(dta_t, axis=1, keepdims=True)
    return dt, a_cum, a_cum_t, a_last


def _bein(spec, a, b):
    return jnp.einsum(spec, a.astype(BF16), b.astype(BF16), preferred_element_type=F32)


SSD_GROUPS_PER_BATCH = 4


def _ssd_group(xs3, bgs, cgs, h3, dt, a_cum, a_cum_t, a_last, *, groups):
    ln = SSD_CHUNK
    lane = lax.broadcasted_iota(jnp.int32, (ln, LANES), 1)
    sub = lax.broadcasted_iota(jnp.int32, (LANES, SSD_STATE), 0)
    row = lax.broadcasted_iota(jnp.int32, (ln, ln), 0)
    col = lax.broadcasted_iota(jnp.int32, (ln, ln), 1)
    lo = lane < SSD_HEAD_DIM
    causal = row >= col
    m_lo, m_hi, dts, acs, lasts, cds, cg3, bg3 = [], [], [], [], [], [], [], []
    for g, bg, cg in zip(groups, bgs, cgs):
        cb = _bdot(cg, bg, NT)
        for j in range(PAIRS_PER_GROUP):
            e0 = 2 * (g * PAIRS_PER_GROUP + j)
            e1 = e0 + 1
            c0, c1 = a_cum[:, e0:e0 + 1], a_cum[:, e1:e1 + 1]
            r0, r1 = a_cum_t[e0:e0 + 1, :], a_cum_t[e1:e1 + 1, :]
            l0, l1 = a_last[e0:e0 + 1, :], a_last[e1:e1 + 1, :]
            m_lo.append(cb * jnp.exp(jnp.where(causal, c0 - r0, -jnp.inf)))
            m_hi.append(cb * jnp.exp(jnp.where(causal, c1 - r1, -jnp.inf)))
            dts.append(jnp.where(lo, dt[:, e0:e0 + 1], dt[:, e1:e1 + 1]))
            acs.append(jnp.where(lo, c0, c1))
            lasts.append(jnp.where(lo, l0, l1))
            cds.append(jnp.exp(jnp.where(sub < SSD_HEAD_DIM, l0, l1)))
            cg3.append(cg)
            bg3.append(bg)
    xd = xs3 * jnp.stack(dts)
    acum = jnp.stack(acs)
    y = (_bein("pls,psq->plq", jnp.stack(m_lo), jnp.where(lo[None], xd, 0.0))
         + _bein("pls,psq->plq", jnp.stack(m_hi), jnp.where(lo[None], 0.0, xd)))
    y = y + _bein("pln,pqn->plq", jnp.stack(cg3), h3) * jnp.exp(acum)
    st = _bein("plq,pln->pqn", xd * jnp.exp(jnp.stack(lasts) - acum), jnp.stack(bg3))
    h_out = h3 * jnp.stack(cds) + st
    return y, h_out


def _group_slabs(groups):
    pairs = [g * PAIRS_PER_GROUP + j for g in groups for j in range(PAIRS_PER_GROUP)]
    return [slice(p * LANES, (p + 1) * LANES) for p in pairs]


def _group_batches():
    return [tuple(range(g, g + SSD_GROUPS_PER_BATCH)) for g in range(0, SSD_GROUPS, SSD_GROUPS_PER_BATCH)]


def _bc_of(xc_ref, g):
    return (xc_ref[:, B_OFF + g * SSD_STATE:B_OFF + (g + 1) * SSD_STATE],
            xc_ref[:, C_OFF + g * SSD_STATE:C_OFF + (g + 1) * SSD_STATE])


def _ssd_in_specs(chunk_of):
    return [
        pl.BlockSpec((SSD_CHUNK, SSD_CONV_CH), lambda i: (chunk_of(i), 0)),
        pl.BlockSpec((SSD_CHUNK, HPAD), lambda i: (chunk_of(i), 0)),
        pl.BlockSpec((HPAD, SSD_CHUNK), lambda i: (0, chunk_of(i))),
        pl.BlockSpec((1, HPAD), lambda i: (0, 0)), pl.BlockSpec((HPAD, 1), lambda i: (0, 0)),
        pl.BlockSpec((1, HPAD), lambda i: (0, 0)), pl.BlockSpec((HPAD, 1), lambda i: (0, 0)),
    ]


def ssd_fwd(xc, dtr, dtr_t, dtb, dtb_t, alog, alog_t, name):
    def body(xc_ref, dtr_ref, dtrt_ref, dtb_ref, dtbt_ref, al_ref, alt_ref, y_ref, hs_ref, h_scr):
        @pl.when(pl.program_id(0) == 0)
        def _():
            h_scr[...] = jnp.zeros_like(h_scr)

        hs_ref[0] = h_scr[...]
        dt, a_cum, a_cum_t, a_last = _ssd_prefix(dtr_ref[...], dtrt_ref[...], dtb_ref[...], dtbt_ref[...],
                                                  al_ref[...], alt_ref[...])
        for groups in _group_batches():
            slabs = _group_slabs(groups)
            bgs, cgs = zip(*[_bc_of(xc_ref, g) for g in groups])
            xs3 = jnp.stack([xc_ref[:, sl] for sl in slabs])
            h3 = jnp.stack([h_scr[sl, :] for sl in slabs])
            y3, h3_out = _ssd_group(xs3, bgs, cgs, h3, dt, a_cum, a_cum_t, a_last, groups=groups)
            for j, sl in enumerate(slabs):
                y_ref[:, sl] = y3[j]
                h_scr[sl, :] = h3_out[j]

    return pl.pallas_call(
        body, name=name, grid=(N_CHUNKS,), in_specs=_ssd_in_specs(lambda i: i),
        out_specs=[pl.BlockSpec((SSD_CHUNK, SSD_INNER), lambda i: (i, 0)),
                   pl.BlockSpec((1, SSD_INNER, SSD_STATE), lambda i: (i, 0, 0))],
        out_shape=[SDS((SEQ, SSD_INNER), F32), SDS((N_CHUNKS, SSD_INNER, SSD_STATE), F32)],
        scratch_shapes=[pltpu.VMEM((SSD_INNER, SSD_STATE), F32)],
        compiler_params=_cparams(("arbitrary",)),
    )(xc, dtr, dtr_t, dtb, dtb_t, alog, alog_t)


def ssd_bwd(xc, dtr, dtr_t, dtb, dtb_t, alog, alog_t, hs, dy, dxs_extra, name):
    rev = lambda i: N_CHUNKS - 1 - i

    def body(xc_ref, dtr_ref, dtrt_ref, dtb_ref, dtbt_ref, al_ref, alt_ref, hs_ref, dy_ref, dxe_ref,
             dxc_ref, ddtr_ref, ddtrt_ref, ddtb_ref, ddtbt_ref, dal_ref, dalt_ref, dh_scr):
        @pl.when(pl.program_id(0) == 0)
        def _():
            dh_scr[...] = jnp.zeros_like(dh_scr)
            for r in (ddtb_ref, ddtbt_ref, dal_ref, dalt_ref):
                r[...] = jnp.zeros_like(r)

        prefix_in = (dtr_ref[...], dtrt_ref[...], dtb_ref[...], dtbt_ref[...], al_ref[...], alt_ref[...])
        (dt, a_cum, a_cum_t, a_last), prefix_vjp = jax.vjp(_ssd_prefix, *prefix_in)
        d_dt = jnp.zeros_like(dt)
        d_acum = jnp.zeros_like(a_cum)
        d_acum_t = jnp.zeros_like(a_cum_t)
        d_alast = jnp.zeros_like(a_last)
        for groups in _group_batches():
            slabs = _group_slabs(groups)
            bgs, cgs = zip(*[_bc_of(xc_ref, g) for g in groups])
            xs3 = jnp.stack([xc_ref[:, sl] for sl in slabs])
            h3 = jnp.stack([hs_ref[0, sl, :] for sl in slabs])
            _, vjp = jax.vjp(functools.partial(_ssd_group, groups=groups), xs3, bgs, cgs, h3, dt, a_cum, a_cum_t, a_last)
            dy3 = jnp.stack([dy_ref[:, sl] for sl in slabs])
            dh3 = jnp.stack([dh_scr[sl, :] for sl in slabs])
            dxs3, d_bgs, d_cgs, dh3_in, ddt, dac, dact, dal = vjp((dy3, dh3))
            for j, sl in enumerate(slabs):
                dxc_ref[:, sl] = dxs3[j] + dxe_ref[:, sl]
                dh_scr[sl, :] = dh3_in[j]
            d_dt, d_acum, d_acum_t, d_alast = d_dt + ddt, d_acum + dac, d_acum_t + dact, d_alast + dal
            for g, d_bg, d_cg in zip(groups, d_bgs, d_cgs):
                dxc_ref[:, B_OFF + g * SSD_STATE:B_OFF + (g + 1) * SSD_STATE] = d_bg
                dxc_ref[:, C_OFF + g * SSD_STATE:C_OFF + (g + 1) * SSD_STATE] = d_cg
        g_dtr, g_dtrt, g_dtb, g_dtbt, g_al, g_alt = prefix_vjp((d_dt, d_acum, d_acum_t, d_alast))
        ddtr_ref[...] = g_dtr
        ddtrt_ref[...] = g_dtrt
        ddtb_ref[...] += g_dtb
        ddtbt_ref[...] += g_dtbt
        dal_ref[...] += g_al
        dalt_ref[...] += g_alt

    in_specs = _ssd_in_specs(rev) + [
        pl.BlockSpec((1, SSD_INNER, SSD_STATE), lambda i: (rev(i), 0, 0)),
        pl.BlockSpec((SSD_CHUNK, SSD_INNER), lambda i: (rev(i), 0)),
        pl.BlockSpec((SSD_CHUNK, SSD_INNER), lambda i: (rev(i), 0)),
    ]
    out_specs = [
        pl.BlockSpec((SSD_CHUNK, SSD_CONV_CH), lambda i: (rev(i), 0)),
        pl.BlockSpec((SSD_CHUNK, HPAD), lambda i: (rev(i), 0)),
        pl.BlockSpec((HPAD, SSD_CHUNK), lambda i: (0, rev(i))),
        pl.BlockSpec((1, HPAD), lambda i: (0, 0)), pl.BlockSpec((HPAD, 1), lambda i: (0, 0)),
        pl.BlockSpec((1, HPAD), lambda i: (0, 0)), pl.BlockSpec((HPAD, 1), lambda i: (0, 0)),
    ]
    out_shape = [SDS((SEQ, SSD_CONV_CH), F32), SDS((SEQ, HPAD), F32), SDS((HPAD, SEQ), F32),
                 SDS((1, HPAD), F32), SDS((HPAD, 1), F32), SDS((1, HPAD), F32), SDS((HPAD, 1), F32)]
    return pl.pallas_call(
        body, name=name, grid=(N_CHUNKS,), in_specs=in_specs, out_specs=out_specs, out_shape=out_shape,
        scratch_shapes=[pltpu.VMEM((SSD_INNER, SSD_STATE), F32)],
        compiler_params=_cparams(("arbitrary",)),
    )(xc, dtr, dtr_t, dtb, dtb_t, alog, alog_t, hs, dy, dxs_extra)


ATTN_SCALE = ATTN_HEAD_DIM ** -0.5


UNITS_PER_PATTERN = SEQ // ATTN_BLOCK
ATTN_BATCH = 8


def _for_unit_batches(batch):
    for g, d in enumerate(ATTN_DILATIONS):
        nb = UNITS_PER_PATTERN // d
        span = d * ATTN_BLOCK

        def trip(t, carry, g=g, d=d, nb=nb, span=span):
            units = []
            for j in range(ATTN_BATCH):
                i = t * ATTN_BATCH + j
                r = i >> (nb.bit_length() - 1)
                n = i & (nb - 1)
                start = r + n * span
                prev = jnp.where(n > 0, start - span, start)
                units.append((pl.ds(start, ATTN_BLOCK, stride=d), pl.ds(prev, ATTN_BLOCK, stride=d), n > 0))
            batch(g, units)
            return carry
        lax.fori_loop(0, UNITS_PER_PATTERN // ATTN_BATCH, trip, 0)


def _unit_operands(units, q_scr, k_scr, v_scr):
    def pair(scr, rows, prows):
        return jnp.concatenate([scr[prows, :], scr[rows, :]], axis=0)
    qb = jnp.stack([q_scr[rows, :] for rows, _, _ in units]).astype(BF16)
    kb = jnp.stack([pair(k_scr, rows, prows) for rows, prows, _ in units]).astype(BF16)
    vb = jnp.stack([pair(v_scr, rows, prows) for rows, prows, _ in units]).astype(BF16)
    return qb, kb, vb


def _unit_scores(qb, kb, units):
    s = jnp.einsum("bqd,bkd->bqk", qb, kb, preferred_element_type=F32) * ATTN_SCALE
    qi = lax.broadcasted_iota(jnp.int32, (ATTN_BLOCK, 2 * ATTN_BLOCK), 0)
    kj = lax.broadcasted_iota(jnp.int32, (ATTN_BLOCK, 2 * ATTN_BLOCK), 1)
    own = (kj >= ATTN_BLOCK) & (kj - ATTN_BLOCK <= qi)
    before = (kj < ATTN_BLOCK) & (kj >= qi)
    keep = jnp.stack([own | (before & has_prev) for _, _, has_prev in units])
    return jnp.where(keep, s, -jnp.inf)


def _head_specs(n_q_groups):
    blk = (SEQ, ATTN_HEAD_DIM)
    q_specs = [pl.BlockSpec(blk, functools.partial(lambda h, g: (0, g * ATTN_KV_HEADS + h), g=g)) for g in range(n_q_groups)]
    head = pl.BlockSpec(blk, lambda h: (0, h))
    table = pl.BlockSpec(blk, lambda h: (0, 0))
    return q_specs, head, table


def attn_fwd(q, k, v, tabs, name):
    q_specs, head, table = _head_specs(ATTN_N_PAT)

    def body(q0_ref, q1_ref, q2_ref, k_ref, v_ref, c_ref, sa_ref, sb_ref, y_ref, lse_ref, *scr):
        qs, og, ls, ks, vs = scr[0:3], scr[3:6], scr[6:9], scr[9], scr[10]
        c, sa, sb = c_ref[...], sa_ref[...], sb_ref[...]
        for g, q_ref in enumerate((q0_ref, q1_ref, q2_ref)):
            qs[g][...] = _rope(q_ref[...].astype(F32), c, sa, sb)
        ks[...] = _rope(k_ref[...].astype(F32), c, sa, sb)
        vs[...] = v_ref[...].astype(F32)

        def batch(g, units):
            qb, kb, vb = _unit_operands(units, qs[g], ks, vs)
            s = _unit_scores(qb, kb, units)
            m = jnp.max(s, axis=2, keepdims=True)
            p = jnp.exp(s - m)
            l = jnp.sum(p, axis=2, keepdims=True)
            o = jnp.einsum("bqk,bkd->bqd", p.astype(BF16), vb, preferred_element_type=F32) / l
            lse_b = m + jnp.log(l)
            for j, (rows, _, _) in enumerate(units):
                og[g][rows, :] = o[j]
                ls[g][rows, :] = jnp.broadcast_to(lse_b[j], (ATTN_BLOCK, LANES))

        _for_unit_batches(batch)
        l0, l1, l2 = ls[0][...], ls[1][...], ls[2][...]
        m = jnp.maximum(jnp.maximum(l0, l1), l2)
        e0, e1, e2 = jnp.exp(l0 - m), jnp.exp(l1 - m), jnp.exp(l2 - m)
        den = e0 + e1 + e2
        y_ref[...] = ((e0 * og[0][...] + e1 * og[1][...] + e2 * og[2][...]) / den).astype(y_ref.dtype)
        lse_ref[...] = m + jnp.log(den)

    blk = (SEQ, ATTN_HEAD_DIM)
    return pl.pallas_call(
        body, name=name, grid=(ATTN_KV_HEADS,), in_specs=[*q_specs, head, head, table, table, table],
        out_specs=[head, head], out_shape=[SDS((SEQ, ATTN_OUT), BF16), SDS((SEQ, ATTN_OUT), F32)],
        scratch_shapes=[pltpu.VMEM(blk, F32)] * (3 * ATTN_N_PAT + 2),
        compiler_params=_cparams(("parallel",)),
    )(q, q, q, k, v, *tabs)


def attn_bwd(q, k, v, tabs, y, lse, dy, name):
    q_specs, head, table = _head_specs(ATTN_N_PAT)

    def body(q0_ref, q1_ref, q2_ref, k_ref, v_ref, c_ref, sa_ref, sb_ref, y_ref, lse_ref, dy_ref,
             dq0_ref, dq1_ref, dq2_ref, dk_ref, dv_ref, *scr):
        qs, dqs, ks, dks, dd, dvs, vs = scr[0:3], scr[3:6], scr[6], scr[7], scr[8], scr[9], scr[10]
        c, sa, sb = c_ref[...], sa_ref[...], sb_ref[...]
        for g, q_ref in enumerate((q0_ref, q1_ref, q2_ref)):
            qs[g][...] = _rope(q_ref[...].astype(F32), c, sa, sb)
        ks[...] = _rope(k_ref[...].astype(F32), c, sa, sb)
        vs[...] = v_ref[...].astype(F32)
        dks[...] = jnp.zeros_like(dks)
        dvs[...] = jnp.zeros_like(dvs)
        dyv = dy_ref[...]
        dd[...] = jnp.broadcast_to(jnp.sum(dyv * y_ref[...].astype(F32), axis=1, keepdims=True), dd.shape)

        def batch(g, units):
            qb, kb, vb = _unit_operands(units, qs[g], ks, vs)
            dob = jnp.stack([dy_ref[rows, :] for rows, _, _ in units]).astype(BF16)
            lse_b = jnp.stack([lse_ref[rows, :][:, 0:1] for rows, _, _ in units])
            dsum_b = jnp.stack([dd[rows, :][:, 0:1] for rows, _, _ in units])
            p = jnp.exp(_unit_scores(qb, kb, units) - lse_b)
            dp = jnp.einsum("bqd,bkd->bqk", dob, vb, preferred_element_type=F32)
            ds = (p * (dp - dsum_b) * ATTN_SCALE).astype(BF16)
            dq = jnp.einsum("bqk,bkd->bqd", ds, kb, preferred_element_type=F32)
            dk = jnp.einsum("bqk,bqd->bkd", ds, qb, preferred_element_type=F32)
            dv = jnp.einsum("bqk,bqd->bkd", p.astype(BF16), dob, preferred_element_type=F32)
            for j, (rows, prows, _) in enumerate(units):
                dqs[g][rows, :] = dq[j]
                dks[prows, :] += dk[j, :ATTN_BLOCK]
                dks[rows, :] += dk[j, ATTN_BLOCK:]
                dvs[prows, :] += dv[j, :ATTN_BLOCK]
                dvs[rows, :] += dv[j, ATTN_BLOCK:]

        _for_unit_batches(batch)
        for g, dq_ref in enumerate((dq0_ref, dq1_ref, dq2_ref)):
            dq_ref[...] = _rope(dqs[g][...], c, -sa, -sb).astype(dq_ref.dtype)
        dk_ref[...] = _rope(dks[...], c, -sa, -sb).astype(dk_ref.dtype)
        dv_ref[...] = dvs[...].astype(dv_ref.dtype)

    blk = (SEQ, ATTN_HEAD_DIM)
    out = SDS((SEQ, ATTN_OUT), BF16)
    return pl.pallas_call(
        body, name=name, grid=(ATTN_KV_HEADS,), in_specs=[*q_specs, head, head, table, table, table, head, head, head],
        out_specs=[head] * 5, out_shape=[out] * 5,
        scratch_shapes=[pltpu.VMEM(blk, F32)] * (2 * ATTN_N_PAT + 5),
        compiler_params=_cparams(("parallel",)),
    )(q, q, q, k, v, *tabs, y, lse, dy)


def layer_fwd(h, getw, prefetch, small, tabs, li):
    n = f"l{li}_"
    sv = {}
    w = dict(getw(0, h))
    u = rms_fwd(h, small["norm_mix"], n + "rms_mix")
    z = matmul(u, w["w_z"], name=n + "mm_z", tb=True, out_dtype=BF16)
    prefetch(1, z)
    xbc = matmul(u, w["w_xbc"], name=n + "mm_xbc", tb=True, out_dtype=BF16)
    dtr = matmul(u, w["w_dt"], name=n + "mm_dt", tb=True)
    q = matmul(u, w["w_q"], name=n + "mm_q", tb=True, out_dtype=BF16)
    k = matmul(u, w["w_k"], name=n + "mm_k", tb=True, out_dtype=BF16)
    v = matmul(u, w["w_v"], name=n + "mm_v", tb=True, out_dtype=BF16)
    gs = matmul(u, w["w_gs"], name=n + "mm_gs", tb=True, out_dtype=BF16)
    ga = matmul(u, w["w_ga"], name=n + "mm_ga", tb=True, out_dtype=BF16)
    xc = conv_fwd(xbc, w["conv_w"], small["conv_b"], n + "conv")
    dtr_t = dtr.T
    y_ssd, hs = ssd_fwd(xc, dtr, dtr_t, small["dt_bias"], small["dt_bias"].T, small["a_log"], small["a_log"].T, n + "ssd")
    yn = ssd_post_fwd(y_ssd, xc, z, small["d_skip_x"], small["ssd_norm"], n + "ssd_post")
    y_attn, lse = attn_fwd(q, k, v, tabs, n + "attn")
    w.update(getw(1, y_ssd))
    a = matmul(yn, w["w_ssd_branch"], name=n + "mm_a", out_dtype=BF16)
    b = matmul(y_attn, w["w_attn_branch"], name=n + "mm_b", out_dtype=BF16)
    merged = gate_fwd(a, b, gs, ga, n + "gate")
    h1 = matmul(merged, w["w_out"], name=n + "mm_o", add=h)
    w.update(getw(2, h1))
    u2 = rms_fwd(h1, small["norm_ffn"], n + "rms_ffn")
    gu = matmul(u2, w["w_gate_up"], name=n + "mm_gu", tb=True, out_dtype=BF16)
    act = swiglu_fwd(gu, n + "swiglu")
    h2 = matmul(act, w["w_down"], name=n + "mm_down", add=h1)
    sv.update(h=h, u=u, z=z, xbc=xbc, dtr=dtr, dtr_t=dtr_t, gs=gs, ga=ga, xc=xc, y_ssd=y_ssd, hs=hs, yn=yn,
              q=q, k=k, v=v, y_attn=y_attn, lse=lse, a=a, b=b, merged=merged, h1=h1, u2=u2, gu=gu, act=act, w=w)
    return h2, sv


def layer_bwd(dh, sv, small, tabs, li, emit):
    n = f"l{li}_b_"
    w = sv["w"]
    gw, gsm = {}, {}
    dact = matmul(dh, w["w_down"], name=n + "mm_dact", tb=True, out_dtype=BF16)
    gw["w_down"] = matmul(sv["act"], dh, name=n + "mm_dwdown", ta=True, out_dtype=BF16)
    dgu = swiglu_bwd(sv["gu"], dact, n + "swiglu")
    gw["w_gate_up"] = matmul(dgu, sv["u2"], name=n + "mm_dwgu", ta=True, out_dtype=BF16)
    tok = emit(2, gw)
    du2 = matmul(dgu, w["w_gate_up"], name=n + "mm_du2")
    dh1, gsm["norm_ffn"] = rms_bwd(sv["h1"], du2, dh, small["norm_ffn"] + tok, n + "rms_ffn")
    dmerged = matmul(dh1, w["w_out"], name=n + "mm_dmerged", tb=True)
    gw["w_out"] = matmul(sv["merged"], dh1, name=n + "mm_dwo", ta=True, out_dtype=BF16)
    da, db, dgs, dga = gate_bwd(sv["a"], sv["b"], sv["gs"], sv["ga"], dmerged, n + "gate")
    gw["w_ssd_branch"] = matmul(sv["yn"], da, name=n + "mm_dwa", ta=True, out_dtype=BF16)
    gw["w_attn_branch"] = matmul(sv["y_attn"], db, name=n + "mm_dwb", ta=True, out_dtype=BF16)
    tok = emit(1, gw)
    dyn = matmul(da, w["w_ssd_branch"], name=n + "mm_dyn", tb=True)
    dyattn = matmul(db, w["w_attn_branch"], name=n + "mm_dyattn", tb=True)
    dy_ssd, dxs_extra, dz, gsm["d_skip_x"], gsm["ssd_norm"] = ssd_post_bwd(
        sv["y_ssd"], sv["xc"], sv["z"], small["d_skip_x"] + tok, small["ssd_norm"], dyn, n + "ssd_post")
    dxc, ddtr, ddtr_t, ddtb, ddtb_t, dal, dal_t = ssd_bwd(
        sv["xc"], sv["dtr"], sv["dtr_t"], small["dt_bias"], small["dt_bias"].T, small["a_log"], small["a_log"].T,
        sv["hs"], dy_ssd, dxs_extra, n + "ssd")
    ddtr = (ddtr + ddtr_t.T).astype(BF16)
    gsm["dt_bias"] = ddtb + ddtb_t.T
    gsm["a_log"] = dal + dal_t.T
    dxbc, gw["conv_w"], gsm["conv_b"] = conv_bwd(sv["xbc"], w["conv_w"], small["conv_b"], dxc, n + "conv")
    dq0, dq1, dq2, dk, dv = attn_bwd(sv["q"], sv["k"], sv["v"], tabs, sv["y_attn"], sv["lse"], dyattn, n + "attn")
    u = sv["u"]
    segs = [("w_z", dz), ("w_xbc", dxbc), ("w_dt", ddtr), ("w_q0", dq0), ("w_q1", dq1), ("w_q2", dq2),
            ("w_k", dk), ("w_v", dv), ("w_gs", dgs), ("w_ga", dga)]
    gin = [matmul(dseg, u, name=n + "mm_d" + key, ta=True, out_dtype=BF16) for key, dseg in segs]
    gin[2] = gin[2][:SSD_HEADS]
    gw["w_in"] = jnp.concatenate(gin, axis=0)
    tok = emit(0, gw)
    du = jnp.zeros((SEQ, D_MODEL), F32) + tok
    for key, dseg in segs:
        du = matmul(dseg, w[key], name=n + "mm_du_" + key, add=du)
    dh0, gsm["norm_mix"] = rms_bwd(sv["h"], du, dh1, small["norm_mix"] + tok, n + "rms_mix")
    return dh0, gsm


def _my_place():
    return lax.axis_index("x"), lax.axis_index("y"), lax.axis_index("c")


def _flip(place, k):
    x, y, c = place
    return (1 - x if k & 4 else x, 1 - y if k & 2 else y, 1 - c if k & 1 else c)


def _index(place):
    return 4 * place[0] + 2 * place[1] + place[2]


ANY = pl.BlockSpec(memory_space=pl.ANY)
CHIP_FLIPS = (4, 2, 6)


def all_gather(xs, name):
    na = len(xs)

    def body(*refs):
        x_refs, o_refs = refs[:na], refs[na:2 * na]
        send_sems, recv_sems, local_sems = refs[2 * na:]
        me = _my_place()
        sibling = _flip(me, 1)
        chips = [_flip(me, f) for f in CHIP_FLIPS]

        def copy(a, kk, block, to, src=None):
            dst = o_refs[a].at[_index(block)]
            return pltpu.make_async_remote_copy(
                src_ref=dst if src is None else src, dst_ref=dst, send_sem=send_sems.at[a, kk],
                recv_sem=recv_sems.at[a, kk], device_id=to, device_id_type=MESH)

        mine = [pltpu.make_async_copy(x_refs[a], o_refs[a].at[_index(me)], local_sems.at[a]) for a in range(na)]
        for cp in mine:
            cp.start()
        first = []
        for j, chip in enumerate(chips):
            first += [copy(a, 1 + j, me, chip, src=x_refs[a]) for a in range(na)]
        first += [copy(a, 0, me, sibling, src=x_refs[a]) for a in range(na)]
        for cp in first:
            cp.start()
        passed = []
        for j, chip in enumerate(chips):
            for a in range(na):
                copy(a, 1 + j, chip, me).wait_recv()
                cp = copy(a, 4 + j, chip, sibling)
                cp.start()
                passed.append(cp)
        for a in range(na):
            copy(a, 0, sibling, me).wait_recv()
        for j, chip in enumerate(chips):
            for a in range(na):
                copy(a, 4 + j, _flip(chip, 1), me).wait_recv()
        for cp in first + passed:
            cp.wait_send()
        for cp in mine:
            cp.wait()

    return pl.pallas_call(
        body, name=name, in_specs=[ANY] * na, out_specs=[ANY] * na,
        out_shape=[SDS((N_DEV,) + t.shape, t.dtype) for t in xs],
        scratch_shapes=[pltpu.SemaphoreType.DMA((na, N_DEV - 1)), pltpu.SemaphoreType.DMA((na, N_DEV - 1)),
                        pltpu.SemaphoreType.DMA((na,))],
    )(*xs)


HBM = pl.BlockSpec(memory_space=pltpu.HBM)
SEM = pl.BlockSpec(memory_space=pltpu.SEMAPHORE)
EFFECT = pltpu.SideEffectType.DATAFLOW_SIDE_EFFECTING
N_PEERS = N_DEV - 1


def _split_copy(src_ref, land_ref, send_sem, recv_sem, me, kk, scatter, landed_from_peer):
    peer = _flip(me, kk)
    src = src_ref.at[_index(peer)] if scatter else src_ref
    dst = land_ref.at[_index(peer if landed_from_peer else me)]
    return pltpu.make_async_remote_copy(src_ref=src, dst_ref=dst, send_sem=send_sem, recv_sem=recv_sem,
                                        device_id=peer, device_id_type=MESH)


ALL_PEERS = tuple(range(1, N_DEV))


def exchange_start(srcs, lands, group_sizes, scatter, name, peers=ALL_PEERS):
    na, ng = len(srcs), len(group_sizes)

    def body(*refs):
        s_refs, l_refs = refs[:na], refs[na:2 * na]
        sems = refs[2 * na:2 * na + 2 * ng]
        token = refs[-1]
        me = _my_place()
        a = 0
        for gi, gsz in enumerate(group_sizes):
            for j in range(gsz):
                for pi, kk in enumerate(peers):
                    slot = j * len(peers) + pi
                    _split_copy(s_refs[a], l_refs[a], sems[2 * gi].at[slot], sems[2 * gi + 1].at[slot],
                                me, kk, scatter, False).start()
                a += 1
        token[...] = jnp.zeros_like(token)

    sem_shapes = []
    for gsz in group_sizes:
        sem_shapes += [pltpu.SemaphoreType.DMA((gsz * len(peers),))] * 2
    ins = [pltpu.with_memory_space_constraint(t, pltpu.HBM) for t in (*srcs, *lands)]
    res = pl.pallas_call(
        body, name=name, in_specs=[HBM] * (2 * na),
        out_specs=[SEM] * (2 * ng) + [HBM] * (2 * na) + [pl.BlockSpec(memory_space=pltpu.VMEM)],
        out_shape=sem_shapes + [pltpu.HBM(t.shape, t.dtype) for t in ins] + [SDS((8, LANES), F32)],
        input_output_aliases={i: 2 * ng + i for i in range(2 * na)},
        compiler_params=pltpu.CompilerParams(has_side_effects=EFFECT),
    )(*ins)
    sems = [(res[2 * gi], res[2 * gi + 1]) for gi in range(ng)]
    thru = res[2 * ng:2 * ng + 2 * na]
    return sems, thru[:na], thru[na:], res[-1]


def _wait_split_copies(s_refs, l_refs, send_sems, recv_sems, scatter, peers):
    me = _my_place()
    for j in range(len(s_refs)):
        for pi, kk in enumerate(peers):
            slot = j * len(peers) + pi
            cp = _split_copy(s_refs[j], l_refs[j], send_sems.at[slot], recv_sems.at[slot], me, kk, scatter, True)
            cp.wait_send()
            cp.wait_recv()


def exchange_wait(srcs, lands, sems, after, scatter, name, peers=ALL_PEERS):
    n = len(srcs)

    def body(*refs):
        s_refs, l_refs = refs[:n], refs[n:2 * n]
        _wait_split_copies(s_refs, l_refs, refs[2 * n], refs[2 * n + 1], scatter, peers)

    res = pl.pallas_call(
        body, name=name, in_specs=[HBM] * (2 * n) + [SEM, SEM, ANY], out_specs=[HBM] * (2 * n),
        out_shape=[pltpu.HBM(t.shape, t.dtype) for t in (*srcs, *lands)],
        input_output_aliases={i: i for i in range(2 * n)},
        compiler_params=pltpu.CompilerParams(has_side_effects=EFFECT),
    )(*srcs, *lands, sems[0], sems[1], after)
    return res[n:]


def _sibling_copies(l_refs, send_sems, recv_sems, arriving):
    me = _my_place()
    sibling = _flip(me, 1)
    held = [me] + [_flip(me, f) for f in CHIP_FLIPS]
    copies = []
    for j, land in enumerate(l_refs):
        for bi, place in enumerate(held):
            blk = land.at[_index(_flip(place, 1) if arriving else place)]
            slot = j * len(held) + bi
            copies.append(pltpu.make_async_remote_copy(src_ref=blk, dst_ref=blk, send_sem=send_sems.at[slot],
                                                       recv_sem=recv_sems.at[slot], device_id=sibling, device_id_type=MESH))
    return copies


def gather_forward(srcs, lands, sems, after, name):
    n = len(srcs)

    def body(*refs):
        s_refs, l_refs = refs[:n], refs[n:2 * n]
        _wait_split_copies(s_refs, l_refs, refs[2 * n], refs[2 * n + 1], False, CHIP_FLIPS)
        for cp in _sibling_copies(l_refs, refs[2 * n + 3], refs[2 * n + 4], False):
            cp.start()

    n_slots = n * (1 + len(CHIP_FLIPS))
    res = pl.pallas_call(
        body, name=name, in_specs=[HBM] * (2 * n) + [SEM, SEM, ANY],
        out_specs=[SEM, SEM] + [HBM] * (2 * n),
        out_shape=[pltpu.SemaphoreType.DMA((n_slots,))] * 2 + [pltpu.HBM(t.shape, t.dtype) for t in (*srcs, *lands)],
        input_output_aliases={i: 2 + i for i in range(2 * n)},
        compiler_params=pltpu.CompilerParams(has_side_effects=EFFECT),
    )(*srcs, *lands, sems[0], sems[1], after)
    return (res[0], res[1]), res[2 + n:]


def gather_finish(lands, sems, after, name):
    n = len(lands)

    def body(*refs):
        l_refs = refs[:n]
        for cp in _sibling_copies(l_refs, refs[n], refs[n + 1], True):
            cp.wait_send()
            cp.wait_recv()

    return pl.pallas_call(
        body, name=name, in_specs=[HBM] * n + [SEM, SEM, ANY], out_specs=[HBM] * n,
        out_shape=[pltpu.HBM(t.shape, t.dtype) for t in lands],
        input_output_aliases={i: i for i in range(n)},
        compiler_params=pltpu.CompilerParams(has_side_effects=EFFECT),
    )(*lands, sems[0], sems[1], after)


def landing_zone(block, me_index):
    land = lax.empty((N_DEV,) + block.shape, block.dtype)
    return lax.dynamic_update_slice(land, block[None], (me_index,) + (0,) * block.ndim)


def sum_parts(parts, name, row_major_3d=False):
    _, r, c = parts.shape
    tc = _pick(c, (256, 128))

    def body(p_ref, o_ref):
        acc = p_ref[0].astype(F32)
        for i in range(1, N_DEV):
            acc = acc + p_ref[i].astype(F32)
        if row_major_3d:
            o_ref[:, 0, :] = acc
        else:
            o_ref[...] = acc

    out_spec = pl.BlockSpec((r, 1, tc), lambda i: (0, 0, i)) if row_major_3d else pl.BlockSpec((r, tc), lambda i: (0, i))
    return pl.pallas_call(
        body, name=name, grid=(c // tc,), in_specs=[pl.BlockSpec((N_DEV, r, tc), lambda i: (0, 0, i))],
        out_specs=out_spec, out_shape=SDS((r, 1, c) if row_major_3d else (r, c), F32),
        compiler_params=_cparams(("parallel",)),
    )(parts)


ADAMW_BLOCK_BYTES = 2 * 1024 * 1024


def adamw(w, g, m, v, name):
    shape = w.shape
    lay, rows, cols = ((1, 1) + shape)[-3:]
    tr = _pick(rows, (256, 128))
    tc = cols if tr * cols * 4 <= ADAMW_BLOCK_BYTES else _pick(cols, (256, 128))
    c1 = 1.0 / (1.0 - ADAM_B1 ** ADAM_STEP)
    c2 = 1.0 / (1.0 - ADAM_B2 ** ADAM_STEP)

    def body(w_ref, g_ref, m_ref, v_ref, d_ref, nm_ref, nv_ref):
        gg = g_ref[...]
        nm = ADAM_B1 * m_ref[...] + (1.0 - ADAM_B1) * gg
        nv = ADAM_B2 * v_ref[...] + (1.0 - ADAM_B2) * (gg * gg)
        d_ref[...] = -ADAM_LR * ((nm * c1) / (jnp.sqrt(nv * c2) + ADAM_EPS) + ADAM_WD * w_ref[...])
        nm_ref[...] = nm
        nv_ref[...] = nv

    spec = pl.BlockSpec((1, tr, tc), lambda l, i, j: (l, i, j))
    outs = pl.pallas_call(
        body, name=name, grid=(lay, rows // tr, cols // tc), in_specs=[spec] * 4, out_specs=[spec] * 3,
        out_shape=[SDS((lay, rows, cols), F32)] * 3, compiler_params=_cparams(("parallel",) * 3),
    )(*[t.reshape(lay, rows, cols) for t in (w, g, m, v)])
    return [o.reshape(shape) for o in outs]


def adamw_layer_inner(w, gs, m, v, name):
    rows, lay, cols = w.shape
    tr = _pick(rows, (256, 220, 128))
    c1 = 1.0 / (1.0 - ADAM_B1 ** ADAM_STEP)
    c2 = 1.0 / (1.0 - ADAM_B2 ** ADAM_STEP)

    def body(*refs):
        w_ref, m_ref, v_ref = refs[:3]
        g_refs = refs[3:3 + lay]
        go_ref, d_ref, nm_ref, nv_ref = refs[3 + lay:]
        for l, g_ref in enumerate(g_refs):
            gg = g_ref[:, 0, :]
            nm = ADAM_B1 * m_ref[:, l, :] + (1.0 - ADAM_B1) * gg
            nv = ADAM_B2 * v_ref[:, l, :] + (1.0 - ADAM_B2) * (gg * gg)
            d_ref[:, l, :] = -ADAM_LR * ((nm * c1) / (jnp.sqrt(nv * c2) + ADAM_EPS) + ADAM_WD * w_ref[:, l, :])
            go_ref[:, l, :] = gg
            nm_ref[:, l, :] = nm
            nv_ref[:, l, :] = nv

    inner = pl.BlockSpec((tr, lay, cols), lambda i: (i, 0, 0))
    plain = pl.BlockSpec((tr, 1, cols), lambda i: (i, 0, 0))
    return pl.pallas_call(
        body, name=name, grid=(rows // tr,), in_specs=[inner] * 3 + [plain] * lay, out_specs=[inner] * 4,
        out_shape=[SDS((rows, lay, cols), F32)] * 4, compiler_params=_cparams(("parallel",)),
    )(w, m, v, *gs)


BIG = ("w_in", "conv_w", "w_ssd_branch", "w_attn_branch", "w_out", "w_gate_up", "w_down")
TRANSPOSED = ("w_in", "w_gate_up")
SMALL = ("norm_mix", "conv_b", "dt_bias", "a_log", "d_skip", "ssd_norm", "norm_ffn")
SMALL_SIZE = {"norm_mix": 1024, "conv_b": 3072, "dt_bias": 32, "a_log": 32, "d_skip": 32, "ssd_norm": 2048, "norm_ffn": 1024}
FLAT_W = 512
SMALL_TOTAL = DEPTH * sum(SMALL_SIZE.values()) + D_MODEL + LANES
SMALL_ROWS = 32
assert SMALL_ROWS * FLAT_W >= SMALL_TOTAL


GROUPS = (("w_in", "conv_w"), ("w_ssd_branch", "w_attn_branch", "w_out"), ("w_gate_up", "w_down"))


def to_wire(k, shard):
    if k in TRANSPOSED:
        return shard.T.astype(BF16)
    return shard if k == "conv_w" else shard.astype(BF16)


def full_weights(k, g):
    if k == "conv_w":
        return {k: g.transpose(1, 0, 2).reshape(SSD_CONV, SSD_CONV_CH)}
    full = g.reshape(-1, g.shape[-1])
    if k != "w_in":
        return {k: full}
    w, off = {}, 0
    for nm, r in IN_ROWS:
        w[nm] = full[off:off + r]
        off += r
    w["w_q"] = full[sum(r for _, r in IN_ROWS[:3]):sum(r for _, r in IN_ROWS[:6])]
    w["w_dt"] = jnp.pad(w["w_dt"], ((0, HPAD - SSD_HEADS), (0, 0)))
    return w


def grads_to_wire(k, g):
    if k == "conv_w":
        return g.reshape(SSD_CONV, N_DEV, SSD_CONV_CH // N_DEV).transpose(1, 0, 2)
    return g.reshape(N_DEV, g.shape[0] // N_DEV, g.shape[1])


def _pad_heads(t):
    return jnp.pad(t.reshape(1, SSD_HEADS), ((0, 0), (0, HPAD - SSD_HEADS)))


def local_step(x, target, getw, prefetch, emit, smalls, norm_final):
    tabs = rope_tables()
    sms = []
    for li in range(DEPTH):
        s = smalls[li]
        sms.append({
            "norm_mix": s["norm_mix"].reshape(1, -1), "conv_b": s["conv_b"].reshape(1, -1),
            "dt_bias": _pad_heads(s["dt_bias"]), "a_log": _pad_heads(s["a_log"]),
            "d_skip_x": jnp.repeat(s["d_skip"], SSD_HEAD_DIM).reshape(1, -1),
            "ssd_norm": s["ssd_norm"].reshape(1, -1), "norm_ffn": s["norm_ffn"].reshape(1, -1)})
    h = x
    saved = []
    for li in range(DEPTH):
        h, sv = layer_fwd(h, functools.partial(getw, li), functools.partial(prefetch, li), sms[li], tabs, li)
        saved.append(sv)
    dh, g_final, loss = loss_head(h, target, norm_final.reshape(1, -1), "loss_head")
    gsms = [None] * DEPTH
    for li in reversed(range(DEPTH)):
        dh, gsm = layer_bwd(dh, saved[li], sms[li], tabs, li, functools.partial(emit, li))
        gsms[li] = {
            "norm_mix": gsm["norm_mix"].reshape(-1), "conv_b": gsm["conv_b"].reshape(-1),
            "dt_bias": gsm["dt_bias"][0, :SSD_HEADS], "a_log": gsm["a_log"][0, :SSD_HEADS],
            "d_skip": gsm["d_skip_x"].reshape(SSD_HEADS, SSD_HEAD_DIM).sum(axis=1),
            "ssd_norm": gsm["ssd_norm"].reshape(-1), "norm_ffn": gsm["norm_ffn"].reshape(-1)}
    return loss, dh, gsms, g_final.reshape(-1)


def kernel(x, norm_mix, w_in, conv_w, conv_b, dt_bias, a_log, d_skip, ssd_norm, w_ssd_branch, w_attn_branch, w_out, norm_ffn, w_gate_up, w_down, norm_final, loss_target, m_norm_mix, m_w_in, m_conv_w, m_conv_b, m_dt_bias, m_a_log, m_d_skip, m_ssd_norm, m_w_ssd_branch, m_w_attn_branch, m_w_out, m_norm_ffn, m_w_gate_up, m_w_down, m_norm_final, v_norm_mix, v_w_in, v_conv_w, v_conv_b, v_dt_bias, v_a_log, v_d_skip, v_ssd_norm, v_w_ssd_branch, v_w_attn_branch, v_w_out, v_norm_ffn, v_w_gate_up, v_w_down, v_norm_final):
    wv = dict(norm_mix=norm_mix, w_in=w_in, conv_w=conv_w, conv_b=conv_b, dt_bias=dt_bias, a_log=a_log, d_skip=d_skip,
              ssd_norm=ssd_norm, w_ssd_branch=w_ssd_branch, w_attn_branch=w_attn_branch, w_out=w_out, norm_ffn=norm_ffn,
              w_gate_up=w_gate_up, w_down=w_down, norm_final=norm_final)
    mv = dict(norm_mix=m_norm_mix, w_in=m_w_in, conv_w=m_conv_w, conv_b=m_conv_b, dt_bias=m_dt_bias, a_log=m_a_log,
              d_skip=m_d_skip, ssd_norm=m_ssd_norm, w_ssd_branch=m_w_ssd_branch, w_attn_branch=m_w_attn_branch,
              w_out=m_w_out, norm_ffn=m_norm_ffn, w_gate_up=m_w_gate_up, w_down=m_w_down, norm_final=m_norm_final)
    vv = dict(norm_mix=v_norm_mix, w_in=v_w_in, conv_w=v_conv_w, conv_b=v_conv_b, dt_bias=v_dt_bias, a_log=v_a_log,
              d_skip=v_d_skip, ssd_norm=v_ssd_norm, w_ssd_branch=v_w_ssd_branch, w_attn_branch=v_w_attn_branch,
              w_out=v_w_out, norm_ffn=v_norm_ffn, w_gate_up=v_w_gate_up, w_down=v_w_down, norm_final=v_norm_final)
    order = ("norm_mix", "w_in", "conv_w", "conv_b", "dt_bias", "a_log", "d_skip", "ssd_norm", "w_ssd_branch",
             "w_attn_branch", "w_out", "norm_ffn", "w_gate_up", "w_down", "norm_final")

    me_index = _index(_my_place())
    smalls = [{k: wv[k][li] for k in SMALL} for li in range(DEPTH)]
    n_groups = len(GROUPS)

    first_lands = all_gather([to_wire(k, wv[k][0]) for k in GROUPS[0]], "gather_first")
    later = [(li, gi) for li in range(DEPTH) for gi in range(n_groups)][1:]
    behind_first = first_lands[1][0, 0, 0] * 0.0
    srcs = [to_wire(k, wv[k][li] + behind_first if k == "conv_w" else wv[k][li]) for li, gi in later for k in GROUPS[gi]]
    sizes = [len(GROUPS[gi]) for _, gi in later]
    w_sems, w_srcs, w_lands, token = exchange_start(srcs, [landing_zone(s, me_index) for s in srcs], sizes, False,
                                                    "gather_start", peers=CHIP_FLIPS)
    smalls[0]["norm_mix"] = smalls[0]["norm_mix"] + token[0, 0]
    second_leg = {}

    def forward(slot, after):
        if slot < len(later) and slot not in second_leg:
            sl = slice(sum(sizes[:slot]), sum(sizes[:slot + 1]))
            second_leg[slot] = gather_forward(w_srcs[sl], w_lands[sl], w_sems[slot], after, f"gather_forward_{slot}")

    def prefetch(li, gi, after):
        if (li, gi) == later[0]:
            forward(0, after)

    def getw(li, gi, after):
        if (li, gi) == (0, 0):
            lands = first_lands
        else:
            slot = later.index((li, gi))
            forward(slot, after)
            sems2, lands2 = second_leg[slot]
            lands = gather_finish(lands2, sems2, after, f"gather_finish_{li}_{gi}")
            forward(slot + 1, lands[0])
        w = {}
        for k, land in zip(GROUPS[gi], lands):
            w.update(full_weights(k, land))
        return w

    pending = []

    def emit(li, gi, gw):
        parts = [grads_to_wire(k, gw[k]) for k in GROUPS[gi]]
        lands = [landing_zone(lax.dynamic_index_in_dim(p, me_index, 0, keepdims=False), me_index) for p in parts]
        sems, p_thru, l_thru, tok = exchange_start(parts, lands, [len(parts)], True, f"grads_start_{li}_{gi}")
        pending.append((li, gi, sems[0], p_thru, l_thru))
        return tok[0, 0]

    loss_p, dx, gsms, g_final = local_step(x[0], loss_target[0], getw, prefetch, emit, smalls, norm_final)

    grads, deltas, new_m, new_v = {}, {}, {}, {}

    def update(k):
        if k == "w_in":
            inner = lambda t: t.transpose(2, 0, 1)
            outs = adamw_layer_inner(inner(wv[k]), shard_g[k], inner(mv[k]), inner(vv[k]), "adamw_" + k)
            grads[k], deltas[k], new_m[k], new_v[k] = (t.transpose(1, 2, 0) for t in outs)
            return outs[3]
        if k in BIG:
            grads[k] = jnp.stack([g.T if k in TRANSPOSED else g for g in shard_g[k]])
        deltas[k], new_m[k], new_v[k] = adamw(wv[k], grads[k], mv[k], vv[k], "adamw_" + k)
        return new_v[k]

    shard_g = {k: [None] * DEPTH for k in BIG}

    def collect(entry, after):
        li, gi, sems, p_thru, l_thru = entry
        recv = exchange_wait(p_thru, l_thru, sems, after, True, f"grads_wait_{li}_{gi}")
        for k, r in zip(GROUPS[gi], recv):
            if k == "conv_w":
                r = r.reshape(N_DEV, 1, -1)
            after = sum_parts(r, f"sum_{k}_{li}", row_major_3d=(k == "w_in"))
            shard_g[k][li] = after if k in TRANSPOSED else after.reshape(wv[k].shape[1:])
        return after

    after = dx
    for entry in pending[:-1]:
        after = collect(entry, after)
    done = [after[:1, :1].reshape(1)]
    for gi in (2, 1):
        for k in GROUPS[gi]:
            done.append(update(k).reshape(-1)[:1])

    flat = [gsms[li][k] for li in range(DEPTH) for k in SMALL] + [g_final, loss_p.reshape(-1)]
    flat.append(jnp.zeros((SMALL_ROWS * FLAT_W - SMALL_TOTAL,), F32))
    small_all = all_gather([jnp.concatenate(flat).reshape(SMALL_ROWS, FLAT_W)], "gather_small")[0]
    small_sum = sum_parts(small_all, "sum_small").reshape(-1)
    off = 0
    per_layer = {k: [] for k in SMALL}
    for li in range(DEPTH):
        for k in SMALL:
            per_layer[k].append(small_sum[off:off + SMALL_SIZE[k]])
            off += SMALL_SIZE[k]
    for k in SMALL:
        grads[k] = jnp.stack(per_layer[k])
    grads["norm_final"] = small_sum[off:off + D_MODEL]
    loss = small_sum[off + D_MODEL]
    for k in (*SMALL, "norm_final"):
        done.append(update(k).reshape(-1)[:1])

    collect(pending[-1], jnp.concatenate(done))
    for k in GROUPS[0]:
        update(k)

    return (loss, dx.reshape(x.shape), *[grads[k] for k in order], *[deltas[k] for k in order],
            *[new_m[k] for k in order], *[new_v[k] for k in order])
```

```python
import functools

import jax
import jax.numpy as jnp
from jax import lax
from jax.experimental import pallas as pl
from jax.experimental.pallas import tpu as pltpu

F32, BF16 = jnp.float32, jnp.bfloat16
SDS = jax.ShapeDtypeStruct
MESH = pl.DeviceIdType.MESH

D_MODEL = 1024
SEQ = 2048
DEPTH = 2
RMS_EPS = 1e-5
SSD_INNER = 2048
SSD_HEAD_DIM = 64
SSD_HEADS = 32
SSD_STATE = 128
SSD_GROUPS = 4
SSD_CONV = 4
SSD_CHUNK = 128
SSD_CONV_CH = 3072
ATTN_HEAD_DIM = 128
ATTN_KV_HEADS = 8
ATTN_DILATIONS = (1, 4, 16)
ATTN_N_PAT = 3
ATTN_BLOCK = 128
ATTN_OUT = 1024
ROPE_THETA = 500000.0
ROPE_DIM = 32
FFN_HIDDEN = 2816
ADAM_LR, ADAM_B1, ADAM_B2, ADAM_EPS, ADAM_WD, ADAM_STEP = 0.001, 0.9, 0.999, 1e-08, 0.01, 10

N_DEV = 8
LANES = 128
VMEM_LIMIT = 56 * 1024 * 1024
HPAD = 128
HIGHEST = lax.Precision.HIGHEST

IN_ROWS = (("w_z", 2048), ("w_xbc", 3072), ("w_dt", 32), ("w_q0", 1024), ("w_q1", 1024), ("w_q2", 1024),
           ("w_k", 1024), ("w_v", 1024), ("w_gs", 1024), ("w_ga", 1024))
N_IN = sum(r for _, r in IN_ROWS)


def _cparams(sem):
    return pltpu.CompilerParams(dimension_semantics=sem, vmem_limit_bytes=VMEM_LIMIT)


def _sigmoid(x):
    return 0.5 * jnp.tanh(0.5 * x) + 0.5


def _silu(x):
    return x * _sigmoid(x)


def _softplus(x):
    return jnp.maximum(x, 0.0) + jnp.log(1.0 + jnp.exp(-jnp.abs(x)))


def _dot(a, b, dims=(((1,), (0,)), ((), ())), precision=None):
    return lax.dot_general(a, b, dims, precision=precision, preferred_element_type=F32)


NT = (((1,), (1,)), ((), ()))
TN = (((0,), (0,)), ((), ()))


def _bdot(a, b, dims=(((1,), (0,)), ((), ()))):
    return _dot(a.astype(BF16), b.astype(BF16), dims)


def _pick(dim, cands):
    for c in cands:
        if dim % c == 0:
            return c
    return dim


def matmul(a, b, *, name, ta=False, tb=False, out_dtype=F32, add=None):
    m, k = (a.shape[1], a.shape[0]) if ta else a.shape
    n = b.shape[0] if tb else b.shape[1]
    tn = _pick(n, (1024, 1408, 512, 256, 128))
    tm = _pick(m, (512, 1408, 256, 128)) if tn == n else _pick(m, (1024, 1408, 512, 256, 128))
    tk = _pick(k, (1024, 1408, 512, 256, 128))
    nk = k // tk
    a_spec = pl.BlockSpec((tk, tm), lambda i, j, kk: (kk, i)) if ta else pl.BlockSpec((tm, tk), lambda i, j, kk: (i, kk))
    b_spec = pl.BlockSpec((tn, tk), lambda i, j, kk: (j, kk)) if tb else pl.BlockSpec((tk, tn), lambda i, j, kk: (kk, j))
    dims = (((0 if ta else 1,), (1 if tb else 0,)), ((), ()))
    has_add = add is not None

    def body(*refs):
        a_ref, b_ref = refs[:2]
        add_ref = refs[2] if has_add else None
        o_ref = refs[3] if has_add else refs[2]
        acc = refs[-1] if nk > 1 else None
        kk = pl.program_id(2)

        def product():
            return _dot(a_ref[...].astype(BF16), b_ref[...].astype(BF16), dims)

        def finish(r):
            if has_add:
                r = r + add_ref[...].astype(F32)
            o_ref[...] = r.astype(o_ref.dtype)

        if nk == 1:
            finish(product())
            return

        @pl.when(kk == 0)
        def _():
            acc[...] = product()

        @pl.when((kk > 0) & (kk < nk - 1))
        def _():
            acc[...] += product()

        @pl.when(kk == nk - 1)
        def _():
            finish(acc[...] + product())

    in_specs = [a_spec, b_spec]
    args = [a, b]
    if has_add:
        in_specs.append(pl.BlockSpec((tm, tn), lambda i, j, kk: (i, j)))
        args.append(add)
    return pl.pallas_call(
        body, name=name, grid=(m // tm, n // tn, nk),
        in_specs=in_specs, out_specs=pl.BlockSpec((tm, tn), lambda i, j, kk: (i, j)),
        out_shape=SDS((m, n), out_dtype), scratch_shapes=[pltpu.VMEM((tm, tn), F32)] if nk > 1 else [],
        compiler_params=_cparams(("parallel", "parallel", "arbitrary")),
    )(*args)


def rowcall(name, fn, rows, params, row_outs, red_outs=(), tr=256):
    s = rows[0].shape[0]
    n_in = len(rows) + len(params)
    n_row = len(row_outs)

    def body(*refs):
        outs = fn(*[r[...].astype(F32) for r in refs[:n_in]])
        if not isinstance(outs, (tuple, list)):
            outs = (outs,)
        orefs = refs[n_in:]
        for r, o in zip(orefs[:n_row], outs[:n_row]):
            r[...] = o.astype(r.dtype)
        if red_outs:
            @pl.when(pl.program_id(0) == 0)
            def _():
                for r in orefs[n_row:]:
                    r[...] = jnp.zeros_like(r)
            for r, o in zip(orefs[n_row:], outs[n_row:]):
                r[...] += o.astype(F32)

    widths = [a[1] if isinstance(a, tuple) else a.shape[1] for a in rows]
    rows = [a[0] if isinstance(a, tuple) else a for a in rows]
    in_specs = [pl.BlockSpec((tr, wd), lambda i: (i, 0)) for wd in widths]
    in_specs += [pl.BlockSpec(p.shape, lambda i: (0, 0)) for p in params]
    out_specs = [pl.BlockSpec((tr, c), lambda i: (i, 0)) for c, _ in row_outs]
    out_specs += [pl.BlockSpec(shp, lambda i: (0, 0)) for shp in red_outs]
    out_shape = [SDS((s, c), dt) for c, dt in row_outs] + [SDS(shp, F32) for shp in red_outs]
    res = pl.pallas_call(
        body, name=name, grid=(s // tr,), in_specs=in_specs, out_specs=out_specs, out_shape=out_shape,
        compiler_params=_cparams(("arbitrary",) if red_outs else ("parallel",)),
    )(*rows, *params)
    return res


def _rms(x, w):
    return x * lax.rsqrt(jnp.mean(x * x, axis=-1, keepdims=True) + RMS_EPS) * w


def rms_fwd(h, w, name):
    return rowcall(name, _rms, [h], [w], [(D_MODEL, BF16)])[0]


def rms_bwd(h, du, dres, w, name):
    def fn(hb, dub, dresb, wb):
        _, vjp = jax.vjp(_rms, hb, wb)
        dh, dw = vjp(dub)
        return dh + dresb, dw
    return rowcall(name, fn, [h, du, dres], [w], [(D_MODEL, F32)], [(1, D_MODEL)])


def loss_head(h, target, w, name):
    def fn(hb, tb, wb):
        def f(hh, ww):
            err = _rms(hh, ww) - tb
            return 0.5 * jnp.sum(jnp.mean(err * err, axis=-1, keepdims=True), axis=0, keepdims=True)
        val, vjp = jax.vjp(f, hb, wb)
        dh, dw = vjp(jnp.ones((1, 1), F32))
        return dh, dw, jnp.broadcast_to(val, (1, LANES))
    return rowcall(name, fn, [h, target], [w], [(D_MODEL, F32)], [(1, D_MODEL), (1, LANES)])


def _gate(a, b, gs, ga):
    return _sigmoid(gs) * a + _sigmoid(ga) * b


def gate_fwd(a, b, gs, ga, name):
    return rowcall(name, _gate, [a, b, gs, ga], [], [(D_MODEL, BF16)])[0]


def gate_bwd(a, b, gs, ga, dm, name):
    def fn(ab, bb, gsb, gab, dmb):
        _, vjp = jax.vjp(_gate, ab, bb, gsb, gab)
        return vjp(dmb)
    return rowcall(name, fn, [a, b, gs, ga, dm], [], [(D_MODEL, BF16)] * 4)


def _swiglu(gu):
    return _silu(gu[:, :FFN_HIDDEN]) * gu[:, FFN_HIDDEN:]


def swiglu_fwd(gu, name):
    return rowcall(name, _swiglu, [gu], [], [(FFN_HIDDEN, BF16)])[0]


def swiglu_bwd(gu, dact, name):
    def fn(gub, db):
        _, vjp = jax.vjp(_swiglu, gub)
        return vjp(db.astype(F32))[0]
    return rowcall(name, fn, [gu, dact], [], [(2 * FFN_HIDDEN, BF16)])[0]


def _ssd_post(y, xs, z, dskip, normw):
    y = (y + dskip * xs) * _silu(z)
    gw = SSD_INNER // SSD_GROUPS
    parts = []
    for g in range(SSD_GROUPS):
        yg = y[:, g * gw:(g + 1) * gw]
        parts.append(yg * lax.rsqrt(jnp.mean(yg * yg, axis=-1, keepdims=True) + RMS_EPS))
    return jnp.concatenate(parts, axis=-1) * normw


def ssd_post_fwd(y, xc, z, dskip, normw, name):
    return rowcall(name, _ssd_post, [y, (xc, SSD_INNER), z], [dskip, normw], [(SSD_INNER, BF16)])[0]


def ssd_post_bwd(y, xc, z, dskip, normw, dyn, name):
    def fn(yb, xsb, zb, dynb, db, nb):
        _, vjp = jax.vjp(_ssd_post, yb, xsb, zb, db, nb)
        return vjp(dynb)
    return rowcall(name, fn, [y, (xc, SSD_INNER), z, dyn], [dskip, normw],
                   [(SSD_INNER, F32), (SSD_INNER, F32), (SSD_INNER, BF16)], [(1, SSD_INNER), (1, SSD_INNER)])


def _rope(t, cosf, sina, sinb):
    return t * cosf + pltpu.roll(t, LANES - ROPE_DIM // 2, 1) * sina + pltpu.roll(t, ROPE_DIM // 2, 1) * sinb


def rope_tables():
    half = ROPE_DIM // 2
    inv = ROPE_THETA ** (-jnp.arange(0, ROPE_DIM, 2, dtype=F32) / ROPE_DIM)
    ang = jnp.arange(SEQ, dtype=F32)[:, None] * inv[None, :]
    cos, sin = jnp.cos(ang), jnp.sin(ang)
    zeros = jnp.zeros((SEQ, LANES - ROPE_DIM), F32)
    z16 = jnp.zeros((SEQ, half), F32)
    cosf = jnp.concatenate([cos, cos, jnp.ones((SEQ, LANES - ROPE_DIM), F32)], axis=1)
    sina = jnp.concatenate([-sin, z16, zeros], axis=1)
    sinb = jnp.concatenate([z16, sin, zeros], axis=1)
    return cosf, sina, sinb


CONV_TC = 256


def _conv_pre(x, w, b, row):
    acc = x * w[SSD_CONV - 1:SSD_CONV, :] + b
    shifted = [x]
    for j in range(1, SSD_CONV):
        xs = jnp.where(row >= j, pltpu.roll(x, j, 0), 0.0)
        shifted.append(xs)
        acc = acc + xs * w[SSD_CONV - 1 - j:SSD_CONV - j, :]
    return acc, shifted


def conv_fwd(xbc, w, b, name):
    def body(x_ref, w_ref, b_ref, o_ref):
        row = lax.broadcasted_iota(jnp.int32, (SEQ, CONV_TC), 0)
        pre, _ = _conv_pre(x_ref[...].astype(F32), w_ref[...], b_ref[...], row)
        o_ref[...] = _silu(pre)
    return pl.pallas_call(
        body, name=name, grid=(SSD_CONV_CH // CONV_TC,),
        in_specs=[pl.BlockSpec((SEQ, CONV_TC), lambda i: (0, i)), pl.BlockSpec((SSD_CONV, CONV_TC), lambda i: (0, i)),
                  pl.BlockSpec((1, CONV_TC), lambda i: (0, i))],
        out_specs=pl.BlockSpec((SEQ, CONV_TC), lambda i: (0, i)),
        out_shape=SDS((SEQ, SSD_CONV_CH), F32), compiler_params=_cparams(("parallel",)),
    )(xbc, w, b)


def conv_bwd(xbc, w, b, dxc, name):
    def body(x_ref, w_ref, b_ref, dy_ref, dx_ref, dw_ref, db_ref):
        row = lax.broadcasted_iota(jnp.int32, (SEQ, CONV_TC), 0)
        wv = w_ref[...]
        pre, shifted = _conv_pre(x_ref[...].astype(F32), wv, b_ref[...], row)
        sg = _sigmoid(pre)
        ds = dy_ref[...] * (sg * (1.0 + pre * (1.0 - sg)))
        dx = ds * wv[SSD_CONV - 1:SSD_CONV, :]
        for j in range(1, SSD_CONV):
            dsj = jnp.where(row < SEQ - j, pltpu.roll(ds, SEQ - j, 0), 0.0)
            dx = dx + dsj * wv[SSD_CONV - 1 - j:SSD_CONV - j, :]
        dx_ref[...] = dx.astype(dx_ref.dtype)
        for j in range(SSD_CONV):
            dw_ref[SSD_CONV - 1 - j:SSD_CONV - j, :] = jnp.sum(ds * shifted[j], axis=0, keepdims=True)
        db_ref[...] = jnp.sum(ds, axis=0, keepdims=True)
    return pl.pallas_call(
        body, name=name, grid=(SSD_CONV_CH // CONV_TC,),
        in_specs=[pl.BlockSpec((SEQ, CONV_TC), lambda i: (0, i)), pl.BlockSpec((SSD_CONV, CONV_TC), lambda i: (0, i)),
                  pl.BlockSpec((1, CONV_TC), lambda i: (0, i)), pl.BlockSpec((SEQ, CONV_TC), lambda i: (0, i))],
        out_specs=[pl.BlockSpec((SEQ, CONV_TC), lambda i: (0, i)), pl.BlockSpec((SSD_CONV, CONV_TC), lambda i: (0, i)),
                   pl.BlockSpec((1, CONV_TC), lambda i: (0, i))],
        out_shape=[SDS((SEQ, SSD_CONV_CH), BF16), SDS((SSD_CONV, SSD_CONV_CH), F32), SDS((1, SSD_CONV_CH), F32)],
        compiler_params=_cparams(("parallel",)),
    )(xbc, w, b, dxc)


N_CHUNKS = SEQ // SSD_CHUNK
N_PAIRS = SSD_HEADS // 2
PAIRS_PER_GROUP = N_PAIRS // SSD_GROUPS
B_OFF = SSD_INNER
C_OFF = SSD_INNER + SSD_GROUPS * SSD_STATE


def _ssd_prefix(dtr, dtr_t, dtb, dtb_t, alog, alog_t):
    ln = SSD_CHUNK
    dt = _softplus(dtr + dtb)
    dt_t = _softplus(dtr_t + dtb_t)
    dta = dt * (-jnp.exp(alog))
    dta_t = dt_t * (-jnp.exp(alog_t))
    r = lax.broadcasted_iota(jnp.int32, (ln, ln), 0)
    c = lax.broadcasted_iota(jnp.int32, (ln, ln), 1)
    a_cum = _dot((r >= c).astype(F32), dta, precision=HIGHEST)
    a_cum_t = _dot(dta_t, (r <= c).astype(F32), precision=HIGHEST)
    a_last = jnp.sum(dta_t, axis=1, keepdims=True)
    return dt, a_cum, a_cum_t, a_last


def _bein(spec, a, b):
    return jnp.einsum(spec, a.astype(BF16), b.astype(BF16), preferred_element_type=F32)


SSD_GROUPS_PER_BATCH = 4


def _ssd_group(xs3, bgs, cgs, h3, dt, a_cum, a_cum_t, a_last, *, groups):
    ln = SSD_CHUNK
    lane = lax.broadcasted_iota(jnp.int32, (ln, LANES), 1)
    sub = lax.broadcasted_iota(jnp.int32, (LANES, SSD_STATE), 0)
    row = lax.broadcasted_iota(jnp.int32, (ln, ln), 0)
    col = lax.broadcasted_iota(jnp.int32, (ln, ln), 1)
    lo = lane < SSD_HEAD_DIM
    causal = row >= col
    m_lo, m_hi, dts, acs, lasts, cds, cg3, bg3 = [], [], [], [], [], [], [], []
    for g, bg, cg in zip(groups, bgs, cgs):
        cb = _bdot(cg, bg, NT)
        for j in range(PAIRS_PER_GROUP):
            e0 = 2 * (g * PAIRS_PER_GROUP + j)
            e1 = e0 + 1
            c0, c1 = a_cum[:, e0:e0 + 1], a_cum[:, e1:e1 + 1]
            r0, r1 = a_cum_t[e0:e0 + 1, :], a_cum_t[e1:e1 + 1, :]
            l0, l1 = a_last[e0:e0 + 1, :], a_last[e1:e1 + 1, :]
            m_lo.append(cb * jnp.exp(jnp.where(causal, c0 - r0, -jnp.inf)))
            m_hi.append(cb * jnp.exp(jnp.where(causal, c1 - r1, -jnp.inf)))
            dts.append(jnp.where(lo, dt[:, e0:e0 + 1], dt[:, e1:e1 + 1]))
            acs.append(jnp.where(lo, c0, c1))
            lasts.append(jnp.where(lo, l0, l1))
            cds.append(jnp.exp(jnp.where(sub < SSD_HEAD_DIM, l0, l1)))
            cg3.append(cg)
            bg3.append(bg)
    xd = xs3 * jnp.stack(dts)
    acum = jnp.stack(acs)
    y = (_bein("pls,psq->plq", jnp.stack(m_lo), jnp.where(lo[None], xd, 0.0))
         + _bein("pls,psq->plq", jnp.stack(m_hi), jnp.where(lo[None], 0.0, xd)))
    y = y + _bein("pln,pqn->plq", jnp.stack(cg3), h3) * jnp.exp(acum)
    st = _bein("plq,pln->pqn", xd * jnp.exp(jnp.stack(lasts) - acum), jnp.stack(bg3))
    h_out = h3 * jnp.stack(cds) + st
    return y, h_out


def _group_slabs(groups):
    pairs = [g * PAIRS_PER_GROUP + j for g in groups for j in range(PAIRS_PER_GROUP)]
    return [slice(p * LANES, (p + 1) * LANES) for p in pairs]


def _group_batches():
    return [tuple(range(g, g + SSD_GROUPS_PER_BATCH)) for g in range(0, SSD_GROUPS, SSD_GROUPS_PER_BATCH)]


def _bc_of(xc_ref, g):
    return (xc_ref[:, B_OFF + g * SSD_STATE:B_OFF + (g + 1) * SSD_STATE],
            xc_ref[:, C_OFF + g * SSD_STATE:C_OFF + (g + 1) * SSD_STATE])


def _ssd_in_specs(chunk_of):
    return [
        pl.BlockSpec((SSD_CHUNK, SSD_CONV_CH), lambda i: (chunk_of(i), 0)),
        pl.BlockSpec((SSD_CHUNK, HPAD), lambda i: (chunk_of(i), 0)),
        pl.BlockSpec((HPAD, SSD_CHUNK), lambda i: (0, chunk_of(i))),
        pl.BlockSpec((1, HPAD), lambda i: (0, 0)), pl.BlockSpec((HPAD, 1), lambda i: (0, 0)),
        pl.BlockSpec((1, HPAD), lambda i: (0, 0)), pl.BlockSpec((HPAD, 1), lambda i: (0, 0)),
    ]


def ssd_fwd(xc, dtr, dtr_t, dtb, dtb_t, alog, alog_t, name):
    def body(xc_ref, dtr_ref, dtrt_ref, dtb_ref, dtbt_ref, al_ref, alt_ref, y_ref, hs_ref, h_scr):
        @pl.when(pl.program_id(0) == 0)
        def _():
            h_scr[...] = jnp.zeros_like(h_scr)

        hs_ref[0] = h_scr[...]
        dt, a_cum, a_cum_t, a_last = _ssd_prefix(dtr_ref[...], dtrt_ref[...], dtb_ref[...], dtbt_ref[...],
                                                  al_ref[...], alt_ref[...])
        for groups in _group_batches():
            slabs = _group_slabs(groups)
            bgs, cgs = zip(*[_bc_of(xc_ref, g) for g in groups])
            xs3 = jnp.stack([xc_ref[:, sl] for sl in slabs])
            h3 = jnp.stack([h_scr[sl, :] for sl in slabs])
            y3, h3_out = _ssd_group(xs3, bgs, cgs, h3, dt, a_cum, a_cum_t, a_last, groups=groups)
            for j, sl in enumerate(slabs):
                y_ref[:, sl] = y3[j]
                h_scr[sl, :] = h3_out[j]

    return pl.pallas_call(
        body, name=name, grid=(N_CHUNKS,), in_specs=_ssd_in_specs(lambda i: i),
        out_specs=[pl.BlockSpec((SSD_CHUNK, SSD_INNER), lambda i: (i, 0)),
                   pl.BlockSpec((1, SSD_INNER, SSD_STATE), lambda i: (i, 0, 0))],
        out_shape=[SDS((SEQ, SSD_INNER), F32), SDS((N_CHUNKS, SSD_INNER, SSD_STATE), F32)],
        scratch_shapes=[pltpu.VMEM((SSD_INNER, SSD_STATE), F32)],
        compiler_params=_cparams(("arbitrary",)),
    )(xc, dtr, dtr_t, dtb, dtb_t, alog, alog_t)


def ssd_bwd(xc, dtr, dtr_t, dtb, dtb_t, alog, alog_t, hs, dy, dxs_extra, name):
    rev = lambda i: N_CHUNKS - 1 - i

    def body(xc_ref, dtr_ref, dtrt_ref, dtb_ref, dtbt_ref, al_ref, alt_ref, hs_ref, dy_ref, dxe_ref,
             dxc_ref, ddtr_ref, ddtrt_ref, ddtb_ref, ddtbt_ref, dal_ref, dalt_ref, dh_scr):
        @pl.when(pl.program_id(0) == 0)
        def _():
            dh_scr[...] = jnp.zeros_like(dh_scr)
            for r in (ddtb_ref, ddtbt_ref, dal_ref, dalt_ref):
                r[...] = jnp.zeros_like(r)

        prefix_in = (dtr_ref[...], dtrt_ref[...], dtb_ref[...], dtbt_ref[...], al_ref[...], alt_ref[...])
        (dt, a_cum, a_cum_t, a_last), prefix_vjp = jax.vjp(_ssd_prefix, *prefix_in)
        d_dt = jnp.zeros_like(dt)
        d_acum = jnp.zeros_like(a_cum)
        d_acum_t = jnp.zeros_like(a_cum_t)
        d_alast = jnp.zeros_like(a_last)
        for groups in _group_batches():
            slabs = _group_slabs(groups)
            bgs, cgs = zip(*[_bc_of(xc_ref, g) for g in groups])
            xs3 = jnp.stack([xc_ref[:, sl] for sl in slabs])
            h3 = jnp.stack([hs_ref[0, sl, :] for sl in slabs])
            _, vjp = jax.vjp(functools.partial(_ssd_group, groups=groups), xs3, bgs, cgs, h3, dt, a_cum, a_cum_t, a_last)
            dy3 = jnp.stack([dy_ref[:, sl] for sl in slabs])
            dh3 = jnp.stack([dh_scr[sl, :] for sl in slabs])
            dxs3, d_bgs, d_cgs, dh3_in, ddt, dac, dact, dal = vjp((dy3, dh3))
            for j, sl in enumerate(slabs):
                dxc_ref[:, sl] = dxs3[j] + dxe_ref[:, sl]
                dh_scr[sl, :] = dh3_in[j]
            d_dt, d_acum, d_acum_t, d_alast = d_dt + ddt, d_acum + dac, d_acum_t + dact, d_alast + dal
            for g, d_bg, d_cg in zip(groups, d_bgs, d_cgs):
                dxc_ref[:, B_OFF + g * SSD_STATE:B_OFF + (g + 1) * SSD_STATE] = d_bg
                dxc_ref[:, C_OFF + g * SSD_STATE:C_OFF + (g + 1) * SSD_STATE] = d_cg
        g_dtr, g_dtrt, g_dtb, g_dtbt, g_al, g_alt = prefix_vjp((d_dt, d_acum, d_acum_t, d_alast))
        ddtr_ref[...] = g_dtr
        ddtrt_ref[...] = g_dtrt
        ddtb_ref[...] += g_dtb
        ddtbt_ref[...] += g_dtbt
        dal_ref[...] += g_al
        dalt_ref[...] += g_alt

    in_specs = _ssd_in_specs(rev) + [
        pl.BlockSpec((1, SSD_INNER, SSD_STATE), lambda i: (rev(i), 0, 0)),
        pl.BlockSpec((SSD_CHUNK, SSD_INNER), lambda i: (rev(i), 0)),
        pl.BlockSpec((SSD_CHUNK, SSD_INNER), lambda i: (rev(i), 0)),
    ]
    out_specs = [
        pl.BlockSpec((SSD_CHUNK, SSD_CONV_CH), lambda i: (rev(i), 0)),
        pl.BlockSpec((SSD_CHUNK, HPAD), lambda i: (rev(i), 0)),
        pl.BlockSpec((HPAD, SSD_CHUNK), lambda i: (0, rev(i))),
        pl.BlockSpec((1, HPAD), lambda i: (0, 0)), pl.BlockSpec((HPAD, 1), lambda i: (0, 0)),
        pl.BlockSpec((1, HPAD), lambda i: (0, 0)), pl.BlockSpec((HPAD, 1), lambda i: (0, 0)),
    ]
    out_shape = [SDS((SEQ, SSD_CONV_CH), F32), SDS((SEQ, HPAD), F32), SDS((HPAD, SEQ), F32),
                 SDS((1, HPAD), F32), SDS((HPAD, 1), F32), SDS((1, HPAD), F32), SDS((HPAD, 1), F32)]
    return pl.pallas_call(
        body, name=name, grid=(N_CHUNKS,), in_specs=in_specs, out_specs=out_specs, out_shape=out_shape,
        scratch_shapes=[pltpu.VMEM((SSD_INNER, SSD_STATE), F32)],
        compiler_params=_cparams(("arbitrary",)),
    )(xc, dtr, dtr_t, dtb, dtb_t, alog, alog_t, hs, dy, dxs_extra)


ATTN_SCALE = ATTN_HEAD_DIM ** -0.5


UNITS_PER_PATTERN = SEQ // ATTN_BLOCK
ATTN_BATCH = 8


def _for_unit_batches(batch):
    for g, d in enumerate(ATTN_DILATIONS):
        nb = UNITS_PER_PATTERN // d
        span = d * ATTN_BLOCK

        def trip(t, carry, g=g, d=d, nb=nb, span=span):
            units = []
            for j in range(ATTN_BATCH):
                i = t * ATTN_BATCH + j
                r = i >> (nb.bit_length() - 1)
                n = i & (nb - 1)
                start = r + n * span
                prev = jnp.where(n > 0, start - span, start)
                units.append((pl.ds(start, ATTN_BLOCK, stride=d), pl.ds(prev, ATTN_BLOCK, stride=d), n > 0))
            batch(g, units)
            return carry
        lax.fori_loop(0, UNITS_PER_PATTERN // ATTN_BATCH, trip, 0)


def _unit_operands(units, q_scr, k_scr, v_scr):
    def pair(scr, rows, prows):
        return jnp.concatenate([scr[prows, :], scr[rows, :]], axis=0)
    qb = jnp.stack([q_scr[rows, :] for rows, _, _ in units]).astype(BF16)
    kb = jnp.stack([pair(k_scr, rows, prows) for rows, prows, _ in units]).astype(BF16)
    vb = jnp.stack([pair(v_scr, rows, prows) for rows, prows, _ in units]).astype(BF16)
    return qb, kb, vb


def _unit_scores(qb, kb, units):
    s = jnp.einsum("bqd,bkd->bqk", qb, kb, preferred_element_type=F32) * ATTN_SCALE
    qi = lax.broadcasted_iota(jnp.int32, (ATTN_BLOCK, 2 * ATTN_BLOCK), 0)
    kj = lax.broadcasted_iota(jnp.int32, (ATTN_BLOCK, 2 * ATTN_BLOCK), 1)
    own = (kj >= ATTN_BLOCK) & (kj - ATTN_BLOCK <= qi)
    before = (kj < ATTN_BLOCK) & (kj >= qi)
    keep = jnp.stack([own | (before & has_prev) for _, _, has_prev in units])
    return jnp.where(keep, s, -jnp.inf)


def _head_specs(n_q_groups):
    blk = (SEQ, ATTN_HEAD_DIM)
    q_specs = [pl.BlockSpec(blk, functools.partial(lambda h, g: (0, g * ATTN_KV_HEADS + h), g=g)) for g in range(n_q_groups)]
    head = pl.BlockSpec(blk, lambda h: (0, h))
    table = pl.BlockSpec(blk, lambda h: (0, 0))
    return q_specs, head, table


def attn_fwd(q, k, v, tabs, name):
    q_specs, head, table = _head_specs(ATTN_N_PAT)

    def body(q0_ref, q1_ref, q2_ref, k_ref, v_ref, c_ref, sa_ref, sb_ref, y_ref, lse_ref, *scr):
        qs, og, ls, ks, vs = scr[0:3], scr[3:6], scr[6:9], scr[9], scr[10]
        c, sa, sb = c_ref[...], sa_ref[...], sb_ref[...]
        for g, q_ref in enumerate((q0_ref, q1_ref, q2_ref)):
            qs[g][...] = _rope(q_ref[...].astype(F32), c, sa, sb)
        ks[...] = _rope(k_ref[...].astype(F32), c, sa, sb)
        vs[...] = v_ref[...].astype(F32)

        def batch(g, units):
            qb, kb, vb = _unit_operands(units, qs[g], ks, vs)
            s = _unit_scores(qb, kb, units)
            m = jnp.max(s, axis=2, keepdims=True)
            p = jnp.exp(s - m)
            l = jnp.sum(p, axis=2, keepdims=True)
            o = jnp.einsum("bqk,bkd->bqd", p.astype(BF16), vb, preferred_element_type=F32) / l
            lse_b = m + jnp.log(l)
            for j, (rows, _, _) in enumerate(units):
                og[g][rows, :] = o[j]
                ls[g][rows, :] = jnp.broadcast_to(lse_b[j], (ATTN_BLOCK, LANES))

        _for_unit_batches(batch)
        l0, l1, l2 = ls[0][...], ls[1][...], ls[2][...]
        m = jnp.maximum(jnp.maximum(l0, l1), l2)
        e0, e1, e2 = jnp.exp(l0 - m), jnp.exp(l1 - m), jnp.exp(l2 - m)
        den = e0 + e1 + e2
        y_ref[...] = ((e0 * og[0][...] + e1 * og[1][...] + e2 * og[2][...]) / den).astype(y_ref.dtype)
        lse_ref[...] = m + jnp.log(den)

    blk = (SEQ, ATTN_HEAD_DIM)
    return pl.pallas_call(
        body, name=name, grid=(ATTN_KV_HEADS,), in_specs=[*q_specs, head, head, table, table, table],
        out_specs=[head, head], out_shape=[SDS((SEQ, ATTN_OUT), BF16), SDS((SEQ, ATTN_OUT), F32)],
        scratch_shapes=[pltpu.VMEM(blk, F32)] * (3 * ATTN_N_PAT + 2),
        compiler_params=_cparams(("parallel",)),
    )(q, q, q, k, v, *tabs)


def attn_bwd(q, k, v, tabs, y, lse, dy, name):
    q_specs, head, table = _head_specs(ATTN_N_PAT)

    def body(q0_ref, q1_ref, q2_ref, k_ref, v_ref, c_ref, sa_ref, sb_ref, y_ref, lse_ref, dy_ref,
             dq0_ref, dq1_ref, dq2_ref, dk_ref, dv_ref, *scr):
        qs, dqs, ks, dks, dd, dvs, vs = scr[0:3], scr[3:6], scr[6], scr[7], scr[8], scr[9], scr[10]
        c, sa, sb = c_ref[...], sa_ref[...], sb_ref[...]
        for g, q_ref in enumerate((q0_ref, q1_ref, q2_ref)):
            qs[g][...] = _rope(q_ref[...].astype(F32), c, sa, sb)
        ks[...] = _rope(k_ref[...].astype(F32), c, sa, sb)
        vs[...] = v_ref[...].astype(F32)
        dks[...] = jnp.zeros_like(dks)
        dvs[...] = jnp.zeros_like(dvs)
        dyv = dy_ref[...]
        dd[...] = jnp.broadcast_to(jnp.sum(dyv * y_ref[...].astype(F32), axis=1, keepdims=True), dd.shape)

        def batch(g, units):
            qb, kb, vb = _unit_operands(units, qs[g], ks, vs)
            dob = jnp.stack([dy_ref[rows, :] for rows, _, _ in units]).astype(BF16)
            lse_b = jnp.stack([lse_ref[rows, :][:, 0:1] for rows, _, _ in units])
            dsum_b = jnp.stack([dd[rows, :][:, 0:1] for rows, _, _ in units])
            p = jnp.exp(_unit_scores(qb, kb, units) - lse_b)
            dp = jnp.einsum("bqd,bkd->bqk", dob, vb, preferred_element_type=F32)
            ds = (p * (dp - dsum_b) * ATTN_SCALE).astype(BF16)
            dq = jnp.einsum("bqk,bkd->bqd", ds, kb, preferred_element_type=F32)
            dk = jnp.einsum("bqk,bqd->bkd", ds, qb, preferred_element_type=F32)
            dv = jnp.einsum("bqk,bqd->bkd", p.astype(BF16), dob, preferred_element_type=F32)
            for j, (rows, prows, _) in enumerate(units):
                dqs[g][rows, :] = dq[j]
                dks[prows, :] += dk[j, :ATTN_BLOCK]
                dks[rows, :] += dk[j, ATTN_BLOCK:]
                dvs[prows, :] += dv[j, :ATTN_BLOCK]
                dvs[rows, :] += dv[j, ATTN_BLOCK:]

        _for_unit_batches(batch)
        for g, dq_ref in enumerate((dq0_ref, dq1_ref, dq2_ref)):
            dq_ref[...] = _rope(dqs[g][...], c, -sa, -sb).astype(dq_ref.dtype)
        dk_ref[...] = _rope(dks[...], c, -sa, -sb).astype(dk_ref.dtype)
        dv_ref[...] = dvs[...].astype(dv_ref.dtype)

    blk = (SEQ, ATTN_HEAD_DIM)
    out = SDS((SEQ, ATTN_OUT), BF16)
    return pl.pallas_call(
        body, name=name, grid=(ATTN_KV_HEADS,), in_specs=[*q_specs, head, head, table, table, table, head, head, head],
        out_specs=[head] * 5, out_shape=[out] * 5,
        scratch_shapes=[pltpu.VMEM(blk, F32)] * (2 * ATTN_N_PAT + 5),
        compiler_params=_cparams(("parallel",)),
    )(q, q, q, k, v, *tabs, y, lse, dy)


def layer_fwd(h, getw, prefetch, small, tabs, li):
    n = f"l{li}_"
    sv = {}
    w = dict(getw(0, h))
    u = rms_fwd(h, small["norm_mix"], n + "rms_mix")
    z = matmul(u, w["w_z"], name=n + "mm_z", tb=True, out_dtype=BF16)
    prefetch(1, z)
    xbc = matmul(u, w["w_xbc"], name=n + "mm_xbc", tb=True, out_dtype=BF16)
    dtr = matmul(u, w["w_dt"], name=n + "mm_dt", tb=True)
    q = matmul(u, w["w_q"], name=n + "mm_q", tb=True, out_dtype=BF16)
    k = matmul(u, w["w_k"], name=n + "mm_k", tb=True, out_dtype=BF16)
    v = matmul(u, w["w_v"], name=n + "mm_v", tb=True, out_dtype=BF16)
    gs = matmul(u, w["w_gs"], name=n + "mm_gs", tb=True, out_dtype=BF16)
    ga = matmul(u, w["w_ga"], name=n + "mm_ga", tb=True, out_dtype=BF16)
    xc = conv_fwd(xbc, w["conv_w"], small["conv_b"], n + "conv")
    dtr_t = dtr.T
    y_ssd, hs = ssd_fwd(xc, dtr, dtr_t, small["dt_bias"], small["dt_bias"].T, small["a_log"], small["a_log"].T, n + "ssd")
    yn = ssd_post_fwd(y_ssd, xc, z, small["d_skip_x"], small["ssd_norm"], n + "ssd_post")
    y_attn, lse = attn_fwd(q, k, v, tabs, n + "attn")
    w.update(getw(1, y_ssd))
    a = matmul(yn, w["w_ssd_branch"], name=n + "mm_a", out_dtype=BF16)
    b = matmul(y_attn, w["w_attn_branch"], name=n + "mm_b", out_dtype=BF16)
    merged = gate_fwd(a, b, gs, ga, n + "gate")
    h1 = matmul(merged, w["w_out"], name=n + "mm_o", add=h)
    w.update(getw(2, h1))
    u2 = rms_fwd(h1, small["norm_ffn"], n + "rms_ffn")
    gu = matmul(u2, w["w_gate_up"], name=n + "mm_gu", tb=True, out_dtype=BF16)
    act = swiglu_fwd(gu, n + "swiglu")
    h2 = matmul(act, w["w_down"], name=n + "mm_down", add=h1)
    sv.update(h=h, u=u, z=z, xbc=xbc, dtr=dtr, dtr_t=dtr_t, gs=gs, ga=ga, xc=xc, y_ssd=y_ssd, hs=hs, yn=yn,
              q=q, k=k, v=v, y_attn=y_attn, lse=lse, a=a, b=b, merged=merged, h1=h1, u2=u2, gu=gu, act=act, w=w)
    return h2, sv


def layer_bwd(dh, sv, small, tabs, li, emit):
    n = f"l{li}_b_"
    w = sv["w"]
    gw, gsm = {}, {}
    dact = matmul(dh, w["w_down"], name=n + "mm_dact", tb=True, out_dtype=BF16)
    gw["w_down"] = matmul(sv["act"], dh, name=n + "mm_dwdown", ta=True, out_dtype=BF16)
    dgu = swiglu_bwd(sv["gu"], dact, n + "swiglu")
    gw["w_gate_up"] = matmul(dgu, sv["u2"], name=n + "mm_dwgu", ta=True, out_dtype=BF16)
    tok = emit(2, gw)
    du2 = matmul(dgu, w["w_gate_up"], name=n + "mm_du2")
    dh1, gsm["norm_ffn"] = rms_bwd(sv["h1"], du2, dh, small["norm_ffn"] + tok, n + "rms_ffn")
    dmerged = matmul(dh1, w["w_out"], name=n + "mm_dmerged", tb=True)
    gw["w_out"] = matmul(sv["merged"], dh1, name=n + "mm_dwo", ta=True, out_dtype=BF16)
    da, db, dgs, dga = gate_bwd(sv["a"], sv["b"], sv["gs"], sv["ga"], dmerged, n + "gate")
    gw["w_ssd_branch"] = matmul(sv["yn"], da, name=n + "mm_dwa", ta=True, out_dtype=BF16)
    gw["w_attn_branch"] = matmul(sv["y_attn"], db, name=n + "mm_dwb", ta=True, out_dtype=BF16)
    tok = emit(1, gw)
    dyn = matmul(da, w["w_ssd_branch"], name=n + "mm_dyn", tb=True)
    dyattn = matmul(db, w["w_attn_branch"], name=n + "mm_dyattn", tb=True)
    dy_ssd, dxs_extra, dz, gsm["d_skip_x"], gsm["ssd_norm"] = ssd_post_bwd(
        sv["y_ssd"], sv["xc"], sv["z"], small["d_skip_x"] + tok, small["ssd_norm"], dyn, n + "ssd_post")
    dxc, ddtr, ddtr_t, ddtb, ddtb_t, dal, dal_t = ssd_bwd(
        sv["xc"], sv["dtr"], sv["dtr_t"], small["dt_bias"], small["dt_bias"].T, small["a_log"], small["a_log"].T,
        sv["hs"], dy_ssd, dxs_extra, n + "ssd")
    ddtr = (ddtr + ddtr_t.T).astype(BF16)
    gsm["dt_bias"] = ddtb + ddtb_t.T
    gsm["a_log"] = dal + dal_t.T
    dxbc, gw["conv_w"], gsm["conv_b"] = conv_bwd(sv["xbc"], w["conv_w"], small["conv_b"], dxc, n + "conv")
    dq0, dq1, dq2, dk, dv = attn_bwd(sv["q"], sv["k"], sv["v"], tabs, sv["y_attn"], sv["lse"], dyattn, n + "attn")
    u = sv["u"]
    segs = [("w_z", dz), ("w_xbc", dxbc), ("w_dt", ddtr), ("w_q0", dq0), ("w_q1", dq1), ("w_q2", dq2),
            ("w_k", dk), ("w_v", dv), ("w_gs", dgs), ("w_ga", dga)]
    gin = [matmul(dseg, u, name=n + "mm_d" + key, ta=True, out_dtype=BF16) for key, dseg in segs]
    gin[2] = gin[2][:SSD_HEADS]
    gw["w_in"] = jnp.concatenate(gin, axis=0)
    tok = emit(0, gw)
    du = jnp.zeros((SEQ, D_MODEL), F32) + tok
    for key, dseg in segs:
        du = matmul(dseg, w[key], name=n + "mm_du_" + key, add=du)
    dh0, gsm["norm_mix"] = rms_bwd(sv["h"], du, dh1, small["norm_mix"] + tok, n + "rms_mix")
    return dh0, gsm


def _my_place():
    return lax.axis_index("x"), lax.axis_index("y"), lax.axis_index("c")


def _flip(place, k):
    x, y, c = place
    return (1 - x if k & 4 else x, 1 - y if k & 2 else y, 1 - c if k & 1 else c)


def _index(place):
    return 4 * place[0] + 2 * place[1] + place[2]


ANY = pl.BlockSpec(memory_space=pl.ANY)
CHIP_FLIPS = (4, 2, 6)
SELF_AND_CHIPS = (0,) + CHIP_FLIPS


def all_gather(xs, name):
    na = len(xs)

    def body(*refs):
        x_refs, o_refs = refs[:na], refs[na:2 * na]
        send_sems, recv_sems, local_sems = refs[2 * na:]
        me = _my_place()
        sibling = _flip(me, 1)
        chips = [_flip(me, f) for f in CHIP_FLIPS]

        def copy(a, kk, block, to, src=None):
            dst = o_refs[a].at[_index(block)]
            return pltpu.make_async_remote_copy(
                src_ref=dst if src is None else src, dst_ref=dst, send_sem=send_sems.at[a, kk],
                recv_sem=recv_sems.at[a, kk], device_id=to, device_id_type=MESH)

        mine = [pltpu.make_async_copy(x_refs[a], o_refs[a].at[_index(me)], local_sems.at[a]) for a in range(na)]
        for cp in mine:
            cp.start()
        first = []
        for j, chip in enumerate(chips):
            first += [copy(a, 1 + j, me, chip, src=x_refs[a]) for a in range(na)]
        first += [copy(a, 0, me, sibling, src=x_refs[a]) for a in range(na)]
        for cp in first:
            cp.start()
        passed = []
        for j, chip in enumerate(chips):
            for a in range(na):
                copy(a, 1 + j, chip, me).wait_recv()
                cp = copy(a, 4 + j, chip, sibling)
                cp.start()
                passed.append(cp)
        for a in range(na):
            copy(a, 0, sibling, me).wait_recv()
        for j, chip in enumerate(chips):
            for a in range(na):
                copy(a, 4 + j, _flip(chip, 1), me).wait_recv()
        for cp in first + passed:
            cp.wait_send()
        for cp in mine:
            cp.wait()

    return pl.pallas_call(
        body, name=name, in_specs=[ANY] * na, out_specs=[ANY] * na,
        out_shape=[SDS((N_DEV,) + t.shape, t.dtype) for t in xs],
        scratch_shapes=[pltpu.SemaphoreType.DMA((na, N_DEV - 1)), pltpu.SemaphoreType.DMA((na, N_DEV - 1)),
                        pltpu.SemaphoreType.DMA((na,))],
    )(*xs)


HBM = pl.BlockSpec(memory_space=pltpu.HBM)
SEM = pl.BlockSpec(memory_space=pltpu.SEMAPHORE)
EFFECT = pltpu.SideEffectType.DATAFLOW_SIDE_EFFECTING
N_PEERS = N_DEV - 1


def _split_copy(src_ref, land_ref, send_sem, recv_sem, me, kk, scatter, landed_from_peer):
    peer = _flip(me, kk)
    src = src_ref.at[_index(peer)] if scatter else src_ref
    dst = land_ref.at[_index(peer if landed_from_peer else me)]
    return pltpu.make_async_remote_copy(src_ref=src, dst_ref=dst, send_sem=send_sem, recv_sem=recv_sem,
                                        device_id=peer, device_id_type=MESH)


ALL_PEERS = tuple(range(1, N_DEV))
EVERYONE = (0,) + ALL_PEERS


def exchange_start(srcs, lands, group_sizes, scatter, name, peers=ALL_PEERS):
    na, ng = len(srcs), len(group_sizes)

    def body(*refs):
        s_refs, l_refs = refs[:na], refs[na:2 * na]
        sems = refs[2 * na:2 * na + 2 * ng]
        token = refs[-1]
        me = _my_place()
        a = 0
        for gi, gsz in enumerate(group_sizes):
            for j in range(gsz):
                for pi, kk in enumerate(peers):
                    slot = j * len(peers) + pi
                    _split_copy(s_refs[a], l_refs[a], sems[2 * gi].at[slot], sems[2 * gi + 1].at[slot],
                                me, kk, scatter, False).start()
                a += 1
        token[...] = jnp.zeros_like(token)

    sem_shapes = []
    for gsz in group_sizes:
        sem_shapes += [pltpu.SemaphoreType.DMA((gsz * len(peers),))] * 2
    ins = [pltpu.with_memory_space_constraint(t, pltpu.HBM) for t in (*srcs, *lands)]
    res = pl.pallas_call(
        body, name=name, in_specs=[HBM] * (2 * na),
        out_specs=[SEM] * (2 * ng) + [HBM] * (2 * na) + [pl.BlockSpec(memory_space=pltpu.VMEM)],
        out_shape=sem_shapes + [pltpu.HBM(t.shape, t.dtype) for t in ins] + [SDS((8, LANES), F32)],
        input_output_aliases={i: 2 * ng + i for i in range(2 * na)},
        compiler_params=pltpu.CompilerParams(has_side_effects=EFFECT),
    )(*ins)
    sems = [(res[2 * gi], res[2 * gi + 1]) for gi in range(ng)]
    thru = res[2 * ng:2 * ng + 2 * na]
    return sems, thru[:na], thru[na:], res[-1]


def _wait_split_copies(s_refs, l_refs, send_sems, recv_sems, scatter, peers):
    me = _my_place()
    for j in range(len(s_refs)):
        for pi, kk in enumerate(peers):
            slot = j * len(peers) + pi
            cp = _split_copy(s_refs[j], l_refs[j], send_sems.at[slot], recv_sems.at[slot], me, kk, scatter, True)
            cp.wait_send()
            cp.wait_recv()


def exchange_wait(srcs, lands, sems, after, scatter, name, peers=ALL_PEERS):
    n = len(srcs)

    def body(*refs):
        s_refs, l_refs = refs[:n], refs[n:2 * n]
        _wait_split_copies(s_refs, l_refs, refs[2 * n], refs[2 * n + 1], scatter, peers)

    res = pl.pallas_call(
        body, name=name, in_specs=[HBM] * (2 * n) + [SEM, SEM, ANY], out_specs=[HBM] * (2 * n),
        out_shape=[pltpu.HBM(t.shape, t.dtype) for t in (*srcs, *lands)],
        input_output_aliases={i: i for i in range(2 * n)},
        compiler_params=pltpu.CompilerParams(has_side_effects=EFFECT),
    )(*srcs, *lands, sems[0], sems[1], after)
    return res[n:]


def _sibling_copies(l_refs, send_sems, recv_sems, arriving):
    me = _my_place()
    sibling = _flip(me, 1)
    held = [me] + [_flip(me, f) for f in CHIP_FLIPS]
    copies = []
    for j, land in enumerate(l_refs):
        for bi, place in enumerate(held):
            blk = land.at[_index(_flip(place, 1) if arriving else place)]
            slot = j * len(held) + bi
            copies.append(pltpu.make_async_remote_copy(src_ref=blk, dst_ref=blk, send_sem=send_sems.at[slot],
                                                       recv_sem=recv_sems.at[slot], device_id=sibling, device_id_type=MESH))
    return copies


def gather_forward(srcs, lands, sems, after, name):
    n = len(srcs)

    def body(*refs):
        s_refs, l_refs = refs[:n], refs[n:2 * n]
        _wait_split_copies(s_refs, l_refs, refs[2 * n], refs[2 * n + 1], False, SELF_AND_CHIPS)
        for cp in _sibling_copies(l_refs, refs[2 * n + 3], refs[2 * n + 4], False):
            cp.start()

    n_slots = n * (1 + len(CHIP_FLIPS))
    res = pl.pallas_call(
        body, name=name, in_specs=[HBM] * (2 * n) + [SEM, SEM, ANY],
        out_specs=[SEM, SEM] + [HBM] * (2 * n),
        out_shape=[pltpu.SemaphoreType.DMA((n_slots,))] * 2 + [pltpu.HBM(t.shape, t.dtype) for t in (*srcs, *lands)],
        input_output_aliases={i: 2 + i for i in range(2 * n)},
        compiler_params=pltpu.CompilerParams(has_side_effects=EFFECT),
    )(*srcs, *lands, sems[0], sems[1], after)
    return (res[0], res[1]), res[2 + n:]


def gather_finish(lands, sems, after, name):
    n = len(lands)

    def body(*refs):
        l_refs = refs[:n]
        for cp in _sibling_copies(l_refs, refs[n], refs[n + 1], True):
            cp.wait_send()
            cp.wait_recv()

    return pl.pallas_call(
        body, name=name, in_specs=[HBM] * n + [SEM, SEM, ANY], out_specs=[HBM] * n,
        out_shape=[pltpu.HBM(t.shape, t.dtype) for t in lands],
        input_output_aliases={i: i for i in range(n)},
        compiler_params=pltpu.CompilerParams(has_side_effects=EFFECT),
    )(*lands, sems[0], sems[1], after)


def landing_zone(block):
    return lax.empty((N_DEV,) + block.shape, block.dtype)


def sum_parts(parts, name, row_major_3d=False):
    _, r, c = parts.shape
    tc = _pick(c, (256, 128))

    def body(p_ref, o_ref):
        acc = p_ref[0].astype(F32)
        for i in range(1, N_DEV):
            acc = acc + p_ref[i].astype(F32)
        if row_major_3d:
            o_ref[:, 0, :] = acc
        else:
            o_ref[...] = acc

    out_spec = pl.BlockSpec((r, 1, tc), lambda i: (0, 0, i)) if row_major_3d else pl.BlockSpec((r, tc), lambda i: (0, i))
    return pl.pallas_call(
        body, name=name, grid=(c // tc,), in_specs=[pl.BlockSpec((N_DEV, r, tc), lambda i: (0, 0, i))],
        out_specs=out_spec, out_shape=SDS((r, 1, c) if row_major_3d else (r, c), F32),
        compiler_params=_cparams(("parallel",)),
    )(parts)


ADAMW_BLOCK_BYTES = 2 * 1024 * 1024


def adamw(w, g, m, v, name):
    shape = w.shape
    lay, rows, cols = ((1, 1) + shape)[-3:]
    tr = _pick(rows, (256, 128))
    tc = cols if tr * cols * 4 <= ADAMW_BLOCK_BYTES else _pick(cols, (256, 128))
    c1 = 1.0 / (1.0 - ADAM_B1 ** ADAM_STEP)
    c2 = 1.0 / (1.0 - ADAM_B2 ** ADAM_STEP)

    def body(w_ref, g_ref, m_ref, v_ref, d_ref, nm_ref, nv_ref):
        gg = g_ref[...]
        nm = ADAM_B1 * m_ref[...] + (1.0 - ADAM_B1) * gg
        nv = ADAM_B2 * v_ref[...] + (1.0 - ADAM_B2) * (gg * gg)
        d_ref[...] = -ADAM_LR * ((nm * c1) / (jnp.sqrt(nv * c2) + ADAM_EPS) + ADAM_WD * w_ref[...])
        nm_ref[...] = nm
        nv_ref[...] = nv

    spec = pl.BlockSpec((1, tr, tc), lambda l, i, j: (l, i, j))
    outs = pl.pallas_call(
        body, name=name, grid=(lay, rows // tr, cols // tc), in_specs=[spec] * 4, out_specs=[spec] * 3,
        out_shape=[SDS((lay, rows, cols), F32)] * 3, compiler_params=_cparams(("parallel",) * 3),
    )(*[t.reshape(lay, rows, cols) for t in (w, g, m, v)])
    return [o.reshape(shape) for o in outs]


def adamw_layer_inner(w, gs, m, v, name):
    rows, lay, cols = w.shape
    tr = _pick(rows, (256, 220, 128))
    c1 = 1.0 / (1.0 - ADAM_B1 ** ADAM_STEP)
    c2 = 1.0 / (1.0 - ADAM_B2 ** ADAM_STEP)

    def body(*refs):
        w_ref, m_ref, v_ref = refs[:3]
        g_refs = refs[3:3 + lay]
        go_ref, d_ref, nm_ref, nv_ref = refs[3 + lay:]
        for l, g_ref in enumerate(g_refs):
            gg = g_ref[:, 0, :]
            nm = ADAM_B1 * m_ref[:, l, :] + (1.0 - ADAM_B1) * gg
            nv = ADAM_B2 * v_ref[:, l, :] + (1.0 - ADAM_B2) * (gg * gg)
            d_ref[:, l, :] = -ADAM_LR * ((nm * c1) / (jnp.sqrt(nv * c2) + ADAM_EPS) + ADAM_WD * w_ref[:, l, :])
            go_ref[:, l, :] = gg
            nm_ref[:, l, :] = nm
            nv_ref[:, l, :] = nv

    inner = pl.BlockSpec((tr, lay, cols), lambda i: (i, 0, 0))
    plain = pl.BlockSpec((tr, 1, cols), lambda i: (i, 0, 0))
    return pl.pallas_call(
        body, name=name, grid=(rows // tr,), in_specs=[inner] * 3 + [plain] * lay, out_specs=[inner] * 4,
        out_shape=[SDS((rows, lay, cols), F32)] * 4, compiler_params=_cparams(("parallel",)),
    )(w, m, v, *gs)


BIG = ("w_in", "conv_w", "w_ssd_branch", "w_attn_branch", "w_out", "w_gate_up", "w_down")
TRANSPOSED = ("w_in", "w_gate_up")
SMALL = ("norm_mix", "conv_b", "dt_bias", "a_log", "d_skip", "ssd_norm", "norm_ffn")
SMALL_SIZE = {"norm_mix": 1024, "conv_b": 3072, "dt_bias": 32, "a_log": 32, "d_skip": 32, "ssd_norm": 2048, "norm_ffn": 1024}
FLAT_W = 512
SMALL_TOTAL = DEPTH * sum(SMALL_SIZE.values()) + D_MODEL + LANES
SMALL_ROWS = 32
assert SMALL_ROWS * FLAT_W >= SMALL_TOTAL


GROUPS = (("w_in", "conv_w"), ("w_ssd_branch", "w_attn_branch", "w_out"), ("w_gate_up", "w_down"))


def to_wire(k, shard):
    if k in TRANSPOSED:
        return shard.T.astype(BF16)
    return shard if k == "conv_w" else shard.astype(BF16)


def full_weights(k, g):
    if k == "conv_w":
        return {k: g.transpose(1, 0, 2).reshape(SSD_CONV, SSD_CONV_CH)}
    full = g.reshape(-1, g.shape[-1])
    if k != "w_in":
        return {k: full}
    w, off = {}, 0
    for nm, r in IN_ROWS:
        w[nm] = full[off:off + r]
        off += r
    w["w_q"] = full[sum(r for _, r in IN_ROWS[:3]):sum(r for _, r in IN_ROWS[:6])]
    w["w_dt"] = jnp.pad(w["w_dt"], ((0, HPAD - SSD_HEADS), (0, 0)))
    return w


def grads_to_wire(k, g):
    if k == "conv_w":
        return g.reshape(SSD_CONV, N_DEV, SSD_CONV_CH // N_DEV).transpose(1, 0, 2)
    return g.reshape(N_DEV, g.shape[0] // N_DEV, g.shape[1])


def _pad_heads(t):
    return jnp.pad(t.reshape(1, SSD_HEADS), ((0, 0), (0, HPAD - SSD_HEADS)))


def local_step(x, target, getw, prefetch, emit, smalls, norm_final):
    tabs = rope_tables()
    sms = []
    for li in range(DEPTH):
        s = smalls[li]
        sms.append({
            "norm_mix": s["norm_mix"].reshape(1, -1), "conv_b": s["conv_b"].reshape(1, -1),
            "dt_bias": _pad_heads(s["dt_bias"]), "a_log": _pad_heads(s["a_log"]),
            "d_skip_x": jnp.repeat(s["d_skip"], SSD_HEAD_DIM).reshape(1, -1),
            "ssd_norm": s["ssd_norm"].reshape(1, -1), "norm_ffn": s["norm_ffn"].reshape(1, -1)})
    h = x
    saved = []
    for li in range(DEPTH):
        h, sv = layer_fwd(h, functools.partial(getw, li), functools.partial(prefetch, li), sms[li], tabs, li)
        saved.append(sv)
    dh, g_final, loss = loss_head(h, target, norm_final.reshape(1, -1), "loss_head")
    gsms = [None] * DEPTH
    for li in reversed(range(DEPTH)):
        dh, gsm = layer_bwd(dh, saved[li], sms[li], tabs, li, functools.partial(emit, li))
        gsms[li] = {
            "norm_mix": gsm["norm_mix"].reshape(-1), "conv_b": gsm["conv_b"].reshape(-1),
            "dt_bias": gsm["dt_bias"][0, :SSD_HEADS], "a_log": gsm["a_log"][0, :SSD_HEADS],
            "d_skip": gsm["d_skip_x"].reshape(SSD_HEADS, SSD_HEAD_DIM).sum(axis=1),
            "ssd_norm": gsm["ssd_norm"].reshape(-1), "norm_ffn": gsm["norm_ffn"].reshape(-1)}
    return loss, dh, gsms, g_final.reshape(-1)


def kernel(x, norm_mix, w_in, conv_w, conv_b, dt_bias, a_log, d_skip, ssd_norm, w_ssd_branch, w_attn_branch, w_out, norm_ffn, w_gate_up, w_down, norm_final, loss_target, m_norm_mix, m_w_in, m_conv_w, m_conv_b, m_dt_bias, m_a_log, m_d_skip, m_ssd_norm, m_w_ssd_branch, m_w_attn_branch, m_w_out, m_norm_ffn, m_w_gate_up, m_w_down, m_norm_final, v_norm_mix, v_w_in, v_conv_w, v_conv_b, v_dt_bias, v_a_log, v_d_skip, v_ssd_norm, v_w_ssd_branch, v_w_attn_branch, v_w_out, v_norm_ffn, v_w_gate_up, v_w_down, v_norm_final):
    wv = dict(norm_mix=norm_mix, w_in=w_in, conv_w=conv_w, conv_b=conv_b, dt_bias=dt_bias, a_log=a_log, d_skip=d_skip,
              ssd_norm=ssd_norm, w_ssd_branch=w_ssd_branch, w_attn_branch=w_attn_branch, w_out=w_out, norm_ffn=norm_ffn,
              w_gate_up=w_gate_up, w_down=w_down, norm_final=norm_final)
    mv = dict(norm_mix=m_norm_mix, w_in=m_w_in, conv_w=m_conv_w, conv_b=m_conv_b, dt_bias=m_dt_bias, a_log=m_a_log,
              d_skip=m_d_skip, ssd_norm=m_ssd_norm, w_ssd_branch=m_w_ssd_branch, w_attn_branch=m_w_attn_branch,
              w_out=m_w_out, norm_ffn=m_norm_ffn, w_gate_up=m_w_gate_up, w_down=m_w_down, norm_final=m_norm_final)
    vv = dict(norm_mix=v_norm_mix, w_in=v_w_in, conv_w=v_conv_w, conv_b=v_conv_b, dt_bias=v_dt_bias, a_log=v_a_log,
              d_skip=v_d_skip, ssd_norm=v_ssd_norm, w_ssd_branch=v_w_ssd_branch, w_attn_branch=v_w_attn_branch,
              w_out=v_w_out, norm_ffn=v_norm_ffn, w_gate_up=v_w_gate_up, w_down=v_w_down, norm_final=v_norm_final)
    order = ("norm_mix", "w_in", "conv_w", "conv_b", "dt_bias", "a_log", "d_skip", "ssd_norm", "w_ssd_branch",
             "w_attn_branch", "w_out", "norm_ffn", "w_gate_up", "w_down", "norm_final")

    smalls = [{k: wv[k][li] for k in SMALL} for li in range(DEPTH)]
    n_groups = len(GROUPS)

    first_lands = all_gather([to_wire(k, wv[k][0]) for k in GROUPS[0]], "gather_first")
    later = [(li, gi) for li in range(DEPTH) for gi in range(n_groups)][1:]
    behind_first = first_lands[1][0, 0, 0] * 0.0
    srcs = [to_wire(k, wv[k][li] + behind_first if k == "conv_w" else wv[k][li]) for li, gi in later for k in GROUPS[gi]]
    sizes = [len(GROUPS[gi]) for _, gi in later]
    w_sems, w_srcs, w_lands, token = exchange_start(srcs, [landing_zone(s) for s in srcs], sizes, False,
                                                    "gather_start", peers=SELF_AND_CHIPS)
    smalls[0]["norm_mix"] = smalls[0]["norm_mix"] + token[0, 0]
    second_leg = {}

    def forward(slot, after):
        if slot < len(later) and slot not in second_leg:
            sl = slice(sum(sizes[:slot]), sum(sizes[:slot + 1]))
            second_leg[slot] = gather_forward(w_srcs[sl], w_lands[sl], w_sems[slot], after, f"gather_forward_{slot}")

    def prefetch(li, gi, after):
        if (li, gi) == later[0]:
            forward(0, after)

    def getw(li, gi, after):
        if (li, gi) == (0, 0):
            lands = first_lands
        else:
            slot = later.index((li, gi))
            forward(slot, after)
            sems2, lands2 = second_leg[slot]
            lands = gather_finish(lands2, sems2, after, f"gather_finish_{li}_{gi}")
            forward(slot + 1, lands[0])
        w = {}
        for k, land in zip(GROUPS[gi], lands):
            w.update(full_weights(k, land))
        return w

    pending = []

    def emit(li, gi, gw):
        parts = [grads_to_wire(k, gw[k]) for k in GROUPS[gi]]
        lands = [landing_zone(p[0]) for p in parts]
        sems, p_thru, l_thru, tok = exchange_start(parts, lands, [len(parts)], True, f"grads_start_{li}_{gi}", peers=EVERYONE)
        pending.append((li, gi, sems[0], p_thru, l_thru))
        return tok[0, 0]

    loss_p, dx, gsms, g_final = local_step(x[0], loss_target[0], getw, prefetch, emit, smalls, norm_final)

    grads, deltas, new_m, new_v = {}, {}, {}, {}

    def update(k):
        if k == "w_in":
            inner = lambda t: t.transpose(2, 0, 1)
            outs = adamw_layer_inner(inner(wv[k]), shard_g[k], inner(mv[k]), inner(vv[k]), "adamw_" + k)
            grads[k], deltas[k], new_m[k], new_v[k] = (t.transpose(1, 2, 0) for t in outs)
            return outs[3]
        if k in BIG:
            grads[k] = jnp.stack([g.T if k in TRANSPOSED else g for g in shard_g[k]])
        deltas[k], new_m[k], new_v[k] = adamw(wv[k], grads[k], mv[k], vv[k], "adamw_" + k)
        return new_v[k]

    shard_g = {k: [None] * DEPTH for k in BIG}

    def collect(entry, after):
        li, gi, sems, p_thru, l_thru = entry
        recv = exchange_wait(p_thru, l_thru, sems, after, True, f"grads_wait_{li}_{gi}", peers=EVERYONE)
        for k, r in zip(GROUPS[gi], recv):
            if k == "conv_w":
                r = r.reshape(N_DEV, 1, -1)
            after = sum_parts(r, f"sum_{k}_{li}", row_major_3d=(k == "w_in"))
            shard_g[k][li] = after if k in TRANSPOSED else after.reshape(wv[k].shape[1:])
        return after

    after = dx
    for entry in pending[:-1]:
        after = collect(entry, after)
    done = [after[:1, :1].reshape(1)]
    for gi in (2, 1):
        for k in GROUPS[gi]:
            done.append(update(k).reshape(-1)[:1])

    flat = [gsms[li][k] for li in range(DEPTH) for k in SMALL] + [g_final, loss_p.reshape(-1)]
    flat.append(jnp.zeros((SMALL_ROWS * FLAT_W - SMALL_TOTAL,), F32))
    small_all = all_gather([jnp.concatenate(flat).reshape(SMALL_ROWS, FLAT_W)], "gather_small")[0]
    small_sum = sum_parts(small_all, "sum_small").reshape(-1)
    off = 0
    per_layer = {k: [] for k in SMALL}
    for li in range(DEPTH):
        for k in SMALL:
            per_layer[k].append(small_sum[off:off + SMALL_SIZE[k]])
            off += SMALL_SIZE[k]
    for k in SMALL:
        grads[k] = jnp.stack(per_layer[k])
    grads["norm_final"] = small_sum[off:off + D_MODEL]
    loss = small_sum[off + D_MODEL]
    for k in (*SMALL, "norm_final"):
        done.append(update(k).reshape(-1)[:1])

    collect(pending[-1], jnp.concatenate(done))
    for k in GROUPS[0]:
        update(k)

    return (loss, dx.reshape(x.shape), *[grads[k] for k in order], *[deltas[k] for k in order],
            *[new_m[k] for k in order], *[new_v[k] for k in order])
```

```python
import functools

import jax
import jax.numpy as jnp
from jax import lax
from jax.experimental import pallas as pl
from jax.experimental.pallas import tpu as pltpu

F32, BF16 = jnp.float32, jnp.bfloat16
SDS = jax.ShapeDtypeStruct
MESH = pl.DeviceIdType.MESH

D_MODEL = 1024
SEQ = 2048
DEPTH = 2
RMS_EPS = 1e-5
SSD_INNER = 2048
SSD_HEAD_DIM = 64
SSD_HEADS = 32
SSD_STATE = 128
SSD_GROUPS = 4
SSD_CONV = 4
SSD_CHUNK = 128
SSD_CONV_CH = 3072
ATTN_HEAD_DIM = 128
ATTN_KV_HEADS = 8
ATTN_DILATIONS = (1, 4, 16)
ATTN_N_PAT = 3
ATTN_BLOCK = 128
ATTN_OUT = 1024
ROPE_THETA = 500000.0
ROPE_DIM = 32
FFN_HIDDEN = 2816
ADAM_LR, ADAM_B1, ADAM_B2, ADAM_EPS, ADAM_WD, ADAM_STEP = 0.001, 0.9, 0.999, 1e-08, 0.01, 10

N_DEV = 8
LANES = 128
VMEM_LIMIT = 56 * 1024 * 1024
HPAD = 128
HIGHEST = lax.Precision.HIGHEST

IN_ROWS = (("w_z", 2048), ("w_xbc", 3072), ("w_dt", 32), ("w_q0", 1024), ("w_q1", 1024), ("w_q2", 1024),
           ("w_k", 1024), ("w_v", 1024), ("w_gs", 1024), ("w_ga", 1024))
N_IN = sum(r for _, r in IN_ROWS)


def _cparams(sem):
    return pltpu.CompilerParams(dimension_semantics=sem, vmem_limit_bytes=VMEM_LIMIT)


def _sigmoid(x):
    return 0.5 * jnp.tanh(0.5 * x) + 0.5


def _silu(x):
    return x * _sigmoid(x)


def _softplus(x):
    return jnp.maximum(x, 0.0) + jnp.log(1.0 + jnp.exp(-jnp.abs(x)))


def _dot(a, b, dims=(((1,), (0,)), ((), ())), precision=None):
    return lax.dot_general(a, b, dims, precision=precision, preferred_element_type=F32)


NT = (((1,), (1,)), ((), ()))
TN = (((0,), (0,)), ((), ()))


def _bdot(a, b, dims=(((1,), (0,)), ((), ()))):
    return _dot(a.astype(BF16), b.astype(BF16), dims)


def _pick(dim, cands):
    for c in cands:
        if dim % c == 0:
            return c
    return dim


def matmul(a, b, *, name, ta=False, tb=False, out_dtype=F32, add=None):
    m, k = (a.shape[1], a.shape[0]) if ta else a.shape
    n = b.shape[0] if tb else b.shape[1]
    tn = _pick(n, (1024, 1408, 512, 256, 128))
    tm = _pick(m, (512, 1408, 256, 128)) if tn == n else _pick(m, (1024, 1408, 512, 256, 128))
    tk = _pick(k, (1024, 1408, 512, 256, 128))
    nk = k // tk
    a_spec = pl.BlockSpec((tk, tm), lambda i, j, kk: (kk, i)) if ta else pl.BlockSpec((tm, tk), lambda i, j, kk: (i, kk))
    b_spec = pl.BlockSpec((tn, tk), lambda i, j, kk: (j, kk)) if tb else pl.BlockSpec((tk, tn), lambda i, j, kk: (kk, j))
    dims = (((0 if ta else 1,), (1 if tb else 0,)), ((), ()))
    has_add = add is not None

    def body(*refs):
        a_ref, b_ref = refs[:2]
        add_ref = refs[2] if has_add else None
        o_ref = refs[3] if has_add else refs[2]
        acc = refs[-1] if nk > 1 else None
        kk = pl.program_id(2)

        def product():
            return _dot(a_ref[...].astype(BF16), b_ref[...].astype(BF16), dims)

        def finish(r):
            if has_add:
                r = r + add_ref[...].astype(F32)
            o_ref[...] = r.astype(o_ref.dtype)

        if nk == 1:
            finish(product())
            return

        @pl.when(kk == 0)
        def _():
            acc[...] = product()

        @pl.when((kk > 0) & (kk < nk - 1))
        def _():
            acc[...] += product()

        @pl.when(kk == nk - 1)
        def _():
            finish(acc[...] + product())

    in_specs = [a_spec, b_spec]
    args = [a, b]
    if has_add:
        in_specs.append(pl.BlockSpec((tm, tn), lambda i, j, kk: (i, j)))
        args.append(add)
    return pl.pallas_call(
        body, name=name, grid=(m // tm, n // tn, nk),
        in_specs=in_specs, out_specs=pl.BlockSpec((tm, tn), lambda i, j, kk: (i, j)),
        out_shape=SDS((m, n), out_dtype), scratch_shapes=[pltpu.VMEM((tm, tn), F32)] if nk > 1 else [],
        compiler_params=_cparams(("parallel", "parallel", "arbitrary")),
    )(*args)


def rowcall(name, fn, rows, params, row_outs, red_outs=(), tr=256):
    s = rows[0].shape[0]
    n_in = len(rows) + len(params)
    n_row = len(row_outs)

    def body(*refs):
        outs = fn(*[r[...].astype(F32) for r in refs[:n_in]])
        if not isinstance(outs, (tuple, list)):
            outs = (outs,)
        orefs = refs[n_in:]
        for r, o in zip(orefs[:n_row], outs[:n_row]):
            r[...] = o.astype(r.dtype)
        if red_outs:
            @pl.when(pl.program_id(0) == 0)
            def _():
                for r in orefs[n_row:]:
                    r[...] = jnp.zeros_like(r)
            for r, o in zip(orefs[n_row:], outs[n_row:]):
                r[...] += o.astype(F32)

    widths = [a[1] if isinstance(a, tuple) else a.shape[1] for a in rows]
    rows = [a[0] if isinstance(a, tuple) else a for a in rows]
    in_specs = [pl.BlockSpec((tr, wd), lambda i: (i, 0)) for wd in widths]
    in_specs += [pl.BlockSpec(p.shape, lambda i: (0, 0)) for p in params]
    out_specs = [pl.BlockSpec((tr, c), lambda i: (i, 0)) for c, _ in row_outs]
    out_specs += [pl.BlockSpec(shp, lambda i: (0, 0)) for shp in red_outs]
    out_shape = [SDS((s, c), dt) for c, dt in row_outs] + [SDS(shp, F32) for shp in red_outs]
    res = pl.pallas_call(
        body, name=name, grid=(s // tr,), in_specs=in_specs, out_specs=out_specs, out_shape=out_shape,
        compiler_params=_cparams(("arbitrary",) if red_outs else ("parallel",)),
    )(*rows, *params)
    return res


def _rms(x, w):
    return x * lax.rsqrt(jnp.mean(x * x, axis=-1, keepdims=True) + RMS_EPS) * w


def rms_fwd(h, w, name):
    return rowcall(name, _rms, [h], [w], [(D_MODEL, BF16)])[0]


def rms_bwd(h, du, dres, w, name):
    def fn(hb, dub, dresb, wb):
        _, vjp = jax.vjp(_rms, hb, wb)
        dh, dw = vjp(dub)
        return dh + dresb, dw
    return rowcall(name, fn, [h, du, dres], [w], [(D_MODEL, F32)], [(1, D_MODEL)])


def loss_head(h, target, w, name):
    def fn(hb, tb, wb):
        def f(hh, ww):
            err = _rms(hh, ww) - tb
            return 0.5 * jnp.sum(jnp.mean(err * err, axis=-1, keepdims=True), axis=0, keepdims=True)
        val, vjp = jax.vjp(f, hb, wb)
        dh, dw = vjp(jnp.ones((1, 1), F32))
        return dh, dw, jnp.broadcast_to(val, (1, LANES))
    return rowcall(name, fn, [h, target], [w], [(D_MODEL, F32)], [(1, D_MODEL), (1, LANES)])


def _gate(a, b, gs, ga):
    return _sigmoid(gs) * a + _sigmoid(ga) * b


def gate_fwd(a, b, gs, ga, name):
    return rowcall(name, _gate, [a, b, gs, ga], [], [(D_MODEL, BF16)])[0]


def gate_bwd(a, b, gs, ga, dm, name):
    def fn(ab, bb, gsb, gab, dmb):
        _, vjp = jax.vjp(_gate, ab, bb, gsb, gab)
        return vjp(dmb)
    return rowcall(name, fn, [a, b, gs, ga, dm], [], [(D_MODEL, BF16)] * 4)


def _swiglu(gu):
    return _silu(gu[:, :FFN_HIDDEN]) * gu[:, FFN_HIDDEN:]


def swiglu_fwd(gu, name):
    return rowcall(name, _swiglu, [gu], [], [(FFN_HIDDEN, BF16)])[0]


def swiglu_bwd(gu, dact, name):
    def fn(gub, db):
        _, vjp = jax.vjp(_swiglu, gub)
        return vjp(db.astype(F32))[0]
    return rowcall(name, fn, [gu, dact], [], [(2 * FFN_HIDDEN, BF16)])[0]


def _ssd_post(y, xs, z, dskip, normw):
    y = (y + dskip * xs) * _silu(z)
    gw = SSD_INNER // SSD_GROUPS
    parts = []
    for g in range(SSD_GROUPS):
        yg = y[:, g * gw:(g + 1) * gw]
        parts.append(yg * lax.rsqrt(jnp.mean(yg * yg, axis=-1, keepdims=True) + RMS_EPS))
    return jnp.concatenate(parts, axis=-1) * normw


def ssd_post_fwd(y, xc, z, dskip, normw, name):
    return rowcall(name, _ssd_post, [y, (xc, SSD_INNER), z], [dskip, normw], [(SSD_INNER, BF16)])[0]


def ssd_post_bwd(y, xc, z, dskip, normw, dyn, name):
    def fn(yb, xsb, zb, dynb, db, nb):
        _, vjp = jax.vjp(_ssd_post, yb, xsb, zb, db, nb)
        return vjp(dynb)
    return rowcall(name, fn, [y, (xc, SSD_INNER), z, dyn], [dskip, normw],
                   [(SSD_INNER, BF16)] * 3, [(1, SSD_INNER), (1, SSD_INNER)])


def _rope(t, cosf, sina, sinb):
    return t * cosf + pltpu.roll(t, LANES - ROPE_DIM // 2, 1) * sina + pltpu.roll(t, ROPE_DIM // 2, 1) * sinb


def rope_tables():
    half = ROPE_DIM // 2
    inv = ROPE_THETA ** (-jnp.arange(0, ROPE_DIM, 2, dtype=F32) / ROPE_DIM)
    ang = jnp.arange(SEQ, dtype=F32)[:, None] * inv[None, :]
    cos, sin = jnp.cos(ang), jnp.sin(ang)
    zeros = jnp.zeros((SEQ, LANES - ROPE_DIM), F32)
    z16 = jnp.zeros((SEQ, half), F32)
    cosf = jnp.concatenate([cos, cos, jnp.ones((SEQ, LANES - ROPE_DIM), F32)], axis=1)
    sina = jnp.concatenate([-sin, z16, zeros], axis=1)
    sinb = jnp.concatenate([z16, sin, zeros], axis=1)
    return cosf, sina, sinb


CONV_TC = 256


def _conv_pre(x, w, b, row):
    acc = x * w[SSD_CONV - 1:SSD_CONV, :] + b
    shifted = [x]
    for j in range(1, SSD_CONV):
        xs = jnp.where(row >= j, pltpu.roll(x, j, 0), 0.0)
        shifted.append(xs)
        acc = acc + xs * w[SSD_CONV - 1 - j:SSD_CONV - j, :]
    return acc, shifted


def conv_fwd(xbc, w, b, name):
    def body(x_ref, w_ref, b_ref, o_ref):
        row = lax.broadcasted_iota(jnp.int32, (SEQ, CONV_TC), 0)
        pre, _ = _conv_pre(x_ref[...].astype(F32), w_ref[...], b_ref[...], row)
        o_ref[...] = _silu(pre).astype(o_ref.dtype)
    return pl.pallas_call(
        body, name=name, grid=(SSD_CONV_CH // CONV_TC,),
        in_specs=[pl.BlockSpec((SEQ, CONV_TC), lambda i: (0, i)), pl.BlockSpec((SSD_CONV, CONV_TC), lambda i: (0, i)),
                  pl.BlockSpec((1, CONV_TC), lambda i: (0, i))],
        out_specs=pl.BlockSpec((SEQ, CONV_TC), lambda i: (0, i)),
        out_shape=SDS((SEQ, SSD_CONV_CH), BF16), compiler_params=_cparams(("parallel",)),
    )(xbc, w, b)


def conv_bwd(xbc, w, b, dxc, name):
    def body(x_ref, w_ref, b_ref, dy_ref, dx_ref, dw_ref, db_ref):
        row = lax.broadcasted_iota(jnp.int32, (SEQ, CONV_TC), 0)
        wv = w_ref[...]
        pre, shifted = _conv_pre(x_ref[...].astype(F32), wv, b_ref[...], row)
        sg = _sigmoid(pre)
        ds = dy_ref[...].astype(F32) * (sg * (1.0 + pre * (1.0 - sg)))
        dx = ds * wv[SSD_CONV - 1:SSD_CONV, :]
        for j in range(1, SSD_CONV):
            dsj = jnp.where(row < SEQ - j, pltpu.roll(ds, SEQ - j, 0), 0.0)
            dx = dx + dsj * wv[SSD_CONV - 1 - j:SSD_CONV - j, :]
        dx_ref[...] = dx.astype(dx_ref.dtype)
        for j in range(SSD_CONV):
            dw_ref[SSD_CONV - 1 - j:SSD_CONV - j, :] = jnp.sum(ds * shifted[j], axis=0, keepdims=True)
        db_ref[...] = jnp.sum(ds, axis=0, keepdims=True)
    return pl.pallas_call(
        body, name=name, grid=(SSD_CONV_CH // CONV_TC,),
        in_specs=[pl.BlockSpec((SEQ, CONV_TC), lambda i: (0, i)), pl.BlockSpec((SSD_CONV, CONV_TC), lambda i: (0, i)),
                  pl.BlockSpec((1, CONV_TC), lambda i: (0, i)), pl.BlockSpec((SEQ, CONV_TC), lambda i: (0, i))],
        out_specs=[pl.BlockSpec((SEQ, CONV_TC), lambda i: (0, i)), pl.BlockSpec((SSD_CONV, CONV_TC), lambda i: (0, i)),
                   pl.BlockSpec((1, CONV_TC), lambda i: (0, i))],
        out_shape=[SDS((SEQ, SSD_CONV_CH), BF16), SDS((SSD_CONV, SSD_CONV_CH), F32), SDS((1, SSD_CONV_CH), F32)],
        compiler_params=_cparams(("parallel",)),
    )(xbc, w, b, dxc)


N_CHUNKS = SEQ // SSD_CHUNK
N_PAIRS = SSD_HEADS // 2
PAIRS_PER_GROUP = N_PAIRS // SSD_GROUPS
B_OFF = SSD_INNER
C_OFF = SSD_INNER + SSD_GROUPS * SSD_STATE


def _ssd_prefix(dtr, dtr_t, dtb, dtb_t, alog, alog_t):
    ln = SSD_CHUNK
    dt = _softplus(dtr + dtb)
    dt_t = _softplus(dtr_t + dtb_t)
    dta = dt * (-jnp.exp(alog))
    dta_t = dt_t * (-jnp.exp(alog_t))
    r = lax.broadcasted_iota(jnp.int32, (ln, ln), 0)
    c = lax.broadcasted_iota(jnp.int32, (ln, ln), 1)
    a_cum = _dot((r >= c).astype(F32), dta, precision=HIGHEST)
    a_cum_t = _dot(dta_t, (r <= c).astype(F32), precision=HIGHEST)
    a_last = jnp.sum(dta_t, axis=1, keepdims=True)
    return dt, a_cum, a_cum_t, a_last


def _bein(spec, a, b):
    return jnp.einsum(spec, a.astype(BF16), b.astype(BF16), preferred_element_type=F32)


SSD_GROUPS_PER_BATCH = 4


def _ssd_group(xs3, bgs, cgs, h3, dt, a_cum, a_cum_t, a_last, *, groups):
    ln = SSD_CHUNK
    lane = lax.broadcasted_iota(jnp.int32, (ln, LANES), 1)
    sub = lax.broadcasted_iota(jnp.int32, (LANES, SSD_STATE), 0)
    row = lax.broadcasted_iota(jnp.int32, (ln, ln), 0)
    col = lax.broadcasted_iota(jnp.int32, (ln, ln), 1)
    lo = lane < SSD_HEAD_DIM
    causal = row >= col
    m_lo, m_hi, dts, acs, lasts, cds, cg3, bg3 = [], [], [], [], [], [], [], []
    for g, bg, cg in zip(groups, bgs, cgs):
        cb = _bdot(cg, bg, NT)
        for j in range(PAIRS_PER_GROUP):
            e0 = 2 * (g * PAIRS_PER_GROUP + j)
            e1 = e0 + 1
            c0, c1 = a_cum[:, e0:e0 + 1], a_cum[:, e1:e1 + 1]
            r0, r1 = a_cum_t[e0:e0 + 1, :], a_cum_t[e1:e1 + 1, :]
            l0, l1 = a_last[e0:e0 + 1, :], a_last[e1:e1 + 1, :]
            m_lo.append(cb * jnp.exp(jnp.where(causal, c0 - r0, -jnp.inf)))
            m_hi.append(cb * jnp.exp(jnp.where(causal, c1 - r1, -jnp.inf)))
            dts.append(jnp.where(lo, dt[:, e0:e0 + 1], dt[:, e1:e1 + 1]))
            acs.append(jnp.where(lo, c0, c1))
            lasts.append(jnp.where(lo, l0, l1))
            cds.append(jnp.exp(jnp.where(sub < SSD_HEAD_DIM, l0, l1)))
            cg3.append(cg)
            bg3.append(bg)
    xd = xs3 * jnp.stack(dts)
    acum = jnp.stack(acs)
    y = (_bein("pls,psq->plq", jnp.stack(m_lo), jnp.where(lo[None], xd, 0.0))
         + _bein("pls,psq->plq", jnp.stack(m_hi), jnp.where(lo[None], 0.0, xd)))
    y = y + _bein("pln,pqn->plq", jnp.stack(cg3), h3) * jnp.exp(acum)
    st = _bein("plq,pln->pqn", xd * jnp.exp(jnp.stack(lasts) - acum), jnp.stack(bg3))
    h_out = h3 * jnp.stack(cds) + st
    return y, h_out


def _group_slabs(groups):
    pairs = [g * PAIRS_PER_GROUP + j for g in groups for j in range(PAIRS_PER_GROUP)]
    return [slice(p * LANES, (p + 1) * LANES) for p in pairs]


def _group_batches():
    return [tuple(range(g, g + SSD_GROUPS_PER_BATCH)) for g in range(0, SSD_GROUPS, SSD_GROUPS_PER_BATCH)]


def _bc_of(xc_ref, g):
    return (xc_ref[:, B_OFF + g * SSD_STATE:B_OFF + (g + 1) * SSD_STATE].astype(F32),
            xc_ref[:, C_OFF + g * SSD_STATE:C_OFF + (g + 1) * SSD_STATE].astype(F32))


def _ssd_in_specs(chunk_of):
    return [
        pl.BlockSpec((SSD_CHUNK, SSD_CONV_CH), lambda i: (chunk_of(i), 0)),
        pl.BlockSpec((SSD_CHUNK, HPAD), lambda i: (chunk_of(i), 0)),
        pl.BlockSpec((HPAD, SSD_CHUNK), lambda i: (0, chunk_of(i))),
        pl.BlockSpec((1, HPAD), lambda i: (0, 0)), pl.BlockSpec((HPAD, 1), lambda i: (0, 0)),
        pl.BlockSpec((1, HPAD), lambda i: (0, 0)), pl.BlockSpec((HPAD, 1), lambda i: (0, 0)),
    ]


def ssd_fwd(xc, dtr, dtr_t, dtb, dtb_t, alog, alog_t, name):
    def body(xc_ref, dtr_ref, dtrt_ref, dtb_ref, dtbt_ref, al_ref, alt_ref, y_ref, hs_ref, h_scr):
        @pl.when(pl.program_id(0) == 0)
        def _():
            h_scr[...] = jnp.zeros_like(h_scr)

        hs_ref[0] = h_scr[...]
        dt, a_cum, a_cum_t, a_last = _ssd_prefix(dtr_ref[...], dtrt_ref[...], dtb_ref[...], dtbt_ref[...],
                                                  al_ref[...], alt_ref[...])
        for groups in _group_batches():
            slabs = _group_slabs(groups)
            bgs, cgs = zip(*[_bc_of(xc_ref, g) for g in groups])
            xs3 = jnp.stack([xc_ref[:, sl] for sl in slabs]).astype(F32)
            h3 = jnp.stack([h_scr[sl, :] for sl in slabs])
            y3, h3_out = _ssd_group(xs3, bgs, cgs, h3, dt, a_cum, a_cum_t, a_last, groups=groups)
            for j, sl in enumerate(slabs):
                y_ref[:, sl] = y3[j].astype(y_ref.dtype)
                h_scr[sl, :] = h3_out[j]

    return pl.pallas_call(
        body, name=name, grid=(N_CHUNKS,), in_specs=_ssd_in_specs(lambda i: i),
        out_specs=[pl.BlockSpec((SSD_CHUNK, SSD_INNER), lambda i: (i, 0)),
                   pl.BlockSpec((1, SSD_INNER, SSD_STATE), lambda i: (i, 0, 0))],
        out_shape=[SDS((SEQ, SSD_INNER), BF16), SDS((N_CHUNKS, SSD_INNER, SSD_STATE), F32)],
        scratch_shapes=[pltpu.VMEM((SSD_INNER, SSD_STATE), F32)],
        compiler_params=_cparams(("arbitrary",)),
    )(xc, dtr, dtr_t, dtb, dtb_t, alog, alog_t)


def ssd_bwd(xc, dtr, dtr_t, dtb, dtb_t, alog, alog_t, hs, dy, dxs_extra, name):
    rev = lambda i: N_CHUNKS - 1 - i

    def body(xc_ref, dtr_ref, dtrt_ref, dtb_ref, dtbt_ref, al_ref, alt_ref, hs_ref, dy_ref, dxe_ref,
             dxc_ref, ddtr_ref, ddtrt_ref, ddtb_ref, ddtbt_ref, dal_ref, dalt_ref, dh_scr):
        @pl.when(pl.program_id(0) == 0)
        def _():
            dh_scr[...] = jnp.zeros_like(dh_scr)
            for r in (ddtb_ref, ddtbt_ref, dal_ref, dalt_ref):
                r[...] = jnp.zeros_like(r)

        prefix_in = (dtr_ref[...], dtrt_ref[...], dtb_ref[...], dtbt_ref[...], al_ref[...], alt_ref[...])
        (dt, a_cum, a_cum_t, a_last), prefix_vjp = jax.vjp(_ssd_prefix, *prefix_in)
        d_dt = jnp.zeros_like(dt)
        d_acum = jnp.zeros_like(a_cum)
        d_acum_t = jnp.zeros_like(a_cum_t)
        d_alast = jnp.zeros_like(a_last)
        for groups in _group_batches():
            slabs = _group_slabs(groups)
            bgs, cgs = zip(*[_bc_of(xc_ref, g) for g in groups])
            xs3 = jnp.stack([xc_ref[:, sl] for sl in slabs]).astype(F32)
            h3 = jnp.stack([hs_ref[0, sl, :] for sl in slabs])
            _, vjp = jax.vjp(functools.partial(_ssd_group, groups=groups), xs3, bgs, cgs, h3, dt, a_cum, a_cum_t, a_last)
            dy3 = jnp.stack([dy_ref[:, sl] for sl in slabs]).astype(F32)
            dh3 = jnp.stack([dh_scr[sl, :] for sl in slabs])
            dxs3, d_bgs, d_cgs, dh3_in, ddt, dac, dact, dal = vjp((dy3, dh3))
            for j, sl in enumerate(slabs):
                dxc_ref[:, sl] = (dxs3[j] + dxe_ref[:, sl].astype(F32)).astype(dxc_ref.dtype)
                dh_scr[sl, :] = dh3_in[j]
            d_dt, d_acum, d_acum_t, d_alast = d_dt + ddt, d_acum + dac, d_acum_t + dact, d_alast + dal
            for g, d_bg, d_cg in zip(groups, d_bgs, d_cgs):
                dxc_ref[:, B_OFF + g * SSD_STATE:B_OFF + (g + 1) * SSD_STATE] = d_bg.astype(dxc_ref.dtype)
                dxc_ref[:, C_OFF + g * SSD_STATE:C_OFF + (g + 1) * SSD_STATE] = d_cg.astype(dxc_ref.dtype)
        g_dtr, g_dtrt, g_dtb, g_dtbt, g_al, g_alt = prefix_vjp((d_dt, d_acum, d_acum_t, d_alast))
        ddtr_ref[...] = g_dtr
        ddtrt_ref[...] = g_dtrt
        ddtb_ref[...] += g_dtb
        ddtbt_ref[...] += g_dtbt
        dal_ref[...] += g_al
        dalt_ref[...] += g_alt

    in_specs = _ssd_in_specs(rev) + [
        pl.BlockSpec((1, SSD_INNER, SSD_STATE), lambda i: (rev(i), 0, 0)),
        pl.BlockSpec((SSD_CHUNK, SSD_INNER), lambda i: (rev(i), 0)),
        pl.BlockSpec((SSD_CHUNK, SSD_INNER), lambda i: (rev(i), 0)),
    ]
    out_specs = [
        pl.BlockSpec((SSD_CHUNK, SSD_CONV_CH), lambda i: (rev(i), 0)),
        pl.BlockSpec((SSD_CHUNK, HPAD), lambda i: (rev(i), 0)),
        pl.BlockSpec((HPAD, SSD_CHUNK), lambda i: (0, rev(i))),
        pl.BlockSpec((1, HPAD), lambda i: (0, 0)), pl.BlockSpec((HPAD, 1), lambda i: (0, 0)),
        pl.BlockSpec((1, HPAD), lambda i: (0, 0)), pl.BlockSpec((HPAD, 1), lambda i: (0, 0)),
    ]
    out_shape = [SDS((SEQ, SSD_CONV_CH), BF16), SDS((SEQ, HPAD), F32), SDS((HPAD, SEQ), F32),
                 SDS((1, HPAD), F32), SDS((HPAD, 1), F32), SDS((1, HPAD), F32), SDS((HPAD, 1), F32)]
    return pl.pallas_call(
        body, name=name, grid=(N_CHUNKS,), in_specs=in_specs, out_specs=out_specs, out_shape=out_shape,
        scratch_shapes=[pltpu.VMEM((SSD_INNER, SSD_STATE), F32)],
        compiler_params=_cparams(("arbitrary",)),
    )(xc, dtr, dtr_t, dtb, dtb_t, alog, alog_t, hs, dy, dxs_extra)


ATTN_SCALE = ATTN_HEAD_DIM ** -0.5


UNITS_PER_PATTERN = SEQ // ATTN_BLOCK
ATTN_BATCH = 8


def _for_unit_batches(batch):
    for g, d in enumerate(ATTN_DILATIONS):
        nb = UNITS_PER_PATTERN // d
        span = d * ATTN_BLOCK

        def trip(t, carry, g=g, d=d, nb=nb, span=span):
            units = []
            for j in range(ATTN_BATCH):
                i = t * ATTN_BATCH + j
                r = i >> (nb.bit_length() - 1)
                n = i & (nb - 1)
                start = r + n * span
                prev = jnp.where(n > 0, start - span, start)
                units.append((pl.ds(start, ATTN_BLOCK, stride=d), pl.ds(prev, ATTN_BLOCK, stride=d), n > 0))
            batch(g, units)
            return carry
        lax.fori_loop(0, UNITS_PER_PATTERN // ATTN_BATCH, trip, 0)


def _unit_operands(units, q_scr, k_scr, v_scr):
    def pair(scr, rows, prows):
        return jnp.concatenate([scr[prows, :], scr[rows, :]], axis=0)
    qb = jnp.stack([q_scr[rows, :] for rows, _, _ in units]).astype(BF16)
    kb = jnp.stack([pair(k_scr, rows, prows) for rows, prows, _ in units]).astype(BF16)
    vb = jnp.stack([pair(v_scr, rows, prows) for rows, prows, _ in units]).astype(BF16)
    return qb, kb, vb


def _unit_scores(qb, kb, units):
    s = jnp.einsum("bqd,bkd->bqk", qb, kb, preferred_element_type=F32) * ATTN_SCALE
    qi = lax.broadcasted_iota(jnp.int32, (ATTN_BLOCK, 2 * ATTN_BLOCK), 0)
    kj = lax.broadcasted_iota(jnp.int32, (ATTN_BLOCK, 2 * ATTN_BLOCK), 1)
    own = (kj >= ATTN_BLOCK) & (kj - ATTN_BLOCK <= qi)
    before = (kj < ATTN_BLOCK) & (kj >= qi)
    keep = jnp.stack([own | (before & has_prev) for _, _, has_prev in units])
    return jnp.where(keep, s, -jnp.inf)


def _head_specs(n_q_groups):
    blk = (SEQ, ATTN_HEAD_DIM)
    q_specs = [pl.BlockSpec(blk, functools.partial(lambda h, g: (0, g * ATTN_KV_HEADS + h), g=g)) for g in range(n_q_groups)]
    head = pl.BlockSpec(blk, lambda h: (0, h))
    table = pl.BlockSpec(blk, lambda h: (0, 0))
    return q_specs, head, table


def attn_fwd(q, k, v, tabs, name):
    q_specs, head, table = _head_specs(ATTN_N_PAT)

    def body(q0_ref, q1_ref, q2_ref, k_ref, v_ref, c_ref, sa_ref, sb_ref, y_ref, lse_ref, *scr):
        qs, og, ls, ks, vs = scr[0:3], scr[3:6], scr[6:9], scr[9], scr[10]
        c, sa, sb = c_ref[...], sa_ref[...], sb_ref[...]
        for g, q_ref in enumerate((q0_ref, q1_ref, q2_ref)):
            qs[g][...] = _rope(q_ref[...].astype(F32), c, sa, sb)
        ks[...] = _rope(k_ref[...].astype(F32), c, sa, sb)
        vs[...] = v_ref[...].astype(F32)

        def batch(g, units):
            qb, kb, vb = _unit_operands(units, qs[g], ks, vs)
            s = _unit_scores(qb, kb, units)
            m = jnp.max(s, axis=2, keepdims=True)
            p = jnp.exp(s - m)
            l = jnp.sum(p, axis=2, keepdims=True)
            o = jnp.einsum("bqk,bkd->bqd", p.astype(BF16), vb, preferred_element_type=F32) / l
            lse_b = m + jnp.log(l)
            for j, (rows, _, _) in enumerate(units):
                og[g][rows, :] = o[j]
                ls[g][rows, :] = jnp.broadcast_to(lse_b[j], (ATTN_BLOCK, LANES))

        _for_unit_batches(batch)
        l0, l1, l2 = ls[0][...], ls[1][...], ls[2][...]
        m = jnp.maximum(jnp.maximum(l0, l1), l2)
        e0, e1, e2 = jnp.exp(l0 - m), jnp.exp(l1 - m), jnp.exp(l2 - m)
        den = e0 + e1 + e2
        y_ref[...] = ((e0 * og[0][...] + e1 * og[1][...] + e2 * og[2][...]) / den).astype(y_ref.dtype)
        lse_ref[...] = m + jnp.log(den)

    blk = (SEQ, ATTN_HEAD_DIM)
    return pl.pallas_call(
        body, name=name, grid=(ATTN_KV_HEADS,), in_specs=[*q_specs, head, head, table, table, table],
        out_specs=[head, head], out_shape=[SDS((SEQ, ATTN_OUT), BF16), SDS((SEQ, ATTN_OUT), F32)],
        scratch_shapes=[pltpu.VMEM(blk, F32)] * (3 * ATTN_N_PAT + 2),
        compiler_params=_cparams(("parallel",)),
    )(q, q, q, k, v, *tabs)


def attn_bwd(q, k, v, tabs, y, lse, dy, name):
    q_specs, head, table = _head_specs(ATTN_N_PAT)

    def body(q0_ref, q1_ref, q2_ref, k_ref, v_ref, c_ref, sa_ref, sb_ref, y_ref, lse_ref, dy_ref,
             dq0_ref, dq1_ref, dq2_ref, dk_ref, dv_ref, *scr):
        qs, dqs, ks, dks, dd, dvs, vs = scr[0:3], scr[3:6], scr[6], scr[7], scr[8], scr[9], scr[10]
        c, sa, sb = c_ref[...], sa_ref[...], sb_ref[...]
        for g, q_ref in enumerate((q0_ref, q1_ref, q2_ref)):
            qs[g][...] = _rope(q_ref[...].astype(F32), c, sa, sb)
        ks[...] = _rope(k_ref[...].astype(F32), c, sa, sb)
        vs[...] = v_ref[...].astype(F32)
        dks[...] = jnp.zeros_like(dks)
        dvs[...] = jnp.zeros_like(dvs)
        dyv = dy_ref[...]
        dd[...] = jnp.broadcast_to(jnp.sum(dyv * y_ref[...].astype(F32), axis=1, keepdims=True), dd.shape)

        def batch(g, units):
            qb, kb, vb = _unit_operands(units, qs[g], ks, vs)
            dob = jnp.stack([dy_ref[rows, :] for rows, _, _ in units]).astype(BF16)
            lse_b = jnp.stack([lse_ref[rows, :][:, 0:1] for rows, _, _ in units])
            dsum_b = jnp.stack([dd[rows, :][:, 0:1] for rows, _, _ in units])
            p = jnp.exp(_unit_scores(qb, kb, units) - lse_b)
            dp = jnp.einsum("bqd,bkd->bqk", dob, vb, preferred_element_type=F32)
            ds = (p * (dp - dsum_b) * ATTN_SCALE).astype(BF16)
            dq = jnp.einsum("bqk,bkd->bqd", ds, kb, preferred_element_type=F32)
            dk = jnp.einsum("bqk,bqd->bkd", ds, qb, preferred_element_type=F32)
            dv = jnp.einsum("bqk,bqd->bkd", p.astype(BF16), dob, preferred_element_type=F32)
            for j, (rows, prows, _) in enumerate(units):
                dqs[g][rows, :] = dq[j]
                dks[prows, :] += dk[j, :ATTN_BLOCK]
                dks[rows, :] += dk[j, ATTN_BLOCK:]
                dvs[prows, :] += dv[j, :ATTN_BLOCK]
                dvs[rows, :] += dv[j, ATTN_BLOCK:]

        _for_unit_batches(batch)
        for g, dq_ref in enumerate((dq0_ref, dq1_ref, dq2_ref)):
            dq_ref[...] = _rope(dqs[g][...], c, -sa, -sb).astype(dq_ref.dtype)
        dk_ref[...] = _rope(dks[...], c, -sa, -sb).astype(dk_ref.dtype)
        dv_ref[...] = dvs[...].astype(dv_ref.dtype)

    blk = (SEQ, ATTN_HEAD_DIM)
    out = SDS((SEQ, ATTN_OUT), BF16)
    return pl.pallas_call(
        body, name=name, grid=(ATTN_KV_HEADS,), in_specs=[*q_specs, head, head, table, table, table, head, head, head],
        out_specs=[head] * 5, out_shape=[out] * 5,
        scratch_shapes=[pltpu.VMEM(blk, F32)] * (2 * ATTN_N_PAT + 5),
        compiler_params=_cparams(("parallel",)),
    )(q, q, q, k, v, *tabs, y, lse, dy)


def layer_fwd(h, getw, prefetch, small, tabs, li):
    n = f"l{li}_"
    sv = {}
    w = dict(getw(0, h))
    u = rms_fwd(h, small["norm_mix"], n + "rms_mix")
    z = matmul(u, w["w_z"], name=n + "mm_z", tb=True, out_dtype=BF16)
    prefetch(1, z)
    xbc = matmul(u, w["w_xbc"], name=n + "mm_xbc", tb=True, out_dtype=BF16)
    dtr = matmul(u, w["w_dt"], name=n + "mm_dt", tb=True)
    q = matmul(u, w["w_q"], name=n + "mm_q", tb=True, out_dtype=BF16)
    k = matmul(u, w["w_k"], name=n + "mm_k", tb=True, out_dtype=BF16)
    v = matmul(u, w["w_v"], name=n + "mm_v", tb=True, out_dtype=BF16)
    gs = matmul(u, w["w_gs"], name=n + "mm_gs", tb=True, out_dtype=BF16)
    ga = matmul(u, w["w_ga"], name=n + "mm_ga", tb=True, out_dtype=BF16)
    xc = conv_fwd(xbc, w["conv_w"], small["conv_b"], n + "conv")
    dtr_t = dtr.T
    y_ssd, hs = ssd_fwd(xc, dtr, dtr_t, small["dt_bias"], small["dt_bias"].T, small["a_log"], small["a_log"].T, n + "ssd")
    yn = ssd_post_fwd(y_ssd, xc, z, small["d_skip_x"], small["ssd_norm"], n + "ssd_post")
    y_attn, lse = attn_fwd(q, k, v, tabs, n + "attn")
    w.update(getw(1, y_ssd))
    a = matmul(yn, w["w_ssd_branch"], name=n + "mm_a", out_dtype=BF16)
    b = matmul(y_attn, w["w_attn_branch"], name=n + "mm_b", out_dtype=BF16)
    merged = gate_fwd(a, b, gs, ga, n + "gate")
    h1 = matmul(merged, w["w_out"], name=n + "mm_o", add=h)
    w.update(getw(2, h1))
    u2 = rms_fwd(h1, small["norm_ffn"], n + "rms_ffn")
    gu = matmul(u2, w["w_gate_up"], name=n + "mm_gu", tb=True, out_dtype=BF16)
    act = swiglu_fwd(gu, n + "swiglu")
    h2 = matmul(act, w["w_down"], name=n + "mm_down", add=h1)
    sv.update(h=h, u=u, z=z, xbc=xbc, dtr=dtr, dtr_t=dtr_t, gs=gs, ga=ga, xc=xc, y_ssd=y_ssd, hs=hs, yn=yn,
              q=q, k=k, v=v, y_attn=y_attn, lse=lse, a=a, b=b, merged=merged, h1=h1, u2=u2, gu=gu, act=act, w=w)
    return h2, sv


def layer_bwd(dh, sv, small, tabs, li, emit):
    n = f"l{li}_b_"
    w = sv["w"]
    gw, gsm = {}, {}
    dact = matmul(dh, w["w_down"], name=n + "mm_dact", tb=True, out_dtype=BF16)
    gw["w_down"] = matmul(sv["act"], dh, name=n + "mm_dwdown", ta=True, out_dtype=BF16)
    dgu = swiglu_bwd(sv["gu"], dact, n + "swiglu")
    gw["w_gate_up"] = matmul(dgu, sv["u2"], name=n + "mm_dwgu", ta=True, out_dtype=BF16)
    tok = emit(2, gw)
    du2 = matmul(dgu, w["w_gate_up"], name=n + "mm_du2")
    dh1, gsm["norm_ffn"] = rms_bwd(sv["h1"], du2, dh, small["norm_ffn"] + tok, n + "rms_ffn")
    dmerged = matmul(dh1, w["w_out"], name=n + "mm_dmerged", tb=True)
    gw["w_out"] = matmul(sv["merged"], dh1, name=n + "mm_dwo", ta=True, out_dtype=BF16)
    da, db, dgs, dga = gate_bwd(sv["a"], sv["b"], sv["gs"], sv["ga"], dmerged, n + "gate")
    gw["w_ssd_branch"] = matmul(sv["yn"], da, name=n + "mm_dwa", ta=True, out_dtype=BF16)
    gw["w_attn_branch"] = matmul(sv["y_attn"], db, name=n + "mm_dwb", ta=True, out_dtype=BF16)
    tok = emit(1, gw)
    dyn = matmul(da, w["w_ssd_branch"], name=n + "mm_dyn", tb=True, out_dtype=BF16)
    dyattn = matmul(db, w["w_attn_branch"], name=n + "mm_dyattn", tb=True)
    dy_ssd, dxs_extra, dz, gsm["d_skip_x"], gsm["ssd_norm"] = ssd_post_bwd(
        sv["y_ssd"], sv["xc"], sv["z"], small["d_skip_x"] + tok, small["ssd_norm"], dyn, n + "ssd_post")
    dxc, ddtr, ddtr_t, ddtb, ddtb_t, dal, dal_t = ssd_bwd(
        sv["xc"], sv["dtr"], sv["dtr_t"], small["dt_bias"], small["dt_bias"].T, small["a_log"], small["a_log"].T,
        sv["hs"], dy_ssd, dxs_extra, n + "ssd")
    ddtr = (ddtr + ddtr_t.T).astype(BF16)
    gsm["dt_bias"] = ddtb + ddtb_t.T
    gsm["a_log"] = dal + dal_t.T
    dxbc, gw["conv_w"], gsm["conv_b"] = conv_bwd(sv["xbc"], w["conv_w"], small["conv_b"], dxc, n + "conv")
    dq0, dq1, dq2, dk, dv = attn_bwd(sv["q"], sv["k"], sv["v"], tabs, sv["y_attn"], sv["lse"], dyattn, n + "attn")
    u = sv["u"]
    segs = [("w_z", dz), ("w_xbc", dxbc), ("w_dt", ddtr), ("w_q0", dq0), ("w_q1", dq1), ("w_q2", dq2),
            ("w_k", dk), ("w_v", dv), ("w_gs", dgs), ("w_ga", dga)]
    gin = [matmul(dseg, u, name=n + "mm_d" + key, ta=True, out_dtype=BF16) for key, dseg in segs]
    gin[2] = gin[2][:SSD_HEADS]
    gw["w_in"] = jnp.concatenate(gin, axis=0)
    tok = emit(0, gw)
    du = jnp.zeros((SEQ, D_MODEL), F32) + tok
    for key, dseg in segs:
        du = matmul(dseg, w[key], name=n + "mm_du_" + key, add=du)
    dh0, gsm["norm_mix"] = rms_bwd(sv["h"], du, dh1, small["norm_mix"] + tok, n + "rms_mix")
    return dh0, gsm


def _my_place():
    return lax.axis_index("x"), lax.axis_index("y"), lax.axis_index("c")


def _flip(place, k):
    x, y, c = place
    return (1 - x if k & 4 else x, 1 - y if k & 2 else y, 1 - c if k & 1 else c)


def _index(place):
    return 4 * place[0] + 2 * place[1] + place[2]


ANY = pl.BlockSpec(memory_space=pl.ANY)
CHIP_FLIPS = (4, 2, 6)
SELF_AND_CHIPS = (0,) + CHIP_FLIPS


def all_gather(xs, name):
    na = len(xs)

    def body(*refs):
        x_refs, o_refs = refs[:na], refs[na:2 * na]
        send_sems, recv_sems, local_sems = refs[2 * na:]
        me = _my_place()
        sibling = _flip(me, 1)
        chips = [_flip(me, f) for f in CHIP_FLIPS]

        def copy(a, kk, block, to, src=None):
            dst = o_refs[a].at[_index(block)]
            return pltpu.make_async_remote_copy(
                src_ref=dst if src is None else src, dst_ref=dst, send_sem=send_sems.at[a, kk],
                recv_sem=recv_sems.at[a, kk], device_id=to, device_id_type=MESH)

        mine = [pltpu.make_async_copy(x_refs[a], o_refs[a].at[_index(me)], local_sems.at[a]) for a in range(na)]
        for cp in mine:
            cp.start()
        first = []
        for j, chip in enumerate(chips):
            first += [copy(a, 1 + j, me, chip, src=x_refs[a]) for a in range(na)]
        first += [copy(a, 0, me, sibling, src=x_refs[a]) for a in range(na)]
        for cp in first:
            cp.start()
        passed = []
        for j, chip in enumerate(chips):
            for a in range(na):
                copy(a, 1 + j, chip, me).wait_recv()
                cp = copy(a, 4 + j, chip, sibling)
                cp.start()
                passed.append(cp)
        for a in range(na):
            copy(a, 0, sibling, me).wait_recv()
        for j, chip in enumerate(chips):
            for a in range(na):
                copy(a, 4 + j, _flip(chip, 1), me).wait_recv()
        for cp in first + passed:
            cp.wait_send()
        for cp in mine:
            cp.wait()

    return pl.pallas_call(
        body, name=name, in_specs=[ANY] * na, out_specs=[ANY] * na,
        out_shape=[SDS((N_DEV,) + t.shape, t.dtype) for t in xs],
        scratch_shapes=[pltpu.SemaphoreType.DMA((na, N_DEV - 1)), pltpu.SemaphoreType.DMA((na, N_DEV - 1)),
                        pltpu.SemaphoreType.DMA((na,))],
    )(*xs)


HBM = pl.BlockSpec(memory_space=pltpu.HBM)
SEM = pl.BlockSpec(memory_space=pltpu.SEMAPHORE)
EFFECT = pltpu.SideEffectType.DATAFLOW_SIDE_EFFECTING
N_PEERS = N_DEV - 1


def _split_copy(src_ref, land_ref, send_sem, recv_sem, me, kk, scatter, landed_from_peer):
    peer = _flip(me, kk)
    src = src_ref.at[_index(peer)] if scatter else src_ref
    dst = land_ref.at[_index(peer if landed_from_peer else me)]
    return pltpu.make_async_remote_copy(src_ref=src, dst_ref=dst, send_sem=send_sem, recv_sem=recv_sem,
                                        device_id=peer, device_id_type=MESH)


ALL_PEERS = tuple(range(1, N_DEV))
EVERYONE = (0,) + ALL_PEERS


def exchange_start(srcs, lands, group_sizes, scatter, name, peers=ALL_PEERS):
    na, ng = len(srcs), len(group_sizes)

    def body(*refs):
        s_refs, l_refs = refs[:na], refs[na:2 * na]
        sems = refs[2 * na:2 * na + 2 * ng]
        token = refs[-1]
        me = _my_place()
        a = 0
        for gi, gsz in enumerate(group_sizes):
            for j in range(gsz):
                for pi, kk in enumerate(peers):
                    slot = j * len(peers) + pi
                    _split_copy(s_refs[a], l_refs[a], sems[2 * gi].at[slot], sems[2 * gi + 1].at[slot],
                                me, kk, scatter, False).start()
                a += 1
        token[...] = jnp.zeros_like(token)

    sem_shapes = []
    for gsz in group_sizes:
        sem_shapes += [pltpu.SemaphoreType.DMA((gsz * len(peers),))] * 2
    ins = [pltpu.with_memory_space_constraint(t, pltpu.HBM) for t in (*srcs, *lands)]
    res = pl.pallas_call(
        body, name=name, in_specs=[HBM] * (2 * na),
        out_specs=[SEM] * (2 * ng) + [HBM] * (2 * na) + [pl.BlockSpec(memory_space=pltpu.VMEM)],
        out_shape=sem_shapes + [pltpu.HBM(t.shape, t.dtype) for t in ins] + [SDS((8, LANES), F32)],
        input_output_aliases={i: 2 * ng + i for i in range(2 * na)},
        compiler_params=pltpu.CompilerParams(has_side_effects=EFFECT),
    )(*ins)
    sems = [(res[2 * gi], res[2 * gi + 1]) for gi in range(ng)]
    thru = res[2 * ng:2 * ng + 2 * na]
    return sems, thru[:na], thru[na:], res[-1]


def _wait_split_copies(s_refs, l_refs, send_sems, recv_sems, scatter, peers):
    me = _my_place()
    for j in range(len(s_refs)):
        for pi, kk in enumerate(peers):
            slot = j * len(peers) + pi
            cp = _split_copy(s_refs[j], l_refs[j], send_sems.at[slot], recv_sems.at[slot], me, kk, scatter, True)
            cp.wait_send()
            cp.wait_recv()


def exchange_wait(srcs, lands, sems, after, scatter, name, peers=ALL_PEERS):
    n = len(srcs)

    def body(*refs):
        s_refs, l_refs = refs[:n], refs[n:2 * n]
        _wait_split_copies(s_refs, l_refs, refs[2 * n], refs[2 * n + 1], scatter, peers)

    res = pl.pallas_call(
        body, name=name, in_specs=[HBM] * (2 * n) + [SEM, SEM, ANY], out_specs=[HBM] * (2 * n),
        out_shape=[pltpu.HBM(t.shape, t.dtype) for t in (*srcs, *lands)],
        input_output_aliases={i: i for i in range(2 * n)},
        compiler_params=pltpu.CompilerParams(has_side_effects=EFFECT),
    )(*srcs, *lands, sems[0], sems[1], after)
    return res[n:]


def _sibling_copies(l_refs, send_sems, recv_sems, arriving):
    me = _my_place()
    sibling = _flip(me, 1)
    held = [me] + [_flip(me, f) for f in CHIP_FLIPS]
    copies = []
    for j, land in enumerate(l_refs):
        for bi, place in enumerate(held):
            blk = land.at[_index(_flip(place, 1) if arriving else place)]
            slot = j * len(held) + bi
            copies.append(pltpu.make_async_remote_copy(src_ref=blk, dst_ref=blk, send_sem=send_sems.at[slot],
                                                       recv_sem=recv_sems.at[slot], device_id=sibling, device_id_type=MESH))
    return copies


def gather_forward(srcs, lands, sems, after, name):
    n = len(srcs)

    def body(*refs):
        s_refs, l_refs = refs[:n], refs[n:2 * n]
        _wait_split_copies(s_refs, l_refs, refs[2 * n], refs[2 * n + 1], False, SELF_AND_CHIPS)
        for cp in _sibling_copies(l_refs, refs[2 * n + 3], refs[2 * n + 4], False):
            cp.start()

    n_slots = n * (1 + len(CHIP_FLIPS))
    res = pl.pallas_call(
        body, name=name, in_specs=[HBM] * (2 * n) + [SEM, SEM, ANY],
        out_specs=[SEM, SEM] + [HBM] * (2 * n),
        out_shape=[pltpu.SemaphoreType.DMA((n_slots,))] * 2 + [pltpu.HBM(t.shape, t.dtype) for t in (*srcs, *lands)],
        input_output_aliases={i: 2 + i for i in range(2 * n)},
        compiler_params=pltpu.CompilerParams(has_side_effects=EFFECT),
    )(*srcs, *lands, sems[0], sems[1], after)
    return (res[0], res[1]), res[2 + n:]


def gather_finish(lands, sems, after, name):
    n = len(lands)

    def body(*refs):
        l_refs = refs[:n]
        for cp in _sibling_copies(l_refs, refs[n], refs[n + 1], True):
            cp.wait_send()
            cp.wait_recv()

    return pl.pallas_call(
        body, name=name, in_specs=[HBM] * n + [SEM, SEM, ANY], out_specs=[HBM] * n,
        out_shape=[pltpu.HBM(t.shape, t.dtype) for t in lands],
        input_output_aliases={i: i for i in range(n)},
        compiler_params=pltpu.CompilerParams(has_side_effects=EFFECT),
    )(*lands, sems[0], sems[1], after)


def landing_zone(block):
    return lax.empty((N_DEV,) + block.shape, block.dtype)


def sum_parts(parts, name, row_major_3d=False):
    _, r, c = parts.shape
    tc = _pick(c, (256, 128))

    def body(p_ref, o_ref):
        acc = p_ref[0].astype(F32)
        for i in range(1, N_DEV):
            acc = acc + p_ref[i].astype(F32)
        if row_major_3d:
            o_ref[:, 0, :] = acc
        else:
            o_ref[...] = acc

    out_spec = pl.BlockSpec((r, 1, tc), lambda i: (0, 0, i)) if row_major_3d else pl.BlockSpec((r, tc), lambda i: (0, i))
    return pl.pallas_call(
        body, name=name, grid=(c // tc,), in_specs=[pl.BlockSpec((N_DEV, r, tc), lambda i: (0, 0, i))],
        out_specs=out_spec, out_shape=SDS((r, 1, c) if row_major_3d else (r, c), F32),
        compiler_params=_cparams(("parallel",)),
    )(parts)


ADAMW_BLOCK_BYTES = 2 * 1024 * 1024


def adamw(w, g, m, v, name):
    shape = w.shape
    lay, rows, cols = ((1, 1) + shape)[-3:]
    tr = _pick(rows, (256, 128))
    tc = cols if tr * cols * 4 <= ADAMW_BLOCK_BYTES else _pick(cols, (256, 128))
    c1 = 1.0 / (1.0 - ADAM_B1 ** ADAM_STEP)
    c2 = 1.0 / (1.0 - ADAM_B2 ** ADAM_STEP)

    def body(w_ref, g_ref, m_ref, v_ref, d_ref, nm_ref, nv_ref):
        gg = g_ref[...]
        nm = ADAM_B1 * m_ref[...] + (1.0 - ADAM_B1) * gg
        nv = ADAM_B2 * v_ref[...] + (1.0 - ADAM_B2) * (gg * gg)
        d_ref[...] = -ADAM_LR * ((nm * c1) / (jnp.sqrt(nv * c2) + ADAM_EPS) + ADAM_WD * w_ref[...])
        nm_ref[...] = nm
        nv_ref[...] = nv

    spec = pl.BlockSpec((1, tr, tc), lambda l, i, j: (l, i, j))
    outs = pl.pallas_call(
        body, name=name, grid=(lay, rows // tr, cols // tc), in_specs=[spec] * 4, out_specs=[spec] * 3,
        out_shape=[SDS((lay, rows, cols), F32)] * 3, compiler_params=_cparams(("parallel",) * 3),
    )(*[t.reshape(lay, rows, cols) for t in (w, g, m, v)])
    return [o.reshape(shape) for o in outs]


def adamw_layer_inner(w, gs, m, v, name):
    rows, lay, cols = w.shape
    tr = _pick(rows, (256, 220, 128))
    c1 = 1.0 / (1.0 - ADAM_B1 ** ADAM_STEP)
    c2 = 1.0 / (1.0 - ADAM_B2 ** ADAM_STEP)

    def body(*refs):
        w_ref, m_ref, v_ref = refs[:3]
        g_refs = refs[3:3 + lay]
        go_ref, d_ref, nm_ref, nv_ref = refs[3 + lay:]
        for l, g_ref in enumerate(g_refs):
            gg = g_ref[:, 0, :]
            nm = ADAM_B1 * m_ref[:, l, :] + (1.0 - ADAM_B1) * gg
            nv = ADAM_B2 * v_ref[:, l, :] + (1.0 - ADAM_B2) * (gg * gg)
            d_ref[:, l, :] = -ADAM_LR * ((nm * c1) / (jnp.sqrt(nv * c2) + ADAM_EPS) + ADAM_WD * w_ref[:, l, :])
            go_ref[:, l, :] = gg
            nm_ref[:, l, :] = nm
            nv_ref[:, l, :] = nv

    inner = pl.BlockSpec((tr, lay, cols), lambda i: (i, 0, 0))
    plain = pl.BlockSpec((tr, 1, cols), lambda i: (i, 0, 0))
    return pl.pallas_call(
        body, name=name, grid=(rows // tr,), in_specs=[inner] * 3 + [plain] * lay, out_specs=[inner] * 4,
        out_shape=[SDS((rows, lay, cols), F32)] * 4, compiler_params=_cparams(("parallel",)),
    )(w, m, v, *gs)


BIG = ("w_in", "conv_w", "w_ssd_branch", "w_attn_branch", "w_out", "w_gate_up", "w_down")
TRANSPOSED = ("w_in", "w_gate_up")
SMALL = ("norm_mix", "conv_b", "dt_bias", "a_log", "d_skip", "ssd_norm", "norm_ffn")
SMALL_SIZE = {"norm_mix": 1024, "conv_b": 3072, "dt_bias": 32, "a_log": 32, "d_skip": 32, "ssd_norm": 2048, "norm_ffn": 1024}
FLAT_W = 512
SMALL_TOTAL = DEPTH * sum(SMALL_SIZE.values()) + D_MODEL + LANES
SMALL_ROWS = 32
assert SMALL_ROWS * FLAT_W >= SMALL_TOTAL


GROUPS = (("w_in", "conv_w"), ("w_ssd_branch", "w_attn_branch", "w_out"), ("w_gate_up", "w_down"))


def to_wire(k, shard):
    if k in TRANSPOSED:
        return shard.T.astype(BF16)
    return shard if k == "conv_w" else shard.astype(BF16)


def full_weights(k, g):
    if k == "conv_w":
        return {k: g.transpose(1, 0, 2).reshape(SSD_CONV, SSD_CONV_CH)}
    full = g.reshape(-1, g.shape[-1])
    if k != "w_in":
        return {k: full}
    w, off = {}, 0
    for nm, r in IN_ROWS:
        w[nm] = full[off:off + r]
        off += r
    w["w_q"] = full[sum(r for _, r in IN_ROWS[:3]):sum(r for _, r in IN_ROWS[:6])]
    w["w_dt"] = jnp.pad(w["w_dt"], ((0, HPAD - SSD_HEADS), (0, 0)))
    return w


def grads_to_wire(k, g):
    if k == "conv_w":
        return g.reshape(SSD_CONV, N_DEV, SSD_CONV_CH // N_DEV).transpose(1, 0, 2)
    return g.reshape(N_DEV, g.shape[0] // N_DEV, g.shape[1])


def _pad_heads(t):
    return jnp.pad(t.reshape(1, SSD_HEADS), ((0, 0), (0, HPAD - SSD_HEADS)))


def local_step(x, target, getw, prefetch, emit, smalls, norm_final):
    tabs = rope_tables()
    sms = []
    for li in range(DEPTH):
        s = smalls[li]
        sms.append({
            "norm_mix": s["norm_mix"].reshape(1, -1), "conv_b": s["conv_b"].reshape(1, -1),
            "dt_bias": _pad_heads(s["dt_bias"]), "a_log": _pad_heads(s["a_log"]),
            "d_skip_x": jnp.repeat(s["d_skip"], SSD_HEAD_DIM).reshape(1, -1),
            "ssd_norm": s["ssd_norm"].reshape(1, -1), "norm_ffn": s["norm_ffn"].reshape(1, -1)})
    h = x
    saved = []
    for li in range(DEPTH):
        h, sv = layer_fwd(h, functools.partial(getw, li), functools.partial(prefetch, li), sms[li], tabs, li)
        saved.append(sv)
    dh, g_final, loss = loss_head(h, target, norm_final.reshape(1, -1), "loss_head")
    gsms = [None] * DEPTH
    for li in reversed(range(DEPTH)):
        dh, gsm = layer_bwd(dh, saved[li], sms[li], tabs, li, functools.partial(emit, li))
        gsms[li] = {
            "norm_mix": gsm["norm_mix"].reshape(-1), "conv_b": gsm["conv_b"].reshape(-1),
            "dt_bias": gsm["dt_bias"][0, :SSD_HEADS], "a_log": gsm["a_log"][0, :SSD_HEADS],
            "d_skip": gsm["d_skip_x"].reshape(SSD_HEADS, SSD_HEAD_DIM).sum(axis=1),
            "ssd_norm": gsm["ssd_norm"].reshape(-1), "norm_ffn": gsm["norm_ffn"].reshape(-1)}
    return loss, dh, gsms, g_final.reshape(-1)


def kernel(x, norm_mix, w_in, conv_w, conv_b, dt_bias, a_log, d_skip, ssd_norm, w_ssd_branch, w_attn_branch, w_out, norm_ffn, w_gate_up, w_down, norm_final, loss_target, m_norm_mix, m_w_in, m_conv_w, m_conv_b, m_dt_bias, m_a_log, m_d_skip, m_ssd_norm, m_w_ssd_branch, m_w_attn_branch, m_w_out, m_norm_ffn, m_w_gate_up, m_w_down, m_norm_final, v_norm_mix, v_w_in, v_conv_w, v_conv_b, v_dt_bias, v_a_log, v_d_skip, v_ssd_norm, v_w_ssd_branch, v_w_attn_branch, v_w_out, v_norm_ffn, v_w_gate_up, v_w_down, v_norm_final):
    wv = dict(norm_mix=norm_mix, w_in=w_in, conv_w=conv_w, conv_b=conv_b, dt_bias=dt_bias, a_log=a_log, d_skip=d_skip,
              ssd_norm=ssd_norm, w_ssd_branch=w_ssd_branch, w_attn_branch=w_attn_branch, w_out=w_out, norm_ffn=norm_ffn,
              w_gate_up=w_gate_up, w_down=w_down, norm_final=norm_final)
    mv = dict(norm_mix=m_norm_mix, w_in=m_w_in, conv_w=m_conv_w, conv_b=m_conv_b, dt_bias=m_dt_bias, a_log=m_a_log,
              d_skip=m_d_skip, ssd_norm=m_ssd_norm, w_ssd_branch=m_w_ssd_branch, w_attn_branch=m_w_attn_branch,
              w_out=m_w_out, norm_ffn=m_norm_ffn, w_gate_up=m_w_gate_up, w_down=m_w_down, norm_final=m_norm_final)
    vv = dict(norm_mix=v_norm_mix, w_in=v_w_in, conv_w=v_conv_w, conv_b=v_conv_b, dt_bias=v_dt_bias, a_log=v_a_log,
              d_skip=v_d_skip, ssd_norm=v_ssd_norm, w_ssd_branch=v_w_ssd_branch, w_attn_branch=v_w_attn_branch,
              w_out=v_w_out, norm_ffn=v_norm_ffn, w_gate_up=v_w_gate_up, w_down=v_w_down, norm_final=v_norm_final)
    order = ("norm_mix", "w_in", "conv_w", "conv_b", "dt_bias", "a_log", "d_skip", "ssd_norm", "w_ssd_branch",
             "w_attn_branch", "w_out", "norm_ffn", "w_gate_up", "w_down", "norm_final")

    smalls = [{k: wv[k][li] for k in SMALL} for li in range(DEPTH)]
    n_groups = len(GROUPS)

    first_lands = all_gather([to_wire(k, wv[k][0]) for k in GROUPS[0]], "gather_first")
    later = [(li, gi) for li in range(DEPTH) for gi in range(n_groups)][1:]
    behind_first = first_lands[1][0, 0, 0] * 0.0
    srcs = [to_wire(k, wv[k][li] + behind_first if k == "conv_w" else wv[k][li]) for li, gi in later for k in GROUPS[gi]]
    sizes = [len(GROUPS[gi]) for _, gi in later]
    w_sems, w_srcs, w_lands, token = exchange_start(srcs, [landing_zone(s) for s in srcs], sizes, False,
                                                    "gather_start", peers=SELF_AND_CHIPS)
    smalls[0]["norm_mix"] = smalls[0]["norm_mix"] + token[0, 0]
    second_leg = {}

    def forward(slot, after):
        if slot < len(later) and slot not in second_leg:
            sl = slice(sum(sizes[:slot]), sum(sizes[:slot + 1]))
            second_leg[slot] = gather_forward(w_srcs[sl], w_lands[sl], w_sems[slot], after, f"gather_forward_{slot}")

    def prefetch(li, gi, after):
        if (li, gi) == later[0]:
            forward(0, after)

    def getw(li, gi, after):
        if (li, gi) == (0, 0):
            lands = first_lands
        else:
            slot = later.index((li, gi))
            forward(slot, after)
            sems2, lands2 = second_leg[slot]
            lands = gather_finish(lands2, sems2, after, f"gather_finish_{li}_{gi}")
            forward(slot + 1, lands[0])
        w = {}
        for k, land in zip(GROUPS[gi], lands):
            w.update(full_weights(k, land))
        return w

    pending = []

    def emit(li, gi, gw):
        parts = [grads_to_wire(k, gw[k]) for k in GROUPS[gi]]
        lands = [landing_zone(p[0]) for p in parts]
        sems, p_thru, l_thru, tok = exchange_start(parts, lands, [len(parts)], True, f"grads_start_{li}_{gi}", peers=EVERYONE)
        pending.append((li, gi, sems[0], p_thru, l_thru))
        return tok[0, 0]

    loss_p, dx, gsms, g_final = local_step(x[0], loss_target[0], getw, prefetch, emit, smalls, norm_final)

    grads, deltas, new_m, new_v = {}, {}, {}, {}

    def update(k):
        if k == "w_in":
            inner = lambda t: t.transpose(2, 0, 1)
            outs = adamw_layer_inner(inner(wv[k]), shard_g[k], inner(mv[k]), inner(vv[k]), "adamw_" + k)
            grads[k], deltas[k], new_m[k], new_v[k] = (t.transpose(1, 2, 0) for t in outs)
            return outs[3]
        if k in BIG:
            grads[k] = jnp.stack([g.T if k in TRANSPOSED else g for g in shard_g[k]])
        deltas[k], new_m[k], new_v[k] = adamw(wv[k], grads[k], mv[k], vv[k], "adamw_" + k)
        return new_v[k]

    shard_g = {k: [None] * DEPTH for k in BIG}

    def collect(entry, after):
        li, gi, sems, p_thru, l_thru = entry
        recv = exchange_wait(p_thru, l_thru, sems, after, True, f"grads_wait_{li}_{gi}", peers=EVERYONE)
        for k, r in zip(GROUPS[gi], recv):
            if k == "conv_w":
                r = r.reshape(N_DEV, 1, -1)
            after = sum_parts(r, f"sum_{k}_{li}", row_major_3d=(k == "w_in"))
            shard_g[k][li] = after if k in TRANSPOSED else after.reshape(wv[k].shape[1:])
        return after

    after = dx
    for entry in pending[:-1]:
        after = collect(entry, after)
    done = [after[:1, :1].reshape(1)]
    for gi in (2, 1):
        for k in GROUPS[gi]:
            done.append(update(k).reshape(-1)[:1])

    flat = [gsms[li][k] for li in range(DEPTH) for k in SMALL] + [g_final, loss_p.reshape(-1)]
    flat.append(jnp.zeros((SMALL_ROWS * FLAT_W - SMALL_TOTAL,), F32))
    small_all = all_gather([jnp.concatenate(flat).reshape(SMALL_ROWS, FLAT_W)], "gather_small")[0]
    small_sum = sum_parts(small_all, "sum_small").reshape(-1)
    off = 0
    per_layer = {k: [] for k in SMALL}
    for li in range(DEPTH):
        for k in SMALL:
            per_layer[k].append(small_sum[off:off + SMALL_SIZE[k]])
            off += SMALL_SIZE[k]
    for k in SMALL:
        grads[k] = jnp.stack(per_layer[k])
    grads["norm_final"] = small_sum[off:off + D_MODEL]
    loss = small_sum[off + D_MODEL]
    for k in (*SMALL, "norm_final"):
        done.append(update(k).reshape(-1)[:1])

    collect(pending[-1], jnp.concatenate(done))
    for k in GROUPS[0]:
        update(k)

    return (loss, dx.reshape(x.shape), *[grads[k] for k in order], *[deltas[k] for k in order],
            *[new_m[k] for k in order], *[new_v[k] for k in order])
```

```python
import functools

import jax
import jax.numpy as jnp
from jax import lax
from jax.experimental import pallas as pl
from jax.experimental.pallas import tpu as pltpu

F32, BF16 = jnp.float32, jnp.bfloat16
SDS = jax.ShapeDtypeStruct
MESH = pl.DeviceIdType.MESH

D_MODEL = 1024
SEQ = 2048
DEPTH = 2
RMS_EPS = 1e-5
SSD_INNER = 2048
SSD_HEAD_DIM = 64
SSD_HEADS = 32
SSD_STATE = 128
SSD_GROUPS = 4
SSD_CONV = 4
SSD_CHUNK = 128
SSD_CONV_CH = 3072
ATTN_HEAD_DIM = 128
ATTN_KV_HEADS = 8
ATTN_DILATIONS = (1, 4, 16)
ATTN_N_PAT = 3
ATTN_BLOCK = 128
ATTN_OUT = 1024
ROPE_THETA = 500000.0
ROPE_DIM = 32
FFN_HIDDEN = 2816
ADAM_LR, ADAM_B1, ADAM_B2, ADAM_EPS, ADAM_WD, ADAM_STEP = 0.001, 0.9, 0.999, 1e-08, 0.01, 10

N_DEV = 8
LANES = 128
VMEM_LIMIT = 56 * 1024 * 1024
HPAD = 128
HIGHEST = lax.Precision.HIGHEST

IN_ROWS = (("w_z", 2048), ("w_xbc", 3072), ("w_dt", 32), ("w_q0", 1024), ("w_q1", 1024), ("w_q2", 1024),
           ("w_k", 1024), ("w_v", 1024), ("w_gs", 1024), ("w_ga", 1024))
N_IN = sum(r for _, r in IN_ROWS)


def _cparams(sem):
    return pltpu.CompilerParams(dimension_semantics=sem, vmem_limit_bytes=VMEM_LIMIT)


def _sigmoid(x):
    return 0.5 * jnp.tanh(0.5 * x) + 0.5


def _silu(x):
    return x * _sigmoid(x)


def _softplus(x):
    return jnp.maximum(x, 0.0) + jnp.log(1.0 + jnp.exp(-jnp.abs(x)))


def _dot(a, b, dims=(((1,), (0,)), ((), ())), precision=None):
    return lax.dot_general(a, b, dims, precision=precision, preferred_element_type=F32)


NT = (((1,), (1,)), ((), ()))
TN = (((0,), (0,)), ((), ()))


def _bdot(a, b, dims=(((1,), (0,)), ((), ()))):
    return _dot(a.astype(BF16), b.astype(BF16), dims)


def _pick(dim, cands):
    for c in cands:
        if dim % c == 0:
            return c
    return dim


def matmul(a, b, *, name, ta=False, tb=False, out_dtype=F32, add=None):
    m, k = (a.shape[1], a.shape[0]) if ta else a.shape
    n = b.shape[0] if tb else b.shape[1]
    tn = _pick(n, (1024, 1408, 512, 256, 128))
    tm = _pick(m, (512, 1408, 256, 128)) if tn == n else _pick(m, (1024, 1408, 512, 256, 128))
    tk = _pick(k, (1024, 1408, 512, 256, 128))
    nk = k // tk
    a_spec = pl.BlockSpec((tk, tm), lambda i, j, kk: (kk, i)) if ta else pl.BlockSpec((tm, tk), lambda i, j, kk: (i, kk))
    b_spec = pl.BlockSpec((tn, tk), lambda i, j, kk: (j, kk)) if tb else pl.BlockSpec((tk, tn), lambda i, j, kk: (kk, j))
    dims = (((0 if ta else 1,), (1 if tb else 0,)), ((), ()))
    has_add = add is not None

    def body(*refs):
        a_ref, b_ref = refs[:2]
        add_ref = refs[2] if has_add else None
        o_ref = refs[3] if has_add else refs[2]
        acc = refs[-1] if nk > 1 else None
        kk = pl.program_id(2)

        def product():
            return _dot(a_ref[...].astype(BF16), b_ref[...].astype(BF16), dims)

        def finish(r):
            if has_add:
                r = r + add_ref[...].astype(F32)
            o_ref[...] = r.astype(o_ref.dtype)

        if nk == 1:
            finish(product())
            return

        @pl.when(kk == 0)
        def _():
            acc[...] = product()

        @pl.when((kk > 0) & (kk < nk - 1))
        def _():
            acc[...] += product()

        @pl.when(kk == nk - 1)
        def _():
            finish(acc[...] + product())

    in_specs = [a_spec, b_spec]
    args = [a, b]
    if has_add:
        in_specs.append(pl.BlockSpec((tm, tn), lambda i, j, kk: (i, j)))
        args.append(add)
    return pl.pallas_call(
        body, name=name, grid=(m // tm, n // tn, nk),
        in_specs=in_specs, out_specs=pl.BlockSpec((tm, tn), lambda i, j, kk: (i, j)),
        out_shape=SDS((m, n), out_dtype), scratch_shapes=[pltpu.VMEM((tm, tn), F32)] if nk > 1 else [],
        compiler_params=_cparams(("parallel", "parallel", "arbitrary")),
    )(*args)


def rowcall(name, fn, rows, params, row_outs, red_outs=(), tr=256):
    s = rows[0].shape[0]
    n_in = len(rows) + len(params)
    n_row = len(row_outs)

    def body(*refs):
        outs = fn(*[r[...].astype(F32) for r in refs[:n_in]])
        if not isinstance(outs, (tuple, list)):
            outs = (outs,)
        orefs = refs[n_in:]
        for r, o in zip(orefs[:n_row], outs[:n_row]):
            r[...] = o.astype(r.dtype)
        if red_outs:
            @pl.when(pl.program_id(0) == 0)
            def _():
                for r in orefs[n_row:]:
                    r[...] = jnp.zeros_like(r)
            for r, o in zip(orefs[n_row:], outs[n_row:]):
                r[...] += o.astype(F32)

    widths = [a[1] if isinstance(a, tuple) else a.shape[1] for a in rows]
    rows = [a[0] if isinstance(a, tuple) else a for a in rows]
    in_specs = [pl.BlockSpec((tr, wd), lambda i: (i, 0)) for wd in widths]
    in_specs += [pl.BlockSpec(p.shape, lambda i: (0, 0)) for p in params]
    out_specs = [pl.BlockSpec((tr, c), lambda i: (i, 0)) for c, _ in row_outs]
    out_specs += [pl.BlockSpec(shp, lambda i: (0, 0)) for shp in red_outs]
    out_shape = [SDS((s, c), dt) for c, dt in row_outs] + [SDS(shp, F32) for shp in red_outs]
    res = pl.pallas_call(
        body, name=name, grid=(s // tr,), in_specs=in_specs, out_specs=out_specs, out_shape=out_shape,
        compiler_params=_cparams(("arbitrary",) if red_outs else ("parallel",)),
    )(*rows, *params)
    return res


def _rms(x, w):
    return x * lax.rsqrt(jnp.mean(x * x, axis=-1, keepdims=True) + RMS_EPS) * w


def rms_fwd(h, w, name):
    return rowcall(name, _rms, [h], [w], [(D_MODEL, BF16)])[0]


def rms_bwd(h, du, dres, w, name):
    def fn(hb, dub, dresb, wb):
        _, vjp = jax.vjp(_rms, hb, wb)
        dh, dw = vjp(dub)
        return dh + dresb, dw
    return rowcall(name, fn, [h, du, dres], [w], [(D_MODEL, F32)], [(1, D_MODEL)])


def loss_head(h, target, w, name):
    def fn(hb, tb, wb):
        def f(hh, ww):
            err = _rms(hh, ww) - tb
            return 0.5 * jnp.sum(jnp.mean(err * err, axis=-1, keepdims=True), axis=0, keepdims=True)
        val, vjp = jax.vjp(f, hb, wb)
        dh, dw = vjp(jnp.ones((1, 1), F32))
        return dh, dw, jnp.broadcast_to(val, (1, LANES))
    return rowcall(name, fn, [h, target], [w], [(D_MODEL, F32)], [(1, D_MODEL), (1, LANES)])


def _gate(a, b, gs, ga):
    return _sigmoid(gs) * a + _sigmoid(ga) * b


def gate_fwd(a, b, gs, ga, name):
    return rowcall(name, _gate, [a, b, gs, ga], [], [(D_MODEL, BF16)])[0]


def gate_bwd(a, b, gs, ga, dm, name):
    def fn(ab, bb, gsb, gab, dmb):
        _, vjp = jax.vjp(_gate, ab, bb, gsb, gab)
        return vjp(dmb)
    return rowcall(name, fn, [a, b, gs, ga, dm], [], [(D_MODEL, BF16)] * 4)


def _swiglu(gu):
    return _silu(gu[:, :FFN_HIDDEN]) * gu[:, FFN_HIDDEN:]


def swiglu_fwd(gu, name):
    return rowcall(name, _swiglu, [gu], [], [(FFN_HIDDEN, BF16)])[0]


def swiglu_bwd(gu, dact, name):
    def fn(gub, db):
        _, vjp = jax.vjp(_swiglu, gub)
        return vjp(db.astype(F32))[0]
    return rowcall(name, fn, [gu, dact], [], [(2 * FFN_HIDDEN, BF16)])[0]


def _ssd_post(y, xs, z, dskip, normw):
    y = (y + dskip * xs) * _silu(z)
    gw = SSD_INNER // SSD_GROUPS
    parts = []
    for g in range(SSD_GROUPS):
        yg = y[:, g * gw:(g + 1) * gw]
        parts.append(yg * lax.rsqrt(jnp.mean(yg * yg, axis=-1, keepdims=True) + RMS_EPS))
    return jnp.concatenate(parts, axis=-1) * normw


def ssd_post_fwd(y, xc, z, dskip, normw, name):
    return rowcall(name, _ssd_post, [y, (xc, SSD_INNER), z], [dskip, normw], [(SSD_INNER, BF16)])[0]


def ssd_post_bwd(y, xc, z, dskip, normw, dyn, name):
    def fn(yb, xsb, zb, dynb, db, nb):
        _, vjp = jax.vjp(_ssd_post, yb, xsb, zb, db, nb)
        return vjp(dynb)
    return rowcall(name, fn, [y, (xc, SSD_INNER), z, dyn], [dskip, normw],
                   [(SSD_INNER, BF16)] * 3, [(1, SSD_INNER), (1, SSD_INNER)])


def _rope(t, cosf, sina, sinb):
    return t * cosf + pltpu.roll(t, LANES - ROPE_DIM // 2, 1) * sina + pltpu.roll(t, ROPE_DIM // 2, 1) * sinb


def rope_tables():
    half = ROPE_DIM // 2
    inv = ROPE_THETA ** (-jnp.arange(0, ROPE_DIM, 2, dtype=F32) / ROPE_DIM)
    ang = jnp.arange(SEQ, dtype=F32)[:, None] * inv[None, :]
    cos, sin = jnp.cos(ang), jnp.sin(ang)
    zeros = jnp.zeros((SEQ, LANES - ROPE_DIM), F32)
    z16 = jnp.zeros((SEQ, half), F32)
    cosf = jnp.concatenate([cos, cos, jnp.ones((SEQ, LANES - ROPE_DIM), F32)], axis=1)
    sina = jnp.concatenate([-sin, z16, zeros], axis=1)
    sinb = jnp.concatenate([z16, sin, zeros], axis=1)
    return cosf, sina, sinb


CONV_TC = 256


def _conv_pre(x, w, b, row):
    acc = x * w[SSD_CONV - 1:SSD_CONV, :] + b
    shifted = [x]
    for j in range(1, SSD_CONV):
        xs = jnp.where(row >= j, pltpu.roll(x, j, 0), 0.0)
        shifted.append(xs)
        acc = acc + xs * w[SSD_CONV - 1 - j:SSD_CONV - j, :]
    return acc, shifted


def conv_fwd(xbc, w, b, name):
    def body(x_ref, w_ref, b_ref, o_ref):
        row = lax.broadcasted_iota(jnp.int32, (SEQ, CONV_TC), 0)
        pre, _ = _conv_pre(x_ref[...].astype(F32), w_ref[...], b_ref[...], row)
        o_ref[...] = _silu(pre).astype(o_ref.dtype)
    return pl.pallas_call(
        body, name=name, grid=(SSD_CONV_CH // CONV_TC,),
        in_specs=[pl.BlockSpec((SEQ, CONV_TC), lambda i: (0, i)), pl.BlockSpec((SSD_CONV, CONV_TC), lambda i: (0, i)),
                  pl.BlockSpec((1, CONV_TC), lambda i: (0, i))],
        out_specs=pl.BlockSpec((SEQ, CONV_TC), lambda i: (0, i)),
        out_shape=SDS((SEQ, SSD_CONV_CH), BF16), compiler_params=_cparams(("parallel",)),
    )(xbc, w, b)


def conv_bwd(xbc, w, b, dxc, name):
    def body(x_ref, w_ref, b_ref, dy_ref, dx_ref, dw_ref, db_ref):
        row = lax.broadcasted_iota(jnp.int32, (SEQ, CONV_TC), 0)
        wv = w_ref[...]
        pre, shifted = _conv_pre(x_ref[...].astype(F32), wv, b_ref[...], row)
        sg = _sigmoid(pre)
        ds = dy_ref[...].astype(F32) * (sg * (1.0 + pre * (1.0 - sg)))
        dx = ds * wv[SSD_CONV - 1:SSD_CONV, :]
        for j in range(1, SSD_CONV):
            dsj = jnp.where(row < SEQ - j, pltpu.roll(ds, SEQ - j, 0), 0.0)
            dx = dx + dsj * wv[SSD_CONV - 1 - j:SSD_CONV - j, :]
        dx_ref[...] = dx.astype(dx_ref.dtype)
        for j in range(SSD_CONV):
            dw_ref[SSD_CONV - 1 - j:SSD_CONV - j, :] = jnp.sum(ds * shifted[j], axis=0, keepdims=True)
        db_ref[...] = jnp.sum(ds, axis=0, keepdims=True)
    return pl.pallas_call(
        body, name=name, grid=(SSD_CONV_CH // CONV_TC,),
        in_specs=[pl.BlockSpec((SEQ, CONV_TC), lambda i: (0, i)), pl.BlockSpec((SSD_CONV, CONV_TC), lambda i: (0, i)),
                  pl.BlockSpec((1, CONV_TC), lambda i: (0, i)), pl.BlockSpec((SEQ, CONV_TC), lambda i: (0, i))],
        out_specs=[pl.BlockSpec((SEQ, CONV_TC), lambda i: (0, i)), pl.BlockSpec((SSD_CONV, CONV_TC), lambda i: (0, i)),
                   pl.BlockSpec((1, CONV_TC), lambda i: (0, i))],
        out_shape=[SDS((SEQ, SSD_CONV_CH), BF16), SDS((SSD_CONV, SSD_CONV_CH), F32), SDS((1, SSD_CONV_CH), F32)],
        compiler_params=_cparams(("parallel",)),
    )(xbc, w, b, dxc)


N_CHUNKS = SEQ // SSD_CHUNK
N_PAIRS = SSD_HEADS // 2
PAIRS_PER_GROUP = N_PAIRS // SSD_GROUPS
B_OFF = SSD_INNER
C_OFF = SSD_INNER + SSD_GROUPS * SSD_STATE


def _ssd_prefix(dtr, dtr_t, dtb, dtb_t, alog, alog_t):
    ln = SSD_CHUNK
    dt = _softplus(dtr + dtb)
    dt_t = _softplus(dtr_t + dtb_t)
    dta = dt * (-jnp.exp(alog))
    dta_t = dt_t * (-jnp.exp(alog_t))
    r = lax.broadcasted_iota(jnp.int32, (ln, ln), 0)
    c = lax.broadcasted_iota(jnp.int32, (ln, ln), 1)
    a_cum = _dot((r >= c).astype(F32), dta, precision=HIGHEST)
    a_cum_t = _dot(dta_t, (r <= c).astype(F32), precision=HIGHEST)
    a_last = jnp.sum(dta_t, axis=1, keepdims=True)
    return dt, a_cum, a_cum_t, a_last


def _bein(spec, a, b):
    return jnp.einsum(spec, a.astype(BF16), b.astype(BF16), preferred_element_type=F32)


SSD_GROUPS_PER_BATCH = 4


def _ssd_group(xs3, bgs, cgs, h3, dt, a_cum, a_cum_t, a_last, *, groups):
    ln = SSD_CHUNK
    lane = lax.broadcasted_iota(jnp.int32, (ln, LANES), 1)
    sub = lax.broadcasted_iota(jnp.int32, (LANES, SSD_STATE), 0)
    row = lax.broadcasted_iota(jnp.int32, (ln, ln), 0)
    col = lax.broadcasted_iota(jnp.int32, (ln, ln), 1)
    lo = lane < SSD_HEAD_DIM
    causal = row >= col
    m_lo, m_hi, dts, acs, lasts, cds, cg3, bg3 = [], [], [], [], [], [], [], []
    for g, bg, cg in zip(groups, bgs, cgs):
        cb = _bdot(cg, bg, NT)
        for j in range(PAIRS_PER_GROUP):
            e0 = 2 * (g * PAIRS_PER_GROUP + j)
            e1 = e0 + 1
            c0, c1 = a_cum[:, e0:e0 + 1], a_cum[:, e1:e1 + 1]
            r0, r1 = a_cum_t[e0:e0 + 1, :], a_cum_t[e1:e1 + 1, :]
            l0, l1 = a_last[e0:e0 + 1, :], a_last[e1:e1 + 1, :]
            m_lo.append(cb * jnp.exp(jnp.where(causal, c0 - r0, -jnp.inf)))
            m_hi.append(cb * jnp.exp(jnp.where(causal, c1 - r1, -jnp.inf)))
            dts.append(jnp.where(lo, dt[:, e0:e0 + 1], dt[:, e1:e1 + 1]))
            acs.append(jnp.where(lo, c0, c1))
            lasts.append(jnp.where(lo, l0, l1))
            cds.append(jnp.exp(jnp.where(sub < SSD_HEAD_DIM, l0, l1)))
            cg3.append(cg)
            bg3.append(bg)
    xd = xs3 * jnp.stack(dts)
    acum = jnp.stack(acs)
    y = (_bein("pls,psq->plq", jnp.stack(m_lo), jnp.where(lo[None], xd, 0.0))
         + _bein("pls,psq->plq", jnp.stack(m_hi), jnp.where(lo[None], 0.0, xd)))
    y = y + _bein("pln,pqn->plq", jnp.stack(cg3), h3) * jnp.exp(acum)
    st = _bein("plq,pln->pqn", xd * jnp.exp(jnp.stack(lasts) - acum), jnp.stack(bg3))
    h_out = h3 * jnp.stack(cds) + st
    return y, h_out


def _group_slabs(groups):
    pairs = [g * PAIRS_PER_GROUP + j for g in groups for j in range(PAIRS_PER_GROUP)]
    return [slice(p * LANES, (p + 1) * LANES) for p in pairs]


def _group_batches():
    return [tuple(range(g, g + SSD_GROUPS_PER_BATCH)) for g in range(0, SSD_GROUPS, SSD_GROUPS_PER_BATCH)]


def _bc_of(xc_ref, g):
    return (xc_ref[:, B_OFF + g * SSD_STATE:B_OFF + (g + 1) * SSD_STATE].astype(F32),
            xc_ref[:, C_OFF + g * SSD_STATE:C_OFF + (g + 1) * SSD_STATE].astype(F32))


def _ssd_in_specs(chunk_of):
    return [
        pl.BlockSpec((SSD_CHUNK, SSD_CONV_CH), lambda i: (chunk_of(i), 0)),
        pl.BlockSpec((SSD_CHUNK, HPAD), lambda i: (chunk_of(i), 0)),
        pl.BlockSpec((HPAD, SSD_CHUNK), lambda i: (0, chunk_of(i))),
        pl.BlockSpec((1, HPAD), lambda i: (0, 0)), pl.BlockSpec((HPAD, 1), lambda i: (0, 0)),
        pl.BlockSpec((1, HPAD), lambda i: (0, 0)), pl.BlockSpec((HPAD, 1), lambda i: (0, 0)),
    ]


def ssd_fwd(xc, dtr, dtr_t, dtb, dtb_t, alog, alog_t, name):
    def body(xc_ref, dtr_ref, dtrt_ref, dtb_ref, dtbt_ref, al_ref, alt_ref, y_ref, hs_ref, h_scr):
        @pl.when(pl.program_id(0) == 0)
        def _():
            h_scr[...] = jnp.zeros_like(h_scr)

        hs_ref[0] = h_scr[...]
        dt, a_cum, a_cum_t, a_last = _ssd_prefix(dtr_ref[...], dtrt_ref[...], dtb_ref[...], dtbt_ref[...],
                                                  al_ref[...], alt_ref[...])
        for groups in _group_batches():
            slabs = _group_slabs(groups)
            bgs, cgs = zip(*[_bc_of(xc_ref, g) for g in groups])
            xs3 = jnp.stack([xc_ref[:, sl] for sl in slabs]).astype(F32)
            h3 = jnp.stack([h_scr[sl, :] for sl in slabs])
            y3, h3_out = _ssd_group(xs3, bgs, cgs, h3, dt, a_cum, a_cum_t, a_last, groups=groups)
            for j, sl in enumerate(slabs):
                y_ref[:, sl] = y3[j].astype(y_ref.dtype)
                h_scr[sl, :] = h3_out[j]

    return pl.pallas_call(
        body, name=name, grid=(N_CHUNKS,), in_specs=_ssd_in_specs(lambda i: i),
        out_specs=[pl.BlockSpec((SSD_CHUNK, SSD_INNER), lambda i: (i, 0)),
                   pl.BlockSpec((1, SSD_INNER, SSD_STATE), lambda i: (i, 0, 0))],
        out_shape=[SDS((SEQ, SSD_INNER), BF16), SDS((N_CHUNKS, SSD_INNER, SSD_STATE), F32)],
        scratch_shapes=[pltpu.VMEM((SSD_INNER, SSD_STATE), F32)],
        compiler_params=_cparams(("arbitrary",)),
    )(xc, dtr, dtr_t, dtb, dtb_t, alog, alog_t)


def ssd_bwd(xc, dtr, dtr_t, dtb, dtb_t, alog, alog_t, hs, dy, dxs_extra, name):
    rev = lambda i: N_CHUNKS - 1 - i

    def body(xc_ref, dtr_ref, dtrt_ref, dtb_ref, dtbt_ref, al_ref, alt_ref, hs_ref, dy_ref, dxe_ref,
             dxc_ref, ddtr_ref, ddtrt_ref, ddtb_ref, ddtbt_ref, dal_ref, dalt_ref, dh_scr):
        @pl.when(pl.program_id(0) == 0)
        def _():
            dh_scr[...] = jnp.zeros_like(dh_scr)
            for r in (ddtb_ref, ddtbt_ref, dal_ref, dalt_ref):
                r[...] = jnp.zeros_like(r)

        prefix_in = (dtr_ref[...], dtrt_ref[...], dtb_ref[...], dtbt_ref[...], al_ref[...], alt_ref[...])
        (dt, a_cum, a_cum_t, a_last), prefix_vjp = jax.vjp(_ssd_prefix, *prefix_in)
        d_dt = jnp.zeros_like(dt)
        d_acum = jnp.zeros_like(a_cum)
        d_acum_t = jnp.zeros_like(a_cum_t)
        d_alast = jnp.zeros_like(a_last)
        for groups in _group_batches():
            slabs = _group_slabs(groups)
            bgs, cgs = zip(*[_bc_of(xc_ref, g) for g in groups])
            xs3 = jnp.stack([xc_ref[:, sl] for sl in slabs]).astype(F32)
            h3 = jnp.stack([hs_ref[0, sl, :] for sl in slabs])
            _, vjp = jax.vjp(functools.partial(_ssd_group, groups=groups), xs3, bgs, cgs, h3, dt, a_cum, a_cum_t, a_last)
            dy3 = jnp.stack([dy_ref[:, sl] for sl in slabs]).astype(F32)
            dh3 = jnp.stack([dh_scr[sl, :] for sl in slabs])
            dxs3, d_bgs, d_cgs, dh3_in, ddt, dac, dact, dal = vjp((dy3, dh3))
            for j, sl in enumerate(slabs):
                dxc_ref[:, sl] = (dxs3[j] + dxe_ref[:, sl].astype(F32)).astype(dxc_ref.dtype)
                dh_scr[sl, :] = dh3_in[j]
            d_dt, d_acum, d_acum_t, d_alast = d_dt + ddt, d_acum + dac, d_acum_t + dact, d_alast + dal
            for g, d_bg, d_cg in zip(groups, d_bgs, d_cgs):
                dxc_ref[:, B_OFF + g * SSD_STATE:B_OFF + (g + 1) * SSD_STATE] = d_bg.astype(dxc_ref.dtype)
                dxc_ref[:, C_OFF + g * SSD_STATE:C_OFF + (g + 1) * SSD_STATE] = d_cg.astype(dxc_ref.dtype)
        g_dtr, g_dtrt, g_dtb, g_dtbt, g_al, g_alt = prefix_vjp((d_dt, d_acum, d_acum_t, d_alast))
        ddtr_ref[...] = g_dtr
        ddtrt_ref[...] = g_dtrt
        ddtb_ref[...] += g_dtb
        ddtbt_ref[...] += g_dtbt
        dal_ref[...] += g_al
        dalt_ref[...] += g_alt

    in_specs = _ssd_in_specs(rev) + [
        pl.BlockSpec((1, SSD_INNER, SSD_STATE), lambda i: (rev(i), 0, 0)),
        pl.BlockSpec((SSD_CHUNK, SSD_INNER), lambda i: (rev(i), 0)),
        pl.BlockSpec((SSD_CHUNK, SSD_INNER), lambda i: (rev(i), 0)),
    ]
    out_specs = [
        pl.BlockSpec((SSD_CHUNK, SSD_CONV_CH), lambda i: (rev(i), 0)),
        pl.BlockSpec((SSD_CHUNK, HPAD), lambda i: (rev(i), 0)),
        pl.BlockSpec((HPAD, SSD_CHUNK), lambda i: (0, rev(i))),
        pl.BlockSpec((1, HPAD), lambda i: (0, 0)), pl.BlockSpec((HPAD, 1), lambda i: (0, 0)),
        pl.BlockSpec((1, HPAD), lambda i: (0, 0)), pl.BlockSpec((HPAD, 1), lambda i: (0, 0)),
    ]
    out_shape = [SDS((SEQ, SSD_CONV_CH), BF16), SDS((SEQ, HPAD), F32), SDS((HPAD, SEQ), F32),
                 SDS((1, HPAD), F32), SDS((HPAD, 1), F32), SDS((1, HPAD), F32), SDS((HPAD, 1), F32)]
    return pl.pallas_call(
        body, name=name, grid=(N_CHUNKS,), in_specs=in_specs, out_specs=out_specs, out_shape=out_shape,
        scratch_shapes=[pltpu.VMEM((SSD_INNER, SSD_STATE), F32)],
        compiler_params=_cparams(("arbitrary",)),
    )(xc, dtr, dtr_t, dtb, dtb_t, alog, alog_t, hs, dy, dxs_extra)


ATTN_SCALE = ATTN_HEAD_DIM ** -0.5


UNITS_PER_PATTERN = SEQ // ATTN_BLOCK
ATTN_BATCH_FWD = 8
ATTN_BATCH_BWD = 16


def _for_unit_batches(batch, per_trip):
    for g, d in enumerate(ATTN_DILATIONS):
        nb = UNITS_PER_PATTERN // d
        span = d * ATTN_BLOCK

        def trip(t, carry, g=g, d=d, nb=nb, span=span):
            units = []
            for j in range(per_trip):
                i = t * per_trip + j
                r = i >> (nb.bit_length() - 1)
                n = i & (nb - 1)
                start = r + n * span
                prev = jnp.where(n > 0, start - span, start)
                units.append((pl.ds(start, ATTN_BLOCK, stride=d), pl.ds(prev, ATTN_BLOCK, stride=d), n > 0))
            batch(g, units)
            return carry
        lax.fori_loop(0, UNITS_PER_PATTERN // per_trip, trip, 0)


def _unit_operands(units, q_scr, k_scr, v_scr):
    def pair(scr, rows, prows):
        return jnp.concatenate([scr[prows, :], scr[rows, :]], axis=0)
    qb = jnp.stack([q_scr[rows, :] for rows, _, _ in units]).astype(BF16)
    kb = jnp.stack([pair(k_scr, rows, prows) for rows, prows, _ in units]).astype(BF16)
    vb = jnp.stack([pair(v_scr, rows, prows) for rows, prows, _ in units]).astype(BF16)
    return qb, kb, vb


def _unit_scores(qb, kb, units):
    s = jnp.einsum("bqd,bkd->bqk", qb, kb, preferred_element_type=F32) * ATTN_SCALE
    qi = lax.broadcasted_iota(jnp.int32, (ATTN_BLOCK, 2 * ATTN_BLOCK), 0)
    kj = lax.broadcasted_iota(jnp.int32, (ATTN_BLOCK, 2 * ATTN_BLOCK), 1)
    own = (kj >= ATTN_BLOCK) & (kj - ATTN_BLOCK <= qi)
    before = (kj < ATTN_BLOCK) & (kj >= qi)
    keep = jnp.stack([own | (before & has_prev) for _, _, has_prev in units])
    return jnp.where(keep, s, -jnp.inf)


def _head_specs(n_q_groups):
    blk = (SEQ, ATTN_HEAD_DIM)
    q_specs = [pl.BlockSpec(blk, functools.partial(lambda h, g: (0, g * ATTN_KV_HEADS + h), g=g)) for g in range(n_q_groups)]
    head = pl.BlockSpec(blk, lambda h: (0, h))
    table = pl.BlockSpec(blk, lambda h: (0, 0))
    return q_specs, head, table


def attn_fwd(q, k, v, tabs, name):
    q_specs, head, table = _head_specs(ATTN_N_PAT)

    def body(q0_ref, q1_ref, q2_ref, k_ref, v_ref, c_ref, sa_ref, sb_ref, y_ref, lse_ref, *scr):
        qs, og, ls, ks, vs = scr[0:3], scr[3:6], scr[6:9], scr[9], scr[10]
        c, sa, sb = c_ref[...], sa_ref[...], sb_ref[...]
        for g, q_ref in enumerate((q0_ref, q1_ref, q2_ref)):
            qs[g][...] = _rope(q_ref[...].astype(F32), c, sa, sb)
        ks[...] = _rope(k_ref[...].astype(F32), c, sa, sb)
        vs[...] = v_ref[...].astype(F32)

        def batch(g, units):
            qb, kb, vb = _unit_operands(units, qs[g], ks, vs)
            s = _unit_scores(qb, kb, units)
            m = jnp.max(s, axis=2, keepdims=True)
            p = jnp.exp(s - m)
            l = jnp.sum(p, axis=2, keepdims=True)
            o = jnp.einsum("bqk,bkd->bqd", p.astype(BF16), vb, preferred_element_type=F32) / l
            lse_b = m + jnp.log(l)
            for j, (rows, _, _) in enumerate(units):
                og[g][rows, :] = o[j]
                ls[g][rows, :] = jnp.broadcast_to(lse_b[j], (ATTN_BLOCK, LANES))

        _for_unit_batches(batch, ATTN_BATCH_FWD)
        l0, l1, l2 = ls[0][...], ls[1][...], ls[2][...]
        m = jnp.maximum(jnp.maximum(l0, l1), l2)
        e0, e1, e2 = jnp.exp(l0 - m), jnp.exp(l1 - m), jnp.exp(l2 - m)
        den = e0 + e1 + e2
        y_ref[...] = ((e0 * og[0][...] + e1 * og[1][...] + e2 * og[2][...]) / den).astype(y_ref.dtype)
        lse_ref[...] = m + jnp.log(den)

    blk = (SEQ, ATTN_HEAD_DIM)
    return pl.pallas_call(
        body, name=name, grid=(ATTN_KV_HEADS,), in_specs=[*q_specs, head, head, table, table, table],
        out_specs=[head, head], out_shape=[SDS((SEQ, ATTN_OUT), BF16), SDS((SEQ, ATTN_OUT), F32)],
        scratch_shapes=[pltpu.VMEM(blk, F32)] * (3 * ATTN_N_PAT + 2),
        compiler_params=_cparams(("parallel",)),
    )(q, q, q, k, v, *tabs)


def attn_bwd(q, k, v, tabs, y, lse, dy, name):
    q_specs, head, table = _head_specs(ATTN_N_PAT)

    def body(q0_ref, q1_ref, q2_ref, k_ref, v_ref, c_ref, sa_ref, sb_ref, y_ref, lse_ref, dy_ref,
             dq0_ref, dq1_ref, dq2_ref, dk_ref, dv_ref, *scr):
        qs, dqs, ks, dks, dd, dvs, vs = scr[0:3], scr[3:6], scr[6], scr[7], scr[8], scr[9], scr[10]
        c, sa, sb = c_ref[...], sa_ref[...], sb_ref[...]
        for g, q_ref in enumerate((q0_ref, q1_ref, q2_ref)):
            qs[g][...] = _rope(q_ref[...].astype(F32), c, sa, sb)
        ks[...] = _rope(k_ref[...].astype(F32), c, sa, sb)
        vs[...] = v_ref[...].astype(F32)
        dks[...] = jnp.zeros_like(dks)
        dvs[...] = jnp.zeros_like(dvs)
        dyv = dy_ref[...]
        dd[...] = jnp.broadcast_to(jnp.sum(dyv * y_ref[...].astype(F32), axis=1, keepdims=True), dd.shape)

        def batch(g, units):
            qb, kb, vb = _unit_operands(units, qs[g], ks, vs)
            dob = jnp.stack([dy_ref[rows, :] for rows, _, _ in units]).astype(BF16)
            lse_b = jnp.stack([lse_ref[rows, :][:, 0:1] for rows, _, _ in units])
            dsum_b = jnp.stack([dd[rows, :][:, 0:1] for rows, _, _ in units])
            p = jnp.exp(_unit_scores(qb, kb, units) - lse_b)
            dp = jnp.einsum("bqd,bkd->bqk", dob, vb, preferred_element_type=F32)
            ds = (p * (dp - dsum_b) * ATTN_SCALE).astype(BF16)
            dq = jnp.einsum("bqk,bkd->bqd", ds, kb, preferred_element_type=F32)
            dk = jnp.einsum("bqk,bqd->bkd", ds, qb, preferred_element_type=F32)
            dv = jnp.einsum("bqk,bqd->bkd", p.astype(BF16), dob, preferred_element_type=F32)
            for j, (rows, prows, _) in enumerate(units):
                dqs[g][rows, :] = dq[j]
                dks[prows, :] += dk[j, :ATTN_BLOCK]
                dks[rows, :] += dk[j, ATTN_BLOCK:]
                dvs[prows, :] += dv[j, :ATTN_BLOCK]
                dvs[rows, :] += dv[j, ATTN_BLOCK:]

        _for_unit_batches(batch, ATTN_BATCH_BWD)
        for g, dq_ref in enumerate((dq0_ref, dq1_ref, dq2_ref)):
            dq_ref[...] = _rope(dqs[g][...], c, -sa, -sb).astype(dq_ref.dtype)
        dk_ref[...] = _rope(dks[...], c, -sa, -sb).astype(dk_ref.dtype)
        dv_ref[...] = dvs[...].astype(dv_ref.dtype)

    blk = (SEQ, ATTN_HEAD_DIM)
    out = SDS((SEQ, ATTN_OUT), BF16)
    return pl.pallas_call(
        body, name=name, grid=(ATTN_KV_HEADS,), in_specs=[*q_specs, head, head, table, table, table, head, head, head],
        out_specs=[head] * 5, out_shape=[out] * 5,
        scratch_shapes=[pltpu.VMEM(blk, F32)] * (2 * ATTN_N_PAT + 5),
        compiler_params=_cparams(("parallel",)),
    )(q, q, q, k, v, *tabs, y, lse, dy)


def layer_fwd(h, getw, prefetch, small, tabs, li):
    n = f"l{li}_"
    sv = {}
    w = dict(getw(0, h))
    u = rms_fwd(h, small["norm_mix"], n + "rms_mix")
    z = matmul(u, w["w_z"], name=n + "mm_z", tb=True, out_dtype=BF16)
    prefetch(1, z)
    xbc = matmul(u, w["w_xbc"], name=n + "mm_xbc", tb=True, out_dtype=BF16)
    dtr = matmul(u, w["w_dt"], name=n + "mm_dt", tb=True)
    q = matmul(u, w["w_q"], name=n + "mm_q", tb=True, out_dtype=BF16)
    k = matmul(u, w["w_k"], name=n + "mm_k", tb=True, out_dtype=BF16)
    v = matmul(u, w["w_v"], name=n + "mm_v", tb=True, out_dtype=BF16)
    gs = matmul(u, w["w_gs"], name=n + "mm_gs", tb=True, out_dtype=BF16)
    ga = matmul(u, w["w_ga"], name=n + "mm_ga", tb=True, out_dtype=BF16)
    xc = conv_fwd(xbc, w["conv_w"], small["conv_b"], n + "conv")
    dtr_t = dtr.T
    y_ssd, hs = ssd_fwd(xc, dtr, dtr_t, small["dt_bias"], small["dt_bias"].T, small["a_log"], small["a_log"].T, n + "ssd")
    yn = ssd_post_fwd(y_ssd, xc, z, small["d_skip_x"], small["ssd_norm"], n + "ssd_post")
    y_attn, lse = attn_fwd(q, k, v, tabs, n + "attn")
    w.update(getw(1, y_ssd))
    a = matmul(yn, w["w_ssd_branch"], name=n + "mm_a", out_dtype=BF16)
    b = matmul(y_attn, w["w_attn_branch"], name=n + "mm_b", out_dtype=BF16)
    merged = gate_fwd(a, b, gs, ga, n + "gate")
    h1 = matmul(merged, w["w_out"], name=n + "mm_o", add=h)
    w.update(getw(2, h1))
    u2 = rms_fwd(h1, small["norm_ffn"], n + "rms_ffn")
    gu = matmul(u2, w["w_gate_up"], name=n + "mm_gu", tb=True, out_dtype=BF16)
    act = swiglu_fwd(gu, n + "swiglu")
    h2 = matmul(act, w["w_down"], name=n + "mm_down", add=h1)
    sv.update(h=h, u=u, z=z, xbc=xbc, dtr=dtr, dtr_t=dtr_t, gs=gs, ga=ga, xc=xc, y_ssd=y_ssd, hs=hs, yn=yn,
              q=q, k=k, v=v, y_attn=y_attn, lse=lse, a=a, b=b, merged=merged, h1=h1, u2=u2, gu=gu, act=act, w=w)
    return h2, sv


def layer_bwd(dh, sv, small, tabs, li, emit):
    n = f"l{li}_b_"
    w = sv["w"]
    gw, gsm = {}, {}
    dact = matmul(dh, w["w_down"], name=n + "mm_dact", tb=True, out_dtype=BF16)
    gw["w_down"] = matmul(sv["act"], dh, name=n + "mm_dwdown", ta=True, out_dtype=BF16)
    dgu = swiglu_bwd(sv["gu"], dact, n + "swiglu")
    gw["w_gate_up"] = matmul(dgu, sv["u2"], name=n + "mm_dwgu", ta=True, out_dtype=BF16)
    tok = emit(2, gw)
    du2 = matmul(dgu, w["w_gate_up"], name=n + "mm_du2")
    dh1, gsm["norm_ffn"] = rms_bwd(sv["h1"], du2, dh, small["norm_ffn"] + tok, n + "rms_ffn")
    dmerged = matmul(dh1, w["w_out"], name=n + "mm_dmerged", tb=True)
    gw["w_out"] = matmul(sv["merged"], dh1, name=n + "mm_dwo", ta=True, out_dtype=BF16)
    da, db, dgs, dga = gate_bwd(sv["a"], sv["b"], sv["gs"], sv["ga"], dmerged, n + "gate")
    gw["w_ssd_branch"] = matmul(sv["yn"], da, name=n + "mm_dwa", ta=True, out_dtype=BF16)
    gw["w_attn_branch"] = matmul(sv["y_attn"], db, name=n + "mm_dwb", ta=True, out_dtype=BF16)
    tok = emit(1, gw)
    dyn = matmul(da, w["w_ssd_branch"], name=n + "mm_dyn", tb=True, out_dtype=BF16)
    dyattn = matmul(db, w["w_attn_branch"], name=n + "mm_dyattn", tb=True)
    dy_ssd, dxs_extra, dz, gsm["d_skip_x"], gsm["ssd_norm"] = ssd_post_bwd(
        sv["y_ssd"], sv["xc"], sv["z"], small["d_skip_x"] + tok, small["ssd_norm"], dyn, n + "ssd_post")
    dxc, ddtr, ddtr_t, ddtb, ddtb_t, dal, dal_t = ssd_bwd(
        sv["xc"], sv["dtr"], sv["dtr_t"], small["dt_bias"], small["dt_bias"].T, small["a_log"], small["a_log"].T,
        sv["hs"], dy_ssd, dxs_extra, n + "ssd")
    ddtr = (ddtr + ddtr_t.T).astype(BF16)
    gsm["dt_bias"] = ddtb + ddtb_t.T
    gsm["a_log"] = dal + dal_t.T
    dxbc, gw["conv_w"], gsm["conv_b"] = conv_bwd(sv["xbc"], w["conv_w"], small["conv_b"], dxc, n + "conv")
    dq0, dq1, dq2, dk, dv = attn_bwd(sv["q"], sv["k"], sv["v"], tabs, sv["y_attn"], sv["lse"], dyattn, n + "attn")
    u = sv["u"]
    segs = [("w_z", dz), ("w_xbc", dxbc), ("w_dt", ddtr), ("w_q0", dq0), ("w_q1", dq1), ("w_q2", dq2),
            ("w_k", dk), ("w_v", dv), ("w_gs", dgs), ("w_ga", dga)]
    gin = [matmul(dseg, u, name=n + "mm_d" + key, ta=True, out_dtype=BF16) for key, dseg in segs]
    gin[2] = gin[2][:SSD_HEADS]
    gw["w_in"] = jnp.concatenate(gin, axis=0)
    tok = emit(0, gw)
    du = jnp.zeros((SEQ, D_MODEL), F32) + tok
    for key, dseg in segs:
        du = matmul(dseg, w[key], name=n + "mm_du_" + key, add=du)
    dh0, gsm["norm_mix"] = rms_bwd(sv["h"], du, dh1, small["norm_mix"] + tok, n + "rms_mix")
    return dh0, gsm


def _my_place():
    return lax.axis_index("x"), lax.axis_index("y"), lax.axis_index("c")


def _flip(place, k):
    x, y, c = place
    return (1 - x if k & 4 else x, 1 - y if k & 2 else y, 1 - c if k & 1 else c)


def _index(place):
    return 4 * place[0] + 2 * place[1] + place[2]


ANY = pl.BlockSpec(memory_space=pl.ANY)
CHIP_FLIPS = (4, 2, 6)
SELF_AND_CHIPS = (0,) + CHIP_FLIPS


def all_gather(xs, name):
    na = len(xs)

    def body(*refs):
        x_refs, o_refs = refs[:na], refs[na:2 * na]
        send_sems, recv_sems, local_sems = refs[2 * na:]
        me = _my_place()
        sibling = _flip(me, 1)
        chips = [_flip(me, f) for f in CHIP_FLIPS]

        def copy(a, kk, block, to, src=None):
            dst = o_refs[a].at[_index(block)]
            return pltpu.make_async_remote_copy(
                src_ref=dst if src is None else src, dst_ref=dst, send_sem=send_sems.at[a, kk],
                recv_sem=recv_sems.at[a, kk], device_id=to, device_id_type=MESH)

        mine = [pltpu.make_async_copy(x_refs[a], o_refs[a].at[_index(me)], local_sems.at[a]) for a in range(na)]
        for cp in mine:
            cp.start()
        first = []
        for j, chip in enumerate(chips):
            first += [copy(a, 1 + j, me, chip, src=x_refs[a]) for a in range(na)]
        first += [copy(a, 0, me, sibling, src=x_refs[a]) for a in range(na)]
        for cp in first:
            cp.start()
        passed = []
        for j, chip in enumerate(chips):
            for a in range(na):
                copy(a, 1 + j, chip, me).wait_recv()
                cp = copy(a, 4 + j, chip, sibling)
                cp.start()
                passed.append(cp)
        for a in range(na):
            copy(a, 0, sibling, me).wait_recv()
        for j, chip in enumerate(chips):
            for a in range(na):
                copy(a, 4 + j, _flip(chip, 1), me).wait_recv()
        for cp in first + passed:
            cp.wait_send()
        for cp in mine:
            cp.wait()

    return pl.pallas_call(
        body, name=name, in_specs=[ANY] * na, out_specs=[ANY] * na,
        out_shape=[SDS((N_DEV,) + t.shape, t.dtype) for t in xs],
        scratch_shapes=[pltpu.SemaphoreType.DMA((na, N_DEV - 1)), pltpu.SemaphoreType.DMA((na, N_DEV - 1)),
                        pltpu.SemaphoreType.DMA((na,))],
    )(*xs)


HBM = pl.BlockSpec(memory_space=pltpu.HBM)
SEM = pl.BlockSpec(memory_space=pltpu.SEMAPHORE)
EFFECT = pltpu.SideEffectType.DATAFLOW_SIDE_EFFECTING
N_PEERS = N_DEV - 1


def _split_copy(src_ref, land_ref, send_sem, recv_sem, me, kk, scatter, landed_from_peer):
    peer = _flip(me, kk)
    src = src_ref.at[_index(peer)] if scatter else src_ref
    dst = land_ref.at[_index(peer if landed_from_peer else me)]
    return pltpu.make_async_remote_copy(src_ref=src, dst_ref=dst, send_sem=send_sem, recv_sem=recv_sem,
                                        device_id=peer, device_id_type=MESH)


ALL_PEERS = tuple(range(1, N_DEV))
EVERYONE = (0,) + ALL_PEERS


def exchange_start(srcs, lands, group_sizes, scatter, name, peers=ALL_PEERS):
    na, ng = len(srcs), len(group_sizes)

    def body(*refs):
        s_refs, l_refs = refs[:na], refs[na:2 * na]
        sems = refs[2 * na:2 * na + 2 * ng]
        token = refs[-1]
        me = _my_place()
        a = 0
        for gi, gsz in enumerate(group_sizes):
            for j in range(gsz):
                for pi, kk in enumerate(peers):
                    slot = j * len(peers) + pi
                    _split_copy(s_refs[a], l_refs[a], sems[2 * gi].at[slot], sems[2 * gi + 1].at[slot],
                                me, kk, scatter, False).start()
                a += 1
        token[...] = jnp.zeros_like(token)

    sem_shapes = []
    for gsz in group_sizes:
        sem_shapes += [pltpu.SemaphoreType.DMA((gsz * len(peers),))] * 2
    ins = [pltpu.with_memory_space_constraint(t, pltpu.HBM) for t in (*srcs, *lands)]
    res = pl.pallas_call(
        body, name=name, in_specs=[HBM] * (2 * na),
        out_specs=[SEM] * (2 * ng) + [HBM] * (2 * na) + [pl.BlockSpec(memory_space=pltpu.VMEM)],
        out_shape=sem_shapes + [pltpu.HBM(t.shape, t.dtype) for t in ins] + [SDS((8, LANES), F32)],
        input_output_aliases={i: 2 * ng + i for i in range(2 * na)},
        compiler_params=pltpu.CompilerParams(has_side_effects=EFFECT),
    )(*ins)
    sems = [(res[2 * gi], res[2 * gi + 1]) for gi in range(ng)]
    thru = res[2 * ng:2 * ng + 2 * na]
    return sems, thru[:na], thru[na:], res[-1]


def _wait_split_copies(s_refs, l_refs, send_sems, recv_sems, scatter, peers):
    me = _my_place()
    for j in range(len(s_refs)):
        for pi, kk in enumerate(peers):
            slot = j * len(peers) + pi
            cp = _split_copy(s_refs[j], l_refs[j], send_sems.at[slot], recv_sems.at[slot], me, kk, scatter, True)
            cp.wait_send()
            cp.wait_recv()


def exchange_wait(srcs, lands, sems, after, scatter, name, peers=ALL_PEERS):
    n = len(srcs)

    def body(*refs):
        s_refs, l_refs = refs[:n], refs[n:2 * n]
        _wait_split_copies(s_refs, l_refs, refs[2 * n], refs[2 * n + 1], scatter, peers)

    res = pl.pallas_call(
        body, name=name, in_specs=[HBM] * (2 * n) + [SEM, SEM, ANY], out_specs=[HBM] * (2 * n),
        out_shape=[pltpu.HBM(t.shape, t.dtype) for t in (*srcs, *lands)],
        input_output_aliases={i: i for i in range(2 * n)},
        compiler_params=pltpu.CompilerParams(has_side_effects=EFFECT),
    )(*srcs, *lands, sems[0], sems[1], after)
    return res[n:]


def _sibling_copies(l_refs, send_sems, recv_sems, arriving):
    me = _my_place()
    sibling = _flip(me, 1)
    held = [me] + [_flip(me, f) for f in CHIP_FLIPS]
    copies = []
    for j, land in enumerate(l_refs):
        for bi, place in enumerate(held):
            blk = land.at[_index(_flip(place, 1) if arriving else place)]
            slot = j * len(held) + bi
            copies.append(pltpu.make_async_remote_copy(src_ref=blk, dst_ref=blk, send_sem=send_sems.at[slot],
                                                       recv_sem=recv_sems.at[slot], device_id=sibling, device_id_type=MESH))
    return copies


def gather_forward(srcs, lands, sems, after, name):
    n = len(srcs)

    def body(*refs):
        s_refs, l_refs = refs[:n], refs[n:2 * n]
        _wait_split_copies(s_refs, l_refs, refs[2 * n], refs[2 * n + 1], False, SELF_AND_CHIPS)
        for cp in _sibling_copies(l_refs, refs[2 * n + 3], refs[2 * n + 4], False):
            cp.start()

    n_slots = n * (1 + len(CHIP_FLIPS))
    res = pl.pallas_call(
        body, name=name, in_specs=[HBM] * (2 * n) + [SEM, SEM, ANY],
        out_specs=[SEM, SEM] + [HBM] * (2 * n),
        out_shape=[pltpu.SemaphoreType.DMA((n_slots,))] * 2 + [pltpu.HBM(t.shape, t.dtype) for t in (*srcs, *lands)],
        input_output_aliases={i: 2 + i for i in range(2 * n)},
        compiler_params=pltpu.CompilerParams(has_side_effects=EFFECT),
    )(*srcs, *lands, sems[0], sems[1], after)
    return (res[0], res[1]), res[2 + n:]


def gather_finish(lands, sems, after, name):
    n = len(lands)

    def body(*refs):
        l_refs = refs[:n]
        for cp in _sibling_copies(l_refs, refs[n], refs[n + 1], True):
            cp.wait_send()
            cp.wait_recv()

    return pl.pallas_call(
        body, name=name, in_specs=[HBM] * n + [SEM, SEM, ANY], out_specs=[HBM] * n,
        out_shape=[pltpu.HBM(t.shape, t.dtype) for t in lands],
        input_output_aliases={i: i for i in range(n)},
        compiler_params=pltpu.CompilerParams(has_side_effects=EFFECT),
    )(*lands, sems[0], sems[1], after)


def landing_zone(block):
    return lax.empty((N_DEV,) + block.shape, block.dtype)


def sum_parts(parts, name, row_major_3d=False):
    _, r, c = parts.shape
    tc = _pick(c, (256, 128))

    def body(p_ref, o_ref):
        acc = p_ref[0].astype(F32)
        for i in range(1, N_DEV):
            acc = acc + p_ref[i].astype(F32)
        if row_major_3d:
            o_ref[:, 0, :] = acc
        else:
            o_ref[...] = acc

    out_spec = pl.BlockSpec((r, 1, tc), lambda i: (0, 0, i)) if row_major_3d else pl.BlockSpec((r, tc), lambda i: (0, i))
    return pl.pallas_call(
        body, name=name, grid=(c // tc,), in_specs=[pl.BlockSpec((N_DEV, r, tc), lambda i: (0, 0, i))],
        out_specs=out_spec, out_shape=SDS((r, 1, c) if row_major_3d else (r, c), F32),
        compiler_params=_cparams(("parallel",)),
    )(parts)


ADAMW_BLOCK_BYTES = 2 * 1024 * 1024


def adamw(w, g, m, v, name):
    shape = w.shape
    lay, rows, cols = ((1, 1) + shape)[-3:]
    tr = _pick(rows, (256, 128))
    tc = cols if tr * cols * 4 <= ADAMW_BLOCK_BYTES else _pick(cols, (256, 128))
    c1 = 1.0 / (1.0 - ADAM_B1 ** ADAM_STEP)
    c2 = 1.0 / (1.0 - ADAM_B2 ** ADAM_STEP)

    def body(w_ref, g_ref, m_ref, v_ref, d_ref, nm_ref, nv_ref):
        gg = g_ref[...]
        nm = ADAM_B1 * m_ref[...] + (1.0 - ADAM_B1) * gg
        nv = ADAM_B2 * v_ref[...] + (1.0 - ADAM_B2) * (gg * gg)
        d_ref[...] = -ADAM_LR * ((nm * c1) / (jnp.sqrt(nv * c2) + ADAM_EPS) + ADAM_WD * w_ref[...])
        nm_ref[...] = nm
        nv_ref[...] = nv

    spec = pl.BlockSpec((1, tr, tc), lambda l, i, j: (l, i, j))
    outs = pl.pallas_call(
        body, name=name, grid=(lay, rows // tr, cols // tc), in_specs=[spec] * 4, out_specs=[spec] * 3,
        out_shape=[SDS((lay, rows, cols), F32)] * 3, compiler_params=_cparams(("parallel",) * 3),
    )(*[t.reshape(lay, rows, cols) for t in (w, g, m, v)])
    return [o.reshape(shape) for o in outs]


def adamw_layer_inner(w, gs, m, v, name):
    rows, lay, cols = w.shape
    tr = _pick(rows, (256, 220, 128))
    c1 = 1.0 / (1.0 - ADAM_B1 ** ADAM_STEP)
    c2 = 1.0 / (1.0 - ADAM_B2 ** ADAM_STEP)

    def body(*refs):
        w_ref, m_ref, v_ref = refs[:3]
        g_refs = refs[3:3 + lay]
        go_ref, d_ref, nm_ref, nv_ref = refs[3 + lay:]
        for l, g_ref in enumerate(g_refs):
            gg = g_ref[:, 0, :]
            nm = ADAM_B1 * m_ref[:, l, :] + (1.0 - ADAM_B1) * gg
            nv = ADAM_B2 * v_ref[:, l, :] + (1.0 - ADAM_B2) * (gg * gg)
            d_ref[:, l, :] = -ADAM_LR * ((nm * c1) / (jnp.sqrt(nv * c2) + ADAM_EPS) + ADAM_WD * w_ref[:, l, :])
            go_ref[:, l, :] = gg
            nm_ref[:, l, :] = nm
            nv_ref[:, l, :] = nv

    inner = pl.BlockSpec((tr, lay, cols), lambda i: (i, 0, 0))
    plain = pl.BlockSpec((tr, 1, cols), lambda i: (i, 0, 0))
    return pl.pallas_call(
        body, name=name, grid=(rows // tr,), in_specs=[inner] * 3 + [plain] * lay, out_specs=[inner] * 4,
        out_shape=[SDS((rows, lay, cols), F32)] * 4, compiler_params=_cparams(("parallel",)),
    )(w, m, v, *gs)


BIG = ("w_in", "conv_w", "w_ssd_branch", "w_attn_branch", "w_out", "w_gate_up", "w_down")
TRANSPOSED = ("w_in", "w_gate_up")
SMALL = ("norm_mix", "conv_b", "dt_bias", "a_log", "d_skip", "ssd_norm", "norm_ffn")
SMALL_SIZE = {"norm_mix": 1024, "conv_b": 3072, "dt_bias": 32, "a_log": 32, "d_skip": 32, "ssd_norm": 2048, "norm_ffn": 1024}
FLAT_W = 512
SMALL_TOTAL = DEPTH * sum(SMALL_SIZE.values()) + D_MODEL + LANES
SMALL_ROWS = 32
assert SMALL_ROWS * FLAT_W >= SMALL_TOTAL


GROUPS = (("w_in", "conv_w"), ("w_ssd_branch", "w_attn_branch", "w_out"), ("w_gate_up", "w_down"))


def to_wire(k, shard):
    if k in TRANSPOSED:
        return shard.T.astype(BF16)
    return shard if k == "conv_w" else shard.astype(BF16)


def full_weights(k, g):
    if k == "conv_w":
        return {k: g.transpose(1, 0, 2).reshape(SSD_CONV, SSD_CONV_CH)}
    full = g.reshape(-1, g.shape[-1])
    if k != "w_in":
        return {k: full}
    w, off = {}, 0
    for nm, r in IN_ROWS:
        w[nm] = full[off:off + r]
        off += r
    w["w_q"] = full[sum(r for _, r in IN_ROWS[:3]):sum(r for _, r in IN_ROWS[:6])]
    w["w_dt"] = jnp.pad(w["w_dt"], ((0, HPAD - SSD_HEADS), (0, 0)))
    return w


def grads_to_wire(k, g):
    if k == "conv_w":
        return g.reshape(SSD_CONV, N_DEV, SSD_CONV_CH // N_DEV).transpose(1, 0, 2)
    return g.reshape(N_DEV, g.shape[0] // N_DEV, g.shape[1])


def _pad_heads(t):
    return jnp.pad(t.reshape(1, SSD_HEADS), ((0, 0), (0, HPAD - SSD_HEADS)))


def local_step(x, target, getw, prefetch, emit, smalls, norm_final):
    tabs = rope_tables()
    sms = []
    for li in range(DEPTH):
        s = smalls[li]
        sms.append({
            "norm_mix": s["norm_mix"].reshape(1, -1), "conv_b": s["conv_b"].reshape(1, -1),
            "dt_bias": _pad_heads(s["dt_bias"]), "a_log": _pad_heads(s["a_log"]),
            "d_skip_x": jnp.repeat(s["d_skip"], SSD_HEAD_DIM).reshape(1, -1),
            "ssd_norm": s["ssd_norm"].reshape(1, -1), "norm_ffn": s["norm_ffn"].reshape(1, -1)})
    h = x
    saved = []
    for li in range(DEPTH):
        h, sv = layer_fwd(h, functools.partial(getw, li), functools.partial(prefetch, li), sms[li], tabs, li)
        saved.append(sv)
    dh, g_final, loss = loss_head(h, target, norm_final.reshape(1, -1), "loss_head")
    gsms = [None] * DEPTH
    for li in reversed(range(DEPTH)):
        dh, gsm = layer_bwd(dh, saved[li], sms[li], tabs, li, functools.partial(emit, li))
        gsms[li] = {
            "norm_mix": gsm["norm_mix"].reshape(-1), "conv_b": gsm["conv_b"].reshape(-1),
            "dt_bias": gsm["dt_bias"][0, :SSD_HEADS], "a_log": gsm["a_log"][0, :SSD_HEADS],
            "d_skip": gsm["d_skip_x"].reshape(SSD_HEADS, SSD_HEAD_DIM).sum(axis=1),
            "ssd_norm": gsm["ssd_norm"].reshape(-1), "norm_ffn": gsm["norm_ffn"].reshape(-1)}
    return loss, dh, gsms, g_final.reshape(-1)


def kernel(x, norm_mix, w_in, conv_w, conv_b, dt_bias, a_log, d_skip, ssd_norm, w_ssd_branch, w_attn_branch, w_out, norm_ffn, w_gate_up, w_down, norm_final, loss_target, m_norm_mix, m_w_in, m_conv_w, m_conv_b, m_dt_bias, m_a_log, m_d_skip, m_ssd_norm, m_w_ssd_branch, m_w_attn_branch, m_w_out, m_norm_ffn, m_w_gate_up, m_w_down, m_norm_final, v_norm_mix, v_w_in, v_conv_w, v_conv_b, v_dt_bias, v_a_log, v_d_skip, v_ssd_norm, v_w_ssd_branch, v_w_attn_branch, v_w_out, v_norm_ffn, v_w_gate_up, v_w_down, v_norm_final):
    wv = dict(norm_mix=norm_mix, w_in=w_in, conv_w=conv_w, conv_b=conv_b, dt_bias=dt_bias, a_log=a_log, d_skip=d_skip,
              ssd_norm=ssd_norm, w_ssd_branch=w_ssd_branch, w_attn_branch=w_attn_branch, w_out=w_out, norm_ffn=norm_ffn,
              w_gate_up=w_gate_up, w_down=w_down, norm_final=norm_final)
    mv = dict(norm_mix=m_norm_mix, w_in=m_w_in, conv_w=m_conv_w, conv_b=m_conv_b, dt_bias=m_dt_bias, a_log=m_a_log,
              d_skip=m_d_skip, ssd_norm=m_ssd_norm, w_ssd_branch=m_w_ssd_branch, w_attn_branch=m_w_attn_branch,
              w_out=m_w_out, norm_ffn=m_norm_ffn, w_gate_up=m_w_gate_up, w_down=m_w_down, norm_final=m_norm_final)
    vv = dict(norm_mix=v_norm_mix, w_in=v_w_in, conv_w=v_conv_w, conv_b=v_conv_b, dt_bias=v_dt_bias, a_log=v_a_log,
              d_skip=v_d_skip, ssd_norm=v_ssd_norm, w_ssd_branch=v_w_ssd_branch, w_attn_branch=v_w_attn_branch,
              w_out=v_w_out, norm_ffn=v_norm_ffn, w_gate_up=v_w_gate_up, w_down=v_w_down, norm_final=v_norm_final)
    order = ("norm_mix", "w_in", "conv_w", "conv_b", "dt_bias", "a_log", "d_skip", "ssd_norm", "w_ssd_branch",
             "w_attn_branch", "w_out", "norm_ffn", "w_gate_up", "w_down", "norm_final")

    smalls = [{k: wv[k][li] for k in SMALL} for li in range(DEPTH)]
    n_groups = len(GROUPS)

    first_lands = all_gather([to_wire(k, wv[k][0]) for k in GROUPS[0]], "gather_first")
    later = [(li, gi) for li in range(DEPTH) for gi in range(n_groups)][1:]
    behind_first = first_lands[1][0, 0, 0] * 0.0
    srcs = [to_wire(k, wv[k][li] + behind_first if k == "conv_w" else wv[k][li]) for li, gi in later for k in GROUPS[gi]]
    sizes = [len(GROUPS[gi]) for _, gi in later]
    w_sems, w_srcs, w_lands, token = exchange_start(srcs, [landing_zone(s) for s in srcs], sizes, False,
                                                    "gather_start", peers=SELF_AND_CHIPS)
    smalls[0]["norm_mix"] = smalls[0]["norm_mix"] + token[0, 0]
    second_leg = {}

    def forward(slot, after):
        if slot < len(later) and slot not in second_leg:
            sl = slice(sum(sizes[:slot]), sum(sizes[:slot + 1]))
            second_leg[slot] = gather_forward(w_srcs[sl], w_lands[sl], w_sems[slot], after, f"gather_forward_{slot}")

    def prefetch(li, gi, after):
        if (li, gi) == later[0]:
            forward(0, after)

    def getw(li, gi, after):
        if (li, gi) == (0, 0):
            lands = first_lands
        else:
            slot = later.index((li, gi))
            forward(slot, after)
            sems2, lands2 = second_leg[slot]
            lands = gather_finish(lands2, sems2, after, f"gather_finish_{li}_{gi}")
            forward(slot + 1, lands[0])
        w = {}
        for k, land in zip(GROUPS[gi], lands):
            w.update(full_weights(k, land))
        return w

    pending = []

    def emit(li, gi, gw):
        parts = [grads_to_wire(k, gw[k]) for k in GROUPS[gi]]
        lands = [landing_zone(p[0]) for p in parts]
        sems, p_thru, l_thru, tok = exchange_start(parts, lands, [len(parts)], True, f"grads_start_{li}_{gi}", peers=EVERYONE)
        pending.append((li, gi, sems[0], p_thru, l_thru))
        return tok[0, 0]

    loss_p, dx, gsms, g_final = local_step(x[0], loss_target[0], getw, prefetch, emit, smalls, norm_final)

    grads, deltas, new_m, new_v = {}, {}, {}, {}

    def update(k):
        if k == "w_in":
            inner = lambda t: t.transpose(2, 0, 1)
            outs = adamw_layer_inner(inner(wv[k]), shard_g[k], inner(mv[k]), inner(vv[k]), "adamw_" + k)
            grads[k], deltas[k], new_m[k], new_v[k] = (t.transpose(1, 2, 0) for t in outs)
            return outs[3]
        if k in BIG:
            grads[k] = jnp.stack([g.T if k in TRANSPOSED else g for g in shard_g[k]])
        deltas[k], new_m[k], new_v[k] = adamw(wv[k], grads[k], mv[k], vv[k], "adamw_" + k)
        return new_v[k]

    shard_g = {k: [None] * DEPTH for k in BIG}

    def collect(entry, after):
        li, gi, sems, p_thru, l_thru = entry
        recv = exchange_wait(p_thru, l_thru, sems, after, True, f"grads_wait_{li}_{gi}", peers=EVERYONE)
        for k, r in zip(GROUPS[gi], recv):
            if k == "conv_w":
                r = r.reshape(N_DEV, 1, -1)
            after = sum_parts(r, f"sum_{k}_{li}", row_major_3d=(k == "w_in"))
            shard_g[k][li] = after if k in TRANSPOSED else after.reshape(wv[k].shape[1:])
        return after

    after = dx
    for entry in pending[:-1]:
        after = collect(entry, after)
    done = [after[:1, :1].reshape(1)]
    for gi in (2, 1):
        for k in GROUPS[gi]:
            done.append(update(k).reshape(-1)[:1])

    flat = [gsms[li][k] for li in range(DEPTH) for k in SMALL] + [g_final, loss_p.reshape(-1)]
    flat.append(jnp.zeros((SMALL_ROWS * FLAT_W - SMALL_TOTAL,), F32))
    small_all = all_gather([jnp.concatenate(flat).reshape(SMALL_ROWS, FLAT_W)], "gather_small")[0]
    small_sum = sum_parts(small_all, "sum_small").reshape(-1)
    off = 0
    per_layer = {k: [] for k in SMALL}
    for li in range(DEPTH):
        for k in SMALL:
            per_layer[k].append(small_sum[off:off + SMALL_SIZE[k]])
            off += SMALL_SIZE[k]
    for k in SMALL:
        grads[k] = jnp.stack(per_layer[k])
    grads["norm_final"] = small_sum[off:off + D_MODEL]
    loss = small_sum[off + D_MODEL]
    for k in (*SMALL, "norm_final"):
        done.append(update(k).reshape(-1)[:1])

    collect(pending[-1], jnp.concatenate(done))
    for k in GROUPS[0]:
        update(k)

    return (loss, dx.reshape(x.shape), *[grads[k] for k in order], *[deltas[k] for k in order],
            *[new_m[k] for k in order], *[new_v[k] for k in order])
```

```python
import functools

import jax
import jax.numpy as jnp
from jax import lax
from jax.experimental import pallas as pl
from jax.experimental.pallas import tpu as pltpu

F32, BF16 = jnp.float32, jnp.bfloat16
SDS = jax.ShapeDtypeStruct
MESH = pl.DeviceIdType.MESH

D_MODEL = 1024
SEQ = 2048
DEPTH = 2
RMS_EPS = 1e-5
SSD_INNER = 2048
SSD_HEAD_DIM = 64
SSD_HEADS = 32
SSD_STATE = 128
SSD_GROUPS = 4
SSD_CONV = 4
SSD_CHUNK = 128
SSD_CONV_CH = 3072
ATTN_HEAD_DIM = 128
ATTN_KV_HEADS = 8
ATTN_DILATIONS = (1, 4, 16)
ATTN_N_PAT = 3
ATTN_BLOCK = 128
ATTN_OUT = 1024
ROPE_THETA = 500000.0
ROPE_DIM = 32
FFN_HIDDEN = 2816
ADAM_LR, ADAM_B1, ADAM_B2, ADAM_EPS, ADAM_WD, ADAM_STEP = 0.001, 0.9, 0.999, 1e-08, 0.01, 10

N_DEV = 8
LANES = 128
VMEM_LIMIT = 56 * 1024 * 1024
HPAD = 128
HIGHEST = lax.Precision.HIGHEST

IN_ROWS = (("w_z", 2048), ("w_xbc", 3072), ("w_dt", 32), ("w_q0", 1024), ("w_q1", 1024), ("w_q2", 1024),
           ("w_k", 1024), ("w_v", 1024), ("w_gs", 1024), ("w_ga", 1024))
N_IN = sum(r for _, r in IN_ROWS)


def _cparams(sem):
    return pltpu.CompilerParams(dimension_semantics=sem, vmem_limit_bytes=VMEM_LIMIT)


def _sigmoid(x):
    return 0.5 * jnp.tanh(0.5 * x) + 0.5


def _silu(x):
    return x * _sigmoid(x)


def _softplus(x):
    return jnp.maximum(x, 0.0) + jnp.log(1.0 + jnp.exp(-jnp.abs(x)))


def _dot(a, b, dims=(((1,), (0,)), ((), ())), precision=None):
    return lax.dot_general(a, b, dims, precision=precision, preferred_element_type=F32)


NT = (((1,), (1,)), ((), ()))
TN = (((0,), (0,)), ((), ()))


def _bdot(a, b, dims=(((1,), (0,)), ((), ()))):
    return _dot(a.astype(BF16), b.astype(BF16), dims)


def _pick(dim, cands):
    for c in cands:
        if dim % c == 0:
            return c
    return dim


WHOLE_K_BUDGET = 40 * 1024 * 1024


def matmul(a, b, *, name, ta=False, tb=False, out_dtype=F32, add=None):
    m, k = (a.shape[1], a.shape[0]) if ta else a.shape
    n = b.shape[0] if tb else b.shape[1]
    tn = _pick(n, (1024, 1408, 512, 256, 128))
    tm = _pick(m, (512, 1408, 256, 128)) if tn == n else _pick(m, (1024, 1408, 512, 256, 128))
    tk = _pick(k, (2048, 1024, 1408, 512, 256, 128))
    whole_k_bytes = 2 * (tm * k * a.dtype.itemsize + k * tn * b.dtype.itemsize)
    if tn == n and whole_k_bytes <= WHOLE_K_BUDGET:
        tk = k
    nk = k // tk
    a_spec = pl.BlockSpec((tk, tm), lambda i, j, kk: (kk, i)) if ta else pl.BlockSpec((tm, tk), lambda i, j, kk: (i, kk))
    b_spec = pl.BlockSpec((tn, tk), lambda i, j, kk: (j, kk)) if tb else pl.BlockSpec((tk, tn), lambda i, j, kk: (kk, j))
    dims = (((0 if ta else 1,), (1 if tb else 0,)), ((), ()))
    has_add = add is not None

    def body(*refs):
        a_ref, b_ref = refs[:2]
        add_ref = refs[2] if has_add else None
        o_ref = refs[3] if has_add else refs[2]
        acc = refs[-1] if nk > 1 else None
        kk = pl.program_id(2)

        def product():
            return _dot(a_ref[...].astype(BF16), b_ref[...].astype(BF16), dims)

        def finish(r):
            if has_add:
                r = r + add_ref[...].astype(F32)
            o_ref[...] = r.astype(o_ref.dtype)

        if nk == 1:
            finish(product())
            return

        @pl.when(kk == 0)
        def _():
            acc[...] = product()

        @pl.when((kk > 0) & (kk < nk - 1))
        def _():
            acc[...] += product()

        @pl.when(kk == nk - 1)
        def _():
            finish(acc[...] + product())

    in_specs = [a_spec, b_spec]
    args = [a, b]
    if has_add:
        in_specs.append(pl.BlockSpec((tm, tn), lambda i, j, kk: (i, j)))
        args.append(add)
    return pl.pallas_call(
        body, name=name, grid=(m // tm, n // tn, nk),
        in_specs=in_specs, out_specs=pl.BlockSpec((tm, tn), lambda i, j, kk: (i, j)),
        out_shape=SDS((m, n), out_dtype), scratch_shapes=[pltpu.VMEM((tm, tn), F32)] if nk > 1 else [],
        compiler_params=_cparams(("parallel", "parallel", "arbitrary")),
    )(*args)


def rowcall(name, fn, rows, params, row_outs, red_outs=(), tr=256):
    s = rows[0].shape[0]
    n_in = len(rows) + len(params)
    n_row = len(row_outs)

    def body(*refs):
        outs = fn(*[r[...].astype(F32) for r in refs[:n_in]])
        if not isinstance(outs, (tuple, list)):
            outs = (outs,)
        orefs = refs[n_in:]
        for r, o in zip(orefs[:n_row], outs[:n_row]):
            r[...] = o.astype(r.dtype)
        if red_outs:
            @pl.when(pl.program_id(0) == 0)
            def _():
                for r in orefs[n_row:]:
                    r[...] = jnp.zeros_like(r)
            for r, o in zip(orefs[n_row:], outs[n_row:]):
                r[...] += o.astype(F32)

    widths = [a[1] if isinstance(a, tuple) else a.shape[1] for a in rows]
    rows = [a[0] if isinstance(a, tuple) else a for a in rows]
    in_specs = [pl.BlockSpec((tr, wd), lambda i: (i, 0)) for wd in widths]
    in_specs += [pl.BlockSpec(p.shape, lambda i: (0, 0)) for p in params]
    out_specs = [pl.BlockSpec((tr, c), lambda i: (i, 0)) for c, _ in row_outs]
    out_specs += [pl.BlockSpec(shp, lambda i: (0, 0)) for shp in red_outs]
    out_shape = [SDS((s, c), dt) for c, dt in row_outs] + [SDS(shp, F32) for shp in red_outs]
    res = pl.pallas_call(
        body, name=name, grid=(s // tr,), in_specs=in_specs, out_specs=out_specs, out_shape=out_shape,
        compiler_params=_cparams(("arbitrary",) if red_outs else ("parallel",)),
    )(*rows, *params)
    return res


def _rms(x, w):
    return x * lax.rsqrt(jnp.mean(x * x, axis=-1, keepdims=True) + RMS_EPS) * w


def rms_fwd(h, w, name):
    return rowcall(name, _rms, [h], [w], [(D_MODEL, BF16)])[0]


def rms_bwd(h, du, dres, w, name):
    def fn(hb, dub, dresb, wb):
        _, vjp = jax.vjp(_rms, hb, wb)
        dh, dw = vjp(dub)
        return dh + dresb, dw
    return rowcall(name, fn, [h, du, dres], [w], [(D_MODEL, F32)], [(1, D_MODEL)])


def loss_head(h, target, w, name):
    def fn(hb, tb, wb):
        def f(hh, ww):
            err = _rms(hh, ww) - tb
            return 0.5 * jnp.sum(jnp.mean(err * err, axis=-1, keepdims=True), axis=0, keepdims=True)
        val, vjp = jax.vjp(f, hb, wb)
        dh, dw = vjp(jnp.ones((1, 1), F32))
        return dh, dw, jnp.broadcast_to(val, (1, LANES))
    return rowcall(name, fn, [h, target], [w], [(D_MODEL, F32)], [(1, D_MODEL), (1, LANES)])


def _gate(a, b, gs, ga):
    return _sigmoid(gs) * a + _sigmoid(ga) * b


def gate_fwd(a, b, gs, ga, name):
    return rowcall(name, _gate, [a, b, gs, ga], [], [(D_MODEL, BF16)])[0]


def gate_bwd(a, b, gs, ga, dm, name):
    def fn(ab, bb, gsb, gab, dmb):
        _, vjp = jax.vjp(_gate, ab, bb, gsb, gab)
        return vjp(dmb)
    return rowcall(name, fn, [a, b, gs, ga, dm], [], [(D_MODEL, BF16)] * 4)


def _swiglu(gu):
    return _silu(gu[:, :FFN_HIDDEN]) * gu[:, FFN_HIDDEN:]


def swiglu_fwd(gu, name):
    return rowcall(name, _swiglu, [gu], [], [(FFN_HIDDEN, BF16)])[0]


def swiglu_bwd(gu, dact, name):
    def fn(gub, db):
        _, vjp = jax.vjp(_swiglu, gub)
        return vjp(db.astype(F32))[0]
    return rowcall(name, fn, [gu, dact], [], [(2 * FFN_HIDDEN, BF16)])[0]


def _ssd_post(y, xs, z, dskip, normw):
    y = (y + dskip * xs) * _silu(z)
    gw = SSD_INNER // SSD_GROUPS
    parts = []
    for g in range(SSD_GROUPS):
        yg = y[:, g * gw:(g + 1) * gw]
        parts.append(yg * lax.rsqrt(jnp.mean(yg * yg, axis=-1, keepdims=True) + RMS_EPS))
    return jnp.concatenate(parts, axis=-1) * normw


def ssd_post_fwd(y, xc, z, dskip, normw, name):
    return rowcall(name, _ssd_post, [y, (xc, SSD_INNER), z], [dskip, normw], [(SSD_INNER, BF16)])[0]


def ssd_post_bwd(y, xc, z, dskip, normw, dyn, name):
    def fn(yb, xsb, zb, dynb, db, nb):
        _, vjp = jax.vjp(_ssd_post, yb, xsb, zb, db, nb)
        return vjp(dynb)
    return rowcall(name, fn, [y, (xc, SSD_INNER), z, dyn], [dskip, normw],
                   [(SSD_INNER, BF16)] * 3, [(1, SSD_INNER), (1, SSD_INNER)])


def _rope(t, cosf, sina, sinb):
    return t * cosf + pltpu.roll(t, LANES - ROPE_DIM // 2, 1) * sina + pltpu.roll(t, ROPE_DIM // 2, 1) * sinb


def rope_tables():
    half = ROPE_DIM // 2
    inv = ROPE_THETA ** (-jnp.arange(0, ROPE_DIM, 2, dtype=F32) / ROPE_DIM)
    ang = jnp.arange(SEQ, dtype=F32)[:, None] * inv[None, :]
    cos, sin = jnp.cos(ang), jnp.sin(ang)
    zeros = jnp.zeros((SEQ, LANES - ROPE_DIM), F32)
    z16 = jnp.zeros((SEQ, half), F32)
    cosf = jnp.concatenate([cos, cos, jnp.ones((SEQ, LANES - ROPE_DIM), F32)], axis=1)
    sina = jnp.concatenate([-sin, z16, zeros], axis=1)
    sinb = jnp.concatenate([z16, sin, zeros], axis=1)
    return cosf, sina, sinb


CONV_TC = 256


def _conv_pre(x, w, b, row):
    acc = x * w[SSD_CONV - 1:SSD_CONV, :] + b
    shifted = [x]
    for j in range(1, SSD_CONV):
        xs = jnp.where(row >= j, pltpu.roll(x, j, 0), 0.0)
        shifted.append(xs)
        acc = acc + xs * w[SSD_CONV - 1 - j:SSD_CONV - j, :]
    return acc, shifted


def conv_fwd(xbc, w, b, name):
    def body(x_ref, w_ref, b_ref, o_ref):
        row = lax.broadcasted_iota(jnp.int32, (SEQ, CONV_TC), 0)
        pre, _ = _conv_pre(x_ref[...].astype(F32), w_ref[...], b_ref[...], row)
        o_ref[...] = _silu(pre).astype(o_ref.dtype)
    return pl.pallas_call(
        body, name=name, grid=(SSD_CONV_CH // CONV_TC,),
        in_specs=[pl.BlockSpec((SEQ, CONV_TC), lambda i: (0, i)), pl.BlockSpec((SSD_CONV, CONV_TC), lambda i: (0, i)),
                  pl.BlockSpec((1, CONV_TC), lambda i: (0, i))],
        out_specs=pl.BlockSpec((SEQ, CONV_TC), lambda i: (0, i)),
        out_shape=SDS((SEQ, SSD_CONV_CH), BF16), compiler_params=_cparams(("parallel",)),
    )(xbc, w, b)


def conv_bwd(xbc, w, b, dxc, name):
    def body(x_ref, w_ref, b_ref, dy_ref, dx_ref, dw_ref, db_ref):
        row = lax.broadcasted_iota(jnp.int32, (SEQ, CONV_TC), 0)
        wv = w_ref[...]
        pre, shifted = _conv_pre(x_ref[...].astype(F32), wv, b_ref[...], row)
        sg = _sigmoid(pre)
        ds = dy_ref[...].astype(F32) * (sg * (1.0 + pre * (1.0 - sg)))
        dx = ds * wv[SSD_CONV - 1:SSD_CONV, :]
        for j in range(1, SSD_CONV):
            dsj = jnp.where(row < SEQ - j, pltpu.roll(ds, SEQ - j, 0), 0.0)
            dx = dx + dsj * wv[SSD_CONV - 1 - j:SSD_CONV - j, :]
        dx_ref[...] = dx.astype(dx_ref.dtype)
        for j in range(SSD_CONV):
            dw_ref[SSD_CONV - 1 - j:SSD_CONV - j, :] = jnp.sum(ds * shifted[j], axis=0, keepdims=True)
        db_ref[...] = jnp.sum(ds, axis=0, keepdims=True)
    return pl.pallas_call(
        body, name=name, grid=(SSD_CONV_CH // CONV_TC,),
        in_specs=[pl.BlockSpec((SEQ, CONV_TC), lambda i: (0, i)), pl.BlockSpec((SSD_CONV, CONV_TC), lambda i: (0, i)),
                  pl.BlockSpec((1, CONV_TC), lambda i: (0, i)), pl.BlockSpec((SEQ, CONV_TC), lambda i: (0, i))],
        out_specs=[pl.BlockSpec((SEQ, CONV_TC), lambda i: (0, i)), pl.BlockSpec((SSD_CONV, CONV_TC), lambda i: (0, i)),
                   pl.BlockSpec((1, CONV_TC), lambda i: (0, i))],
        out_shape=[SDS((SEQ, SSD_CONV_CH), BF16), SDS((SSD_CONV, SSD_CONV_CH), F32), SDS((1, SSD_CONV_CH), F32)],
        compiler_params=_cparams(("parallel",)),
    )(xbc, w, b, dxc)


N_CHUNKS = SEQ // SSD_CHUNK
N_PAIRS = SSD_HEADS // 2
PAIRS_PER_GROUP = N_PAIRS // SSD_GROUPS
B_OFF = SSD_INNER
C_OFF = SSD_INNER + SSD_GROUPS * SSD_STATE


def _ssd_prefix(dtr, dtr_t, dtb, dtb_t, alog, alog_t):
    ln = SSD_CHUNK
    dt = _softplus(dtr + dtb)
    dt_t = _softplus(dtr_t + dtb_t)
    dta = dt * (-jnp.exp(alog))
    dta_t = dt_t * (-jnp.exp(alog_t))
    r = lax.broadcasted_iota(jnp.int32, (ln, ln), 0)
    c = lax.broadcasted_iota(jnp.int32, (ln, ln), 1)
    a_cum = _dot((r >= c).astype(F32), dta, precision=HIGHEST)
    a_cum_t = _dot(dta_t, (r <= c).astype(F32), precision=HIGHEST)
    a_last = jnp.sum(dta_t, axis=1, keepdims=True)
    return dt, a_cum, a_cum_t, a_last


def _bein(spec, a, b):
    return jnp.einsum(spec, a.astype(BF16), b.astype(BF16), preferred_element_type=F32)


SSD_GROUPS_PER_BATCH = 4


def _ssd_group(xs3, bgs, cgs, h3, dt, a_cum, a_cum_t, a_last, *, groups):
    ln = SSD_CHUNK
    lane = lax.broadcasted_iota(jnp.int32, (ln, LANES), 1)
    sub = lax.broadcasted_iota(jnp.int32, (LANES, SSD_STATE), 0)
    row = lax.broadcasted_iota(jnp.int32, (ln, ln), 0)
    col = lax.broadcasted_iota(jnp.int32, (ln, ln), 1)
    lo = lane < SSD_HEAD_DIM
    causal = row >= col
    m_lo, m_hi, dts, acs, lasts, cds, cg3, bg3 = [], [], [], [], [], [], [], []
    for g, bg, cg in zip(groups, bgs, cgs):
        cb = _bdot(cg, bg, NT)
        for j in range(PAIRS_PER_GROUP):
            e0 = 2 * (g * PAIRS_PER_GROUP + j)
            e1 = e0 + 1
            c0, c1 = a_cum[:, e0:e0 + 1], a_cum[:, e1:e1 + 1]
            r0, r1 = a_cum_t[e0:e0 + 1, :], a_cum_t[e1:e1 + 1, :]
            l0, l1 = a_last[e0:e0 + 1, :], a_last[e1:e1 + 1, :]
            m_lo.append(cb * jnp.exp(jnp.where(causal, c0 - r0, -jnp.inf)))
            m_hi.append(cb * jnp.exp(jnp.where(causal, c1 - r1, -jnp.inf)))
            dts.append(jnp.where(lo, dt[:, e0:e0 + 1], dt[:, e1:e1 + 1]))
            acs.append(jnp.where(lo, c0, c1))
            lasts.append(jnp.where(lo, l0, l1))
            cds.append(jnp.exp(jnp.where(sub < SSD_HEAD_DIM, l0, l1)))
            cg3.append(cg)
            bg3.append(bg)
    xd = xs3 * jnp.stack(dts)
    acum = jnp.stack(acs)
    y = (_bein("pls,psq->plq", jnp.stack(m_lo), jnp.where(lo[None], xd, 0.0))
         + _bein("pls,psq->plq", jnp.stack(m_hi), jnp.where(lo[None], 0.0, xd)))
    y = y + _bein("pln,pqn->plq", jnp.stack(cg3), h3) * jnp.exp(acum)
    st = _bein("plq,pln->pqn", xd * jnp.exp(jnp.stack(lasts) - acum), jnp.stack(bg3))
    h_out = h3 * jnp.stack(cds) + st
    return y, h_out


def _group_slabs(groups):
    pairs = [g * PAIRS_PER_GROUP + j for g in groups for j in range(PAIRS_PER_GROUP)]
    return [slice(p * LANES, (p + 1) * LANES) for p in pairs]


def _group_batches():
    return [tuple(range(g, g + SSD_GROUPS_PER_BATCH)) for g in range(0, SSD_GROUPS, SSD_GROUPS_PER_BATCH)]


def _bc_of(xc_ref, g):
    return (xc_ref[:, B_OFF + g * SSD_STATE:B_OFF + (g + 1) * SSD_STATE].astype(F32),
            xc_ref[:, C_OFF + g * SSD_STATE:C_OFF + (g + 1) * SSD_STATE].astype(F32))


def _ssd_in_specs(chunk_of):
    return [
        pl.BlockSpec((SSD_CHUNK, SSD_CONV_CH), lambda i: (chunk_of(i), 0)),
        pl.BlockSpec((SSD_CHUNK, HPAD), lambda i: (chunk_of(i), 0)),
        pl.BlockSpec((HPAD, SSD_CHUNK), lambda i: (0, chunk_of(i))),
        pl.BlockSpec((1, HPAD), lambda i: (0, 0)), pl.BlockSpec((HPAD, 1), lambda i: (0, 0)),
        pl.BlockSpec((1, HPAD), lambda i: (0, 0)), pl.BlockSpec((HPAD, 1), lambda i: (0, 0)),
    ]


def ssd_fwd(xc, dtr, dtr_t, dtb, dtb_t, alog, alog_t, name):
    def body(xc_ref, dtr_ref, dtrt_ref, dtb_ref, dtbt_ref, al_ref, alt_ref, y_ref, hs_ref, h_scr):
        @pl.when(pl.program_id(0) == 0)
        def _():
            h_scr[...] = jnp.zeros_like(h_scr)

        hs_ref[0] = h_scr[...]
        dt, a_cum, a_cum_t, a_last = _ssd_prefix(dtr_ref[...], dtrt_ref[...], dtb_ref[...], dtbt_ref[...],
                                                  al_ref[...], alt_ref[...])
        for groups in _group_batches():
            slabs = _group_slabs(groups)
            bgs, cgs = zip(*[_bc_of(xc_ref, g) for g in groups])
            xs3 = jnp.stack([xc_ref[:, sl] for sl in slabs]).astype(F32)
            h3 = jnp.stack([h_scr[sl, :] for sl in slabs])
            y3, h3_out = _ssd_group(xs3, bgs, cgs, h3, dt, a_cum, a_cum_t, a_last, groups=groups)
            for j, sl in enumerate(slabs):
                y_ref[:, sl] = y3[j].astype(y_ref.dtype)
                h_scr[sl, :] = h3_out[j]

    return pl.pallas_call(
        body, name=name, grid=(N_CHUNKS,), in_specs=_ssd_in_specs(lambda i: i),
        out_specs=[pl.BlockSpec((SSD_CHUNK, SSD_INNER), lambda i: (i, 0)),
                   pl.BlockSpec((1, SSD_INNER, SSD_STATE), lambda i: (i, 0, 0))],
        out_shape=[SDS((SEQ, SSD_INNER), BF16), SDS((N_CHUNKS, SSD_INNER, SSD_STATE), F32)],
        scratch_shapes=[pltpu.VMEM((SSD_INNER, SSD_STATE), F32)],
        compiler_params=_cparams(("arbitrary",)),
    )(xc, dtr, dtr_t, dtb, dtb_t, alog, alog_t)


def ssd_bwd(xc, dtr, dtr_t, dtb, dtb_t, alog, alog_t, hs, dy, dxs_extra, name):
    rev = lambda i: N_CHUNKS - 1 - i

    def body(xc_ref, dtr_ref, dtrt_ref, dtb_ref, dtbt_ref, al_ref, alt_ref, hs_ref, dy_ref, dxe_ref,
             dxc_ref, ddtr_ref, ddtrt_ref, ddtb_ref, ddtbt_ref, dal_ref, dalt_ref, dh_scr):
        @pl.when(pl.program_id(0) == 0)
        def _():
            dh_scr[...] = jnp.zeros_like(dh_scr)
            for r in (ddtb_ref, ddtbt_ref, dal_ref, dalt_ref):
                r[...] = jnp.zeros_like(r)

        prefix_in = (dtr_ref[...], dtrt_ref[...], dtb_ref[...], dtbt_ref[...], al_ref[...], alt_ref[...])
        (dt, a_cum, a_cum_t, a_last), prefix_vjp = jax.vjp(_ssd_prefix, *prefix_in)
        d_dt = jnp.zeros_like(dt)
        d_acum = jnp.zeros_like(a_cum)
        d_acum_t = jnp.zeros_like(a_cum_t)
        d_alast = jnp.zeros_like(a_last)
        for groups in _group_batches():
            slabs = _group_slabs(groups)
            bgs, cgs = zip(*[_bc_of(xc_ref, g) for g in groups])
            xs3 = jnp.stack([xc_ref[:, sl] for sl in slabs]).astype(F32)
            h3 = jnp.stack([hs_ref[0, sl, :] for sl in slabs])
            _, vjp = jax.vjp(functools.partial(_ssd_group, groups=groups), xs3, bgs, cgs, h3, dt, a_cum, a_cum_t, a_last)
            dy3 = jnp.stack([dy_ref[:, sl] for sl in slabs]).astype(F32)
            dh3 = jnp.stack([dh_scr[sl, :] for sl in slabs])
            dxs3, d_bgs, d_cgs, dh3_in, ddt, dac, dact, dal = vjp((dy3, dh3))
            for j, sl in enumerate(slabs):
                dxc_ref[:, sl] = (dxs3[j] + dxe_ref[:, sl].astype(F32)).astype(dxc_ref.dtype)
                dh_scr[sl, :] = dh3_in[j]
            d_dt, d_acum, d_acum_t, d_alast = d_dt + ddt, d_acum + dac, d_acum_t + dact, d_alast + dal
            for g, d_bg, d_cg in zip(groups, d_bgs, d_cgs):
                dxc_ref[:, B_OFF + g * SSD_STATE:B_OFF + (g + 1) * SSD_STATE] = d_bg.astype(dxc_ref.dtype)
                dxc_ref[:, C_OFF + g * SSD_STATE:C_OFF + (g + 1) * SSD_STATE] = d_cg.astype(dxc_ref.dtype)
        g_dtr, g_dtrt, g_dtb, g_dtbt, g_al, g_alt = prefix_vjp((d_dt, d_acum, d_acum_t, d_alast))
        ddtr_ref[...] = g_dtr
        ddtrt_ref[...] = g_dtrt
        ddtb_ref[...] += g_dtb
        ddtbt_ref[...] += g_dtbt
        dal_ref[...] += g_al
        dalt_ref[...] += g_alt

    in_specs = _ssd_in_specs(rev) + [
        pl.BlockSpec((1, SSD_INNER, SSD_STATE), lambda i: (rev(i), 0, 0)),
        pl.BlockSpec((SSD_CHUNK, SSD_INNER), lambda i: (rev(i), 0)),
        pl.BlockSpec((SSD_CHUNK, SSD_INNER), lambda i: (rev(i), 0)),
    ]
    out_specs = [
        pl.BlockSpec((SSD_CHUNK, SSD_CONV_CH), lambda i: (rev(i), 0)),
        pl.BlockSpec((SSD_CHUNK, HPAD), lambda i: (rev(i), 0)),
        pl.BlockSpec((HPAD, SSD_CHUNK), lambda i: (0, rev(i))),
        pl.BlockSpec((1, HPAD), lambda i: (0, 0)), pl.BlockSpec((HPAD, 1), lambda i: (0, 0)),
        pl.BlockSpec((1, HPAD), lambda i: (0, 0)), pl.BlockSpec((HPAD, 1), lambda i: (0, 0)),
    ]
    out_shape = [SDS((SEQ, SSD_CONV_CH), BF16), SDS((SEQ, HPAD), F32), SDS((HPAD, SEQ), F32),
                 SDS((1, HPAD), F32), SDS((HPAD, 1), F32), SDS((1, HPAD), F32), SDS((HPAD, 1), F32)]
    return pl.pallas_call(
        body, name=name, grid=(N_CHUNKS,), in_specs=in_specs, out_specs=out_specs, out_shape=out_shape,
        scratch_shapes=[pltpu.VMEM((SSD_INNER, SSD_STATE), F32)],
        compiler_params=_cparams(("arbitrary",)),
    )(xc, dtr, dtr_t, dtb, dtb_t, alog, alog_t, hs, dy, dxs_extra)


ATTN_SCALE = ATTN_HEAD_DIM ** -0.5


UNITS_PER_PATTERN = SEQ // ATTN_BLOCK
ATTN_BATCH_FWD = 8
ATTN_BATCH_BWD = 16


def _for_unit_batches(batch, per_trip):
    for g, d in enumerate(ATTN_DILATIONS):
        nb = UNITS_PER_PATTERN // d
        span = d * ATTN_BLOCK

        def trip(t, carry, g=g, d=d, nb=nb, span=span):
            units = []
            for j in range(per_trip):
                i = t * per_trip + j
                r = i >> (nb.bit_length() - 1)
                n = i & (nb - 1)
                start = r + n * span
                prev = jnp.where(n > 0, start - span, start)
                units.append((pl.ds(start, ATTN_BLOCK, stride=d), pl.ds(prev, ATTN_BLOCK, stride=d), n > 0))
            batch(g, units)
            return carry
        lax.fori_loop(0, UNITS_PER_PATTERN // per_trip, trip, 0)


def _unit_operands(units, q_scr, k_scr, v_scr):
    def pair(scr, rows, prows):
        return jnp.concatenate([scr[prows, :], scr[rows, :]], axis=0)
    qb = jnp.stack([q_scr[rows, :] for rows, _, _ in units]).astype(BF16)
    kb = jnp.stack([pair(k_scr, rows, prows) for rows, prows, _ in units]).astype(BF16)
    vb = jnp.stack([pair(v_scr, rows, prows) for rows, prows, _ in units]).astype(BF16)
    return qb, kb, vb


def _unit_scores(qb, kb, units):
    s = jnp.einsum("bqd,bkd->bqk", qb, kb, preferred_element_type=F32) * ATTN_SCALE
    qi = lax.broadcasted_iota(jnp.int32, (ATTN_BLOCK, 2 * ATTN_BLOCK), 0)
    kj = lax.broadcasted_iota(jnp.int32, (ATTN_BLOCK, 2 * ATTN_BLOCK), 1)
    own = (kj >= ATTN_BLOCK) & (kj - ATTN_BLOCK <= qi)
    before = (kj < ATTN_BLOCK) & (kj >= qi)
    keep = jnp.stack([own | (before & has_prev) for _, _, has_prev in units])
    return jnp.where(keep, s, -jnp.inf)


def _head_specs(n_q_groups):
    blk = (SEQ, ATTN_HEAD_DIM)
    q_specs = [pl.BlockSpec(blk, functools.partial(lambda h, g: (0, g * ATTN_KV_HEADS + h), g=g)) for g in range(n_q_groups)]
    head = pl.BlockSpec(blk, lambda h: (0, h))
    table = pl.BlockSpec(blk, lambda h: (0, 0))
    return q_specs, head, table


def attn_fwd(q, k, v, tabs, name):
    q_specs, head, table = _head_specs(ATTN_N_PAT)

    def body(q0_ref, q1_ref, q2_ref, k_ref, v_ref, c_ref, sa_ref, sb_ref, y_ref, lse_ref, *scr):
        qs, og, ls, ks, vs = scr[0:3], scr[3:6], scr[6:9], scr[9], scr[10]
        c, sa, sb = c_ref[...], sa_ref[...], sb_ref[...]
        for g, q_ref in enumerate((q0_ref, q1_ref, q2_ref)):
            qs[g][...] = _rope(q_ref[...].astype(F32), c, sa, sb)
        ks[...] = _rope(k_ref[...].astype(F32), c, sa, sb)
        vs[...] = v_ref[...].astype(F32)

        def batch(g, units):
            qb, kb, vb = _unit_operands(units, qs[g], ks, vs)
            s = _unit_scores(qb, kb, units)
            m = jnp.max(s, axis=2, keepdims=True)
            p = jnp.exp(s - m)
            l = jnp.sum(p, axis=2, keepdims=True)
            o = jnp.einsum("bqk,bkd->bqd", p.astype(BF16), vb, preferred_element_type=F32) / l
            lse_b = m + jnp.log(l)
            for j, (rows, _, _) in enumerate(units):
                og[g][rows, :] = o[j]
                ls[g][rows, :] = jnp.broadcast_to(lse_b[j], (ATTN_BLOCK, LANES))

        _for_unit_batches(batch, ATTN_BATCH_FWD)
        l0, l1, l2 = ls[0][...], ls[1][...], ls[2][...]
        m = jnp.maximum(jnp.maximum(l0, l1), l2)
        e0, e1, e2 = jnp.exp(l0 - m), jnp.exp(l1 - m), jnp.exp(l2 - m)
        den = e0 + e1 + e2
        y_ref[...] = ((e0 * og[0][...] + e1 * og[1][...] + e2 * og[2][...]) / den).astype(y_ref.dtype)
        lse_ref[...] = m + jnp.log(den)

    blk = (SEQ, ATTN_HEAD_DIM)
    return pl.pallas_call(
        body, name=name, grid=(ATTN_KV_HEADS,), in_specs=[*q_specs, head, head, table, table, table],
        out_specs=[head, head], out_shape=[SDS((SEQ, ATTN_OUT), BF16), SDS((SEQ, ATTN_OUT), F32)],
        scratch_shapes=[pltpu.VMEM(blk, F32)] * (3 * ATTN_N_PAT + 2),
        compiler_params=_cparams(("parallel",)),
    )(q, q, q, k, v, *tabs)


def attn_bwd(q, k, v, tabs, y, lse, dy, name):
    q_specs, head, table = _head_specs(ATTN_N_PAT)

    def body(q0_ref, q1_ref, q2_ref, k_ref, v_ref, c_ref, sa_ref, sb_ref, y_ref, lse_ref, dy_ref,
             dq0_ref, dq1_ref, dq2_ref, dk_ref, dv_ref, *scr):
        qs, dqs, ks, dks, dd, dvs, vs = scr[0:3], scr[3:6], scr[6], scr[7], scr[8], scr[9], scr[10]
        c, sa, sb = c_ref[...], sa_ref[...], sb_ref[...]
        for g, q_ref in enumerate((q0_ref, q1_ref, q2_ref)):
            qs[g][...] = _rope(q_ref[...].astype(F32), c, sa, sb)
        ks[...] = _rope(k_ref[...].astype(F32), c, sa, sb)
        vs[...] = v_ref[...].astype(F32)
        dks[...] = jnp.zeros_like(dks)
        dvs[...] = jnp.zeros_like(dvs)
        dyv = dy_ref[...]
        dd[...] = jnp.broadcast_to(jnp.sum(dyv * y_ref[...].astype(F32), axis=1, keepdims=True), dd.shape)

        def batch(g, units):
            qb, kb, vb = _unit_operands(units, qs[g], ks, vs)
            dob = jnp.stack([dy_ref[rows, :] for rows, _, _ in units]).astype(BF16)
            lse_b = jnp.stack([lse_ref[rows, :][:, 0:1] for rows, _, _ in units])
            dsum_b = jnp.stack([dd[rows, :][:, 0:1] for rows, _, _ in units])
            p = jnp.exp(_unit_scores(qb, kb, units) - lse_b)
            dp = jnp.einsum("bqd,bkd->bqk", dob, vb, preferred_element_type=F32)
            ds = (p * (dp - dsum_b) * ATTN_SCALE).astype(BF16)
            dq = jnp.einsum("bqk,bkd->bqd", ds, kb, preferred_element_type=F32)
            dk = jnp.einsum("bqk,bqd->bkd", ds, qb, preferred_element_type=F32)
            dv = jnp.einsum("bqk,bqd->bkd", p.astype(BF16), dob, preferred_element_type=F32)
            for j, (rows, prows, _) in enumerate(units):
                dqs[g][rows, :] = dq[j]
                dks[prows, :] += dk[j, :ATTN_BLOCK]
                dks[rows, :] += dk[j, ATTN_BLOCK:]
                dvs[prows, :] += dv[j, :ATTN_BLOCK]
                dvs[rows, :] += dv[j, ATTN_BLOCK:]

        _for_unit_batches(batch, ATTN_BATCH_BWD)
        for g, dq_ref in enumerate((dq0_ref, dq1_ref, dq2_ref)):
            dq_ref[...] = _rope(dqs[g][...], c, -sa, -sb).astype(dq_ref.dtype)
        dk_ref[...] = _rope(dks[...], c, -sa, -sb).astype(dk_ref.dtype)
        dv_ref[...] = dvs[...].astype(dv_ref.dtype)

    blk = (SEQ, ATTN_HEAD_DIM)
    out = SDS((SEQ, ATTN_OUT), BF16)
    return pl.pallas_call(
        body, name=name, grid=(ATTN_KV_HEADS,), in_specs=[*q_specs, head, head, table, table, table, head, head, head],
        out_specs=[head] * 5, out_shape=[out] * 5,
        scratch_shapes=[pltpu.VMEM(blk, F32)] * (2 * ATTN_N_PAT + 5),
        compiler_params=_cparams(("parallel",)),
    )(q, q, q, k, v, *tabs, y, lse, dy)


def layer_fwd(h, getw, prefetch, small, tabs, li):
    n = f"l{li}_"
    sv = {}
    w = dict(getw(0, h))
    u = rms_fwd(h, small["norm_mix"], n + "rms_mix")
    z = matmul(u, w["w_z"], name=n + "mm_z", tb=True, out_dtype=BF16)
    prefetch(1, z)
    xbc = matmul(u, w["w_xbc"], name=n + "mm_xbc", tb=True, out_dtype=BF16)
    dtr = matmul(u, w["w_dt"], name=n + "mm_dt", tb=True)
    q = matmul(u, w["w_q"], name=n + "mm_q", tb=True, out_dtype=BF16)
    k = matmul(u, w["w_k"], name=n + "mm_k", tb=True, out_dtype=BF16)
    v = matmul(u, w["w_v"], name=n + "mm_v", tb=True, out_dtype=BF16)
    gs = matmul(u, w["w_gs"], name=n + "mm_gs", tb=True, out_dtype=BF16)
    ga = matmul(u, w["w_ga"], name=n + "mm_ga", tb=True, out_dtype=BF16)
    xc = conv_fwd(xbc, w["conv_w"], small["conv_b"], n + "conv")
    dtr_t = dtr.T
    y_ssd, hs = ssd_fwd(xc, dtr, dtr_t, small["dt_bias"], small["dt_bias"].T, small["a_log"], small["a_log"].T, n + "ssd")
    yn = ssd_post_fwd(y_ssd, xc, z, small["d_skip_x"], small["ssd_norm"], n + "ssd_post")
    y_attn, lse = attn_fwd(q, k, v, tabs, n + "attn")
    w.update(getw(1, y_ssd))
    a = matmul(yn, w["w_ssd_branch"], name=n + "mm_a", out_dtype=BF16)
    b = matmul(y_attn, w["w_attn_branch"], name=n + "mm_b", out_dtype=BF16)
    merged = gate_fwd(a, b, gs, ga, n + "gate")
    h1 = matmul(merged, w["w_out"], name=n + "mm_o", add=h)
    w.update(getw(2, h1))
    u2 = rms_fwd(h1, small["norm_ffn"], n + "rms_ffn")
    gu = matmul(u2, w["w_gate_up"], name=n + "mm_gu", tb=True, out_dtype=BF16)
    act = swiglu_fwd(gu, n + "swiglu")
    h2 = matmul(act, w["w_down"], name=n + "mm_down", add=h1)
    sv.update(h=h, u=u, z=z, xbc=xbc, dtr=dtr, dtr_t=dtr_t, gs=gs, ga=ga, xc=xc, y_ssd=y_ssd, hs=hs, yn=yn,
              q=q, k=k, v=v, y_attn=y_attn, lse=lse, a=a, b=b, merged=merged, h1=h1, u2=u2, gu=gu, act=act, w=w)
    return h2, sv


def layer_bwd(dh, sv, small, tabs, li, emit):
    n = f"l{li}_b_"
    w = sv["w"]
    gw, gsm = {}, {}
    dact = matmul(dh, w["w_down"], name=n + "mm_dact", tb=True, out_dtype=BF16)
    gw["w_down"] = matmul(sv["act"], dh, name=n + "mm_dwdown", ta=True, out_dtype=BF16)
    dgu = swiglu_bwd(sv["gu"], dact, n + "swiglu")
    gw["w_gate_up"] = matmul(dgu, sv["u2"], name=n + "mm_dwgu", ta=True, out_dtype=BF16)
    tok = emit(2, gw)
    du2 = matmul(dgu, w["w_gate_up"], name=n + "mm_du2")
    dh1, gsm["norm_ffn"] = rms_bwd(sv["h1"], du2, dh, small["norm_ffn"] + tok, n + "rms_ffn")
    dmerged = matmul(dh1, w["w_out"], name=n + "mm_dmerged", tb=True)
    gw["w_out"] = matmul(sv["merged"], dh1, name=n + "mm_dwo", ta=True, out_dtype=BF16)
    da, db, dgs, dga = gate_bwd(sv["a"], sv["b"], sv["gs"], sv["ga"], dmerged, n + "gate")
    gw["w_ssd_branch"] = matmul(sv["yn"], da, name=n + "mm_dwa", ta=True, out_dtype=BF16)
    gw["w_attn_branch"] = matmul(sv["y_attn"], db, name=n + "mm_dwb", ta=True, out_dtype=BF16)
    tok = emit(1, gw)
    dyn = matmul(da, w["w_ssd_branch"], name=n + "mm_dyn", tb=True, out_dtype=BF16)
    dyattn = matmul(db, w["w_attn_branch"], name=n + "mm_dyattn", tb=True)
    dy_ssd, dxs_extra, dz, gsm["d_skip_x"], gsm["ssd_norm"] = ssd_post_bwd(
        sv["y_ssd"], sv["xc"], sv["z"], small["d_skip_x"] + tok, small["ssd_norm"], dyn, n + "ssd_post")
    dxc, ddtr, ddtr_t, ddtb, ddtb_t, dal, dal_t = ssd_bwd(
        sv["xc"], sv["dtr"], sv["dtr_t"], small["dt_bias"], small["dt_bias"].T, small["a_log"], small["a_log"].T,
        sv["hs"], dy_ssd, dxs_extra, n + "ssd")
    ddtr = (ddtr + ddtr_t.T).astype(BF16)
    gsm["dt_bias"] = ddtb + ddtb_t.T
    gsm["a_log"] = dal + dal_t.T
    dxbc, gw["conv_w"], gsm["conv_b"] = conv_bwd(sv["xbc"], w["conv_w"], small["conv_b"], dxc, n + "conv")
    dq0, dq1, dq2, dk, dv = attn_bwd(sv["q"], sv["k"], sv["v"], tabs, sv["y_attn"], sv["lse"], dyattn, n + "attn")
    u = sv["u"]
    segs = [("w_z", dz), ("w_xbc", dxbc), ("w_dt", ddtr), ("w_q0", dq0), ("w_q1", dq1), ("w_q2", dq2),
            ("w_k", dk), ("w_v", dv), ("w_gs", dgs), ("w_ga", dga)]
    gin = [matmul(dseg, u, name=n + "mm_d" + key, ta=True, out_dtype=BF16) for key, dseg in segs]
    gin[2] = gin[2][:SSD_HEADS]
    gw["w_in"] = jnp.concatenate(gin, axis=0)
    tok = emit(0, gw)
    du = jnp.zeros((SEQ, D_MODEL), F32) + tok
    for key, dseg in segs:
        du = matmul(dseg, w[key], name=n + "mm_du_" + key, add=du)
    dh0, gsm["norm_mix"] = rms_bwd(sv["h"], du, dh1, small["norm_mix"] + tok, n + "rms_mix")
    return dh0, gsm


def _my_place():
    return lax.axis_index("x"), lax.axis_index("y"), lax.axis_index("c")


def _flip(place, k):
    x, y, c = place
    return (1 - x if k & 4 else x, 1 - y if k & 2 else y, 1 - c if k & 1 else c)


def _index(place):
    return 4 * place[0] + 2 * place[1] + place[2]


ANY = pl.BlockSpec(memory_space=pl.ANY)
CHIP_FLIPS = (4, 2, 6)
SELF_AND_CHIPS = (0,) + CHIP_FLIPS


def all_gather(xs, name):
    na = len(xs)

    def body(*refs):
        x_refs, o_refs = refs[:na], refs[na:2 * na]
        send_sems, recv_sems, local_sems = refs[2 * na:]
        me = _my_place()
        sibling = _flip(me, 1)
        chips = [_flip(me, f) for f in CHIP_FLIPS]

        def copy(a, kk, block, to, src=None):
            dst = o_refs[a].at[_index(block)]
            return pltpu.make_async_remote_copy(
                src_ref=dst if src is None else src, dst_ref=dst, send_sem=send_sems.at[a, kk],
                recv_sem=recv_sems.at[a, kk], device_id=to, device_id_type=MESH)

        mine = [pltpu.make_async_copy(x_refs[a], o_refs[a].at[_index(me)], local_sems.at[a]) for a in range(na)]
        for cp in mine:
            cp.start()
        first = []
        for j, chip in enumerate(chips):
            first += [copy(a, 1 + j, me, chip, src=x_refs[a]) for a in range(na)]
        first += [copy(a, 0, me, sibling, src=x_refs[a]) for a in range(na)]
        for cp in first:
            cp.start()
        passed = []
        for j, chip in enumerate(chips):
            for a in range(na):
                copy(a, 1 + j, chip, me).wait_recv()
                cp = copy(a, 4 + j, chip, sibling)
                cp.start()
                passed.append(cp)
        for a in range(na):
            copy(a, 0, sibling, me).wait_recv()
        for j, chip in enumerate(chips):
            for a in range(na):
                copy(a, 4 + j, _flip(chip, 1), me).wait_recv()
        for cp in first + passed:
            cp.wait_send()
        for cp in mine:
            cp.wait()

    return pl.pallas_call(
        body, name=name, in_specs=[ANY] * na, out_specs=[ANY] * na,
        out_shape=[SDS((N_DEV,) + t.shape, t.dtype) for t in xs],
        scratch_shapes=[pltpu.SemaphoreType.DMA((na, N_DEV - 1)), pltpu.SemaphoreType.DMA((na, N_DEV - 1)),
                        pltpu.SemaphoreType.DMA((na,))],
    )(*xs)


HBM = pl.BlockSpec(memory_space=pltpu.HBM)
SEM = pl.BlockSpec(memory_space=pltpu.SEMAPHORE)
EFFECT = pltpu.SideEffectType.DATAFLOW_SIDE_EFFECTING
N_PEERS = N_DEV - 1


def _split_copy(src_ref, land_ref, send_sem, recv_sem, me, kk, scatter, landed_from_peer):
    peer = _flip(me, kk)
    src = src_ref.at[_index(peer)] if scatter else src_ref
    dst = land_ref.at[_index(peer if landed_from_peer else me)]
    return pltpu.make_async_remote_copy(src_ref=src, dst_ref=dst, send_sem=send_sem, recv_sem=recv_sem,
                                        device_id=peer, device_id_type=MESH)


ALL_PEERS = tuple(range(1, N_DEV))
EVERYONE = (0,) + ALL_PEERS


def exchange_start(srcs, lands, group_sizes, scatter, name, peers=ALL_PEERS):
    na, ng = len(srcs), len(group_sizes)

    def body(*refs):
        s_refs, l_refs = refs[:na], refs[na:2 * na]
        sems = refs[2 * na:2 * na + 2 * ng]
        token = refs[-1]
        me = _my_place()
        a = 0
        for gi, gsz in enumerate(group_sizes):
            for j in range(gsz):
                for pi, kk in enumerate(peers):
                    slot = j * len(peers) + pi
                    _split_copy(s_refs[a], l_refs[a], sems[2 * gi].at[slot], sems[2 * gi + 1].at[slot],
                                me, kk, scatter, False).start()
                a += 1
        token[...] = jnp.zeros_like(token)

    sem_shapes = []
    for gsz in group_sizes:
        sem_shapes += [pltpu.SemaphoreType.DMA((gsz * len(peers),))] * 2
    ins = [pltpu.with_memory_space_constraint(t, pltpu.HBM) for t in (*srcs, *lands)]
    res = pl.pallas_call(
        body, name=name, in_specs=[HBM] * (2 * na),
        out_specs=[SEM] * (2 * ng) + [HBM] * (2 * na) + [pl.BlockSpec(memory_space=pltpu.VMEM)],
        out_shape=sem_shapes + [pltpu.HBM(t.shape, t.dtype) for t in ins] + [SDS((8, LANES), F32)],
        input_output_aliases={i: 2 * ng + i for i in range(2 * na)},
        compiler_params=pltpu.CompilerParams(has_side_effects=EFFECT),
    )(*ins)
    sems = [(res[2 * gi], res[2 * gi + 1]) for gi in range(ng)]
    thru = res[2 * ng:2 * ng + 2 * na]
    return sems, thru[:na], thru[na:], res[-1]


def _wait_split_copies(s_refs, l_refs, send_sems, recv_sems, scatter, peers):
    me = _my_place()
    for j in range(len(s_refs)):
        for pi, kk in enumerate(peers):
            slot = j * len(peers) + pi
            cp = _split_copy(s_refs[j], l_refs[j], send_sems.at[slot], recv_sems.at[slot], me, kk, scatter, True)
            cp.wait_send()
            cp.wait_recv()


def exchange_wait(srcs, lands, sems, after, scatter, name, peers=ALL_PEERS):
    n = len(srcs)

    def body(*refs):
        s_refs, l_refs = refs[:n], refs[n:2 * n]
        _wait_split_copies(s_refs, l_refs, refs[2 * n], refs[2 * n + 1], scatter, peers)

    res = pl.pallas_call(
        body, name=name, in_specs=[HBM] * (2 * n) + [SEM, SEM, ANY], out_specs=[HBM] * (2 * n),
        out_shape=[pltpu.HBM(t.shape, t.dtype) for t in (*srcs, *lands)],
        input_output_aliases={i: i for i in range(2 * n)},
        compiler_params=pltpu.CompilerParams(has_side_effects=EFFECT),
    )(*srcs, *lands, sems[0], sems[1], after)
    return res[n:]


def _sibling_copies(l_refs, send_sems, recv_sems, arriving):
    me = _my_place()
    sibling = _flip(me, 1)
    held = [me] + [_flip(me, f) for f in CHIP_FLIPS]
    copies = []
    for j, land in enumerate(l_refs):
        for bi, place in enumerate(held):
            blk = land.at[_index(_flip(place, 1) if arriving else place)]
            slot = j * len(held) + bi
            copies.append(pltpu.make_async_remote_copy(src_ref=blk, dst_ref=blk, send_sem=send_sems.at[slot],
                                                       recv_sem=recv_sems.at[slot], device_id=sibling, device_id_type=MESH))
    return copies


def gather_forward(srcs, lands, sems, after, name):
    n = len(srcs)

    def body(*refs):
        s_refs, l_refs = refs[:n], refs[n:2 * n]
        _wait_split_copies(s_refs, l_refs, refs[2 * n], refs[2 * n + 1], False, SELF_AND_CHIPS)
        for cp in _sibling_copies(l_refs, refs[2 * n + 3], refs[2 * n + 4], False):
            cp.start()

    n_slots = n * (1 + len(CHIP_FLIPS))
    res = pl.pallas_call(
        body, name=name, in_specs=[HBM] * (2 * n) + [SEM, SEM, ANY],
        out_specs=[SEM, SEM] + [HBM] * (2 * n),
        out_shape=[pltpu.SemaphoreType.DMA((n_slots,))] * 2 + [pltpu.HBM(t.shape, t.dtype) for t in (*srcs, *lands)],
        input_output_aliases={i: 2 + i for i in range(2 * n)},
        compiler_params=pltpu.CompilerParams(has_side_effects=EFFECT),
    )(*srcs, *lands, sems[0], sems[1], after)
    return (res[0], res[1]), res[2 + n:]


def gather_finish(lands, sems, after, name):
    n = len(lands)

    def body(*refs):
        l_refs = refs[:n]
        for cp in _sibling_copies(l_refs, refs[n], refs[n + 1], True):
            cp.wait_send()
            cp.wait_recv()

    return pl.pallas_call(
        body, name=name, in_specs=[HBM] * n + [SEM, SEM, ANY], out_specs=[HBM] * n,
        out_shape=[pltpu.HBM(t.shape, t.dtype) for t in lands],
        input_output_aliases={i: i for i in range(n)},
        compiler_params=pltpu.CompilerParams(has_side_effects=EFFECT),
    )(*lands, sems[0], sems[1], after)


def landing_zone(block):
    return lax.empty((N_DEV,) + block.shape, block.dtype)


def sum_parts(parts, name, row_major_3d=False):
    _, r, c = parts.shape
    tc = _pick(c, (256, 128))

    def body(p_ref, o_ref):
        acc = p_ref[0].astype(F32)
        for i in range(1, N_DEV):
            acc = acc + p_ref[i].astype(F32)
        if row_major_3d:
            o_ref[:, 0, :] = acc
        else:
            o_ref[...] = acc

    out_spec = pl.BlockSpec((r, 1, tc), lambda i: (0, 0, i)) if row_major_3d else pl.BlockSpec((r, tc), lambda i: (0, i))
    return pl.pallas_call(
        body, name=name, grid=(c // tc,), in_specs=[pl.BlockSpec((N_DEV, r, tc), lambda i: (0, 0, i))],
        out_specs=out_spec, out_shape=SDS((r, 1, c) if row_major_3d else (r, c), F32),
        compiler_params=_cparams(("parallel",)),
    )(parts)


ADAMW_BLOCK_BYTES = 2 * 1024 * 1024


def adamw(w, g, m, v, name):
    shape = w.shape
    lay, rows, cols = ((1, 1) + shape)[-3:]
    tr = _pick(rows, (256, 128))
    tc = cols if tr * cols * 4 <= ADAMW_BLOCK_BYTES else _pick(cols, (256, 128))
    c1 = 1.0 / (1.0 - ADAM_B1 ** ADAM_STEP)
    c2 = 1.0 / (1.0 - ADAM_B2 ** ADAM_STEP)

    def body(w_ref, g_ref, m_ref, v_ref, d_ref, nm_ref, nv_ref):
        gg = g_ref[...]
        nm = ADAM_B1 * m_ref[...] + (1.0 - ADAM_B1) * gg
        nv = ADAM_B2 * v_ref[...] + (1.0 - ADAM_B2) * (gg * gg)
        d_ref[...] = -ADAM_LR * ((nm * c1) / (jnp.sqrt(nv * c2) + ADAM_EPS) + ADAM_WD * w_ref[...])
        nm_ref[...] = nm
        nv_ref[...] = nv

    spec = pl.BlockSpec((1, tr, tc), lambda l, i, j: (l, i, j))
    outs = pl.pallas_call(
        body, name=name, grid=(lay, rows // tr, cols // tc), in_specs=[spec] * 4, out_specs=[spec] * 3,
        out_shape=[SDS((lay, rows, cols), F32)] * 3, compiler_params=_cparams(("parallel",) * 3),
    )(*[t.reshape(lay, rows, cols) for t in (w, g, m, v)])
    return [o.reshape(shape) for o in outs]


def adamw_layer_inner(w, gs, m, v, name):
    rows, lay, cols = w.shape
    tr = _pick(rows, (256, 220, 128))
    c1 = 1.0 / (1.0 - ADAM_B1 ** ADAM_STEP)
    c2 = 1.0 / (1.0 - ADAM_B2 ** ADAM_STEP)

    def body(*refs):
        w_ref, m_ref, v_ref = refs[:3]
        g_refs = refs[3:3 + lay]
        go_ref, d_ref, nm_ref, nv_ref = refs[3 + lay:]
        for l, g_ref in enumerate(g_refs):
            gg = g_ref[:, 0, :]
            nm = ADAM_B1 * m_ref[:, l, :] + (1.0 - ADAM_B1) * gg
            nv = ADAM_B2 * v_ref[:, l, :] + (1.0 - ADAM_B2) * (gg * gg)
            d_ref[:, l, :] = -ADAM_LR * ((nm * c1) / (jnp.sqrt(nv * c2) + ADAM_EPS) + ADAM_WD * w_ref[:, l, :])
            go_ref[:, l, :] = gg
            nm_ref[:, l, :] = nm
            nv_ref[:, l, :] = nv

    inner = pl.BlockSpec((tr, lay, cols), lambda i: (i, 0, 0))
    plain = pl.BlockSpec((tr, 1, cols), lambda i: (i, 0, 0))
    return pl.pallas_call(
        body, name=name, grid=(rows // tr,), in_specs=[inner] * 3 + [plain] * lay, out_specs=[inner] * 4,
        out_shape=[SDS((rows, lay, cols), F32)] * 4, compiler_params=_cparams(("parallel",)),
    )(w, m, v, *gs)


BIG = ("w_in", "conv_w", "w_ssd_branch", "w_attn_branch", "w_out", "w_gate_up", "w_down")
TRANSPOSED = ("w_in", "w_gate_up")
SMALL = ("norm_mix", "conv_b", "dt_bias", "a_log", "d_skip", "ssd_norm", "norm_ffn")
SMALL_SIZE = {"norm_mix": 1024, "conv_b": 3072, "dt_bias": 32, "a_log": 32, "d_skip": 32, "ssd_norm": 2048, "norm_ffn": 1024}
FLAT_W = 512
SMALL_TOTAL = DEPTH * sum(SMALL_SIZE.values()) + D_MODEL + LANES
SMALL_ROWS = 32
assert SMALL_ROWS * FLAT_W >= SMALL_TOTAL


GROUPS = (("w_in", "conv_w"), ("w_ssd_branch", "w_attn_branch", "w_out"), ("w_gate_up", "w_down"))


def to_wire(k, shard):
    if k in TRANSPOSED:
        return shard.T.astype(BF16)
    return shard if k == "conv_w" else shard.astype(BF16)


def full_weights(k, g):
    if k == "conv_w":
        return {k: g.transpose(1, 0, 2).reshape(SSD_CONV, SSD_CONV_CH)}
    full = g.reshape(-1, g.shape[-1])
    if k != "w_in":
        return {k: full}
    w, off = {}, 0
    for nm, r in IN_ROWS:
        w[nm] = full[off:off + r]
        off += r
    w["w_q"] = full[sum(r for _, r in IN_ROWS[:3]):sum(r for _, r in IN_ROWS[:6])]
    w["w_dt"] = jnp.pad(w["w_dt"], ((0, HPAD - SSD_HEADS), (0, 0)))
    return w


def grads_to_wire(k, g):
    if k == "conv_w":
        return g.reshape(SSD_CONV, N_DEV, SSD_CONV_CH // N_DEV).transpose(1, 0, 2)
    return g.reshape(N_DEV, g.shape[0] // N_DEV, g.shape[1])


def _pad_heads(t):
    return jnp.pad(t.reshape(1, SSD_HEADS), ((0, 0), (0, HPAD - SSD_HEADS)))


def local_step(x, target, getw, prefetch, emit, smalls, norm_final):
    tabs = rope_tables()
    sms = []
    for li in range(DEPTH):
        s = smalls[li]
        sms.append({
            "norm_mix": s["norm_mix"].reshape(1, -1), "conv_b": s["conv_b"].reshape(1, -1),
            "dt_bias": _pad_heads(s["dt_bias"]), "a_log": _pad_heads(s["a_log"]),
            "d_skip_x": jnp.repeat(s["d_skip"], SSD_HEAD_DIM).reshape(1, -1),
            "ssd_norm": s["ssd_norm"].reshape(1, -1), "norm_ffn": s["norm_ffn"].reshape(1, -1)})
    h = x
    saved = []
    for li in range(DEPTH):
        h, sv = layer_fwd(h, functools.partial(getw, li), functools.partial(prefetch, li), sms[li], tabs, li)
        saved.append(sv)
    dh, g_final, loss = loss_head(h, target, norm_final.reshape(1, -1), "loss_head")
    gsms = [None] * DEPTH
    for li in reversed(range(DEPTH)):
        dh, gsm = layer_bwd(dh, saved[li], sms[li], tabs, li, functools.partial(emit, li))
        gsms[li] = {
            "norm_mix": gsm["norm_mix"].reshape(-1), "conv_b": gsm["conv_b"].reshape(-1),
            "dt_bias": gsm["dt_bias"][0, :SSD_HEADS], "a_log": gsm["a_log"][0, :SSD_HEADS],
            "d_skip": gsm["d_skip_x"].reshape(SSD_HEADS, SSD_HEAD_DIM).sum(axis=1),
            "ssd_norm": gsm["ssd_norm"].reshape(-1), "norm_ffn": gsm["norm_ffn"].reshape(-1)}
    return loss, dh, gsms, g_final.reshape(-1)


def kernel(x, norm_mix, w_in, conv_w, conv_b, dt_bias, a_log, d_skip, ssd_norm, w_ssd_branch, w_attn_branch, w_out, norm_ffn, w_gate_up, w_down, norm_final, loss_target, m_norm_mix, m_w_in, m_conv_w, m_conv_b, m_dt_bias, m_a_log, m_d_skip, m_ssd_norm, m_w_ssd_branch, m_w_attn_branch, m_w_out, m_norm_ffn, m_w_gate_up, m_w_down, m_norm_final, v_norm_mix, v_w_in, v_conv_w, v_conv_b, v_dt_bias, v_a_log, v_d_skip, v_ssd_norm, v_w_ssd_branch, v_w_attn_branch, v_w_out, v_norm_ffn, v_w_gate_up, v_w_down, v_norm_final):
    wv = dict(norm_mix=norm_mix, w_in=w_in, conv_w=conv_w, conv_b=conv_b, dt_bias=dt_bias, a_log=a_log, d_skip=d_skip,
              ssd_norm=ssd_norm, w_ssd_branch=w_ssd_branch, w_attn_branch=w_attn_branch, w_out=w_out, norm_ffn=norm_ffn,
              w_gate_up=w_gate_up, w_down=w_down, norm_final=norm_final)
    mv = dict(norm_mix=m_norm_mix, w_in=m_w_in, conv_w=m_conv_w, conv_b=m_conv_b, dt_bias=m_dt_bias, a_log=m_a_log,
              d_skip=m_d_skip, ssd_norm=m_ssd_norm, w_ssd_branch=m_w_ssd_branch, w_attn_branch=m_w_attn_branch,
              w_out=m_w_out, norm_ffn=m_norm_ffn, w_gate_up=m_w_gate_up, w_down=m_w_down, norm_final=m_norm_final)
    vv = dict(norm_mix=v_norm_mix, w_in=v_w_in, conv_w=v_conv_w, conv_b=v_conv_b, dt_bias=v_dt_bias, a_log=v_a_log,
              d_skip=v_d_skip, ssd_norm=v_ssd_norm, w_ssd_branch=v_w_ssd_branch, w_attn_branch=v_w_attn_branch,
              w_out=v_w_out, norm_ffn=v_norm_ffn, w_gate_up=v_w_gate_up, w_down=v_w_down, norm_final=v_norm_final)
    order = ("norm_mix", "w_in", "conv_w", "conv_b", "dt_bias", "a_log", "d_skip", "ssd_norm", "w_ssd_branch",
             "w_attn_branch", "w_out", "norm_ffn", "w_gate_up", "w_down", "norm_final")

    smalls = [{k: wv[k][li] for k in SMALL} for li in range(DEPTH)]
    n_groups = len(GROUPS)

    first_lands = all_gather([to_wire(k, wv[k][0]) for k in GROUPS[0]], "gather_first")
    later = [(li, gi) for li in range(DEPTH) for gi in range(n_groups)][1:]
    behind_first = first_lands[1][0, 0, 0] * 0.0
    srcs = [to_wire(k, wv[k][li] + behind_first if k == "conv_w" else wv[k][li]) for li, gi in later for k in GROUPS[gi]]
    sizes = [len(GROUPS[gi]) for _, gi in later]
    w_sems, w_srcs, w_lands, token = exchange_start(srcs, [landing_zone(s) for s in srcs], sizes, False,
                                                    "gather_start", peers=SELF_AND_CHIPS)
    smalls[0]["norm_mix"] = smalls[0]["norm_mix"] + token[0, 0]
    second_leg = {}

    def forward(slot, after):
        if slot < len(later) and slot not in second_leg:
            sl = slice(sum(sizes[:slot]), sum(sizes[:slot + 1]))
            second_leg[slot] = gather_forward(w_srcs[sl], w_lands[sl], w_sems[slot], after, f"gather_forward_{slot}")

    def prefetch(li, gi, after):
        if (li, gi) == later[0]:
            forward(0, after)

    def getw(li, gi, after):
        if (li, gi) == (0, 0):
            lands = first_lands
        else:
            slot = later.index((li, gi))
            forward(slot, after)
            sems2, lands2 = second_leg[slot]
            lands = gather_finish(lands2, sems2, after, f"gather_finish_{li}_{gi}")
            forward(slot + 1, lands[0])
        w = {}
        for k, land in zip(GROUPS[gi], lands):
            w.update(full_weights(k, land))
        return w

    pending = []

    def emit(li, gi, gw):
        parts = [grads_to_wire(k, gw[k]) for k in GROUPS[gi]]
        lands = [landing_zone(p[0]) for p in parts]
        sems, p_thru, l_thru, tok = exchange_start(parts, lands, [len(parts)], True, f"grads_start_{li}_{gi}", peers=EVERYONE)
        pending.append((li, gi, sems[0], p_thru, l_thru))
        return tok[0, 0]

    loss_p, dx, gsms, g_final = local_step(x[0], loss_target[0], getw, prefetch, emit, smalls, norm_final)

    grads, deltas, new_m, new_v = {}, {}, {}, {}

    def update(k):
        if k == "w_in":
            inner = lambda t: t.transpose(2, 0, 1)
            outs = adamw_layer_inner(inner(wv[k]), shard_g[k], inner(mv[k]), inner(vv[k]), "adamw_" + k)
            grads[k], deltas[k], new_m[k], new_v[k] = (t.transpose(1, 2, 0) for t in outs)
            return outs[3]
        if k in BIG:
            grads[k] = jnp.stack([g.T if k in TRANSPOSED else g for g in shard_g[k]])
        deltas[k], new_m[k], new_v[k] = adamw(wv[k], grads[k], mv[k], vv[k], "adamw_" + k)
        return new_v[k]

    shard_g = {k: [None] * DEPTH for k in BIG}

    def collect(entry, after):
        li, gi, sems, p_thru, l_thru = entry
        recv = exchange_wait(p_thru, l_thru, sems, after, True, f"grads_wait_{li}_{gi}", peers=EVERYONE)
        for k, r in zip(GROUPS[gi], recv):
            if k == "conv_w":
                r = r.reshape(N_DEV, 1, -1)
            after = sum_parts(r, f"sum_{k}_{li}", row_major_3d=(k == "w_in"))
            shard_g[k][li] = after if k in TRANSPOSED else after.reshape(wv[k].shape[1:])
        return after

    after = dx
    for entry in pending[:-1]:
        after = collect(entry, after)
    done = [after[:1, :1].reshape(1)]
    for gi in (2, 1):
        for k in GROUPS[gi]:
            done.append(update(k).reshape(-1)[:1])

    flat = [gsms[li][k] for li in range(DEPTH) for k in SMALL] + [g_final, loss_p.reshape(-1)]
    flat.append(jnp.zeros((SMALL_ROWS * FLAT_W - SMALL_TOTAL,), F32))
    small_all = all_gather([jnp.concatenate(flat).reshape(SMALL_ROWS, FLAT_W)], "gather_small")[0]
    small_sum = sum_parts(small_all, "sum_small").reshape(-1)
    off = 0
    per_layer = {k: [] for k in SMALL}
    for li in range(DEPTH):
        for k in SMALL:
            per_layer[k].append(small_sum[off:off + SMALL_SIZE[k]])
            off += SMALL_SIZE[k]
    for k in SMALL:
        grads[k] = jnp.stack(per_layer[k])
    grads["norm_final"] = small_sum[off:off + D_MODEL]
    loss = small_sum[off + D_MODEL]
    for k in (*SMALL, "norm_final"):
        done.append(update(k).reshape(-1)[:1])

    collect(pending[-1], jnp.concatenate(done))
    for k in GROUPS[0]:
        update(k)

    return (loss, dx.reshape(x.shape), *[grads[k] for k in order], *[deltas[k] for k in order],
            *[new_m[k] for k in order], *[new_v[k] for k in order])
```

```python
import functools

import jax
import jax.numpy as jnp
from jax import lax
from jax.experimental import pallas as pl
from jax.experimental.pallas import tpu as pltpu

F32, BF16 = jnp.float32, jnp.bfloat16
SDS = jax.ShapeDtypeStruct
MESH = pl.DeviceIdType.MESH

D_MODEL = 1024
SEQ = 2048
DEPTH = 2
RMS_EPS = 1e-5
SSD_INNER = 2048
SSD_HEAD_DIM = 64
SSD_HEADS = 32
SSD_STATE = 128
SSD_GROUPS = 4
SSD_CONV = 4
SSD_CHUNK = 128
SSD_CONV_CH = 3072
ATTN_HEAD_DIM = 128
ATTN_KV_HEADS = 8
ATTN_DILATIONS = (1, 4, 16)
ATTN_N_PAT = 3
ATTN_BLOCK = 128
ATTN_OUT = 1024
ROPE_THETA = 500000.0
ROPE_DIM = 32
FFN_HIDDEN = 2816
ADAM_LR, ADAM_B1, ADAM_B2, ADAM_EPS, ADAM_WD, ADAM_STEP = 0.001, 0.9, 0.999, 1e-08, 0.01, 10

N_DEV = 8
LANES = 128
VMEM_LIMIT = 56 * 1024 * 1024
HPAD = 128
HIGHEST = lax.Precision.HIGHEST

IN_ROWS = (("w_z", 2048), ("w_xbc", 3072), ("w_dt", 32), ("w_q0", 1024), ("w_q1", 1024), ("w_q2", 1024),
           ("w_k", 1024), ("w_v", 1024), ("w_gs", 1024), ("w_ga", 1024))
N_IN = sum(r for _, r in IN_ROWS)


def _cparams(sem):
    return pltpu.CompilerParams(dimension_semantics=sem, vmem_limit_bytes=VMEM_LIMIT)


def _sigmoid(x):
    return 0.5 * jnp.tanh(0.5 * x) + 0.5


def _silu(x):
    return x * _sigmoid(x)


def _softplus(x):
    return jnp.maximum(x, 0.0) + jnp.log(1.0 + jnp.exp(-jnp.abs(x)))


def _dot(a, b, dims=(((1,), (0,)), ((), ())), precision=None):
    return lax.dot_general(a, b, dims, precision=precision, preferred_element_type=F32)


NT = (((1,), (1,)), ((), ()))
TN = (((0,), (0,)), ((), ()))


def _bdot(a, b, dims=(((1,), (0,)), ((), ()))):
    return _dot(a.astype(BF16), b.astype(BF16), dims)


def _pick(dim, cands):
    for c in cands:
        if dim % c == 0:
            return c
    return dim


WHOLE_K_BUDGET = 40 * 1024 * 1024
RESIDENT_B_BYTES = 12 * 1024 * 1024
OUT_TILE_BYTES = 6 * 1024 * 1024


def matmul(a, b, *, name, ta=False, tb=False, out_dtype=F32, add=None):
    m, k = (a.shape[1], a.shape[0]) if ta else a.shape
    n = b.shape[0] if tb else b.shape[1]
    out_bytes = jnp.dtype(out_dtype).itemsize + (4 if add is not None else 0)
    if k * n * b.dtype.itemsize <= RESIDENT_B_BYTES:
        tn = n
        tm = next(t for t in (512, 256, 128) if m % t == 0 and t * n * out_bytes <= OUT_TILE_BYTES)
    else:
        tn = _pick(n, (1024, 1408, 512, 256, 128))
        tm = _pick(m, (512, 1408, 256, 128)) if tn == n else _pick(m, (1024, 1408, 512, 256, 128))
    tk = _pick(k, (2048, 1024, 1408, 512, 256, 128))
    whole_k_bytes = 2 * (tm * k * a.dtype.itemsize + k * tn * b.dtype.itemsize)
    if tn == n and whole_k_bytes <= WHOLE_K_BUDGET:
        tk = k
    nk = k // tk
    a_spec = pl.BlockSpec((tk, tm), lambda i, j, kk: (kk, i)) if ta else pl.BlockSpec((tm, tk), lambda i, j, kk: (i, kk))
    b_spec = pl.BlockSpec((tn, tk), lambda i, j, kk: (j, kk)) if tb else pl.BlockSpec((tk, tn), lambda i, j, kk: (kk, j))
    dims = (((0 if ta else 1,), (1 if tb else 0,)), ((), ()))
    has_add = add is not None

    def body(*refs):
        a_ref, b_ref = refs[:2]
        add_ref = refs[2] if has_add else None
        o_ref = refs[3] if has_add else refs[2]
        acc = refs[-1] if nk > 1 else None
        kk = pl.program_id(2)

        def product():
            return _dot(a_ref[...].astype(BF16), b_ref[...].astype(BF16), dims)

        def finish(r):
            if has_add:
                r = r + add_ref[...].astype(F32)
            o_ref[...] = r.astype(o_ref.dtype)

        if nk == 1:
            finish(product())
            return

        @pl.when(kk == 0)
        def _():
            acc[...] = product()

        @pl.when((kk > 0) & (kk < nk - 1))
        def _():
            acc[...] += product()

        @pl.when(kk == nk - 1)
        def _():
            finish(acc[...] + product())

    in_specs = [a_spec, b_spec]
    args = [a, b]
    if has_add:
        in_specs.append(pl.BlockSpec((tm, tn), lambda i, j, kk: (i, j)))
        args.append(add)
    return pl.pallas_call(
        body, name=name, grid=(m // tm, n // tn, nk),
        in_specs=in_specs, out_specs=pl.BlockSpec((tm, tn), lambda i, j, kk: (i, j)),
        out_shape=SDS((m, n), out_dtype), scratch_shapes=[pltpu.VMEM((tm, tn), F32)] if nk > 1 else [],
        compiler_params=_cparams(("parallel", "parallel", "arbitrary")),
    )(*args)


def rowcall(name, fn, rows, params, row_outs, red_outs=(), tr=256):
    s = rows[0].shape[0]
    n_in = len(rows) + len(params)
    n_row = len(row_outs)

    def body(*refs):
        outs = fn(*[r[...].astype(F32) for r in refs[:n_in]])
        if not isinstance(outs, (tuple, list)):
            outs = (outs,)
        orefs = refs[n_in:]
        for r, o in zip(orefs[:n_row], outs[:n_row]):
            r[...] = o.astype(r.dtype)
        if red_outs:
            @pl.when(pl.program_id(0) == 0)
            def _():
                for r in orefs[n_row:]:
                    r[...] = jnp.zeros_like(r)
            for r, o in zip(orefs[n_row:], outs[n_row:]):
                r[...] += o.astype(F32)

    widths = [a[1] if isinstance(a, tuple) else a.shape[1] for a in rows]
    rows = [a[0] if isinstance(a, tuple) else a for a in rows]
    in_specs = [pl.BlockSpec((tr, wd), lambda i: (i, 0)) for wd in widths]
    in_specs += [pl.BlockSpec(p.shape, lambda i: (0, 0)) for p in params]
    out_specs = [pl.BlockSpec((tr, c), lambda i: (i, 0)) for c, _ in row_outs]
    out_specs += [pl.BlockSpec(shp, lambda i: (0, 0)) for shp in red_outs]
    out_shape = [SDS((s, c), dt) for c, dt in row_outs] + [SDS(shp, F32) for shp in red_outs]
    res = pl.pallas_call(
        body, name=name, grid=(s // tr,), in_specs=in_specs, out_specs=out_specs, out_shape=out_shape,
        compiler_params=_cparams(("arbitrary",) if red_outs else ("parallel",)),
    )(*rows, *params)
    return res


def _rms(x, w):
    return x * lax.rsqrt(jnp.mean(x * x, axis=-1, keepdims=True) + RMS_EPS) * w


def rms_fwd(h, w, name):
    return rowcall(name, _rms, [h], [w], [(D_MODEL, BF16)])[0]


def rms_bwd(h, du, dres, w, name):
    def fn(hb, dub, dresb, wb):
        _, vjp = jax.vjp(_rms, hb, wb)
        dh, dw = vjp(dub)
        return dh + dresb, dw
    return rowcall(name, fn, [h, du, dres], [w], [(D_MODEL, F32)], [(1, D_MODEL)])


def loss_head(h, target, w, name):
    def fn(hb, tb, wb):
        def f(hh, ww):
            err = _rms(hh, ww) - tb
            return 0.5 * jnp.sum(jnp.mean(err * err, axis=-1, keepdims=True), axis=0, keepdims=True)
        val, vjp = jax.vjp(f, hb, wb)
        dh, dw = vjp(jnp.ones((1, 1), F32))
        return dh, dw, jnp.broadcast_to(val, (1, LANES))
    return rowcall(name, fn, [h, target], [w], [(D_MODEL, F32)], [(1, D_MODEL), (1, LANES)])


def _gate(a, b, gs, ga):
    return _sigmoid(gs) * a + _sigmoid(ga) * b


def gate_fwd(a, b, gs, ga, name):
    return rowcall(name, _gate, [a, b, gs, ga], [], [(D_MODEL, BF16)])[0]


def gate_bwd(a, b, gs, ga, dm, name):
    def fn(ab, bb, gsb, gab, dmb):
        _, vjp = jax.vjp(_gate, ab, bb, gsb, gab)
        return vjp(dmb)
    return rowcall(name, fn, [a, b, gs, ga, dm], [], [(D_MODEL, BF16)] * 4)


def _swiglu(gu):
    return _silu(gu[:, :FFN_HIDDEN]) * gu[:, FFN_HIDDEN:]


def swiglu_fwd(gu, name):
    return rowcall(name, _swiglu, [gu], [], [(FFN_HIDDEN, BF16)])[0]


def swiglu_bwd(gu, dact, name):
    def fn(gub, db):
        _, vjp = jax.vjp(_swiglu, gub)
        return vjp(db.astype(F32))[0]
    return rowcall(name, fn, [gu, dact], [], [(2 * FFN_HIDDEN, BF16)])[0]


def _ssd_post(y, xs, z, dskip, normw):
    y = (y + dskip * xs) * _silu(z)
    gw = SSD_INNER // SSD_GROUPS
    parts = []
    for g in range(SSD_GROUPS):
        yg = y[:, g * gw:(g + 1) * gw]
        parts.append(yg * lax.rsqrt(jnp.mean(yg * yg, axis=-1, keepdims=True) + RMS_EPS))
    return jnp.concatenate(parts, axis=-1) * normw


def ssd_post_fwd(y, xc, z, dskip, normw, name):
    return rowcall(name, _ssd_post, [y, (xc, SSD_INNER), z], [dskip, normw], [(SSD_INNER, BF16)])[0]


def ssd_post_bwd(y, xc, z, dskip, normw, dyn, name):
    def fn(yb, xsb, zb, dynb, db, nb):
        _, vjp = jax.vjp(_ssd_post, yb, xsb, zb, db, nb)
        return vjp(dynb)
    return rowcall(name, fn, [y, (xc, SSD_INNER), z, dyn], [dskip, normw],
                   [(SSD_INNER, BF16)] * 3, [(1, SSD_INNER), (1, SSD_INNER)])


def _rope(t, cosf, sina, sinb):
    return t * cosf + pltpu.roll(t, LANES - ROPE_DIM // 2, 1) * sina + pltpu.roll(t, ROPE_DIM // 2, 1) * sinb


def rope_tables():
    half = ROPE_DIM // 2
    inv = ROPE_THETA ** (-jnp.arange(0, ROPE_DIM, 2, dtype=F32) / ROPE_DIM)
    ang = jnp.arange(SEQ, dtype=F32)[:, None] * inv[None, :]
    cos, sin = jnp.cos(ang), jnp.sin(ang)
    zeros = jnp.zeros((SEQ, LANES - ROPE_DIM), F32)
    z16 = jnp.zeros((SEQ, half), F32)
    cosf = jnp.concatenate([cos, cos, jnp.ones((SEQ, LANES - ROPE_DIM), F32)], axis=1)
    sina = jnp.concatenate([-sin, z16, zeros], axis=1)
    sinb = jnp.concatenate([z16, sin, zeros], axis=1)
    return cosf, sina, sinb


CONV_TC = 256


def _conv_pre(x, w, b, row):
    acc = x * w[SSD_CONV - 1:SSD_CONV, :] + b
    shifted = [x]
    for j in range(1, SSD_CONV):
        xs = jnp.where(row >= j, pltpu.roll(x, j, 0), 0.0)
        shifted.append(xs)
        acc = acc + xs * w[SSD_CONV - 1 - j:SSD_CONV - j, :]
    return acc, shifted


def conv_fwd(xbc, w, b, name):
    def body(x_ref, w_ref, b_ref, o_ref):
        row = lax.broadcasted_iota(jnp.int32, (SEQ, CONV_TC), 0)
        pre, _ = _conv_pre(x_ref[...].astype(F32), w_ref[...], b_ref[...], row)
        o_ref[...] = _silu(pre).astype(o_ref.dtype)
    return pl.pallas_call(
        body, name=name, grid=(SSD_CONV_CH // CONV_TC,),
        in_specs=[pl.BlockSpec((SEQ, CONV_TC), lambda i: (0, i)), pl.BlockSpec((SSD_CONV, CONV_TC), lambda i: (0, i)),
                  pl.BlockSpec((1, CONV_TC), lambda i: (0, i))],
        out_specs=pl.BlockSpec((SEQ, CONV_TC), lambda i: (0, i)),
        out_shape=SDS((SEQ, SSD_CONV_CH), BF16), compiler_params=_cparams(("parallel",)),
    )(xbc, w, b)


def conv_bwd(xbc, w, b, dxc, name):
    def body(x_ref, w_ref, b_ref, dy_ref, dx_ref, dw_ref, db_ref):
        row = lax.broadcasted_iota(jnp.int32, (SEQ, CONV_TC), 0)
        wv = w_ref[...]
        pre, shifted = _conv_pre(x_ref[...].astype(F32), wv, b_ref[...], row)
        sg = _sigmoid(pre)
        ds = dy_ref[...].astype(F32) * (sg * (1.0 + pre * (1.0 - sg)))
        dx = ds * wv[SSD_CONV - 1:SSD_CONV, :]
        for j in range(1, SSD_CONV):
            dsj = jnp.where(row < SEQ - j, pltpu.roll(ds, SEQ - j, 0), 0.0)
            dx = dx + dsj * wv[SSD_CONV - 1 - j:SSD_CONV - j, :]
        dx_ref[...] = dx.astype(dx_ref.dtype)
        for j in range(SSD_CONV):
            dw_ref[SSD_CONV - 1 - j:SSD_CONV - j, :] = jnp.sum(ds * shifted[j], axis=0, keepdims=True)
        db_ref[...] = jnp.sum(ds, axis=0, keepdims=True)
    return pl.pallas_call(
        body, name=name, grid=(SSD_CONV_CH // CONV_TC,),
        in_specs=[pl.BlockSpec((SEQ, CONV_TC), lambda i: (0, i)), pl.BlockSpec((SSD_CONV, CONV_TC), lambda i: (0, i)),
                  pl.BlockSpec((1, CONV_TC), lambda i: (0, i)), pl.BlockSpec((SEQ, CONV_TC), lambda i: (0, i))],
        out_specs=[pl.BlockSpec((SEQ, CONV_TC), lambda i: (0, i)), pl.BlockSpec((SSD_CONV, CONV_TC), lambda i: (0, i)),
                   pl.BlockSpec((1, CONV_TC), lambda i: (0, i))],
        out_shape=[SDS((SEQ, SSD_CONV_CH), BF16), SDS((SSD_CONV, SSD_CONV_CH), F32), SDS((1, SSD_CONV_CH), F32)],
        compiler_params=_cparams(("parallel",)),
    )(xbc, w, b, dxc)


N_CHUNKS = SEQ // SSD_CHUNK
N_PAIRS = SSD_HEADS // 2
PAIRS_PER_GROUP = N_PAIRS // SSD_GROUPS
B_OFF = SSD_INNER
C_OFF = SSD_INNER + SSD_GROUPS * SSD_STATE


def _ssd_prefix(dtr, dtr_t, dtb, dtb_t, alog, alog_t):
    ln = SSD_CHUNK
    dt = _softplus(dtr + dtb)
    dt_t = _softplus(dtr_t + dtb_t)
    dta = dt * (-jnp.exp(alog))
    dta_t = dt_t * (-jnp.exp(alog_t))
    r = lax.broadcasted_iota(jnp.int32, (ln, ln), 0)
    c = lax.broadcasted_iota(jnp.int32, (ln, ln), 1)
    a_cum = _dot((r >= c).astype(F32), dta, precision=HIGHEST)
    a_cum_t = _dot(dta_t, (r <= c).astype(F32), precision=HIGHEST)
    a_last = jnp.sum(dta_t, axis=1, keepdims=True)
    return dt, a_cum, a_cum_t, a_last


def _bein(spec, a, b):
    return jnp.einsum(spec, a.astype(BF16), b.astype(BF16), preferred_element_type=F32)


SSD_GROUPS_PER_BATCH = 4


def _ssd_group(xs3, bgs, cgs, h3, dt, a_cum, a_cum_t, a_last, *, groups):
    ln = SSD_CHUNK
    lane = lax.broadcasted_iota(jnp.int32, (ln, LANES), 1)
    sub = lax.broadcasted_iota(jnp.int32, (LANES, SSD_STATE), 0)
    row = lax.broadcasted_iota(jnp.int32, (ln, ln), 0)
    col = lax.broadcasted_iota(jnp.int32, (ln, ln), 1)
    lo = lane < SSD_HEAD_DIM
    causal = row >= col
    m_lo, m_hi, dts, acs, lasts, cds, cg3, bg3 = [], [], [], [], [], [], [], []
    for g, bg, cg in zip(groups, bgs, cgs):
        cb = _bdot(cg, bg, NT)
        for j in range(PAIRS_PER_GROUP):
            e0 = 2 * (g * PAIRS_PER_GROUP + j)
            e1 = e0 + 1
            c0, c1 = a_cum[:, e0:e0 + 1], a_cum[:, e1:e1 + 1]
            r0, r1 = a_cum_t[e0:e0 + 1, :], a_cum_t[e1:e1 + 1, :]
            l0, l1 = a_last[e0:e0 + 1, :], a_last[e1:e1 + 1, :]
            m_lo.append(cb * jnp.exp(jnp.where(causal, c0 - r0, -jnp.inf)))
            m_hi.append(cb * jnp.exp(jnp.where(causal, c1 - r1, -jnp.inf)))
            dts.append(jnp.where(lo, dt[:, e0:e0 + 1], dt[:, e1:e1 + 1]))
            acs.append(jnp.where(lo, c0, c1))
            lasts.append(jnp.where(lo, l0, l1))
            cds.append(jnp.exp(jnp.where(sub < SSD_HEAD_DIM, l0, l1)))
            cg3.append(cg)
            bg3.append(bg)
    xd = xs3 * jnp.stack(dts)
    acum = jnp.stack(acs)
    y = (_bein("pls,psq->plq", jnp.stack(m_lo), jnp.where(lo[None], xd, 0.0))
         + _bein("pls,psq->plq", jnp.stack(m_hi), jnp.where(lo[None], 0.0, xd)))
    y = y + _bein("pln,pqn->plq", jnp.stack(cg3), h3) * jnp.exp(acum)
    st = _bein("plq,pln->pqn", xd * jnp.exp(jnp.stack(lasts) - acum), jnp.stack(bg3))
    h_out = h3 * jnp.stack(cds) + st
    return y, h_out


def _group_slabs(groups):
    pairs = [g * PAIRS_PER_GROUP + j for g in groups for j in range(PAIRS_PER_GROUP)]
    return [slice(p * LANES, (p + 1) * LANES) for p in pairs]


def _group_batches():
    return [tuple(range(g, g + SSD_GROUPS_PER_BATCH)) for g in range(0, SSD_GROUPS, SSD_GROUPS_PER_BATCH)]


def _bc_of(xc_ref, g):
    return (xc_ref[:, B_OFF + g * SSD_STATE:B_OFF + (g + 1) * SSD_STATE].astype(F32),
            xc_ref[:, C_OFF + g * SSD_STATE:C_OFF + (g + 1) * SSD_STATE].astype(F32))


def _ssd_in_specs(chunk_of):
    return [
        pl.BlockSpec((SSD_CHUNK, SSD_CONV_CH), lambda i: (chunk_of(i), 0)),
        pl.BlockSpec((SSD_CHUNK, HPAD), lambda i: (chunk_of(i), 0)),
        pl.BlockSpec((HPAD, SSD_CHUNK), lambda i: (0, chunk_of(i))),
        pl.BlockSpec((1, HPAD), lambda i: (0, 0)), pl.BlockSpec((HPAD, 1), lambda i: (0, 0)),
        pl.BlockSpec((1, HPAD), lambda i: (0, 0)), pl.BlockSpec((HPAD, 1), lambda i: (0, 0)),
    ]


def ssd_fwd(xc, dtr, dtr_t, dtb, dtb_t, alog, alog_t, name):
    def body(xc_ref, dtr_ref, dtrt_ref, dtb_ref, dtbt_ref, al_ref, alt_ref, y_ref, hs_ref, h_scr):
        @pl.when(pl.program_id(0) == 0)
        def _():
            h_scr[...] = jnp.zeros_like(h_scr)

        hs_ref[0] = h_scr[...]
        dt, a_cum, a_cum_t, a_last = _ssd_prefix(dtr_ref[...], dtrt_ref[...], dtb_ref[...], dtbt_ref[...],
                                                  al_ref[...], alt_ref[...])
        for groups in _group_batches():
            slabs = _group_slabs(groups)
            bgs, cgs = zip(*[_bc_of(xc_ref, g) for g in groups])
            xs3 = jnp.stack([xc_ref[:, sl] for sl in slabs]).astype(F32)
            h3 = jnp.stack([h_scr[sl, :] for sl in slabs])
            y3, h3_out = _ssd_group(xs3, bgs, cgs, h3, dt, a_cum, a_cum_t, a_last, groups=groups)
            for j, sl in enumerate(slabs):
                y_ref[:, sl] = y3[j].astype(y_ref.dtype)
                h_scr[sl, :] = h3_out[j]

    return pl.pallas_call(
        body, name=name, grid=(N_CHUNKS,), in_specs=_ssd_in_specs(lambda i: i),
        out_specs=[pl.BlockSpec((SSD_CHUNK, SSD_INNER), lambda i: (i, 0)),
                   pl.BlockSpec((1, SSD_INNER, SSD_STATE), lambda i: (i, 0, 0))],
        out_shape=[SDS((SEQ, SSD_INNER), BF16), SDS((N_CHUNKS, SSD_INNER, SSD_STATE), F32)],
        scratch_shapes=[pltpu.VMEM((SSD_INNER, SSD_STATE), F32)],
        compiler_params=_cparams(("arbitrary",)),
    )(xc, dtr, dtr_t, dtb, dtb_t, alog, alog_t)


def ssd_bwd(xc, dtr, dtr_t, dtb, dtb_t, alog, alog_t, hs, dy, dxs_extra, name):
    rev = lambda i: N_CHUNKS - 1 - i

    def body(xc_ref, dtr_ref, dtrt_ref, dtb_ref, dtbt_ref, al_ref, alt_ref, hs_ref, dy_ref, dxe_ref,
             dxc_ref, ddtr_ref, ddtrt_ref, ddtb_ref, ddtbt_ref, dal_ref, dalt_ref, dh_scr):
        @pl.when(pl.program_id(0) == 0)
        def _():
            dh_scr[...] = jnp.zeros_like(dh_scr)
            for r in (ddtb_ref, ddtbt_ref, dal_ref, dalt_ref):
                r[...] = jnp.zeros_like(r)

        prefix_in = (dtr_ref[...], dtrt_ref[...], dtb_ref[...], dtbt_ref[...], al_ref[...], alt_ref[...])
        (dt, a_cum, a_cum_t, a_last), prefix_vjp = jax.vjp(_ssd_prefix, *prefix_in)
        d_dt = jnp.zeros_like(dt)
        d_acum = jnp.zeros_like(a_cum)
        d_acum_t = jnp.zeros_like(a_cum_t)
        d_alast = jnp.zeros_like(a_last)
        for groups in _group_batches():
            slabs = _group_slabs(groups)
            bgs, cgs = zip(*[_bc_of(xc_ref, g) for g in groups])
            xs3 = jnp.stack([xc_ref[:, sl] for sl in slabs]).astype(F32)
            h3 = jnp.stack([hs_ref[0, sl, :] for sl in slabs])
            _, vjp = jax.vjp(functools.partial(_ssd_group, groups=groups), xs3, bgs, cgs, h3, dt, a_cum, a_cum_t, a_last)
            dy3 = jnp.stack([dy_ref[:, sl] for sl in slabs]).astype(F32)
            dh3 = jnp.stack([dh_scr[sl, :] for sl in slabs])
            dxs3, d_bgs, d_cgs, dh3_in, ddt, dac, dact, dal = vjp((dy3, dh3))
            for j, sl in enumerate(slabs):
                dxc_ref[:, sl] = (dxs3[j] + dxe_ref[:, sl].astype(F32)).astype(dxc_ref.dtype)
                dh_scr[sl, :] = dh3_in[j]
            d_dt, d_acum, d_acum_t, d_alast = d_dt + ddt, d_acum + dac, d_acum_t + dact, d_alast + dal
            for g, d_bg, d_cg in zip(groups, d_bgs, d_cgs):
                dxc_ref[:, B_OFF + g * SSD_STATE:B_OFF + (g + 1) * SSD_STATE] = d_bg.astype(dxc_ref.dtype)
                dxc_ref[:, C_OFF + g * SSD_STATE:C_OFF + (g + 1) * SSD_STATE] = d_cg.astype(dxc_ref.dtype)
        g_dtr, g_dtrt, g_dtb, g_dtbt, g_al, g_alt = prefix_vjp((d_dt, d_acum, d_acum_t, d_alast))
        ddtr_ref[...] = g_dtr
        ddtrt_ref[...] = g_dtrt
        ddtb_ref[...] += g_dtb
        ddtbt_ref[...] += g_dtbt
        dal_ref[...] += g_al
        dalt_ref[...] += g_alt

    in_specs = _ssd_in_specs(rev) + [
        pl.BlockSpec((1, SSD_INNER, SSD_STATE), lambda i: (rev(i), 0, 0)),
        pl.BlockSpec((SSD_CHUNK, SSD_INNER), lambda i: (rev(i), 0)),
        pl.BlockSpec((SSD_CHUNK, SSD_INNER), lambda i: (rev(i), 0)),
    ]
    out_specs = [
        pl.BlockSpec((SSD_CHUNK, SSD_CONV_CH), lambda i: (rev(i), 0)),
        pl.BlockSpec((SSD_CHUNK, HPAD), lambda i: (rev(i), 0)),
        pl.BlockSpec((HPAD, SSD_CHUNK), lambda i: (0, rev(i))),
        pl.BlockSpec((1, HPAD), lambda i: (0, 0)), pl.BlockSpec((HPAD, 1), lambda i: (0, 0)),
        pl.BlockSpec((1, HPAD), lambda i: (0, 0)), pl.BlockSpec((HPAD, 1), lambda i: (0, 0)),
    ]
    out_shape = [SDS((SEQ, SSD_CONV_CH), BF16), SDS((SEQ, HPAD), F32), SDS((HPAD, SEQ), F32),
                 SDS((1, HPAD), F32), SDS((HPAD, 1), F32), SDS((1, HPAD), F32), SDS((HPAD, 1), F32)]
    return pl.pallas_call(
        body, name=name, grid=(N_CHUNKS,), in_specs=in_specs, out_specs=out_specs, out_shape=out_shape,
        scratch_shapes=[pltpu.VMEM((SSD_INNER, SSD_STATE), F32)],
        compiler_params=_cparams(("arbitrary",)),
    )(xc, dtr, dtr_t, dtb, dtb_t, alog, alog_t, hs, dy, dxs_extra)


ATTN_SCALE = ATTN_HEAD_DIM ** -0.5


UNITS_PER_PATTERN = SEQ // ATTN_BLOCK
ATTN_BATCH_FWD = 8
ATTN_BATCH_BWD = 16


def _for_unit_batches(batch, per_trip):
    for g, d in enumerate(ATTN_DILATIONS):
        nb = UNITS_PER_PATTERN // d
        span = d * ATTN_BLOCK

        def trip(t, carry, g=g, d=d, nb=nb, span=span):
            units = []
            for j in range(per_trip):
                i = t * per_trip + j
                r = i >> (nb.bit_length() - 1)
                n = i & (nb - 1)
                start = r + n * span
                prev = jnp.where(n > 0, start - span, start)
                units.append((pl.ds(start, ATTN_BLOCK, stride=d), pl.ds(prev, ATTN_BLOCK, stride=d), n > 0))
            batch(g, units)
            return carry
        lax.fori_loop(0, UNITS_PER_PATTERN // per_trip, trip, 0)


def _unit_operands(units, q_scr, k_scr, v_scr):
    def pair(scr, rows, prows):
        return jnp.concatenate([scr[prows, :], scr[rows, :]], axis=0)
    qb = jnp.stack([q_scr[rows, :] for rows, _, _ in units]).astype(BF16)
    kb = jnp.stack([pair(k_scr, rows, prows) for rows, prows, _ in units]).astype(BF16)
    vb = jnp.stack([pair(v_scr, rows, prows) for rows, prows, _ in units]).astype(BF16)
    return qb, kb, vb


def _unit_scores(qb, kb, units):
    s = jnp.einsum("bqd,bkd->bqk", qb, kb, preferred_element_type=F32) * ATTN_SCALE
    qi = lax.broadcasted_iota(jnp.int32, (ATTN_BLOCK, 2 * ATTN_BLOCK), 0)
    kj = lax.broadcasted_iota(jnp.int32, (ATTN_BLOCK, 2 * ATTN_BLOCK), 1)
    own = (kj >= ATTN_BLOCK) & (kj - ATTN_BLOCK <= qi)
    before = (kj < ATTN_BLOCK) & (kj >= qi)
    keep = jnp.stack([own | (before & has_prev) for _, _, has_prev in units])
    return jnp.where(keep, s, -jnp.inf)


def _head_specs(n_q_groups):
    blk = (SEQ, ATTN_HEAD_DIM)
    q_specs = [pl.BlockSpec(blk, functools.partial(lambda h, g: (0, g * ATTN_KV_HEADS + h), g=g)) for g in range(n_q_groups)]
    head = pl.BlockSpec(blk, lambda h: (0, h))
    table = pl.BlockSpec(blk, lambda h: (0, 0))
    return q_specs, head, table


def attn_fwd(q, k, v, tabs, name):
    q_specs, head, table = _head_specs(ATTN_N_PAT)

    def body(q0_ref, q1_ref, q2_ref, k_ref, v_ref, c_ref, sa_ref, sb_ref, y_ref, lse_ref, *scr):
        qs, og, ls, ks, vs = scr[0:3], scr[3:6], scr[6:9], scr[9], scr[10]
        c, sa, sb = c_ref[...], sa_ref[...], sb_ref[...]
        for g, q_ref in enumerate((q0_ref, q1_ref, q2_ref)):
            qs[g][...] = _rope(q_ref[...].astype(F32), c, sa, sb)
        ks[...] = _rope(k_ref[...].astype(F32), c, sa, sb)
        vs[...] = v_ref[...].astype(F32)

        def batch(g, units):
            qb, kb, vb = _unit_operands(units, qs[g], ks, vs)
            s = _unit_scores(qb, kb, units)
            m = jnp.max(s, axis=2, keepdims=True)
            p = jnp.exp(s - m)
            l = jnp.sum(p, axis=2, keepdims=True)
            o = jnp.einsum("bqk,bkd->bqd", p.astype(BF16), vb, preferred_element_type=F32) / l
            lse_b = m + jnp.log(l)
            for j, (rows, _, _) in enumerate(units):
                og[g][rows, :] = o[j]
                ls[g][rows, :] = jnp.broadcast_to(lse_b[j], (ATTN_BLOCK, LANES))

        _for_unit_batches(batch, ATTN_BATCH_FWD)
        l0, l1, l2 = ls[0][...], ls[1][...], ls[2][...]
        m = jnp.maximum(jnp.maximum(l0, l1), l2)
        e0, e1, e2 = jnp.exp(l0 - m), jnp.exp(l1 - m), jnp.exp(l2 - m)
        den = e0 + e1 + e2
        y_ref[...] = ((e0 * og[0][...] + e1 * og[1][...] + e2 * og[2][...]) / den).astype(y_ref.dtype)
        lse_ref[...] = m + jnp.log(den)

    blk = (SEQ, ATTN_HEAD_DIM)
    return pl.pallas_call(
        body, name=name, grid=(ATTN_KV_HEADS,), in_specs=[*q_specs, head, head, table, table, table],
        out_specs=[head, head], out_shape=[SDS((SEQ, ATTN_OUT), BF16), SDS((SEQ, ATTN_OUT), F32)],
        scratch_shapes=[pltpu.VMEM(blk, F32)] * (3 * ATTN_N_PAT + 2),
        compiler_params=_cparams(("parallel",)),
    )(q, q, q, k, v, *tabs)


def attn_bwd(q, k, v, tabs, y, lse, dy, name):
    q_specs, head, table = _head_specs(ATTN_N_PAT)

    def body(q0_ref, q1_ref, q2_ref, k_ref, v_ref, c_ref, sa_ref, sb_ref, y_ref, lse_ref, dy_ref,
             dq0_ref, dq1_ref, dq2_ref, dk_ref, dv_ref, *scr):
        qs, dqs, ks, dks, dd, dvs, vs = scr[0:3], scr[3:6], scr[6], scr[7], scr[8], scr[9], scr[10]
        c, sa, sb = c_ref[...], sa_ref[...], sb_ref[...]
        for g, q_ref in enumerate((q0_ref, q1_ref, q2_ref)):
            qs[g][...] = _rope(q_ref[...].astype(F32), c, sa, sb)
        ks[...] = _rope(k_ref[...].astype(F32), c, sa, sb)
        vs[...] = v_ref[...].astype(F32)
        dks[...] = jnp.zeros_like(dks)
        dvs[...] = jnp.zeros_like(dvs)
        dyv = dy_ref[...]
        dd[...] = jnp.broadcast_to(jnp.sum(dyv * y_ref[...].astype(F32), axis=1, keepdims=True), dd.shape)

        def batch(g, units):
            qb, kb, vb = _unit_operands(units, qs[g], ks, vs)
            dob = jnp.stack([dy_ref[rows, :] for rows, _, _ in units]).astype(BF16)
            lse_b = jnp.stack([lse_ref[rows, :][:, 0:1] for rows, _, _ in units])
            dsum_b = jnp.stack([dd[rows, :][:, 0:1] for rows, _, _ in units])
            p = jnp.exp(_unit_scores(qb, kb, units) - lse_b)
            dp = jnp.einsum("bqd,bkd->bqk", dob, vb, preferred_element_type=F32)
            ds = (p * (dp - dsum_b) * ATTN_SCALE).astype(BF16)
            dq = jnp.einsum("bqk,bkd->bqd", ds, kb, preferred_element_type=F32)
            dk = jnp.einsum("bqk,bqd->bkd", ds, qb, preferred_element_type=F32)
            dv = jnp.einsum("bqk,bqd->bkd", p.astype(BF16), dob, preferred_element_type=F32)
            for j, (rows, prows, _) in enumerate(units):
                dqs[g][rows, :] = dq[j]
                dks[prows, :] += dk[j, :ATTN_BLOCK]
                dks[rows, :] += dk[j, ATTN_BLOCK:]
                dvs[prows, :] += dv[j, :ATTN_BLOCK]
                dvs[rows, :] += dv[j, ATTN_BLOCK:]

        _for_unit_batches(batch, ATTN_BATCH_BWD)
        for g, dq_ref in enumerate((dq0_ref, dq1_ref, dq2_ref)):
            dq_ref[...] = _rope(dqs[g][...], c, -sa, -sb).astype(dq_ref.dtype)
        dk_ref[...] = _rope(dks[...], c, -sa, -sb).astype(dk_ref.dtype)
        dv_ref[...] = dvs[...].astype(dv_ref.dtype)

    blk = (SEQ, ATTN_HEAD_DIM)
    out = SDS((SEQ, ATTN_OUT), BF16)
    return pl.pallas_call(
        body, name=name, grid=(ATTN_KV_HEADS,), in_specs=[*q_specs, head, head, table, table, table, head, head, head],
        out_specs=[head] * 5, out_shape=[out] * 5,
        scratch_shapes=[pltpu.VMEM(blk, F32)] * (2 * ATTN_N_PAT + 5),
        compiler_params=_cparams(("parallel",)),
    )(q, q, q, k, v, *tabs, y, lse, dy)


def layer_fwd(h, getw, prefetch, small, tabs, li):
    n = f"l{li}_"
    sv = {}
    w = dict(getw(0, h))
    u = rms_fwd(h, small["norm_mix"], n + "rms_mix")
    z = matmul(u, w["w_z"], name=n + "mm_z", tb=True, out_dtype=BF16)
    prefetch(1, z)
    xbc = matmul(u, w["w_xbc"], name=n + "mm_xbc", tb=True, out_dtype=BF16)
    dtr = matmul(u, w["w_dt"], name=n + "mm_dt", tb=True)
    q = matmul(u, w["w_q"], name=n + "mm_q", tb=True, out_dtype=BF16)
    k = matmul(u, w["w_k"], name=n + "mm_k", tb=True, out_dtype=BF16)
    v = matmul(u, w["w_v"], name=n + "mm_v", tb=True, out_dtype=BF16)
    gs = matmul(u, w["w_gs"], name=n + "mm_gs", tb=True, out_dtype=BF16)
    ga = matmul(u, w["w_ga"], name=n + "mm_ga", tb=True, out_dtype=BF16)
    xc = conv_fwd(xbc, w["conv_w"], small["conv_b"], n + "conv")
    dtr_t = dtr.T
    y_ssd, hs = ssd_fwd(xc, dtr, dtr_t, small["dt_bias"], small["dt_bias"].T, small["a_log"], small["a_log"].T, n + "ssd")
    yn = ssd_post_fwd(y_ssd, xc, z, small["d_skip_x"], small["ssd_norm"], n + "ssd_post")
    y_attn, lse = attn_fwd(q, k, v, tabs, n + "attn")
    w.update(getw(1, y_ssd))
    a = matmul(yn, w["w_ssd_branch"], name=n + "mm_a", out_dtype=BF16)
    b = matmul(y_attn, w["w_attn_branch"], name=n + "mm_b", out_dtype=BF16)
    merged = gate_fwd(a, b, gs, ga, n + "gate")
    h1 = matmul(merged, w["w_out"], name=n + "mm_o", add=h)
    w.update(getw(2, h1))
    u2 = rms_fwd(h1, small["norm_ffn"], n + "rms_ffn")
    gu = matmul(u2, w["w_gate_up"], name=n + "mm_gu", tb=True, out_dtype=BF16)
    act = swiglu_fwd(gu, n + "swiglu")
    h2 = matmul(act, w["w_down"], name=n + "mm_down", add=h1)
    sv.update(h=h, u=u, z=z, xbc=xbc, dtr=dtr, dtr_t=dtr_t, gs=gs, ga=ga, xc=xc, y_ssd=y_ssd, hs=hs, yn=yn,
              q=q, k=k, v=v, y_attn=y_attn, lse=lse, a=a, b=b, merged=merged, h1=h1, u2=u2, gu=gu, act=act, w=w)
    return h2, sv


def layer_bwd(dh, sv, small, tabs, li, emit):
    n = f"l{li}_b_"
    w = sv["w"]
    gw, gsm = {}, {}
    dact = matmul(dh, w["w_down"], name=n + "mm_dact", tb=True, out_dtype=BF16)
    gw["w_down"] = matmul(sv["act"], dh, name=n + "mm_dwdown", ta=True, out_dtype=BF16)
    dgu = swiglu_bwd(sv["gu"], dact, n + "swiglu")
    gw["w_gate_up"] = matmul(dgu, sv["u2"], name=n + "mm_dwgu", ta=True, out_dtype=BF16)
    tok = emit(2, gw)
    du2 = matmul(dgu, w["w_gate_up"], name=n + "mm_du2")
    dh1, gsm["norm_ffn"] = rms_bwd(sv["h1"], du2, dh, small["norm_ffn"] + tok, n + "rms_ffn")
    dmerged = matmul(dh1, w["w_out"], name=n + "mm_dmerged", tb=True)
    gw["w_out"] = matmul(sv["merged"], dh1, name=n + "mm_dwo", ta=True, out_dtype=BF16)
    da, db, dgs, dga = gate_bwd(sv["a"], sv["b"], sv["gs"], sv["ga"], dmerged, n + "gate")
    gw["w_ssd_branch"] = matmul(sv["yn"], da, name=n + "mm_dwa", ta=True, out_dtype=BF16)
    gw["w_attn_branch"] = matmul(sv["y_attn"], db, name=n + "mm_dwb", ta=True, out_dtype=BF16)
    tok = emit(1, gw)
    dyn = matmul(da, w["w_ssd_branch"], name=n + "mm_dyn", tb=True, out_dtype=BF16)
    dyattn = matmul(db, w["w_attn_branch"], name=n + "mm_dyattn", tb=True)
    dy_ssd, dxs_extra, dz, gsm["d_skip_x"], gsm["ssd_norm"] = ssd_post_bwd(
        sv["y_ssd"], sv["xc"], sv["z"], small["d_skip_x"] + tok, small["ssd_norm"], dyn, n + "ssd_post")
    dxc, ddtr, ddtr_t, ddtb, ddtb_t, dal, dal_t = ssd_bwd(
        sv["xc"], sv["dtr"], sv["dtr_t"], small["dt_bias"], small["dt_bias"].T, small["a_log"], small["a_log"].T,
        sv["hs"], dy_ssd, dxs_extra, n + "ssd")
    ddtr = (ddtr + ddtr_t.T).astype(BF16)
    gsm["dt_bias"] = ddtb + ddtb_t.T
    gsm["a_log"] = dal + dal_t.T
    dxbc, gw["conv_w"], gsm["conv_b"] = conv_bwd(sv["xbc"], w["conv_w"], small["conv_b"], dxc, n + "conv")
    dq0, dq1, dq2, dk, dv = attn_bwd(sv["q"], sv["k"], sv["v"], tabs, sv["y_attn"], sv["lse"], dyattn, n + "attn")
    u = sv["u"]
    segs = [("w_z", dz), ("w_xbc", dxbc), ("w_dt", ddtr), ("w_q0", dq0), ("w_q1", dq1), ("w_q2", dq2),
            ("w_k", dk), ("w_v", dv), ("w_gs", dgs), ("w_ga", dga)]
    gin = [matmul(dseg, u, name=n + "mm_d" + key, ta=True, out_dtype=BF16) for key, dseg in segs]
    gin[2] = gin[2][:SSD_HEADS]
    gw["w_in"] = jnp.concatenate(gin, axis=0)
    tok = emit(0, gw)
    du = jnp.zeros((SEQ, D_MODEL), F32) + tok
    for key, dseg in segs:
        du = matmul(dseg, w[key], name=n + "mm_du_" + key, add=du)
    dh0, gsm["norm_mix"] = rms_bwd(sv["h"], du, dh1, small["norm_mix"] + tok, n + "rms_mix")
    return dh0, gsm


def _my_place():
    return lax.axis_index("x"), lax.axis_index("y"), lax.axis_index("c")


def _flip(place, k):
    x, y, c = place
    return (1 - x if k & 4 else x, 1 - y if k & 2 else y, 1 - c if k & 1 else c)


def _index(place):
    return 4 * place[0] + 2 * place[1] + place[2]


ANY = pl.BlockSpec(memory_space=pl.ANY)
CHIP_FLIPS = (4, 2, 6)
SELF_AND_CHIPS = (0,) + CHIP_FLIPS


def all_gather(xs, name):
    na = len(xs)

    def body(*refs):
        x_refs, o_refs = refs[:na], refs[na:2 * na]
        send_sems, recv_sems, local_sems = refs[2 * na:]
        me = _my_place()
        sibling = _flip(me, 1)
        chips = [_flip(me, f) for f in CHIP_FLIPS]

        def copy(a, kk, block, to, src=None):
            dst = o_refs[a].at[_index(block)]
            return pltpu.make_async_remote_copy(
                src_ref=dst if src is None else src, dst_ref=dst, send_sem=send_sems.at[a, kk],
                recv_sem=recv_sems.at[a, kk], device_id=to, device_id_type=MESH)

        mine = [pltpu.make_async_copy(x_refs[a], o_refs[a].at[_index(me)], local_sems.at[a]) for a in range(na)]
        for cp in mine:
            cp.start()
        first = []
        for j, chip in enumerate(chips):
            first += [copy(a, 1 + j, me, chip, src=x_refs[a]) for a in range(na)]
        first += [copy(a, 0, me, sibling, src=x_refs[a]) for a in range(na)]
        for cp in first:
            cp.start()
        passed = []
        for j, chip in enumerate(chips):
            for a in range(na):
                copy(a, 1 + j, chip, me).wait_recv()
                cp = copy(a, 4 + j, chip, sibling)
                cp.start()
                passed.append(cp)
        for a in range(na):
            copy(a, 0, sibling, me).wait_recv()
        for j, chip in enumerate(chips):
            for a in range(na):
                copy(a, 4 + j, _flip(chip, 1), me).wait_recv()
        for cp in first + passed:
            cp.wait_send()
        for cp in mine:
            cp.wait()

    return pl.pallas_call(
        body, name=name, in_specs=[ANY] * na, out_specs=[ANY] * na,
        out_shape=[SDS((N_DEV,) + t.shape, t.dtype) for t in xs],
        scratch_shapes=[pltpu.SemaphoreType.DMA((na, N_DEV - 1)), pltpu.SemaphoreType.DMA((na, N_DEV - 1)),
                        pltpu.SemaphoreType.DMA((na,))],
    )(*xs)


HBM = pl.BlockSpec(memory_space=pltpu.HBM)
SEM = pl.BlockSpec(memory_space=pltpu.SEMAPHORE)
EFFECT = pltpu.SideEffectType.DATAFLOW_SIDE_EFFECTING
N_PEERS = N_DEV - 1


def _split_copy(src_ref, land_ref, send_sem, recv_sem, me, kk, scatter, landed_from_peer):
    peer = _flip(me, kk)
    src = src_ref.at[_index(peer)] if scatter else src_ref
    dst = land_ref.at[_index(peer if landed_from_peer else me)]
    return pltpu.make_async_remote_copy(src_ref=src, dst_ref=dst, send_sem=send_sem, recv_sem=recv_sem,
                                        device_id=peer, device_id_type=MESH)


ALL_PEERS = tuple(range(1, N_DEV))
EVERYONE = (0,) + ALL_PEERS


def exchange_start(srcs, lands, group_sizes, scatter, name, peers=ALL_PEERS):
    na, ng = len(srcs), len(group_sizes)

    def body(*refs):
        s_refs, l_refs = refs[:na], refs[na:2 * na]
        sems = refs[2 * na:2 * na + 2 * ng]
        token = refs[-1]
        me = _my_place()
        a = 0
        for gi, gsz in enumerate(group_sizes):
            for j in range(gsz):
                for pi, kk in enumerate(peers):
                    slot = j * len(peers) + pi
                    _split_copy(s_refs[a], l_refs[a], sems[2 * gi].at[slot], sems[2 * gi + 1].at[slot],
                                me, kk, scatter, False).start()
                a += 1
        token[...] = jnp.zeros_like(token)

    sem_shapes = []
    for gsz in group_sizes:
        sem_shapes += [pltpu.SemaphoreType.DMA((gsz * len(peers),))] * 2
    ins = [pltpu.with_memory_space_constraint(t, pltpu.HBM) for t in (*srcs, *lands)]
    res = pl.pallas_call(
        body, name=name, in_specs=[HBM] * (2 * na),
        out_specs=[SEM] * (2 * ng) + [HBM] * (2 * na) + [pl.BlockSpec(memory_space=pltpu.VMEM)],
        out_shape=sem_shapes + [pltpu.HBM(t.shape, t.dtype) for t in ins] + [SDS((8, LANES), F32)],
        input_output_aliases={i: 2 * ng + i for i in range(2 * na)},
        compiler_params=pltpu.CompilerParams(has_side_effects=EFFECT),
    )(*ins)
    sems = [(res[2 * gi], res[2 * gi + 1]) for gi in range(ng)]
    thru = res[2 * ng:2 * ng + 2 * na]
    return sems, thru[:na], thru[na:], res[-1]


def _wait_split_copies(s_refs, l_refs, send_sems, recv_sems, scatter, peers):
    me = _my_place()
    for j in range(len(s_refs)):
        for pi, kk in enumerate(peers):
            slot = j * len(peers) + pi
            cp = _split_copy(s_refs[j], l_refs[j], send_sems.at[slot], recv_sems.at[slot], me, kk, scatter, True)
            cp.wait_send()
            cp.wait_recv()


def exchange_wait(srcs, lands, sems, after, scatter, name, peers=ALL_PEERS):
    n = len(srcs)

    def body(*refs):
        s_refs, l_refs = refs[:n], refs[n:2 * n]
        _wait_split_copies(s_refs, l_refs, refs[2 * n], refs[2 * n + 1], scatter, peers)

    res = pl.pallas_call(
        body, name=name, in_specs=[HBM] * (2 * n) + [SEM, SEM, ANY], out_specs=[HBM] * (2 * n),
        out_shape=[pltpu.HBM(t.shape, t.dtype) for t in (*srcs, *lands)],
        input_output_aliases={i: i for i in range(2 * n)},
        compiler_params=pltpu.CompilerParams(has_side_effects=EFFECT),
    )(*srcs, *lands, sems[0], sems[1], after)
    return res[n:]


def _sibling_copies(l_refs, send_sems, recv_sems, arriving):
    me = _my_place()
    sibling = _flip(me, 1)
    held = [me] + [_flip(me, f) for f in CHIP_FLIPS]
    copies = []
    for j, land in enumerate(l_refs):
        for bi, place in enumerate(held):
            blk = land.at[_index(_flip(place, 1) if arriving else place)]
            slot = j * len(held) + bi
            copies.append(pltpu.make_async_remote_copy(src_ref=blk, dst_ref=blk, send_sem=send_sems.at[slot],
                                                       recv_sem=recv_sems.at[slot], device_id=sibling, device_id_type=MESH))
    return copies


def gather_forward(srcs, lands, sems, after, name):
    n = len(srcs)

    def body(*refs):
        s_refs, l_refs = refs[:n], refs[n:2 * n]
        _wait_split_copies(s_refs, l_refs, refs[2 * n], refs[2 * n + 1], False, SELF_AND_CHIPS)
        for cp in _sibling_copies(l_refs, refs[2 * n + 3], refs[2 * n + 4], False):
            cp.start()

    n_slots = n * (1 + len(CHIP_FLIPS))
    res = pl.pallas_call(
        body, name=name, in_specs=[HBM] * (2 * n) + [SEM, SEM, ANY],
        out_specs=[SEM, SEM] + [HBM] * (2 * n),
        out_shape=[pltpu.SemaphoreType.DMA((n_slots,))] * 2 + [pltpu.HBM(t.shape, t.dtype) for t in (*srcs, *lands)],
        input_output_aliases={i: 2 + i for i in range(2 * n)},
        compiler_params=pltpu.CompilerParams(has_side_effects=EFFECT),
    )(*srcs, *lands, sems[0], sems[1], after)
    return (res[0], res[1]), res[2 + n:]


def gather_finish(lands, sems, after, name):
    n = len(lands)

    def body(*refs):
        l_refs = refs[:n]
        for cp in _sibling_copies(l_refs, refs[n], refs[n + 1], True):
            cp.wait_send()
            cp.wait_recv()

    return pl.pallas_call(
        body, name=name, in_specs=[HBM] * n + [SEM, SEM, ANY], out_specs=[HBM] * n,
        out_shape=[pltpu.HBM(t.shape, t.dtype) for t in lands],
        input_output_aliases={i: i for i in range(n)},
        compiler_params=pltpu.CompilerParams(has_side_effects=EFFECT),
    )(*lands, sems[0], sems[1], after)


def landing_zone(block):
    return lax.empty((N_DEV,) + block.shape, block.dtype)


def sum_parts(parts, name, row_major_3d=False):
    _, r, c = parts.shape
    tc = _pick(c, (256, 128))

    def body(p_ref, o_ref):
        acc = p_ref[0].astype(F32)
        for i in range(1, N_DEV):
            acc = acc + p_ref[i].astype(F32)
        if row_major_3d:
            o_ref[:, 0, :] = acc
        else:
            o_ref[...] = acc

    out_spec = pl.BlockSpec((r, 1, tc), lambda i: (0, 0, i)) if row_major_3d else pl.BlockSpec((r, tc), lambda i: (0, i))
    return pl.pallas_call(
        body, name=name, grid=(c // tc,), in_specs=[pl.BlockSpec((N_DEV, r, tc), lambda i: (0, 0, i))],
        out_specs=out_spec, out_shape=SDS((r, 1, c) if row_major_3d else (r, c), F32),
        compiler_params=_cparams(("parallel",)),
    )(parts)


ADAMW_BLOCK_BYTES = 2 * 1024 * 1024


def adamw(w, g, m, v, name):
    shape = w.shape
    lay, rows, cols = ((1, 1) + shape)[-3:]
    tr = _pick(rows, (256, 128))
    tc = cols if tr * cols * 4 <= ADAMW_BLOCK_BYTES else _pick(cols, (256, 128))
    c1 = 1.0 / (1.0 - ADAM_B1 ** ADAM_STEP)
    c2 = 1.0 / (1.0 - ADAM_B2 ** ADAM_STEP)

    def body(w_ref, g_ref, m_ref, v_ref, d_ref, nm_ref, nv_ref):
        gg = g_ref[...]
        nm = ADAM_B1 * m_ref[...] + (1.0 - ADAM_B1) * gg
        nv = ADAM_B2 * v_ref[...] + (1.0 - ADAM_B2) * (gg * gg)
        d_ref[...] = -ADAM_LR * ((nm * c1) / (jnp.sqrt(nv * c2) + ADAM_EPS) + ADAM_WD * w_ref[...])
        nm_ref[...] = nm
        nv_ref[...] = nv

    spec = pl.BlockSpec((1, tr, tc), lambda l, i, j: (l, i, j))
    outs = pl.pallas_call(
        body, name=name, grid=(lay, rows // tr, cols // tc), in_specs=[spec] * 4, out_specs=[spec] * 3,
        out_shape=[SDS((lay, rows, cols), F32)] * 3, compiler_params=_cparams(("parallel",) * 3),
    )(*[t.reshape(lay, rows, cols) for t in (w, g, m, v)])
    return [o.reshape(shape) for o in outs]


def adamw_layer_inner(w, gs, m, v, name):
    rows, lay, cols = w.shape
    tr = _pick(rows, (256, 220, 128))
    c1 = 1.0 / (1.0 - ADAM_B1 ** ADAM_STEP)
    c2 = 1.0 / (1.0 - ADAM_B2 ** ADAM_STEP)

    def body(*refs):
        w_ref, m_ref, v_ref = refs[:3]
        g_refs = refs[3:3 + lay]
        go_ref, d_ref, nm_ref, nv_ref = refs[3 + lay:]
        for l, g_ref in enumerate(g_refs):
            gg = g_ref[:, 0, :]
            nm = ADAM_B1 * m_ref[:, l, :] + (1.0 - ADAM_B1) * gg
            nv = ADAM_B2 * v_ref[:, l, :] + (1.0 - ADAM_B2) * (gg * gg)
            d_ref[:, l, :] = -ADAM_LR * ((nm * c1) / (jnp.sqrt(nv * c2) + ADAM_EPS) + ADAM_WD * w_ref[:, l, :])
            go_ref[:, l, :] = gg
            nm_ref[:, l, :] = nm
            nv_ref[:, l, :] = nv

    inner = pl.BlockSpec((tr, lay, cols), lambda i: (i, 0, 0))
    plain = pl.BlockSpec((tr, 1, cols), lambda i: (i, 0, 0))
    return pl.pallas_call(
        body, name=name, grid=(rows // tr,), in_specs=[inner] * 3 + [plain] * lay, out_specs=[inner] * 4,
        out_shape=[SDS((rows, lay, cols), F32)] * 4, compiler_params=_cparams(("parallel",)),
    )(w, m, v, *gs)


BIG = ("w_in", "conv_w", "w_ssd_branch", "w_attn_branch", "w_out", "w_gate_up", "w_down")
TRANSPOSED = ("w_in", "w_gate_up")
SMALL = ("norm_mix", "conv_b", "dt_bias", "a_log", "d_skip", "ssd_norm", "norm_ffn")
SMALL_SIZE = {"norm_mix": 1024, "conv_b": 3072, "dt_bias": 32, "a_log": 32, "d_skip": 32, "ssd_norm": 2048, "norm_ffn": 1024}
FLAT_W = 512
SMALL_TOTAL = DEPTH * sum(SMALL_SIZE.values()) + D_MODEL + LANES
SMALL_ROWS = 32
assert SMALL_ROWS * FLAT_W >= SMALL_TOTAL


GROUPS = (("w_in", "conv_w"), ("w_ssd_branch", "w_attn_branch", "w_out"), ("w_gate_up", "w_down"))


def to_wire(k, shard):
    if k in TRANSPOSED:
        return shard.T.astype(BF16)
    return shard if k == "conv_w" else shard.astype(BF16)


def full_weights(k, g):
    if k == "conv_w":
        return {k: g.transpose(1, 0, 2).reshape(SSD_CONV, SSD_CONV_CH)}
    full = g.reshape(-1, g.shape[-1])
    if k != "w_in":
        return {k: full}
    w, off = {}, 0
    for nm, r in IN_ROWS:
        w[nm] = full[off:off + r]
        off += r
    w["w_q"] = full[sum(r for _, r in IN_ROWS[:3]):sum(r for _, r in IN_ROWS[:6])]
    w["w_dt"] = jnp.pad(w["w_dt"], ((0, HPAD - SSD_HEADS), (0, 0)))
    return w


def grads_to_wire(k, g):
    if k == "conv_w":
        return g.reshape(SSD_CONV, N_DEV, SSD_CONV_CH // N_DEV).transpose(1, 0, 2)
    return g.reshape(N_DEV, g.shape[0] // N_DEV, g.shape[1])


def _pad_heads(t):
    return jnp.pad(t.reshape(1, SSD_HEADS), ((0, 0), (0, HPAD - SSD_HEADS)))


def local_step(x, target, getw, prefetch, emit, smalls, norm_final):
    tabs = rope_tables()
    sms = []
    for li in range(DEPTH):
        s = smalls[li]
        sms.append({
            "norm_mix": s["norm_mix"].reshape(1, -1), "conv_b": s["conv_b"].reshape(1, -1),
            "dt_bias": _pad_heads(s["dt_bias"]), "a_log": _pad_heads(s["a_log"]),
            "d_skip_x": jnp.repeat(s["d_skip"], SSD_HEAD_DIM).reshape(1, -1),
            "ssd_norm": s["ssd_norm"].reshape(1, -1), "norm_ffn": s["norm_ffn"].reshape(1, -1)})
    h = x
    saved = []
    for li in range(DEPTH):
        h, sv = layer_fwd(h, functools.partial(getw, li), functools.partial(prefetch, li), sms[li], tabs, li)
        saved.append(sv)
    dh, g_final, loss = loss_head(h, target, norm_final.reshape(1, -1), "loss_head")
    gsms = [None] * DEPTH
    for li in reversed(range(DEPTH)):
        dh, gsm = layer_bwd(dh, saved[li], sms[li], tabs, li, functools.partial(emit, li))
        gsms[li] = {
            "norm_mix": gsm["norm_mix"].reshape(-1), "conv_b": gsm["conv_b"].reshape(-1),
            "dt_bias": gsm["dt_bias"][0, :SSD_HEADS], "a_log": gsm["a_log"][0, :SSD_HEADS],
            "d_skip": gsm["d_skip_x"].reshape(SSD_HEADS, SSD_HEAD_DIM).sum(axis=1),
            "ssd_norm": gsm["ssd_norm"].reshape(-1), "norm_ffn": gsm["norm_ffn"].reshape(-1)}
    return loss, dh, gsms, g_final.reshape(-1)


def kernel(x, norm_mix, w_in, conv_w, conv_b, dt_bias, a_log, d_skip, ssd_norm, w_ssd_branch, w_attn_branch, w_out, norm_ffn, w_gate_up, w_down, norm_final, loss_target, m_norm_mix, m_w_in, m_conv_w, m_conv_b, m_dt_bias, m_a_log, m_d_skip, m_ssd_norm, m_w_ssd_branch, m_w_attn_branch, m_w_out, m_norm_ffn, m_w_gate_up, m_w_down, m_norm_final, v_norm_mix, v_w_in, v_conv_w, v_conv_b, v_dt_bias, v_a_log, v_d_skip, v_ssd_norm, v_w_ssd_branch, v_w_attn_branch, v_w_out, v_norm_ffn, v_w_gate_up, v_w_down, v_norm_final):
    wv = dict(norm_mix=norm_mix, w_in=w_in, conv_w=conv_w, conv_b=conv_b, dt_bias=dt_bias, a_log=a_log, d_skip=d_skip,
              ssd_norm=ssd_norm, w_ssd_branch=w_ssd_branch, w_attn_branch=w_attn_branch, w_out=w_out, norm_ffn=norm_ffn,
              w_gate_up=w_gate_up, w_down=w_down, norm_final=norm_final)
    mv = dict(norm_mix=m_norm_mix, w_in=m_w_in, conv_w=m_conv_w, conv_b=m_conv_b, dt_bias=m_dt_bias, a_log=m_a_log,
              d_skip=m_d_skip, ssd_norm=m_ssd_norm, w_ssd_branch=m_w_ssd_branch, w_attn_branch=m_w_attn_branch,
              w_out=m_w_out, norm_ffn=m_norm_ffn, w_gate_up=m_w_gate_up, w_down=m_w_down, norm_final=m_norm_final)
    vv = dict(norm_mix=v_norm_mix, w_in=v_w_in, conv_w=v_conv_w, conv_b=v_conv_b, dt_bias=v_dt_bias, a_log=v_a_log,
              d_skip=v_d_skip, ssd_norm=v_ssd_norm, w_ssd_branch=v_w_ssd_branch, w_attn_branch=v_w_attn_branch,
              w_out=v_w_out, norm_ffn=v_norm_ffn, w_gate_up=v_w_gate_up, w_down=v_w_down, norm_final=v_norm_final)
    order = ("norm_mix", "w_in", "conv_w", "conv_b", "dt_bias", "a_log", "d_skip", "ssd_norm", "w_ssd_branch",
             "w_attn_branch", "w_out", "norm_ffn", "w_gate_up", "w_down", "norm_final")

    smalls = [{k: wv[k][li] for k in SMALL} for li in range(DEPTH)]
    n_groups = len(GROUPS)

    first_lands = all_gather([to_wire(k, wv[k][0]) for k in GROUPS[0]], "gather_first")
    later = [(li, gi) for li in range(DEPTH) for gi in range(n_groups)][1:]
    behind_first = first_lands[1][0, 0, 0] * 0.0
    srcs = [to_wire(k, wv[k][li] + behind_first if k == "conv_w" else wv[k][li]) for li, gi in later for k in GROUPS[gi]]
    sizes = [len(GROUPS[gi]) for _, gi in later]
    w_sems, w_srcs, w_lands, token = exchange_start(srcs, [landing_zone(s) for s in srcs], sizes, False,
                                                    "gather_start", peers=SELF_AND_CHIPS)
    smalls[0]["norm_mix"] = smalls[0]["norm_mix"] + token[0, 0]
    second_leg = {}

    def forward(slot, after):
        if slot < len(later) and slot not in second_leg:
            sl = slice(sum(sizes[:slot]), sum(sizes[:slot + 1]))
            second_leg[slot] = gather_forward(w_srcs[sl], w_lands[sl], w_sems[slot], after, f"gather_forward_{slot}")

    def prefetch(li, gi, after):
        if (li, gi) == later[0]:
            forward(0, after)

    def getw(li, gi, after):
        if (li, gi) == (0, 0):
            lands = first_lands
        else:
            slot = later.index((li, gi))
            forward(slot, after)
            sems2, lands2 = second_leg[slot]
            lands = gather_finish(lands2, sems2, after, f"gather_finish_{li}_{gi}")
            forward(slot + 1, lands[0])
        w = {}
        for k, land in zip(GROUPS[gi], lands):
            w.update(full_weights(k, land))
        return w

    pending = []

    def emit(li, gi, gw):
        parts = [grads_to_wire(k, gw[k]) for k in GROUPS[gi]]
        lands = [landing_zone(p[0]) for p in parts]
        sems, p_thru, l_thru, tok = exchange_start(parts, lands, [len(parts)], True, f"grads_start_{li}_{gi}", peers=EVERYONE)
        pending.append((li, gi, sems[0], p_thru, l_thru))
        return tok[0, 0]

    loss_p, dx, gsms, g_final = local_step(x[0], loss_target[0], getw, prefetch, emit, smalls, norm_final)

    grads, deltas, new_m, new_v = {}, {}, {}, {}

    def update(k):
        if k == "w_in":
            inner = lambda t: t.transpose(2, 0, 1)
            outs = adamw_layer_inner(inner(wv[k]), shard_g[k], inner(mv[k]), inner(vv[k]), "adamw_" + k)
            grads[k], deltas[k], new_m[k], new_v[k] = (t.transpose(1, 2, 0) for t in outs)
            return outs[3]
        if k in BIG:
            grads[k] = jnp.stack([g.T if k in TRANSPOSED else g for g in shard_g[k]])
        deltas[k], new_m[k], new_v[k] = adamw(wv[k], grads[k], mv[k], vv[k], "adamw_" + k)
        return new_v[k]

    shard_g = {k: [None] * DEPTH for k in BIG}

    def collect(entry, after):
        li, gi, sems, p_thru, l_thru = entry
        recv = exchange_wait(p_thru, l_thru, sems, after, True, f"grads_wait_{li}_{gi}", peers=EVERYONE)
        for k, r in zip(GROUPS[gi], recv):
            if k == "conv_w":
                r = r.reshape(N_DEV, 1, -1)
            after = sum_parts(r, f"sum_{k}_{li}", row_major_3d=(k == "w_in"))
            shard_g[k][li] = after if k in TRANSPOSED else after.reshape(wv[k].shape[1:])
        return after

    after = dx
    for entry in pending[:-1]:
        after = collect(entry, after)
    done = [after[:1, :1].reshape(1)]
    for gi in (2, 1):
        for k in GROUPS[gi]:
            done.append(update(k).reshape(-1)[:1])

    flat = [gsms[li][k] for li in range(DEPTH) for k in SMALL] + [g_final, loss_p.reshape(-1)]
    flat.append(jnp.zeros((SMALL_ROWS * FLAT_W - SMALL_TOTAL,), F32))
    small_all = all_gather([jnp.concatenate(flat).reshape(SMALL_ROWS, FLAT_W)], "gather_small")[0]
    small_sum = sum_parts(small_all, "sum_small").reshape(-1)
    off = 0
    per_layer = {k: [] for k in SMALL}
    for li in range(DEPTH):
        for k in SMALL:
            per_layer[k].append(small_sum[off:off + SMALL_SIZE[k]])
            off += SMALL_SIZE[k]
    for k in SMALL:
        grads[k] = jnp.stack(per_layer[k])
    grads["norm_final"] = small_sum[off:off + D_MODEL]
    loss = small_sum[off + D_MODEL]
    for k in (*SMALL, "norm_final"):
        done.append(update(k).reshape(-1)[:1])

    collect(pending[-1], jnp.concatenate(done))
    for k in GROUPS[0]:
        update(k)

    return (loss, dx.reshape(x.shape), *[grads[k] for k in order], *[deltas[k] for k in order],
            *[new_m[k] for k in order], *[new_v[k] for k in order])
```

```python
import functools

import jax
import jax.numpy as jnp
from jax import lax
from jax.experimental import pallas as pl
from jax.experimental.pallas import tpu as pltpu

F32, BF16 = jnp.float32, jnp.bfloat16
SDS = jax.ShapeDtypeStruct
MESH = pl.DeviceIdType.MESH

D_MODEL = 1024
SEQ = 2048
DEPTH = 2
RMS_EPS = 1e-5
SSD_INNER = 2048
SSD_HEAD_DIM = 64
SSD_HEADS = 32
SSD_STATE = 128
SSD_GROUPS = 4
SSD_CONV = 4
SSD_CHUNK = 128
SSD_CONV_CH = 3072
ATTN_HEAD_DIM = 128
ATTN_KV_HEADS = 8
ATTN_DILATIONS = (1, 4, 16)
ATTN_N_PAT = 3
ATTN_BLOCK = 128
ATTN_OUT = 1024
ROPE_THETA = 500000.0
ROPE_DIM = 32
FFN_HIDDEN = 2816
ADAM_LR, ADAM_B1, ADAM_B2, ADAM_EPS, ADAM_WD, ADAM_STEP = 0.001, 0.9, 0.999, 1e-08, 0.01, 10

N_DEV = 8
LANES = 128
VMEM_LIMIT = 56 * 1024 * 1024
HPAD = 128
HIGHEST = lax.Precision.HIGHEST

IN_ROWS = (("w_z", 2048), ("w_xbc", 3072), ("w_dt", 32), ("w_q0", 1024), ("w_q1", 1024), ("w_q2", 1024),
           ("w_k", 1024), ("w_v", 1024), ("w_gs", 1024), ("w_ga", 1024))
N_IN = sum(r for _, r in IN_ROWS)


def _cparams(sem):
    return pltpu.CompilerParams(dimension_semantics=sem, vmem_limit_bytes=VMEM_LIMIT)


def _sigmoid(x):
    return 0.5 * jnp.tanh(0.5 * x) + 0.5


def _silu(x):
    return x * _sigmoid(x)


def _softplus(x):
    return jnp.maximum(x, 0.0) + jnp.log(1.0 + jnp.exp(-jnp.abs(x)))


def _dot(a, b, dims=(((1,), (0,)), ((), ())), precision=None):
    return lax.dot_general(a, b, dims, precision=precision, preferred_element_type=F32)


NT = (((1,), (1,)), ((), ()))
TN = (((0,), (0,)), ((), ()))


def _bdot(a, b, dims=(((1,), (0,)), ((), ()))):
    return _dot(a.astype(BF16), b.astype(BF16), dims)


def _pick(dim, cands):
    for c in cands:
        if dim % c == 0:
            return c
    return dim


WHOLE_K_BUDGET = 40 * 1024 * 1024
RESIDENT_B_BYTES = 12 * 1024 * 1024
OUT_TILE_BYTES = 6 * 1024 * 1024


def matmul(a, b, *, name, ta=False, tb=False, out_dtype=F32, add=None):
    m, k = (a.shape[1], a.shape[0]) if ta else a.shape
    n = b.shape[0] if tb else b.shape[1]
    out_bytes = jnp.dtype(out_dtype).itemsize + (4 if add is not None else 0)
    if k * n * b.dtype.itemsize <= RESIDENT_B_BYTES:
        tn = n
        tm = next(t for t in (512, 256, 128) if m % t == 0 and t * n * out_bytes <= OUT_TILE_BYTES)
    else:
        tn = _pick(n, (1024, 1408, 512, 256, 128))
        tm = _pick(m, (512, 1408, 256, 128)) if tn == n else _pick(m, (1024, 1408, 512, 256, 128))
    tk = _pick(k, (2048, 1024, 1408, 512, 256, 128))
    whole_k_bytes = 2 * (tm * k * a.dtype.itemsize + k * tn * b.dtype.itemsize)
    if tn == n and whole_k_bytes <= WHOLE_K_BUDGET:
        tk = k
    nk = k // tk
    a_spec = pl.BlockSpec((tk, tm), lambda i, j, kk: (kk, i)) if ta else pl.BlockSpec((tm, tk), lambda i, j, kk: (i, kk))
    b_spec = pl.BlockSpec((tn, tk), lambda i, j, kk: (j, kk)) if tb else pl.BlockSpec((tk, tn), lambda i, j, kk: (kk, j))
    dims = (((0 if ta else 1,), (1 if tb else 0,)), ((), ()))
    has_add = add is not None

    def body(*refs):
        a_ref, b_ref = refs[:2]
        add_ref = refs[2] if has_add else None
        o_ref = refs[3] if has_add else refs[2]
        acc = refs[-1] if nk > 1 else None
        kk = pl.program_id(2)

        def product():
            return _dot(a_ref[...].astype(BF16), b_ref[...].astype(BF16), dims)

        def finish(r):
            if has_add:
                r = r + add_ref[...].astype(F32)
            o_ref[...] = r.astype(o_ref.dtype)

        if nk == 1:
            finish(product())
            return

        @pl.when(kk == 0)
        def _():
            acc[...] = product()

        @pl.when((kk > 0) & (kk < nk - 1))
        def _():
            acc[...] += product()

        @pl.when(kk == nk - 1)
        def _():
            finish(acc[...] + product())

    in_specs = [a_spec, b_spec]
    args = [a, b]
    if has_add:
        in_specs.append(pl.BlockSpec((tm, tn), lambda i, j, kk: (i, j)))
        args.append(add)
    return pl.pallas_call(
        body, name=name, grid=(m // tm, n // tn, nk),
        in_specs=in_specs, out_specs=pl.BlockSpec((tm, tn), lambda i, j, kk: (i, j)),
        out_shape=SDS((m, n), out_dtype), scratch_shapes=[pltpu.VMEM((tm, tn), F32)] if nk > 1 else [],
        compiler_params=_cparams(("parallel", "parallel", "arbitrary")),
    )(*args)


def matmul_rows(a, b, post, extras, outs, *, name, tb=False, tm=256):
    m, k = a.shape
    dims = NT if tb else (((1,), (0,)), ((), ()))
    ne = len(extras)

    def body(*refs):
        a_ref, b_ref = refs[:2]
        e_refs, o_refs = refs[2:2 + ne], refs[2 + ne:]
        res = post(_dot(a_ref[...].astype(BF16), b_ref[...].astype(BF16), dims), *[r[...] for r in e_refs])
        for r, val in zip(o_refs, res):
            r[...] = val.astype(r.dtype)

    row = lambda width: pl.BlockSpec((tm, width), lambda i: (i, 0))
    return pl.pallas_call(
        body, name=name, grid=(m // tm,),
        in_specs=[row(k), pl.BlockSpec(b.shape, lambda i: (0, 0))] + [row(e.shape[1]) for e in extras],
        out_specs=[row(c) for c, _ in outs], out_shape=[SDS((m, c), dt) for c, dt in outs],
        compiler_params=_cparams(("parallel",)),
    )(a, b, *extras)


def rowcall(name, fn, rows, params, row_outs, red_outs=(), tr=256):
    s = rows[0].shape[0]
    n_in = len(rows) + len(params)
    n_row = len(row_outs)

    def body(*refs):
        outs = fn(*[r[...].astype(F32) for r in refs[:n_in]])
        if not isinstance(outs, (tuple, list)):
            outs = (outs,)
        orefs = refs[n_in:]
        for r, o in zip(orefs[:n_row], outs[:n_row]):
            r[...] = o.astype(r.dtype)
        if red_outs:
            @pl.when(pl.program_id(0) == 0)
            def _():
                for r in orefs[n_row:]:
                    r[...] = jnp.zeros_like(r)
            for r, o in zip(orefs[n_row:], outs[n_row:]):
                r[...] += o.astype(F32)

    widths = [a[1] if isinstance(a, tuple) else a.shape[1] for a in rows]
    rows = [a[0] if isinstance(a, tuple) else a for a in rows]
    in_specs = [pl.BlockSpec((tr, wd), lambda i: (i, 0)) for wd in widths]
    in_specs += [pl.BlockSpec(p.shape, lambda i: (0, 0)) for p in params]
    out_specs = [pl.BlockSpec((tr, c), lambda i: (i, 0)) for c, _ in row_outs]
    out_specs += [pl.BlockSpec(shp, lambda i: (0, 0)) for shp in red_outs]
    out_shape = [SDS((s, c), dt) for c, dt in row_outs] + [SDS(shp, F32) for shp in red_outs]
    res = pl.pallas_call(
        body, name=name, grid=(s // tr,), in_specs=in_specs, out_specs=out_specs, out_shape=out_shape,
        compiler_params=_cparams(("arbitrary",) if red_outs else ("parallel",)),
    )(*rows, *params)
    return res


def _rms(x, w):
    return x * lax.rsqrt(jnp.mean(x * x, axis=-1, keepdims=True) + RMS_EPS) * w


def rms_fwd(h, w, name):
    return rowcall(name, _rms, [h], [w], [(D_MODEL, BF16)])[0]


def rms_bwd(h, du, dres, w, name):
    def fn(hb, dub, dresb, wb):
        _, vjp = jax.vjp(_rms, hb, wb)
        dh, dw = vjp(dub)
        return dh + dresb, dw
    return rowcall(name, fn, [h, du, dres], [w], [(D_MODEL, F32)], [(1, D_MODEL)])


def loss_head(h, target, w, name):
    def fn(hb, tb, wb):
        def f(hh, ww):
            err = _rms(hh, ww) - tb
            return 0.5 * jnp.sum(jnp.mean(err * err, axis=-1, keepdims=True), axis=0, keepdims=True)
        val, vjp = jax.vjp(f, hb, wb)
        dh, dw = vjp(jnp.ones((1, 1), F32))
        return dh, dw, jnp.broadcast_to(val, (1, LANES))
    return rowcall(name, fn, [h, target], [w], [(D_MODEL, F32)], [(1, D_MODEL), (1, LANES)])


def _gate(a, b, gs, ga):
    return _sigmoid(gs) * a + _sigmoid(ga) * b


def gate_fwd(a, b, gs, ga, name):
    return rowcall(name, _gate, [a, b, gs, ga], [], [(D_MODEL, BF16)])[0]


def gate_bwd(a, b, gs, ga, dm, name):
    def fn(ab, bb, gsb, gab, dmb):
        _, vjp = jax.vjp(_gate, ab, bb, gsb, gab)
        return vjp(dmb)
    return rowcall(name, fn, [a, b, gs, ga, dm], [], [(D_MODEL, BF16)] * 4)


def _swiglu(gu):
    return _silu(gu[:, :FFN_HIDDEN]) * gu[:, FFN_HIDDEN:]


def gate_up_fwd(u2, w_gate_up_t, name):
    def post(acc):
        gu = acc.astype(BF16)
        return gu, _swiglu(gu.astype(F32))
    return matmul_rows(u2, w_gate_up_t, post, [], [(2 * FFN_HIDDEN, BF16), (FFN_HIDDEN, BF16)], name=name, tb=True)


def gate_up_bwd(dh, w_down, gu, name):
    def post(acc, gub):
        _, vjp = jax.vjp(_swiglu, gub.astype(F32))
        return vjp(acc.astype(BF16).astype(F32))
    return matmul_rows(dh, w_down, post, [gu], [(2 * FFN_HIDDEN, BF16)], name=name, tb=True)[0]


def _ssd_post(y, xs, z, dskip, normw):
    y = (y + dskip * xs) * _silu(z)
    gw = SSD_INNER // SSD_GROUPS
    parts = []
    for g in range(SSD_GROUPS):
        yg = y[:, g * gw:(g + 1) * gw]
        parts.append(yg * lax.rsqrt(jnp.mean(yg * yg, axis=-1, keepdims=True) + RMS_EPS))
    return jnp.concatenate(parts, axis=-1) * normw


def ssd_post_fwd(y, xc, z, dskip, normw, name):
    return rowcall(name, _ssd_post, [y, (xc, SSD_INNER), z], [dskip, normw], [(SSD_INNER, BF16)])[0]


def ssd_post_bwd(y, xc, z, dskip, normw, dyn, name):
    def fn(yb, xsb, zb, dynb, db, nb):
        _, vjp = jax.vjp(_ssd_post, yb, xsb, zb, db, nb)
        return vjp(dynb)
    return rowcall(name, fn, [y, (xc, SSD_INNER), z, dyn], [dskip, normw],
                   [(SSD_INNER, BF16)] * 3, [(1, SSD_INNER), (1, SSD_INNER)])


def _rope(t, cosf, sina, sinb):
    return t * cosf + pltpu.roll(t, LANES - ROPE_DIM // 2, 1) * sina + pltpu.roll(t, ROPE_DIM // 2, 1) * sinb


def rope_tables():
    half = ROPE_DIM // 2
    inv = ROPE_THETA ** (-jnp.arange(0, ROPE_DIM, 2, dtype=F32) / ROPE_DIM)
    ang = jnp.arange(SEQ, dtype=F32)[:, None] * inv[None, :]
    cos, sin = jnp.cos(ang), jnp.sin(ang)
    zeros = jnp.zeros((SEQ, LANES - ROPE_DIM), F32)
    z16 = jnp.zeros((SEQ, half), F32)
    cosf = jnp.concatenate([cos, cos, jnp.ones((SEQ, LANES - ROPE_DIM), F32)], axis=1)
    sina = jnp.concatenate([-sin, z16, zeros], axis=1)
    sinb = jnp.concatenate([z16, sin, zeros], axis=1)
    return cosf, sina, sinb


CONV_TC = 256


def _conv_pre(x, w, b, row):
    acc = x * w[SSD_CONV - 1:SSD_CONV, :] + b
    shifted = [x]
    for j in range(1, SSD_CONV):
        xs = jnp.where(row >= j, pltpu.roll(x, j, 0), 0.0)
        shifted.append(xs)
        acc = acc + xs * w[SSD_CONV - 1 - j:SSD_CONV - j, :]
    return acc, shifted


def conv_fwd(xbc, w, b, name):
    def body(x_ref, w_ref, b_ref, o_ref):
        row = lax.broadcasted_iota(jnp.int32, (SEQ, CONV_TC), 0)
        pre, _ = _conv_pre(x_ref[...].astype(F32), w_ref[...], b_ref[...], row)
        o_ref[...] = _silu(pre).astype(o_ref.dtype)
    return pl.pallas_call(
        body, name=name, grid=(SSD_CONV_CH // CONV_TC,),
        in_specs=[pl.BlockSpec((SEQ, CONV_TC), lambda i: (0, i)), pl.BlockSpec((SSD_CONV, CONV_TC), lambda i: (0, i)),
                  pl.BlockSpec((1, CONV_TC), lambda i: (0, i))],
        out_specs=pl.BlockSpec((SEQ, CONV_TC), lambda i: (0, i)),
        out_shape=SDS((SEQ, SSD_CONV_CH), BF16), compiler_params=_cparams(("parallel",)),
    )(xbc, w, b)


def conv_bwd(xbc, w, b, dxc, name):
    def body(x_ref, w_ref, b_ref, dy_ref, dx_ref, dw_ref, db_ref):
        row = lax.broadcasted_iota(jnp.int32, (SEQ, CONV_TC), 0)
        wv = w_ref[...]
        pre, shifted = _conv_pre(x_ref[...].astype(F32), wv, b_ref[...], row)
        sg = _sigmoid(pre)
        ds = dy_ref[...].astype(F32) * (sg * (1.0 + pre * (1.0 - sg)))
        dx = ds * wv[SSD_CONV - 1:SSD_CONV, :]
        for j in range(1, SSD_CONV):
            dsj = jnp.where(row < SEQ - j, pltpu.roll(ds, SEQ - j, 0), 0.0)
            dx = dx + dsj * wv[SSD_CONV - 1 - j:SSD_CONV - j, :]
        dx_ref[...] = dx.astype(dx_ref.dtype)
        for j in range(SSD_CONV):
            dw_ref[SSD_CONV - 1 - j:SSD_CONV - j, :] = jnp.sum(ds * shifted[j], axis=0, keepdims=True)
        db_ref[...] = jnp.sum(ds, axis=0, keepdims=True)
    return pl.pallas_call(
        body, name=name, grid=(SSD_CONV_CH // CONV_TC,),
        in_specs=[pl.BlockSpec((SEQ, CONV_TC), lambda i: (0, i)), pl.BlockSpec((SSD_CONV, CONV_TC), lambda i: (0, i)),
                  pl.BlockSpec((1, CONV_TC), lambda i: (0, i)), pl.BlockSpec((SEQ, CONV_TC), lambda i: (0, i))],
        out_specs=[pl.BlockSpec((SEQ, CONV_TC), lambda i: (0, i)), pl.BlockSpec((SSD_CONV, CONV_TC), lambda i: (0, i)),
                   pl.BlockSpec((1, CONV_TC), lambda i: (0, i))],
        out_shape=[SDS((SEQ, SSD_CONV_CH), BF16), SDS((SSD_CONV, SSD_CONV_CH), F32), SDS((1, SSD_CONV_CH), F32)],
        compiler_params=_cparams(("parallel",)),
    )(xbc, w, b, dxc)


N_CHUNKS = SEQ // SSD_CHUNK
N_PAIRS = SSD_HEADS // 2
PAIRS_PER_GROUP = N_PAIRS // SSD_GROUPS
B_OFF = SSD_INNER
C_OFF = SSD_INNER + SSD_GROUPS * SSD_STATE


def _ssd_prefix(dtr, dtr_t, dtb, dtb_t, alog, alog_t):
    ln = SSD_CHUNK
    dt = _softplus(dtr + dtb)
    dt_t = _softplus(dtr_t + dtb_t)
    dta = dt * (-jnp.exp(alog))
    dta_t = dt_t * (-jnp.exp(alog_t))
    r = lax.broadcasted_iota(jnp.int32, (ln, ln), 0)
    c = lax.broadcasted_iota(jnp.int32, (ln, ln), 1)
    a_cum = _dot((r >= c).astype(F32), dta, precision=HIGHEST)
    a_cum_t = _dot(dta_t, (r <= c).astype(F32), precision=HIGHEST)
    a_last = jnp.sum(dta_t, axis=1, keepdims=True)
    return dt, a_cum, a_cum_t, a_last


def _bein(spec, a, b):
    return jnp.einsum(spec, a.astype(BF16), b.astype(BF16), preferred_element_type=F32)


SSD_GROUPS_PER_BATCH = 4


def _ssd_group(xs3, bgs, cgs, h3, dt, a_cum, a_cum_t, a_last, *, groups):
    ln = SSD_CHUNK
    lane = lax.broadcasted_iota(jnp.int32, (ln, LANES), 1)
    sub = lax.broadcasted_iota(jnp.int32, (LANES, SSD_STATE), 0)
    row = lax.broadcasted_iota(jnp.int32, (ln, ln), 0)
    col = lax.broadcasted_iota(jnp.int32, (ln, ln), 1)
    lo = lane < SSD_HEAD_DIM
    causal = row >= col
    m_lo, m_hi, dts, acs, lasts, cds, cg3, bg3 = [], [], [], [], [], [], [], []
    for g, bg, cg in zip(groups, bgs, cgs):
        cb = _bdot(cg, bg, NT)
        for j in range(PAIRS_PER_GROUP):
            e0 = 2 * (g * PAIRS_PER_GROUP + j)
            e1 = e0 + 1
            c0, c1 = a_cum[:, e0:e0 + 1], a_cum[:, e1:e1 + 1]
            r0, r1 = a_cum_t[e0:e0 + 1, :], a_cum_t[e1:e1 + 1, :]
            l0, l1 = a_last[e0:e0 + 1, :], a_last[e1:e1 + 1, :]
            m_lo.append(cb * jnp.exp(jnp.where(causal, c0 - r0, -jnp.inf)))
            m_hi.append(cb * jnp.exp(jnp.where(causal, c1 - r1, -jnp.inf)))
            dts.append(jnp.where(lo, dt[:, e0:e0 + 1], dt[:, e1:e1 + 1]))
            acs.append(jnp.where(lo, c0, c1))
            lasts.append(jnp.where(lo, l0, l1))
            cds.append(jnp.exp(jnp.where(sub < SSD_HEAD_DIM, l0, l1)))
            cg3.append(cg)
            bg3.append(bg)
    xd = xs3 * jnp.stack(dts)
    acum = jnp.stack(acs)
    y = (_bein("pls,psq->plq", jnp.stack(m_lo), jnp.where(lo[None], xd, 0.0))
         + _bein("pls,psq->plq", jnp.stack(m_hi), jnp.where(lo[None], 0.0, xd)))
    y = y + _bein("pln,pqn->plq", jnp.stack(cg3), h3) * jnp.exp(acum)
    st = _bein("plq,pln->pqn", xd * jnp.exp(jnp.stack(lasts) - acum), jnp.stack(bg3))
    h_out = h3 * jnp.stack(cds) + st
    return y, h_out


def _group_slabs(groups):
    pairs = [g * PAIRS_PER_GROUP + j for g in groups for j in range(PAIRS_PER_GROUP)]
    return [slice(p * LANES, (p + 1) * LANES) for p in pairs]


def _group_batches():
    return [tuple(range(g, g + SSD_GROUPS_PER_BATCH)) for g in range(0, SSD_GROUPS, SSD_GROUPS_PER_BATCH)]


def _bc_of(xc_ref, g):
    return (xc_ref[:, B_OFF + g * SSD_STATE:B_OFF + (g + 1) * SSD_STATE].astype(F32),
            xc_ref[:, C_OFF + g * SSD_STATE:C_OFF + (g + 1) * SSD_STATE].astype(F32))


def _ssd_in_specs(chunk_of):
    return [
        pl.BlockSpec((SSD_CHUNK, SSD_CONV_CH), lambda i: (chunk_of(i), 0)),
        pl.BlockSpec((SSD_CHUNK, HPAD), lambda i: (chunk_of(i), 0)),
        pl.BlockSpec((HPAD, SSD_CHUNK), lambda i: (0, chunk_of(i))),
        pl.BlockSpec((1, HPAD), lambda i: (0, 0)), pl.BlockSpec((HPAD, 1), lambda i: (0, 0)),
        pl.BlockSpec((1, HPAD), lambda i: (0, 0)), pl.BlockSpec((HPAD, 1), lambda i: (0, 0)),
    ]


def ssd_fwd(xc, dtr, dtr_t, dtb, dtb_t, alog, alog_t, name):
    def body(xc_ref, dtr_ref, dtrt_ref, dtb_ref, dtbt_ref, al_ref, alt_ref, y_ref, hs_ref, h_scr):
        @pl.when(pl.program_id(0) == 0)
        def _():
            h_scr[...] = jnp.zeros_like(h_scr)

        hs_ref[0] = h_scr[...]
        dt, a_cum, a_cum_t, a_last = _ssd_prefix(dtr_ref[...], dtrt_ref[...], dtb_ref[...], dtbt_ref[...],
                                                  al_ref[...], alt_ref[...])
        for groups in _group_batches():
            slabs = _group_slabs(groups)
            bgs, cgs = zip(*[_bc_of(xc_ref, g) for g in groups])
            xs3 = jnp.stack([xc_ref[:, sl] for sl in slabs]).astype(F32)
            h3 = jnp.stack([h_scr[sl, :] for sl in slabs])
            y3, h3_out = _ssd_group(xs3, bgs, cgs, h3, dt, a_cum, a_cum_t, a_last, groups=groups)
            for j, sl in enumerate(slabs):
                y_ref[:, sl] = y3[j].astype(y_ref.dtype)
                h_scr[sl, :] = h3_out[j]

    return pl.pallas_call(
        body, name=name, grid=(N_CHUNKS,), in_specs=_ssd_in_specs(lambda i: i),
        out_specs=[pl.BlockSpec((SSD_CHUNK, SSD_INNER), lambda i: (i, 0)),
                   pl.BlockSpec((1, SSD_INNER, SSD_STATE), lambda i: (i, 0, 0))],
        out_shape=[SDS((SEQ, SSD_INNER), BF16), SDS((N_CHUNKS, SSD_INNER, SSD_STATE), F32)],
        scratch_shapes=[pltpu.VMEM((SSD_INNER, SSD_STATE), F32)],
        compiler_params=_cparams(("arbitrary",)),
    )(xc, dtr, dtr_t, dtb, dtb_t, alog, alog_t)


def ssd_bwd(xc, dtr, dtr_t, dtb, dtb_t, alog, alog_t, hs, dy, dxs_extra, name):
    rev = lambda i: N_CHUNKS - 1 - i

    def body(xc_ref, dtr_ref, dtrt_ref, dtb_ref, dtbt_ref, al_ref, alt_ref, hs_ref, dy_ref, dxe_ref,
             dxc_ref, ddtr_ref, ddtrt_ref, ddtb_ref, ddtbt_ref, dal_ref, dalt_ref, dh_scr):
        @pl.when(pl.program_id(0) == 0)
        def _():
            dh_scr[...] = jnp.zeros_like(dh_scr)
            for r in (ddtb_ref, ddtbt_ref, dal_ref, dalt_ref):
                r[...] = jnp.zeros_like(r)

        prefix_in = (dtr_ref[...], dtrt_ref[...], dtb_ref[...], dtbt_ref[...], al_ref[...], alt_ref[...])
        (dt, a_cum, a_cum_t, a_last), prefix_vjp = jax.vjp(_ssd_prefix, *prefix_in)
        d_dt = jnp.zeros_like(dt)
        d_acum = jnp.zeros_like(a_cum)
        d_acum_t = jnp.zeros_like(a_cum_t)
        d_alast = jnp.zeros_like(a_last)
        for groups in _group_batches():
            slabs = _group_slabs(groups)
            bgs, cgs = zip(*[_bc_of(xc_ref, g) for g in groups])
            xs3 = jnp.stack([xc_ref[:, sl] for sl in slabs]).astype(F32)
            h3 = jnp.stack([hs_ref[0, sl, :] for sl in slabs])
            _, vjp = jax.vjp(functools.partial(_ssd_group, groups=groups), xs3, bgs, cgs, h3, dt, a_cum, a_cum_t, a_last)
            dy3 = jnp.stack([dy_ref[:, sl] for sl in slabs]).astype(F32)
            dh3 = jnp.stack([dh_scr[sl, :] for sl in slabs])
            dxs3, d_bgs, d_cgs, dh3_in, ddt, dac, dact, dal = vjp((dy3, dh3))
            for j, sl in enumerate(slabs):
                dxc_ref[:, sl] = (dxs3[j] + dxe_ref[:, sl].astype(F32)).astype(dxc_ref.dtype)
                dh_scr[sl, :] = dh3_in[j]
            d_dt, d_acum, d_acum_t, d_alast = d_dt + ddt, d_acum + dac, d_acum_t + dact, d_alast + dal
            for g, d_bg, d_cg in zip(groups, d_bgs, d_cgs):
                dxc_ref[:, B_OFF + g * SSD_STATE:B_OFF + (g + 1) * SSD_STATE] = d_bg.astype(dxc_ref.dtype)
                dxc_ref[:, C_OFF + g * SSD_STATE:C_OFF + (g + 1) * SSD_STATE] = d_cg.astype(dxc_ref.dtype)
        g_dtr, g_dtrt, g_dtb, g_dtbt, g_al, g_alt = prefix_vjp((d_dt, d_acum, d_acum_t, d_alast))
        ddtr_ref[...] = g_dtr
        ddtrt_ref[...] = g_dtrt
        ddtb_ref[...] += g_dtb
        ddtbt_ref[...] += g_dtbt
        dal_ref[...] += g_al
        dalt_ref[...] += g_alt

    in_specs = _ssd_in_specs(rev) + [
        pl.BlockSpec((1, SSD_INNER, SSD_STATE), lambda i: (rev(i), 0, 0)),
        pl.BlockSpec((SSD_CHUNK, SSD_INNER), lambda i: (rev(i), 0)),
        pl.BlockSpec((SSD_CHUNK, SSD_INNER), lambda i: (rev(i), 0)),
    ]
    out_specs = [
        pl.BlockSpec((SSD_CHUNK, SSD_CONV_CH), lambda i: (rev(i), 0)),
        pl.BlockSpec((SSD_CHUNK, HPAD), lambda i: (rev(i), 0)),
        pl.BlockSpec((HPAD, SSD_CHUNK), lambda i: (0, rev(i))),
        pl.BlockSpec((1, HPAD), lambda i: (0, 0)), pl.BlockSpec((HPAD, 1), lambda i: (0, 0)),
        pl.BlockSpec((1, HPAD), lambda i: (0, 0)), pl.BlockSpec((HPAD, 1), lambda i: (0, 0)),
    ]
    out_shape = [SDS((SEQ, SSD_CONV_CH), BF16), SDS((SEQ, HPAD), F32), SDS((HPAD, SEQ), F32),
                 SDS((1, HPAD), F32), SDS((HPAD, 1), F32), SDS((1, HPAD), F32), SDS((HPAD, 1), F32)]
    return pl.pallas_call(
        body, name=name, grid=(N_CHUNKS,), in_specs=in_specs, out_specs=out_specs, out_shape=out_shape,
        scratch_shapes=[pltpu.VMEM((SSD_INNER, SSD_STATE), F32)],
        compiler_params=_cparams(("arbitrary",)),
    )(xc, dtr, dtr_t, dtb, dtb_t, alog, alog_t, hs, dy, dxs_extra)


ATTN_SCALE = ATTN_HEAD_DIM ** -0.5


UNITS_PER_PATTERN = SEQ // ATTN_BLOCK
ATTN_BATCH_FWD = 8
ATTN_BATCH_BWD = 16


def _for_unit_batches(batch, per_trip):
    for g, d in enumerate(ATTN_DILATIONS):
        nb = UNITS_PER_PATTERN // d
        span = d * ATTN_BLOCK

        def trip(t, carry, g=g, d=d, nb=nb, span=span):
            units = []
            for j in range(per_trip):
                i = t * per_trip + j
                r = i >> (nb.bit_length() - 1)
                n = i & (nb - 1)
                start = r + n * span
                prev = jnp.where(n > 0, start - span, start)
                units.append((pl.ds(start, ATTN_BLOCK, stride=d), pl.ds(prev, ATTN_BLOCK, stride=d), n > 0))
            batch(g, units)
            return carry
        lax.fori_loop(0, UNITS_PER_PATTERN // per_trip, trip, 0)


def _unit_operands(units, q_scr, k_scr, v_scr):
    def pair(scr, rows, prows):
        return jnp.concatenate([scr[prows, :], scr[rows, :]], axis=0)
    qb = jnp.stack([q_scr[rows, :] for rows, _, _ in units]).astype(BF16)
    kb = jnp.stack([pair(k_scr, rows, prows) for rows, prows, _ in units]).astype(BF16)
    vb = jnp.stack([pair(v_scr, rows, prows) for rows, prows, _ in units]).astype(BF16)
    return qb, kb, vb


def _unit_scores(qb, kb, units):
    s = jnp.einsum("bqd,bkd->bqk", qb, kb, preferred_element_type=F32) * ATTN_SCALE
    qi = lax.broadcasted_iota(jnp.int32, (ATTN_BLOCK, 2 * ATTN_BLOCK), 0)
    kj = lax.broadcasted_iota(jnp.int32, (ATTN_BLOCK, 2 * ATTN_BLOCK), 1)
    own = (kj >= ATTN_BLOCK) & (kj - ATTN_BLOCK <= qi)
    before = (kj < ATTN_BLOCK) & (kj >= qi)
    keep = jnp.stack([own | (before & has_prev) for _, _, has_prev in units])
    return jnp.where(keep, s, -jnp.inf)


def _head_specs(n_q_groups):
    blk = (SEQ, ATTN_HEAD_DIM)
    q_specs = [pl.BlockSpec(blk, functools.partial(lambda h, g: (0, g * ATTN_KV_HEADS + h), g=g)) for g in range(n_q_groups)]
    head = pl.BlockSpec(blk, lambda h: (0, h))
    table = pl.BlockSpec(blk, lambda h: (0, 0))
    return q_specs, head, table


def attn_fwd(q, k, v, tabs, name):
    q_specs, head, table = _head_specs(ATTN_N_PAT)

    def body(q0_ref, q1_ref, q2_ref, k_ref, v_ref, c_ref, sa_ref, sb_ref, y_ref, lse_ref, *scr):
        qs, og, ls, ks, vs = scr[0:3], scr[3:6], scr[6:9], scr[9], scr[10]
        c, sa, sb = c_ref[...], sa_ref[...], sb_ref[...]
        for g, q_ref in enumerate((q0_ref, q1_ref, q2_ref)):
            qs[g][...] = _rope(q_ref[...].astype(F32), c, sa, sb)
        ks[...] = _rope(k_ref[...].astype(F32), c, sa, sb)
        vs[...] = v_ref[...].astype(F32)

        def batch(g, units):
            qb, kb, vb = _unit_operands(units, qs[g], ks, vs)
            s = _unit_scores(qb, kb, units)
            m = jnp.max(s, axis=2, keepdims=True)
            p = jnp.exp(s - m)
            l = jnp.sum(p, axis=2, keepdims=True)
            o = jnp.einsum("bqk,bkd->bqd", p.astype(BF16), vb, preferred_element_type=F32) / l
            lse_b = m + jnp.log(l)
            for j, (rows, _, _) in enumerate(units):
                og[g][rows, :] = o[j]
                ls[g][rows, :] = jnp.broadcast_to(lse_b[j], (ATTN_BLOCK, LANES))

        _for_unit_batches(batch, ATTN_BATCH_FWD)
        l0, l1, l2 = ls[0][...], ls[1][...], ls[2][...]
        m = jnp.maximum(jnp.maximum(l0, l1), l2)
        e0, e1, e2 = jnp.exp(l0 - m), jnp.exp(l1 - m), jnp.exp(l2 - m)
        den = e0 + e1 + e2
        y_ref[...] = ((e0 * og[0][...] + e1 * og[1][...] + e2 * og[2][...]) / den).astype(y_ref.dtype)
        lse_ref[...] = m + jnp.log(den)

    blk = (SEQ, ATTN_HEAD_DIM)
    return pl.pallas_call(
        body, name=name, grid=(ATTN_KV_HEADS,), in_specs=[*q_specs, head, head, table, table, table],
        out_specs=[head, head], out_shape=[SDS((SEQ, ATTN_OUT), BF16), SDS((SEQ, ATTN_OUT), F32)],
        scratch_shapes=[pltpu.VMEM(blk, F32)] * (3 * ATTN_N_PAT + 2),
        compiler_params=_cparams(("parallel",)),
    )(q, q, q, k, v, *tabs)


def attn_bwd(q, k, v, tabs, y, lse, dy, name):
    q_specs, head, table = _head_specs(ATTN_N_PAT)

    def body(q0_ref, q1_ref, q2_ref, k_ref, v_ref, c_ref, sa_ref, sb_ref, y_ref, lse_ref, dy_ref,
             dq0_ref, dq1_ref, dq2_ref, dk_ref, dv_ref, *scr):
        qs, dqs, ks, dks, dd, dvs, vs = scr[0:3], scr[3:6], scr[6], scr[7], scr[8], scr[9], scr[10]
        c, sa, sb = c_ref[...], sa_ref[...], sb_ref[...]
        for g, q_ref in enumerate((q0_ref, q1_ref, q2_ref)):
            qs[g][...] = _rope(q_ref[...].astype(F32), c, sa, sb)
        ks[...] = _rope(k_ref[...].astype(F32), c, sa, sb)
        vs[...] = v_ref[...].astype(F32)
        dks[...] = jnp.zeros_like(dks)
        dvs[...] = jnp.zeros_like(dvs)
        dyv = dy_ref[...]
        dd[...] = jnp.broadcast_to(jnp.sum(dyv * y_ref[...].astype(F32), axis=1, keepdims=True), dd.shape)

        def batch(g, units):
            qb, kb, vb = _unit_operands(units, qs[g], ks, vs)
            dob = jnp.stack([dy_ref[rows, :] for rows, _, _ in units]).astype(BF16)
            lse_b = jnp.stack([lse_ref[rows, :][:, 0:1] for rows, _, _ in units])
            dsum_b = jnp.stack([dd[rows, :][:, 0:1] for rows, _, _ in units])
            p = jnp.exp(_unit_scores(qb, kb, units) - lse_b)
            dp = jnp.einsum("bqd,bkd->bqk", dob, vb, preferred_element_type=F32)
            ds = (p * (dp - dsum_b) * ATTN_SCALE).astype(BF16)
            dq = jnp.einsum("bqk,bkd->bqd", ds, kb, preferred_element_type=F32)
            dk = jnp.einsum("bqk,bqd->bkd", ds, qb, preferred_element_type=F32)
            dv = jnp.einsum("bqk,bqd->bkd", p.astype(BF16), dob, preferred_element_type=F32)
            for j, (rows, prows, _) in enumerate(units):
                dqs[g][rows, :] = dq[j]
                dks[prows, :] += dk[j, :ATTN_BLOCK]
                dks[rows, :] += dk[j, ATTN_BLOCK:]
                dvs[prows, :] += dv[j, :ATTN_BLOCK]
                dvs[rows, :] += dv[j, ATTN_BLOCK:]

        _for_unit_batches(batch, ATTN_BATCH_BWD)
        for g, dq_ref in enumerate((dq0_ref, dq1_ref, dq2_ref)):
            dq_ref[...] = _rope(dqs[g][...], c, -sa, -sb).astype(dq_ref.dtype)
        dk_ref[...] = _rope(dks[...], c, -sa, -sb).astype(dk_ref.dtype)
        dv_ref[...] = dvs[...].astype(dv_ref.dtype)

    blk = (SEQ, ATTN_HEAD_DIM)
    out = SDS((SEQ, ATTN_OUT), BF16)
    return pl.pallas_call(
        body, name=name, grid=(ATTN_KV_HEADS,), in_specs=[*q_specs, head, head, table, table, table, head, head, head],
        out_specs=[head] * 5, out_shape=[out] * 5,
        scratch_shapes=[pltpu.VMEM(blk, F32)] * (2 * ATTN_N_PAT + 5),
        compiler_params=_cparams(("parallel",)),
    )(q, q, q, k, v, *tabs, y, lse, dy)


def layer_fwd(h, getw, prefetch, small, tabs, li):
    n = f"l{li}_"
    sv = {}
    w = dict(getw(0, h))
    u = rms_fwd(h, small["norm_mix"], n + "rms_mix")
    z = matmul(u, w["w_z"], name=n + "mm_z", tb=True, out_dtype=BF16)
    prefetch(1, z)
    xbc = matmul(u, w["w_xbc"], name=n + "mm_xbc", tb=True, out_dtype=BF16)
    dtr = matmul(u, w["w_dt"], name=n + "mm_dt", tb=True)
    q = matmul(u, w["w_q"], name=n + "mm_q", tb=True, out_dtype=BF16)
    k = matmul(u, w["w_k"], name=n + "mm_k", tb=True, out_dtype=BF16)
    v = matmul(u, w["w_v"], name=n + "mm_v", tb=True, out_dtype=BF16)
    gs = matmul(u, w["w_gs"], name=n + "mm_gs", tb=True, out_dtype=BF16)
    ga = matmul(u, w["w_ga"], name=n + "mm_ga", tb=True, out_dtype=BF16)
    xc = conv_fwd(xbc, w["conv_w"], small["conv_b"], n + "conv")
    dtr_t = dtr.T
    y_ssd, hs = ssd_fwd(xc, dtr, dtr_t, small["dt_bias"], small["dt_bias"].T, small["a_log"], small["a_log"].T, n + "ssd")
    yn = ssd_post_fwd(y_ssd, xc, z, small["d_skip_x"], small["ssd_norm"], n + "ssd_post")
    y_attn, lse = attn_fwd(q, k, v, tabs, n + "attn")
    w.update(getw(1, y_ssd))
    a = matmul(yn, w["w_ssd_branch"], name=n + "mm_a", out_dtype=BF16)
    b = matmul(y_attn, w["w_attn_branch"], name=n + "mm_b", out_dtype=BF16)
    merged = gate_fwd(a, b, gs, ga, n + "gate")
    h1 = matmul(merged, w["w_out"], name=n + "mm_o", add=h)
    w.update(getw(2, h1))
    u2 = rms_fwd(h1, small["norm_ffn"], n + "rms_ffn")
    gu, act = gate_up_fwd(u2, w["w_gate_up"], n + "mm_gu_swiglu")
    h2 = matmul(act, w["w_down"], name=n + "mm_down", add=h1)
    sv.update(h=h, u=u, z=z, xbc=xbc, dtr=dtr, dtr_t=dtr_t, gs=gs, ga=ga, xc=xc, y_ssd=y_ssd, hs=hs, yn=yn,
              q=q, k=k, v=v, y_attn=y_attn, lse=lse, a=a, b=b, merged=merged, h1=h1, u2=u2, gu=gu, act=act, w=w)
    return h2, sv


def layer_bwd(dh, sv, small, tabs, li, emit):
    n = f"l{li}_b_"
    w = sv["w"]
    gw, gsm = {}, {}
    gw["w_down"] = matmul(sv["act"], dh, name=n + "mm_dwdown", ta=True, out_dtype=BF16)
    dgu = gate_up_bwd(dh, w["w_down"], sv["gu"], n + "mm_dact_swiglu")
    gw["w_gate_up"] = matmul(dgu, sv["u2"], name=n + "mm_dwgu", ta=True, out_dtype=BF16)
    tok = emit(2, gw)
    du2 = matmul(dgu, w["w_gate_up"], name=n + "mm_du2")
    dh1, gsm["norm_ffn"] = rms_bwd(sv["h1"], du2, dh, small["norm_ffn"] + tok, n + "rms_ffn")
    dmerged = matmul(dh1, w["w_out"], name=n + "mm_dmerged", tb=True)
    gw["w_out"] = matmul(sv["merged"], dh1, name=n + "mm_dwo", ta=True, out_dtype=BF16)
    da, db, dgs, dga = gate_bwd(sv["a"], sv["b"], sv["gs"], sv["ga"], dmerged, n + "gate")
    gw["w_ssd_branch"] = matmul(sv["yn"], da, name=n + "mm_dwa", ta=True, out_dtype=BF16)
    gw["w_attn_branch"] = matmul(sv["y_attn"], db, name=n + "mm_dwb", ta=True, out_dtype=BF16)
    tok = emit(1, gw)
    dyn = matmul(da, w["w_ssd_branch"], name=n + "mm_dyn", tb=True, out_dtype=BF16)
    dyattn = matmul(db, w["w_attn_branch"], name=n + "mm_dyattn", tb=True)
    dy_ssd, dxs_extra, dz, gsm["d_skip_x"], gsm["ssd_norm"] = ssd_post_bwd(
        sv["y_ssd"], sv["xc"], sv["z"], small["d_skip_x"] + tok, small["ssd_norm"], dyn, n + "ssd_post")
    dxc, ddtr, ddtr_t, ddtb, ddtb_t, dal, dal_t = ssd_bwd(
        sv["xc"], sv["dtr"], sv["dtr_t"], small["dt_bias"], small["dt_bias"].T, small["a_log"], small["a_log"].T,
        sv["hs"], dy_ssd, dxs_extra, n + "ssd")
    ddtr = (ddtr + ddtr_t.T).astype(BF16)
    gsm["dt_bias"] = ddtb + ddtb_t.T
    gsm["a_log"] = dal + dal_t.T
    dxbc, gw["conv_w"], gsm["conv_b"] = conv_bwd(sv["xbc"], w["conv_w"], small["conv_b"], dxc, n + "conv")
    dq0, dq1, dq2, dk, dv = attn_bwd(sv["q"], sv["k"], sv["v"], tabs, sv["y_attn"], sv["lse"], dyattn, n + "attn")
    u = sv["u"]
    segs = [("w_z", dz), ("w_xbc", dxbc), ("w_dt", ddtr), ("w_q0", dq0), ("w_q1", dq1), ("w_q2", dq2),
            ("w_k", dk), ("w_v", dv), ("w_gs", dgs), ("w_ga", dga)]
    gin = [matmul(dseg, u, name=n + "mm_d" + key, ta=True, out_dtype=BF16) for key, dseg in segs]
    gin[2] = gin[2][:SSD_HEADS]
    gw["w_in"] = jnp.concatenate(gin, axis=0)
    tok = emit(0, gw)
    du = jnp.zeros((SEQ, D_MODEL), F32) + tok
    for key, dseg in segs:
        du = matmul(dseg, w[key], name=n + "mm_du_" + key, add=du)
    dh0, gsm["norm_mix"] = rms_bwd(sv["h"], du, dh1, small["norm_mix"] + tok, n + "rms_mix")
    return dh0, gsm


def _my_place():
    return lax.axis_index("x"), lax.axis_index("y"), lax.axis_index("c")


def _flip(place, k):
    x, y, c = place
    return (1 - x if k & 4 else x, 1 - y if k & 2 else y, 1 - c if k & 1 else c)


def _index(place):
    return 4 * place[0] + 2 * place[1] + place[2]


ANY = pl.BlockSpec(memory_space=pl.ANY)
CHIP_FLIPS = (4, 2, 6)
SELF_AND_CHIPS = (0,) + CHIP_FLIPS


def all_gather(xs, name):
    na = len(xs)

    def body(*refs):
        x_refs, o_refs = refs[:na], refs[na:2 * na]
        send_sems, recv_sems, local_sems = refs[2 * na:]
        me = _my_place()
        sibling = _flip(me, 1)
        chips = [_flip(me, f) for f in CHIP_FLIPS]

        def copy(a, kk, block, to, src=None):
            dst = o_refs[a].at[_index(block)]
            return pltpu.make_async_remote_copy(
                src_ref=dst if src is None else src, dst_ref=dst, send_sem=send_sems.at[a, kk],
                recv_sem=recv_sems.at[a, kk], device_id=to, device_id_type=MESH)

        mine = [pltpu.make_async_copy(x_refs[a], o_refs[a].at[_index(me)], local_sems.at[a]) for a in range(na)]
        for cp in mine:
            cp.start()
        first = []
        for j, chip in enumerate(chips):
            first += [copy(a, 1 + j, me, chip, src=x_refs[a]) for a in range(na)]
        first += [copy(a, 0, me, sibling, src=x_refs[a]) for a in range(na)]
        for cp in first:
            cp.start()
        passed = []
        for j, chip in enumerate(chips):
            for a in range(na):
                copy(a, 1 + j, chip, me).wait_recv()
                cp = copy(a, 4 + j, chip, sibling)
                cp.start()
                passed.append(cp)
        for a in range(na):
            copy(a, 0, sibling, me).wait_recv()
        for j, chip in enumerate(chips):
            for a in range(na):
                copy(a, 4 + j, _flip(chip, 1), me).wait_recv()
        for cp in first + passed:
            cp.wait_send()
        for cp in mine:
            cp.wait()

    return pl.pallas_call(
        body, name=name, in_specs=[ANY] * na, out_specs=[ANY] * na,
        out_shape=[SDS((N_DEV,) + t.shape, t.dtype) for t in xs],
        scratch_shapes=[pltpu.SemaphoreType.DMA((na, N_DEV - 1)), pltpu.SemaphoreType.DMA((na, N_DEV - 1)),
                        pltpu.SemaphoreType.DMA((na,))],
    )(*xs)


HBM = pl.BlockSpec(memory_space=pltpu.HBM)
SEM = pl.BlockSpec(memory_space=pltpu.SEMAPHORE)
EFFECT = pltpu.SideEffectType.DATAFLOW_SIDE_EFFECTING
N_PEERS = N_DEV - 1


def _split_copy(src_ref, land_ref, send_sem, recv_sem, me, kk, scatter, landed_from_peer):
    peer = _flip(me, kk)
    src = src_ref.at[_index(peer)] if scatter else src_ref
    dst = land_ref.at[_index(peer if landed_from_peer else me)]
    return pltpu.make_async_remote_copy(src_ref=src, dst_ref=dst, send_sem=send_sem, recv_sem=recv_sem,
                                        device_id=peer, device_id_type=MESH)


ALL_PEERS = tuple(range(1, N_DEV))
EVERYONE = (0,) + ALL_PEERS


def exchange_start(srcs, lands, group_sizes, scatter, name, peers=ALL_PEERS):
    na, ng = len(srcs), len(group_sizes)

    def body(*refs):
        s_refs, l_refs = refs[:na], refs[na:2 * na]
        sems = refs[2 * na:2 * na + 2 * ng]
        token = refs[-1]
        me = _my_place()
        a = 0
        for gi, gsz in enumerate(group_sizes):
            for j in range(gsz):
                for pi, kk in enumerate(peers):
                    slot = j * len(peers) + pi
                    _split_copy(s_refs[a], l_refs[a], sems[2 * gi].at[slot], sems[2 * gi + 1].at[slot],
                                me, kk, scatter, False).start()
                a += 1
        token[...] = jnp.zeros_like(token)

    sem_shapes = []
    for gsz in group_sizes:
        sem_shapes += [pltpu.SemaphoreType.DMA((gsz * len(peers),))] * 2
    ins = [pltpu.with_memory_space_constraint(t, pltpu.HBM) for t in (*srcs, *lands)]
    res = pl.pallas_call(
        body, name=name, in_specs=[HBM] * (2 * na),
        out_specs=[SEM] * (2 * ng) + [HBM] * (2 * na) + [pl.BlockSpec(memory_space=pltpu.VMEM)],
        out_shape=sem_shapes + [pltpu.HBM(t.shape, t.dtype) for t in ins] + [SDS((8, LANES), F32)],
        input_output_aliases={i: 2 * ng + i for i in range(2 * na)},
        compiler_params=pltpu.CompilerParams(has_side_effects=EFFECT),
    )(*ins)
    sems = [(res[2 * gi], res[2 * gi + 1]) for gi in range(ng)]
    thru = res[2 * ng:2 * ng + 2 * na]
    return sems, thru[:na], thru[na:], res[-1]


def _wait_split_copies(s_refs, l_refs, send_sems, recv_sems, scatter, peers):
    me = _my_place()
    for j in range(len(s_refs)):
        for pi, kk in enumerate(peers):
            slot = j * len(peers) + pi
            cp = _split_copy(s_refs[j], l_refs[j], send_sems.at[slot], recv_sems.at[slot], me, kk, scatter, True)
            cp.wait_send()
            cp.wait_recv()


def exchange_wait(srcs, lands, sems, after, scatter, name, peers=ALL_PEERS):
    n = len(srcs)

    def body(*refs):
        s_refs, l_refs = refs[:n], refs[n:2 * n]
        _wait_split_copies(s_refs, l_refs, refs[2 * n], refs[2 * n + 1], scatter, peers)

    res = pl.pallas_call(
        body, name=name, in_specs=[HBM] * (2 * n) + [SEM, SEM, ANY], out_specs=[HBM] * (2 * n),
        out_shape=[pltpu.HBM(t.shape, t.dtype) for t in (*srcs, *lands)],
        input_output_aliases={i: i for i in range(2 * n)},
        compiler_params=pltpu.CompilerParams(has_side_effects=EFFECT),
    )(*srcs, *lands, sems[0], sems[1], after)
    return res[n:]


def _sibling_copies(l_refs, send_sems, recv_sems, arriving):
    me = _my_place()
    sibling = _flip(me, 1)
    held = [me] + [_flip(me, f) for f in CHIP_FLIPS]
    copies = []
    for j, land in enumerate(l_refs):
        for bi, place in enumerate(held):
            blk = land.at[_index(_flip(place, 1) if arriving else place)]
            slot = j * len(held) + bi
            copies.append(pltpu.make_async_remote_copy(src_ref=blk, dst_ref=blk, send_sem=send_sems.at[slot],
                                                       recv_sem=recv_sems.at[slot], device_id=sibling, device_id_type=MESH))
    return copies


def gather_forward(srcs, lands, sems, after, name):
    n = len(srcs)

    def body(*refs):
        s_refs, l_refs = refs[:n], refs[n:2 * n]
        _wait_split_copies(s_refs, l_refs, refs[2 * n], refs[2 * n + 1], False, SELF_AND_CHIPS)
        for cp in _sibling_copies(l_refs, refs[2 * n + 3], refs[2 * n + 4], False):
            cp.start()

    n_slots = n * (1 + len(CHIP_FLIPS))
    res = pl.pallas_call(
        body, name=name, in_specs=[HBM] * (2 * n) + [SEM, SEM, ANY],
        out_specs=[SEM, SEM] + [HBM] * (2 * n),
        out_shape=[pltpu.SemaphoreType.DMA((n_slots,))] * 2 + [pltpu.HBM(t.shape, t.dtype) for t in (*srcs, *lands)],
        input_output_aliases={i: 2 + i for i in range(2 * n)},
        compiler_params=pltpu.CompilerParams(has_side_effects=EFFECT),
    )(*srcs, *lands, sems[0], sems[1], after)
    return (res[0], res[1]), res[2 + n:]


def gather_finish(lands, sems, after, name):
    n = len(lands)

    def body(*refs):
        l_refs = refs[:n]
        for cp in _sibling_copies(l_refs, refs[n], refs[n + 1], True):
            cp.wait_send()
            cp.wait_recv()

    return pl.pallas_call(
        body, name=name, in_specs=[HBM] * n + [SEM, SEM, ANY], out_specs=[HBM] * n,
        out_shape=[pltpu.HBM(t.shape, t.dtype) for t in lands],
        input_output_aliases={i: i for i in range(n)},
        compiler_params=pltpu.CompilerParams(has_side_effects=EFFECT),
    )(*lands, sems[0], sems[1], after)


def landing_zone(block):
    return lax.empty((N_DEV,) + block.shape, block.dtype)


def sum_parts(parts, name, row_major_3d=False):
    _, r, c = parts.shape
    tc = _pick(c, (256, 128))

    def body(p_ref, o_ref):
        acc = p_ref[0].astype(F32)
        for i in range(1, N_DEV):
            acc = acc + p_ref[i].astype(F32)
        if row_major_3d:
            o_ref[:, 0, :] = acc
        else:
            o_ref[...] = acc

    out_spec = pl.BlockSpec((r, 1, tc), lambda i: (0, 0, i)) if row_major_3d else pl.BlockSpec((r, tc), lambda i: (0, i))
    return pl.pallas_call(
        body, name=name, grid=(c // tc,), in_specs=[pl.BlockSpec((N_DEV, r, tc), lambda i: (0, 0, i))],
        out_specs=out_spec, out_shape=SDS((r, 1, c) if row_major_3d else (r, c), F32),
        compiler_params=_cparams(("parallel",)),
    )(parts)


ADAMW_BLOCK_BYTES = 2 * 1024 * 1024


def adamw(w, g, m, v, name):
    shape = w.shape
    lay, rows, cols = ((1, 1) + shape)[-3:]
    tr = _pick(rows, (256, 128))
    tc = cols if tr * cols * 4 <= ADAMW_BLOCK_BYTES else _pick(cols, (256, 128))
    c1 = 1.0 / (1.0 - ADAM_B1 ** ADAM_STEP)
    c2 = 1.0 / (1.0 - ADAM_B2 ** ADAM_STEP)

    def body(w_ref, g_ref, m_ref, v_ref, d_ref, nm_ref, nv_ref):
        gg = g_ref[...]
        nm = ADAM_B1 * m_ref[...] + (1.0 - ADAM_B1) * gg
        nv = ADAM_B2 * v_ref[...] + (1.0 - ADAM_B2) * (gg * gg)
        d_ref[...] = -ADAM_LR * ((nm * c1) / (jnp.sqrt(nv * c2) + ADAM_EPS) + ADAM_WD * w_ref[...])
        nm_ref[...] = nm
        nv_ref[...] = nv

    spec = pl.BlockSpec((1, tr, tc), lambda l, i, j: (l, i, j))
    outs = pl.pallas_call(
        body, name=name, grid=(lay, rows // tr, cols // tc), in_specs=[spec] * 4, out_specs=[spec] * 3,
        out_shape=[SDS((lay, rows, cols), F32)] * 3, compiler_params=_cparams(("parallel",) * 3),
    )(*[t.reshape(lay, rows, cols) for t in (w, g, m, v)])
    return [o.reshape(shape) for o in outs]


def adamw_layer_inner(w, gs, m, v, name):
    rows, lay, cols = w.shape
    tr = _pick(rows, (256, 220, 128))
    c1 = 1.0 / (1.0 - ADAM_B1 ** ADAM_STEP)
    c2 = 1.0 / (1.0 - ADAM_B2 ** ADAM_STEP)

    def body(*refs):
        w_ref, m_ref, v_ref = refs[:3]
        g_refs = refs[3:3 + lay]
        go_ref, d_ref, nm_ref, nv_ref = refs[3 + lay:]
        for l, g_ref in enumerate(g_refs):
            gg = g_ref[:, 0, :]
            nm = ADAM_B1 * m_ref[:, l, :] + (1.0 - ADAM_B1) * gg
            nv = ADAM_B2 * v_ref[:, l, :] + (1.0 - ADAM_B2) * (gg * gg)
            d_ref[:, l, :] = -ADAM_LR * ((nm * c1) / (jnp.sqrt(nv * c2) + ADAM_EPS) + ADAM_WD * w_ref[:, l, :])
            go_ref[:, l, :] = gg
            nm_ref[:, l, :] = nm
            nv_ref[:, l, :] = nv

    inner = pl.BlockSpec((tr, lay, cols), lambda i: (i, 0, 0))
    plain = pl.BlockSpec((tr, 1, cols), lambda i: (i, 0, 0))
    return pl.pallas_call(
        body, name=name, grid=(rows // tr,), in_specs=[inner] * 3 + [plain] * lay, out_specs=[inner] * 4,
        out_shape=[SDS((rows, lay, cols), F32)] * 4, compiler_params=_cparams(("parallel",)),
    )(w, m, v, *gs)


BIG = ("w_in", "conv_w", "w_ssd_branch", "w_attn_branch", "w_out", "w_gate_up", "w_down")
TRANSPOSED = ("w_in", "w_gate_up")
SMALL = ("norm_mix", "conv_b", "dt_bias", "a_log", "d_skip", "ssd_norm", "norm_ffn")
SMALL_SIZE = {"norm_mix": 1024, "conv_b": 3072, "dt_bias": 32, "a_log": 32, "d_skip": 32, "ssd_norm": 2048, "norm_ffn": 1024}
FLAT_W = 512
SMALL_TOTAL = DEPTH * sum(SMALL_SIZE.values()) + D_MODEL + LANES
SMALL_ROWS = 32
assert SMALL_ROWS * FLAT_W >= SMALL_TOTAL


GROUPS = (("w_in", "conv_w"), ("w_ssd_branch", "w_attn_branch", "w_out"), ("w_gate_up", "w_down"))


def to_wire(k, shard):
    if k in TRANSPOSED:
        return shard.T.astype(BF16)
    return shard if k == "conv_w" else shard.astype(BF16)


def full_weights(k, g):
    if k == "conv_w":
        return {k: g.transpose(1, 0, 2).reshape(SSD_CONV, SSD_CONV_CH)}
    full = g.reshape(-1, g.shape[-1])
    if k != "w_in":
        return {k: full}
    w, off = {}, 0
    for nm, r in IN_ROWS:
        w[nm] = full[off:off + r]
        off += r
    w["w_q"] = full[sum(r for _, r in IN_ROWS[:3]):sum(r for _, r in IN_ROWS[:6])]
    w["w_dt"] = jnp.pad(w["w_dt"], ((0, HPAD - SSD_HEADS), (0, 0)))
    return w


def grads_to_wire(k, g):
    if k == "conv_w":
        return g.reshape(SSD_CONV, N_DEV, SSD_CONV_CH // N_DEV).transpose(1, 0, 2)
    return g.reshape(N_DEV, g.shape[0] // N_DEV, g.shape[1])


def _pad_heads(t):
    return jnp.pad(t.reshape(1, SSD_HEADS), ((0, 0), (0, HPAD - SSD_HEADS)))


def local_step(x, target, getw, prefetch, emit, smalls, norm_final):
    tabs = rope_tables()
    sms = []
    for li in range(DEPTH):
        s = smalls[li]
        sms.append({
            "norm_mix": s["norm_mix"].reshape(1, -1), "conv_b": s["conv_b"].reshape(1, -1),
            "dt_bias": _pad_heads(s["dt_bias"]), "a_log": _pad_heads(s["a_log"]),
            "d_skip_x": jnp.repeat(s["d_skip"], SSD_HEAD_DIM).reshape(1, -1),
            "ssd_norm": s["ssd_norm"].reshape(1, -1), "norm_ffn": s["norm_ffn"].reshape(1, -1)})
    h = x
    saved = []
    for li in range(DEPTH):
        h, sv = layer_fwd(h, functools.partial(getw, li), functools.partial(prefetch, li), sms[li], tabs, li)
        saved.append(sv)
    dh, g_final, loss = loss_head(h, target, norm_final.reshape(1, -1), "loss_head")
    gsms = [None] * DEPTH
    for li in reversed(range(DEPTH)):
        dh, gsm = layer_bwd(dh, saved[li], sms[li], tabs, li, functools.partial(emit, li))
        gsms[li] = {
            "norm_mix": gsm["norm_mix"].reshape(-1), "conv_b": gsm["conv_b"].reshape(-1),
            "dt_bias": gsm["dt_bias"][0, :SSD_HEADS], "a_log": gsm["a_log"][0, :SSD_HEADS],
            "d_skip": gsm["d_skip_x"].reshape(SSD_HEADS, SSD_HEAD_DIM).sum(axis=1),
            "ssd_norm": gsm["ssd_norm"].reshape(-1), "norm_ffn": gsm["norm_ffn"].reshape(-1)}
    return loss, dh, gsms, g_final.reshape(-1)


def kernel(x, norm_mix, w_in, conv_w, conv_b, dt_bias, a_log, d_skip, ssd_norm, w_ssd_branch, w_attn_branch, w_out, norm_ffn, w_gate_up, w_down, norm_final, loss_target, m_norm_mix, m_w_in, m_conv_w, m_conv_b, m_dt_bias, m_a_log, m_d_skip, m_ssd_norm, m_w_ssd_branch, m_w_attn_branch, m_w_out, m_norm_ffn, m_w_gate_up, m_w_down, m_norm_final, v_norm_mix, v_w_in, v_conv_w, v_conv_b, v_dt_bias, v_a_log, v_d_skip, v_ssd_norm, v_w_ssd_branch, v_w_attn_branch, v_w_out, v_norm_ffn, v_w_gate_up, v_w_down, v_norm_final):
    wv = dict(norm_mix=norm_mix, w_in=w_in, conv_w=conv_w, conv_b=conv_b, dt_bias=dt_bias, a_log=a_log, d_skip=d_skip,
              ssd_norm=ssd_norm, w_ssd_branch=w_ssd_branch, w_attn_branch=w_attn_branch, w_out=w_out, norm_ffn=norm_ffn,
              w_gate_up=w_gate_up, w_down=w_down, norm_final=norm_final)
    mv = dict(norm_mix=m_norm_mix, w_in=m_w_in, conv_w=m_conv_w, conv_b=m_conv_b, dt_bias=m_dt_bias, a_log=m_a_log,
              d_skip=m_d_skip, ssd_norm=m_ssd_norm, w_ssd_branch=m_w_ssd_branch, w_attn_branch=m_w_attn_branch,
              w_out=m_w_out, norm_ffn=m_norm_ffn, w_gate_up=m_w_gate_up, w_down=m_w_down, norm_final=m_norm_final)
    vv = dict(norm_mix=v_norm_mix, w_in=v_w_in, conv_w=v_conv_w, conv_b=v_conv_b, dt_bias=v_dt_bias, a_log=v_a_log,
              d_skip=v_d_skip, ssd_norm=v_ssd_norm, w_ssd_branch=v_w_ssd_branch, w_attn_branch=v_w_attn_branch,
              w_out=v_w_out, norm_ffn=v_norm_ffn, w_gate_up=v_w_gate_up, w_down=v_w_down, norm_final=v_norm_final)
    order = ("norm_mix", "w_in", "conv_w", "conv_b", "dt_bias", "a_log", "d_skip", "ssd_norm", "w_ssd_branch",
             "w_attn_branch", "w_out", "norm_ffn", "w_gate_up", "w_down", "norm_final")

    smalls = [{k: wv[k][li] for k in SMALL} for li in range(DEPTH)]
    n_groups = len(GROUPS)

    first_lands = all_gather([to_wire(k, wv[k][0]) for k in GROUPS[0]], "gather_first")
    later = [(li, gi) for li in range(DEPTH) for gi in range(n_groups)][1:]
    behind_first = first_lands[1][0, 0, 0] * 0.0
    srcs = [to_wire(k, wv[k][li] + behind_first if k == "conv_w" else wv[k][li]) for li, gi in later for k in GROUPS[gi]]
    sizes = [len(GROUPS[gi]) for _, gi in later]
    w_sems, w_srcs, w_lands, token = exchange_start(srcs, [landing_zone(s) for s in srcs], sizes, False,
                                                    "gather_start", peers=SELF_AND_CHIPS)
    smalls[0]["norm_mix"] = smalls[0]["norm_mix"] + token[0, 0]
    second_leg = {}

    def forward(slot, after):
        if slot < len(later) and slot not in second_leg:
            sl = slice(sum(sizes[:slot]), sum(sizes[:slot + 1]))
            second_leg[slot] = gather_forward(w_srcs[sl], w_lands[sl], w_sems[slot], after, f"gather_forward_{slot}")

    def prefetch(li, gi, after):
        if (li, gi) == later[0]:
            forward(0, after)

    def getw(li, gi, after):
        if (li, gi) == (0, 0):
            lands = first_lands
        else:
            slot = later.index((li, gi))
            forward(slot, after)
            sems2, lands2 = second_leg[slot]
            lands = gather_finish(lands2, sems2, after, f"gather_finish_{li}_{gi}")
            forward(slot + 1, lands[0])
        w = {}
        for k, land in zip(GROUPS[gi], lands):
            w.update(full_weights(k, land))
        return w

    pending = []

    def emit(li, gi, gw):
        parts = [grads_to_wire(k, gw[k]) for k in GROUPS[gi]]
        lands = [landing_zone(p[0]) for p in parts]
        sems, p_thru, l_thru, tok = exchange_start(parts, lands, [len(parts)], True, f"grads_start_{li}_{gi}", peers=EVERYONE)
        pending.append((li, gi, sems[0], p_thru, l_thru))
        return tok[0, 0]

    loss_p, dx, gsms, g_final = local_step(x[0], loss_target[0], getw, prefetch, emit, smalls, norm_final)

    grads, deltas, new_m, new_v = {}, {}, {}, {}

    def update(k):
        if k == "w_in":
            inner = lambda t: t.transpose(2, 0, 1)
            outs = adamw_layer_inner(inner(wv[k]), shard_g[k], inner(mv[k]), inner(vv[k]), "adamw_" + k)
            grads[k], deltas[k], new_m[k], new_v[k] = (t.transpose(1, 2, 0) for t in outs)
            return outs[3]
        if k in BIG:
            grads[k] = jnp.stack([g.T if k in TRANSPOSED else g for g in shard_g[k]])
        deltas[k], new_m[k], new_v[k] = adamw(wv[k], grads[k], mv[k], vv[k], "adamw_" + k)
        return new_v[k]

    shard_g = {k: [None] * DEPTH for k in BIG}

    def collect(entry, after):
        li, gi, sems, p_thru, l_thru = entry
        recv = exchange_wait(p_thru, l_thru, sems, after, True, f"grads_wait_{li}_{gi}", peers=EVERYONE)
        for k, r in zip(GROUPS[gi], recv):
            if k == "conv_w":
                r = r.reshape(N_DEV, 1, -1)
            after = sum_parts(r, f"sum_{k}_{li}", row_major_3d=(k == "w_in"))
            shard_g[k][li] = after if k in TRANSPOSED else after.reshape(wv[k].shape[1:])
        return after

    after = dx
    for entry in pending[:-1]:
        after = collect(entry, after)
    done = [after[:1, :1].reshape(1)]
    for gi in (2, 1):
        for k in GROUPS[gi]:
            done.append(update(k).reshape(-1)[:1])

    flat = [gsms[li][k] for li in range(DEPTH) for k in SMALL] + [g_final, loss_p.reshape(-1)]
    flat.append(jnp.zeros((SMALL_ROWS * FLAT_W - SMALL_TOTAL,), F32))
    small_all = all_gather([jnp.concatenate(flat).reshape(SMALL_ROWS, FLAT_W)], "gather_small")[0]
    small_sum = sum_parts(small_all, "sum_small").reshape(-1)
    off = 0
    per_layer = {k: [] for k in SMALL}
    for li in range(DEPTH):
        for k in SMALL:
            per_layer[k].append(small_sum[off:off + SMALL_SIZE[k]])
            off += SMALL_SIZE[k]
    for k in SMALL:
        grads[k] = jnp.stack(per_layer[k])
    grads["norm_final"] = small_sum[off:off + D_MODEL]
    loss = small_sum[off + D_MODEL]
    for k in (*SMALL, "norm_final"):
        done.append(update(k).reshape(-1)[:1])

    collect(pending[-1], jnp.concatenate(done))
    for k in GROUPS[0]:
        update(k)

    return (loss, dx.reshape(x.shape), *[grads[k] for k in order], *[deltas[k] for k in order],
            *[new_m[k] for k in order], *[new_v[k] for k in order])
```

```python
import functools

import jax
import jax.numpy as jnp
from jax import lax
from jax.experimental import pallas as pl
from jax.experimental.pallas import tpu as pltpu

F32, BF16 = jnp.float32, jnp.bfloat16
SDS = jax.ShapeDtypeStruct
MESH = pl.DeviceIdType.MESH

D_MODEL = 1024
SEQ = 2048
DEPTH = 2
RMS_EPS = 1e-5
SSD_INNER = 2048
SSD_HEAD_DIM = 64
SSD_HEADS = 32
SSD_STATE = 128
SSD_GROUPS = 4
SSD_CONV = 4
SSD_CHUNK = 128
SSD_CONV_CH = 3072
ATTN_HEAD_DIM = 128
ATTN_KV_HEADS = 8
ATTN_DILATIONS = (1, 4, 16)
ATTN_N_PAT = 3
ATTN_BLOCK = 128
ATTN_OUT = 1024
ROPE_THETA = 500000.0
ROPE_DIM = 32
FFN_HIDDEN = 2816
ADAM_LR, ADAM_B1, ADAM_B2, ADAM_EPS, ADAM_WD, ADAM_STEP = 0.001, 0.9, 0.999, 1e-08, 0.01, 10

N_DEV = 8
LANES = 128
VMEM_LIMIT = 56 * 1024 * 1024
HPAD = 128
HIGHEST = lax.Precision.HIGHEST

IN_ROWS = (("w_z", 2048), ("w_xbc", 3072), ("w_dt", 32), ("w_q0", 1024), ("w_q1", 1024), ("w_q2", 1024),
           ("w_k", 1024), ("w_v", 1024), ("w_gs", 1024), ("w_ga", 1024))
N_IN = sum(r for _, r in IN_ROWS)


def _cparams(sem):
    return pltpu.CompilerParams(dimension_semantics=sem, vmem_limit_bytes=VMEM_LIMIT)


def _sigmoid(x):
    return 0.5 * jnp.tanh(0.5 * x) + 0.5


def _silu(x):
    return x * _sigmoid(x)


def _softplus(x):
    return jnp.maximum(x, 0.0) + jnp.log(1.0 + jnp.exp(-jnp.abs(x)))


def _dot(a, b, dims=(((1,), (0,)), ((), ())), precision=None):
    return lax.dot_general(a, b, dims, precision=precision, preferred_element_type=F32)


NT = (((1,), (1,)), ((), ()))
TN = (((0,), (0,)), ((), ()))


def _bdot(a, b, dims=(((1,), (0,)), ((), ()))):
    return _dot(a.astype(BF16), b.astype(BF16), dims)


def _pick(dim, cands):
    for c in cands:
        if dim % c == 0:
            return c
    return dim


WHOLE_K_BUDGET = 40 * 1024 * 1024
RESIDENT_B_BYTES = 12 * 1024 * 1024
OUT_TILE_BYTES = 6 * 1024 * 1024


def matmul(a, b, *, name, ta=False, tb=False, out_dtype=F32, add=None):
    m, k = (a.shape[1], a.shape[0]) if ta else a.shape
    n = b.shape[0] if tb else b.shape[1]
    out_bytes = jnp.dtype(out_dtype).itemsize + (4 if add is not None else 0)
    if k * n * b.dtype.itemsize <= RESIDENT_B_BYTES:
        tn = n
        tm = next(t for t in (512, 256, 128) if m % t == 0 and t * n * out_bytes <= OUT_TILE_BYTES)
    else:
        tn = _pick(n, (1024, 1408, 512, 256, 128))
        tm = _pick(m, (512, 1408, 256, 128)) if tn == n else _pick(m, (1024, 1408, 512, 256, 128))
    tk = _pick(k, (2048, 1024, 1408, 512, 256, 128))
    whole_k_bytes = 2 * (tm * k * a.dtype.itemsize + k * tn * b.dtype.itemsize)
    if tn == n and whole_k_bytes <= WHOLE_K_BUDGET:
        tk = k
    nk = k // tk
    a_spec = pl.BlockSpec((tk, tm), lambda i, j, kk: (kk, i)) if ta else pl.BlockSpec((tm, tk), lambda i, j, kk: (i, kk))
    b_spec = pl.BlockSpec((tn, tk), lambda i, j, kk: (j, kk)) if tb else pl.BlockSpec((tk, tn), lambda i, j, kk: (kk, j))
    dims = (((0 if ta else 1,), (1 if tb else 0,)), ((), ()))
    has_add = add is not None

    def body(*refs):
        a_ref, b_ref = refs[:2]
        add_ref = refs[2] if has_add else None
        o_ref = refs[3] if has_add else refs[2]
        acc = refs[-1] if nk > 1 else None
        kk = pl.program_id(2)

        def product():
            return _dot(a_ref[...].astype(BF16), b_ref[...].astype(BF16), dims)

        def finish(r):
            if has_add:
                r = r + add_ref[...].astype(F32)
            o_ref[...] = r.astype(o_ref.dtype)

        if nk == 1:
            finish(product())
            return

        @pl.when(kk == 0)
        def _():
            acc[...] = product()

        @pl.when((kk > 0) & (kk < nk - 1))
        def _():
            acc[...] += product()

        @pl.when(kk == nk - 1)
        def _():
            finish(acc[...] + product())

    in_specs = [a_spec, b_spec]
    args = [a, b]
    if has_add:
        in_specs.append(pl.BlockSpec((tm, tn), lambda i, j, kk: (i, j)))
        args.append(add)
    return pl.pallas_call(
        body, name=name, grid=(m // tm, n // tn, nk),
        in_specs=in_specs, out_specs=pl.BlockSpec((tm, tn), lambda i, j, kk: (i, j)),
        out_shape=SDS((m, n), out_dtype), scratch_shapes=[pltpu.VMEM((tm, tn), F32)] if nk > 1 else [],
        compiler_params=_cparams(("parallel", "parallel", "arbitrary")),
    )(*args)


def matmul_rows(a, b, post, extras, outs, *, name, tb=False, tm=256):
    a_list, b_list = (list(a), list(b)) if isinstance(a, (list, tuple)) else ([a], [b])
    m = a_list[0].shape[0]
    dims = NT if tb else (((1,), (0,)), ((), ()))
    npr, ne = len(a_list), len(extras)

    def body(*refs):
        a_refs, b_refs = refs[:npr], refs[npr:2 * npr]
        e_refs, o_refs = refs[2 * npr:2 * npr + ne], refs[2 * npr + ne:]
        prods = [_dot(ar[...].astype(BF16), br[...].astype(BF16), dims) for ar, br in zip(a_refs, b_refs)]
        res = post(*prods, *[r[...] for r in e_refs])
        for r, val in zip(o_refs, res):
            r[...] = val.astype(r.dtype)

    row = lambda width: pl.BlockSpec((tm, width), lambda i: (i, 0))
    whole = lambda t: pl.BlockSpec(t.shape, lambda i: (0, 0))
    return pl.pallas_call(
        body, name=name, grid=(m // tm,),
        in_specs=[row(t.shape[1]) for t in a_list] + [whole(t) for t in b_list] + [row(e.shape[1]) for e in extras],
        out_specs=[row(c) for c, _ in outs], out_shape=[SDS((m, c), dt) for c, dt in outs],
        compiler_params=_cparams(("parallel",)),
    )(*a_list, *b_list, *extras)


def rowcall(name, fn, rows, params, row_outs, red_outs=(), tr=256):
    s = rows[0].shape[0]
    n_in = len(rows) + len(params)
    n_row = len(row_outs)

    def body(*refs):
        outs = fn(*[r[...].astype(F32) for r in refs[:n_in]])
        if not isinstance(outs, (tuple, list)):
            outs = (outs,)
        orefs = refs[n_in:]
        for r, o in zip(orefs[:n_row], outs[:n_row]):
            r[...] = o.astype(r.dtype)
        if red_outs:
            @pl.when(pl.program_id(0) == 0)
            def _():
                for r in orefs[n_row:]:
                    r[...] = jnp.zeros_like(r)
            for r, o in zip(orefs[n_row:], outs[n_row:]):
                r[...] += o.astype(F32)

    widths = [a[1] if isinstance(a, tuple) else a.shape[1] for a in rows]
    rows = [a[0] if isinstance(a, tuple) else a for a in rows]
    in_specs = [pl.BlockSpec((tr, wd), lambda i: (i, 0)) for wd in widths]
    in_specs += [pl.BlockSpec(p.shape, lambda i: (0, 0)) for p in params]
    out_specs = [pl.BlockSpec((tr, c), lambda i: (i, 0)) for c, _ in row_outs]
    out_specs += [pl.BlockSpec(shp, lambda i: (0, 0)) for shp in red_outs]
    out_shape = [SDS((s, c), dt) for c, dt in row_outs] + [SDS(shp, F32) for shp in red_outs]
    res = pl.pallas_call(
        body, name=name, grid=(s // tr,), in_specs=in_specs, out_specs=out_specs, out_shape=out_shape,
        compiler_params=_cparams(("arbitrary",) if red_outs else ("parallel",)),
    )(*rows, *params)
    return res


def _rms(x, w):
    return x * lax.rsqrt(jnp.mean(x * x, axis=-1, keepdims=True) + RMS_EPS) * w


def rms_fwd(h, w, name):
    return rowcall(name, _rms, [h], [w], [(D_MODEL, BF16)])[0]


def rms_bwd(h, du, dres, w, name):
    def fn(hb, dub, dresb, wb):
        _, vjp = jax.vjp(_rms, hb, wb)
        dh, dw = vjp(dub)
        return dh + dresb, dw
    return rowcall(name, fn, [h, du, dres], [w], [(D_MODEL, F32)], [(1, D_MODEL)])


def loss_head(h, target, w, name):
    def fn(hb, tb, wb):
        def f(hh, ww):
            err = _rms(hh, ww) - tb
            return 0.5 * jnp.sum(jnp.mean(err * err, axis=-1, keepdims=True), axis=0, keepdims=True)
        val, vjp = jax.vjp(f, hb, wb)
        dh, dw = vjp(jnp.ones((1, 1), F32))
        return dh, dw, jnp.broadcast_to(val, (1, LANES))
    return rowcall(name, fn, [h, target], [w], [(D_MODEL, F32)], [(1, D_MODEL), (1, LANES)])


def _gate(a, b, gs, ga):
    return _sigmoid(gs) * a + _sigmoid(ga) * b


def gate_fwd(yn, w_ssd, y_attn, w_attn, gs, ga, name):
    def post(pa, pb, gsb, gab):
        a, b = pa.astype(BF16), pb.astype(BF16)
        return a, b, _gate(a.astype(F32), b.astype(F32), gsb.astype(F32), gab.astype(F32))
    return matmul_rows([yn, y_attn], [w_ssd, w_attn], post, [gs, ga], [(D_MODEL, BF16)] * 3, name=name)


def gate_bwd(dh1, w_out, a, b, gs, ga, name):
    def post(dm, ab, bb, gsb, gab):
        _, vjp = jax.vjp(_gate, ab.astype(F32), bb.astype(F32), gsb.astype(F32), gab.astype(F32))
        return vjp(dm)
    return matmul_rows(dh1, w_out, post, [a, b, gs, ga], [(D_MODEL, BF16)] * 4, name=name, tb=True)


def _swiglu(gu):
    return _silu(gu[:, :FFN_HIDDEN]) * gu[:, FFN_HIDDEN:]


def gate_up_fwd(u2, w_gate_up_t, name):
    def post(acc):
        gu = acc.astype(BF16)
        return gu, _swiglu(gu.astype(F32))
    return matmul_rows(u2, w_gate_up_t, post, [], [(2 * FFN_HIDDEN, BF16), (FFN_HIDDEN, BF16)], name=name, tb=True)


def gate_up_bwd(dh, w_down, gu, name):
    def post(acc, gub):
        _, vjp = jax.vjp(_swiglu, gub.astype(F32))
        return vjp(acc.astype(BF16).astype(F32))
    return matmul_rows(dh, w_down, post, [gu], [(2 * FFN_HIDDEN, BF16)], name=name, tb=True)[0]


def _ssd_post(y, xs, z, dskip, normw):
    y = (y + dskip * xs) * _silu(z)
    gw = SSD_INNER // SSD_GROUPS
    parts = []
    for g in range(SSD_GROUPS):
        yg = y[:, g * gw:(g + 1) * gw]
        parts.append(yg * lax.rsqrt(jnp.mean(yg * yg, axis=-1, keepdims=True) + RMS_EPS))
    return jnp.concatenate(parts, axis=-1) * normw


def ssd_post_fwd(y, xc, z, dskip, normw, name):
    return rowcall(name, _ssd_post, [y, (xc, SSD_INNER), z], [dskip, normw], [(SSD_INNER, BF16)])[0]


def ssd_post_bwd(y, xc, z, dskip, normw, dyn, name):
    def fn(yb, xsb, zb, dynb, db, nb):
        _, vjp = jax.vjp(_ssd_post, yb, xsb, zb, db, nb)
        return vjp(dynb)
    return rowcall(name, fn, [y, (xc, SSD_INNER), z, dyn], [dskip, normw],
                   [(SSD_INNER, BF16)] * 3, [(1, SSD_INNER), (1, SSD_INNER)])


def _rope(t, cosf, sina, sinb):
    return t * cosf + pltpu.roll(t, LANES - ROPE_DIM // 2, 1) * sina + pltpu.roll(t, ROPE_DIM // 2, 1) * sinb


def rope_tables():
    half = ROPE_DIM // 2
    inv = ROPE_THETA ** (-jnp.arange(0, ROPE_DIM, 2, dtype=F32) / ROPE_DIM)
    ang = jnp.arange(SEQ, dtype=F32)[:, None] * inv[None, :]
    cos, sin = jnp.cos(ang), jnp.sin(ang)
    zeros = jnp.zeros((SEQ, LANES - ROPE_DIM), F32)
    z16 = jnp.zeros((SEQ, half), F32)
    cosf = jnp.concatenate([cos, cos, jnp.ones((SEQ, LANES - ROPE_DIM), F32)], axis=1)
    sina = jnp.concatenate([-sin, z16, zeros], axis=1)
    sinb = jnp.concatenate([z16, sin, zeros], axis=1)
    return cosf, sina, sinb


CONV_TC = 256


def _conv_pre(x, w, b, row):
    acc = x * w[SSD_CONV - 1:SSD_CONV, :] + b
    shifted = [x]
    for j in range(1, SSD_CONV):
        xs = jnp.where(row >= j, pltpu.roll(x, j, 0), 0.0)
        shifted.append(xs)
        acc = acc + xs * w[SSD_CONV - 1 - j:SSD_CONV - j, :]
    return acc, shifted


def conv_fwd(xbc, w, b, name):
    def body(x_ref, w_ref, b_ref, o_ref):
        row = lax.broadcasted_iota(jnp.int32, (SEQ, CONV_TC), 0)
        pre, _ = _conv_pre(x_ref[...].astype(F32), w_ref[...], b_ref[...], row)
        o_ref[...] = _silu(pre).astype(o_ref.dtype)
    return pl.pallas_call(
        body, name=name, grid=(SSD_CONV_CH // CONV_TC,),
        in_specs=[pl.BlockSpec((SEQ, CONV_TC), lambda i: (0, i)), pl.BlockSpec((SSD_CONV, CONV_TC), lambda i: (0, i)),
                  pl.BlockSpec((1, CONV_TC), lambda i: (0, i))],
        out_specs=pl.BlockSpec((SEQ, CONV_TC), lambda i: (0, i)),
        out_shape=SDS((SEQ, SSD_CONV_CH), BF16), compiler_params=_cparams(("parallel",)),
    )(xbc, w, b)


def conv_bwd(xbc, w, b, dxc, name):
    def body(x_ref, w_ref, b_ref, dy_ref, dx_ref, dw_ref, db_ref):
        row = lax.broadcasted_iota(jnp.int32, (SEQ, CONV_TC), 0)
        wv = w_ref[...]
        pre, shifted = _conv_pre(x_ref[...].astype(F32), wv, b_ref[...], row)
        sg = _sigmoid(pre)
        ds = dy_ref[...].astype(F32) * (sg * (1.0 + pre * (1.0 - sg)))
        dx = ds * wv[SSD_CONV - 1:SSD_CONV, :]
        for j in range(1, SSD_CONV):
            dsj = jnp.where(row < SEQ - j, pltpu.roll(ds, SEQ - j, 0), 0.0)
            dx = dx + dsj * wv[SSD_CONV - 1 - j:SSD_CONV - j, :]
        dx_ref[...] = dx.astype(dx_ref.dtype)
        for j in range(SSD_CONV):
            dw_ref[SSD_CONV - 1 - j:SSD_CONV - j, :] = jnp.sum(ds * shifted[j], axis=0, keepdims=True)
        db_ref[...] = jnp.sum(ds, axis=0, keepdims=True)
    return pl.pallas_call(
        body, name=name, grid=(SSD_CONV_CH // CONV_TC,),
        in_specs=[pl.BlockSpec((SEQ, CONV_TC), lambda i: (0, i)), pl.BlockSpec((SSD_CONV, CONV_TC), lambda i: (0, i)),
                  pl.BlockSpec((1, CONV_TC), lambda i: (0, i)), pl.BlockSpec((SEQ, CONV_TC), lambda i: (0, i))],
        out_specs=[pl.BlockSpec((SEQ, CONV_TC), lambda i: (0, i)), pl.BlockSpec((SSD_CONV, CONV_TC), lambda i: (0, i)),
                   pl.BlockSpec((1, CONV_TC), lambda i: (0, i))],
        out_shape=[SDS((SEQ, SSD_CONV_CH), BF16), SDS((SSD_CONV, SSD_CONV_CH), F32), SDS((1, SSD_CONV_CH), F32)],
        compiler_params=_cparams(("parallel",)),
    )(xbc, w, b, dxc)


N_CHUNKS = SEQ // SSD_CHUNK
N_PAIRS = SSD_HEADS // 2
PAIRS_PER_GROUP = N_PAIRS // SSD_GROUPS
B_OFF = SSD_INNER
C_OFF = SSD_INNER + SSD_GROUPS * SSD_STATE


def _ssd_prefix(dtr, dtr_t, dtb, dtb_t, alog, alog_t):
    ln = SSD_CHUNK
    dt = _softplus(dtr + dtb)
    dt_t = _softplus(dtr_t + dtb_t)
    dta = dt * (-jnp.exp(alog))
    dta_t = dt_t * (-jnp.exp(alog_t))
    r = lax.broadcasted_iota(jnp.int32, (ln, ln), 0)
    c = lax.broadcasted_iota(jnp.int32, (ln, ln), 1)
    a_cum = _dot((r >= c).astype(F32), dta, precision=HIGHEST)
    a_cum_t = _dot(dta_t, (r <= c).astype(F32), precision=HIGHEST)
    a_last = jnp.sum(dta_t, axis=1, keepdims=True)
    return dt, a_cum, a_cum_t, a_last


def _bein(spec, a, b):
    return jnp.einsum(spec, a.astype(BF16), b.astype(BF16), preferred_element_type=F32)


SSD_GROUPS_PER_BATCH = 4


def _ssd_group(xs3, bgs, cgs, h3, dt, a_cum, a_cum_t, a_last, *, groups):
    ln = SSD_CHUNK
    lane = lax.broadcasted_iota(jnp.int32, (ln, LANES), 1)
    sub = lax.broadcasted_iota(jnp.int32, (LANES, SSD_STATE), 0)
    row = lax.broadcasted_iota(jnp.int32, (ln, ln), 0)
    col = lax.broadcasted_iota(jnp.int32, (ln, ln), 1)
    lo = lane < SSD_HEAD_DIM
    causal = row >= col
    m_lo, m_hi, dts, acs, lasts, cds, cg3, bg3 = [], [], [], [], [], [], [], []
    for g, bg, cg in zip(groups, bgs, cgs):
        cb = _bdot(cg, bg, NT)
        for j in range(PAIRS_PER_GROUP):
            e0 = 2 * (g * PAIRS_PER_GROUP + j)
            e1 = e0 + 1
            c0, c1 = a_cum[:, e0:e0 + 1], a_cum[:, e1:e1 + 1]
            r0, r1 = a_cum_t[e0:e0 + 1, :], a_cum_t[e1:e1 + 1, :]
            l0, l1 = a_last[e0:e0 + 1, :], a_last[e1:e1 + 1, :]
            m_lo.append(cb * jnp.exp(jnp.where(causal, c0 - r0, -jnp.inf)))
            m_hi.append(cb * jnp.exp(jnp.where(causal, c1 - r1, -jnp.inf)))
            dts.append(jnp.where(lo, dt[:, e0:e0 + 1], dt[:, e1:e1 + 1]))
            acs.append(jnp.where(lo, c0, c1))
            lasts.append(jnp.where(lo, l0, l1))
            cds.append(jnp.exp(jnp.where(sub < SSD_HEAD_DIM, l0, l1)))
            cg3.append(cg)
            bg3.append(bg)
    xd = xs3 * jnp.stack(dts)
    acum = jnp.stack(acs)
    y = (_bein("pls,psq->plq", jnp.stack(m_lo), jnp.where(lo[None], xd, 0.0))
         + _bein("pls,psq->plq", jnp.stack(m_hi), jnp.where(lo[None], 0.0, xd)))
    y = y + _bein("pln,pqn->plq", jnp.stack(cg3), h3) * jnp.exp(acum)
    st = _bein("plq,pln->pqn", xd * jnp.exp(jnp.stack(lasts) - acum), jnp.stack(bg3))
    h_out = h3 * jnp.stack(cds) + st
    return y, h_out


def _group_slabs(groups):
    pairs = [g * PAIRS_PER_GROUP + j for g in groups for j in range(PAIRS_PER_GROUP)]
    return [slice(p * LANES, (p + 1) * LANES) for p in pairs]


def _group_batches():
    return [tuple(range(g, g + SSD_GROUPS_PER_BATCH)) for g in range(0, SSD_GROUPS, SSD_GROUPS_PER_BATCH)]


def _bc_of(xc_ref, g):
    return (xc_ref[:, B_OFF + g * SSD_STATE:B_OFF + (g + 1) * SSD_STATE].astype(F32),
            xc_ref[:, C_OFF + g * SSD_STATE:C_OFF + (g + 1) * SSD_STATE].astype(F32))


def _ssd_in_specs(chunk_of):
    return [
        pl.BlockSpec((SSD_CHUNK, SSD_CONV_CH), lambda i: (chunk_of(i), 0)),
        pl.BlockSpec((SSD_CHUNK, HPAD), lambda i: (chunk_of(i), 0)),
        pl.BlockSpec((HPAD, SSD_CHUNK), lambda i: (0, chunk_of(i))),
        pl.BlockSpec((1, HPAD), lambda i: (0, 0)), pl.BlockSpec((HPAD, 1), lambda i: (0, 0)),
        pl.BlockSpec((1, HPAD), lambda i: (0, 0)), pl.BlockSpec((HPAD, 1), lambda i: (0, 0)),
    ]


def ssd_fwd(xc, dtr, dtr_t, dtb, dtb_t, alog, alog_t, name):
    def body(xc_ref, dtr_ref, dtrt_ref, dtb_ref, dtbt_ref, al_ref, alt_ref, y_ref, hs_ref, h_scr):
        @pl.when(pl.program_id(0) == 0)
        def _():
            h_scr[...] = jnp.zeros_like(h_scr)

        hs_ref[0] = h_scr[...]
        dt, a_cum, a_cum_t, a_last = _ssd_prefix(dtr_ref[...], dtrt_ref[...], dtb_ref[...], dtbt_ref[...],
                                                  al_ref[...], alt_ref[...])
        for groups in _group_batches():
            slabs = _group_slabs(groups)
            bgs, cgs = zip(*[_bc_of(xc_ref, g) for g in groups])
            xs3 = jnp.stack([xc_ref[:, sl] for sl in slabs]).astype(F32)
            h3 = jnp.stack([h_scr[sl, :] for sl in slabs])
            y3, h3_out = _ssd_group(xs3, bgs, cgs, h3, dt, a_cum, a_cum_t, a_last, groups=groups)
            for j, sl in enumerate(slabs):
                y_ref[:, sl] = y3[j].astype(y_ref.dtype)
                h_scr[sl, :] = h3_out[j]

    return pl.pallas_call(
        body, name=name, grid=(N_CHUNKS,), in_specs=_ssd_in_specs(lambda i: i),
        out_specs=[pl.BlockSpec((SSD_CHUNK, SSD_INNER), lambda i: (i, 0)),
                   pl.BlockSpec((1, SSD_INNER, SSD_STATE), lambda i: (i, 0, 0))],
        out_shape=[SDS((SEQ, SSD_INNER), BF16), SDS((N_CHUNKS, SSD_INNER, SSD_STATE), F32)],
        scratch_shapes=[pltpu.VMEM((SSD_INNER, SSD_STATE), F32)],
        compiler_params=_cparams(("arbitrary",)),
    )(xc, dtr, dtr_t, dtb, dtb_t, alog, alog_t)


def ssd_bwd(xc, dtr, dtr_t, dtb, dtb_t, alog, alog_t, hs, dy, dxs_extra, name):
    rev = lambda i: N_CHUNKS - 1 - i

    def body(xc_ref, dtr_ref, dtrt_ref, dtb_ref, dtbt_ref, al_ref, alt_ref, hs_ref, dy_ref, dxe_ref,
             dxc_ref, ddtr_ref, ddtrt_ref, ddtb_ref, ddtbt_ref, dal_ref, dalt_ref, dh_scr):
        @pl.when(pl.program_id(0) == 0)
        def _():
            dh_scr[...] = jnp.zeros_like(dh_scr)
            for r in (ddtb_ref, ddtbt_ref, dal_ref, dalt_ref):
                r[...] = jnp.zeros_like(r)

        prefix_in = (dtr_ref[...], dtrt_ref[...], dtb_ref[...], dtbt_ref[...], al_ref[...], alt_ref[...])
        (dt, a_cum, a_cum_t, a_last), prefix_vjp = jax.vjp(_ssd_prefix, *prefix_in)
        d_dt = jnp.zeros_like(dt)
        d_acum = jnp.zeros_like(a_cum)
        d_acum_t = jnp.zeros_like(a_cum_t)
        d_alast = jnp.zeros_like(a_last)
        for groups in _group_batches():
            slabs = _group_slabs(groups)
            bgs, cgs = zip(*[_bc_of(xc_ref, g) for g in groups])
            xs3 = jnp.stack([xc_ref[:, sl] for sl in slabs]).astype(F32)
            h3 = jnp.stack([hs_ref[0, sl, :] for sl in slabs])
            _, vjp = jax.vjp(functools.partial(_ssd_group, groups=groups), xs3, bgs, cgs, h3, dt, a_cum, a_cum_t, a_last)
            dy3 = jnp.stack([dy_ref[:, sl] for sl in slabs]).astype(F32)
            dh3 = jnp.stack([dh_scr[sl, :] for sl in slabs])
            dxs3, d_bgs, d_cgs, dh3_in, ddt, dac, dact, dal = vjp((dy3, dh3))
            for j, sl in enumerate(slabs):
                dxc_ref[:, sl] = (dxs3[j] + dxe_ref[:, sl].astype(F32)).astype(dxc_ref.dtype)
                dh_scr[sl, :] = dh3_in[j]
            d_dt, d_acum, d_acum_t, d_alast = d_dt + ddt, d_acum + dac, d_acum_t + dact, d_alast + dal
            for g, d_bg, d_cg in zip(groups, d_bgs, d_cgs):
                dxc_ref[:, B_OFF + g * SSD_STATE:B_OFF + (g + 1) * SSD_STATE] = d_bg.astype(dxc_ref.dtype)
                dxc_ref[:, C_OFF + g * SSD_STATE:C_OFF + (g + 1) * SSD_STATE] = d_cg.astype(dxc_ref.dtype)
        g_dtr, g_dtrt, g_dtb, g_dtbt, g_al, g_alt = prefix_vjp((d_dt, d_acum, d_acum_t, d_alast))
        ddtr_ref[...] = g_dtr
        ddtrt_ref[...] = g_dtrt
        ddtb_ref[...] += g_dtb
        ddtbt_ref[...] += g_dtbt
        dal_ref[...] += g_al
        dalt_ref[...] += g_alt

    in_specs = _ssd_in_specs(rev) + [
        pl.BlockSpec((1, SSD_INNER, SSD_STATE), lambda i: (rev(i), 0, 0)),
        pl.BlockSpec((SSD_CHUNK, SSD_INNER), lambda i: (rev(i), 0)),
        pl.BlockSpec((SSD_CHUNK, SSD_INNER), lambda i: (rev(i), 0)),
    ]
    out_specs = [
        pl.BlockSpec((SSD_CHUNK, SSD_CONV_CH), lambda i: (rev(i), 0)),
        pl.BlockSpec((SSD_CHUNK, HPAD), lambda i: (rev(i), 0)),
        pl.BlockSpec((HPAD, SSD_CHUNK), lambda i: (0, rev(i))),
        pl.BlockSpec((1, HPAD), lambda i: (0, 0)), pl.BlockSpec((HPAD, 1), lambda i: (0, 0)),
        pl.BlockSpec((1, HPAD), lambda i: (0, 0)), pl.BlockSpec((HPAD, 1), lambda i: (0, 0)),
    ]
    out_shape = [SDS((SEQ, SSD_CONV_CH), BF16), SDS((SEQ, HPAD), F32), SDS((HPAD, SEQ), F32),
                 SDS((1, HPAD), F32), SDS((HPAD, 1), F32), SDS((1, HPAD), F32), SDS((HPAD, 1), F32)]
    return pl.pallas_call(
        body, name=name, grid=(N_CHUNKS,), in_specs=in_specs, out_specs=out_specs, out_shape=out_shape,
        scratch_shapes=[pltpu.VMEM((SSD_INNER, SSD_STATE), F32)],
        compiler_params=_cparams(("arbitrary",)),
    )(xc, dtr, dtr_t, dtb, dtb_t, alog, alog_t, hs, dy, dxs_extra)


ATTN_SCALE = ATTN_HEAD_DIM ** -0.5


UNITS_PER_PATTERN = SEQ // ATTN_BLOCK
ATTN_BATCH_FWD = 8
ATTN_BATCH_BWD = 16


def _for_unit_batches(batch, per_trip):
    for g, d in enumerate(ATTN_DILATIONS):
        nb = UNITS_PER_PATTERN // d
        span = d * ATTN_BLOCK

        def trip(t, carry, g=g, d=d, nb=nb, span=span):
            units = []
            for j in range(per_trip):
                i = t * per_trip + j
                r = i >> (nb.bit_length() - 1)
                n = i & (nb - 1)
                start = r + n * span
                prev = jnp.where(n > 0, start - span, start)
                units.append((pl.ds(start, ATTN_BLOCK, stride=d), pl.ds(prev, ATTN_BLOCK, stride=d), n > 0))
            batch(g, units)
            return carry
        lax.fori_loop(0, UNITS_PER_PATTERN // per_trip, trip, 0)


def _unit_operands(units, q_scr, k_scr, v_scr):
    def pair(scr, rows, prows):
        return jnp.concatenate([scr[prows, :], scr[rows, :]], axis=0)
    qb = jnp.stack([q_scr[rows, :] for rows, _, _ in units]).astype(BF16)
    kb = jnp.stack([pair(k_scr, rows, prows) for rows, prows, _ in units]).astype(BF16)
    vb = jnp.stack([pair(v_scr, rows, prows) for rows, prows, _ in units]).astype(BF16)
    return qb, kb, vb


def _unit_scores(qb, kb, units):
    s = jnp.einsum("bqd,bkd->bqk", qb, kb, preferred_element_type=F32) * ATTN_SCALE
    qi = lax.broadcasted_iota(jnp.int32, (ATTN_BLOCK, 2 * ATTN_BLOCK), 0)
    kj = lax.broadcasted_iota(jnp.int32, (ATTN_BLOCK, 2 * ATTN_BLOCK), 1)
    own = (kj >= ATTN_BLOCK) & (kj - ATTN_BLOCK <= qi)
    before = (kj < ATTN_BLOCK) & (kj >= qi)
    keep = jnp.stack([own | (before & has_prev) for _, _, has_prev in units])
    return jnp.where(keep, s, -jnp.inf)


def _head_specs(n_q_groups):
    blk = (SEQ, ATTN_HEAD_DIM)
    q_specs = [pl.BlockSpec(blk, functools.partial(lambda h, g: (0, g * ATTN_KV_HEADS + h), g=g)) for g in range(n_q_groups)]
    head = pl.BlockSpec(blk, lambda h: (0, h))
    table = pl.BlockSpec(blk, lambda h: (0, 0))
    return q_specs, head, table


def attn_fwd(q, k, v, tabs, name):
    q_specs, head, table = _head_specs(ATTN_N_PAT)

    def body(q0_ref, q1_ref, q2_ref, k_ref, v_ref, c_ref, sa_ref, sb_ref, y_ref, lse_ref, *scr):
        qs, og, ls, ks, vs = scr[0:3], scr[3:6], scr[6:9], scr[9], scr[10]
        c, sa, sb = c_ref[...], sa_ref[...], sb_ref[...]
        for g, q_ref in enumerate((q0_ref, q1_ref, q2_ref)):
            qs[g][...] = _rope(q_ref[...].astype(F32), c, sa, sb)
        ks[...] = _rope(k_ref[...].astype(F32), c, sa, sb)
        vs[...] = v_ref[...].astype(F32)

        def batch(g, units):
            qb, kb, vb = _unit_operands(units, qs[g], ks, vs)
            s = _unit_scores(qb, kb, units)
            m = jnp.max(s, axis=2, keepdims=True)
            p = jnp.exp(s - m)
            l = jnp.sum(p, axis=2, keepdims=True)
            o = jnp.einsum("bqk,bkd->bqd", p.astype(BF16), vb, preferred_element_type=F32) / l
            lse_b = m + jnp.log(l)
            for j, (rows, _, _) in enumerate(units):
                og[g][rows, :] = o[j]
                ls[g][rows, :] = jnp.broadcast_to(lse_b[j], (ATTN_BLOCK, LANES))

        _for_unit_batches(batch, ATTN_BATCH_FWD)
        l0, l1, l2 = ls[0][...], ls[1][...], ls[2][...]
        m = jnp.maximum(jnp.maximum(l0, l1), l2)
        e0, e1, e2 = jnp.exp(l0 - m), jnp.exp(l1 - m), jnp.exp(l2 - m)
        den = e0 + e1 + e2
        y_ref[...] = ((e0 * og[0][...] + e1 * og[1][...] + e2 * og[2][...]) / den).astype(y_ref.dtype)
        lse_ref[...] = m + jnp.log(den)

    blk = (SEQ, ATTN_HEAD_DIM)
    return pl.pallas_call(
        body, name=name, grid=(ATTN_KV_HEADS,), in_specs=[*q_specs, head, head, table, table, table],
        out_specs=[head, head], out_shape=[SDS((SEQ, ATTN_OUT), BF16), SDS((SEQ, ATTN_OUT), F32)],
        scratch_shapes=[pltpu.VMEM(blk, F32)] * (3 * ATTN_N_PAT + 2),
        compiler_params=_cparams(("parallel",)),
    )(q, q, q, k, v, *tabs)


def attn_bwd(q, k, v, tabs, y, lse, dy, name):
    q_specs, head, table = _head_specs(ATTN_N_PAT)

    def body(q0_ref, q1_ref, q2_ref, k_ref, v_ref, c_ref, sa_ref, sb_ref, y_ref, lse_ref, dy_ref,
             dq0_ref, dq1_ref, dq2_ref, dk_ref, dv_ref, *scr):
        qs, dqs, ks, dks, dd, dvs, vs = scr[0:3], scr[3:6], scr[6], scr[7], scr[8], scr[9], scr[10]
        c, sa, sb = c_ref[...], sa_ref[...], sb_ref[...]
        for g, q_ref in enumerate((q0_ref, q1_ref, q2_ref)):
            qs[g][...] = _rope(q_ref[...].astype(F32), c, sa, sb)
        ks[...] = _rope(k_ref[...].astype(F32), c, sa, sb)
        vs[...] = v_ref[...].astype(F32)
        dks[...] = jnp.zeros_like(dks)
        dvs[...] = jnp.zeros_like(dvs)
        dyv = dy_ref[...]
        dd[...] = jnp.broadcast_to(jnp.sum(dyv * y_ref[...].astype(F32), axis=1, keepdims=True), dd.shape)

        def batch(g, units):
            qb, kb, vb = _unit_operands(units, qs[g], ks, vs)
            dob = jnp.stack([dy_ref[rows, :] for rows, _, _ in units]).astype(BF16)
            lse_b = jnp.stack([lse_ref[rows, :][:, 0:1] for rows, _, _ in units])
            dsum_b = jnp.stack([dd[rows, :][:, 0:1] for rows, _, _ in units])
            p = jnp.exp(_unit_scores(qb, kb, units) - lse_b)
            dp = jnp.einsum("bqd,bkd->bqk", dob, vb, preferred_element_type=F32)
            ds = (p * (dp - dsum_b) * ATTN_SCALE).astype(BF16)
            dq = jnp.einsum("bqk,bkd->bqd", ds, kb, preferred_element_type=F32)
            dk = jnp.einsum("bqk,bqd->bkd", ds, qb, preferred_element_type=F32)
            dv = jnp.einsum("bqk,bqd->bkd", p.astype(BF16), dob, preferred_element_type=F32)
            for j, (rows, prows, _) in enumerate(units):
                dqs[g][rows, :] = dq[j]
                dks[prows, :] += dk[j, :ATTN_BLOCK]
                dks[rows, :] += dk[j, ATTN_BLOCK:]
                dvs[prows, :] += dv[j, :ATTN_BLOCK]
                dvs[rows, :] += dv[j, ATTN_BLOCK:]

        _for_unit_batches(batch, ATTN_BATCH_BWD)
        for g, dq_ref in enumerate((dq0_ref, dq1_ref, dq2_ref)):
            dq_ref[...] = _rope(dqs[g][...], c, -sa, -sb).astype(dq_ref.dtype)
        dk_ref[...] = _rope(dks[...], c, -sa, -sb).astype(dk_ref.dtype)
        dv_ref[...] = dvs[...].astype(dv_ref.dtype)

    blk = (SEQ, ATTN_HEAD_DIM)
    out = SDS((SEQ, ATTN_OUT), BF16)
    return pl.pallas_call(
        body, name=name, grid=(ATTN_KV_HEADS,), in_specs=[*q_specs, head, head, table, table, table, head, head, head],
        out_specs=[head] * 5, out_shape=[out] * 5,
        scratch_shapes=[pltpu.VMEM(blk, F32)] * (2 * ATTN_N_PAT + 5),
        compiler_params=_cparams(("parallel",)),
    )(q, q, q, k, v, *tabs, y, lse, dy)


def layer_fwd(h, getw, prefetch, small, tabs, li):
    n = f"l{li}_"
    sv = {}
    w = dict(getw(0, h))
    u = rms_fwd(h, small["norm_mix"], n + "rms_mix")
    z = matmul(u, w["w_z"], name=n + "mm_z", tb=True, out_dtype=BF16)
    prefetch(1, z)
    xbc = matmul(u, w["w_xbc"], name=n + "mm_xbc", tb=True, out_dtype=BF16)
    dtr = matmul(u, w["w_dt"], name=n + "mm_dt", tb=True)
    q = matmul(u, w["w_q"], name=n + "mm_q", tb=True, out_dtype=BF16)
    k = matmul(u, w["w_k"], name=n + "mm_k", tb=True, out_dtype=BF16)
    v = matmul(u, w["w_v"], name=n + "mm_v", tb=True, out_dtype=BF16)
    gs = matmul(u, w["w_gs"], name=n + "mm_gs", tb=True, out_dtype=BF16)
    ga = matmul(u, w["w_ga"], name=n + "mm_ga", tb=True, out_dtype=BF16)
    xc = conv_fwd(xbc, w["conv_w"], small["conv_b"], n + "conv")
    dtr_t = dtr.T
    y_ssd, hs = ssd_fwd(xc, dtr, dtr_t, small["dt_bias"], small["dt_bias"].T, small["a_log"], small["a_log"].T, n + "ssd")
    yn = ssd_post_fwd(y_ssd, xc, z, small["d_skip_x"], small["ssd_norm"], n + "ssd_post")
    y_attn, lse = attn_fwd(q, k, v, tabs, n + "attn")
    w.update(getw(1, y_ssd))
    a, b, merged = gate_fwd(yn, w["w_ssd_branch"], y_attn, w["w_attn_branch"], gs, ga, n + "mm_ab_gate")
    h1 = matmul(merged, w["w_out"], name=n + "mm_o", add=h)
    w.update(getw(2, h1))
    u2 = rms_fwd(h1, small["norm_ffn"], n + "rms_ffn")
    gu, act = gate_up_fwd(u2, w["w_gate_up"], n + "mm_gu_swiglu")
    h2 = matmul(act, w["w_down"], name=n + "mm_down", add=h1)
    sv.update(h=h, u=u, z=z, xbc=xbc, dtr=dtr, dtr_t=dtr_t, gs=gs, ga=ga, xc=xc, y_ssd=y_ssd, hs=hs, yn=yn,
              q=q, k=k, v=v, y_attn=y_attn, lse=lse, a=a, b=b, merged=merged, h1=h1, u2=u2, gu=gu, act=act, w=w)
    return h2, sv


def layer_bwd(dh, sv, small, tabs, li, emit):
    n = f"l{li}_b_"
    w = sv["w"]
    gw, gsm = {}, {}
    gw["w_down"] = matmul(sv["act"], dh, name=n + "mm_dwdown", ta=True, out_dtype=BF16)
    dgu = gate_up_bwd(dh, w["w_down"], sv["gu"], n + "mm_dact_swiglu")
    gw["w_gate_up"] = matmul(dgu, sv["u2"], name=n + "mm_dwgu", ta=True, out_dtype=BF16)
    tok = emit(2, gw)
    du2 = matmul(dgu, w["w_gate_up"], name=n + "mm_du2")
    dh1, gsm["norm_ffn"] = rms_bwd(sv["h1"], du2, dh, small["norm_ffn"] + tok, n + "rms_ffn")
    gw["w_out"] = matmul(sv["merged"], dh1, name=n + "mm_dwo", ta=True, out_dtype=BF16)
    da, db, dgs, dga = gate_bwd(dh1, w["w_out"], sv["a"], sv["b"], sv["gs"], sv["ga"], n + "mm_dmerged_gate")
    gw["w_ssd_branch"] = matmul(sv["yn"], da, name=n + "mm_dwa", ta=True, out_dtype=BF16)
    gw["w_attn_branch"] = matmul(sv["y_attn"], db, name=n + "mm_dwb", ta=True, out_dtype=BF16)
    tok = emit(1, gw)
    dyn = matmul(da, w["w_ssd_branch"], name=n + "mm_dyn", tb=True, out_dtype=BF16)
    dyattn = matmul(db, w["w_attn_branch"], name=n + "mm_dyattn", tb=True)
    dy_ssd, dxs_extra, dz, gsm["d_skip_x"], gsm["ssd_norm"] = ssd_post_bwd(
        sv["y_ssd"], sv["xc"], sv["z"], small["d_skip_x"] + tok, small["ssd_norm"], dyn, n + "ssd_post")
    dxc, ddtr, ddtr_t, ddtb, ddtb_t, dal, dal_t = ssd_bwd(
        sv["xc"], sv["dtr"], sv["dtr_t"], small["dt_bias"], small["dt_bias"].T, small["a_log"], small["a_log"].T,
        sv["hs"], dy_ssd, dxs_extra, n + "ssd")
    ddtr = (ddtr + ddtr_t.T).astype(BF16)
    gsm["dt_bias"] = ddtb + ddtb_t.T
    gsm["a_log"] = dal + dal_t.T
    dxbc, gw["conv_w"], gsm["conv_b"] = conv_bwd(sv["xbc"], w["conv_w"], small["conv_b"], dxc, n + "conv")
    dq0, dq1, dq2, dk, dv = attn_bwd(sv["q"], sv["k"], sv["v"], tabs, sv["y_attn"], sv["lse"], dyattn, n + "attn")
    u = sv["u"]
    segs = [("w_z", dz), ("w_xbc", dxbc), ("w_dt", ddtr), ("w_q0", dq0), ("w_q1", dq1), ("w_q2", dq2),
            ("w_k", dk), ("w_v", dv), ("w_gs", dgs), ("w_ga", dga)]
    gin = [matmul(dseg, u, name=n + "mm_d" + key, ta=True, out_dtype=BF16) for key, dseg in segs]
    gin[2] = gin[2][:SSD_HEADS]
    gw["w_in"] = jnp.concatenate(gin, axis=0)
    tok = emit(0, gw)
    du = jnp.zeros((SEQ, D_MODEL), F32) + tok
    for key, dseg in segs:
        du = matmul(dseg, w[key], name=n + "mm_du_" + key, add=du)
    dh0, gsm["norm_mix"] = rms_bwd(sv["h"], du, dh1, small["norm_mix"] + tok, n + "rms_mix")
    return dh0, gsm


def _my_place():
    return lax.axis_index("x"), lax.axis_index("y"), lax.axis_index("c")


def _flip(place, k):
    x, y, c = place
    return (1 - x if k & 4 else x, 1 - y if k & 2 else y, 1 - c if k & 1 else c)


def _index(place):
    return 4 * place[0] + 2 * place[1] + place[2]


ANY = pl.BlockSpec(memory_space=pl.ANY)
CHIP_FLIPS = (4, 2, 6)
SELF_AND_CHIPS = (0,) + CHIP_FLIPS


def all_gather(xs, name):
    na = len(xs)

    def body(*refs):
        x_refs, o_refs = refs[:na], refs[na:2 * na]
        send_sems, recv_sems, local_sems = refs[2 * na:]
        me = _my_place()
        sibling = _flip(me, 1)
        chips = [_flip(me, f) for f in CHIP_FLIPS]

        def copy(a, kk, block, to, src=None):
            dst = o_refs[a].at[_index(block)]
            return pltpu.make_async_remote_copy(
                src_ref=dst if src is None else src, dst_ref=dst, send_sem=send_sems.at[a, kk],
                recv_sem=recv_sems.at[a, kk], device_id=to, device_id_type=MESH)

        mine = [pltpu.make_async_copy(x_refs[a], o_refs[a].at[_index(me)], local_sems.at[a]) for a in range(na)]
        for cp in mine:
            cp.start()
        first = []
        for j, chip in enumerate(chips):
            first += [copy(a, 1 + j, me, chip, src=x_refs[a]) for a in range(na)]
        first += [copy(a, 0, me, sibling, src=x_refs[a]) for a in range(na)]
        for cp in first:
            cp.start()
        passed = []
        for j, chip in enumerate(chips):
            for a in range(na):
                copy(a, 1 + j, chip, me).wait_recv()
                cp = copy(a, 4 + j, chip, sibling)
                cp.start()
                passed.append(cp)
        for a in range(na):
            copy(a, 0, sibling, me).wait_recv()
        for j, chip in enumerate(chips):
            for a in range(na):
                copy(a, 4 + j, _flip(chip, 1), me).wait_recv()
        for cp in first + passed:
            cp.wait_send()
        for cp in mine:
            cp.wait()

    return pl.pallas_call(
        body, name=name, in_specs=[ANY] * na, out_specs=[ANY] * na,
        out_shape=[SDS((N_DEV,) + t.shape, t.dtype) for t in xs],
        scratch_shapes=[pltpu.SemaphoreType.DMA((na, N_DEV - 1)), pltpu.SemaphoreType.DMA((na, N_DEV - 1)),
                        pltpu.SemaphoreType.DMA((na,))],
    )(*xs)


HBM = pl.BlockSpec(memory_space=pltpu.HBM)
SEM = pl.BlockSpec(memory_space=pltpu.SEMAPHORE)
EFFECT = pltpu.SideEffectType.DATAFLOW_SIDE_EFFECTING
N_PEERS = N_DEV - 1


def _split_copy(src_ref, land_ref, send_sem, recv_sem, me, kk, scatter, landed_from_peer):
    peer = _flip(me, kk)
    src = src_ref.at[_index(peer)] if scatter else src_ref
    dst = land_ref.at[_index(peer if landed_from_peer else me)]
    return pltpu.make_async_remote_copy(src_ref=src, dst_ref=dst, send_sem=send_sem, recv_sem=recv_sem,
                                        device_id=peer, device_id_type=MESH)


ALL_PEERS = tuple(range(1, N_DEV))
EVERYONE = (0,) + ALL_PEERS


def exchange_start(srcs, lands, group_sizes, scatter, name, peers=ALL_PEERS):
    na, ng = len(srcs), len(group_sizes)

    def body(*refs):
        s_refs, l_refs = refs[:na], refs[na:2 * na]
        sems = refs[2 * na:2 * na + 2 * ng]
        token = refs[-1]
        me = _my_place()
        a = 0
        for gi, gsz in enumerate(group_sizes):
            for j in range(gsz):
                for pi, kk in enumerate(peers):
                    slot = j * len(peers) + pi
                    _split_copy(s_refs[a], l_refs[a], sems[2 * gi].at[slot], sems[2 * gi + 1].at[slot],
                                me, kk, scatter, False).start()
                a += 1
        token[...] = jnp.zeros_like(token)

    sem_shapes = []
    for gsz in group_sizes:
        sem_shapes += [pltpu.SemaphoreType.DMA((gsz * len(peers),))] * 2
    ins = [pltpu.with_memory_space_constraint(t, pltpu.HBM) for t in (*srcs, *lands)]
    res = pl.pallas_call(
        body, name=name, in_specs=[HBM] * (2 * na),
        out_specs=[SEM] * (2 * ng) + [HBM] * (2 * na) + [pl.BlockSpec(memory_space=pltpu.VMEM)],
        out_shape=sem_shapes + [pltpu.HBM(t.shape, t.dtype) for t in ins] + [SDS((8, LANES), F32)],
        input_output_aliases={i: 2 * ng + i for i in range(2 * na)},
        compiler_params=pltpu.CompilerParams(has_side_effects=EFFECT),
    )(*ins)
    sems = [(res[2 * gi], res[2 * gi + 1]) for gi in range(ng)]
    thru = res[2 * ng:2 * ng + 2 * na]
    return sems, thru[:na], thru[na:], res[-1]


def _wait_split_copies(s_refs, l_refs, send_sems, recv_sems, scatter, peers):
    me = _my_place()
    for j in range(len(s_refs)):
        for pi, kk in enumerate(peers):
            slot = j * len(peers) + pi
            cp = _split_copy(s_refs[j], l_refs[j], send_sems.at[slot], recv_sems.at[slot], me, kk, scatter, True)
            cp.wait_send()
            cp.wait_recv()


def exchange_wait(srcs, lands, sems, after, scatter, name, peers=ALL_PEERS):
    n = len(srcs)

    def body(*refs):
        s_refs, l_refs = refs[:n], refs[n:2 * n]
        _wait_split_copies(s_refs, l_refs, refs[2 * n], refs[2 * n + 1], scatter, peers)

    res = pl.pallas_call(
        body, name=name, in_specs=[HBM] * (2 * n) + [SEM, SEM, ANY], out_specs=[HBM] * (2 * n),
        out_shape=[pltpu.HBM(t.shape, t.dtype) for t in (*srcs, *lands)],
        input_output_aliases={i: i for i in range(2 * n)},
        compiler_params=pltpu.CompilerParams(has_side_effects=EFFECT),
    )(*srcs, *lands, sems[0], sems[1], after)
    return res[n:]


def _sibling_copies(l_refs, send_sems, recv_sems, arriving):
    me = _my_place()
    sibling = _flip(me, 1)
    held = [me] + [_flip(me, f) for f in CHIP_FLIPS]
    copies = []
    for j, land in enumerate(l_refs):
        for bi, place in enumerate(held):
            blk = land.at[_index(_flip(place, 1) if arriving else place)]
            slot = j * len(held) + bi
            copies.append(pltpu.make_async_remote_copy(src_ref=blk, dst_ref=blk, send_sem=send_sems.at[slot],
                                                       recv_sem=recv_sems.at[slot], device_id=sibling, device_id_type=MESH))
    return copies


def gather_forward(srcs, lands, sems, after, name):
    n = len(srcs)

    def body(*refs):
        s_refs, l_refs = refs[:n], refs[n:2 * n]
        _wait_split_copies(s_refs, l_refs, refs[2 * n], refs[2 * n + 1], False, SELF_AND_CHIPS)
        for cp in _sibling_copies(l_refs, refs[2 * n + 3], refs[2 * n + 4], False):
            cp.start()

    n_slots = n * (1 + len(CHIP_FLIPS))
    res = pl.pallas_call(
        body, name=name, in_specs=[HBM] * (2 * n) + [SEM, SEM, ANY],
        out_specs=[SEM, SEM] + [HBM] * (2 * n),
        out_shape=[pltpu.SemaphoreType.DMA((n_slots,))] * 2 + [pltpu.HBM(t.shape, t.dtype) for t in (*srcs, *lands)],
        input_output_aliases={i: 2 + i for i in range(2 * n)},
        compiler_params=pltpu.CompilerParams(has_side_effects=EFFECT),
    )(*srcs, *lands, sems[0], sems[1], after)
    return (res[0], res[1]), res[2 + n:]


def gather_finish(lands, sems, after, name):
    n = len(lands)

    def body(*refs):
        l_refs = refs[:n]
        for cp in _sibling_copies(l_refs, refs[n], refs[n + 1], True):
            cp.wait_send()
            cp.wait_recv()

    return pl.pallas_call(
        body, name=name, in_specs=[HBM] * n + [SEM, SEM, ANY], out_specs=[HBM] * n,
        out_shape=[pltpu.HBM(t.shape, t.dtype) for t in lands],
        input_output_aliases={i: i for i in range(n)},
        compiler_params=pltpu.CompilerParams(has_side_effects=EFFECT),
    )(*lands, sems[0], sems[1], after)


def landing_zone(block):
    return lax.empty((N_DEV,) + block.shape, block.dtype)


def sum_parts(parts, name, row_major_3d=False):
    _, r, c = parts.shape
    tc = _pick(c, (256, 128))

    def body(p_ref, o_ref):
        acc = p_ref[0].astype(F32)
        for i in range(1, N_DEV):
            acc = acc + p_ref[i].astype(F32)
        if row_major_3d:
            o_ref[:, 0, :] = acc
        else:
            o_ref[...] = acc

    out_spec = pl.BlockSpec((r, 1, tc), lambda i: (0, 0, i)) if row_major_3d else pl.BlockSpec((r, tc), lambda i: (0, i))
    return pl.pallas_call(
        body, name=name, grid=(c // tc,), in_specs=[pl.BlockSpec((N_DEV, r, tc), lambda i: (0, 0, i))],
        out_specs=out_spec, out_shape=SDS((r, 1, c) if row_major_3d else (r, c), F32),
        compiler_params=_cparams(("parallel",)),
    )(parts)


ADAMW_BLOCK_BYTES = 2 * 1024 * 1024


def adamw(w, g, m, v, name):
    shape = w.shape
    lay, rows, cols = ((1, 1) + shape)[-3:]
    tr = _pick(rows, (256, 128))
    tc = cols if tr * cols * 4 <= ADAMW_BLOCK_BYTES else _pick(cols, (256, 128))
    c1 = 1.0 / (1.0 - ADAM_B1 ** ADAM_STEP)
    c2 = 1.0 / (1.0 - ADAM_B2 ** ADAM_STEP)

    def body(w_ref, g_ref, m_ref, v_ref, d_ref, nm_ref, nv_ref):
        gg = g_ref[...]
        nm = ADAM_B1 * m_ref[...] + (1.0 - ADAM_B1) * gg
        nv = ADAM_B2 * v_ref[...] + (1.0 - ADAM_B2) * (gg * gg)
        d_ref[...] = -ADAM_LR * ((nm * c1) / (jnp.sqrt(nv * c2) + ADAM_EPS) + ADAM_WD * w_ref[...])
        nm_ref[...] = nm
        nv_ref[...] = nv

    spec = pl.BlockSpec((1, tr, tc), lambda l, i, j: (l, i, j))
    outs = pl.pallas_call(
        body, name=name, grid=(lay, rows // tr, cols // tc), in_specs=[spec] * 4, out_specs=[spec] * 3,
        out_shape=[SDS((lay, rows, cols), F32)] * 3, compiler_params=_cparams(("parallel",) * 3),
    )(*[t.reshape(lay, rows, cols) for t in (w, g, m, v)])
    return [o.reshape(shape) for o in outs]


def adamw_layer_inner(w, gs, m, v, name):
    rows, lay, cols = w.shape
    tr = _pick(rows, (256, 220, 128))
    c1 = 1.0 / (1.0 - ADAM_B1 ** ADAM_STEP)
    c2 = 1.0 / (1.0 - ADAM_B2 ** ADAM_STEP)

    def body(*refs):
        w_ref, m_ref, v_ref = refs[:3]
        g_refs = refs[3:3 + lay]
        go_ref, d_ref, nm_ref, nv_ref = refs[3 + lay:]
        for l, g_ref in enumerate(g_refs):
            gg = g_ref[:, 0, :]
            nm = ADAM_B1 * m_ref[:, l, :] + (1.0 - ADAM_B1) * gg
            nv = ADAM_B2 * v_ref[:, l, :] + (1.0 - ADAM_B2) * (gg * gg)
            d_ref[:, l, :] = -ADAM_LR * ((nm * c1) / (jnp.sqrt(nv * c2) + ADAM_EPS) + ADAM_WD * w_ref[:, l, :])
            go_ref[:, l, :] = gg
            nm_ref[:, l, :] = nm
            nv_ref[:, l, :] = nv

    inner = pl.BlockSpec((tr, lay, cols), lambda i: (i, 0, 0))
    plain = pl.BlockSpec((tr, 1, cols), lambda i: (i, 0, 0))
    return pl.pallas_call(
        body, name=name, grid=(rows // tr,), in_specs=[inner] * 3 + [plain] * lay, out_specs=[inner] * 4,
        out_shape=[SDS((rows, lay, cols), F32)] * 4, compiler_params=_cparams(("parallel",)),
    )(w, m, v, *gs)


BIG = ("w_in", "conv_w", "w_ssd_branch", "w_attn_branch", "w_out", "w_gate_up", "w_down")
TRANSPOSED = ("w_in", "w_gate_up")
SMALL = ("norm_mix", "conv_b", "dt_bias", "a_log", "d_skip", "ssd_norm", "norm_ffn")
SMALL_SIZE = {"norm_mix": 1024, "conv_b": 3072, "dt_bias": 32, "a_log": 32, "d_skip": 32, "ssd_norm": 2048, "norm_ffn": 1024}
FLAT_W = 512
SMALL_TOTAL = DEPTH * sum(SMALL_SIZE.values()) + D_MODEL + LANES
SMALL_ROWS = 32
assert SMALL_ROWS * FLAT_W >= SMALL_TOTAL


GROUPS = (("w_in", "conv_w"), ("w_ssd_branch", "w_attn_branch", "w_out"), ("w_gate_up", "w_down"))


def to_wire(k, shard):
    if k in TRANSPOSED:
        return shard.T.astype(BF16)
    return shard if k == "conv_w" else shard.astype(BF16)


def full_weights(k, g):
    if k == "conv_w":
        return {k: g.transpose(1, 0, 2).reshape(SSD_CONV, SSD_CONV_CH)}
    full = g.reshape(-1, g.shape[-1])
    if k != "w_in":
        return {k: full}
    w, off = {}, 0
    for nm, r in IN_ROWS:
        w[nm] = full[off:off + r]
        off += r
    w["w_q"] = full[sum(r for _, r in IN_ROWS[:3]):sum(r for _, r in IN_ROWS[:6])]
    w["w_dt"] = jnp.pad(w["w_dt"], ((0, HPAD - SSD_HEADS), (0, 0)))
    return w


def grads_to_wire(k, g):
    if k == "conv_w":
        return g.reshape(SSD_CONV, N_DEV, SSD_CONV_CH // N_DEV).transpose(1, 0, 2)
    return g.reshape(N_DEV, g.shape[0] // N_DEV, g.shape[1])


def _pad_heads(t):
    return jnp.pad(t.reshape(1, SSD_HEADS), ((0, 0), (0, HPAD - SSD_HEADS)))


def local_step(x, target, getw, prefetch, emit, smalls, norm_final):
    tabs = rope_tables()
    sms = []
    for li in range(DEPTH):
        s = smalls[li]
        sms.append({
            "norm_mix": s["norm_mix"].reshape(1, -1), "conv_b": s["conv_b"].reshape(1, -1),
            "dt_bias": _pad_heads(s["dt_bias"]), "a_log": _pad_heads(s["a_log"]),
            "d_skip_x": jnp.repeat(s["d_skip"], SSD_HEAD_DIM).reshape(1, -1),
            "ssd_norm": s["ssd_norm"].reshape(1, -1), "norm_ffn": s["norm_ffn"].reshape(1, -1)})
    h = x
    saved = []
    for li in range(DEPTH):
        h, sv = layer_fwd(h, functools.partial(getw, li), functools.partial(prefetch, li), sms[li], tabs, li)
        saved.append(sv)
    dh, g_final, loss = loss_head(h, target, norm_final.reshape(1, -1), "loss_head")
    gsms = [None] * DEPTH
    for li in reversed(range(DEPTH)):
        dh, gsm = layer_bwd(dh, saved[li], sms[li], tabs, li, functools.partial(emit, li))
        gsms[li] = {
            "norm_mix": gsm["norm_mix"].reshape(-1), "conv_b": gsm["conv_b"].reshape(-1),
            "dt_bias": gsm["dt_bias"][0, :SSD_HEADS], "a_log": gsm["a_log"][0, :SSD_HEADS],
            "d_skip": gsm["d_skip_x"].reshape(SSD_HEADS, SSD_HEAD_DIM).sum(axis=1),
            "ssd_norm": gsm["ssd_norm"].reshape(-1), "norm_ffn": gsm["norm_ffn"].reshape(-1)}
    return loss, dh, gsms, g_final.reshape(-1)


def kernel(x, norm_mix, w_in, conv_w, conv_b, dt_bias, a_log, d_skip, ssd_norm, w_ssd_branch, w_attn_branch, w_out, norm_ffn, w_gate_up, w_down, norm_final, loss_target, m_norm_mix, m_w_in, m_conv_w, m_conv_b, m_dt_bias, m_a_log, m_d_skip, m_ssd_norm, m_w_ssd_branch, m_w_attn_branch, m_w_out, m_norm_ffn, m_w_gate_up, m_w_down, m_norm_final, v_norm_mix, v_w_in, v_conv_w, v_conv_b, v_dt_bias, v_a_log, v_d_skip, v_ssd_norm, v_w_ssd_branch, v_w_attn_branch, v_w_out, v_norm_ffn, v_w_gate_up, v_w_down, v_norm_final):
    wv = dict(norm_mix=norm_mix, w_in=w_in, conv_w=conv_w, conv_b=conv_b, dt_bias=dt_bias, a_log=a_log, d_skip=d_skip,
              ssd_norm=ssd_norm, w_ssd_branch=w_ssd_branch, w_attn_branch=w_attn_branch, w_out=w_out, norm_ffn=norm_ffn,
              w_gate_up=w_gate_up, w_down=w_down, norm_final=norm_final)
    mv = dict(norm_mix=m_norm_mix, w_in=m_w_in, conv_w=m_conv_w, conv_b=m_conv_b, dt_bias=m_dt_bias, a_log=m_a_log,
              d_skip=m_d_skip, ssd_norm=m_ssd_norm, w_ssd_branch=m_w_ssd_branch, w_attn_branch=m_w_attn_branch,
              w_out=m_w_out, norm_ffn=m_norm_ffn, w_gate_up=m_w_gate_up, w_down=m_w_down, norm_final=m_norm_final)
    vv = dict(norm_mix=v_norm_mix, w_in=v_w_in, conv_w=v_conv_w, conv_b=v_conv_b, dt_bias=v_dt_bias, a_log=v_a_log,
              d_skip=v_d_skip, ssd_norm=v_ssd_norm, w_ssd_branch=v_w_ssd_branch, w_attn_branch=v_w_attn_branch,
              w_out=v_w_out, norm_ffn=v_norm_ffn, w_gate_up=v_w_gate_up, w_down=v_w_down, norm_final=v_norm_final)
    order = ("norm_mix", "w_in", "conv_w", "conv_b", "dt_bias", "a_log", "d_skip", "ssd_norm", "w_ssd_branch",
             "w_attn_branch", "w_out", "norm_ffn", "w_gate_up", "w_down", "norm_final")

    smalls = [{k: wv[k][li] for k in SMALL} for li in range(DEPTH)]
    n_groups = len(GROUPS)

    first_lands = all_gather([to_wire(k, wv[k][0]) for k in GROUPS[0]], "gather_first")
    later = [(li, gi) for li in range(DEPTH) for gi in range(n_groups)][1:]
    behind_first = first_lands[1][0, 0, 0] * 0.0
    srcs = [to_wire(k, wv[k][li] + behind_first if k == "conv_w" else wv[k][li]) for li, gi in later for k in GROUPS[gi]]
    sizes = [len(GROUPS[gi]) for _, gi in later]
    w_sems, w_srcs, w_lands, token = exchange_start(srcs, [landing_zone(s) for s in srcs], sizes, False,
                                                    "gather_start", peers=SELF_AND_CHIPS)
    smalls[0]["norm_mix"] = smalls[0]["norm_mix"] + token[0, 0]
    second_leg = {}

    def forward(slot, after):
        if slot < len(later) and slot not in second_leg:
            sl = slice(sum(sizes[:slot]), sum(sizes[:slot + 1]))
            second_leg[slot] = gather_forward(w_srcs[sl], w_lands[sl], w_sems[slot], after, f"gather_forward_{slot}")

    def prefetch(li, gi, after):
        if (li, gi) == later[0]:
            forward(0, after)

    def getw(li, gi, after):
        if (li, gi) == (0, 0):
            lands = first_lands
        else:
            slot = later.index((li, gi))
            forward(slot, after)
            sems2, lands2 = second_leg[slot]
            lands = gather_finish(lands2, sems2, after, f"gather_finish_{li}_{gi}")
            forward(slot + 1, lands[0])
        w = {}
        for k, land in zip(GROUPS[gi], lands):
            w.update(full_weights(k, land))
        return w

    pending = []

    def emit(li, gi, gw):
        parts = [grads_to_wire(k, gw[k]) for k in GROUPS[gi]]
        lands = [landing_zone(p[0]) for p in parts]
        sems, p_thru, l_thru, tok = exchange_start(parts, lands, [len(parts)], True, f"grads_start_{li}_{gi}", peers=EVERYONE)
        pending.append((li, gi, sems[0], p_thru, l_thru))
        return tok[0, 0]

    loss_p, dx, gsms, g_final = local_step(x[0], loss_target[0], getw, prefetch, emit, smalls, norm_final)

    grads, deltas, new_m, new_v = {}, {}, {}, {}

    def update(k):
        if k == "w_in":
            inner = lambda t: t.transpose(2, 0, 1)
            outs = adamw_layer_inner(inner(wv[k]), shard_g[k], inner(mv[k]), inner(vv[k]), "adamw_" + k)
            grads[k], deltas[k], new_m[k], new_v[k] = (t.transpose(1, 2, 0) for t in outs)
            return outs[3]
        if k in BIG:
            grads[k] = jnp.stack([g.T if k in TRANSPOSED else g for g in shard_g[k]])
        deltas[k], new_m[k], new_v[k] = adamw(wv[k], grads[k], mv[k], vv[k], "adamw_" + k)
        return new_v[k]

    shard_g = {k: [None] * DEPTH for k in BIG}

    def collect(entry, after):
        li, gi, sems, p_thru, l_thru = entry
        recv = exchange_wait(p_thru, l_thru, sems, after, True, f"grads_wait_{li}_{gi}", peers=EVERYONE)
        for k, r in zip(GROUPS[gi], recv):
            if k == "conv_w":
                r = r.reshape(N_DEV, 1, -1)
            after = sum_parts(r, f"sum_{k}_{li}", row_major_3d=(k == "w_in"))
            shard_g[k][li] = after if k in TRANSPOSED else after.reshape(wv[k].shape[1:])
        return after

    after = dx
    for entry in pending[:-1]:
        after = collect(entry, after)
    done = [after[:1, :1].reshape(1)]
    for gi in (2, 1):
        for k in GROUPS[gi]:
            done.append(update(k).reshape(-1)[:1])

    flat = [gsms[li][k] for li in range(DEPTH) for k in SMALL] + [g_final, loss_p.reshape(-1)]
    flat.append(jnp.zeros((SMALL_ROWS * FLAT_W - SMALL_TOTAL,), F32))
    small_all = all_gather([jnp.concatenate(flat).reshape(SMALL_ROWS, FLAT_W)], "gather_small")[0]
    small_sum = sum_parts(small_all, "sum_small").reshape(-1)
    off = 0
    per_layer = {k: [] for k in SMALL}
    for li in range(DEPTH):
        for k in SMALL:
            per_layer[k].append(small_sum[off:off + SMALL_SIZE[k]])
            off += SMALL_SIZE[k]
    for k in SMALL:
        grads[k] = jnp.stack(per_layer[k])
    grads["norm_final"] = small_sum[off:off + D_MODEL]
    loss = small_sum[off + D_MODEL]
    for k in (*SMALL, "norm_final"):
        done.append(update(k).reshape(-1)[:1])

    collect(pending[-1], jnp.concatenate(done))
    for k in GROUPS[0]:
        update(k)

    return (loss, dx.reshape(x.shape), *[grads[k] for k in order], *[deltas[k] for k in order],
            *[new_m[k] for k in order], *[new_v[k] for k in order])
```

```python
import functools

import jax
import jax.numpy as jnp
from jax import lax
from jax.experimental import pallas as pl
from jax.experimental.pallas import tpu as pltpu

F32, BF16 = jnp.float32, jnp.bfloat16
SDS = jax.ShapeDtypeStruct
MESH = pl.DeviceIdType.MESH

D_MODEL = 1024
SEQ = 2048
DEPTH = 2
RMS_EPS = 1e-5
SSD_INNER = 2048
SSD_HEAD_DIM = 64
SSD_HEADS = 32
SSD_STATE = 128
SSD_GROUPS = 4
SSD_CONV = 4
SSD_CHUNK = 128
SSD_CONV_CH = 3072
ATTN_HEAD_DIM = 128
ATTN_KV_HEADS = 8
ATTN_DILATIONS = (1, 4, 16)
ATTN_N_PAT = 3
ATTN_BLOCK = 128
ATTN_OUT = 1024
ROPE_THETA = 500000.0
ROPE_DIM = 32
FFN_HIDDEN = 2816
ADAM_LR, ADAM_B1, ADAM_B2, ADAM_EPS, ADAM_WD, ADAM_STEP = 0.001, 0.9, 0.999, 1e-08, 0.01, 10

N_DEV = 8
LANES = 128
VMEM_LIMIT = 56 * 1024 * 1024
HPAD = 128
HIGHEST = lax.Precision.HIGHEST

IN_ROWS = (("w_z", 2048), ("w_xbc", 3072), ("w_dt", 32), ("w_q0", 1024), ("w_q1", 1024), ("w_q2", 1024),
           ("w_k", 1024), ("w_v", 1024), ("w_gs", 1024), ("w_ga", 1024))


def _cparams(sem):
    return pltpu.CompilerParams(dimension_semantics=sem, vmem_limit_bytes=VMEM_LIMIT)


def _sigmoid(x):
    return 0.5 * jnp.tanh(0.5 * x) + 0.5


def _silu(x):
    return x * _sigmoid(x)


def _softplus(x):
    return jnp.maximum(x, 0.0) + jnp.log(1.0 + jnp.exp(-jnp.abs(x)))


def _dot(a, b, dims=(((1,), (0,)), ((), ())), precision=None):
    return lax.dot_general(a, b, dims, precision=precision, preferred_element_type=F32)


NT = (((1,), (1,)), ((), ()))


def _bdot(a, b, dims=(((1,), (0,)), ((), ()))):
    return _dot(a.astype(BF16), b.astype(BF16), dims)


def _pick(dim, cands):
    for c in cands:
        if dim % c == 0:
            return c
    return dim


WHOLE_K_BUDGET = 40 * 1024 * 1024
RESIDENT_B_BYTES = 12 * 1024 * 1024
OUT_TILE_BYTES = 6 * 1024 * 1024


def matmul(a, b, *, name, ta=False, tb=False, out_dtype=F32, add=None):
    m, k = (a.shape[1], a.shape[0]) if ta else a.shape
    n = b.shape[0] if tb else b.shape[1]
    out_bytes = jnp.dtype(out_dtype).itemsize + (4 if add is not None else 0)
    if k * n * b.dtype.itemsize <= RESIDENT_B_BYTES:
        tn = n
        tm = next(t for t in (512, 256, 128) if m % t == 0 and t * n * out_bytes <= OUT_TILE_BYTES)
    else:
        tn = _pick(n, (1024, 1408, 512, 256, 128))
        tm = _pick(m, (512, 1408, 256, 128)) if tn == n else _pick(m, (1024, 1408, 512, 256, 128))
    tk = _pick(k, (2048, 1024, 1408, 512, 256, 128))
    whole_k_bytes = 2 * (tm * k * a.dtype.itemsize + k * tn * b.dtype.itemsize)
    if tn == n and whole_k_bytes <= WHOLE_K_BUDGET:
        tk = k
    nk = k // tk
    a_spec = pl.BlockSpec((tk, tm), lambda i, j, kk: (kk, i)) if ta else pl.BlockSpec((tm, tk), lambda i, j, kk: (i, kk))
    b_spec = pl.BlockSpec((tn, tk), lambda i, j, kk: (j, kk)) if tb else pl.BlockSpec((tk, tn), lambda i, j, kk: (kk, j))
    dims = (((0 if ta else 1,), (1 if tb else 0,)), ((), ()))
    has_add = add is not None

    def body(*refs):
        a_ref, b_ref = refs[:2]
        add_ref = refs[2] if has_add else None
        o_ref = refs[3] if has_add else refs[2]
        acc = refs[-1] if nk > 1 else None
        kk = pl.program_id(2)

        def product():
            return _dot(a_ref[...].astype(BF16), b_ref[...].astype(BF16), dims)

        def finish(r):
            if has_add:
                r = r + add_ref[...].astype(F32)
            o_ref[...] = r.astype(o_ref.dtype)

        if nk == 1:
            finish(product())
            return

        @pl.when(kk == 0)
        def _():
            acc[...] = product()

        @pl.when((kk > 0) & (kk < nk - 1))
        def _():
            acc[...] += product()

        @pl.when(kk == nk - 1)
        def _():
            finish(acc[...] + product())

    in_specs = [a_spec, b_spec]
    args = [a, b]
    if has_add:
        in_specs.append(pl.BlockSpec((tm, tn), lambda i, j, kk: (i, j)))
        args.append(add)
    return pl.pallas_call(
        body, name=name, grid=(m // tm, n // tn, nk),
        in_specs=in_specs, out_specs=pl.BlockSpec((tm, tn), lambda i, j, kk: (i, j)),
        out_shape=SDS((m, n), out_dtype), scratch_shapes=[pltpu.VMEM((tm, tn), F32)] if nk > 1 else [],
        compiler_params=_cparams(("parallel", "parallel", "arbitrary")),
    )(*args)


def matmul_rows(a, b, post, extras, outs, *, name, tb=False, tm=256):
    a_list, b_list = (list(a), list(b)) if isinstance(a, (list, tuple)) else ([a], [b])
    m = a_list[0].shape[0]
    dims = NT if tb else (((1,), (0,)), ((), ()))
    npr, ne = len(a_list), len(extras)

    def body(*refs):
        a_refs, b_refs = refs[:npr], refs[npr:2 * npr]
        e_refs, o_refs = refs[2 * npr:2 * npr + ne], refs[2 * npr + ne:]
        prods = [_dot(ar[...].astype(BF16), br[...].astype(BF16), dims) for ar, br in zip(a_refs, b_refs)]
        res = post(*prods, *[r[...] for r in e_refs])
        for r, val in zip(o_refs, res):
            r[...] = val.astype(r.dtype)

    row = lambda width: pl.BlockSpec((tm, width), lambda i: (i, 0))
    whole = lambda t: pl.BlockSpec(t.shape, lambda i: (0, 0))
    return pl.pallas_call(
        body, name=name, grid=(m // tm,),
        in_specs=[row(t.shape[1]) for t in a_list] + [whole(t) for t in b_list] + [row(e.shape[1]) for e in extras],
        out_specs=[row(c) for c, _ in outs], out_shape=[SDS((m, c), dt) for c, dt in outs],
        compiler_params=_cparams(("parallel",)),
    )(*a_list, *b_list, *extras)


def rowcall(name, fn, rows, params, row_outs, red_outs=(), tr=256):
    s = rows[0].shape[0]
    n_in = len(rows) + len(params)
    n_row = len(row_outs)

    def body(*refs):
        outs = fn(*[r[...].astype(F32) for r in refs[:n_in]])
        if not isinstance(outs, (tuple, list)):
            outs = (outs,)
        orefs = refs[n_in:]
        for r, o in zip(orefs[:n_row], outs[:n_row]):
            r[...] = o.astype(r.dtype)
        if red_outs:
            @pl.when(pl.program_id(0) == 0)
            def _():
                for r in orefs[n_row:]:
                    r[...] = jnp.zeros_like(r)
            for r, o in zip(orefs[n_row:], outs[n_row:]):
                r[...] += o.astype(F32)

    widths = [a[1] if isinstance(a, tuple) else a.shape[1] for a in rows]
    rows = [a[0] if isinstance(a, tuple) else a for a in rows]
    in_specs = [pl.BlockSpec((tr, wd), lambda i: (i, 0)) for wd in widths]
    in_specs += [pl.BlockSpec(p.shape, lambda i: (0, 0)) for p in params]
    out_specs = [pl.BlockSpec((tr, c), lambda i: (i, 0)) for c, _ in row_outs]
    out_specs += [pl.BlockSpec(shp, lambda i: (0, 0)) for shp in red_outs]
    out_shape = [SDS((s, c), dt) for c, dt in row_outs] + [SDS(shp, F32) for shp in red_outs]
    res = pl.pallas_call(
        body, name=name, grid=(s // tr,), in_specs=in_specs, out_specs=out_specs, out_shape=out_shape,
        compiler_params=_cparams(("arbitrary",) if red_outs else ("parallel",)),
    )(*rows, *params)
    return res


def _rms(x, w):
    return x * lax.rsqrt(jnp.mean(x * x, axis=-1, keepdims=True) + RMS_EPS) * w


def rms_fwd(h, w, name):
    return rowcall(name, _rms, [h], [w], [(D_MODEL, BF16)])[0]


def rms_bwd(h, du, dres, w, name):
    def fn(hb, dub, dresb, wb):
        _, vjp = jax.vjp(_rms, hb, wb)
        dh, dw = vjp(dub)
        return dh + dresb, dw
    return rowcall(name, fn, [h, du, dres], [w], [(D_MODEL, F32)], [(1, D_MODEL)])


def loss_head(h, target, w, name):
    def fn(hb, tb, wb):
        def f(hh, ww):
            err = _rms(hh, ww) - tb
            return 0.5 * jnp.sum(jnp.mean(err * err, axis=-1, keepdims=True), axis=0, keepdims=True)
        val, vjp = jax.vjp(f, hb, wb)
        dh, dw = vjp(jnp.ones((1, 1), F32))
        return dh, dw, jnp.broadcast_to(val, (1, LANES))
    return rowcall(name, fn, [h, target], [w], [(D_MODEL, F32)], [(1, D_MODEL), (1, LANES)])


def _gate(a, b, gs, ga):
    return _sigmoid(gs) * a + _sigmoid(ga) * b


def gate_fwd(yn, w_ssd, y_attn, w_attn, gs, ga, name):
    def post(pa, pb, gsb, gab):
        a, b = pa.astype(BF16), pb.astype(BF16)
        return a, b, _gate(a.astype(F32), b.astype(F32), gsb.astype(F32), gab.astype(F32))
    return matmul_rows([yn, y_attn], [w_ssd, w_attn], post, [gs, ga], [(D_MODEL, BF16)] * 3, name=name)


def gate_bwd(dh1, w_out, a, b, gs, ga, name):
    def post(dm, ab, bb, gsb, gab):
        _, vjp = jax.vjp(_gate, ab.astype(F32), bb.astype(F32), gsb.astype(F32), gab.astype(F32))
        return vjp(dm)
    return matmul_rows(dh1, w_out, post, [a, b, gs, ga], [(D_MODEL, BF16)] * 4, name=name, tb=True)


def _swiglu(gu):
    return _silu(gu[:, :FFN_HIDDEN]) * gu[:, FFN_HIDDEN:]


def gate_up_fwd(u2, w_gate_up_t, name):
    def post(acc):
        gu = acc.astype(BF16)
        return gu, _swiglu(gu.astype(F32))
    return matmul_rows(u2, w_gate_up_t, post, [], [(2 * FFN_HIDDEN, BF16), (FFN_HIDDEN, BF16)], name=name, tb=True)


def gate_up_bwd(dh, w_down, gu, name):
    def post(acc, gub):
        _, vjp = jax.vjp(_swiglu, gub.astype(F32))
        return vjp(acc.astype(BF16).astype(F32))
    return matmul_rows(dh, w_down, post, [gu], [(2 * FFN_HIDDEN, BF16)], name=name, tb=True)[0]


def _ssd_post(y, xs, z, dskip, normw):
    y = (y + dskip * xs) * _silu(z)
    gw = SSD_INNER // SSD_GROUPS
    parts = []
    for g in range(SSD_GROUPS):
        yg = y[:, g * gw:(g + 1) * gw]
        parts.append(yg * lax.rsqrt(jnp.mean(yg * yg, axis=-1, keepdims=True) + RMS_EPS))
    return jnp.concatenate(parts, axis=-1) * normw


def ssd_post_fwd(y, xc, z, dskip, normw, name):
    return rowcall(name, _ssd_post, [y, (xc, SSD_INNER), z], [dskip, normw], [(SSD_INNER, BF16)])[0]


def ssd_post_bwd(y, xc, z, dskip, normw, dyn, name):
    def fn(yb, xsb, zb, dynb, db, nb):
        _, vjp = jax.vjp(_ssd_post, yb, xsb, zb, db, nb)
        return vjp(dynb)
    return rowcall(name, fn, [y, (xc, SSD_INNER), z, dyn], [dskip, normw],
                   [(SSD_INNER, BF16)] * 3, [(1, SSD_INNER), (1, SSD_INNER)])


def _rope(t, cosf, sina, sinb):
    return t * cosf + pltpu.roll(t, LANES - ROPE_DIM // 2, 1) * sina + pltpu.roll(t, ROPE_DIM // 2, 1) * sinb


def rope_tables():
    half = ROPE_DIM // 2
    inv = ROPE_THETA ** (-jnp.arange(0, ROPE_DIM, 2, dtype=F32) / ROPE_DIM)
    ang = jnp.arange(SEQ, dtype=F32)[:, None] * inv[None, :]
    cos, sin = jnp.cos(ang), jnp.sin(ang)
    zeros = jnp.zeros((SEQ, LANES - ROPE_DIM), F32)
    z16 = jnp.zeros((SEQ, half), F32)
    cosf = jnp.concatenate([cos, cos, jnp.ones((SEQ, LANES - ROPE_DIM), F32)], axis=1)
    sina = jnp.concatenate([-sin, z16, zeros], axis=1)
    sinb = jnp.concatenate([z16, sin, zeros], axis=1)
    return cosf, sina, sinb


CONV_TC = 256


def _conv_pre(x, w, b, row):
    acc = x * w[SSD_CONV - 1:SSD_CONV, :] + b
    shifted = [x]
    for j in range(1, SSD_CONV):
        xs = jnp.where(row >= j, pltpu.roll(x, j, 0), 0.0)
        shifted.append(xs)
        acc = acc + xs * w[SSD_CONV - 1 - j:SSD_CONV - j, :]
    return acc, shifted


def conv_fwd(xbc, w, b, name):
    def body(x_ref, w_ref, b_ref, o_ref):
        row = lax.broadcasted_iota(jnp.int32, (SEQ, CONV_TC), 0)
        pre, _ = _conv_pre(x_ref[...].astype(F32), w_ref[...], b_ref[...], row)
        o_ref[...] = _silu(pre).astype(o_ref.dtype)
    return pl.pallas_call(
        body, name=name, grid=(SSD_CONV_CH // CONV_TC,),
        in_specs=[pl.BlockSpec((SEQ, CONV_TC), lambda i: (0, i)), pl.BlockSpec((SSD_CONV, CONV_TC), lambda i: (0, i)),
                  pl.BlockSpec((1, CONV_TC), lambda i: (0, i))],
        out_specs=pl.BlockSpec((SEQ, CONV_TC), lambda i: (0, i)),
        out_shape=SDS((SEQ, SSD_CONV_CH), BF16), compiler_params=_cparams(("parallel",)),
    )(xbc, w, b)


def conv_bwd(xbc, w, b, dxc, name):
    def body(x_ref, w_ref, b_ref, dy_ref, dx_ref, dw_ref, db_ref):
        row = lax.broadcasted_iota(jnp.int32, (SEQ, CONV_TC), 0)
        wv = w_ref[...]
        pre, shifted = _conv_pre(x_ref[...].astype(F32), wv, b_ref[...], row)
        sg = _sigmoid(pre)
        ds = dy_ref[...].astype(F32) * (sg * (1.0 + pre * (1.0 - sg)))
        dx = ds * wv[SSD_CONV - 1:SSD_CONV, :]
        for j in range(1, SSD_CONV):
            dsj = jnp.where(row < SEQ - j, pltpu.roll(ds, SEQ - j, 0), 0.0)
            dx = dx + dsj * wv[SSD_CONV - 1 - j:SSD_CONV - j, :]
        dx_ref[...] = dx.astype(dx_ref.dtype)
        for j in range(SSD_CONV):
            dw_ref[SSD_CONV - 1 - j:SSD_CONV - j, :] = jnp.sum(ds * shifted[j], axis=0, keepdims=True)
        db_ref[...] = jnp.sum(ds, axis=0, keepdims=True)
    return pl.pallas_call(
        body, name=name, grid=(SSD_CONV_CH // CONV_TC,),
        in_specs=[pl.BlockSpec((SEQ, CONV_TC), lambda i: (0, i)), pl.BlockSpec((SSD_CONV, CONV_TC), lambda i: (0, i)),
                  pl.BlockSpec((1, CONV_TC), lambda i: (0, i)), pl.BlockSpec((SEQ, CONV_TC), lambda i: (0, i))],
        out_specs=[pl.BlockSpec((SEQ, CONV_TC), lambda i: (0, i)), pl.BlockSpec((SSD_CONV, CONV_TC), lambda i: (0, i)),
                   pl.BlockSpec((1, CONV_TC), lambda i: (0, i))],
        out_shape=[SDS((SEQ, SSD_CONV_CH), BF16), SDS((SSD_CONV, SSD_CONV_CH), F32), SDS((1, SSD_CONV_CH), F32)],
        compiler_params=_cparams(("parallel",)),
    )(xbc, w, b, dxc)


N_CHUNKS = SEQ // SSD_CHUNK
N_PAIRS = SSD_HEADS // 2
PAIRS_PER_GROUP = N_PAIRS // SSD_GROUPS
B_OFF = SSD_INNER
C_OFF = SSD_INNER + SSD_GROUPS * SSD_STATE


def _ssd_prefix(dtr, dtr_t, dtb, dtb_t, alog, alog_t):
    ln = SSD_CHUNK
    dt = _softplus(dtr + dtb)
    dt_t = _softplus(dtr_t + dtb_t)
    dta = dt * (-jnp.exp(alog))
    dta_t = dt_t * (-jnp.exp(alog_t))
    r = lax.broadcasted_iota(jnp.int32, (ln, ln), 0)
    c = lax.broadcasted_iota(jnp.int32, (ln, ln), 1)
    a_cum = _dot((r >= c).astype(F32), dta, precision=HIGHEST)
    a_cum_t = _dot(dta_t, (r <= c).astype(F32), precision=HIGHEST)
    a_last = jnp.sum(dta_t, axis=1, keepdims=True)
    return dt, a_cum, a_cum_t, a_last


def _bein(spec, a, b):
    return jnp.einsum(spec, a.astype(BF16), b.astype(BF16), preferred_element_type=F32)


SSD_GROUPS_PER_BATCH = 4


def _ssd_group(xs3, bgs, cgs, h3, dt, a_cum, a_cum_t, a_last, *, groups):
    ln = SSD_CHUNK
    lane = lax.broadcasted_iota(jnp.int32, (ln, LANES), 1)
    sub = lax.broadcasted_iota(jnp.int32, (LANES, SSD_STATE), 0)
    row = lax.broadcasted_iota(jnp.int32, (ln, ln), 0)
    col = lax.broadcasted_iota(jnp.int32, (ln, ln), 1)
    lo = lane < SSD_HEAD_DIM
    causal = row >= col
    m_lo, m_hi, dts, acs, lasts, cds, cg3, bg3 = [], [], [], [], [], [], [], []
    for g, bg, cg in zip(groups, bgs, cgs):
        cb = _bdot(cg, bg, NT)
        for j in range(PAIRS_PER_GROUP):
            e0 = 2 * (g * PAIRS_PER_GROUP + j)
            e1 = e0 + 1
            c0, c1 = a_cum[:, e0:e0 + 1], a_cum[:, e1:e1 + 1]
            r0, r1 = a_cum_t[e0:e0 + 1, :], a_cum_t[e1:e1 + 1, :]
            l0, l1 = a_last[e0:e0 + 1, :], a_last[e1:e1 + 1, :]
            m_lo.append(cb * jnp.exp(jnp.where(causal, c0 - r0, -jnp.inf)))
            m_hi.append(cb * jnp.exp(jnp.where(causal, c1 - r1, -jnp.inf)))
            dts.append(jnp.where(lo, dt[:, e0:e0 + 1], dt[:, e1:e1 + 1]))
            acs.append(jnp.where(lo, c0, c1))
            lasts.append(jnp.where(lo, l0, l1))
            cds.append(jnp.exp(jnp.where(sub < SSD_HEAD_DIM, l0, l1)))
            cg3.append(cg)
            bg3.append(bg)
    xd = xs3 * jnp.stack(dts)
    acum = jnp.stack(acs)
    y = (_bein("pls,psq->plq", jnp.stack(m_lo), jnp.where(lo[None], xd, 0.0))
         + _bein("pls,psq->plq", jnp.stack(m_hi), jnp.where(lo[None], 0.0, xd)))
    y = y + _bein("pln,pqn->plq", jnp.stack(cg3), h3) * jnp.exp(acum)
    st = _bein("plq,pln->pqn", xd * jnp.exp(jnp.stack(lasts) - acum), jnp.stack(bg3))
    h_out = h3 * jnp.stack(cds) + st
    return y, h_out


def _group_slabs(groups):
    pairs = [g * PAIRS_PER_GROUP + j for g in groups for j in range(PAIRS_PER_GROUP)]
    return [slice(p * LANES, (p + 1) * LANES) for p in pairs]


def _group_batches():
    return [tuple(range(g, g + SSD_GROUPS_PER_BATCH)) for g in range(0, SSD_GROUPS, SSD_GROUPS_PER_BATCH)]


def _bc_of(xc_ref, g):
    return (xc_ref[:, B_OFF + g * SSD_STATE:B_OFF + (g + 1) * SSD_STATE].astype(F32),
            xc_ref[:, C_OFF + g * SSD_STATE:C_OFF + (g + 1) * SSD_STATE].astype(F32))


def _ssd_in_specs(chunk_of):
    return [
        pl.BlockSpec((SSD_CHUNK, SSD_CONV_CH), lambda i: (chunk_of(i), 0)),
        pl.BlockSpec((SSD_CHUNK, HPAD), lambda i: (chunk_of(i), 0)),
        pl.BlockSpec((HPAD, SSD_CHUNK), lambda i: (0, chunk_of(i))),
        pl.BlockSpec((1, HPAD), lambda i: (0, 0)), pl.BlockSpec((HPAD, 1), lambda i: (0, 0)),
        pl.BlockSpec((1, HPAD), lambda i: (0, 0)), pl.BlockSpec((HPAD, 1), lambda i: (0, 0)),
    ]


def ssd_fwd(xc, dtr, dtr_t, dtb, dtb_t, alog, alog_t, name):
    def body(xc_ref, dtr_ref, dtrt_ref, dtb_ref, dtbt_ref, al_ref, alt_ref, y_ref, hs_ref, h_scr):
        @pl.when(pl.program_id(0) == 0)
        def _():
            h_scr[...] = jnp.zeros_like(h_scr)

        hs_ref[0] = h_scr[...]
        dt, a_cum, a_cum_t, a_last = _ssd_prefix(dtr_ref[...], dtrt_ref[...], dtb_ref[...], dtbt_ref[...],
                                                  al_ref[...], alt_ref[...])
        for groups in _group_batches():
            slabs = _group_slabs(groups)
            bgs, cgs = zip(*[_bc_of(xc_ref, g) for g in groups])
            xs3 = jnp.stack([xc_ref[:, sl] for sl in slabs]).astype(F32)
            h3 = jnp.stack([h_scr[sl, :] for sl in slabs])
            y3, h3_out = _ssd_group(xs3, bgs, cgs, h3, dt, a_cum, a_cum_t, a_last, groups=groups)
            for j, sl in enumerate(slabs):
                y_ref[:, sl] = y3[j].astype(y_ref.dtype)
                h_scr[sl, :] = h3_out[j]

    return pl.pallas_call(
        body, name=name, grid=(N_CHUNKS,), in_specs=_ssd_in_specs(lambda i: i),
        out_specs=[pl.BlockSpec((SSD_CHUNK, SSD_INNER), lambda i: (i, 0)),
                   pl.BlockSpec((1, SSD_INNER, SSD_STATE), lambda i: (i, 0, 0))],
        out_shape=[SDS((SEQ, SSD_INNER), BF16), SDS((N_CHUNKS, SSD_INNER, SSD_STATE), F32)],
        scratch_shapes=[pltpu.VMEM((SSD_INNER, SSD_STATE), F32)],
        compiler_params=_cparams(("arbitrary",)),
    )(xc, dtr, dtr_t, dtb, dtb_t, alog, alog_t)


def ssd_bwd(xc, dtr, dtr_t, dtb, dtb_t, alog, alog_t, hs, dy, dxs_extra, name):
    rev = lambda i: N_CHUNKS - 1 - i

    def body(xc_ref, dtr_ref, dtrt_ref, dtb_ref, dtbt_ref, al_ref, alt_ref, hs_ref, dy_ref, dxe_ref,
             dxc_ref, ddtr_ref, ddtrt_ref, ddtb_ref, ddtbt_ref, dal_ref, dalt_ref, dh_scr):
        @pl.when(pl.program_id(0) == 0)
        def _():
            dh_scr[...] = jnp.zeros_like(dh_scr)
            for r in (ddtb_ref, ddtbt_ref, dal_ref, dalt_ref):
                r[...] = jnp.zeros_like(r)

        prefix_in = (dtr_ref[...], dtrt_ref[...], dtb_ref[...], dtbt_ref[...], al_ref[...], alt_ref[...])
        (dt, a_cum, a_cum_t, a_last), prefix_vjp = jax.vjp(_ssd_prefix, *prefix_in)
        d_dt = jnp.zeros_like(dt)
        d_acum = jnp.zeros_like(a_cum)
        d_acum_t = jnp.zeros_like(a_cum_t)
        d_alast = jnp.zeros_like(a_last)
        for groups in _group_batches():
            slabs = _group_slabs(groups)
            bgs, cgs = zip(*[_bc_of(xc_ref, g) for g in groups])
            xs3 = jnp.stack([xc_ref[:, sl] for sl in slabs]).astype(F32)
            h3 = jnp.stack([hs_ref[0, sl, :] for sl in slabs])
            _, vjp = jax.vjp(functools.partial(_ssd_group, groups=groups), xs3, bgs, cgs, h3, dt, a_cum, a_cum_t, a_last)
            dy3 = jnp.stack([dy_ref[:, sl] for sl in slabs]).astype(F32)
            dh3 = jnp.stack([dh_scr[sl, :] for sl in slabs])
            dxs3, d_bgs, d_cgs, dh3_in, ddt, dac, dact, dal = vjp((dy3, dh3))
            for j, sl in enumerate(slabs):
                dxc_ref[:, sl] = (dxs3[j] + dxe_ref[:, sl].astype(F32)).astype(dxc_ref.dtype)
                dh_scr[sl, :] = dh3_in[j]
            d_dt, d_acum, d_acum_t, d_alast = d_dt + ddt, d_acum + dac, d_acum_t + dact, d_alast + dal
            for g, d_bg, d_cg in zip(groups, d_bgs, d_cgs):
                dxc_ref[:, B_OFF + g * SSD_STATE:B_OFF + (g + 1) * SSD_STATE] = d_bg.astype(dxc_ref.dtype)
                dxc_ref[:, C_OFF + g * SSD_STATE:C_OFF + (g + 1) * SSD_STATE] = d_cg.astype(dxc_ref.dtype)
        g_dtr, g_dtrt, g_dtb, g_dtbt, g_al, g_alt = prefix_vjp((d_dt, d_acum, d_acum_t, d_alast))
        ddtr_ref[...] = g_dtr
        ddtrt_ref[...] = g_dtrt
        ddtb_ref[...] += g_dtb
        ddtbt_ref[...] += g_dtbt
        dal_ref[...] += g_al
        dalt_ref[...] += g_alt

    in_specs = _ssd_in_specs(rev) + [
        pl.BlockSpec((1, SSD_INNER, SSD_STATE), lambda i: (rev(i), 0, 0)),
        pl.BlockSpec((SSD_CHUNK, SSD_INNER), lambda i: (rev(i), 0)),
        pl.BlockSpec((SSD_CHUNK, SSD_INNER), lambda i: (rev(i), 0)),
    ]
    out_specs = [
        pl.BlockSpec((SSD_CHUNK, SSD_CONV_CH), lambda i: (rev(i), 0)),
        pl.BlockSpec((SSD_CHUNK, HPAD), lambda i: (rev(i), 0)),
        pl.BlockSpec((HPAD, SSD_CHUNK), lambda i: (0, rev(i))),
        pl.BlockSpec((1, HPAD), lambda i: (0, 0)), pl.BlockSpec((HPAD, 1), lambda i: (0, 0)),
        pl.BlockSpec((1, HPAD), lambda i: (0, 0)), pl.BlockSpec((HPAD, 1), lambda i: (0, 0)),
    ]
    out_shape = [SDS((SEQ, SSD_CONV_CH), BF16), SDS((SEQ, HPAD), F32), SDS((HPAD, SEQ), F32),
                 SDS((1, HPAD), F32), SDS((HPAD, 1), F32), SDS((1, HPAD), F32), SDS((HPAD, 1), F32)]
    return pl.pallas_call(
        body, name=name, grid=(N_CHUNKS,), in_specs=in_specs, out_specs=out_specs, out_shape=out_shape,
        scratch_shapes=[pltpu.VMEM((SSD_INNER, SSD_STATE), F32)],
        compiler_params=_cparams(("arbitrary",)),
    )(xc, dtr, dtr_t, dtb, dtb_t, alog, alog_t, hs, dy, dxs_extra)


ATTN_SCALE = ATTN_HEAD_DIM ** -0.5


UNITS_PER_PATTERN = SEQ // ATTN_BLOCK
ATTN_BATCH_FWD = 8
ATTN_BATCH_BWD = 16


def _for_unit_batches(batch, per_trip):
    for g, d in enumerate(ATTN_DILATIONS):
        nb = UNITS_PER_PATTERN // d
        span = d * ATTN_BLOCK

        def trip(t, carry, g=g, d=d, nb=nb, span=span):
            units = []
            for j in range(per_trip):
                i = t * per_trip + j
                r = i >> (nb.bit_length() - 1)
                n = i & (nb - 1)
                start = r + n * span
                prev = jnp.where(n > 0, start - span, start)
                units.append((pl.ds(start, ATTN_BLOCK, stride=d), pl.ds(prev, ATTN_BLOCK, stride=d), n > 0))
            batch(g, units)
            return carry
        lax.fori_loop(0, UNITS_PER_PATTERN // per_trip, trip, 0)


def _unit_operands(units, q_scr, k_scr, v_scr):
    def pair(scr, rows, prows):
        return jnp.concatenate([scr[prows, :], scr[rows, :]], axis=0)
    qb = jnp.stack([q_scr[rows, :] for rows, _, _ in units]).astype(BF16)
    kb = jnp.stack([pair(k_scr, rows, prows) for rows, prows, _ in units]).astype(BF16)
    vb = jnp.stack([pair(v_scr, rows, prows) for rows, prows, _ in units]).astype(BF16)
    return qb, kb, vb


def _unit_scores(qb, kb, units):
    s = jnp.einsum("bqd,bkd->bqk", qb, kb, preferred_element_type=F32) * ATTN_SCALE
    qi = lax.broadcasted_iota(jnp.int32, (ATTN_BLOCK, 2 * ATTN_BLOCK), 0)
    kj = lax.broadcasted_iota(jnp.int32, (ATTN_BLOCK, 2 * ATTN_BLOCK), 1)
    own = (kj >= ATTN_BLOCK) & (kj - ATTN_BLOCK <= qi)
    before = (kj < ATTN_BLOCK) & (kj >= qi)
    keep = jnp.stack([own | (before & has_prev) for _, _, has_prev in units])
    return jnp.where(keep, s, -jnp.inf)


def _head_specs(n_q_groups):
    blk = (SEQ, ATTN_HEAD_DIM)
    q_specs = [pl.BlockSpec(blk, functools.partial(lambda h, g: (0, g * ATTN_KV_HEADS + h), g=g)) for g in range(n_q_groups)]
    head = pl.BlockSpec(blk, lambda h: (0, h))
    table = pl.BlockSpec(blk, lambda h: (0, 0))
    return q_specs, head, table


def attn_fwd(q, k, v, tabs, name):
    q_specs, head, table = _head_specs(ATTN_N_PAT)

    def body(q0_ref, q1_ref, q2_ref, k_ref, v_ref, c_ref, sa_ref, sb_ref, y_ref, lse_ref, *scr):
        qs, og, ls, ks, vs = scr[0:3], scr[3:6], scr[6:9], scr[9], scr[10]
        c, sa, sb = c_ref[...], sa_ref[...], sb_ref[...]
        for g, q_ref in enumerate((q0_ref, q1_ref, q2_ref)):
            qs[g][...] = _rope(q_ref[...].astype(F32), c, sa, sb)
        ks[...] = _rope(k_ref[...].astype(F32), c, sa, sb)
        vs[...] = v_ref[...].astype(F32)

        def batch(g, units):
            qb, kb, vb = _unit_operands(units, qs[g], ks, vs)
            s = _unit_scores(qb, kb, units)
            m = jnp.max(s, axis=2, keepdims=True)
            p = jnp.exp(s - m)
            l = jnp.sum(p, axis=2, keepdims=True)
            o = jnp.einsum("bqk,bkd->bqd", p.astype(BF16), vb, preferred_element_type=F32) / l
            lse_b = m + jnp.log(l)
            for j, (rows, _, _) in enumerate(units):
                og[g][rows, :] = o[j]
                ls[g][rows, :] = jnp.broadcast_to(lse_b[j], (ATTN_BLOCK, LANES))

        _for_unit_batches(batch, ATTN_BATCH_FWD)
        l0, l1, l2 = ls[0][...], ls[1][...], ls[2][...]
        m = jnp.maximum(jnp.maximum(l0, l1), l2)
        e0, e1, e2 = jnp.exp(l0 - m), jnp.exp(l1 - m), jnp.exp(l2 - m)
        den = e0 + e1 + e2
        y_ref[...] = ((e0 * og[0][...] + e1 * og[1][...] + e2 * og[2][...]) / den).astype(y_ref.dtype)
        lse_ref[...] = m + jnp.log(den)

    blk = (SEQ, ATTN_HEAD_DIM)
    return pl.pallas_call(
        body, name=name, grid=(ATTN_KV_HEADS,), in_specs=[*q_specs, head, head, table, table, table],
        out_specs=[head, head], out_shape=[SDS((SEQ, ATTN_OUT), BF16), SDS((SEQ, ATTN_OUT), F32)],
        scratch_shapes=[pltpu.VMEM(blk, F32)] * (3 * ATTN_N_PAT + 2),
        compiler_params=_cparams(("parallel",)),
    )(q, q, q, k, v, *tabs)


def attn_bwd(q, k, v, tabs, y, lse, dy, name):
    q_specs, head, table = _head_specs(ATTN_N_PAT)

    def body(q0_ref, q1_ref, q2_ref, k_ref, v_ref, c_ref, sa_ref, sb_ref, y_ref, lse_ref, dy_ref,
             dq0_ref, dq1_ref, dq2_ref, dk_ref, dv_ref, *scr):
        qs, dqs, ks, dks, dd, dvs, vs = scr[0:3], scr[3:6], scr[6], scr[7], scr[8], scr[9], scr[10]
        c, sa, sb = c_ref[...], sa_ref[...], sb_ref[...]
        for g, q_ref in enumerate((q0_ref, q1_ref, q2_ref)):
            qs[g][...] = _rope(q_ref[...].astype(F32), c, sa, sb)
        ks[...] = _rope(k_ref[...].astype(F32), c, sa, sb)
        vs[...] = v_ref[...].astype(F32)
        dks[...] = jnp.zeros_like(dks)
        dvs[...] = jnp.zeros_like(dvs)
        dyv = dy_ref[...]
        dd[...] = jnp.broadcast_to(jnp.sum(dyv * y_ref[...].astype(F32), axis=1, keepdims=True), dd.shape)

        def batch(g, units):
            qb, kb, vb = _unit_operands(units, qs[g], ks, vs)
            dob = jnp.stack([dy_ref[rows, :] for rows, _, _ in units]).astype(BF16)
            lse_b = jnp.stack([lse_ref[rows, :][:, 0:1] for rows, _, _ in units])
            dsum_b = jnp.stack([dd[rows, :][:, 0:1] for rows, _, _ in units])
            p = jnp.exp(_unit_scores(qb, kb, units) - lse_b)
            dp = jnp.einsum("bqd,bkd->bqk", dob, vb, preferred_element_type=F32)
            ds = (p * (dp - dsum_b) * ATTN_SCALE).astype(BF16)
            dq = jnp.einsum("bqk,bkd->bqd", ds, kb, preferred_element_type=F32)
            dk = jnp.einsum("bqk,bqd->bkd", ds, qb, preferred_element_type=F32)
            dv = jnp.einsum("bqk,bqd->bkd", p.astype(BF16), dob, preferred_element_type=F32)
            for j, (rows, prows, _) in enumerate(units):
                dqs[g][rows, :] = dq[j]
                dks[prows, :] += dk[j, :ATTN_BLOCK]
                dks[rows, :] += dk[j, ATTN_BLOCK:]
                dvs[prows, :] += dv[j, :ATTN_BLOCK]
                dvs[rows, :] += dv[j, ATTN_BLOCK:]

        _for_unit_batches(batch, ATTN_BATCH_BWD)
        for g, dq_ref in enumerate((dq0_ref, dq1_ref, dq2_ref)):
            dq_ref[...] = _rope(dqs[g][...], c, -sa, -sb).astype(dq_ref.dtype)
        dk_ref[...] = _rope(dks[...], c, -sa, -sb).astype(dk_ref.dtype)
        dv_ref[...] = dvs[...].astype(dv_ref.dtype)

    blk = (SEQ, ATTN_HEAD_DIM)
    out = SDS((SEQ, ATTN_OUT), BF16)
    return pl.pallas_call(
        body, name=name, grid=(ATTN_KV_HEADS,), in_specs=[*q_specs, head, head, table, table, table, head, head, head],
        out_specs=[head] * 5, out_shape=[out] * 5,
        scratch_shapes=[pltpu.VMEM(blk, F32)] * (2 * ATTN_N_PAT + 5),
        compiler_params=_cparams(("parallel",)),
    )(q, q, q, k, v, *tabs, y, lse, dy)


def layer_fwd(h, getw, prefetch, small, tabs, li):
    n = f"l{li}_"
    sv = {}
    w = dict(getw(0, h))
    u = rms_fwd(h, small["norm_mix"], n + "rms_mix")
    z = matmul(u, w["w_z"], name=n + "mm_z", tb=True, out_dtype=BF16)
    prefetch(1, z)
    xbc = matmul(u, w["w_xbc"], name=n + "mm_xbc", tb=True, out_dtype=BF16)
    dtr = matmul(u, w["w_dt"], name=n + "mm_dt", tb=True)
    q = matmul(u, w["w_q"], name=n + "mm_q", tb=True, out_dtype=BF16)
    k = matmul(u, w["w_k"], name=n + "mm_k", tb=True, out_dtype=BF16)
    v = matmul(u, w["w_v"], name=n + "mm_v", tb=True, out_dtype=BF16)
    gs = matmul(u, w["w_gs"], name=n + "mm_gs", tb=True, out_dtype=BF16)
    ga = matmul(u, w["w_ga"], name=n + "mm_ga", tb=True, out_dtype=BF16)
    xc = conv_fwd(xbc, w["conv_w"], small["conv_b"], n + "conv")
    dtr_t = dtr.T
    y_ssd, hs = ssd_fwd(xc, dtr, dtr_t, small["dt_bias"], small["dt_bias"].T, small["a_log"], small["a_log"].T, n + "ssd")
    yn = ssd_post_fwd(y_ssd, xc, z, small["d_skip_x"], small["ssd_norm"], n + "ssd_post")
    y_attn, lse = attn_fwd(q, k, v, tabs, n + "attn")
    w.update(getw(1, y_ssd))
    a, b, merged = gate_fwd(yn, w["w_ssd_branch"], y_attn, w["w_attn_branch"], gs, ga, n + "mm_ab_gate")
    h1 = matmul(merged, w["w_out"], name=n + "mm_o", add=h)
    w.update(getw(2, h1))
    u2 = rms_fwd(h1, small["norm_ffn"], n + "rms_ffn")
    gu, act = gate_up_fwd(u2, w["w_gate_up"], n + "mm_gu_swiglu")
    h2 = matmul(act, w["w_down"], name=n + "mm_down", add=h1)
    sv.update(h=h, u=u, z=z, xbc=xbc, dtr=dtr, dtr_t=dtr_t, gs=gs, ga=ga, xc=xc, y_ssd=y_ssd, hs=hs, yn=yn,
              q=q, k=k, v=v, y_attn=y_attn, lse=lse, a=a, b=b, merged=merged, h1=h1, u2=u2, gu=gu, act=act, w=w)
    return h2, sv


def layer_bwd(dh, sv, small, tabs, li, emit):
    n = f"l{li}_b_"
    w = sv["w"]
    gw, gsm = {}, {}
    gw["w_down"] = matmul(sv["act"], dh, name=n + "mm_dwdown", ta=True, out_dtype=BF16)
    dgu = gate_up_bwd(dh, w["w_down"], sv["gu"], n + "mm_dact_swiglu")
    gw["w_gate_up"] = matmul(dgu, sv["u2"], name=n + "mm_dwgu", ta=True, out_dtype=BF16)
    tok = emit(2, gw)
    du2 = matmul(dgu, w["w_gate_up"], name=n + "mm_du2")
    dh1, gsm["norm_ffn"] = rms_bwd(sv["h1"], du2, dh, small["norm_ffn"] + tok, n + "rms_ffn")
    gw["w_out"] = matmul(sv["merged"], dh1, name=n + "mm_dwo", ta=True, out_dtype=BF16)
    da, db, dgs, dga = gate_bwd(dh1, w["w_out"], sv["a"], sv["b"], sv["gs"], sv["ga"], n + "mm_dmerged_gate")
    gw["w_ssd_branch"] = matmul(sv["yn"], da, name=n + "mm_dwa", ta=True, out_dtype=BF16)
    gw["w_attn_branch"] = matmul(sv["y_attn"], db, name=n + "mm_dwb", ta=True, out_dtype=BF16)
    tok = emit(1, gw)
    dyn = matmul(da, w["w_ssd_branch"], name=n + "mm_dyn", tb=True, out_dtype=BF16)
    dyattn = matmul(db, w["w_attn_branch"], name=n + "mm_dyattn", tb=True)
    dy_ssd, dxs_extra, dz, gsm["d_skip_x"], gsm["ssd_norm"] = ssd_post_bwd(
        sv["y_ssd"], sv["xc"], sv["z"], small["d_skip_x"] + tok, small["ssd_norm"], dyn, n + "ssd_post")
    dxc, ddtr, ddtr_t, ddtb, ddtb_t, dal, dal_t = ssd_bwd(
        sv["xc"], sv["dtr"], sv["dtr_t"], small["dt_bias"], small["dt_bias"].T, small["a_log"], small["a_log"].T,
        sv["hs"], dy_ssd, dxs_extra, n + "ssd")
    ddtr = (ddtr + ddtr_t.T).astype(BF16)
    gsm["dt_bias"] = ddtb + ddtb_t.T
    gsm["a_log"] = dal + dal_t.T
    dxbc, gw["conv_w"], gsm["conv_b"] = conv_bwd(sv["xbc"], w["conv_w"], small["conv_b"], dxc, n + "conv")
    dq0, dq1, dq2, dk, dv = attn_bwd(sv["q"], sv["k"], sv["v"], tabs, sv["y_attn"], sv["lse"], dyattn, n + "attn")
    u = sv["u"]
    segs = [("w_z", dz), ("w_xbc", dxbc), ("w_dt", ddtr), ("w_q0", dq0), ("w_q1", dq1), ("w_q2", dq2),
            ("w_k", dk), ("w_v", dv), ("w_gs", dgs), ("w_ga", dga)]
    gin = [matmul(dseg, u, name=n + "mm_d" + key, ta=True, out_dtype=BF16) for key, dseg in segs]
    gin[2] = gin[2][:SSD_HEADS]
    gw["w_in"] = jnp.concatenate(gin, axis=0)
    tok = emit(0, gw)
    du = jnp.zeros((SEQ, D_MODEL), F32) + tok
    for si, (key, dseg) in enumerate(segs):
        du = matmul(dseg, w[key], name=n + "mm_du_" + key, add=du)
        if si == 2:
            du = du + emit(-1, du)
    dh0, gsm["norm_mix"] = rms_bwd(sv["h"], du, dh1, small["norm_mix"] + tok, n + "rms_mix")
    return dh0, gsm


def _my_place():
    return lax.axis_index("x"), lax.axis_index("y"), lax.axis_index("c")


def _flip(place, k):
    x, y, c = place
    return (1 - x if k & 4 else x, 1 - y if k & 2 else y, 1 - c if k & 1 else c)


def _index(place):
    return 4 * place[0] + 2 * place[1] + place[2]


ANY = pl.BlockSpec(memory_space=pl.ANY)
CHIP_FLIPS = (4, 2, 6)
SELF_AND_CHIPS = (0,) + CHIP_FLIPS


def all_gather(xs, name):
    na = len(xs)

    def body(*refs):
        x_refs, o_refs = refs[:na], refs[na:2 * na]
        send_sems, recv_sems, local_sems = refs[2 * na:]
        me = _my_place()
        sibling = _flip(me, 1)
        chips = [_flip(me, f) for f in CHIP_FLIPS]

        def copy(a, kk, block, to, src=None):
            dst = o_refs[a].at[_index(block)]
            return pltpu.make_async_remote_copy(
                src_ref=dst if src is None else src, dst_ref=dst, send_sem=send_sems.at[a, kk],
                recv_sem=recv_sems.at[a, kk], device_id=to, device_id_type=MESH)

        mine = [pltpu.make_async_copy(x_refs[a], o_refs[a].at[_index(me)], local_sems.at[a]) for a in range(na)]
        for cp in mine:
            cp.start()
        first = []
        for j, chip in enumerate(chips):
            first += [copy(a, 1 + j, me, chip, src=x_refs[a]) for a in range(na)]
        first += [copy(a, 0, me, sibling, src=x_refs[a]) for a in range(na)]
        for cp in first:
            cp.start()
        passed = []
        for j, chip in enumerate(chips):
            for a in range(na):
                copy(a, 1 + j, chip, me).wait_recv()
                cp = copy(a, 4 + j, chip, sibling)
                cp.start()
                passed.append(cp)
        for a in range(na):
            copy(a, 0, sibling, me).wait_recv()
        for j, chip in enumerate(chips):
            for a in range(na):
                copy(a, 4 + j, _flip(chip, 1), me).wait_recv()
        for cp in first + passed:
            cp.wait_send()
        for cp in mine:
            cp.wait()

    return pl.pallas_call(
        body, name=name, in_specs=[ANY] * na, out_specs=[ANY] * na,
        out_shape=[SDS((N_DEV,) + t.shape, t.dtype) for t in xs],
        scratch_shapes=[pltpu.SemaphoreType.DMA((na, N_DEV - 1)), pltpu.SemaphoreType.DMA((na, N_DEV - 1)),
                        pltpu.SemaphoreType.DMA((na,))],
    )(*xs)


HBM = pl.BlockSpec(memory_space=pltpu.HBM)
SEM = pl.BlockSpec(memory_space=pltpu.SEMAPHORE)
EFFECT = pltpu.SideEffectType.DATAFLOW_SIDE_EFFECTING


def _split_copy(src_ref, land_ref, send_sem, recv_sem, me, kk, scatter, landed_from_peer):
    peer = _flip(me, kk)
    src = src_ref.at[_index(peer)] if scatter else src_ref
    dst = land_ref.at[_index(peer if landed_from_peer else me)]
    return pltpu.make_async_remote_copy(src_ref=src, dst_ref=dst, send_sem=send_sem, recv_sem=recv_sem,
                                        device_id=peer, device_id_type=MESH)


ALL_PEERS = tuple(range(1, N_DEV))
EVERYONE = (0,) + ALL_PEERS


def exchange_start(srcs, lands, group_sizes, scatter, name, peers=ALL_PEERS):
    na, ng = len(srcs), len(group_sizes)

    def body(*refs):
        s_refs, l_refs = refs[:na], refs[na:2 * na]
        sems = refs[2 * na:2 * na + 2 * ng]
        token = refs[-1]
        me = _my_place()
        a = 0
        for gi, gsz in enumerate(group_sizes):
            for j in range(gsz):
                for pi, kk in enumerate(peers):
                    slot = j * len(peers) + pi
                    _split_copy(s_refs[a], l_refs[a], sems[2 * gi].at[slot], sems[2 * gi + 1].at[slot],
                                me, kk, scatter, False).start()
                a += 1
        token[...] = jnp.zeros_like(token)

    sem_shapes = []
    for gsz in group_sizes:
        sem_shapes += [pltpu.SemaphoreType.DMA((gsz * len(peers),))] * 2
    ins = [pltpu.with_memory_space_constraint(t, pltpu.HBM) for t in (*srcs, *lands)]
    res = pl.pallas_call(
        body, name=name, in_specs=[HBM] * (2 * na),
        out_specs=[SEM] * (2 * ng) + [HBM] * (2 * na) + [pl.BlockSpec(memory_space=pltpu.VMEM)],
        out_shape=sem_shapes + [pltpu.HBM(t.shape, t.dtype) for t in ins] + [SDS((8, LANES), F32)],
        input_output_aliases={i: 2 * ng + i for i in range(2 * na)},
        compiler_params=pltpu.CompilerParams(has_side_effects=EFFECT),
    )(*ins)
    sems = [(res[2 * gi], res[2 * gi + 1]) for gi in range(ng)]
    thru = res[2 * ng:2 * ng + 2 * na]
    return sems, thru[:na], thru[na:], res[-1]


def _wait_split_copies(s_refs, l_refs, send_sems, recv_sems, scatter, peers):
    me = _my_place()
    for j in range(len(s_refs)):
        for pi, kk in enumerate(peers):
            slot = j * len(peers) + pi
            cp = _split_copy(s_refs[j], l_refs[j], send_sems.at[slot], recv_sems.at[slot], me, kk, scatter, True)
            cp.wait_send()
            cp.wait_recv()


def exchange_wait(srcs, lands, sems, after, scatter, name, peers=ALL_PEERS):
    n = len(srcs)

    def body(*refs):
        s_refs, l_refs = refs[:n], refs[n:2 * n]
        _wait_split_copies(s_refs, l_refs, refs[2 * n], refs[2 * n + 1], scatter, peers)

    res = pl.pallas_call(
        body, name=name, in_specs=[HBM] * (2 * n) + [SEM, SEM, ANY], out_specs=[HBM] * (2 * n),
        out_shape=[pltpu.HBM(t.shape, t.dtype) for t in (*srcs, *lands)],
        input_output_aliases={i: i for i in range(2 * n)},
        compiler_params=pltpu.CompilerParams(has_side_effects=EFFECT),
    )(*srcs, *lands, sems[0], sems[1], after)
    return res[n:]


def _sibling_copies(l_refs, send_sems, recv_sems, arriving):
    me = _my_place()
    sibling = _flip(me, 1)
    held = [me] + [_flip(me, f) for f in CHIP_FLIPS]
    copies = []
    for j, land in enumerate(l_refs):
        for bi, place in enumerate(held):
            blk = land.at[_index(_flip(place, 1) if arriving else place)]
            slot = j * len(held) + bi
            copies.append(pltpu.make_async_remote_copy(src_ref=blk, dst_ref=blk, send_sem=send_sems.at[slot],
                                                       recv_sem=recv_sems.at[slot], device_id=sibling, device_id_type=MESH))
    return copies


def gather_forward(srcs, lands, sems, after, name):
    n = len(srcs)

    def body(*refs):
        s_refs, l_refs = refs[:n], refs[n:2 * n]
        _wait_split_copies(s_refs, l_refs, refs[2 * n], refs[2 * n + 1], False, SELF_AND_CHIPS)
        for cp in _sibling_copies(l_refs, refs[2 * n + 3], refs[2 * n + 4], False):
            cp.start()

    n_slots = n * (1 + len(CHIP_FLIPS))
    res = pl.pallas_call(
        body, name=name, in_specs=[HBM] * (2 * n) + [SEM, SEM, ANY],
        out_specs=[SEM, SEM] + [HBM] * (2 * n),
        out_shape=[pltpu.SemaphoreType.DMA((n_slots,))] * 2 + [pltpu.HBM(t.shape, t.dtype) for t in (*srcs, *lands)],
        input_output_aliases={i: 2 + i for i in range(2 * n)},
        compiler_params=pltpu.CompilerParams(has_side_effects=EFFECT),
    )(*srcs, *lands, sems[0], sems[1], after)
    return (res[0], res[1]), res[2 + n:]


def gather_finish(lands, sems, after, name):
    n = len(lands)

    def body(*refs):
        l_refs = refs[:n]
        for cp in _sibling_copies(l_refs, refs[n], refs[n + 1], True):
            cp.wait_send()
            cp.wait_recv()

    return pl.pallas_call(
        body, name=name, in_specs=[HBM] * n + [SEM, SEM, ANY], out_specs=[HBM] * n,
        out_shape=[pltpu.HBM(t.shape, t.dtype) for t in lands],
        input_output_aliases={i: i for i in range(n)},
        compiler_params=pltpu.CompilerParams(has_side_effects=EFFECT),
    )(*lands, sems[0], sems[1], after)


def landing_zone(block):
    return lax.empty((N_DEV,) + block.shape, block.dtype)


def sum_parts(parts, name, row_major_3d=False):
    n_parts, r, c = parts.shape
    tc = _pick(c, (256, 128))

    def body(p_ref, o_ref):
        acc = p_ref[0].astype(F32)
        for i in range(1, n_parts):
            acc = acc + p_ref[i].astype(F32)
        if row_major_3d:
            o_ref[:, 0, :] = acc
        else:
            o_ref[...] = acc

    out_spec = pl.BlockSpec((r, 1, tc), lambda i: (0, 0, i)) if row_major_3d else pl.BlockSpec((r, tc), lambda i: (0, i))
    return pl.pallas_call(
        body, name=name, grid=(c // tc,), in_specs=[pl.BlockSpec((n_parts, r, tc), lambda i: (0, 0, i))],
        out_specs=out_spec, out_shape=SDS((r, 1, c) if row_major_3d else (r, c), F32),
        compiler_params=_cparams(("parallel",)),
    )(parts)


SELF_CHIP_FIRST = (0,) + CHIP_FLIPS
N_CHIPS = len(SELF_CHIP_FIRST)


def _chip_index(place):
    return 2 * place[0] + place[1]


def _split_call(body, name, arrays, n_sems, sems_in=None, after=None):
    n = len(arrays)
    thru = [pltpu.HBM(t.shape, t.dtype) for t in arrays]
    if sems_in is None:
        ins = [pltpu.with_memory_space_constraint(t, pltpu.HBM) for t in arrays]
        res = pl.pallas_call(
            body, name=name, in_specs=[HBM] * n, out_specs=[SEM, SEM] + [HBM] * n + [pl.BlockSpec(memory_space=pltpu.VMEM)],
            out_shape=[pltpu.SemaphoreType.DMA((n_sems,))] * 2 + thru + [SDS((8, LANES), F32)],
            input_output_aliases={i: 2 + i for i in range(n)}, compiler_params=pltpu.CompilerParams(has_side_effects=EFFECT),
        )(*ins)
        return (res[0], res[1]), res[2:2 + n], res[-1]
    return pl.pallas_call(
        body, name=name, in_specs=[HBM] * n + [SEM, SEM, ANY], out_specs=[HBM] * n, out_shape=thru,
        input_output_aliases={i: i for i in range(n)}, compiler_params=pltpu.CompilerParams(has_side_effects=EFFECT),
    )(*arrays, sems_in[0], sems_in[1], after)


def _pair_copies(p_ref, l_ref, ssem, rsem, arriving):
    me = _my_place()
    copies = []
    for j, f in enumerate(SELF_CHIP_FIRST):
        owner = _flip(me, f) if arriving else _flip(_flip(me, f), 1)
        blk = _index(owner)
        copies.append(pltpu.make_async_remote_copy(src_ref=p_ref.at[blk], dst_ref=l_ref.at[blk], send_sem=ssem.at[j],
                                                   recv_sem=rsem.at[j], device_id=_flip(me, 1), device_id_type=MESH))
    return copies


def _chip_copies(p_ref, l_ref, ssem, rsem, arriving):
    me = _my_place()
    copies = []
    for j, f in enumerate(SELF_CHIP_FIRST):
        target = _flip(me, f)
        src = p_ref.at[_chip_index(target)]
        dst = l_ref.at[_chip_index(target if arriving else me)]
        copies.append(pltpu.make_async_remote_copy(src_ref=src, dst_ref=dst, send_sem=ssem.at[j], recv_sem=rsem.at[j],
                                                   device_id=target, device_id_type=MESH))
    return copies


def pair_start(parts, name):
    def body(p_ref, l_ref, ssem, rsem, p_thru, l_thru, token):
        for cp in _pair_copies(p_ref, l_ref, ssem, rsem, False):
            cp.start()
        token[...] = jnp.zeros_like(token)
    return _split_call(body, name, [parts, lax.empty(parts.shape, parts.dtype)], N_CHIPS)


def pair_sum(parts, land, sems, after, name):
    def wait_body(p_ref, l_ref, ssem, rsem, after_ref, p_thru, l_thru):
        for cp in _pair_copies(p_ref, l_ref, ssem, rsem, True):
            cp.wait_send()
            cp.wait_recv()
    parts, land = _split_call(wait_body, name + "_wait", [parts, land], N_CHIPS, sems, after)
    _, r, c = parts.shape
    tc = _pick(c, (256, 128))

    def body(p_ref, l_ref, o_ref):
        core = lax.axis_index("c")
        o_ref[0] = (p_ref[0, core].astype(F32) + l_ref[0, core].astype(F32)).astype(o_ref.dtype)

    spec = pl.BlockSpec((1, 2, r, tc), lambda k, j: (k, 0, 0, j))
    return pl.pallas_call(
        body, name=name, grid=(N_CHIPS, c // tc), in_specs=[spec, spec],
        out_specs=pl.BlockSpec((1, r, tc), lambda k, j: (k, 0, j)), out_shape=SDS((N_CHIPS, r, c), parts.dtype),
        compiler_params=_cparams(("parallel", "parallel")),
    )(parts.reshape(N_CHIPS, 2, r, c), land.reshape(N_CHIPS, 2, r, c))


def chip_start(pre, name):
    def body(p_ref, l_ref, ssem, rsem, p_thru, l_thru, token):
        for cp in _chip_copies(p_ref, l_ref, ssem, rsem, False):
            cp.start()
        token[...] = jnp.zeros_like(token)
    return _split_call(body, name, [pre, lax.empty(pre.shape, pre.dtype)], N_CHIPS)


def chip_wait(pre, land, sems, after, name):
    def body(p_ref, l_ref, ssem, rsem, after_ref, p_thru, l_thru):
        for cp in _chip_copies(p_ref, l_ref, ssem, rsem, True):
            cp.wait_send()
            cp.wait_recv()
    return _split_call(body, name, [pre, land], N_CHIPS, sems, after)[1]


ADAMW_BLOCK_BYTES = 2 * 1024 * 1024


def adamw(w, g, m, v, name):
    shape = w.shape
    lay, rows, cols = ((1, 1) + shape)[-3:]
    tr = _pick(rows, (256, 128))
    tc = cols if tr * cols * 4 <= ADAMW_BLOCK_BYTES else _pick(cols, (256, 128))
    c1 = 1.0 / (1.0 - ADAM_B1 ** ADAM_STEP)
    c2 = 1.0 / (1.0 - ADAM_B2 ** ADAM_STEP)

    def body(w_ref, g_ref, m_ref, v_ref, d_ref, nm_ref, nv_ref):
        gg = g_ref[...]
        nm = ADAM_B1 * m_ref[...] + (1.0 - ADAM_B1) * gg
        nv = ADAM_B2 * v_ref[...] + (1.0 - ADAM_B2) * (gg * gg)
        d_ref[...] = -ADAM_LR * ((nm * c1) / (jnp.sqrt(nv * c2) + ADAM_EPS) + ADAM_WD * w_ref[...])
        nm_ref[...] = nm
        nv_ref[...] = nv

    spec = pl.BlockSpec((1, tr, tc), lambda l, i, j: (l, i, j))
    outs = pl.pallas_call(
        body, name=name, grid=(lay, rows // tr, cols // tc), in_specs=[spec] * 4, out_specs=[spec] * 3,
        out_shape=[SDS((lay, rows, cols), F32)] * 3, compiler_params=_cparams(("parallel",) * 3),
    )(*[t.reshape(lay, rows, cols) for t in (w, g, m, v)])
    return [o.reshape(shape) for o in outs]


def adamw_layer_inner(w, gs, m, v, name):
    rows, lay, cols = w.shape
    tr = _pick(rows, (256, 220, 128))
    c1 = 1.0 / (1.0 - ADAM_B1 ** ADAM_STEP)
    c2 = 1.0 / (1.0 - ADAM_B2 ** ADAM_STEP)

    def body(*refs):
        w_ref, m_ref, v_ref = refs[:3]
        g_refs = refs[3:3 + lay]
        go_ref, d_ref, nm_ref, nv_ref = refs[3 + lay:]
        for l, g_ref in enumerate(g_refs):
            gg = g_ref[:, 0, :]
            nm = ADAM_B1 * m_ref[:, l, :] + (1.0 - ADAM_B1) * gg
            nv = ADAM_B2 * v_ref[:, l, :] + (1.0 - ADAM_B2) * (gg * gg)
            d_ref[:, l, :] = -ADAM_LR * ((nm * c1) / (jnp.sqrt(nv * c2) + ADAM_EPS) + ADAM_WD * w_ref[:, l, :])
            go_ref[:, l, :] = gg
            nm_ref[:, l, :] = nm
            nv_ref[:, l, :] = nv

    inner = pl.BlockSpec((tr, lay, cols), lambda i: (i, 0, 0))
    plain = pl.BlockSpec((tr, 1, cols), lambda i: (i, 0, 0))
    return pl.pallas_call(
        body, name=name, grid=(rows // tr,), in_specs=[inner] * 3 + [plain] * lay, out_specs=[inner] * 4,
        out_shape=[SDS((rows, lay, cols), F32)] * 4, compiler_params=_cparams(("parallel",)),
    )(w, m, v, *gs)


BIG = ("w_in", "conv_w", "w_ssd_branch", "w_attn_branch", "w_out", "w_gate_up", "w_down")
TRANSPOSED = ("w_in", "w_gate_up")
SMALL = ("norm_mix", "conv_b", "dt_bias", "a_log", "d_skip", "ssd_norm", "norm_ffn")
SMALL_SIZE = {"norm_mix": 1024, "conv_b": 3072, "dt_bias": 32, "a_log": 32, "d_skip": 32, "ssd_norm": 2048, "norm_ffn": 1024}
FLAT_W = 512
SMALL_TOTAL = DEPTH * sum(SMALL_SIZE.values()) + D_MODEL + LANES
SMALL_ROWS = 32
assert SMALL_ROWS * FLAT_W >= SMALL_TOTAL


GROUPS = (("w_in", "conv_w"), ("w_ssd_branch", "w_attn_branch", "w_out"), ("w_gate_up", "w_down"))


def to_wire(k, shard):
    if k in TRANSPOSED:
        return shard.T.astype(BF16)
    return shard if k == "conv_w" else shard.astype(BF16)


def full_weights(k, g):
    if k == "conv_w":
        return {k: g.transpose(1, 0, 2).reshape(SSD_CONV, SSD_CONV_CH)}
    full = g.reshape(-1, g.shape[-1])
    if k != "w_in":
        return {k: full}
    w, off = {}, 0
    for nm, r in IN_ROWS:
        w[nm] = full[off:off + r]
        off += r
    w["w_q"] = full[sum(r for _, r in IN_ROWS[:3]):sum(r for _, r in IN_ROWS[:6])]
    w["w_dt"] = jnp.pad(w["w_dt"], ((0, HPAD - SSD_HEADS), (0, 0)))
    return w


def grads_to_wire(k, g):
    if k == "conv_w":
        return g.reshape(SSD_CONV, N_DEV, SSD_CONV_CH // N_DEV).transpose(1, 0, 2)
    return g.reshape(N_DEV, g.shape[0] // N_DEV, g.shape[1])


def _pad_heads(t):
    return jnp.pad(t.reshape(1, SSD_HEADS), ((0, 0), (0, HPAD - SSD_HEADS)))


def local_step(x, target, getw, prefetch, emit, smalls, norm_final):
    tabs = rope_tables()
    sms = []
    for li in range(DEPTH):
        s = smalls[li]
        sms.append({
            "norm_mix": s["norm_mix"].reshape(1, -1), "conv_b": s["conv_b"].reshape(1, -1),
            "dt_bias": _pad_heads(s["dt_bias"]), "a_log": _pad_heads(s["a_log"]),
            "d_skip_x": jnp.repeat(s["d_skip"], SSD_HEAD_DIM).reshape(1, -1),
            "ssd_norm": s["ssd_norm"].reshape(1, -1), "norm_ffn": s["norm_ffn"].reshape(1, -1)})
    h = x
    saved = []
    for li in range(DEPTH):
        h, sv = layer_fwd(h, functools.partial(getw, li), functools.partial(prefetch, li), sms[li], tabs, li)
        saved.append(sv)
    dh, g_final, loss = loss_head(h, target, norm_final.reshape(1, -1), "loss_head")
    gsms = [None] * DEPTH
    for li in reversed(range(DEPTH)):
        dh, gsm = layer_bwd(dh, saved[li], sms[li], tabs, li, functools.partial(emit, li))
        gsms[li] = {
            "norm_mix": gsm["norm_mix"].reshape(-1), "conv_b": gsm["conv_b"].reshape(-1),
            "dt_bias": gsm["dt_bias"][0, :SSD_HEADS], "a_log": gsm["a_log"][0, :SSD_HEADS],
            "d_skip": gsm["d_skip_x"].reshape(SSD_HEADS, SSD_HEAD_DIM).sum(axis=1),
            "ssd_norm": gsm["ssd_norm"].reshape(-1), "norm_ffn": gsm["norm_ffn"].reshape(-1)}
    return loss, dh, gsms, g_final.reshape(-1)


def kernel(x, norm_mix, w_in, conv_w, conv_b, dt_bias, a_log, d_skip, ssd_norm, w_ssd_branch, w_attn_branch, w_out, norm_ffn, w_gate_up, w_down, norm_final, loss_target, m_norm_mix, m_w_in, m_conv_w, m_conv_b, m_dt_bias, m_a_log, m_d_skip, m_ssd_norm, m_w_ssd_branch, m_w_attn_branch, m_w_out, m_norm_ffn, m_w_gate_up, m_w_down, m_norm_final, v_norm_mix, v_w_in, v_conv_w, v_conv_b, v_dt_bias, v_a_log, v_d_skip, v_ssd_norm, v_w_ssd_branch, v_w_attn_branch, v_w_out, v_norm_ffn, v_w_gate_up, v_w_down, v_norm_final):
    wv = dict(norm_mix=norm_mix, w_in=w_in, conv_w=conv_w, conv_b=conv_b, dt_bias=dt_bias, a_log=a_log, d_skip=d_skip,
              ssd_norm=ssd_norm, w_ssd_branch=w_ssd_branch, w_attn_branch=w_attn_branch, w_out=w_out, norm_ffn=norm_ffn,
              w_gate_up=w_gate_up, w_down=w_down, norm_final=norm_final)
    mv = dict(norm_mix=m_norm_mix, w_in=m_w_in, conv_w=m_conv_w, conv_b=m_conv_b, dt_bias=m_dt_bias, a_log=m_a_log,
              d_skip=m_d_skip, ssd_norm=m_ssd_norm, w_ssd_branch=m_w_ssd_branch, w_attn_branch=m_w_attn_branch,
              w_out=m_w_out, norm_ffn=m_norm_ffn, w_gate_up=m_w_gate_up, w_down=m_w_down, norm_final=m_norm_final)
    vv = dict(norm_mix=v_norm_mix, w_in=v_w_in, conv_w=v_conv_w, conv_b=v_conv_b, dt_bias=v_dt_bias, a_log=v_a_log,
              d_skip=v_d_skip, ssd_norm=v_ssd_norm, w_ssd_branch=v_w_ssd_branch, w_attn_branch=v_w_attn_branch,
              w_out=v_w_out, norm_ffn=v_norm_ffn, w_gate_up=v_w_gate_up, w_down=v_w_down, norm_final=v_norm_final)
    order = ("norm_mix", "w_in", "conv_w", "conv_b", "dt_bias", "a_log", "d_skip", "ssd_norm", "w_ssd_branch",
             "w_attn_branch", "w_out", "norm_ffn", "w_gate_up", "w_down", "norm_final")

    smalls = [{k: wv[k][li] for k in SMALL} for li in range(DEPTH)]
    n_groups = len(GROUPS)

    first_lands = all_gather([to_wire(k, wv[k][0]) for k in GROUPS[0]], "gather_first")
    later = [(li, gi) for li in range(DEPTH) for gi in range(n_groups)][1:]
    behind_first = first_lands[1][0, 0, 0] * 0.0
    srcs = [to_wire(k, wv[k][li] + behind_first if k == "conv_w" else wv[k][li]) for li, gi in later for k in GROUPS[gi]]
    sizes = [len(GROUPS[gi]) for _, gi in later]
    w_sems, w_srcs, w_lands, token = exchange_start(srcs, [landing_zone(s) for s in srcs], sizes, False,
                                                    "gather_start", peers=SELF_AND_CHIPS)
    smalls[0]["norm_mix"] = smalls[0]["norm_mix"] + token[0, 0]
    second_leg = {}

    def forward(slot, after):
        if slot < len(later) and slot not in second_leg:
            sl = slice(sum(sizes[:slot]), sum(sizes[:slot + 1]))
            second_leg[slot] = gather_forward(w_srcs[sl], w_lands[sl], w_sems[slot], after, f"gather_forward_{slot}")

    def prefetch(li, gi, after):
        if (li, gi) == later[0]:
            forward(0, after)

    def getw(li, gi, after):
        if (li, gi) == (0, 0):
            lands = first_lands
        else:
            slot = later.index((li, gi))
            forward(slot, after)
            sems2, lands2 = second_leg[slot]
            lands = gather_finish(lands2, sems2, after, f"gather_finish_{li}_{gi}")
            forward(slot + 1, lands[0])
        w = {}
        for k, land in zip(GROUPS[gi], lands):
            w.update(full_weights(k, land))
        return w

    pending = []

    last = {}

    def emit(li, gi, gw):
        if gi == -1:
            if li != 0:
                return 0.0
            pre = pair_sum(last["parts"], last["land"], last["sems"], gw, "grads_pair_sum")
            last["sems2"], (last["pre"], last["land2"]), tok = chip_start(pre, "grads_chip_start")
            return tok[0, 0]
        names, tok0 = GROUPS[gi], 0.0
        if (li, gi) == (0, 0):
            last["sems"], (last["parts"], last["land"]), t0 = pair_start(grads_to_wire("w_in", gw["w_in"]), "grads_pair_start")
            names, tok0 = ("conv_w",), t0[0, 0]
        parts = [grads_to_wire(k, gw[k]) for k in names]
        lands = [landing_zone(p[0]) for p in parts]
        sems, p_thru, l_thru, tok = exchange_start(parts, lands, [len(parts)], True, f"grads_start_{li}_{gi}", peers=EVERYONE)
        pending.append((li, gi, names, sems[0], p_thru, l_thru))
        return tok[0, 0] + tok0

    loss_p, dx, gsms, g_final = local_step(x[0], loss_target[0], getw, prefetch, emit, smalls, norm_final)

    grads, deltas, new_m, new_v = {}, {}, {}, {}

    def update(k):
        if k == "w_in":
            inner = lambda t: t.transpose(2, 0, 1)
            outs = adamw_layer_inner(inner(wv[k]), shard_g[k], inner(mv[k]), inner(vv[k]), "adamw_" + k)
            grads[k], deltas[k], new_m[k], new_v[k] = (t.transpose(1, 2, 0) for t in outs)
            return outs[3]
        if k in BIG:
            grads[k] = jnp.stack([g.T if k in TRANSPOSED else g for g in shard_g[k]])
        deltas[k], new_m[k], new_v[k] = adamw(wv[k], grads[k], mv[k], vv[k], "adamw_" + k)
        return new_v[k]

    shard_g = {k: [None] * DEPTH for k in BIG}

    def collect(entry, after):
        li, gi, names, sems, p_thru, l_thru = entry
        recv = exchange_wait(p_thru, l_thru, sems, after, True, f"grads_wait_{li}_{gi}", peers=EVERYONE)
        for k, r in zip(names, recv):
            if k == "conv_w":
                r = r.reshape(N_DEV, 1, -1)
            after = sum_parts(r, f"sum_{k}_{li}", row_major_3d=(k == "w_in"))
            shard_g[k][li] = after if k in TRANSPOSED else after.reshape(wv[k].shape[1:])
        if (li, gi) == (0, 0):
            land2 = chip_wait(last["pre"], last["land2"], last["sems2"], after, "grads_chip_wait")
            after = sum_parts(land2, "sum_w_in_0", row_major_3d=True)
            shard_g["w_in"][0] = after
        return after

    after = dx
    for entry in pending[:-1]:
        after = collect(entry, after)
    done = [after[:1, :1].reshape(1)]
    for gi in (2, 1):
        for k in GROUPS[gi]:
            done.append(update(k).reshape(-1)[:1])

    flat = [gsms[li][k] for li in range(DEPTH) for k in SMALL] + [g_final, loss_p.reshape(-1)]
    flat.append(jnp.zeros((SMALL_ROWS * FLAT_W - SMALL_TOTAL,), F32))
    small_all = all_gather([jnp.concatenate(flat).reshape(SMALL_ROWS, FLAT_W)], "gather_small")[0]
    small_sum = sum_parts(small_all, "sum_small").reshape(-1)
    off = 0
    per_layer = {k: [] for k in SMALL}
    for li in range(DEPTH):
        for k in SMALL:
            per_layer[k].append(small_sum[off:off + SMALL_SIZE[k]])
            off += SMALL_SIZE[k]
    for k in SMALL:
        grads[k] = jnp.stack(per_layer[k])
    grads["norm_final"] = small_sum[off:off + D_MODEL]
    loss = small_sum[off + D_MODEL]
    for k in (*SMALL, "norm_final"):
        done.append(update(k).reshape(-1)[:1])

    collect(pending[-1], jnp.concatenate(done))
    for k in GROUPS[0]:
        update(k)

    return (loss, dx.reshape(x.shape), *[grads[k] for k in order], *[deltas[k] for k in order],
            *[new_m[k] for k in order], *[new_v[k] for k in order])
```

```python
import functools

import jax
import jax.numpy as jnp
from jax import lax
from jax.experimental import pallas as pl
from jax.experimental.pallas import tpu as pltpu

F32, BF16 = jnp.float32, jnp.bfloat16
SDS = jax.ShapeDtypeStruct
MESH = pl.DeviceIdType.MESH

D_MODEL = 1024
SEQ = 2048
DEPTH = 2
RMS_EPS = 1e-5
SSD_INNER = 2048
SSD_HEAD_DIM = 64
SSD_HEADS = 32
SSD_STATE = 128
SSD_GROUPS = 4
SSD_CONV = 4
SSD_CHUNK = 128
SSD_CONV_CH = 3072
ATTN_HEAD_DIM = 128
ATTN_KV_HEADS = 8
ATTN_DILATIONS = (1, 4, 16)
ATTN_N_PAT = 3
ATTN_BLOCK = 128
ATTN_OUT = 1024
ROPE_THETA = 500000.0
ROPE_DIM = 32
FFN_HIDDEN = 2816
ADAM_LR, ADAM_B1, ADAM_B2, ADAM_EPS, ADAM_WD, ADAM_STEP = 0.001, 0.9, 0.999, 1e-08, 0.01, 10

N_DEV = 8
LANES = 128
VMEM_LIMIT = 56 * 1024 * 1024
HPAD = 128
HIGHEST = lax.Precision.HIGHEST

IN_ROWS = (("w_z", 2048), ("w_xbc", 3072), ("w_dt", 32), ("w_q0", 1024), ("w_q1", 1024), ("w_q2", 1024),
           ("w_k", 1024), ("w_v", 1024), ("w_gs", 1024), ("w_ga", 1024))


def _cparams(sem):
    return pltpu.CompilerParams(dimension_semantics=sem, vmem_limit_bytes=VMEM_LIMIT)


def _sigmoid(x):
    return 0.5 * jnp.tanh(0.5 * x) + 0.5


def _silu(x):
    return x * _sigmoid(x)


def _softplus(x):
    return jnp.maximum(x, 0.0) + jnp.log(1.0 + jnp.exp(-jnp.abs(x)))


def _dot(a, b, dims=(((1,), (0,)), ((), ())), precision=None):
    return lax.dot_general(a, b, dims, precision=precision, preferred_element_type=F32)


NT = (((1,), (1,)), ((), ()))


def _bdot(a, b, dims=(((1,), (0,)), ((), ()))):
    return _dot(a.astype(BF16), b.astype(BF16), dims)


def _pick(dim, cands):
    for c in cands:
        if dim % c == 0:
            return c
    return dim


WHOLE_K_BUDGET = 40 * 1024 * 1024
RESIDENT_B_BYTES = 12 * 1024 * 1024
OUT_TILE_BYTES = 6 * 1024 * 1024


def matmul(a, b, *, name, ta=False, tb=False, out_dtype=F32, add=None):
    m, k = (a.shape[1], a.shape[0]) if ta else a.shape
    n = b.shape[0] if tb else b.shape[1]
    out_bytes = jnp.dtype(out_dtype).itemsize + (4 if add is not None else 0)
    if k * n * b.dtype.itemsize <= RESIDENT_B_BYTES:
        tn = n
        tm = next(t for t in (512, 256, 128) if m % t == 0 and t * n * out_bytes <= OUT_TILE_BYTES)
    else:
        tn = _pick(n, (1024, 1408, 512, 256, 128))
        tm = _pick(m, (512, 1408, 256, 128)) if tn == n else _pick(m, (1024, 1408, 512, 256, 128))
    tk = _pick(k, (2048, 1024, 1408, 512, 256, 128))
    whole_k_bytes = 2 * (tm * k * a.dtype.itemsize + k * tn * b.dtype.itemsize)
    if tn == n and whole_k_bytes <= WHOLE_K_BUDGET:
        tk = k
    nk = k // tk
    a_spec = pl.BlockSpec((tk, tm), lambda i, j, kk: (kk, i)) if ta else pl.BlockSpec((tm, tk), lambda i, j, kk: (i, kk))
    b_spec = pl.BlockSpec((tn, tk), lambda i, j, kk: (j, kk)) if tb else pl.BlockSpec((tk, tn), lambda i, j, kk: (kk, j))
    dims = (((0 if ta else 1,), (1 if tb else 0,)), ((), ()))
    has_add = add is not None

    def body(*refs):
        a_ref, b_ref = refs[:2]
        add_ref = refs[2] if has_add else None
        o_ref = refs[3] if has_add else refs[2]
        acc = refs[-1] if nk > 1 else None
        kk = pl.program_id(2)

        def product():
            return _dot(a_ref[...].astype(BF16), b_ref[...].astype(BF16), dims)

        def finish(r):
            if has_add:
                r = r + add_ref[...].astype(F32)
            o_ref[...] = r.astype(o_ref.dtype)

        if nk == 1:
            finish(product())
            return

        @pl.when(kk == 0)
        def _():
            acc[...] = product()

        @pl.when((kk > 0) & (kk < nk - 1))
        def _():
            acc[...] += product()

        @pl.when(kk == nk - 1)
        def _():
            finish(acc[...] + product())

    in_specs = [a_spec, b_spec]
    args = [a, b]
    if has_add:
        in_specs.append(pl.BlockSpec((tm, tn), lambda i, j, kk: (i, j)))
        args.append(add)
    return pl.pallas_call(
        body, name=name, grid=(m // tm, n // tn, nk),
        in_specs=in_specs, out_specs=pl.BlockSpec((tm, tn), lambda i, j, kk: (i, j)),
        out_shape=SDS((m, n), out_dtype), scratch_shapes=[pltpu.VMEM((tm, tn), F32)] if nk > 1 else [],
        compiler_params=_cparams(("parallel", "parallel", "arbitrary")),
    )(*args)


def matmul_rows(a, b, post, extras, outs, *, name, tb=False, tm=256):
    a_list, b_list = (list(a), list(b)) if isinstance(a, (list, tuple)) else ([a], [b])
    m = a_list[0].shape[0]
    dims = NT if tb else (((1,), (0,)), ((), ()))
    npr, ne = len(a_list), len(extras)

    def body(*refs):
        a_refs, b_refs = refs[:npr], refs[npr:2 * npr]
        e_refs, o_refs = refs[2 * npr:2 * npr + ne], refs[2 * npr + ne:]
        prods = [_dot(ar[...].astype(BF16), br[...].astype(BF16), dims) for ar, br in zip(a_refs, b_refs)]
        res = post(*prods, *[r[...] for r in e_refs])
        for r, val in zip(o_refs, res):
            r[...] = val.astype(r.dtype)

    row = lambda width: pl.BlockSpec((tm, width), lambda i: (i, 0))
    whole = lambda t: pl.BlockSpec(t.shape, lambda i: (0, 0))
    return pl.pallas_call(
        body, name=name, grid=(m // tm,),
        in_specs=[row(t.shape[1]) for t in a_list] + [whole(t) for t in b_list] + [row(e.shape[1]) for e in extras],
        out_specs=[row(c) for c, _ in outs], out_shape=[SDS((m, c), dt) for c, dt in outs],
        compiler_params=_cparams(("parallel",)),
    )(*a_list, *b_list, *extras)


def rowcall(name, fn, rows, params, row_outs, red_outs=(), tr=256):
    s = rows[0].shape[0]
    n_in = len(rows) + len(params)
    n_row = len(row_outs)

    def body(*refs):
        outs = fn(*[r[...].astype(F32) for r in refs[:n_in]])
        if not isinstance(outs, (tuple, list)):
            outs = (outs,)
        orefs = refs[n_in:]
        for r, o in zip(orefs[:n_row], outs[:n_row]):
            r[...] = o.astype(r.dtype)
        if red_outs:
            @pl.when(pl.program_id(0) == 0)
            def _():
                for r in orefs[n_row:]:
                    r[...] = jnp.zeros_like(r)
            for r, o in zip(orefs[n_row:], outs[n_row:]):
                r[...] += o.astype(F32)

    widths = [a[1] if isinstance(a, tuple) else a.shape[1] for a in rows]
    rows = [a[0] if isinstance(a, tuple) else a for a in rows]
    in_specs = [pl.BlockSpec((tr, wd), lambda i: (i, 0)) for wd in widths]
    in_specs += [pl.BlockSpec(p.shape, lambda i: (0, 0)) for p in params]
    out_specs = [pl.BlockSpec((tr, c), lambda i: (i, 0)) for c, _ in row_outs]
    out_specs += [pl.BlockSpec(shp, lambda i: (0, 0)) for shp in red_outs]
    out_shape = [SDS((s, c), dt) for c, dt in row_outs] + [SDS(shp, F32) for shp in red_outs]
    res = pl.pallas_call(
        body, name=name, grid=(s // tr,), in_specs=in_specs, out_specs=out_specs, out_shape=out_shape,
        compiler_params=_cparams(("arbitrary",) if red_outs else ("parallel",)),
    )(*rows, *params)
    return res


def _rms(x, w):
    return x * lax.rsqrt(jnp.mean(x * x, axis=-1, keepdims=True) + RMS_EPS) * w


def rms_fwd(h, w, name):
    return rowcall(name, _rms, [h], [w], [(D_MODEL, BF16)])[0]


def rms_bwd(h, du, dres, w, name):
    def fn(hb, dub, dresb, wb):
        _, vjp = jax.vjp(_rms, hb, wb)
        dh, dw = vjp(dub)
        return dh + dresb, dw
    return rowcall(name, fn, [h, du, dres], [w], [(D_MODEL, F32)], [(1, D_MODEL)])


def loss_head(h, target, w, name):
    def fn(hb, tb, wb):
        def f(hh, ww):
            err = _rms(hh, ww) - tb
            return 0.5 * jnp.sum(jnp.mean(err * err, axis=-1, keepdims=True), axis=0, keepdims=True)
        val, vjp = jax.vjp(f, hb, wb)
        dh, dw = vjp(jnp.ones((1, 1), F32))
        return dh, dw, jnp.broadcast_to(val, (1, LANES))
    return rowcall(name, fn, [h, target], [w], [(D_MODEL, F32)], [(1, D_MODEL), (1, LANES)])


def _gate(a, b, gs, ga):
    return _sigmoid(gs) * a + _sigmoid(ga) * b


def gate_fwd(yn, w_ssd, y_attn, w_attn, gs, ga, name):
    def post(pa, pb, gsb, gab):
        a, b = pa.astype(BF16), pb.astype(BF16)
        return a, b, _gate(a.astype(F32), b.astype(F32), gsb.astype(F32), gab.astype(F32))
    return matmul_rows([yn, y_attn], [w_ssd, w_attn], post, [gs, ga], [(D_MODEL, BF16)] * 3, name=name)


def gate_bwd(dh1, w_out, a, b, gs, ga, name):
    def post(dm, ab, bb, gsb, gab):
        _, vjp = jax.vjp(_gate, ab.astype(F32), bb.astype(F32), gsb.astype(F32), gab.astype(F32))
        return vjp(dm)
    return matmul_rows(dh1, w_out, post, [a, b, gs, ga], [(D_MODEL, BF16)] * 4, name=name, tb=True)


def _swiglu(gu):
    return _silu(gu[:, :FFN_HIDDEN]) * gu[:, FFN_HIDDEN:]


def gate_up_fwd(u2, w_gate_up_t, name):
    def post(acc):
        gu = acc.astype(BF16)
        return gu, _swiglu(gu.astype(F32))
    return matmul_rows(u2, w_gate_up_t, post, [], [(2 * FFN_HIDDEN, BF16), (FFN_HIDDEN, BF16)], name=name, tb=True)


def gate_up_bwd(dh, w_down, gu, name):
    def post(acc, gub):
        _, vjp = jax.vjp(_swiglu, gub.astype(F32))
        return vjp(acc.astype(BF16).astype(F32))
    return matmul_rows(dh, w_down, post, [gu], [(2 * FFN_HIDDEN, BF16)], name=name, tb=True)[0]


def _ssd_post(y, xs, z, dskip, normw):
    y = (y + dskip * xs) * _silu(z)
    gw = SSD_INNER // SSD_GROUPS
    parts = []
    for g in range(SSD_GROUPS):
        yg = y[:, g * gw:(g + 1) * gw]
        parts.append(yg * lax.rsqrt(jnp.mean(yg * yg, axis=-1, keepdims=True) + RMS_EPS))
    return jnp.concatenate(parts, axis=-1) * normw


def ssd_post_fwd(y, xc, z, dskip, normw, name):
    return rowcall(name, _ssd_post, [y, (xc, SSD_INNER), z], [dskip, normw], [(SSD_INNER, BF16)])[0]


def ssd_post_bwd(y, xc, z, dskip, normw, dyn, name):
    def fn(yb, xsb, zb, dynb, db, nb):
        _, vjp = jax.vjp(_ssd_post, yb, xsb, zb, db, nb)
        return vjp(dynb)
    return rowcall(name, fn, [y, (xc, SSD_INNER), z, dyn], [dskip, normw],
                   [(SSD_INNER, BF16)] * 3, [(1, SSD_INNER), (1, SSD_INNER)])


def _rope(t, cosf, sina, sinb):
    return t * cosf + pltpu.roll(t, LANES - ROPE_DIM // 2, 1) * sina + pltpu.roll(t, ROPE_DIM // 2, 1) * sinb


def rope_tables():
    half = ROPE_DIM // 2
    inv = ROPE_THETA ** (-jnp.arange(0, ROPE_DIM, 2, dtype=F32) / ROPE_DIM)
    ang = jnp.arange(SEQ, dtype=F32)[:, None] * inv[None, :]
    cos, sin = jnp.cos(ang), jnp.sin(ang)
    zeros = jnp.zeros((SEQ, LANES - ROPE_DIM), F32)
    z16 = jnp.zeros((SEQ, half), F32)
    cosf = jnp.concatenate([cos, cos, jnp.ones((SEQ, LANES - ROPE_DIM), F32)], axis=1)
    sina = jnp.concatenate([-sin, z16, zeros], axis=1)
    sinb = jnp.concatenate([z16, sin, zeros], axis=1)
    return cosf, sina, sinb


CONV_TC = 256


def _conv_pre(x, w, b, row):
    acc = x * w[SSD_CONV - 1:SSD_CONV, :] + b
    shifted = [x]
    for j in range(1, SSD_CONV):
        xs = jnp.where(row >= j, pltpu.roll(x, j, 0), 0.0)
        shifted.append(xs)
        acc = acc + xs * w[SSD_CONV - 1 - j:SSD_CONV - j, :]
    return acc, shifted


def conv_fwd(xbc, w, b, name):
    def body(x_ref, w_ref, b_ref, o_ref):
        row = lax.broadcasted_iota(jnp.int32, (SEQ, CONV_TC), 0)
        pre, _ = _conv_pre(x_ref[...].astype(F32), w_ref[...], b_ref[...], row)
        o_ref[...] = _silu(pre).astype(o_ref.dtype)
    return pl.pallas_call(
        body, name=name, grid=(SSD_CONV_CH // CONV_TC,),
        in_specs=[pl.BlockSpec((SEQ, CONV_TC), lambda i: (0, i)), pl.BlockSpec((SSD_CONV, CONV_TC), lambda i: (0, i)),
                  pl.BlockSpec((1, CONV_TC), lambda i: (0, i))],
        out_specs=pl.BlockSpec((SEQ, CONV_TC), lambda i: (0, i)),
        out_shape=SDS((SEQ, SSD_CONV_CH), BF16), compiler_params=_cparams(("parallel",)),
    )(xbc, w, b)


def conv_bwd(xbc, w, b, dxc, name):
    def body(x_ref, w_ref, b_ref, dy_ref, dx_ref, dw_ref, db_ref):
        row = lax.broadcasted_iota(jnp.int32, (SEQ, CONV_TC), 0)
        wv = w_ref[...]
        pre, shifted = _conv_pre(x_ref[...].astype(F32), wv, b_ref[...], row)
        sg = _sigmoid(pre)
        ds = dy_ref[...].astype(F32) * (sg * (1.0 + pre * (1.0 - sg)))
        dx = ds * wv[SSD_CONV - 1:SSD_CONV, :]
        for j in range(1, SSD_CONV):
            dsj = jnp.where(row < SEQ - j, pltpu.roll(ds, SEQ - j, 0), 0.0)
            dx = dx + dsj * wv[SSD_CONV - 1 - j:SSD_CONV - j, :]
        dx_ref[...] = dx.astype(dx_ref.dtype)
        for j in range(SSD_CONV):
            dw_ref[SSD_CONV - 1 - j:SSD_CONV - j, :] = jnp.sum(ds * shifted[j], axis=0, keepdims=True)
        db_ref[...] = jnp.sum(ds, axis=0, keepdims=True)
    return pl.pallas_call(
        body, name=name, grid=(SSD_CONV_CH // CONV_TC,),
        in_specs=[pl.BlockSpec((SEQ, CONV_TC), lambda i: (0, i)), pl.BlockSpec((SSD_CONV, CONV_TC), lambda i: (0, i)),
                  pl.BlockSpec((1, CONV_TC), lambda i: (0, i)), pl.BlockSpec((SEQ, CONV_TC), lambda i: (0, i))],
        out_specs=[pl.BlockSpec((SEQ, CONV_TC), lambda i: (0, i)), pl.BlockSpec((SSD_CONV, CONV_TC), lambda i: (0, i)),
                   pl.BlockSpec((1, CONV_TC), lambda i: (0, i))],
        out_shape=[SDS((SEQ, SSD_CONV_CH), BF16), SDS((SSD_CONV, SSD_CONV_CH), F32), SDS((1, SSD_CONV_CH), F32)],
        compiler_params=_cparams(("parallel",)),
    )(xbc, w, b, dxc)


N_CHUNKS = SEQ // SSD_CHUNK
N_PAIRS = SSD_HEADS // 2
PAIRS_PER_GROUP = N_PAIRS // SSD_GROUPS
B_OFF = SSD_INNER
C_OFF = SSD_INNER + SSD_GROUPS * SSD_STATE


def _ssd_prefix(dtr, dtr_t, dtb, dtb_t, alog, alog_t):
    ln = SSD_CHUNK
    dt = _softplus(dtr + dtb)
    dt_t = _softplus(dtr_t + dtb_t)
    dta = dt * (-jnp.exp(alog))
    dta_t = dt_t * (-jnp.exp(alog_t))
    r = lax.broadcasted_iota(jnp.int32, (ln, ln), 0)
    c = lax.broadcasted_iota(jnp.int32, (ln, ln), 1)
    a_cum = _dot((r >= c).astype(F32), dta, precision=HIGHEST)
    a_cum_t = _dot(dta_t, (r <= c).astype(F32), precision=HIGHEST)
    a_last = jnp.sum(dta_t, axis=1, keepdims=True)
    return dt, a_cum, a_cum_t, a_last


def _bein(spec, a, b):
    return jnp.einsum(spec, a.astype(BF16), b.astype(BF16), preferred_element_type=F32)


SSD_GROUPS_PER_BATCH = 4


def _ssd_group(xs3, bgs, cgs, h3, dt, a_cum, a_cum_t, a_last, *, groups):
    ln = SSD_CHUNK
    lane = lax.broadcasted_iota(jnp.int32, (ln, LANES), 1)
    sub = lax.broadcasted_iota(jnp.int32, (LANES, SSD_STATE), 0)
    row = lax.broadcasted_iota(jnp.int32, (ln, ln), 0)
    col = lax.broadcasted_iota(jnp.int32, (ln, ln), 1)
    lo = lane < SSD_HEAD_DIM
    causal = row >= col
    m_lo, m_hi, dts, acs, lasts, cds, cg3, bg3 = [], [], [], [], [], [], [], []
    for g, bg, cg in zip(groups, bgs, cgs):
        cb = _bdot(cg, bg, NT)
        for j in range(PAIRS_PER_GROUP):
            e0 = 2 * (g * PAIRS_PER_GROUP + j)
            e1 = e0 + 1
            c0, c1 = a_cum[:, e0:e0 + 1], a_cum[:, e1:e1 + 1]
            r0, r1 = a_cum_t[e0:e0 + 1, :], a_cum_t[e1:e1 + 1, :]
            l0, l1 = a_last[e0:e0 + 1, :], a_last[e1:e1 + 1, :]
            m_lo.append(cb * jnp.exp(jnp.where(causal, c0 - r0, -jnp.inf)))
            m_hi.append(cb * jnp.exp(jnp.where(causal, c1 - r1, -jnp.inf)))
            dts.append(jnp.where(lo, dt[:, e0:e0 + 1], dt[:, e1:e1 + 1]))
            acs.append(jnp.where(lo, c0, c1))
            lasts.append(jnp.where(lo, l0, l1))
            cds.append(jnp.exp(jnp.where(sub < SSD_HEAD_DIM, l0, l1)))
            cg3.append(cg)
            bg3.append(bg)
    xd = xs3 * jnp.stack(dts)
    acum = jnp.stack(acs)
    y = (_bein("pls,psq->plq", jnp.stack(m_lo), jnp.where(lo[None], xd, 0.0))
         + _bein("pls,psq->plq", jnp.stack(m_hi), jnp.where(lo[None], 0.0, xd)))
    y = y + _bein("pln,pqn->plq", jnp.stack(cg3), h3) * jnp.exp(acum)
    st = _bein("plq,pln->pqn", xd * jnp.exp(jnp.stack(lasts) - acum), jnp.stack(bg3))
    h_out = h3 * jnp.stack(cds) + st
    return y, h_out


def _group_slabs(groups):
    pairs = [g * PAIRS_PER_GROUP + j for g in groups for j in range(PAIRS_PER_GROUP)]
    return [slice(p * LANES, (p + 1) * LANES) for p in pairs]


def _group_batches():
    return [tuple(range(g, g + SSD_GROUPS_PER_BATCH)) for g in range(0, SSD_GROUPS, SSD_GROUPS_PER_BATCH)]


def _bc_of(xc_ref, g):
    return (xc_ref[:, B_OFF + g * SSD_STATE:B_OFF + (g + 1) * SSD_STATE].astype(F32),
            xc_ref[:, C_OFF + g * SSD_STATE:C_OFF + (g + 1) * SSD_STATE].astype(F32))


def _ssd_in_specs(chunk_of):
    return [
        pl.BlockSpec((SSD_CHUNK, SSD_CONV_CH), lambda i: (chunk_of(i), 0)),
        pl.BlockSpec((SSD_CHUNK, HPAD), lambda i: (chunk_of(i), 0)),
        pl.BlockSpec((HPAD, SSD_CHUNK), lambda i: (0, chunk_of(i))),
        pl.BlockSpec((1, HPAD), lambda i: (0, 0)), pl.BlockSpec((HPAD, 1), lambda i: (0, 0)),
        pl.BlockSpec((1, HPAD), lambda i: (0, 0)), pl.BlockSpec((HPAD, 1), lambda i: (0, 0)),
    ]


def ssd_fwd(xc, dtr, dtr_t, dtb, dtb_t, alog, alog_t, name):
    def body(xc_ref, dtr_ref, dtrt_ref, dtb_ref, dtbt_ref, al_ref, alt_ref, y_ref, hs_ref, h_scr):
        @pl.when(pl.program_id(0) == 0)
        def _():
            h_scr[...] = jnp.zeros_like(h_scr)

        hs_ref[0] = h_scr[...]
        dt, a_cum, a_cum_t, a_last = _ssd_prefix(dtr_ref[...], dtrt_ref[...], dtb_ref[...], dtbt_ref[...],
                                                  al_ref[...], alt_ref[...])
        for groups in _group_batches():
            slabs = _group_slabs(groups)
            bgs, cgs = zip(*[_bc_of(xc_ref, g) for g in groups])
            xs3 = jnp.stack([xc_ref[:, sl] for sl in slabs]).astype(F32)
            h3 = jnp.stack([h_scr[sl, :] for sl in slabs])
            y3, h3_out = _ssd_group(xs3, bgs, cgs, h3, dt, a_cum, a_cum_t, a_last, groups=groups)
            for j, sl in enumerate(slabs):
                y_ref[:, sl] = y3[j].astype(y_ref.dtype)
                h_scr[sl, :] = h3_out[j]

    return pl.pallas_call(
        body, name=name, grid=(N_CHUNKS,), in_specs=_ssd_in_specs(lambda i: i),
        out_specs=[pl.BlockSpec((SSD_CHUNK, SSD_INNER), lambda i: (i, 0)),
                   pl.BlockSpec((1, SSD_INNER, SSD_STATE), lambda i: (i, 0, 0))],
        out_shape=[SDS((SEQ, SSD_INNER), BF16), SDS((N_CHUNKS, SSD_INNER, SSD_STATE), F32)],
        scratch_shapes=[pltpu.VMEM((SSD_INNER, SSD_STATE), F32)],
        compiler_params=_cparams(("arbitrary",)),
    )(xc, dtr, dtr_t, dtb, dtb_t, alog, alog_t)


def ssd_bwd(xc, dtr, dtr_t, dtb, dtb_t, alog, alog_t, hs, dy, dxs_extra, name):
    rev = lambda i: N_CHUNKS - 1 - i

    def body(xc_ref, dtr_ref, dtrt_ref, dtb_ref, dtbt_ref, al_ref, alt_ref, hs_ref, dy_ref, dxe_ref,
             dxc_ref, ddtr_ref, ddtrt_ref, ddtb_ref, ddtbt_ref, dal_ref, dalt_ref, dh_scr):
        @pl.when(pl.program_id(0) == 0)
        def _():
            dh_scr[...] = jnp.zeros_like(dh_scr)
            for r in (ddtb_ref, ddtbt_ref, dal_ref, dalt_ref):
                r[...] = jnp.zeros_like(r)

        prefix_in = (dtr_ref[...], dtrt_ref[...], dtb_ref[...], dtbt_ref[...], al_ref[...], alt_ref[...])
        (dt, a_cum, a_cum_t, a_last), prefix_vjp = jax.vjp(_ssd_prefix, *prefix_in)
        d_dt = jnp.zeros_like(dt)
        d_acum = jnp.zeros_like(a_cum)
        d_acum_t = jnp.zeros_like(a_cum_t)
        d_alast = jnp.zeros_like(a_last)
        for groups in _group_batches():
            slabs = _group_slabs(groups)
            bgs, cgs = zip(*[_bc_of(xc_ref, g) for g in groups])
            xs3 = jnp.stack([xc_ref[:, sl] for sl in slabs]).astype(F32)
            h3 = jnp.stack([hs_ref[0, sl, :] for sl in slabs])
            _, vjp = jax.vjp(functools.partial(_ssd_group, groups=groups), xs3, bgs, cgs, h3, dt, a_cum, a_cum_t, a_last)
            dy3 = jnp.stack([dy_ref[:, sl] for sl in slabs]).astype(F32)
            dh3 = jnp.stack([dh_scr[sl, :] for sl in slabs])
            dxs3, d_bgs, d_cgs, dh3_in, ddt, dac, dact, dal = vjp((dy3, dh3))
            for j, sl in enumerate(slabs):
                dxc_ref[:, sl] = (dxs3[j] + dxe_ref[:, sl].astype(F32)).astype(dxc_ref.dtype)
                dh_scr[sl, :] = dh3_in[j]
            d_dt, d_acum, d_acum_t, d_alast = d_dt + ddt, d_acum + dac, d_acum_t + dact, d_alast + dal
            for g, d_bg, d_cg in zip(groups, d_bgs, d_cgs):
                dxc_ref[:, B_OFF + g * SSD_STATE:B_OFF + (g + 1) * SSD_STATE] = d_bg.astype(dxc_ref.dtype)
                dxc_ref[:, C_OFF + g * SSD_STATE:C_OFF + (g + 1) * SSD_STATE] = d_cg.astype(dxc_ref.dtype)
        g_dtr, g_dtrt, g_dtb, g_dtbt, g_al, g_alt = prefix_vjp((d_dt, d_acum, d_acum_t, d_alast))
        ddtr_ref[...] = g_dtr
        ddtrt_ref[...] = g_dtrt
        ddtb_ref[...] += g_dtb
        ddtbt_ref[...] += g_dtbt
        dal_ref[...] += g_al
        dalt_ref[...] += g_alt

    in_specs = _ssd_in_specs(rev) + [
        pl.BlockSpec((1, SSD_INNER, SSD_STATE), lambda i: (rev(i), 0, 0)),
        pl.BlockSpec((SSD_CHUNK, SSD_INNER), lambda i: (rev(i), 0)),
        pl.BlockSpec((SSD_CHUNK, SSD_INNER), lambda i: (rev(i), 0)),
    ]
    out_specs = [
        pl.BlockSpec((SSD_CHUNK, SSD_CONV_CH), lambda i: (rev(i), 0)),
        pl.BlockSpec((SSD_CHUNK, HPAD), lambda i: (rev(i), 0)),
        pl.BlockSpec((HPAD, SSD_CHUNK), lambda i: (0, rev(i))),
        pl.BlockSpec((1, HPAD), lambda i: (0, 0)), pl.BlockSpec((HPAD, 1), lambda i: (0, 0)),
        pl.BlockSpec((1, HPAD), lambda i: (0, 0)), pl.BlockSpec((HPAD, 1), lambda i: (0, 0)),
    ]
    out_shape = [SDS((SEQ, SSD_CONV_CH), BF16), SDS((SEQ, HPAD), F32), SDS((HPAD, SEQ), F32),
                 SDS((1, HPAD), F32), SDS((HPAD, 1), F32), SDS((1, HPAD), F32), SDS((HPAD, 1), F32)]
    return pl.pallas_call(
        body, name=name, grid=(N_CHUNKS,), in_specs=in_specs, out_specs=out_specs, out_shape=out_shape,
        scratch_shapes=[pltpu.VMEM((SSD_INNER, SSD_STATE), F32)],
        compiler_params=_cparams(("arbitrary",)),
    )(xc, dtr, dtr_t, dtb, dtb_t, alog, alog_t, hs, dy, dxs_extra)


ATTN_SCALE = ATTN_HEAD_DIM ** -0.5


UNITS_PER_PATTERN = SEQ // ATTN_BLOCK
ATTN_BATCH_FWD = 8
ATTN_BATCH_BWD = 16


def _for_unit_batches(batch, per_trip):
    for g, d in enumerate(ATTN_DILATIONS):
        nb = UNITS_PER_PATTERN // d
        span = d * ATTN_BLOCK

        def trip(t, carry, g=g, d=d, nb=nb, span=span):
            units = []
            for j in range(per_trip):
                i = t * per_trip + j
                r = i >> (nb.bit_length() - 1)
                n = i & (nb - 1)
                start = r + n * span
                prev = jnp.where(n > 0, start - span, start)
                units.append((pl.ds(start, ATTN_BLOCK, stride=d), pl.ds(prev, ATTN_BLOCK, stride=d), n > 0))
            batch(g, units)
            return carry
        lax.fori_loop(0, UNITS_PER_PATTERN // per_trip, trip, 0)


def _unit_operands(units, q_scr, k_scr, v_scr):
    def pair(scr, rows, prows):
        return jnp.concatenate([scr[prows, :], scr[rows, :]], axis=0)
    qb = jnp.stack([q_scr[rows, :] for rows, _, _ in units]).astype(BF16)
    kb = jnp.stack([pair(k_scr, rows, prows) for rows, prows, _ in units]).astype(BF16)
    vb = jnp.stack([pair(v_scr, rows, prows) for rows, prows, _ in units]).astype(BF16)
    return qb, kb, vb


def _unit_scores(qb, kb, units):
    s = jnp.einsum("bqd,bkd->bqk", qb, kb, preferred_element_type=F32) * ATTN_SCALE
    qi = lax.broadcasted_iota(jnp.int32, (ATTN_BLOCK, 2 * ATTN_BLOCK), 0)
    kj = lax.broadcasted_iota(jnp.int32, (ATTN_BLOCK, 2 * ATTN_BLOCK), 1)
    own = (kj >= ATTN_BLOCK) & (kj - ATTN_BLOCK <= qi)
    before = (kj < ATTN_BLOCK) & (kj >= qi)
    keep = jnp.stack([own | (before & has_prev) for _, _, has_prev in units])
    return jnp.where(keep, s, -jnp.inf)


def _head_specs(n_q_groups):
    blk = (SEQ, ATTN_HEAD_DIM)
    q_specs = [pl.BlockSpec(blk, functools.partial(lambda h, g: (0, g * ATTN_KV_HEADS + h), g=g)) for g in range(n_q_groups)]
    head = pl.BlockSpec(blk, lambda h: (0, h))
    table = pl.BlockSpec(blk, lambda h: (0, 0))
    return q_specs, head, table


def attn_fwd(q, k, v, tabs, name):
    q_specs, head, table = _head_specs(ATTN_N_PAT)

    def body(q0_ref, q1_ref, q2_ref, k_ref, v_ref, c_ref, sa_ref, sb_ref, y_ref, lse_ref, *scr):
        qs, og, ls, ks, vs = scr[0:3], scr[3:6], scr[6:9], scr[9], scr[10]
        c, sa, sb = c_ref[...], sa_ref[...], sb_ref[...]
        for g, q_ref in enumerate((q0_ref, q1_ref, q2_ref)):
            qs[g][...] = _rope(q_ref[...].astype(F32), c, sa, sb)
        ks[...] = _rope(k_ref[...].astype(F32), c, sa, sb)
        vs[...] = v_ref[...].astype(F32)

        def batch(g, units):
            qb, kb, vb = _unit_operands(units, qs[g], ks, vs)
            s = _unit_scores(qb, kb, units)
            m = jnp.max(s, axis=2, keepdims=True)
            p = jnp.exp(s - m)
            l = jnp.sum(p, axis=2, keepdims=True)
            o = jnp.einsum("bqk,bkd->bqd", p.astype(BF16), vb, preferred_element_type=F32) / l
            lse_b = m + jnp.log(l)
            for j, (rows, _, _) in enumerate(units):
                og[g][rows, :] = o[j]
                ls[g][rows, :] = jnp.broadcast_to(lse_b[j], (ATTN_BLOCK, LANES))

        _for_unit_batches(batch, ATTN_BATCH_FWD)
        l0, l1, l2 = ls[0][...], ls[1][...], ls[2][...]
        m = jnp.maximum(jnp.maximum(l0, l1), l2)
        e0, e1, e2 = jnp.exp(l0 - m), jnp.exp(l1 - m), jnp.exp(l2 - m)
        den = e0 + e1 + e2
        y_ref[...] = ((e0 * og[0][...] + e1 * og[1][...] + e2 * og[2][...]) / den).astype(y_ref.dtype)
        lse_ref[...] = m + jnp.log(den)

    blk = (SEQ, ATTN_HEAD_DIM)
    return pl.pallas_call(
        body, name=name, grid=(ATTN_KV_HEADS,), in_specs=[*q_specs, head, head, table, table, table],
        out_specs=[head, head], out_shape=[SDS((SEQ, ATTN_OUT), BF16), SDS((SEQ, ATTN_OUT), F32)],
        scratch_shapes=[pltpu.VMEM(blk, F32)] * (3 * ATTN_N_PAT + 2),
        compiler_params=_cparams(("parallel",)),
    )(q, q, q, k, v, *tabs)


def attn_bwd(q, k, v, tabs, y, lse, dy, name):
    q_specs, head, table = _head_specs(ATTN_N_PAT)

    def body(q0_ref, q1_ref, q2_ref, k_ref, v_ref, c_ref, sa_ref, sb_ref, y_ref, lse_ref, dy_ref,
             dq0_ref, dq1_ref, dq2_ref, dk_ref, dv_ref, *scr):
        qs, dqs, ks, dks, dd, dvs, vs = scr[0:3], scr[3:6], scr[6], scr[7], scr[8], scr[9], scr[10]
        c, sa, sb = c_ref[...], sa_ref[...], sb_ref[...]
        for g, q_ref in enumerate((q0_ref, q1_ref, q2_ref)):
            qs[g][...] = _rope(q_ref[...].astype(F32), c, sa, sb)
        ks[...] = _rope(k_ref[...].astype(F32), c, sa, sb)
        vs[...] = v_ref[...].astype(F32)
        dks[...] = jnp.zeros_like(dks)
        dvs[...] = jnp.zeros_like(dvs)
        dyv = dy_ref[...]
        dd[...] = jnp.broadcast_to(jnp.sum(dyv * y_ref[...].astype(F32), axis=1, keepdims=True), dd.shape)

        def batch(g, units):
            qb, kb, vb = _unit_operands(units, qs[g], ks, vs)
            dob = jnp.stack([dy_ref[rows, :] for rows, _, _ in units]).astype(BF16)
            lse_b = jnp.stack([lse_ref[rows, :][:, 0:1] for rows, _, _ in units])
            dsum_b = jnp.stack([dd[rows, :][:, 0:1] for rows, _, _ in units])
            p = jnp.exp(_unit_scores(qb, kb, units) - lse_b)
            dp = jnp.einsum("bqd,bkd->bqk", dob, vb, preferred_element_type=F32)
            ds = (p * (dp - dsum_b) * ATTN_SCALE).astype(BF16)
            dq = jnp.einsum("bqk,bkd->bqd", ds, kb, preferred_element_type=F32)
            dk = jnp.einsum("bqk,bqd->bkd", ds, qb, preferred_element_type=F32)
            dv = jnp.einsum("bqk,bqd->bkd", p.astype(BF16), dob, preferred_element_type=F32)
            for j, (rows, prows, _) in enumerate(units):
                dqs[g][rows, :] = dq[j]
                dks[prows, :] += dk[j, :ATTN_BLOCK]
                dks[rows, :] += dk[j, ATTN_BLOCK:]
                dvs[prows, :] += dv[j, :ATTN_BLOCK]
                dvs[rows, :] += dv[j, ATTN_BLOCK:]

        _for_unit_batches(batch, ATTN_BATCH_BWD)
        for g, dq_ref in enumerate((dq0_ref, dq1_ref, dq2_ref)):
            dq_ref[...] = _rope(dqs[g][...], c, -sa, -sb).astype(dq_ref.dtype)
        dk_ref[...] = _rope(dks[...], c, -sa, -sb).astype(dk_ref.dtype)
        dv_ref[...] = dvs[...].astype(dv_ref.dtype)

    blk = (SEQ, ATTN_HEAD_DIM)
    out = SDS((SEQ, ATTN_OUT), BF16)
    return pl.pallas_call(
        body, name=name, grid=(ATTN_KV_HEADS,), in_specs=[*q_specs, head, head, table, table, table, head, head, head],
        out_specs=[head] * 5, out_shape=[out] * 5,
        scratch_shapes=[pltpu.VMEM(blk, F32)] * (2 * ATTN_N_PAT + 5),
        compiler_params=_cparams(("parallel",)),
    )(q, q, q, k, v, *tabs, y, lse, dy)


def layer_fwd(h, getw, prefetch, small, tabs, li):
    n = f"l{li}_"
    sv = {}
    w = dict(getw(0, h))
    u = rms_fwd(h, small["norm_mix"], n + "rms_mix")
    z = matmul(u, w["w_z"], name=n + "mm_z", tb=True, out_dtype=BF16)
    prefetch(1, z)
    xbc = matmul(u, w["w_xbc"], name=n + "mm_xbc", tb=True, out_dtype=BF16)
    dtr = matmul(u, w["w_dt"], name=n + "mm_dt", tb=True)
    q = matmul(u, w["w_q"], name=n + "mm_q", tb=True, out_dtype=BF16)
    k = matmul(u, w["w_k"], name=n + "mm_k", tb=True, out_dtype=BF16)
    v = matmul(u, w["w_v"], name=n + "mm_v", tb=True, out_dtype=BF16)
    gs = matmul(u, w["w_gs"], name=n + "mm_gs", tb=True, out_dtype=BF16)
    ga = matmul(u, w["w_ga"], name=n + "mm_ga", tb=True, out_dtype=BF16)
    xc = conv_fwd(xbc, w["conv_w"], small["conv_b"], n + "conv")
    dtr_t = dtr.T
    y_ssd, hs = ssd_fwd(xc, dtr, dtr_t, small["dt_bias"], small["dt_bias"].T, small["a_log"], small["a_log"].T, n + "ssd")
    yn = ssd_post_fwd(y_ssd, xc, z, small["d_skip_x"], small["ssd_norm"], n + "ssd_post")
    y_attn, lse = attn_fwd(q, k, v, tabs, n + "attn")
    w.update(getw(1, y_ssd))
    a, b, merged = gate_fwd(yn, w["w_ssd_branch"], y_attn, w["w_attn_branch"], gs, ga, n + "mm_ab_gate")
    h1 = matmul(merged, w["w_out"], name=n + "mm_o", add=h)
    w.update(getw(2, h1))
    u2 = rms_fwd(h1, small["norm_ffn"], n + "rms_ffn")
    gu, act = gate_up_fwd(u2, w["w_gate_up"], n + "mm_gu_swiglu")
    h2 = matmul(act, w["w_down"], name=n + "mm_down", add=h1)
    sv.update(h=h, u=u, z=z, xbc=xbc, dtr=dtr, dtr_t=dtr_t, gs=gs, ga=ga, xc=xc, y_ssd=y_ssd, hs=hs, yn=yn,
              q=q, k=k, v=v, y_attn=y_attn, lse=lse, a=a, b=b, merged=merged, h1=h1, u2=u2, gu=gu, act=act, w=w)
    return h2, sv


def layer_bwd(dh, sv, small, tabs, li, emit):
    n = f"l{li}_b_"
    w = sv["w"]
    gw, gsm = {}, {}
    gw["w_down"] = matmul(sv["act"], dh, name=n + "mm_dwdown", ta=True, out_dtype=BF16)
    dgu = gate_up_bwd(dh, w["w_down"], sv["gu"], n + "mm_dact_swiglu")
    gw["w_gate_up"] = matmul(dgu, sv["u2"], name=n + "mm_dwgu", ta=True, out_dtype=BF16)
    tok = emit(2, gw)
    du2 = matmul(dgu, w["w_gate_up"], name=n + "mm_du2")
    dh1, gsm["norm_ffn"] = rms_bwd(sv["h1"], du2, dh, small["norm_ffn"] + tok, n + "rms_ffn")
    gw["w_out"] = matmul(sv["merged"], dh1, name=n + "mm_dwo", ta=True, out_dtype=BF16)
    da, db, dgs, dga = gate_bwd(dh1, w["w_out"], sv["a"], sv["b"], sv["gs"], sv["ga"], n + "mm_dmerged_gate")
    gw["w_ssd_branch"] = matmul(sv["yn"], da, name=n + "mm_dwa", ta=True, out_dtype=BF16)
    gw["w_attn_branch"] = matmul(sv["y_attn"], db, name=n + "mm_dwb", ta=True, out_dtype=BF16)
    tok = emit(1, gw)
    dyn = matmul(da, w["w_ssd_branch"], name=n + "mm_dyn", tb=True, out_dtype=BF16)
    dyattn = matmul(db, w["w_attn_branch"], name=n + "mm_dyattn", tb=True)
    dy_ssd, dxs_extra, dz, gsm["d_skip_x"], gsm["ssd_norm"] = ssd_post_bwd(
        sv["y_ssd"], sv["xc"], sv["z"], small["d_skip_x"] + tok, small["ssd_norm"], dyn, n + "ssd_post")
    dxc, ddtr, ddtr_t, ddtb, ddtb_t, dal, dal_t = ssd_bwd(
        sv["xc"], sv["dtr"], sv["dtr_t"], small["dt_bias"], small["dt_bias"].T, small["a_log"], small["a_log"].T,
        sv["hs"], dy_ssd, dxs_extra, n + "ssd")
    ddtr = (ddtr + ddtr_t.T).astype(BF16)
    gsm["dt_bias"] = ddtb + ddtb_t.T
    gsm["a_log"] = dal + dal_t.T
    dxbc, gw["conv_w"], gsm["conv_b"] = conv_bwd(sv["xbc"], w["conv_w"], small["conv_b"], dxc, n + "conv")
    dq0, dq1, dq2, dk, dv = attn_bwd(sv["q"], sv["k"], sv["v"], tabs, sv["y_attn"], sv["lse"], dyattn, n + "attn")
    u = sv["u"]
    segs = [("w_z", dz), ("w_xbc", dxbc), ("w_dt", ddtr), ("w_q0", dq0), ("w_q1", dq1), ("w_q2", dq2),
            ("w_k", dk), ("w_v", dv), ("w_gs", dgs), ("w_ga", dga)]
    gin = [matmul(dseg, u, name=n + "mm_d" + key, ta=True, out_dtype=BF16) for key, dseg in segs]
    gin[2] = gin[2][:SSD_HEADS]
    gw["w_in"] = jnp.concatenate(gin, axis=0)
    tok = emit(0, gw)
    du = jnp.zeros((SEQ, D_MODEL), F32) + tok
    for si, (key, dseg) in enumerate(segs):
        du = matmul(dseg, w[key], name=n + "mm_du_" + key, add=du)
        if si == 2:
            du = du + emit(-1, du)
    dh0, gsm["norm_mix"] = rms_bwd(sv["h"], du, dh1, small["norm_mix"] + tok, n + "rms_mix")
    return dh0, gsm


def _my_place():
    return lax.axis_index("x"), lax.axis_index("y"), lax.axis_index("c")


def _flip(place, k):
    x, y, c = place
    return (1 - x if k & 4 else x, 1 - y if k & 2 else y, 1 - c if k & 1 else c)


def _index(place):
    return 4 * place[0] + 2 * place[1] + place[2]


ANY = pl.BlockSpec(memory_space=pl.ANY)
CHIP_FLIPS = (4, 2, 6)
SELF_AND_CHIPS = (0,) + CHIP_FLIPS


def all_gather(xs, name):
    na = len(xs)

    def body(*refs):
        x_refs, o_refs = refs[:na], refs[na:2 * na]
        send_sems, recv_sems, local_sems = refs[2 * na:]
        me = _my_place()
        sibling = _flip(me, 1)
        chips = [_flip(me, f) for f in CHIP_FLIPS]

        def copy(a, kk, block, to, src=None):
            dst = o_refs[a].at[_index(block)]
            return pltpu.make_async_remote_copy(
                src_ref=dst if src is None else src, dst_ref=dst, send_sem=send_sems.at[a, kk],
                recv_sem=recv_sems.at[a, kk], device_id=to, device_id_type=MESH)

        mine = [pltpu.make_async_copy(x_refs[a], o_refs[a].at[_index(me)], local_sems.at[a]) for a in range(na)]
        for cp in mine:
            cp.start()
        first = []
        for j, chip in enumerate(chips):
            first += [copy(a, 1 + j, me, chip, src=x_refs[a]) for a in range(na)]
        first += [copy(a, 0, me, sibling, src=x_refs[a]) for a in range(na)]
        for cp in first:
            cp.start()
        passed = []
        for j, chip in enumerate(chips):
            for a in range(na):
                copy(a, 1 + j, chip, me).wait_recv()
                cp = copy(a, 4 + j, chip, sibling)
                cp.start()
                passed.append(cp)
        for a in range(na):
            copy(a, 0, sibling, me).wait_recv()
        for j, chip in enumerate(chips):
            for a in range(na):
                copy(a, 4 + j, _flip(chip, 1), me).wait_recv()
        for cp in first + passed:
            cp.wait_send()
        for cp in mine:
            cp.wait()

    return pl.pallas_call(
        body, name=name, in_specs=[ANY] * na, out_specs=[ANY] * na,
        out_shape=[SDS((N_DEV,) + t.shape, t.dtype) for t in xs],
        scratch_shapes=[pltpu.SemaphoreType.DMA((na, N_DEV - 1)), pltpu.SemaphoreType.DMA((na, N_DEV - 1)),
                        pltpu.SemaphoreType.DMA((na,))],
    )(*xs)


HBM = pl.BlockSpec(memory_space=pltpu.HBM)
SEM = pl.BlockSpec(memory_space=pltpu.SEMAPHORE)
EFFECT = pltpu.SideEffectType.DATAFLOW_SIDE_EFFECTING


def _split_copy(src_ref, land_ref, send_sem, recv_sem, me, kk, scatter, landed_from_peer):
    peer = _flip(me, kk)
    src = src_ref.at[_index(peer)] if scatter else src_ref
    dst = land_ref.at[_index(peer if landed_from_peer else me)]
    return pltpu.make_async_remote_copy(src_ref=src, dst_ref=dst, send_sem=send_sem, recv_sem=recv_sem,
                                        device_id=peer, device_id_type=MESH)


ALL_PEERS = tuple(range(1, N_DEV))
EVERYONE = (0,) + ALL_PEERS


def exchange_start(srcs, lands, group_sizes, scatter, name, peers=ALL_PEERS):
    na, ng = len(srcs), len(group_sizes)

    def body(*refs):
        s_refs, l_refs = refs[:na], refs[na:2 * na]
        sems = refs[2 * na:2 * na + 2 * ng]
        token = refs[-1]
        me = _my_place()
        a = 0
        for gi, gsz in enumerate(group_sizes):
            for j in range(gsz):
                for pi, kk in enumerate(peers):
                    slot = j * len(peers) + pi
                    _split_copy(s_refs[a], l_refs[a], sems[2 * gi].at[slot], sems[2 * gi + 1].at[slot],
                                me, kk, scatter, False).start()
                a += 1
        token[...] = jnp.zeros_like(token)

    sem_shapes = []
    for gsz in group_sizes:
        sem_shapes += [pltpu.SemaphoreType.DMA((gsz * len(peers),))] * 2
    ins = [pltpu.with_memory_space_constraint(t, pltpu.HBM) for t in (*srcs, *lands)]
    res = pl.pallas_call(
        body, name=name, in_specs=[HBM] * (2 * na),
        out_specs=[SEM] * (2 * ng) + [HBM] * (2 * na) + [pl.BlockSpec(memory_space=pltpu.VMEM)],
        out_shape=sem_shapes + [pltpu.HBM(t.shape, t.dtype) for t in ins] + [SDS((8, LANES), F32)],
        input_output_aliases={i: 2 * ng + i for i in range(2 * na)},
        compiler_params=pltpu.CompilerParams(has_side_effects=EFFECT),
    )(*ins)
    sems = [(res[2 * gi], res[2 * gi + 1]) for gi in range(ng)]
    thru = res[2 * ng:2 * ng + 2 * na]
    return sems, thru[:na], thru[na:], res[-1]


def _wait_split_copies(s_refs, l_refs, send_sems, recv_sems, scatter, peers):
    me = _my_place()
    for j in range(len(s_refs)):
        for pi, kk in enumerate(peers):
            slot = j * len(peers) + pi
            cp = _split_copy(s_refs[j], l_refs[j], send_sems.at[slot], recv_sems.at[slot], me, kk, scatter, True)
            cp.wait_send()
            cp.wait_recv()


def exchange_wait(srcs, lands, sems, after, scatter, name, peers=ALL_PEERS):
    n = len(srcs)

    def body(*refs):
        s_refs, l_refs = refs[:n], refs[n:2 * n]
        _wait_split_copies(s_refs, l_refs, refs[2 * n], refs[2 * n + 1], scatter, peers)

    res = pl.pallas_call(
        body, name=name, in_specs=[HBM] * (2 * n) + [SEM, SEM, ANY], out_specs=[HBM] * (2 * n),
        out_shape=[pltpu.HBM(t.shape, t.dtype) for t in (*srcs, *lands)],
        input_output_aliases={i: i for i in range(2 * n)},
        compiler_params=pltpu.CompilerParams(has_side_effects=EFFECT),
    )(*srcs, *lands, sems[0], sems[1], after)
    return res[n:]


def _sibling_copies(l_refs, send_sems, recv_sems, arriving):
    me = _my_place()
    sibling = _flip(me, 1)
    held = [me] + [_flip(me, f) for f in CHIP_FLIPS]
    copies = []
    for j, land in enumerate(l_refs):
        for bi, place in enumerate(held):
            blk = land.at[_index(_flip(place, 1) if arriving else place)]
            slot = j * len(held) + bi
            copies.append(pltpu.make_async_remote_copy(src_ref=blk, dst_ref=blk, send_sem=send_sems.at[slot],
                                                       recv_sem=recv_sems.at[slot], device_id=sibling, device_id_type=MESH))
    return copies


def gather_forward(srcs, lands, sems, after, name):
    n = len(srcs)

    def body(*refs):
        s_refs, l_refs = refs[:n], refs[n:2 * n]
        _wait_split_copies(s_refs, l_refs, refs[2 * n], refs[2 * n + 1], False, SELF_AND_CHIPS)
        for cp in _sibling_copies(l_refs, refs[2 * n + 3], refs[2 * n + 4], False):
            cp.start()

    n_slots = n * (1 + len(CHIP_FLIPS))
    res = pl.pallas_call(
        body, name=name, in_specs=[HBM] * (2 * n) + [SEM, SEM, ANY],
        out_specs=[SEM, SEM] + [HBM] * (2 * n),
        out_shape=[pltpu.SemaphoreType.DMA((n_slots,))] * 2 + [pltpu.HBM(t.shape, t.dtype) for t in (*srcs, *lands)],
        input_output_aliases={i: 2 + i for i in range(2 * n)},
        compiler_params=pltpu.CompilerParams(has_side_effects=EFFECT),
    )(*srcs, *lands, sems[0], sems[1], after)
    return (res[0], res[1]), res[2 + n:]


def gather_finish(lands, sems, after, name):
    n = len(lands)

    def body(*refs):
        l_refs = refs[:n]
        for cp in _sibling_copies(l_refs, refs[n], refs[n + 1], True):
            cp.wait_send()
            cp.wait_recv()

    return pl.pallas_call(
        body, name=name, in_specs=[HBM] * n + [SEM, SEM, ANY], out_specs=[HBM] * n,
        out_shape=[pltpu.HBM(t.shape, t.dtype) for t in lands],
        input_output_aliases={i: i for i in range(n)},
        compiler_params=pltpu.CompilerParams(has_side_effects=EFFECT),
    )(*lands, sems[0], sems[1], after)


def landing_zone(block):
    return lax.empty((N_DEV,) + block.shape, block.dtype)


def sum_parts(parts, name, row_major_3d=False):
    n_parts, r, c = parts.shape
    tc = _pick(c, (256, 128))

    def body(p_ref, o_ref):
        acc = p_ref[0].astype(F32)
        for i in range(1, n_parts):
            acc = acc + p_ref[i].astype(F32)
        if row_major_3d:
            o_ref[:, 0, :] = acc
        else:
            o_ref[...] = acc

    out_spec = pl.BlockSpec((r, 1, tc), lambda i: (0, 0, i)) if row_major_3d else pl.BlockSpec((r, tc), lambda i: (0, i))
    return pl.pallas_call(
        body, name=name, grid=(c // tc,), in_specs=[pl.BlockSpec((n_parts, r, tc), lambda i: (0, 0, i))],
        out_specs=out_spec, out_shape=SDS((r, 1, c) if row_major_3d else (r, c), F32),
        compiler_params=_cparams(("parallel",)),
    )(parts)


SELF_CHIP_FIRST = (0,) + CHIP_FLIPS
N_CHIPS = len(SELF_CHIP_FIRST)


def _chip_index(place):
    return 2 * place[0] + place[1]


def _split_call(body, name, arrays, n_sems, sems_in=None, after=None):
    n = len(arrays)
    thru = [pltpu.HBM(t.shape, t.dtype) for t in arrays]
    if sems_in is None:
        ins = [pltpu.with_memory_space_constraint(t, pltpu.HBM) for t in arrays]
        res = pl.pallas_call(
            body, name=name, in_specs=[HBM] * n, out_specs=[SEM, SEM] + [HBM] * n + [pl.BlockSpec(memory_space=pltpu.VMEM)],
            out_shape=[pltpu.SemaphoreType.DMA((n_sems,))] * 2 + thru + [SDS((8, LANES), F32)],
            input_output_aliases={i: 2 + i for i in range(n)}, compiler_params=pltpu.CompilerParams(has_side_effects=EFFECT),
        )(*ins)
        return (res[0], res[1]), res[2:2 + n], res[-1]
    return pl.pallas_call(
        body, name=name, in_specs=[HBM] * n + [SEM, SEM, ANY], out_specs=[HBM] * n, out_shape=thru,
        input_output_aliases={i: i for i in range(n)}, compiler_params=pltpu.CompilerParams(has_side_effects=EFFECT),
    )(*arrays, sems_in[0], sems_in[1], after)


def _pair_copies(p_ref, l_ref, ssem, rsem, arriving):
    me = _my_place()
    copies = []
    for j, f in enumerate(SELF_CHIP_FIRST):
        owner = _flip(me, f) if arriving else _flip(_flip(me, f), 1)
        blk = _index(owner)
        copies.append(pltpu.make_async_remote_copy(src_ref=p_ref.at[blk], dst_ref=l_ref.at[blk], send_sem=ssem.at[j],
                                                   recv_sem=rsem.at[j], device_id=_flip(me, 1), device_id_type=MESH))
    return copies


def _chip_copies(p_ref, l_ref, ssem, rsem, arriving):
    me = _my_place()
    copies = []
    for j, f in enumerate(SELF_CHIP_FIRST):
        target = _flip(me, f)
        src = p_ref.at[_chip_index(target)]
        dst = l_ref.at[_chip_index(target if arriving else me)]
        copies.append(pltpu.make_async_remote_copy(src_ref=src, dst_ref=dst, send_sem=ssem.at[j], recv_sem=rsem.at[j],
                                                   device_id=target, device_id_type=MESH))
    return copies


def pair_start(parts, name):
    def body(p_ref, l_ref, ssem, rsem, p_thru, l_thru, token):
        for cp in _pair_copies(p_ref, l_ref, ssem, rsem, False):
            cp.start()
        token[...] = jnp.zeros_like(token)
    return _split_call(body, name, [parts, lax.empty(parts.shape, parts.dtype)], N_CHIPS)


def pair_sum(parts, land, sems, after, name):
    def wait_body(p_ref, l_ref, ssem, rsem, after_ref, p_thru, l_thru):
        for cp in _pair_copies(p_ref, l_ref, ssem, rsem, True):
            cp.wait_send()
            cp.wait_recv()
    parts, land = _split_call(wait_body, name + "_wait", [parts, land], N_CHIPS, sems, after)
    _, r, c = parts.shape
    tc = _pick(c, (256, 128))

    def body(core_ref, p_ref, l_ref, o_ref):
        o_ref[0] = (p_ref[0].astype(F32) + l_ref[0].astype(F32)).astype(o_ref.dtype)

    spec = pl.BlockSpec((1, r, tc), lambda k, j, core_ref: (2 * k + core_ref[0], 0, j))
    return pl.pallas_call(
        body, name=name,
        grid_spec=pltpu.PrefetchScalarGridSpec(
            num_scalar_prefetch=1, grid=(N_CHIPS, c // tc), in_specs=[spec, spec],
            out_specs=pl.BlockSpec((1, r, tc), lambda k, j, core_ref: (k, 0, j))),
        out_shape=SDS((N_CHIPS, r, c), parts.dtype), compiler_params=_cparams(("parallel", "parallel")),
    )(lax.axis_index("c").astype(jnp.int32).reshape(1), parts, land)


def chip_start(pre, name):
    def body(p_ref, l_ref, ssem, rsem, p_thru, l_thru, token):
        for cp in _chip_copies(p_ref, l_ref, ssem, rsem, False):
            cp.start()
        token[...] = jnp.zeros_like(token)
    return _split_call(body, name, [pre, lax.empty(pre.shape, pre.dtype)], N_CHIPS)


def chip_wait(pre, land, sems, after, name):
    def body(p_ref, l_ref, ssem, rsem, after_ref, p_thru, l_thru):
        for cp in _chip_copies(p_ref, l_ref, ssem, rsem, True):
            cp.wait_send()
            cp.wait_recv()
    return _split_call(body, name, [pre, land], N_CHIPS, sems, after)[1]


ADAMW_BLOCK_BYTES = 2 * 1024 * 1024


def adamw(w, g, m, v, name):
    shape = w.shape
    lay, rows, cols = ((1, 1) + shape)[-3:]
    tr = _pick(rows, (256, 128))
    tc = cols if tr * cols * 4 <= ADAMW_BLOCK_BYTES else _pick(cols, (256, 128))
    c1 = 1.0 / (1.0 - ADAM_B1 ** ADAM_STEP)
    c2 = 1.0 / (1.0 - ADAM_B2 ** ADAM_STEP)

    def body(w_ref, g_ref, m_ref, v_ref, d_ref, nm_ref, nv_ref):
        gg = g_ref[...]
        nm = ADAM_B1 * m_ref[...] + (1.0 - ADAM_B1) * gg
        nv = ADAM_B2 * v_ref[...] + (1.0 - ADAM_B2) * (gg * gg)
        d_ref[...] = -ADAM_LR * ((nm * c1) / (jnp.sqrt(nv * c2) + ADAM_EPS) + ADAM_WD * w_ref[...])
        nm_ref[...] = nm
        nv_ref[...] = nv

    spec = pl.BlockSpec((1, tr, tc), lambda l, i, j: (l, i, j))
    outs = pl.pallas_call(
        body, name=name, grid=(lay, rows // tr, cols // tc), in_specs=[spec] * 4, out_specs=[spec] * 3,
        out_shape=[SDS((lay, rows, cols), F32)] * 3, compiler_params=_cparams(("parallel",) * 3),
    )(*[t.reshape(lay, rows, cols) for t in (w, g, m, v)])
    return [o.reshape(shape) for o in outs]


def adamw_layer_inner(w, gs, m, v, name):
    rows, lay, cols = w.shape
    tr = _pick(rows, (256, 220, 128))
    c1 = 1.0 / (1.0 - ADAM_B1 ** ADAM_STEP)
    c2 = 1.0 / (1.0 - ADAM_B2 ** ADAM_STEP)

    def body(*refs):
        w_ref, m_ref, v_ref = refs[:3]
        g_refs = refs[3:3 + lay]
        go_ref, d_ref, nm_ref, nv_ref = refs[3 + lay:]
        for l, g_ref in enumerate(g_refs):
            gg = g_ref[:, 0, :]
            nm = ADAM_B1 * m_ref[:, l, :] + (1.0 - ADAM_B1) * gg
            nv = ADAM_B2 * v_ref[:, l, :] + (1.0 - ADAM_B2) * (gg * gg)
            d_ref[:, l, :] = -ADAM_LR * ((nm * c1) / (jnp.sqrt(nv * c2) + ADAM_EPS) + ADAM_WD * w_ref[:, l, :])
            go_ref[:, l, :] = gg
            nm_ref[:, l, :] = nm
            nv_ref[:, l, :] = nv

    inner = pl.BlockSpec((tr, lay, cols), lambda i: (i, 0, 0))
    plain = pl.BlockSpec((tr, 1, cols), lambda i: (i, 0, 0))
    return pl.pallas_call(
        body, name=name, grid=(rows // tr,), in_specs=[inner] * 3 + [plain] * lay, out_specs=[inner] * 4,
        out_shape=[SDS((rows, lay, cols), F32)] * 4, compiler_params=_cparams(("parallel",)),
    )(w, m, v, *gs)


BIG = ("w_in", "conv_w", "w_ssd_branch", "w_attn_branch", "w_out", "w_gate_up", "w_down")
TRANSPOSED = ("w_in", "w_gate_up")
SMALL = ("norm_mix", "conv_b", "dt_bias", "a_log", "d_skip", "ssd_norm", "norm_ffn")
SMALL_SIZE = {"norm_mix": 1024, "conv_b": 3072, "dt_bias": 32, "a_log": 32, "d_skip": 32, "ssd_norm": 2048, "norm_ffn": 1024}
FLAT_W = 512
SMALL_TOTAL = DEPTH * sum(SMALL_SIZE.values()) + D_MODEL + LANES
SMALL_ROWS = 32
assert SMALL_ROWS * FLAT_W >= SMALL_TOTAL


GROUPS = (("w_in", "conv_w"), ("w_ssd_branch", "w_attn_branch", "w_out"), ("w_gate_up", "w_down"))


def to_wire(k, shard):
    if k in TRANSPOSED:
        return shard.T.astype(BF16)
    return shard if k == "conv_w" else shard.astype(BF16)


def full_weights(k, g):
    if k == "conv_w":
        return {k: g.transpose(1, 0, 2).reshape(SSD_CONV, SSD_CONV_CH)}
    full = g.reshape(-1, g.shape[-1])
    if k != "w_in":
        return {k: full}
    w, off = {}, 0
    for nm, r in IN_ROWS:
        w[nm] = full[off:off + r]
        off += r
    w["w_q"] = full[sum(r for _, r in IN_ROWS[:3]):sum(r for _, r in IN_ROWS[:6])]
    w["w_dt"] = jnp.pad(w["w_dt"], ((0, HPAD - SSD_HEADS), (0, 0)))
    return w


def grads_to_wire(k, g):
    if k == "conv_w":
        return g.reshape(SSD_CONV, N_DEV, SSD_CONV_CH // N_DEV).transpose(1, 0, 2)
    return g.reshape(N_DEV, g.shape[0] // N_DEV, g.shape[1])


def _pad_heads(t):
    return jnp.pad(t.reshape(1, SSD_HEADS), ((0, 0), (0, HPAD - SSD_HEADS)))


def local_step(x, target, getw, prefetch, emit, smalls, norm_final):
    tabs = rope_tables()
    sms = []
    for li in range(DEPTH):
        s = smalls[li]
        sms.append({
            "norm_mix": s["norm_mix"].reshape(1, -1), "conv_b": s["conv_b"].reshape(1, -1),
            "dt_bias": _pad_heads(s["dt_bias"]), "a_log": _pad_heads(s["a_log"]),
            "d_skip_x": jnp.repeat(s["d_skip"], SSD_HEAD_DIM).reshape(1, -1),
            "ssd_norm": s["ssd_norm"].reshape(1, -1), "norm_ffn": s["norm_ffn"].reshape(1, -1)})
    h = x
    saved = []
    for li in range(DEPTH):
        h, sv = layer_fwd(h, functools.partial(getw, li), functools.partial(prefetch, li), sms[li], tabs, li)
        saved.append(sv)
    dh, g_final, loss = loss_head(h, target, norm_final.reshape(1, -1), "loss_head")
    gsms = [None] * DEPTH
    for li in reversed(range(DEPTH)):
        dh, gsm = layer_bwd(dh, saved[li], sms[li], tabs, li, functools.partial(emit, li))
        gsms[li] = {
            "norm_mix": gsm["norm_mix"].reshape(-1), "conv_b": gsm["conv_b"].reshape(-1),
            "dt_bias": gsm["dt_bias"][0, :SSD_HEADS], "a_log": gsm["a_log"][0, :SSD_HEADS],
            "d_skip": gsm["d_skip_x"].reshape(SSD_HEADS, SSD_HEAD_DIM).sum(axis=1),
            "ssd_norm": gsm["ssd_norm"].reshape(-1), "norm_ffn": gsm["norm_ffn"].reshape(-1)}
    return loss, dh, gsms, g_final.reshape(-1)


def kernel(x, norm_mix, w_in, conv_w, conv_b, dt_bias, a_log, d_skip, ssd_norm, w_ssd_branch, w_attn_branch, w_out, norm_ffn, w_gate_up, w_down, norm_final, loss_target, m_norm_mix, m_w_in, m_conv_w, m_conv_b, m_dt_bias, m_a_log, m_d_skip, m_ssd_norm, m_w_ssd_branch, m_w_attn_branch, m_w_out, m_norm_ffn, m_w_gate_up, m_w_down, m_norm_final, v_norm_mix, v_w_in, v_conv_w, v_conv_b, v_dt_bias, v_a_log, v_d_skip, v_ssd_norm, v_w_ssd_branch, v_w_attn_branch, v_w_out, v_norm_ffn, v_w_gate_up, v_w_down, v_norm_final):
    wv = dict(norm_mix=norm_mix, w_in=w_in, conv_w=conv_w, conv_b=conv_b, dt_bias=dt_bias, a_log=a_log, d_skip=d_skip,
              ssd_norm=ssd_norm, w_ssd_branch=w_ssd_branch, w_attn_branch=w_attn_branch, w_out=w_out, norm_ffn=norm_ffn,
              w_gate_up=w_gate_up, w_down=w_down, norm_final=norm_final)
    mv = dict(norm_mix=m_norm_mix, w_in=m_w_in, conv_w=m_conv_w, conv_b=m_conv_b, dt_bias=m_dt_bias, a_log=m_a_log,
              d_skip=m_d_skip, ssd_norm=m_ssd_norm, w_ssd_branch=m_w_ssd_branch, w_attn_branch=m_w_attn_branch,
              w_out=m_w_out, norm_ffn=m_norm_ffn, w_gate_up=m_w_gate_up, w_down=m_w_down, norm_final=m_norm_final)
    vv = dict(norm_mix=v_norm_mix, w_in=v_w_in, conv_w=v_conv_w, conv_b=v_conv_b, dt_bias=v_dt_bias, a_log=v_a_log,
              d_skip=v_d_skip, ssd_norm=v_ssd_norm, w_ssd_branch=v_w_ssd_branch, w_attn_branch=v_w_attn_branch,
              w_out=v_w_out, norm_ffn=v_norm_ffn, w_gate_up=v_w_gate_up, w_down=v_w_down, norm_final=v_norm_final)
    order = ("norm_mix", "w_in", "conv_w", "conv_b", "dt_bias", "a_log", "d_skip", "ssd_norm", "w_ssd_branch",
             "w_attn_branch", "w_out", "norm_ffn", "w_gate_up", "w_down", "norm_final")

    smalls = [{k: wv[k][li] for k in SMALL} for li in range(DEPTH)]
    n_groups = len(GROUPS)

    first_lands = all_gather([to_wire(k, wv[k][0]) for k in GROUPS[0]], "gather_first")
    later = [(li, gi) for li in range(DEPTH) for gi in range(n_groups)][1:]
    behind_first = first_lands[1][0, 0, 0] * 0.0
    srcs = [to_wire(k, wv[k][li] + behind_first if k == "conv_w" else wv[k][li]) for li, gi in later for k in GROUPS[gi]]
    sizes = [len(GROUPS[gi]) for _, gi in later]
    w_sems, w_srcs, w_lands, token = exchange_start(srcs, [landing_zone(s) for s in srcs], sizes, False,
                                                    "gather_start", peers=SELF_AND_CHIPS)
    smalls[0]["norm_mix"] = smalls[0]["norm_mix"] + token[0, 0]
    second_leg = {}

    def forward(slot, after):
        if slot < len(later) and slot not in second_leg:
            sl = slice(sum(sizes[:slot]), sum(sizes[:slot + 1]))
            second_leg[slot] = gather_forward(w_srcs[sl], w_lands[sl], w_sems[slot], after, f"gather_forward_{slot}")

    def prefetch(li, gi, after):
        if (li, gi) == later[0]:
            forward(0, after)

    def getw(li, gi, after):
        if (li, gi) == (0, 0):
            lands = first_lands
        else:
            slot = later.index((li, gi))
            forward(slot, after)
            sems2, lands2 = second_leg[slot]
            lands = gather_finish(lands2, sems2, after, f"gather_finish_{li}_{gi}")
            forward(slot + 1, lands[0])
        w = {}
        for k, land in zip(GROUPS[gi], lands):
            w.update(full_weights(k, land))
        return w

    pending = []

    last = {}

    def emit(li, gi, gw):
        if gi == -1:
            if li != 0:
                return 0.0
            pre = pair_sum(last["parts"], last["land"], last["sems"], gw, "grads_pair_sum")
            last["sems2"], (last["pre"], last["land2"]), tok = chip_start(pre, "grads_chip_start")
            return tok[0, 0]
        names, tok0 = GROUPS[gi], 0.0
        if (li, gi) == (0, 0):
            last["sems"], (last["parts"], last["land"]), t0 = pair_start(grads_to_wire("w_in", gw["w_in"]), "grads_pair_start")
            names, tok0 = ("conv_w",), t0[0, 0]
        parts = [grads_to_wire(k, gw[k]) for k in names]
        lands = [landing_zone(p[0]) for p in parts]
        sems, p_thru, l_thru, tok = exchange_start(parts, lands, [len(parts)], True, f"grads_start_{li}_{gi}", peers=EVERYONE)
        pending.append((li, gi, names, sems[0], p_thru, l_thru))
        return tok[0, 0] + tok0

    loss_p, dx, gsms, g_final = local_step(x[0], loss_target[0], getw, prefetch, emit, smalls, norm_final)

    grads, deltas, new_m, new_v = {}, {}, {}, {}

    def update(k):
        if k == "w_in":
            inner = lambda t: t.transpose(2, 0, 1)
            outs = adamw_layer_inner(inner(wv[k]), shard_g[k], inner(mv[k]), inner(vv[k]), "adamw_" + k)
            grads[k], deltas[k], new_m[k], new_v[k] = (t.transpose(1, 2, 0) for t in outs)
            return outs[3]
        if k in BIG:
            grads[k] = jnp.stack([g.T if k in TRANSPOSED else g for g in shard_g[k]])
        deltas[k], new_m[k], new_v[k] = adamw(wv[k], grads[k], mv[k], vv[k], "adamw_" + k)
        return new_v[k]

    shard_g = {k: [None] * DEPTH for k in BIG}

    def collect(entry, after):
        li, gi, names, sems, p_thru, l_thru = entry
        recv = exchange_wait(p_thru, l_thru, sems, after, True, f"grads_wait_{li}_{gi}", peers=EVERYONE)
        for k, r in zip(names, recv):
            if k == "conv_w":
                r = r.reshape(N_DEV, 1, -1)
            after = sum_parts(r, f"sum_{k}_{li}", row_major_3d=(k == "w_in"))
            shard_g[k][li] = after if k in TRANSPOSED else after.reshape(wv[k].shape[1:])
        if (li, gi) == (0, 0):
            land2 = chip_wait(last["pre"], last["land2"], last["sems2"], after, "grads_chip_wait")
            after = sum_parts(land2, "sum_w_in_0", row_major_3d=True)
            shard_g["w_in"][0] = after
        return after

    after = dx
    for entry in pending[:-1]:
        after = collect(entry, after)
    done = [after[:1, :1].reshape(1)]
    for gi in (2, 1):
        for k in GROUPS[gi]:
            done.append(update(k).reshape(-1)[:1])

    flat = [gsms[li][k] for li in range(DEPTH) for k in SMALL] + [g_final, loss_p.reshape(-1)]
    flat.append(jnp.zeros((SMALL_ROWS * FLAT_W - SMALL_TOTAL,), F32))
    small_all = all_gather([jnp.concatenate(flat).reshape(SMALL_ROWS, FLAT_W)], "gather_small")[0]
    small_sum = sum_parts(small_all, "sum_small").reshape(-1)
    off = 0
    per_layer = {k: [] for k in SMALL}
    for li in range(DEPTH):
        for k in SMALL:
            per_layer[k].append(small_sum[off:off + SMALL_SIZE[k]])
            off += SMALL_SIZE[k]
    for k in SMALL:
        grads[k] = jnp.stack(per_layer[k])
    grads["norm_final"] = small_sum[off:off + D_MODEL]
    loss = small_sum[off + D_MODEL]
    for k in (*SMALL, "norm_final"):
        done.append(update(k).reshape(-1)[:1])

    collect(pending[-1], jnp.concatenate(done))
    for k in GROUPS[0]:
        update(k)

    return (loss, dx.reshape(x.shape), *[grads[k] for k in order], *[deltas[k] for k in order],
            *[new_m[k] for k in order], *[new_v[k] for k in order])
```

```python
import functools

import jax
import jax.numpy as jnp
from jax import lax
from jax.experimental import pallas as pl
from jax.experimental.pallas import tpu as pltpu

F32, BF16 = jnp.float32, jnp.bfloat16
SDS = jax.ShapeDtypeStruct
MESH = pl.DeviceIdType.MESH

D_MODEL = 1024
SEQ = 2048
DEPTH = 2
RMS_EPS = 1e-5
SSD_INNER = 2048
SSD_HEAD_DIM = 64
SSD_HEADS = 32
SSD_STATE = 128
SSD_GROUPS = 4
SSD_CONV = 4
SSD_CHUNK = 128
SSD_CONV_CH = 3072
ATTN_HEAD_DIM = 128
ATTN_KV_HEADS = 8
ATTN_DILATIONS = (1, 4, 16)
ATTN_N_PAT = 3
ATTN_BLOCK = 128
ATTN_OUT = 1024
ROPE_THETA = 500000.0
ROPE_DIM = 32
FFN_HIDDEN = 2816
ADAM_LR, ADAM_B1, ADAM_B2, ADAM_EPS, ADAM_WD, ADAM_STEP = 0.001, 0.9, 0.999, 1e-08, 0.01, 10

N_DEV = 8
LANES = 128
VMEM_LIMIT = 56 * 1024 * 1024
HPAD = 128
HIGHEST = lax.Precision.HIGHEST

IN_ROWS = (("w_z", 2048), ("w_xbc", 3072), ("w_dt", 32), ("w_q0", 1024), ("w_q1", 1024), ("w_q2", 1024),
           ("w_k", 1024), ("w_v", 1024), ("w_gs", 1024), ("w_ga", 1024))


def _cparams(sem):
    return pltpu.CompilerParams(dimension_semantics=sem, vmem_limit_bytes=VMEM_LIMIT)


def _sigmoid(x):
    return 0.5 * jnp.tanh(0.5 * x) + 0.5


def _silu(x):
    return x * _sigmoid(x)


def _softplus(x):
    return jnp.maximum(x, 0.0) + jnp.log(1.0 + jnp.exp(-jnp.abs(x)))


def _dot(a, b, dims=(((1,), (0,)), ((), ())), precision=None):
    return lax.dot_general(a, b, dims, precision=precision, preferred_element_type=F32)


NT = (((1,), (1,)), ((), ()))


def _bdot(a, b, dims=(((1,), (0,)), ((), ()))):
    return _dot(a.astype(BF16), b.astype(BF16), dims)


def _pick(dim, cands):
    for c in cands:
        if dim % c == 0:
            return c
    return dim


WHOLE_K_BUDGET = 40 * 1024 * 1024
RESIDENT_B_BYTES = 12 * 1024 * 1024
OUT_TILE_BYTES = 6 * 1024 * 1024


def matmul(a, b, *, name, ta=False, tb=False, out_dtype=F32, add=None):
    m, k = (a.shape[1], a.shape[0]) if ta else a.shape
    n = b.shape[0] if tb else b.shape[1]
    out_bytes = jnp.dtype(out_dtype).itemsize + (4 if add is not None else 0)
    if k * n * b.dtype.itemsize <= RESIDENT_B_BYTES:
        tn = n
        tm = next(t for t in (512, 256, 128) if m % t == 0 and t * n * out_bytes <= OUT_TILE_BYTES)
    else:
        tn = _pick(n, (1024, 1408, 512, 256, 128))
        tm = _pick(m, (512, 1408, 256, 128)) if tn == n else _pick(m, (1024, 1408, 512, 256, 128))
    tk = _pick(k, (2048, 1024, 1408, 512, 256, 128))
    whole_k_bytes = 2 * (tm * k * a.dtype.itemsize + k * tn * b.dtype.itemsize)
    if tn == n and whole_k_bytes <= WHOLE_K_BUDGET:
        tk = k
    nk = k // tk
    a_spec = pl.BlockSpec((tk, tm), lambda i, j, kk: (kk, i)) if ta else pl.BlockSpec((tm, tk), lambda i, j, kk: (i, kk))
    b_spec = pl.BlockSpec((tn, tk), lambda i, j, kk: (j, kk)) if tb else pl.BlockSpec((tk, tn), lambda i, j, kk: (kk, j))
    dims = (((0 if ta else 1,), (1 if tb else 0,)), ((), ()))
    has_add = add is not None

    def body(*refs):
        a_ref, b_ref = refs[:2]
        add_ref = refs[2] if has_add else None
        o_ref = refs[3] if has_add else refs[2]
        acc = refs[-1] if nk > 1 else None
        kk = pl.program_id(2)

        def product():
            return _dot(a_ref[...].astype(BF16), b_ref[...].astype(BF16), dims)

        def finish(r):
            if has_add:
                r = r + add_ref[...].astype(F32)
            o_ref[...] = r.astype(o_ref.dtype)

        if nk == 1:
            finish(product())
            return

        @pl.when(kk == 0)
        def _():
            acc[...] = product()

        @pl.when((kk > 0) & (kk < nk - 1))
        def _():
            acc[...] += product()

        @pl.when(kk == nk - 1)
        def _():
            finish(acc[...] + product())

    in_specs = [a_spec, b_spec]
    args = [a, b]
    if has_add:
        in_specs.append(pl.BlockSpec((tm, tn), lambda i, j, kk: (i, j)))
        args.append(add)
    return pl.pallas_call(
        body, name=name, grid=(m // tm, n // tn, nk),
        in_specs=in_specs, out_specs=pl.BlockSpec((tm, tn), lambda i, j, kk: (i, j)),
        out_shape=SDS((m, n), out_dtype), scratch_shapes=[pltpu.VMEM((tm, tn), F32)] if nk > 1 else [],
        compiler_params=_cparams(("parallel", "parallel", "arbitrary")),
    )(*args)


def matmul_rows(a, b, post, extras, outs, *, name, tb=False, tm=256):
    a_list, b_list = (list(a), list(b)) if isinstance(a, (list, tuple)) else ([a], [b])
    m = a_list[0].shape[0]
    dims = NT if tb else (((1,), (0,)), ((), ()))
    npr, ne = len(a_list), len(extras)

    def body(*refs):
        a_refs, b_refs = refs[:npr], refs[npr:2 * npr]
        e_refs, o_refs = refs[2 * npr:2 * npr + ne], refs[2 * npr + ne:]
        prods = [_dot(ar[...].astype(BF16), br[...].astype(BF16), dims) for ar, br in zip(a_refs, b_refs)]
        res = post(*prods, *[r[...] for r in e_refs])
        for r, val in zip(o_refs, res):
            r[...] = val.astype(r.dtype)

    row = lambda width: pl.BlockSpec((tm, width), lambda i: (i, 0))
    whole = lambda t: pl.BlockSpec(t.shape, lambda i: (0, 0))
    return pl.pallas_call(
        body, name=name, grid=(m // tm,),
        in_specs=[row(t.shape[1]) for t in a_list] + [whole(t) for t in b_list] + [row(e.shape[1]) for e in extras],
        out_specs=[row(c) for c, _ in outs], out_shape=[SDS((m, c), dt) for c, dt in outs],
        compiler_params=_cparams(("parallel",)),
    )(*a_list, *b_list, *extras)


def rowcall(name, fn, rows, params, row_outs, red_outs=(), tr=256):
    s = rows[0].shape[0]
    n_in = len(rows) + len(params)
    n_row = len(row_outs)

    def body(*refs):
        outs = fn(*[r[...].astype(F32) for r in refs[:n_in]])
        if not isinstance(outs, (tuple, list)):
            outs = (outs,)
        orefs = refs[n_in:]
        for r, o in zip(orefs[:n_row], outs[:n_row]):
            r[...] = o.astype(r.dtype)
        if red_outs:
            @pl.when(pl.program_id(0) == 0)
            def _():
                for r in orefs[n_row:]:
                    r[...] = jnp.zeros_like(r)
            for r, o in zip(orefs[n_row:], outs[n_row:]):
                r[...] += o.astype(F32)

    widths = [a[1] if isinstance(a, tuple) else a.shape[1] for a in rows]
    rows = [a[0] if isinstance(a, tuple) else a for a in rows]
    in_specs = [pl.BlockSpec((tr, wd), lambda i: (i, 0)) for wd in widths]
    in_specs += [pl.BlockSpec(p.shape, lambda i: (0, 0)) for p in params]
    out_specs = [pl.BlockSpec((tr, c), lambda i: (i, 0)) for c, _ in row_outs]
    out_specs += [pl.BlockSpec(shp, lambda i: (0, 0)) for shp in red_outs]
    out_shape = [SDS((s, c), dt) for c, dt in row_outs] + [SDS(shp, F32) for shp in red_outs]
    res = pl.pallas_call(
        body, name=name, grid=(s // tr,), in_specs=in_specs, out_specs=out_specs, out_shape=out_shape,
        compiler_params=_cparams(("arbitrary",) if red_outs else ("parallel",)),
    )(*rows, *params)
    return res


def _rms(x, w):
    return x * lax.rsqrt(jnp.mean(x * x, axis=-1, keepdims=True) + RMS_EPS) * w


def rms_fwd(h, w, name):
    return rowcall(name, _rms, [h], [w], [(D_MODEL, BF16)])[0]


def rms_bwd(h, du, dres, w, name):
    def fn(hb, dub, dresb, wb):
        _, vjp = jax.vjp(_rms, hb, wb)
        dh, dw = vjp(dub)
        return dh + dresb, dw
    return rowcall(name, fn, [h, du, dres], [w], [(D_MODEL, F32)], [(1, D_MODEL)])


def loss_head(h, target, w, name):
    def fn(hb, tb, wb):
        def f(hh, ww):
            err = _rms(hh, ww) - tb
            return 0.5 * jnp.sum(jnp.mean(err * err, axis=-1, keepdims=True), axis=0, keepdims=True)
        val, vjp = jax.vjp(f, hb, wb)
        dh, dw = vjp(jnp.ones((1, 1), F32))
        return dh, dw, jnp.broadcast_to(val, (1, LANES))
    return rowcall(name, fn, [h, target], [w], [(D_MODEL, F32)], [(1, D_MODEL), (1, LANES)])


def _gate(a, b, gs, ga):
    return _sigmoid(gs) * a + _sigmoid(ga) * b


def gate_fwd(yn, w_ssd, y_attn, w_attn, gs, ga, name):
    def post(pa, pb, gsb, gab):
        a, b = pa.astype(BF16), pb.astype(BF16)
        return a, b, _gate(a.astype(F32), b.astype(F32), gsb.astype(F32), gab.astype(F32))
    return matmul_rows([yn, y_attn], [w_ssd, w_attn], post, [gs, ga], [(D_MODEL, BF16)] * 3, name=name)


def gate_bwd(dh1, w_out, a, b, gs, ga, name):
    def post(dm, ab, bb, gsb, gab):
        _, vjp = jax.vjp(_gate, ab.astype(F32), bb.astype(F32), gsb.astype(F32), gab.astype(F32))
        return vjp(dm)
    return matmul_rows(dh1, w_out, post, [a, b, gs, ga], [(D_MODEL, BF16)] * 4, name=name, tb=True)


def _swiglu(gu):
    return _silu(gu[:, :FFN_HIDDEN]) * gu[:, FFN_HIDDEN:]


def gate_up_fwd(u2, w_gate_up_t, name):
    def post(acc):
        gu = acc.astype(BF16)
        return gu, _swiglu(gu.astype(F32))
    return matmul_rows(u2, w_gate_up_t, post, [], [(2 * FFN_HIDDEN, BF16), (FFN_HIDDEN, BF16)], name=name, tb=True)


def gate_up_bwd(dh, w_down, gu, name):
    def post(acc, gub):
        _, vjp = jax.vjp(_swiglu, gub.astype(F32))
        return vjp(acc.astype(BF16).astype(F32))
    return matmul_rows(dh, w_down, post, [gu], [(2 * FFN_HIDDEN, BF16)], name=name, tb=True)[0]


def _ssd_post(y, xs, z, dskip, normw):
    y = (y + dskip * xs) * _silu(z)
    gw = SSD_INNER // SSD_GROUPS
    parts = []
    for g in range(SSD_GROUPS):
        yg = y[:, g * gw:(g + 1) * gw]
        parts.append(yg * lax.rsqrt(jnp.mean(yg * yg, axis=-1, keepdims=True) + RMS_EPS))
    return jnp.concatenate(parts, axis=-1) * normw


def ssd_post_fwd(y, xc, z, dskip, normw, name):
    return rowcall(name, _ssd_post, [y, (xc, SSD_INNER), z], [dskip, normw], [(SSD_INNER, BF16)])[0]


def ssd_post_bwd(y, xc, z, dskip, normw, dyn, name):
    def fn(yb, xsb, zb, dynb, db, nb):
        _, vjp = jax.vjp(_ssd_post, yb, xsb, zb, db, nb)
        return vjp(dynb)
    return rowcall(name, fn, [y, (xc, SSD_INNER), z, dyn], [dskip, normw],
                   [(SSD_INNER, BF16)] * 3, [(1, SSD_INNER), (1, SSD_INNER)])


def _rope(t, cosf, sina, sinb):
    return t * cosf + pltpu.roll(t, LANES - ROPE_DIM // 2, 1) * sina + pltpu.roll(t, ROPE_DIM // 2, 1) * sinb


def rope_tables():
    half = ROPE_DIM // 2
    inv = ROPE_THETA ** (-jnp.arange(0, ROPE_DIM, 2, dtype=F32) / ROPE_DIM)
    ang = jnp.arange(SEQ, dtype=F32)[:, None] * inv[None, :]
    cos, sin = jnp.cos(ang), jnp.sin(ang)
    zeros = jnp.zeros((SEQ, LANES - ROPE_DIM), F32)
    z16 = jnp.zeros((SEQ, half), F32)
    cosf = jnp.concatenate([cos, cos, jnp.ones((SEQ, LANES - ROPE_DIM), F32)], axis=1)
    sina = jnp.concatenate([-sin, z16, zeros], axis=1)
    sinb = jnp.concatenate([z16, sin, zeros], axis=1)
    return cosf, sina, sinb


CONV_TC = 256


def _conv_pre(x, w, b, row):
    acc = x * w[SSD_CONV - 1:SSD_CONV, :] + b
    shifted = [x]
    for j in range(1, SSD_CONV):
        xs = jnp.where(row >= j, pltpu.roll(x, j, 0), 0.0)
        shifted.append(xs)
        acc = acc + xs * w[SSD_CONV - 1 - j:SSD_CONV - j, :]
    return acc, shifted


def conv_fwd(xbc, w, b, name):
    def body(x_ref, w_ref, b_ref, o_ref):
        row = lax.broadcasted_iota(jnp.int32, (SEQ, CONV_TC), 0)
        pre, _ = _conv_pre(x_ref[...].astype(F32), w_ref[...], b_ref[...], row)
        o_ref[...] = _silu(pre).astype(o_ref.dtype)
    return pl.pallas_call(
        body, name=name, grid=(SSD_CONV_CH // CONV_TC,),
        in_specs=[pl.BlockSpec((SEQ, CONV_TC), lambda i: (0, i)), pl.BlockSpec((SSD_CONV, CONV_TC), lambda i: (0, i)),
                  pl.BlockSpec((1, CONV_TC), lambda i: (0, i))],
        out_specs=pl.BlockSpec((SEQ, CONV_TC), lambda i: (0, i)),
        out_shape=SDS((SEQ, SSD_CONV_CH), BF16), compiler_params=_cparams(("parallel",)),
    )(xbc, w, b)


def conv_bwd(xbc, w, b, dxc, name):
    def body(x_ref, w_ref, b_ref, dy_ref, dx_ref, dw_ref, db_ref):
        row = lax.broadcasted_iota(jnp.int32, (SEQ, CONV_TC), 0)
        wv = w_ref[...]
        pre, shifted = _conv_pre(x_ref[...].astype(F32), wv, b_ref[...], row)
        sg = _sigmoid(pre)
        ds = dy_ref[...].astype(F32) * (sg * (1.0 + pre * (1.0 - sg)))
        dx = ds * wv[SSD_CONV - 1:SSD_CONV, :]
        for j in range(1, SSD_CONV):
            dsj = jnp.where(row < SEQ - j, pltpu.roll(ds, SEQ - j, 0), 0.0)
            dx = dx + dsj * wv[SSD_CONV - 1 - j:SSD_CONV - j, :]
        dx_ref[...] = dx.astype(dx_ref.dtype)
        for j in range(SSD_CONV):
            dw_ref[SSD_CONV - 1 - j:SSD_CONV - j, :] = jnp.sum(ds * shifted[j], axis=0, keepdims=True)
        db_ref[...] = jnp.sum(ds, axis=0, keepdims=True)
    return pl.pallas_call(
        body, name=name, grid=(SSD_CONV_CH // CONV_TC,),
        in_specs=[pl.BlockSpec((SEQ, CONV_TC), lambda i: (0, i)), pl.BlockSpec((SSD_CONV, CONV_TC), lambda i: (0, i)),
                  pl.BlockSpec((1, CONV_TC), lambda i: (0, i)), pl.BlockSpec((SEQ, CONV_TC), lambda i: (0, i))],
        out_specs=[pl.BlockSpec((SEQ, CONV_TC), lambda i: (0, i)), pl.BlockSpec((SSD_CONV, CONV_TC), lambda i: (0, i)),
                   pl.BlockSpec((1, CONV_TC), lambda i: (0, i))],
        out_shape=[SDS((SEQ, SSD_CONV_CH), BF16), SDS((SSD_CONV, SSD_CONV_CH), F32), SDS((1, SSD_CONV_CH), F32)],
        compiler_params=_cparams(("parallel",)),
    )(xbc, w, b, dxc)


N_CHUNKS = SEQ // SSD_CHUNK
N_PAIRS = SSD_HEADS // 2
PAIRS_PER_GROUP = N_PAIRS // SSD_GROUPS
B_OFF = SSD_INNER
C_OFF = SSD_INNER + SSD_GROUPS * SSD_STATE


def _ssd_prefix(dtr, dtr_t, dtb, dtb_t, alog, alog_t):
    ln = SSD_CHUNK
    dt = _softplus(dtr + dtb)
    dt_t = _softplus(dtr_t + dtb_t)
    dta = dt * (-jnp.exp(alog))
    dta_t = dt_t * (-jnp.exp(alog_t))
    r = lax.broadcasted_iota(jnp.int32, (ln, ln), 0)
    c = lax.broadcasted_iota(jnp.int32, (ln, ln), 1)
    a_cum = _dot((r >= c).astype(F32), dta, precision=HIGHEST)
    a_cum_t = _dot(dta_t, (r <= c).astype(F32), precision=HIGHEST)
    a_last = jnp.sum(dta_t, axis=1, keepdims=True)
    return dt, a_cum, a_cum_t, a_last


def _bein(spec, a, b):
    return jnp.einsum(spec, a.astype(BF16), b.astype(BF16), preferred_element_type=F32)


SSD_GROUPS_PER_BATCH = 4


def _ssd_group(xs3, bgs, cgs, h3, dt, a_cum, a_cum_t, a_last, *, groups):
    ln = SSD_CHUNK
    lane = lax.broadcasted_iota(jnp.int32, (ln, LANES), 1)
    sub = lax.broadcasted_iota(jnp.int32, (LANES, SSD_STATE), 0)
    row = lax.broadcasted_iota(jnp.int32, (ln, ln), 0)
    col = lax.broadcasted_iota(jnp.int32, (ln, ln), 1)
    lo = lane < SSD_HEAD_DIM
    causal = row >= col
    m_lo, m_hi, dts, acs, lasts, cds, cg3, bg3 = [], [], [], [], [], [], [], []
    for g, bg, cg in zip(groups, bgs, cgs):
        cb = _bdot(cg, bg, NT)
        for j in range(PAIRS_PER_GROUP):
            e0 = 2 * (g * PAIRS_PER_GROUP + j)
            e1 = e0 + 1
            c0, c1 = a_cum[:, e0:e0 + 1], a_cum[:, e1:e1 + 1]
            r0, r1 = a_cum_t[e0:e0 + 1, :], a_cum_t[e1:e1 + 1, :]
            l0, l1 = a_last[e0:e0 + 1, :], a_last[e1:e1 + 1, :]
            m_lo.append(cb * jnp.exp(jnp.where(causal, c0 - r0, -jnp.inf)))
            m_hi.append(cb * jnp.exp(jnp.where(causal, c1 - r1, -jnp.inf)))
            dts.append(jnp.where(lo, dt[:, e0:e0 + 1], dt[:, e1:e1 + 1]))
            acs.append(jnp.where(lo, c0, c1))
            lasts.append(jnp.where(lo, l0, l1))
            cds.append(jnp.exp(jnp.where(sub < SSD_HEAD_DIM, l0, l1)))
            cg3.append(cg)
            bg3.append(bg)
    xd = xs3 * jnp.stack(dts)
    acum = jnp.stack(acs)
    y = (_bein("pls,psq->plq", jnp.stack(m_lo), jnp.where(lo[None], xd, 0.0))
         + _bein("pls,psq->plq", jnp.stack(m_hi), jnp.where(lo[None], 0.0, xd)))
    y = y + _bein("pln,pqn->plq", jnp.stack(cg3), h3) * jnp.exp(acum)
    st = _bein("plq,pln->pqn", xd * jnp.exp(jnp.stack(lasts) - acum), jnp.stack(bg3))
    h_out = h3 * jnp.stack(cds) + st
    return y, h_out


def _group_slabs(groups):
    pairs = [g * PAIRS_PER_GROUP + j for g in groups for j in range(PAIRS_PER_GROUP)]
    return [slice(p * LANES, (p + 1) * LANES) for p in pairs]


def _group_batches():
    return [tuple(range(g, g + SSD_GROUPS_PER_BATCH)) for g in range(0, SSD_GROUPS, SSD_GROUPS_PER_BATCH)]


def _bc_of(xc_ref, g):
    return (xc_ref[:, B_OFF + g * SSD_STATE:B_OFF + (g + 1) * SSD_STATE].astype(F32),
            xc_ref[:, C_OFF + g * SSD_STATE:C_OFF + (g + 1) * SSD_STATE].astype(F32))


def _ssd_in_specs(chunk_of):
    return [
        pl.BlockSpec((SSD_CHUNK, SSD_CONV_CH), lambda i: (chunk_of(i), 0)),
        pl.BlockSpec((SSD_CHUNK, HPAD), lambda i: (chunk_of(i), 0)),
        pl.BlockSpec((HPAD, SSD_CHUNK), lambda i: (0, chunk_of(i))),
        pl.BlockSpec((1, HPAD), lambda i: (0, 0)), pl.BlockSpec((HPAD, 1), lambda i: (0, 0)),
        pl.BlockSpec((1, HPAD), lambda i: (0, 0)), pl.BlockSpec((HPAD, 1), lambda i: (0, 0)),
    ]


def ssd_fwd(xc, dtr, dtr_t, dtb, dtb_t, alog, alog_t, name):
    def body(xc_ref, dtr_ref, dtrt_ref, dtb_ref, dtbt_ref, al_ref, alt_ref, y_ref, hs_ref, h_scr):
        @pl.when(pl.program_id(0) == 0)
        def _():
            h_scr[...] = jnp.zeros_like(h_scr)

        hs_ref[0] = h_scr[...].astype(hs_ref.dtype)
        dt, a_cum, a_cum_t, a_last = _ssd_prefix(dtr_ref[...], dtrt_ref[...], dtb_ref[...], dtbt_ref[...],
                                                  al_ref[...], alt_ref[...])
        for groups in _group_batches():
            slabs = _group_slabs(groups)
            bgs, cgs = zip(*[_bc_of(xc_ref, g) for g in groups])
            xs3 = jnp.stack([xc_ref[:, sl] for sl in slabs]).astype(F32)
            h3 = jnp.stack([h_scr[sl, :] for sl in slabs])
            y3, h3_out = _ssd_group(xs3, bgs, cgs, h3, dt, a_cum, a_cum_t, a_last, groups=groups)
            for j, sl in enumerate(slabs):
                y_ref[:, sl] = y3[j].astype(y_ref.dtype)
                h_scr[sl, :] = h3_out[j]

    return pl.pallas_call(
        body, name=name, grid=(N_CHUNKS,), in_specs=_ssd_in_specs(lambda i: i),
        out_specs=[pl.BlockSpec((SSD_CHUNK, SSD_INNER), lambda i: (i, 0)),
                   pl.BlockSpec((1, SSD_INNER, SSD_STATE), lambda i: (i, 0, 0))],
        out_shape=[SDS((SEQ, SSD_INNER), BF16), SDS((N_CHUNKS, SSD_INNER, SSD_STATE), BF16)],
        scratch_shapes=[pltpu.VMEM((SSD_INNER, SSD_STATE), F32)],
        compiler_params=_cparams(("arbitrary",)),
    )(xc, dtr, dtr_t, dtb, dtb_t, alog, alog_t)


def ssd_bwd(xc, dtr, dtr_t, dtb, dtb_t, alog, alog_t, hs, dy, dxs_extra, name):
    rev = lambda i: N_CHUNKS - 1 - i

    def body(xc_ref, dtr_ref, dtrt_ref, dtb_ref, dtbt_ref, al_ref, alt_ref, hs_ref, dy_ref, dxe_ref,
             dxc_ref, ddtr_ref, ddtrt_ref, ddtb_ref, ddtbt_ref, dal_ref, dalt_ref, dh_scr):
        @pl.when(pl.program_id(0) == 0)
        def _():
            dh_scr[...] = jnp.zeros_like(dh_scr)
            for r in (ddtb_ref, ddtbt_ref, dal_ref, dalt_ref):
                r[...] = jnp.zeros_like(r)

        prefix_in = (dtr_ref[...], dtrt_ref[...], dtb_ref[...], dtbt_ref[...], al_ref[...], alt_ref[...])
        (dt, a_cum, a_cum_t, a_last), prefix_vjp = jax.vjp(_ssd_prefix, *prefix_in)
        d_dt = jnp.zeros_like(dt)
        d_acum = jnp.zeros_like(a_cum)
        d_acum_t = jnp.zeros_like(a_cum_t)
        d_alast = jnp.zeros_like(a_last)
        for groups in _group_batches():
            slabs = _group_slabs(groups)
            bgs, cgs = zip(*[_bc_of(xc_ref, g) for g in groups])
            xs3 = jnp.stack([xc_ref[:, sl] for sl in slabs]).astype(F32)
            h3 = jnp.stack([hs_ref[0, sl, :] for sl in slabs]).astype(F32)
            _, vjp = jax.vjp(functools.partial(_ssd_group, groups=groups), xs3, bgs, cgs, h3, dt, a_cum, a_cum_t, a_last)
            dy3 = jnp.stack([dy_ref[:, sl] for sl in slabs]).astype(F32)
            dh3 = jnp.stack([dh_scr[sl, :] for sl in slabs])
            dxs3, d_bgs, d_cgs, dh3_in, ddt, dac, dact, dal = vjp((dy3, dh3))
            for j, sl in enumerate(slabs):
                dxc_ref[:, sl] = (dxs3[j] + dxe_ref[:, sl].astype(F32)).astype(dxc_ref.dtype)
                dh_scr[sl, :] = dh3_in[j]
            d_dt, d_acum, d_acum_t, d_alast = d_dt + ddt, d_acum + dac, d_acum_t + dact, d_alast + dal
            for g, d_bg, d_cg in zip(groups, d_bgs, d_cgs):
                dxc_ref[:, B_OFF + g * SSD_STATE:B_OFF + (g + 1) * SSD_STATE] = d_bg.astype(dxc_ref.dtype)
                dxc_ref[:, C_OFF + g * SSD_STATE:C_OFF + (g + 1) * SSD_STATE] = d_cg.astype(dxc_ref.dtype)
        g_dtr, g_dtrt, g_dtb, g_dtbt, g_al, g_alt = prefix_vjp((d_dt, d_acum, d_acum_t, d_alast))
        ddtr_ref[...] = g_dtr
        ddtrt_ref[...] = g_dtrt
        ddtb_ref[...] += g_dtb
        ddtbt_ref[...] += g_dtbt
        dal_ref[...] += g_al
        dalt_ref[...] += g_alt

    in_specs = _ssd_in_specs(rev) + [
        pl.BlockSpec((1, SSD_INNER, SSD_STATE), lambda i: (rev(i), 0, 0)),
        pl.BlockSpec((SSD_CHUNK, SSD_INNER), lambda i: (rev(i), 0)),
        pl.BlockSpec((SSD_CHUNK, SSD_INNER), lambda i: (rev(i), 0)),
    ]
    out_specs = [
        pl.BlockSpec((SSD_CHUNK, SSD_CONV_CH), lambda i: (rev(i), 0)),
        pl.BlockSpec((SSD_CHUNK, HPAD), lambda i: (rev(i), 0)),
        pl.BlockSpec((HPAD, SSD_CHUNK), lambda i: (0, rev(i))),
        pl.BlockSpec((1, HPAD), lambda i: (0, 0)), pl.BlockSpec((HPAD, 1), lambda i: (0, 0)),
        pl.BlockSpec((1, HPAD), lambda i: (0, 0)), pl.BlockSpec((HPAD, 1), lambda i: (0, 0)),
    ]
    out_shape = [SDS((SEQ, SSD_CONV_CH), BF16), SDS((SEQ, HPAD), F32), SDS((HPAD, SEQ), F32),
                 SDS((1, HPAD), F32), SDS((HPAD, 1), F32), SDS((1, HPAD), F32), SDS((HPAD, 1), F32)]
    return pl.pallas_call(
        body, name=name, grid=(N_CHUNKS,), in_specs=in_specs, out_specs=out_specs, out_shape=out_shape,
        scratch_shapes=[pltpu.VMEM((SSD_INNER, SSD_STATE), F32)],
        compiler_params=_cparams(("arbitrary",)),
    )(xc, dtr, dtr_t, dtb, dtb_t, alog, alog_t, hs, dy, dxs_extra)


ATTN_SCALE = ATTN_HEAD_DIM ** -0.5


UNITS_PER_PATTERN = SEQ // ATTN_BLOCK
ATTN_BATCH_FWD = 8
ATTN_BATCH_BWD = 16


def _for_unit_batches(batch, per_trip):
    for g, d in enumerate(ATTN_DILATIONS):
        nb = UNITS_PER_PATTERN // d
        span = d * ATTN_BLOCK

        def trip(t, carry, g=g, d=d, nb=nb, span=span):
            units = []
            for j in range(per_trip):
                i = t * per_trip + j
                r = i >> (nb.bit_length() - 1)
                n = i & (nb - 1)
                start = r + n * span
                prev = jnp.where(n > 0, start - span, start)
                units.append((pl.ds(start, ATTN_BLOCK, stride=d), pl.ds(prev, ATTN_BLOCK, stride=d), n > 0))
            batch(g, units)
            return carry
        lax.fori_loop(0, UNITS_PER_PATTERN // per_trip, trip, 0)


def _unit_operands(units, q_scr, k_scr, v_scr):
    def pair(scr, rows, prows):
        return jnp.concatenate([scr[prows, :], scr[rows, :]], axis=0)
    qb = jnp.stack([q_scr[rows, :] for rows, _, _ in units]).astype(BF16)
    kb = jnp.stack([pair(k_scr, rows, prows) for rows, prows, _ in units]).astype(BF16)
    vb = jnp.stack([pair(v_scr, rows, prows) for rows, prows, _ in units]).astype(BF16)
    return qb, kb, vb


def _unit_scores(qb, kb, units):
    s = jnp.einsum("bqd,bkd->bqk", qb, kb, preferred_element_type=F32) * ATTN_SCALE
    qi = lax.broadcasted_iota(jnp.int32, (ATTN_BLOCK, 2 * ATTN_BLOCK), 0)
    kj = lax.broadcasted_iota(jnp.int32, (ATTN_BLOCK, 2 * ATTN_BLOCK), 1)
    own = (kj >= ATTN_BLOCK) & (kj - ATTN_BLOCK <= qi)
    before = (kj < ATTN_BLOCK) & (kj >= qi)
    keep = jnp.stack([own | (before & has_prev) for _, _, has_prev in units])
    return jnp.where(keep, s, -jnp.inf)


def _head_specs(n_q_groups):
    blk = (SEQ, ATTN_HEAD_DIM)
    q_specs = [pl.BlockSpec(blk, functools.partial(lambda h, g: (0, g * ATTN_KV_HEADS + h), g=g)) for g in range(n_q_groups)]
    head = pl.BlockSpec(blk, lambda h: (0, h))
    table = pl.BlockSpec(blk, lambda h: (0, 0))
    return q_specs, head, table


def attn_fwd(q, k, v, tabs, name):
    q_specs, head, table = _head_specs(ATTN_N_PAT)

    def body(q0_ref, q1_ref, q2_ref, k_ref, v_ref, c_ref, sa_ref, sb_ref, y_ref, lse_ref, *scr):
        qs, og, ls, ks, vs = scr[0:3], scr[3:6], scr[6:9], scr[9], scr[10]
        c, sa, sb = c_ref[...], sa_ref[...], sb_ref[...]
        for g, q_ref in enumerate((q0_ref, q1_ref, q2_ref)):
            qs[g][...] = _rope(q_ref[...].astype(F32), c, sa, sb)
        ks[...] = _rope(k_ref[...].astype(F32), c, sa, sb)
        vs[...] = v_ref[...].astype(F32)

        def batch(g, units):
            qb, kb, vb = _unit_operands(units, qs[g], ks, vs)
            s = _unit_scores(qb, kb, units)
            m = jnp.max(s, axis=2, keepdims=True)
            p = jnp.exp(s - m)
            l = jnp.sum(p, axis=2, keepdims=True)
            o = jnp.einsum("bqk,bkd->bqd", p.astype(BF16), vb, preferred_element_type=F32) / l
            lse_b = m + jnp.log(l)
            for j, (rows, _, _) in enumerate(units):
                og[g][rows, :] = o[j]
                ls[g][rows, :] = jnp.broadcast_to(lse_b[j], (ATTN_BLOCK, LANES))

        _for_unit_batches(batch, ATTN_BATCH_FWD)
        l0, l1, l2 = ls[0][...], ls[1][...], ls[2][...]
        m = jnp.maximum(jnp.maximum(l0, l1), l2)
        e0, e1, e2 = jnp.exp(l0 - m), jnp.exp(l1 - m), jnp.exp(l2 - m)
        den = e0 + e1 + e2
        y_ref[...] = ((e0 * og[0][...] + e1 * og[1][...] + e2 * og[2][...]) / den).astype(y_ref.dtype)
        lse_ref[...] = m + jnp.log(den)

    blk = (SEQ, ATTN_HEAD_DIM)
    return pl.pallas_call(
        body, name=name, grid=(ATTN_KV_HEADS,), in_specs=[*q_specs, head, head, table, table, table],
        out_specs=[head, head], out_shape=[SDS((SEQ, ATTN_OUT), BF16), SDS((SEQ, ATTN_OUT), F32)],
        scratch_shapes=[pltpu.VMEM(blk, F32)] * (3 * ATTN_N_PAT + 2),
        compiler_params=_cparams(("parallel",)),
    )(q, q, q, k, v, *tabs)


def attn_bwd(q, k, v, tabs, y, lse, dy, name):
    q_specs, head, table = _head_specs(ATTN_N_PAT)

    def body(q0_ref, q1_ref, q2_ref, k_ref, v_ref, c_ref, sa_ref, sb_ref, y_ref, lse_ref, dy_ref,
             dq0_ref, dq1_ref, dq2_ref, dk_ref, dv_ref, *scr):
        qs, dqs, ks, dks, dd, dvs, vs = scr[0:3], scr[3:6], scr[6], scr[7], scr[8], scr[9], scr[10]
        c, sa, sb = c_ref[...], sa_ref[...], sb_ref[...]
        for g, q_ref in enumerate((q0_ref, q1_ref, q2_ref)):
            qs[g][...] = _rope(q_ref[...].astype(F32), c, sa, sb)
        ks[...] = _rope(k_ref[...].astype(F32), c, sa, sb)
        vs[...] = v_ref[...].astype(F32)
        dks[...] = jnp.zeros_like(dks)
        dvs[...] = jnp.zeros_like(dvs)
        dyv = dy_ref[...]
        dd[...] = jnp.broadcast_to(jnp.sum(dyv * y_ref[...].astype(F32), axis=1, keepdims=True), dd.shape)

        def batch(g, units):
            qb, kb, vb = _unit_operands(units, qs[g], ks, vs)
            dob = jnp.stack([dy_ref[rows, :] for rows, _, _ in units]).astype(BF16)
            lse_b = jnp.stack([lse_ref[rows, :][:, 0:1] for rows, _, _ in units])
            dsum_b = jnp.stack([dd[rows, :][:, 0:1] for rows, _, _ in units])
            p = jnp.exp(_unit_scores(qb, kb, units) - lse_b)
            dp = jnp.einsum("bqd,bkd->bqk", dob, vb, preferred_element_type=F32)
            ds = (p * (dp - dsum_b) * ATTN_SCALE).astype(BF16)
            dq = jnp.einsum("bqk,bkd->bqd", ds, kb, preferred_element_type=F32)
            dk = jnp.einsum("bqk,bqd->bkd", ds, qb, preferred_element_type=F32)
            dv = jnp.einsum("bqk,bqd->bkd", p.astype(BF16), dob, preferred_element_type=F32)
            for j, (rows, prows, _) in enumerate(units):
                dqs[g][rows, :] = dq[j]
                dks[prows, :] += dk[j, :ATTN_BLOCK]
                dks[rows, :] += dk[j, ATTN_BLOCK:]
                dvs[prows, :] += dv[j, :ATTN_BLOCK]
                dvs[rows, :] += dv[j, ATTN_BLOCK:]

        _for_unit_batches(batch, ATTN_BATCH_BWD)
        for g, dq_ref in enumerate((dq0_ref, dq1_ref, dq2_ref)):
            dq_ref[...] = _rope(dqs[g][...], c, -sa, -sb).astype(dq_ref.dtype)
        dk_ref[...] = _rope(dks[...], c, -sa, -sb).astype(dk_ref.dtype)
        dv_ref[...] = dvs[...].astype(dv_ref.dtype)

    blk = (SEQ, ATTN_HEAD_DIM)
    out = SDS((SEQ, ATTN_OUT), BF16)
    return pl.pallas_call(
        body, name=name, grid=(ATTN_KV_HEADS,), in_specs=[*q_specs, head, head, table, table, table, head, head, head],
        out_specs=[head] * 5, out_shape=[out] * 5,
        scratch_shapes=[pltpu.VMEM(blk, F32)] * (2 * ATTN_N_PAT + 5),
        compiler_params=_cparams(("parallel",)),
    )(q, q, q, k, v, *tabs, y, lse, dy)


def layer_fwd(h, getw, prefetch, small, tabs, li):
    n = f"l{li}_"
    sv = {}
    w = dict(getw(0, h))
    u = rms_fwd(h, small["norm_mix"], n + "rms_mix")
    z = matmul(u, w["w_z"], name=n + "mm_z", tb=True, out_dtype=BF16)
    prefetch(1, z)
    xbc = matmul(u, w["w_xbc"], name=n + "mm_xbc", tb=True, out_dtype=BF16)
    dtr = matmul(u, w["w_dt"], name=n + "mm_dt", tb=True)
    q = matmul(u, w["w_q"], name=n + "mm_q", tb=True, out_dtype=BF16)
    k = matmul(u, w["w_k"], name=n + "mm_k", tb=True, out_dtype=BF16)
    v = matmul(u, w["w_v"], name=n + "mm_v", tb=True, out_dtype=BF16)
    gs = matmul(u, w["w_gs"], name=n + "mm_gs", tb=True, out_dtype=BF16)
    ga = matmul(u, w["w_ga"], name=n + "mm_ga", tb=True, out_dtype=BF16)
    xc = conv_fwd(xbc, w["conv_w"], small["conv_b"], n + "conv")
    dtr_t = dtr.T
    y_ssd, hs = ssd_fwd(xc, dtr, dtr_t, small["dt_bias"], small["dt_bias"].T, small["a_log"], small["a_log"].T, n + "ssd")
    yn = ssd_post_fwd(y_ssd, xc, z, small["d_skip_x"], small["ssd_norm"], n + "ssd_post")
    y_attn, lse = attn_fwd(q, k, v, tabs, n + "attn")
    w.update(getw(1, y_ssd))
    a, b, merged = gate_fwd(yn, w["w_ssd_branch"], y_attn, w["w_attn_branch"], gs, ga, n + "mm_ab_gate")
    h1 = matmul(merged, w["w_out"], name=n + "mm_o", add=h)
    w.update(getw(2, h1))
    u2 = rms_fwd(h1, small["norm_ffn"], n + "rms_ffn")
    gu, act = gate_up_fwd(u2, w["w_gate_up"], n + "mm_gu_swiglu")
    h2 = matmul(act, w["w_down"], name=n + "mm_down", add=h1)
    sv.update(h=h, u=u, z=z, xbc=xbc, dtr=dtr, dtr_t=dtr_t, gs=gs, ga=ga, xc=xc, y_ssd=y_ssd, hs=hs, yn=yn,
              q=q, k=k, v=v, y_attn=y_attn, lse=lse, a=a, b=b, merged=merged, h1=h1, u2=u2, gu=gu, act=act, w=w)
    return h2, sv


def layer_bwd(dh, sv, small, tabs, li, emit):
    n = f"l{li}_b_"
    w = sv["w"]
    gw, gsm = {}, {}
    gw["w_down"] = matmul(sv["act"], dh, name=n + "mm_dwdown", ta=True, out_dtype=BF16)
    dgu = gate_up_bwd(dh, w["w_down"], sv["gu"], n + "mm_dact_swiglu")
    gw["w_gate_up"] = matmul(dgu, sv["u2"], name=n + "mm_dwgu", ta=True, out_dtype=BF16)
    tok = emit(2, gw)
    du2 = matmul(dgu, w["w_gate_up"], name=n + "mm_du2")
    dh1, gsm["norm_ffn"] = rms_bwd(sv["h1"], du2, dh, small["norm_ffn"] + tok, n + "rms_ffn")
    gw["w_out"] = matmul(sv["merged"], dh1, name=n + "mm_dwo", ta=True, out_dtype=BF16)
    da, db, dgs, dga = gate_bwd(dh1, w["w_out"], sv["a"], sv["b"], sv["gs"], sv["ga"], n + "mm_dmerged_gate")
    gw["w_ssd_branch"] = matmul(sv["yn"], da, name=n + "mm_dwa", ta=True, out_dtype=BF16)
    gw["w_attn_branch"] = matmul(sv["y_attn"], db, name=n + "mm_dwb", ta=True, out_dtype=BF16)
    tok = emit(1, gw)
    dyn = matmul(da, w["w_ssd_branch"], name=n + "mm_dyn", tb=True, out_dtype=BF16)
    dyattn = matmul(db, w["w_attn_branch"], name=n + "mm_dyattn", tb=True)
    dy_ssd, dxs_extra, dz, gsm["d_skip_x"], gsm["ssd_norm"] = ssd_post_bwd(
        sv["y_ssd"], sv["xc"], sv["z"], small["d_skip_x"] + tok, small["ssd_norm"], dyn, n + "ssd_post")
    dxc, ddtr, ddtr_t, ddtb, ddtb_t, dal, dal_t = ssd_bwd(
        sv["xc"], sv["dtr"], sv["dtr_t"], small["dt_bias"], small["dt_bias"].T, small["a_log"], small["a_log"].T,
        sv["hs"], dy_ssd, dxs_extra, n + "ssd")
    ddtr = (ddtr + ddtr_t.T).astype(BF16)
    gsm["dt_bias"] = ddtb + ddtb_t.T
    gsm["a_log"] = dal + dal_t.T
    dxbc, gw["conv_w"], gsm["conv_b"] = conv_bwd(sv["xbc"], w["conv_w"], small["conv_b"], dxc, n + "conv")
    dq0, dq1, dq2, dk, dv = attn_bwd(sv["q"], sv["k"], sv["v"], tabs, sv["y_attn"], sv["lse"], dyattn, n + "attn")
    u = sv["u"]
    segs = [("w_z", dz), ("w_xbc", dxbc), ("w_dt", ddtr), ("w_q0", dq0), ("w_q1", dq1), ("w_q2", dq2),
            ("w_k", dk), ("w_v", dv), ("w_gs", dgs), ("w_ga", dga)]
    gin = [matmul(dseg, u, name=n + "mm_d" + key, ta=True, out_dtype=BF16) for key, dseg in segs]
    gin[2] = gin[2][:SSD_HEADS]
    gw["w_in"] = jnp.concatenate(gin, axis=0)
    tok = emit(0, gw)
    du = jnp.zeros((SEQ, D_MODEL), F32) + tok
    for si, (key, dseg) in enumerate(segs):
        du = matmul(dseg, w[key], name=n + "mm_du_" + key, add=du)
        if si == 2:
            du = du + emit(-1, du)
    dh0, gsm["norm_mix"] = rms_bwd(sv["h"], du, dh1, small["norm_mix"] + tok, n + "rms_mix")
    return dh0, gsm


def _my_place():
    return lax.axis_index("x"), lax.axis_index("y"), lax.axis_index("c")


def _flip(place, k):
    x, y, c = place
    return (1 - x if k & 4 else x, 1 - y if k & 2 else y, 1 - c if k & 1 else c)


def _index(place):
    return 4 * place[0] + 2 * place[1] + place[2]


ANY = pl.BlockSpec(memory_space=pl.ANY)
CHIP_FLIPS = (4, 2, 6)
SELF_AND_CHIPS = (0,) + CHIP_FLIPS


def all_gather(xs, name):
    na = len(xs)

    def body(*refs):
        x_refs, o_refs = refs[:na], refs[na:2 * na]
        send_sems, recv_sems, local_sems = refs[2 * na:]
        me = _my_place()
        sibling = _flip(me, 1)
        chips = [_flip(me, f) for f in CHIP_FLIPS]

        def copy(a, kk, block, to, src=None):
            dst = o_refs[a].at[_index(block)]
            return pltpu.make_async_remote_copy(
                src_ref=dst if src is None else src, dst_ref=dst, send_sem=send_sems.at[a, kk],
                recv_sem=recv_sems.at[a, kk], device_id=to, device_id_type=MESH)

        mine = [pltpu.make_async_copy(x_refs[a], o_refs[a].at[_index(me)], local_sems.at[a]) for a in range(na)]
        for cp in mine:
            cp.start()
        first = []
        for j, chip in enumerate(chips):
            first += [copy(a, 1 + j, me, chip, src=x_refs[a]) for a in range(na)]
        first += [copy(a, 0, me, sibling, src=x_refs[a]) for a in range(na)]
        for cp in first:
            cp.start()
        passed = []
        for j, chip in enumerate(chips):
            for a in range(na):
                copy(a, 1 + j, chip, me).wait_recv()
                cp = copy(a, 4 + j, chip, sibling)
                cp.start()
                passed.append(cp)
        for a in range(na):
            copy(a, 0, sibling, me).wait_recv()
        for j, chip in enumerate(chips):
            for a in range(na):
                copy(a, 4 + j, _flip(chip, 1), me).wait_recv()
        for cp in first + passed:
            cp.wait_send()
        for cp in mine:
            cp.wait()

    return pl.pallas_call(
        body, name=name, in_specs=[ANY] * na, out_specs=[ANY] * na,
        out_shape=[SDS((N_DEV,) + t.shape, t.dtype) for t in xs],
        scratch_shapes=[pltpu.SemaphoreType.DMA((na, N_DEV - 1)), pltpu.SemaphoreType.DMA((na, N_DEV - 1)),
                        pltpu.SemaphoreType.DMA((na,))],
    )(*xs)


HBM = pl.BlockSpec(memory_space=pltpu.HBM)
SEM = pl.BlockSpec(memory_space=pltpu.SEMAPHORE)
EFFECT = pltpu.SideEffectType.DATAFLOW_SIDE_EFFECTING


def _split_copy(src_ref, land_ref, send_sem, recv_sem, me, kk, scatter, landed_from_peer):
    peer = _flip(me, kk)
    src = src_ref.at[_index(peer)] if scatter else src_ref
    dst = land_ref.at[_index(peer if landed_from_peer else me)]
    return pltpu.make_async_remote_copy(src_ref=src, dst_ref=dst, send_sem=send_sem, recv_sem=recv_sem,
                                        device_id=peer, device_id_type=MESH)


ALL_PEERS = tuple(range(1, N_DEV))
EVERYONE = (0,) + ALL_PEERS


def exchange_start(srcs, lands, group_sizes, scatter, name, peers=ALL_PEERS):
    na, ng = len(srcs), len(group_sizes)

    def body(*refs):
        s_refs, l_refs = refs[:na], refs[na:2 * na]
        sems = refs[2 * na:2 * na + 2 * ng]
        token = refs[-1]
        me = _my_place()
        a = 0
        for gi, gsz in enumerate(group_sizes):
            for j in range(gsz):
                for pi, kk in enumerate(peers):
                    slot = j * len(peers) + pi
                    _split_copy(s_refs[a], l_refs[a], sems[2 * gi].at[slot], sems[2 * gi + 1].at[slot],
                                me, kk, scatter, False).start()
                a += 1
        token[...] = jnp.zeros_like(token)

    sem_shapes = []
    for gsz in group_sizes:
        sem_shapes += [pltpu.SemaphoreType.DMA((gsz * len(peers),))] * 2
    ins = [pltpu.with_memory_space_constraint(t, pltpu.HBM) for t in (*srcs, *lands)]
    res = pl.pallas_call(
        body, name=name, in_specs=[HBM] * (2 * na),
        out_specs=[SEM] * (2 * ng) + [HBM] * (2 * na) + [pl.BlockSpec(memory_space=pltpu.VMEM)],
        out_shape=sem_shapes + [pltpu.HBM(t.shape, t.dtype) for t in ins] + [SDS((8, LANES), F32)],
        input_output_aliases={i: 2 * ng + i for i in range(2 * na)},
        compiler_params=pltpu.CompilerParams(has_side_effects=EFFECT),
    )(*ins)
    sems = [(res[2 * gi], res[2 * gi + 1]) for gi in range(ng)]
    thru = res[2 * ng:2 * ng + 2 * na]
    return sems, thru[:na], thru[na:], res[-1]


def _wait_split_copies(s_refs, l_refs, send_sems, recv_sems, scatter, peers):
    me = _my_place()
    for j in range(len(s_refs)):
        for pi, kk in enumerate(peers):
            slot = j * len(peers) + pi
            cp = _split_copy(s_refs[j], l_refs[j], send_sems.at[slot], recv_sems.at[slot], me, kk, scatter, True)
            cp.wait_send()
            cp.wait_recv()


def exchange_wait(srcs, lands, sems, after, scatter, name, peers=ALL_PEERS):
    n = len(srcs)

    def body(*refs):
        s_refs, l_refs = refs[:n], refs[n:2 * n]
        _wait_split_copies(s_refs, l_refs, refs[2 * n], refs[2 * n + 1], scatter, peers)

    res = pl.pallas_call(
        body, name=name, in_specs=[HBM] * (2 * n) + [SEM, SEM, ANY], out_specs=[HBM] * (2 * n),
        out_shape=[pltpu.HBM(t.shape, t.dtype) for t in (*srcs, *lands)],
        input_output_aliases={i: i for i in range(2 * n)},
        compiler_params=pltpu.CompilerParams(has_side_effects=EFFECT),
    )(*srcs, *lands, sems[0], sems[1], after)
    return res[n:]


def _sibling_copies(l_refs, send_sems, recv_sems, arriving):
    me = _my_place()
    sibling = _flip(me, 1)
    held = [me] + [_flip(me, f) for f in CHIP_FLIPS]
    copies = []
    for j, land in enumerate(l_refs):
        for bi, place in enumerate(held):
            blk = land.at[_index(_flip(place, 1) if arriving else place)]
            slot = j * len(held) + bi
            copies.append(pltpu.make_async_remote_copy(src_ref=blk, dst_ref=blk, send_sem=send_sems.at[slot],
                                                       recv_sem=recv_sems.at[slot], device_id=sibling, device_id_type=MESH))
    return copies


def gather_forward(srcs, lands, sems, after, name):
    n = len(srcs)

    def body(*refs):
        s_refs, l_refs = refs[:n], refs[n:2 * n]
        _wait_split_copies(s_refs, l_refs, refs[2 * n], refs[2 * n + 1], False, SELF_AND_CHIPS)
        for cp in _sibling_copies(l_refs, refs[2 * n + 3], refs[2 * n + 4], False):
            cp.start()

    n_slots = n * (1 + len(CHIP_FLIPS))
    res = pl.pallas_call(
        body, name=name, in_specs=[HBM] * (2 * n) + [SEM, SEM, ANY],
        out_specs=[SEM, SEM] + [HBM] * (2 * n),
        out_shape=[pltpu.SemaphoreType.DMA((n_slots,))] * 2 + [pltpu.HBM(t.shape, t.dtype) for t in (*srcs, *lands)],
        input_output_aliases={i: 2 + i for i in range(2 * n)},
        compiler_params=pltpu.CompilerParams(has_side_effects=EFFECT),
    )(*srcs, *lands, sems[0], sems[1], after)
    return (res[0], res[1]), res[2 + n:]


def gather_finish(lands, sems, after, name):
    n = len(lands)

    def body(*refs):
        l_refs = refs[:n]
        for cp in _sibling_copies(l_refs, refs[n], refs[n + 1], True):
            cp.wait_send()
            cp.wait_recv()

    return pl.pallas_call(
        body, name=name, in_specs=[HBM] * n + [SEM, SEM, ANY], out_specs=[HBM] * n,
        out_shape=[pltpu.HBM(t.shape, t.dtype) for t in lands],
        input_output_aliases={i: i for i in range(n)},
        compiler_params=pltpu.CompilerParams(has_side_effects=EFFECT),
    )(*lands, sems[0], sems[1], after)


def landing_zone(block):
    return lax.empty((N_DEV,) + block.shape, block.dtype)


def sum_parts(parts, name, row_major_3d=False):
    n_parts, r, c = parts.shape
    tc = _pick(c, (256, 128))

    def body(p_ref, o_ref):
        acc = p_ref[0].astype(F32)
        for i in range(1, n_parts):
            acc = acc + p_ref[i].astype(F32)
        if row_major_3d:
            o_ref[:, 0, :] = acc
        else:
            o_ref[...] = acc

    out_spec = pl.BlockSpec((r, 1, tc), lambda i: (0, 0, i)) if row_major_3d else pl.BlockSpec((r, tc), lambda i: (0, i))
    return pl.pallas_call(
        body, name=name, grid=(c // tc,), in_specs=[pl.BlockSpec((n_parts, r, tc), lambda i: (0, 0, i))],
        out_specs=out_spec, out_shape=SDS((r, 1, c) if row_major_3d else (r, c), F32),
        compiler_params=_cparams(("parallel",)),
    )(parts)


SELF_CHIP_FIRST = (0,) + CHIP_FLIPS
N_CHIPS = len(SELF_CHIP_FIRST)


def _chip_index(place):
    return 2 * place[0] + place[1]


def _split_call(body, name, arrays, n_sems, sems_in=None, after=None):
    n = len(arrays)
    thru = [pltpu.HBM(t.shape, t.dtype) for t in arrays]
    if sems_in is None:
        ins = [pltpu.with_memory_space_constraint(t, pltpu.HBM) for t in arrays]
        res = pl.pallas_call(
            body, name=name, in_specs=[HBM] * n, out_specs=[SEM, SEM] + [HBM] * n + [pl.BlockSpec(memory_space=pltpu.VMEM)],
            out_shape=[pltpu.SemaphoreType.DMA((n_sems,))] * 2 + thru + [SDS((8, LANES), F32)],
            input_output_aliases={i: 2 + i for i in range(n)}, compiler_params=pltpu.CompilerParams(has_side_effects=EFFECT),
        )(*ins)
        return (res[0], res[1]), res[2:2 + n], res[-1]
    return pl.pallas_call(
        body, name=name, in_specs=[HBM] * n + [SEM, SEM, ANY], out_specs=[HBM] * n, out_shape=thru,
        input_output_aliases={i: i for i in range(n)}, compiler_params=pltpu.CompilerParams(has_side_effects=EFFECT),
    )(*arrays, sems_in[0], sems_in[1], after)


def _pair_copies(p_ref, l_ref, ssem, rsem, arriving):
    me = _my_place()
    copies = []
    for j, f in enumerate(SELF_CHIP_FIRST):
        owner = _flip(me, f) if arriving else _flip(_flip(me, f), 1)
        blk = _index(owner)
        copies.append(pltpu.make_async_remote_copy(src_ref=p_ref.at[blk], dst_ref=l_ref.at[blk], send_sem=ssem.at[j],
                                                   recv_sem=rsem.at[j], device_id=_flip(me, 1), device_id_type=MESH))
    return copies


def _chip_copies(p_ref, l_ref, ssem, rsem, arriving):
    me = _my_place()
    copies = []
    for j, f in enumerate(SELF_CHIP_FIRST):
        target = _flip(me, f)
        src = p_ref.at[_chip_index(target)]
        dst = l_ref.at[_chip_index(target if arriving else me)]
        copies.append(pltpu.make_async_remote_copy(src_ref=src, dst_ref=dst, send_sem=ssem.at[j], recv_sem=rsem.at[j],
                                                   device_id=target, device_id_type=MESH))
    return copies


def pair_start(parts, name):
    def body(p_ref, l_ref, ssem, rsem, p_thru, l_thru, token):
        for cp in _pair_copies(p_ref, l_ref, ssem, rsem, False):
            cp.start()
        token[...] = jnp.zeros_like(token)
    return _split_call(body, name, [parts, lax.empty(parts.shape, parts.dtype)], N_CHIPS)


def pair_sum(parts, land, sems, after, name):
    def wait_body(p_ref, l_ref, ssem, rsem, after_ref, p_thru, l_thru):
        for cp in _pair_copies(p_ref, l_ref, ssem, rsem, True):
            cp.wait_send()
            cp.wait_recv()
    parts, land = _split_call(wait_body, name + "_wait", [parts, land], N_CHIPS, sems, after)
    _, r, c = parts.shape
    tc = _pick(c, (256, 128))

    def body(core_ref, p_ref, l_ref, o_ref):
        o_ref[0] = (p_ref[0].astype(F32) + l_ref[0].astype(F32)).astype(o_ref.dtype)

    spec = pl.BlockSpec((1, r, tc), lambda k, j, core_ref: (2 * k + core_ref[0], 0, j))
    return pl.pallas_call(
        body, name=name,
        grid_spec=pltpu.PrefetchScalarGridSpec(
            num_scalar_prefetch=1, grid=(N_CHIPS, c // tc), in_specs=[spec, spec],
            out_specs=pl.BlockSpec((1, r, tc), lambda k, j, core_ref: (k, 0, j))),
        out_shape=SDS((N_CHIPS, r, c), parts.dtype), compiler_params=_cparams(("parallel", "parallel")),
    )(lax.axis_index("c").astype(jnp.int32).reshape(1), parts, land)


def chip_start(pre, name):
    def body(p_ref, l_ref, ssem, rsem, p_thru, l_thru, token):
        for cp in _chip_copies(p_ref, l_ref, ssem, rsem, False):
            cp.start()
        token[...] = jnp.zeros_like(token)
    return _split_call(body, name, [pre, lax.empty(pre.shape, pre.dtype)], N_CHIPS)


def chip_wait(pre, land, sems, after, name):
    def body(p_ref, l_ref, ssem, rsem, after_ref, p_thru, l_thru):
        for cp in _chip_copies(p_ref, l_ref, ssem, rsem, True):
            cp.wait_send()
            cp.wait_recv()
    return _split_call(body, name, [pre, land], N_CHIPS, sems, after)[1]


ADAMW_BLOCK_BYTES = 2 * 1024 * 1024


def adamw(w, g, m, v, name):
    shape = w.shape
    lay, rows, cols = ((1, 1) + shape)[-3:]
    tr = _pick(rows, (256, 128))
    tc = cols if tr * cols * 4 <= ADAMW_BLOCK_BYTES else _pick(cols, (256, 128))
    c1 = 1.0 / (1.0 - ADAM_B1 ** ADAM_STEP)
    c2 = 1.0 / (1.0 - ADAM_B2 ** ADAM_STEP)

    def body(w_ref, g_ref, m_ref, v_ref, d_ref, nm_ref, nv_ref):
        gg = g_ref[...]
        nm = ADAM_B1 * m_ref[...] + (1.0 - ADAM_B1) * gg
        nv = ADAM_B2 * v_ref[...] + (1.0 - ADAM_B2) * (gg * gg)
        d_ref[...] = -ADAM_LR * ((nm * c1) / (jnp.sqrt(nv * c2) + ADAM_EPS) + ADAM_WD * w_ref[...])
        nm_ref[...] = nm
        nv_ref[...] = nv

    spec = pl.BlockSpec((1, tr, tc), lambda l, i, j: (l, i, j))
    outs = pl.pallas_call(
        body, name=name, grid=(lay, rows // tr, cols // tc), in_specs=[spec] * 4, out_specs=[spec] * 3,
        out_shape=[SDS((lay, rows, cols), F32)] * 3, compiler_params=_cparams(("parallel",) * 3),
    )(*[t.reshape(lay, rows, cols) for t in (w, g, m, v)])
    return [o.reshape(shape) for o in outs]


def adamw_layer_inner(w, gs, m, v, name):
    rows, lay, cols = w.shape
    tr = _pick(rows, (256, 220, 128))
    c1 = 1.0 / (1.0 - ADAM_B1 ** ADAM_STEP)
    c2 = 1.0 / (1.0 - ADAM_B2 ** ADAM_STEP)

    def body(*refs):
        w_ref, m_ref, v_ref = refs[:3]
        g_refs = refs[3:3 + lay]
        go_ref, d_ref, nm_ref, nv_ref = refs[3 + lay:]
        for l, g_ref in enumerate(g_refs):
            gg = g_ref[:, 0, :]
            nm = ADAM_B1 * m_ref[:, l, :] + (1.0 - ADAM_B1) * gg
            nv = ADAM_B2 * v_ref[:, l, :] + (1.0 - ADAM_B2) * (gg * gg)
            d_ref[:, l, :] = -ADAM_LR * ((nm * c1) / (jnp.sqrt(nv * c2) + ADAM_EPS) + ADAM_WD * w_ref[:, l, :])
            go_ref[:, l, :] = gg
            nm_ref[:, l, :] = nm
            nv_ref[:, l, :] = nv

    inner = pl.BlockSpec((tr, lay, cols), lambda i: (i, 0, 0))
    plain = pl.BlockSpec((tr, 1, cols), lambda i: (i, 0, 0))
    return pl.pallas_call(
        body, name=name, grid=(rows // tr,), in_specs=[inner] * 3 + [plain] * lay, out_specs=[inner] * 4,
        out_shape=[SDS((rows, lay, cols), F32)] * 4, compiler_params=_cparams(("parallel",)),
    )(w, m, v, *gs)


BIG = ("w_in", "conv_w", "w_ssd_branch", "w_attn_branch", "w_out", "w_gate_up", "w_down")
TRANSPOSED = ("w_in", "w_gate_up")
SMALL = ("norm_mix", "conv_b", "dt_bias", "a_log", "d_skip", "ssd_norm", "norm_ffn")
SMALL_SIZE = {"norm_mix": 1024, "conv_b": 3072, "dt_bias": 32, "a_log": 32, "d_skip": 32, "ssd_norm": 2048, "norm_ffn": 1024}
FLAT_W = 512
SMALL_TOTAL = DEPTH * sum(SMALL_SIZE.values()) + D_MODEL + LANES
SMALL_ROWS = 32
assert SMALL_ROWS * FLAT_W >= SMALL_TOTAL


GROUPS = (("w_in", "conv_w"), ("w_ssd_branch", "w_attn_branch", "w_out"), ("w_gate_up", "w_down"))


def to_wire(k, shard):
    if k in TRANSPOSED:
        return shard.T.astype(BF16)
    return shard if k == "conv_w" else shard.astype(BF16)


def full_weights(k, g):
    if k == "conv_w":
        return {k: g.transpose(1, 0, 2).reshape(SSD_CONV, SSD_CONV_CH)}
    full = g.reshape(-1, g.shape[-1])
    if k != "w_in":
        return {k: full}
    w, off = {}, 0
    for nm, r in IN_ROWS:
        w[nm] = full[off:off + r]
        off += r
    w["w_q"] = full[sum(r for _, r in IN_ROWS[:3]):sum(r for _, r in IN_ROWS[:6])]
    w["w_dt"] = jnp.pad(w["w_dt"], ((0, HPAD - SSD_HEADS), (0, 0)))
    return w


def grads_to_wire(k, g):
    if k == "conv_w":
        return g.reshape(SSD_CONV, N_DEV, SSD_CONV_CH // N_DEV).transpose(1, 0, 2)
    return g.reshape(N_DEV, g.shape[0] // N_DEV, g.shape[1])


def _pad_heads(t):
    return jnp.pad(t.reshape(1, SSD_HEADS), ((0, 0), (0, HPAD - SSD_HEADS)))


def local_step(x, target, getw, prefetch, emit, smalls, norm_final):
    tabs = rope_tables()
    sms = []
    for li in range(DEPTH):
        s = smalls[li]
        sms.append({
            "norm_mix": s["norm_mix"].reshape(1, -1), "conv_b": s["conv_b"].reshape(1, -1),
            "dt_bias": _pad_heads(s["dt_bias"]), "a_log": _pad_heads(s["a_log"]),
            "d_skip_x": jnp.repeat(s["d_skip"], SSD_HEAD_DIM).reshape(1, -1),
            "ssd_norm": s["ssd_norm"].reshape(1, -1), "norm_ffn": s["norm_ffn"].reshape(1, -1)})
    h = x
    saved = []
    for li in range(DEPTH):
        h, sv = layer_fwd(h, functools.partial(getw, li), functools.partial(prefetch, li), sms[li], tabs, li)
        saved.append(sv)
    dh, g_final, loss = loss_head(h, target, norm_final.reshape(1, -1), "loss_head")
    gsms = [None] * DEPTH
    for li in reversed(range(DEPTH)):
        dh, gsm = layer_bwd(dh, saved[li], sms[li], tabs, li, functools.partial(emit, li))
        gsms[li] = {
            "norm_mix": gsm["norm_mix"].reshape(-1), "conv_b": gsm["conv_b"].reshape(-1),
            "dt_bias": gsm["dt_bias"][0, :SSD_HEADS], "a_log": gsm["a_log"][0, :SSD_HEADS],
            "d_skip": gsm["d_skip_x"].reshape(SSD_HEADS, SSD_HEAD_DIM).sum(axis=1),
            "ssd_norm": gsm["ssd_norm"].reshape(-1), "norm_ffn": gsm["norm_ffn"].reshape(-1)}
    return loss, dh, gsms, g_final.reshape(-1)


def kernel(x, norm_mix, w_in, conv_w, conv_b, dt_bias, a_log, d_skip, ssd_norm, w_ssd_branch, w_attn_branch, w_out, norm_ffn, w_gate_up, w_down, norm_final, loss_target, m_norm_mix, m_w_in, m_conv_w, m_conv_b, m_dt_bias, m_a_log, m_d_skip, m_ssd_norm, m_w_ssd_branch, m_w_attn_branch, m_w_out, m_norm_ffn, m_w_gate_up, m_w_down, m_norm_final, v_norm_mix, v_w_in, v_conv_w, v_conv_b, v_dt_bias, v_a_log, v_d_skip, v_ssd_norm, v_w_ssd_branch, v_w_attn_branch, v_w_out, v_norm_ffn, v_w_gate_up, v_w_down, v_norm_final):
    wv = dict(norm_mix=norm_mix, w_in=w_in, conv_w=conv_w, conv_b=conv_b, dt_bias=dt_bias, a_log=a_log, d_skip=d_skip,
              ssd_norm=ssd_norm, w_ssd_branch=w_ssd_branch, w_attn_branch=w_attn_branch, w_out=w_out, norm_ffn=norm_ffn,
              w_gate_up=w_gate_up, w_down=w_down, norm_final=norm_final)
    mv = dict(norm_mix=m_norm_mix, w_in=m_w_in, conv_w=m_conv_w, conv_b=m_conv_b, dt_bias=m_dt_bias, a_log=m_a_log,
              d_skip=m_d_skip, ssd_norm=m_ssd_norm, w_ssd_branch=m_w_ssd_branch, w_attn_branch=m_w_attn_branch,
              w_out=m_w_out, norm_ffn=m_norm_ffn, w_gate_up=m_w_gate_up, w_down=m_w_down, norm_final=m_norm_final)
    vv = dict(norm_mix=v_norm_mix, w_in=v_w_in, conv_w=v_conv_w, conv_b=v_conv_b, dt_bias=v_dt_bias, a_log=v_a_log,
              d_skip=v_d_skip, ssd_norm=v_ssd_norm, w_ssd_branch=v_w_ssd_branch, w_attn_branch=v_w_attn_branch,
              w_out=v_w_out, norm_ffn=v_norm_ffn, w_gate_up=v_w_gate_up, w_down=v_w_down, norm_final=v_norm_final)
    order = ("norm_mix", "w_in", "conv_w", "conv_b", "dt_bias", "a_log", "d_skip", "ssd_norm", "w_ssd_branch",
             "w_attn_branch", "w_out", "norm_ffn", "w_gate_up", "w_down", "norm_final")

    smalls = [{k: wv[k][li] for k in SMALL} for li in range(DEPTH)]
    n_groups = len(GROUPS)

    first_lands = all_gather([to_wire(k, wv[k][0]) for k in GROUPS[0]], "gather_first")
    later = [(li, gi) for li in range(DEPTH) for gi in range(n_groups)][1:]
    behind_first = first_lands[1][0, 0, 0] * 0.0
    srcs = [to_wire(k, wv[k][li] + behind_first if k == "conv_w" else wv[k][li]) for li, gi in later for k in GROUPS[gi]]
    sizes = [len(GROUPS[gi]) for _, gi in later]
    w_sems, w_srcs, w_lands, token = exchange_start(srcs, [landing_zone(s) for s in srcs], sizes, False,
                                                    "gather_start", peers=SELF_AND_CHIPS)
    smalls[0]["norm_mix"] = smalls[0]["norm_mix"] + token[0, 0]
    second_leg = {}

    def forward(slot, after):
        if slot < len(later) and slot not in second_leg:
            sl = slice(sum(sizes[:slot]), sum(sizes[:slot + 1]))
            second_leg[slot] = gather_forward(w_srcs[sl], w_lands[sl], w_sems[slot], after, f"gather_forward_{slot}")

    def prefetch(li, gi, after):
        if (li, gi) == later[0]:
            forward(0, after)

    def getw(li, gi, after):
        if (li, gi) == (0, 0):
            lands = first_lands
        else:
            slot = later.index((li, gi))
            forward(slot, after)
            sems2, lands2 = second_leg[slot]
            lands = gather_finish(lands2, sems2, after, f"gather_finish_{li}_{gi}")
            forward(slot + 1, lands[0])
        w = {}
        for k, land in zip(GROUPS[gi], lands):
            w.update(full_weights(k, land))
        return w

    pending = []

    last = {}

    def emit(li, gi, gw):
        if gi == -1:
            if li != 0:
                return 0.0
            pre = pair_sum(last["parts"], last["land"], last["sems"], gw, "grads_pair_sum")
            last["sems2"], (last["pre"], last["land2"]), tok = chip_start(pre, "grads_chip_start")
            return tok[0, 0]
        names, tok0 = GROUPS[gi], 0.0
        if (li, gi) == (0, 0):
            last["sems"], (last["parts"], last["land"]), t0 = pair_start(grads_to_wire("w_in", gw["w_in"]), "grads_pair_start")
            names, tok0 = ("conv_w",), t0[0, 0]
        parts = [grads_to_wire(k, gw[k]) for k in names]
        lands = [landing_zone(p[0]) for p in parts]
        sems, p_thru, l_thru, tok = exchange_start(parts, lands, [len(parts)], True, f"grads_start_{li}_{gi}", peers=EVERYONE)
        pending.append((li, gi, names, sems[0], p_thru, l_thru))
        return tok[0, 0] + tok0

    loss_p, dx, gsms, g_final = local_step(x[0], loss_target[0], getw, prefetch, emit, smalls, norm_final)

    grads, deltas, new_m, new_v = {}, {}, {}, {}

    def update(k):
        if k == "w_in":
            inner = lambda t: t.transpose(2, 0, 1)
            outs = adamw_layer_inner(inner(wv[k]), shard_g[k], inner(mv[k]), inner(vv[k]), "adamw_" + k)
            grads[k], deltas[k], new_m[k], new_v[k] = (t.transpose(1, 2, 0) for t in outs)
            return outs[3]
        if k in BIG:
            grads[k] = jnp.stack([g.T if k in TRANSPOSED else g for g in shard_g[k]])
        deltas[k], new_m[k], new_v[k] = adamw(wv[k], grads[k], mv[k], vv[k], "adamw_" + k)
        return new_v[k]

    shard_g = {k: [None] * DEPTH for k in BIG}

    def collect(entry, after):
        li, gi, names, sems, p_thru, l_thru = entry
        recv = exchange_wait(p_thru, l_thru, sems, after, True, f"grads_wait_{li}_{gi}", peers=EVERYONE)
        for k, r in zip(names, recv):
            if k == "conv_w":
                r = r.reshape(N_DEV, 1, -1)
            after = sum_parts(r, f"sum_{k}_{li}", row_major_3d=(k == "w_in"))
            shard_g[k][li] = after if k in TRANSPOSED else after.reshape(wv[k].shape[1:])
        if (li, gi) == (0, 0):
            land2 = chip_wait(last["pre"], last["land2"], last["sems2"], after, "grads_chip_wait")
            after = sum_parts(land2, "sum_w_in_0", row_major_3d=True)
            shard_g["w_in"][0] = after
        return after

    after = dx
    for entry in pending[:-1]:
        after = collect(entry, after)
    done = [after[:1, :1].reshape(1)]
    for gi in (2, 1):
        for k in GROUPS[gi]:
            done.append(update(k).reshape(-1)[:1])

    flat = [gsms[li][k] for li in range(DEPTH) for k in SMALL] + [g_final, loss_p.reshape(-1)]
    flat.append(jnp.zeros((SMALL_ROWS * FLAT_W - SMALL_TOTAL,), F32))
    small_all = all_gather([jnp.concatenate(flat).reshape(SMALL_ROWS, FLAT_W)], "gather_small")[0]
    small_sum = sum_parts(small_all, "sum_small").reshape(-1)
    off = 0
    per_layer = {k: [] for k in SMALL}
    for li in range(DEPTH):
        for k in SMALL:
            per_layer[k].append(small_sum[off:off + SMALL_SIZE[k]])
            off += SMALL_SIZE[k]
    for k in SMALL:
        grads[k] = jnp.stack(per_layer[k])
    grads["norm_final"] = small_sum[off:off + D_MODEL]
    loss = small_sum[off + D_MODEL]
    for k in (*SMALL, "norm_final"):
        done.append(update(k).reshape(-1)[:1])

    collect(pending[-1], jnp.concatenate(done))
    for k in GROUPS[0]:
        update(k)

    return (loss, dx.reshape(x.shape), *[grads[k] for k in order], *[deltas[k] for k in order],
            *[new_m[k] for k in order], *[new_v[k] for k in order])
```
